```python
import math
import jax
import jax.numpy as jnp
from jax import lax
import numpy as np

D_MODEL = 1024
BATCH = 16
SEQ = 256
DEPTH = 2
DEC_BATCH = 4
DEC_SEQ = 1024
PAST_LEN = 256

F32 = jnp.float32
GRID_W = 64
CHUNK = 128
ROPE_BASE = 10000.0
N_EVEN = (DEPTH + 1) // 2
N_ODD = DEPTH // 2
MIX_WIDTH = D_MODEL
N_MOD = 6
A_WIDTH = MIX_WIDTH // 2
A_GROUPS = 8
A_GROUP_DIM = A_WIDTH // A_GROUPS
B_WIDTH = MIX_WIDTH - A_WIDTH
B_HEAD = 64
B_HEADS = B_WIDTH // B_HEAD
B_W_RANK = 64
B_A_RANK = 64
B_G_RANK = 128
B_COLS = 3 * B_WIDTH + B_W_RANK + B_A_RANK + B_G_RANK
EVEN_IN = 2 * A_WIDTH + B_COLS
W_DECAY_SCALE = math.exp(-0.5)
RWKV_GN_EPS = 64e-5
C_HEADS = 4
C_QK = 64
C_V = 2 * C_QK
C_WIDTH = C_HEADS * C_V
D_HEADS = 4
D_QK = 64
D_V = 128
D_WIDTH = D_HEADS * D_V
RET_EXP_FWD = (5.0, 7.0, 9.0, 11.0)
RET_EXP_BWD = (6.0, 8.0, 10.0, 12.0)
ODD_IN = 2 * C_HEADS * 2 * C_QK + C_WIDTH + 2 * D_HEADS * D_QK + 2 * D_WIDTH
N_EXPERTS = 32
TOP_K = 4
D_FF = D_MODEL
SWIGLU_LIMIT = 7.0
SWIGLU_ALPHA = 1.702

kernel_name = 'hybrid_diffusion_prefix_trunk_step'


def rmsnorm(x, g, eps=1e-6):
    xf = x.astype(F32)
    y = xf * lax.rsqrt(jnp.mean(xf * xf, axis=-1, keepdims=True) + eps)
    return (y * g).astype(x.dtype)


def layernorm(x, eps):
    xf = x.astype(F32)
    mu = jnp.mean(xf, axis=-1, keepdims=True)
    var = jnp.mean(jnp.square(xf - mu), axis=-1, keepdims=True)
    return ((xf - mu) * lax.rsqrt(var + eps)).astype(x.dtype)


def adaln(cvec, w, b):
    m = jax.nn.silu(cvec) @ w + b
    return [m[:, None, i * D_MODEL:(i + 1) * D_MODEL] for i in range(N_MOD)]


def modulate(h, shift, scale):
    return h * (1.0 + scale) + shift


def axial_rope(x):
    L, d = x.shape[1], x.shape[-1]
    rows = L // GRID_W
    row = jnp.repeat(jnp.arange(rows), GRID_W).astype(F32)
    col = jnp.tile(jnp.arange(GRID_W), rows).astype(F32)
    nf = d // 4
    inv = jnp.power(ROPE_BASE, -jnp.arange(nf, dtype=F32) / nf)
    bshape = (1, L) + (1,) * (x.ndim - 3) + (nf,)

    def rot(xa, pos):
        ang = (pos[:, None] * inv[None, :]).reshape(bshape)
        cos = jnp.cos(ang).astype(x.dtype)
        sin = jnp.sin(ang).astype(x.dtype)
        x1, x2 = xa[..., :nf], xa[..., nf:]
        return jnp.concatenate([x1 * cos - x2 * sin, x1 * sin + x2 * cos], axis=-1)

    return jnp.concatenate([rot(x[..., :d // 2], row), rot(x[..., d // 2:], col)], axis=-1)


def centred_shift(z, mu_prev, mu_next):
    zp = jnp.pad(z, ((0, 0), (1, 0), (0, 0)))[:, :-1]
    zn = jnp.pad(z, ((0, 0), (0, 1), (0, 0)))[:, 1:]
    return z + mu_prev * (zp - z) + mu_next * (zn - z)


def chunk_sgu(u, v, ln_g, w_s, b_s):
    B, L, _ = u.shape
    vn = layernorm(v, 1e-5) * ln_g
    vc = vn.reshape(B, L // CHUNK, CHUNK, A_GROUPS, A_GROUP_DIM)
    s = jnp.einsum('gij,bnjgc->bnigc', w_s, vc) + jnp.swapaxes(b_s, 0, 1)[:, :, None]
    return u * s.reshape(B, L, A_WIDTH)


def wkv_scan(r, w, k, v, kk, a, s0):
    def step(s, inp):
        r_t, w_t, k_t, v_t, kk_t, a_t = inp
        sa = jnp.einsum('bhvk,bhk->bhv', s, kk_t)
        s = (s * w_t[:, :, None, :] - sa[..., None] * (a_t * kk_t)[:, :, None, :]
             + v_t[..., None] * k_t[:, :, None, :])
        return s, jnp.einsum('bhvk,bhk->bhv', s, r_t)

    xs = tuple(jnp.moveaxis(t, 1, 0) for t in (r, w, k, v, kk, a))
    s_fin, ys = lax.scan(step, s0.astype(F32), xs)
    return jnp.moveaxis(ys, 0, 1), s_fin


def rwkv7_bidir(zb, p, s0_f, s0_b):
    B, L, _ = zb.shape
    zb = centred_shift(zb, p['mu'][0], p['mu'][1]).astype(F32)
    offs = [B_WIDTH, 2 * B_WIDTH, 3 * B_WIDTH, 3 * B_WIDTH + B_W_RANK, 3 * B_WIDTH + B_W_RANK + B_A_RANK]
    r, k, v, wd, ad, gd = jnp.split(zb, offs, axis=-1)
    heads = lambda t: t.reshape(B, L, B_HEADS, B_HEAD)
    flip = lambda t: t[:, ::-1]
    g = jax.nn.sigmoid(gd) @ p['g_up']
    kk = heads(k * p['k_k'])
    kk = kk / jnp.maximum(jnp.sqrt(jnp.sum(kk * kk, axis=-1, keepdims=True)), 1e-6)
    rh, kh, vh = heads(r), heads(k), heads(v)
    outs, states = [], []
    for dd, s0 in ((0, s0_f), (1, s0_b)):
        w = jnp.exp(-W_DECAY_SCALE * jax.nn.sigmoid(p['w0'][dd] + jnp.tanh(wd) @ p['w_up'][dd]))
        a = jax.nn.sigmoid(p['a0'][dd] + ad @ p['a_up'][dd])
        kt = k * (1.0 + (a - 1.0) * p['k_a'])
        args = (rh, heads(w), heads(kt), vh, kk, heads(a))
        if dd == 1:
            args = tuple(flip(t) for t in args)
        y, s_fin = wkv_scan(*args, s0)
        outs.append(flip(y) if dd == 1 else y)
        states.append(s_fin)
    y = (layernorm(outs[0] + outs[1], RWKV_GN_EPS) * p['gn_g'].reshape(B_HEADS, B_HEAD)
         + p['gn_b'].reshape(B_HEADS, B_HEAD))
    bonus = jnp.sum(rh * kh * p['r_k'], axis=-1, keepdims=True) * vh
    out = (y + bonus).reshape(B, L, B_WIDTH) * g
    return out, states[0], states[1]


def even_mixer(h, p, s0_f, s0_b):
    z = h @ p['w_in']
    u, va, zb = z[..., :A_WIDTH], z[..., A_WIDTH:2 * A_WIDTH], z[..., 2 * A_WIDTH:]
    a_out = chunk_sgu(jax.nn.gelu(u, approximate=False), jax.nn.gelu(va, approximate=False),
                      p['sgu_ln_g'], p['sgu_w'], p['sgu_b'])
    b_out, s_f, s_b = rwkv7_bidir(zb, p, s0_f, s0_b)
    y = jnp.concatenate([a_out.astype(h.dtype), b_out.astype(h.dtype)], axis=-1) @ p['w_out']
    return y.astype(h.dtype), s_f, s_b


def diff_attend(q, k, v, lam):
    B, Lq, H, _, d = q.shape
    dv = v.shape[-1]
    qb = jnp.moveaxis(q.reshape(B, Lq // CHUNK, CHUNK, H, 2, d), 1, 0)
    vf = v.astype(F32)

    def block(q_blk):
        s = jnp.einsum('bqhmd,bkhmd->bhmqk', q_blk, k).astype(F32) * (d ** -0.5)
        pr = jax.nn.softmax(s, axis=-1)
        amap = pr[:, :, 0] - lam * pr[:, :, 1]
        return jnp.einsum('bhqk,bkhe->bqhe', amap, vf)

    o = lax.map(block, qb)
    return jnp.moveaxis(o, 0, 1).reshape(B, Lq, H, dv)


def log_gammas(exps):
    e = jnp.asarray(exps, F32)
    return jnp.log1p(-jnp.exp2(-e))


def retention_dir(q, k, v, log_gamma, s0, strict):
    B, L, H, dk = q.shape
    dv = v.shape[-1]
    n = L // CHUNK
    qc = q.astype(F32).reshape(B, n, CHUNK, H, dk)
    kc = k.astype(F32).reshape(B, n, CHUNK, H, dk)
    vc = v.astype(F32).reshape(B, n, CHUNK, H, dv)
    pos = jnp.arange(CHUNK, dtype=F32)
    diff = pos[:, None] - pos[None, :]
    mask = (diff > 0) if strict else (diff >= 0)
    decay = jnp.where(mask[None], jnp.exp(log_gamma[:, None, None] * jnp.where(mask, diff, 0.0)[None]), 0.0)
    scores = jnp.einsum('bnihd,bnjhd->bnhij', qc, kc) * decay
    o_intra = jnp.einsum('bnhij,bnjhe->bnihe', scores, vc)
    k_w = jnp.exp(log_gamma[:, None] * (CHUNK - 1 - pos)[None, :])
    kv = jnp.einsum('bnjhd,hj,bnjhe->bnhde', kc, k_w, vc)
    chunk_decay = jnp.exp(log_gamma * CHUNK)[None, :, None, None]

    def step(s, kv_n):
        return chunk_decay * s + kv_n, s

    s_fin, s_prev = lax.scan(step, s0.astype(F32), jnp.moveaxis(kv, 1, 0))
    q_w = jnp.exp(log_gamma[:, None] * (pos + 1.0)[None, :])
    o_cross = jnp.einsum('bnihd,hi,nbhde->bnihe', qc, q_w, s_prev)
    return (o_intra + o_cross).reshape(B, L, H, dv), s_fin


def odd_mixer(h, p, ctx_k, ctx_v, s0_f, s0_b, latent):
    B, L, _ = h.shape
    z = h @ p['w_in']
    offs = np.cumsum([C_HEADS * 2 * C_QK, C_HEADS * 2 * C_QK, C_WIDTH, D_HEADS * D_QK, D_HEADS * D_QK, D_WIDTH])
    cq, ck, cv, rq, rk, rv, rg = jnp.split(z, [int(o) for o in offs], axis=-1)
    cq = rmsnorm(cq.reshape(B, L, C_HEADS, 2, C_QK), p['qk_g'][0])
    ck = rmsnorm(ck.reshape(B, L, C_HEADS, 2, C_QK), p['qk_g'][1])
    cv = cv.reshape(B, L, C_HEADS, C_V)
    rq = rq.reshape(B, L, D_HEADS, D_QK)
    rk = rk.reshape(B, L, D_HEADS, D_QK)
    rv = rv.reshape(B, L, D_HEADS, D_V)
    if latent:
        cq, ck, rq, rk = axial_rope(cq), axial_rope(ck), axial_rope(rq), axial_rope(rk)
        keys = jnp.concatenate([ck, ctx_k.reshape(B, -1, C_HEADS, 2, C_QK).astype(ck.dtype)], axis=1)
        vals = jnp.concatenate([cv, ctx_v.astype(cv.dtype)], axis=1)
    else:
        keys, vals = ck, cv
    lv = p['lam'].astype(F32)
    lam = jnp.exp(jnp.sum(lv[0] * lv[1])) - jnp.exp(jnp.sum(lv[2] * lv[3])) + p['lambda_init']
    c_out = diff_attend(cq, keys, vals, lam)
    c_out = rmsnorm(c_out, p['subln_g']) * (1.0 - p['lambda_init'])
    rq = rq * (D_QK ** -0.5)
    o_f, s_f = retention_dir(rq, rk, rv, log_gammas(RET_EXP_FWD), s0_f, False)
    o_b, s_b = retention_dir(rq[:, ::-1], rk[:, ::-1], rv[:, ::-1], log_gammas(RET_EXP_BWD), s0_b, True)
    ret = layernorm(o_f + o_b[:, ::-1], 1e-5) * p['ret_gn_g'].reshape(D_HEADS, D_V)
    r_out = jax.nn.silu(rg.astype(F32)) * ret.reshape(B, L, D_WIDTH)
    y = jnp.concatenate([c_out.reshape(B, L, C_WIDTH).astype(h.dtype), r_out.astype(h.dtype)], axis=-1) @ p['w_out']
    return y.astype(h.dtype), ck.reshape(B, L, C_HEADS, 2 * C_QK), cv, s_f, s_b


def moe(t, router_w, router_b, w_gu, b_gu, w_dn, b_dn):
    logits = (t @ router_w + router_b).astype(F32)
    top_val, top_idx = lax.top_k(logits, TOP_K)
    gates = jax.nn.softmax(top_val, axis=-1)
    dense_gate = jnp.sum(jax.nn.one_hot(top_idx, N_EXPERTS, dtype=F32) * gates[..., None], axis=1)

    def expert(acc, ex):
        wgu, bgu, wdn, bdn, ge = ex
        gu = t @ wgu + bgu
        gt = jnp.minimum(gu[:, :D_FF], SWIGLU_LIMIT)
        up = jnp.clip(gu[:, D_FF:], -SWIGLU_LIMIT, SWIGLU_LIMIT)
        y = ((up + 1.0) * gt * jax.nn.sigmoid(SWIGLU_ALPHA * gt)) @ wdn + bdn
        return acc + ge[:, None] * y.astype(F32), None

    acc, _ = lax.scan(expert, jnp.zeros(t.shape, F32), (w_gu, b_gu, w_dn, b_dn, dense_gate.T))
    return acc.astype(t.dtype)


def setup_inputs(seed: int = 0) -> dict:
    key = jax.random.key(seed)
    ks = iter(jax.random.split(key, 64))
    nrm = lambda shape, scale: jax.random.normal(next(ks), shape, F32) * scale
    unif = lambda shape, lo, hi: jax.random.uniform(next(ks), shape, F32, lo, hi)
    D = D_MODEL
    return {
        'x_prompt': nrm((BATCH, SEQ, D), 1.0),
        'x_sample': nrm((DEC_BATCH, DEC_SEQ, D), 1.0),
        'state_rwkv': nrm((DEC_BATCH, N_EVEN, 2, B_HEADS, B_HEAD, B_HEAD), 0.1),
        'cache_k_diff': nrm((DEC_BATCH, N_ODD, PAST_LEN, C_HEADS, 2 * C_QK), 1.0),
        'cache_v_diff': nrm((DEC_BATCH, N_ODD, PAST_LEN, C_HEADS, C_V), 1.0),
        'state_retention': nrm((DEC_BATCH, N_ODD, 2, D_HEADS, D_QK, D_V), 0.5),
        'c': nrm((DEC_BATCH, D), 1.0),
        'c_ctx': nrm((D,), 1.0),
        'norm_g': 1.0 + nrm((DEPTH, 2, D), 0.02),
        'ada_w': nrm((DEPTH, D, N_MOD * D), 0.5 * D ** -0.5),
        'ada_b': nrm((DEPTH, N_MOD * D), 0.02),
        'e_w_in': nrm((N_EVEN, D, EVEN_IN), D ** -0.5),
        'e_w_out': nrm((N_EVEN, MIX_WIDTH, D), MIX_WIDTH ** -0.5),
        'sgu_ln_g': 1.0 + nrm((N_EVEN, A_WIDTH), 0.02),
        'sgu_w': nrm((N_EVEN, A_GROUPS, CHUNK, CHUNK), CHUNK ** -0.5),
        'sgu_b': 1.0 + nrm((N_EVEN, A_GROUPS, CHUNK), 0.1),
        'rw_mu': unif((N_EVEN, 2, B_COLS), 0.0, 0.5),
        'rw_w0': nrm((N_EVEN, 2, B_WIDTH), 0.5),
        'rw_w_up': nrm((N_EVEN, 2, B_W_RANK, B_WIDTH), 0.5 * B_W_RANK ** -0.5),
        'rw_a0': nrm((N_EVEN, 2, B_WIDTH), 0.5),
        'rw_a_up': nrm((N_EVEN, 2, B_A_RANK, B_WIDTH), 0.5 * B_A_RANK ** -0.5),
        'rw_g_up': nrm((N_EVEN, B_G_RANK, B_WIDTH), B_G_RANK ** -0.5),
        'rw_k_k': 0.85 + nrm((N_EVEN, B_WIDTH), 0.05),
        'rw_k_a': 1.0 + nrm((N_EVEN, B_WIDTH), 0.05),
        'rw_r_k': nrm((N_EVEN, B_HEADS, B_HEAD), 0.1),
        'rw_gn_g': 1.0 + nrm((N_EVEN, B_WIDTH), 0.02),
        'rw_gn_b': nrm((N_EVEN, B_WIDTH), 0.02),
        'o_w_in': nrm((N_ODD, D, ODD_IN), D ** -0.5),
        'o_w_out': nrm((N_ODD, MIX_WIDTH, D), MIX_WIDTH ** -0.5),
        'da_qk_g': 1.0 + nrm((N_ODD, 2, C_QK), 0.02),
        'da_lam': nrm((N_ODD, 4, C_QK), 0.1),
        'da_subln_g': 1.0 + nrm((N_ODD, C_V), 0.02),
        'ret_gn_g': 1.0 + nrm((N_ODD, D_WIDTH), 0.02),
        'router_w': nrm((DEPTH, D, N_EXPERTS), D ** -0.5),
        'router_b': nrm((DEPTH, N_EXPERTS), 0.01),
        'ex_w_gu': nrm((DEPTH, N_EXPERTS, D, 2 * D_FF), D ** -0.5),
        'ex_b_gu': nrm((DEPTH, N_EXPERTS, 2 * D_FF), 0.01),
        'ex_w_dn': nrm((DEPTH, N_EXPERTS, D_FF, D), D_FF ** -0.5),
        'ex_b_dn': nrm((DEPTH, N_EXPERTS, D), 0.01),
    }


def reference(x_prompt, x_sample, state_rwkv, cache_k_diff, cache_v_diff, state_retention, c, c_ctx,
              norm_g, ada_w, ada_b, e_w_in, e_w_out, sgu_ln_g, sgu_w, sgu_b, rw_mu, rw_w0, rw_w_up,
              rw_a0, rw_a_up, rw_g_up, rw_k_k, rw_k_a, rw_r_k, rw_gn_g, rw_gn_b, o_w_in, o_w_out,
              da_qk_g, da_lam, da_subln_g, ret_gn_g, router_w, router_b, ex_w_gu, ex_b_gu, ex_w_dn, ex_b_dn):
    yp, ys = x_prompt, x_sample
    bp = x_prompt.shape[0]
    new_rwkv, new_k, new_v, new_ret = [], [], [], []
    for l in range(DEPTH):
        mod_p = adaln(c_ctx[None, :], ada_w[l], ada_b[l])
        mod_s = adaln(c, ada_w[l], ada_b[l])
        hp = modulate(rmsnorm(yp, norm_g[l, 0]), mod_p[0], mod_p[1])
        hs = modulate(rmsnorm(ys, norm_g[l, 0]), mod_s[0], mod_s[1])
        if l % 2 == 0:
            e = l // 2
            p = {'w_in': e_w_in[e], 'w_out': e_w_out[e], 'sgu_ln_g': sgu_ln_g[e], 'sgu_w': sgu_w[e],
                 'sgu_b': sgu_b[e], 'mu': rw_mu[e], 'w0': rw_w0[e], 'w_up': rw_w_up[e], 'a0': rw_a0[e],
                 'a_up': rw_a_up[e], 'g_up': rw_g_up[e], 'k_k': rw_k_k[e], 'k_a': rw_k_a[e],
                 'r_k': rw_r_k[e], 'gn_g': rw_gn_g[e], 'gn_b': rw_gn_b[e]}
            zeros = jnp.zeros((bp, B_HEADS, B_HEAD, B_HEAD), F32)
            op, s_f, s_b = even_mixer(hp, p, zeros, zeros)
            os_, _, _ = even_mixer(hs, p, state_rwkv[:, e, 0], state_rwkv[:, e, 1])
            new_rwkv.append(jnp.stack([s_f, s_b], axis=1))
        else:
            o = l // 2
            p = {'w_in': o_w_in[o], 'w_out': o_w_out[o], 'qk_g': da_qk_g[o], 'lam': da_lam[o],
                 'subln_g': da_subln_g[o], 'ret_gn_g': ret_gn_g[o],
                 'lambda_init': 0.8 - 0.6 * math.exp(-0.3 * l)}
            zeros = jnp.zeros((bp, D_HEADS, D_QK, D_V), F32)
            op, kp, vp, s_f, s_b = odd_mixer(hp, p, None, None, zeros, zeros, False)
            os_, _, _, _, _ = odd_mixer(hs, p, cache_k_diff[:, o], cache_v_diff[:, o],
                                         state_retention[:, o, 0], state_retention[:, o, 1], True)
            new_k.append(kp)
            new_v.append(vp)
            new_ret.append(jnp.stack([s_f, s_b], axis=1))
        yp = yp + mod_p[2] * op
        ys = ys + mod_s[2] * os_
        hp = modulate(rmsnorm(yp, norm_g[l, 1]), mod_p[3], mod_p[4])
        hs = modulate(rmsnorm(ys, norm_g[l, 1]), mod_s[3], mod_s[4])
        tokens = jnp.concatenate([hp.reshape(-1, D_MODEL), hs.reshape(-1, D_MODEL)], axis=0)
        f = moe(tokens, router_w[l], router_b[l], ex_w_gu[l], ex_b_gu[l], ex_w_dn[l], ex_b_dn[l])
        n_p = hp.shape[0] * hp.shape[1]
        yp = yp + mod_p[5] * f[:n_p].reshape(yp.shape)
        ys = ys + mod_s[5] * f[n_p:].reshape(ys.shape)
    return (yp, ys, jnp.stack(new_rwkv, axis=1), jnp.stack(new_k, axis=1), jnp.stack(new_v, axis=1),
            jnp.stack(new_ret, axis=1))
```

```python
import functools
import math

import numpy as np
import jax
import jax.numpy as jnp
from jax import lax
from jax.experimental import pallas as pl
from jax.experimental.pallas import tpu as pltpu

F32 = jnp.float32
BF16 = jnp.bfloat16
I32 = jnp.int32
HIGHEST = lax.Precision.HIGHEST

D = 1024
N_PROMPT, L_PROMPT = 16, 256
N_SAMPLE, L_SAMPLE = 4, 1024
PAST = 256
T_PROMPT = N_PROMPT * L_PROMPT
T_SAMPLE = N_SAMPLE * L_SAMPLE
T = T_PROMPT + T_SAMPLE
TM = 256
NT = T // TM
PROMPT_TILES = T_PROMPT // TM
TILES_PER_SAMPLE = L_SAMPLE // TM
N_MOD = 6
HALF = 512
B_COLS = 1792
EVEN_IN = 2 * HALF + B_COLS
ODD_IN = 3072
HEAD = 64
W_DECAY_SCALE = math.exp(-0.5)
RWKV_GN_EPS = 64e-5
RW_CHUNK = 64
RET_CHUNK = 128
RET_EXP = ((5.0, 7.0, 9.0, 11.0), (6.0, 8.0, 10.0, 12.0))
N_EXPERTS = 32
TOP_K = 4
SWIGLU_LIMIT = 7.0
SWIGLU_ALPHA = 1.702
N_ASSIGN = T * TOP_K
MOE_TILES = N_ASSIGN // TM + N_EXPERTS
R_PAD = MOE_TILES * TM
LANES = 128

NN = (((1,), (0,)), ((), ()))
NT_DIMS = (((1,), (1,)), ((), ()))
TN = (((0,), (0,)), ((), ()))


def _group(i):
    return jnp.where(i < PROMPT_TILES, 0, 1 + (i - PROMPT_TILES) // TILES_PER_SAMPLE)


def _mm(a, b, dims=NN, passes=1):
    dg = functools.partial(lax.dot_general, dimension_numbers=dims, preferred_element_type=F32)
    if passes == 1:
        return dg(a.astype(BF16), b.astype(BF16))
    a = a.astype(F32)
    b = b.astype(F32)
    ah = a.astype(BF16)
    al = (a - ah.astype(F32)).astype(BF16)
    bh = b.astype(BF16)
    bl = (b - bh.astype(F32)).astype(BF16)
    return dg(ah, bh) + (dg(ah, bl) + dg(al, bh))


def _group_sum(x, ones_bd):
    xh = x.astype(BF16)
    xl = (x - xh.astype(F32)).astype(BF16)
    return (jnp.dot(xh, ones_bd, preferred_element_type=F32)
            + jnp.dot(xl, ones_bd, preferred_element_type=F32))


def _full(shape):
    nd = len(shape)
    return pl.BlockSpec(shape, lambda *_: (0,) * nd)


def _params(sem, vmem_mb=None):
    kw = {}
    if vmem_mb is not None:
        kw["vmem_limit_bytes"] = vmem_mb * 1024 * 1024
    return pltpu.CompilerParams(dimension_semantics=sem, **kw)


def _adaln_kernel(c_ref, w_ref, b_ref, o_ref):
    c = c_ref[...]
    s = c * jax.nn.sigmoid(c)
    o_ref[0] = jnp.dot(s, w_ref[0], precision=HIGHEST, preferred_element_type=F32) + b_ref[0]


def _adaln(cvec8, ada_w, ada_b):
    depth, _, n = ada_w.shape
    bn = 1536
    return pl.pallas_call(
        _adaln_kernel,
        out_shape=jax.ShapeDtypeStruct((depth, 8, n), F32),
        grid=(depth, n // bn),
        in_specs=[pl.BlockSpec((8, D), lambda l, j: (0, 0)),
                  pl.BlockSpec((1, D, bn), lambda l, j: (l, 0, j)),
                  pl.BlockSpec((1, 1, bn), lambda l, j: (l, 0, j))],
        out_specs=pl.BlockSpec((1, 8, bn), lambda l, j: (l, 0, j)),
        compiler_params=_params(("arbitrary", "arbitrary"), 40),
        name="adaln",
    )(cvec8, ada_w, ada_b.reshape(depth, 1, n))


def _in_kernel(*refs, has_f, splits):
    if has_f:
        x_ref, f_ref, modp_ref, g_ref, mod_ref, w_ref = refs[:6]
        outs = refs[6:]
        x = x_ref[...] + modp_ref[0][:, 5 * D:6 * D] * f_ref[...]
        outs[0][...] = x
        outs = outs[1:]
    else:
        x_ref, g_ref, mod_ref, w_ref = refs[:4]
        outs = refs[4:]
        x = x_ref[...]
    mod = mod_ref[0]
    y = x * lax.rsqrt(jnp.mean(x * x, axis=-1, keepdims=True) + 1e-6) * g_ref[...]
    h = (y * (1.0 + mod[:, D:2 * D]) + mod[:, 0:D]).astype(BF16)
    off = 0
    for o_ref, n in zip(outs, splits):
        o_ref[...] = jnp.dot(h, w_ref[:, off:off + n], preferred_element_type=F32)
        off += n


def _in_proj(x, g, mod, w_bf16, splits, f=None, mod_prev=None):
    n = w_bf16.shape[1]
    row = lambda i: (i, 0)
    modspec = pl.BlockSpec((1, 1, N_MOD * D), lambda i: (_group(i), 0, 0))
    in_specs = [pl.BlockSpec((TM, D), row)]
    args = [x]
    if f is not None:
        in_specs += [pl.BlockSpec((TM, D), row), modspec]
        args += [f, mod_prev]
    in_specs += [_full((1, D)), modspec, _full((D, n))]
    args += [g.reshape(1, D), mod, w_bf16]
    out_shape = [jax.ShapeDtypeStruct((T, s), F32) for s in splits]
    out_specs = [pl.BlockSpec((TM, s), row) for s in splits]
    if f is not None:
        out_shape = [jax.ShapeDtypeStruct((T, D), F32)] + out_shape
        out_specs = [pl.BlockSpec((TM, D), row)] + out_specs
    return pl.pallas_call(
        functools.partial(_in_kernel, has_f=f is not None, splits=splits),
        out_shape=out_shape, grid=(NT,), in_specs=in_specs, out_specs=out_specs,
        compiler_params=_params(("arbitrary",), 48),
        name="in_proj",
    )(*args)


def _gelu(x):
    return 0.5 * x * (1.0 + lax.erf(x * (1.0 / math.sqrt(2.0))))


def _sgu_kernel(za_ref, lng_ref, ws_ref, bs_ref, o_ref):
    u = _gelu(za_ref[:, 0:HALF])
    va = _gelu(za_ref[:, HALF:2 * HALF])
    mu = jnp.mean(va, axis=-1, keepdims=True)
    dv = va - mu
    var = jnp.mean(dv * dv, axis=-1, keepdims=True)
    vn = dv * lax.rsqrt(var + 1e-5) * lng_ref[...]
    lane = lax.broadcasted_iota(I32, (LANES, LANES), 1)
    first = lane < HEAD
    for c in range(TM // LANES):
        rows = slice(c * LANES, (c + 1) * LANES)
        for p in range(HALF // LANES):
            cols = slice(p * LANES, (p + 1) * LANES)
            vp = vn[rows, cols]
            s = (jnp.dot(ws_ref[2 * p], jnp.where(first, vp, 0.0).astype(BF16), preferred_element_type=F32)
                 + jnp.dot(ws_ref[2 * p + 1], jnp.where(first, 0.0, vp).astype(BF16), preferred_element_type=F32))
            o_ref[rows, cols] = u[rows, cols] * (s + bs_ref[:, cols])


def _sgu(za, ln_g, w_s_bf16, bs_full):
    return pl.pallas_call(
        _sgu_kernel,
        out_shape=jax.ShapeDtypeStruct((T, HALF), F32),
        grid=(NT,),
        in_specs=[pl.BlockSpec((TM, 2 * HALF), lambda i: (i, 0)), _full((1, HALF)),
                  _full((8, LANES, LANES)), _full((LANES, HALF))],
        out_specs=pl.BlockSpec((TM, HALF), lambda i: (i, 0)),
        compiler_params=_params(("arbitrary",)),
        name="sgu",
    )(za, ln_g.reshape(1, HALF), w_s_bf16, bs_full)


def _rwkv_prep_kernel(zb_ref, zp_ref, zn_ref, mu_ref, kk_ref, ka_ref, rk_ref, w0_ref, a0_ref,
                      wup_ref, aup_ref, gup_ref, ones_ref,
                      r_ref, v_ref, kkn_ref, bonus_ref, g_ref, lw_ref, kt_ref, b_ref):
    i = pl.program_id(0)
    in_sample = i >= PROMPT_TILES
    pos = (i - PROMPT_TILES) % TILES_PER_SAMPLE
    is_first = jnp.logical_or(jnp.logical_not(in_sample), pos == 0)
    is_last = jnp.logical_or(jnp.logical_not(in_sample), pos == TILES_PER_SAMPLE - 1)
    zb = zb_ref[...]
    prev_row = jnp.where(is_first, 0.0, zp_ref[7:8, :])
    next_row = jnp.where(is_last, 0.0, zn_ref[0:1, :])
    rowid = lax.broadcasted_iota(I32, (TM, 1), 0)
    zp = jnp.where(rowid == 0, prev_row, pltpu.roll(zb, 1, 0))
    zn = jnp.where(rowid == TM - 1, next_row, pltpu.roll(zb, TM - 1, 0))
    zs = zb + mu_ref[0:1, :] * (zp - zb) + mu_ref[1:2, :] * (zn - zb)
    r = zs[:, 0:HALF]
    k = zs[:, HALF:2 * HALF]
    v = zs[:, 2 * HALF:3 * HALF]
    wa = zs[:, 3 * HALF:3 * HALF + LANES]
    gd = zs[:, 3 * HALF + LANES:B_COLS]
    ones_bd = ones_ref[...]
    r_ref[...] = r
    v_ref[...] = v
    g_ref[...] = jnp.dot(jax.nn.sigmoid(gd).astype(BF16), gup_ref[...], preferred_element_type=F32)
    kk = k * kk_ref[...]
    kkn = kk / jnp.maximum(jnp.sqrt(_group_sum(kk * kk, ones_bd)), 1e-6)
    kkn_ref[...] = kkn
    bonus_ref[...] = _group_sum(r * k * rk_ref[...], ones_bd) * v
    tw = jnp.tanh(wa).astype(BF16)
    wa16 = wa.astype(BF16)
    for dd in range(2):
        lw_ref[dd] = -W_DECAY_SCALE * jax.nn.sigmoid(
            w0_ref[dd:dd + 1, :] + jnp.dot(tw, wup_ref[dd], preferred_element_type=F32))
        a = jax.nn.sigmoid(a0_ref[dd:dd + 1, :] + jnp.dot(wa16, aup_ref[dd], preferred_element_type=F32))
        kt_ref[dd] = k * (1.0 + (a - 1.0) * ka_ref[...])
        b_ref[dd] = a * kkn


def _rwkv_prep(zb, mu, k_k, k_a, r_k, w0, a0, wup_pad, aup_pad, g_up, ones_bd):
    row = lambda i: (i, 0)
    halo = TM // 8
    one = jax.ShapeDtypeStruct((T, HALF), F32)
    two = jax.ShapeDtypeStruct((2, T, HALF), F32)
    o1 = pl.BlockSpec((TM, HALF), row)
    o2 = pl.BlockSpec((2, TM, HALF), lambda i: (0, i, 0))
    return pl.pallas_call(
        _rwkv_prep_kernel,
        out_shape=[one, one, one, one, one, two, two, two],
        grid=(NT,),
        in_specs=[pl.BlockSpec((TM, B_COLS), row),
                  pl.BlockSpec((8, B_COLS), lambda i: (jnp.maximum(i * halo - 1, 0), 0)),
                  pl.BlockSpec((8, B_COLS), lambda i: (jnp.minimum((i + 1) * halo, T // 8 - 1), 0)),
                  _full((2, B_COLS)), _full((1, HALF)), _full((1, HALF)), _full((1, HALF)),
                  _full((2, HALF)), _full((2, HALF)),
                  _full((2, LANES, HALF)), _full((2, LANES, HALF)), _full((LANES, HALF)),
                  _full((HALF, HALF))],
        out_specs=[o1, o1, o1, o1, o1, o2, o2, o2],
        compiler_params=_params(("arbitrary",), 48),
        name="rwkv_prep",
    )(zb, zb, zb, mu, k_k.reshape(1, HALF), k_a.reshape(1, HALF), r_k.reshape(1, HALF), w0, a0,
      wup_pad, aup_pad, g_up, ones_bd)


def _rwkv_scan_kernel(*refs, has_s0, n_chunks):
    if has_s0:
        r_ref, v_ref, kk_ref, lw_ref, kt_ref, b_ref, s0_ref, y_ref, sfin_ref, s_ref = refs
    else:
        r_ref, v_ref, kk_ref, lw_ref, kt_ref, b_ref, y_ref, sfin_ref, s_ref = refs
    C = RW_CHUNK
    dd = pl.program_id(0)
    j = pl.program_id(2)
    rev = dd == 1

    @pl.when(j == 0)
    def _():
        if has_s0:
            s_ref[...] = s0_ref[0, 0]
        else:
            s_ref[...] = jnp.zeros_like(s_ref)

    ti = lax.broadcasted_iota(I32, (C, C), 0)
    tj = lax.broadcasted_iota(I32, (C, C), 1)
    sgn = jnp.where(rev, -1, 1).astype(I32)
    tri = jnp.where((ti - tj) * sgn >= 0, 1.0, 0.0).astype(F32)
    lw = lw_ref[0]
    cs = jnp.dot(tri, lw, precision=HIGHEST, preferred_element_type=F32)
    ctot = jnp.where(rev, cs[0:1, :], cs[C - 1:C, :])
    e_neg = jnp.exp(-cs)
    q1 = kk_ref[...] * jnp.exp(cs - lw)
    k1 = kt_ref[0] * e_neg
    b1 = b_ref[0] * e_neg
    r1 = r_ref[...] * jnp.exp(cs)
    e_tail = jnp.exp(ctot - cs)
    k2 = kt_ref[0] * e_tail
    b2 = b_ref[0] * e_tail
    e_tot = jnp.exp(ctot)
    v_all = v_ref[...]

    bi = lax.broadcasted_iota(I32, (LANES, LANES), 0)
    bj = lax.broadcasted_iota(I32, (LANES, LANES), 1)
    same = (bi >> 6) == (bj >> 6)
    pi = bi & (C - 1)
    pj = bj & (C - 1)
    order = (pi - pj) * sgn
    strict = jnp.logical_and(same, order > 0)
    incl = jnp.logical_and(same, order >= 0)
    eye = (bi == bj).astype(F32)
    lane = lax.broadcasted_iota(I32, (C, LANES), 1)
    h0 = lane < HEAD

    def stack(x):
        return jnp.concatenate([jnp.where(h0, x, 0.0), jnp.where(h0, 0.0, x)], axis=0)

    def fold(x):
        return x[0:C] + x[C:2 * C]

    for p in range(HALF // LANES):
        cols = slice(p * LANES, (p + 1) * LANES)
        q1p, k1p, b1p, r1p = q1[:, cols], k1[:, cols], b1[:, cols], r1[:, cols]
        vp = v_all[:, cols]
        lhs = jnp.concatenate([stack(q1p), stack(r1p)], axis=0)
        rhs = jnp.concatenate([k1p, k1p, b1p, b1p], axis=0)
        gm = _mm(lhs, rhs, NT_DIMS, 3)
        mk = jnp.where(strict, gm[0:2 * C, 0:2 * C], 0.0)
        mb = jnp.where(strict, gm[0:2 * C, 2 * C:4 * C], 0.0)
        nk = jnp.where(incl, gm[2 * C:4 * C, 0:2 * C], 0.0)
        nb = jnp.where(incl, gm[2 * C:4 * C, 2 * C:4 * C], 0.0)
        s = 1
        tinv = eye - jnp.where((pi >> 1) == (pj >> 1), mb, 0.0)
        s = 2
        while s < C:
            sh = s.bit_length() - 1
            blk = jnp.logical_and((pi >> (sh + 1)) == (pj >> (sh + 1)), (pi >> sh) != (pj >> sh))
            noff = jnp.where(blk, mb, 0.0)
            tinv = tinv - _mm(_mm(tinv, noff, NN, 3), tinv, NN, 3)
            s *= 2
        vst = stack(vp)
        mkv = fold(_mm(mk, vst, NN, 3))
        nkv = fold(_mm(nk, vst, NN, 3))
        u0 = fold(_mm(tinv, stack(mkv), NN, 3))
        s_bd = s_ref[p]
        qr = _mm(jnp.concatenate([q1p, r1p], axis=0), s_bd, NT_DIMS, 3)
        u = u0 + fold(_mm(tinv, stack(qr[0:C]), NN, 3))
        y = qr[C:2 * C] + nkv - fold(_mm(nb, stack(u), NN, 3))
        y_ref[0, :, cols] = y
        upd = _mm(jnp.concatenate([vp, u], axis=0),
                  jnp.concatenate([k2[:, cols], -b2[:, cols]], axis=0), TN, 3)
        s_new = s_bd * e_tot[:, cols] + jnp.where(same, upd, 0.0)
        s_ref[p] = s_new

        @pl.when(j == n_chunks - 1)
        def _():
            sfin_ref[0, 0, p] = s_new


def _rwkv_scan(r, v, kk, lw, kt, b, row0, n_seq, seq_len, s0_bd=None):
    C = RW_CHUNK
    n_chunks = seq_len // C
    blk0 = row0 // C

    def rowblk(dd, s, j):
        return blk0 + s * n_chunks + jnp.where(dd == 1, n_chunks - 1 - j, j)

    one = pl.BlockSpec((C, HALF), lambda dd, s, j: (rowblk(dd, s, j), 0))
    two = pl.BlockSpec((1, C, HALF), lambda dd, s, j: (dd, rowblk(dd, s, j), 0))
    st = pl.BlockSpec((1, 1, 4, LANES, LANES), lambda dd, s, j: (dd, s, 0, 0, 0))
    in_specs = [one, one, one, two, two, two]
    args = [r, v, kk, lw, kt, b]
    if s0_bd is not None:
        in_specs.append(st)
        args.append(s0_bd)
    return pl.pallas_call(
        functools.partial(_rwkv_scan_kernel, has_s0=s0_bd is not None, n_chunks=n_chunks),
        out_shape=[jax.ShapeDtypeStruct((2, n_seq * seq_len, HALF), F32),
                   jax.ShapeDtypeStruct((2, n_seq, 4, LANES, LANES), F32)],
        grid=(2, n_seq, n_chunks),
        in_specs=in_specs,
        out_specs=[pl.BlockSpec((1, C, HALF),
                                lambda dd, s, j: (dd, s * n_chunks + jnp.where(dd == 1, n_chunks - 1 - j, j), 0)),
                   st],
        scratch_shapes=[pltpu.VMEM((4, LANES, LANES), F32)],
        compiler_params=_params(("arbitrary", "arbitrary", "arbitrary")),
        name="rwkv_scan",
    )(*args)


def _rope(x, cos, sin_signed, first16):
    w = x.shape[1]
    partner = jnp.where(first16, pltpu.roll(x, w - 16, 1), pltpu.roll(x, 16, 1))
    return x * cos + partner * sin_signed


def _odd_prep_kernel(zc_ref, zr_ref, cos_ref, sin_ref, qkg_ref, ones_ref, cq_ref, ck_ref, ckraw_ref, rqk_ref):
    ones_bd = ones_ref[...]
    cos = cos_ref[...]
    sin = sin_ref[...]
    lane = lax.broadcasted_iota(I32, (TM, HALF), 1)
    first16 = (lane & 31) < 16
    for idx, (o_ref, raw_ref) in enumerate(((cq_ref, None), (ck_ref, ckraw_ref))):
        x = zc_ref[:, idx * HALF:(idx + 1) * HALF]
        ms = _group_sum(x * x, ones_bd) * (1.0 / HEAD)
        xn = x * lax.rsqrt(ms + 1e-6) * qkg_ref[idx:idx + 1, :]
        if raw_ref is not None:
            raw_ref[...] = xn
        o_ref[...] = _rope(xn, cos, sin, first16)
    rqk = _rope(zr_ref[...], cos, sin, first16)
    lane2 = lax.broadcasted_iota(I32, (TM, HALF), 1)
    rqk_ref[...] = jnp.where(lane2 < HALF // 2, rqk * (HEAD ** -0.5), rqk)


def _odd_prep(zc, zr, cos_tab, sin_tab, qkg_tiled, ones_bd):
    row = lambda i: (i, 0)
    tab = lambda i: (jnp.where(i < PROMPT_TILES, 0, 1 + (i - PROMPT_TILES) % TILES_PER_SAMPLE), 0)
    one = jax.ShapeDtypeStruct((T, HALF), F32)
    o1 = pl.BlockSpec((TM, HALF), row)
    return pl.pallas_call(
        _odd_prep_kernel,
        out_shape=[one, one, one, one], grid=(NT,),
        in_specs=[pl.BlockSpec((TM, 2 * HALF), row), pl.BlockSpec((TM, HALF), row),
                  pl.BlockSpec((TM, HALF), tab), pl.BlockSpec((TM, HALF), tab),
                  _full((2, HALF)), _full((HALF, HALF))],
        out_specs=[o1, o1, o1, o1],
        compiler_params=_params(("arbitrary",)),
        name="odd_prep",
    )(zc, zr, cos_tab, sin_tab, qkg_tiled, ones_bd)


def _attn_kernel(*refs, has_ctx, one_minus_li):
    if has_ctx:
        q_ref, k_ref, v_ref, kc_ref, vc_ref, lam_ref, sg_ref, o_ref = refs
    else:
        q_ref, k_ref, v_ref, lam_ref, sg_ref, o_ref = refs
    lam = lam_ref[...]
    lane = lax.broadcasted_iota(I32, (LANES, LANES), 1)
    m0 = lane < HEAD
    scale = HEAD ** -0.5
    for h in range(4):
        cols = slice(h * LANES, (h + 1) * LANES)
        qp = q_ref[:, cols]
        segs = [(k_ref[:, cols], v_ref[:, cols])]
        if has_ctx:
            segs.append((kc_ref[0, :, cols], vc_ref[0, :, cols]))
        outs = []
        for qm in (jnp.where(m0, qp, 0.0), jnp.where(m0, 0.0, qp)):
            qm16 = qm.astype(BF16)
            ss = [lax.dot_general(qm16, ks.astype(BF16), NT_DIMS, preferred_element_type=F32) * scale
                  for ks, _ in segs]
            mx = ss[0].max(axis=-1, keepdims=True)
            for s_ in ss[1:]:
                mx = jnp.maximum(mx, s_.max(axis=-1, keepdims=True))
            ps = [jnp.exp(s_ - mx) for s_ in ss]
            den = ps[0].sum(axis=-1, keepdims=True)
            for p_ in ps[1:]:
                den = den + p_.sum(axis=-1, keepdims=True)
            outs.append([p_ / den for p_ in ps])
        acc = None
        for si, (_, vs) in enumerate(segs):
            amap = outs[0][si] - lam * outs[1][si]
            t = jnp.dot(amap.astype(BF16), vs.astype(BF16), preferred_element_type=F32)
            acc = t if acc is None else acc + t
        nrm = acc * lax.rsqrt(jnp.mean(acc * acc, axis=-1, keepdims=True) + 1e-6) * sg_ref[...]
        o_ref[:, cols] = nrm * one_minus_li


def _attn(cq, ck, zc, row0, n_seq, seq_len, lam, subln_g, one_minus_li, ctx_k=None, ctx_v=None):
    nq = seq_len // LANES
    qb0 = row0 // LANES
    sb0 = row0 // seq_len
    in_specs = [pl.BlockSpec((LANES, HALF), lambda s, q: (qb0 + s * nq + q, 0)),
                pl.BlockSpec((seq_len, HALF), lambda s, q: (sb0 + s, 0)),
                pl.BlockSpec((seq_len, HALF), lambda s, q: (sb0 + s, 2))]
    args = [cq, ck, zc]
    if ctx_k is not None:
        in_specs += [pl.BlockSpec((1, PAST, HALF), lambda s, q: (s, 0, 0))] * 2
        args += [ctx_k, ctx_v]
    in_specs += [_full((1, 1)), _full((1, LANES))]
    args += [lam.reshape(1, 1), subln_g.reshape(1, LANES)]
    return pl.pallas_call(
        functools.partial(_attn_kernel, has_ctx=ctx_k is not None, one_minus_li=one_minus_li),
        out_shape=jax.ShapeDtypeStruct((n_seq * seq_len, HALF), F32),
        grid=(n_seq, nq), in_specs=in_specs,
        out_specs=pl.BlockSpec((LANES, HALF), lambda s, q: (s * nq + q, 0)),
        compiler_params=_params(("arbitrary", "arbitrary"), 48),
        name="diff_attn",
    )(*args)


def _ret_kernel(*refs, has_s0, n_chunks):
    if has_s0:
        qk_ref, v_ref, s0_ref, o_ref, sfin_ref, s_ref = refs
    else:
        qk_ref, v_ref, o_ref, sfin_ref, s_ref = refs
    C = RET_CHUNK
    dd = pl.program_id(0)
    j = pl.program_id(2)
    rev = dd == 1

    @pl.when(j == 0)
    def _():
        if has_s0:
            s_ref[...] = s0_ref[0, 0]
        else:
            s_ref[...] = jnp.zeros_like(s_ref)

    ii = lax.broadcasted_iota(I32, (C, C), 0)
    jj = lax.broadcasted_iota(I32, (C, C), 1)
    sgn = jnp.where(rev, -1, 1).astype(I32)
    diff = (ii - jj) * sgn
    mask = diff + jnp.where(rev, 0, 1).astype(I32) > 0
    dist = diff.astype(F32)
    ri = lax.broadcasted_iota(I32, (C, 1), 0)
    kpow = jnp.where(rev, ri, C - 1 - ri).astype(F32)
    qpow = jnp.where(rev, C - ri, ri + 1).astype(F32)
    lane = lax.broadcasted_iota(I32, (C, LANES), 1)
    for h in range(4):
        lg = jnp.where(rev, _LOG_GAMMA[1][h], _LOG_GAMMA[0][h])
        p = h // 2
        hm = (lane < HEAD) if h % 2 == 0 else (lane >= HEAD)
        qp = jnp.where(hm, qk_ref[:, p * LANES:(p + 1) * LANES], 0.0)
        kp = jnp.where(hm, qk_ref[:, HALF // 2 + p * LANES:HALF // 2 + (p + 1) * LANES], 0.0)
        vh = v_ref[:, h * LANES:(h + 1) * LANES].astype(BF16)
        decay = jnp.where(mask, jnp.exp(lg * jnp.where(mask, dist, 0.0)), 0.0)
        sc = lax.dot_general(qp.astype(BF16), kp.astype(BF16), NT_DIMS, preferred_element_type=F32) * decay
        o_intra = jnp.dot(sc.astype(BF16), vh, preferred_element_type=F32)
        s_prev = s_ref[h]
        o_cross = jnp.dot((qp * jnp.exp(lg * qpow)).astype(BF16), s_prev.astype(BF16),
                          preferred_element_type=F32)
        o_ref[0, :, h * LANES:(h + 1) * LANES] = o_intra + o_cross
        kv = lax.dot_general((kp * jnp.exp(lg * kpow)).astype(BF16), vh, TN, preferred_element_type=F32)
        cd = jnp.where(rev, math.exp(_LOG_GAMMA[1][h] * C), math.exp(_LOG_GAMMA[0][h] * C))
        s_new = cd * s_prev + kv
        s_ref[h] = s_new

        @pl.when(j == n_chunks - 1)
        def _():
            sfin_ref[0, 0, h] = s_new


_LOG_GAMMA = tuple(tuple(float(np.log1p(-np.exp2(-np.float32(e)), dtype=np.float32)) for e in es)
                   for es in RET_EXP)


def _retention(rqk, zr, row0, n_seq, seq_len, s0=None):
    C = RET_CHUNK
    n_chunks = seq_len // C
    blk0 = row0 // C

    def rowblk(dd, s, j):
        return blk0 + s * n_chunks + jnp.where(dd == 1, n_chunks - 1 - j, j)

    st = pl.BlockSpec((1, 1, 4, LANES, LANES), lambda dd, s, j: (dd, s, 0, 0, 0))
    in_specs = [pl.BlockSpec((C, HALF), lambda dd, s, j: (rowblk(dd, s, j), 0)),
                pl.BlockSpec((C, HALF), lambda dd, s, j: (rowblk(dd, s, j), 1))]
    args = [rqk, zr]
    if s0 is not None:
        in_specs.append(st)
        args.append(s0)
    return pl.pallas_call(
        functools.partial(_ret_kernel, has_s0=s0 is not None, n_chunks=n_chunks),
        out_shape=[jax.ShapeDtypeStruct((2, n_seq * seq_len, HALF), F32),
                   jax.ShapeDtypeStruct((2, n_seq, 4, LANES, LANES), F32)],
        grid=(2, n_seq, n_chunks), in_specs=in_specs,
        out_specs=[pl.BlockSpec((1, C, HALF),
                                lambda dd, s, j: (dd, s * n_chunks + jnp.where(dd == 1, n_chunks - 1 - j, j), 0)),
                   st],
        scratch_shapes=[pltpu.VMEM((4, LANES, LANES), F32)],
        compiler_params=_params(("arbitrary", "arbitrary", "arbitrary")),
        name="retention",
    )(*args)


def _out_kernel(*refs, even):
    if even:
        (a_ref, yf_ref, yb_ref, bonus_ref, g_ref, gng_ref, gnb_ref, ones_ref,
         x_ref, mod_ref, ng_ref, wo_ref, rw_ref, rb_ref, y_ref, xp_ref, ti_ref, tg_ref) = refs
        ones_bd = ones_ref[...]
        ys = yf_ref[0] + yb_ref[0]
        mu = _group_sum(ys, ones_bd) * (1.0 / HEAD)
        dv = ys - mu
        var = _group_sum(dv * dv, ones_bd) * (1.0 / HEAD)
        yn = dv * lax.rsqrt(var + RWKV_GN_EPS) * gng_ref[...] + gnb_ref[...]
        left = a_ref[...]
        right = (yn + bonus_ref[...]) * g_ref[...]
    else:
        (c_ref, of_ref, ob_ref, rg_ref, gng_ref,
         x_ref, mod_ref, ng_ref, wo_ref, rw_ref, rb_ref, y_ref, xp_ref, ti_ref, tg_ref) = refs
        left = c_ref[...]
        rg = rg_ref[...]
        gate = rg * jax.nn.sigmoid(rg)
        os_ = of_ref[0] + ob_ref[0]
        parts = []
        for h in range(4):
            oh = os_[:, h * LANES:(h + 1) * LANES]
            mu = jnp.mean(oh, axis=-1, keepdims=True)
            dv = oh - mu
            var = jnp.mean(dv * dv, axis=-1, keepdims=True)
            parts.append(dv * lax.rsqrt(var + 1e-5))
        right = gate * (jnp.concatenate(parts, axis=1) * gng_ref[...])
    mod = mod_ref[0]
    o = (jnp.dot(left.astype(BF16), wo_ref[0:HALF, :], preferred_element_type=F32)
         + jnp.dot(right.astype(BF16), wo_ref[HALF:2 * HALF, :], preferred_element_type=F32))
    y = x_ref[...] + mod[:, 2 * D:3 * D] * o
    y_ref[...] = y
    yn2 = y * lax.rsqrt(jnp.mean(y * y, axis=-1, keepdims=True) + 1e-6) * ng_ref[...]
    t = yn2 * (1.0 + mod[:, 4 * D:5 * D]) + mod[:, 3 * D:4 * D]
    tb = lax.bitcast_convert_type(t.astype(BF16).astype(F32), I32)
    xp_ref[...] = jnp.bitwise_or(lax.shift_right_logical(tb[:, 0:HALF], 16),
                                 jnp.bitwise_and(tb[:, HALF:D], jnp.int32(-65536)))
    logits = jnp.dot(t, rw_ref[...], precision=HIGHEST, preferred_element_type=F32) + rb_ref[...]
    lane = lax.broadcasted_iota(I32, (TM, LANES), 1)
    neg = jnp.float32(-jnp.inf)
    lg = jnp.where(lane < N_EXPERTS, logits, neg)
    vals, idxs = [], []
    for _ in range(TOP_K):
        m = jnp.max(lg, axis=-1, keepdims=True)
        ix = jnp.min(jnp.where(lg == m, lane, LANES), axis=-1, keepdims=True)
        vals.append(m)
        idxs.append(ix)
        lg = jnp.where(lane == ix, neg, lg)
    es = [jnp.exp(vv - vals[0]) for vv in vals]
    den = es[0] + es[1] + es[2] + es[3]
    ti = jnp.zeros((TM, LANES), I32)
    tg = jnp.zeros((TM, LANES), F32)
    for kk in range(TOP_K):
        ti = jnp.where(lane == kk, idxs[kk], ti)
        tg = jnp.where(lane == kk, es[kk] / den, tg)
    ti_ref[...] = ti
    tg_ref[...] = tg


def _out_proj(even, mix_args, mix_specs, x, mod, norm_g, w_out_bf16, rw_pad, rb_pad):
    row = lambda i: (i, 0)
    modspec = pl.BlockSpec((1, 1, N_MOD * D), lambda i: (_group(i), 0, 0))
    in_specs = list(mix_specs) + [pl.BlockSpec((TM, D), row), modspec, _full((1, D)), _full((D, D)),
                                  _full((D, LANES)), _full((1, LANES))]
    args = list(mix_args) + [x, mod, norm_g.reshape(1, D), w_out_bf16, rw_pad, rb_pad]
    return pl.pallas_call(
        functools.partial(_out_kernel, even=even),
        out_shape=[jax.ShapeDtypeStruct((T, D), F32), jax.ShapeDtypeStruct((T, HALF), I32),
                   jax.ShapeDtypeStruct((T, LANES), I32), jax.ShapeDtypeStruct((T, LANES), F32)],
        grid=(NT,), in_specs=in_specs,
        out_specs=[pl.BlockSpec((TM, D), row), pl.BlockSpec((TM, HALF), row),
                   pl.BlockSpec((TM, LANES), row), pl.BlockSpec((TM, LANES), row)],
        compiler_params=_params(("arbitrary",), 48),
        name="out_proj",
    )(*args)


def _route_meta(ti, tg):
    e = ti[:, :TOP_K].reshape(-1)
    g = tg[:, :TOP_K].reshape(-1)
    onehot = (e[:, None] == jnp.arange(N_EXPERTS, dtype=I32)[None, :]).astype(I32)
    csum = jnp.cumsum(onehot, axis=0)
    rank = jnp.sum(csum * onehot, axis=1) - 1
    counts = csum[-1]
    padded = ((counts + TM - 1) // TM) * TM
    pend = jnp.cumsum(padded)
    pstart = pend - padded
    dest = pstart[e] + rank
    tok = jnp.arange(N_ASSIGN, dtype=I32) // TOP_K
    row_token = jnp.zeros((R_PAD,), I32).at[dest].set(tok)
    row_gate = jnp.zeros((R_PAD,), F32).at[dest].set(g)
    n_tiles = (pend[-1] // TM).astype(I32)
    tiles = jnp.arange(MOE_TILES, dtype=I32)
    te = jnp.minimum(jnp.searchsorted(pend, tiles * TM, side="right").astype(I32), N_EXPERTS - 1)
    te = jnp.where(tiles < n_tiles, te, te[jnp.maximum(n_tiles - 1, 0)])
    return te, n_tiles.reshape(1), row_token, row_gate.reshape(R_PAD, 1)


def _moe_up_kernel(te_ref, nt_ref, rt_ref, x_ref, w_ref, b_ref, act_ref, xg_ref):
    i = pl.program_id(0)

    @pl.when(i < nt_ref[0])
    def _():
        base = i * TM

        def gather(r, carry):
            tok = rt_ref[base + r]
            xg_ref[pl.ds(r, 1), :] = x_ref[pl.ds(tok, 1), :]
            return carry

        lax.fori_loop(0, TM, gather, 0, unroll=8)
        w = xg_ref[...]
        lo = lax.bitcast_convert_type(lax.shift_left(w, 16), F32).astype(BF16)
        hi = lax.bitcast_convert_type(jnp.bitwise_and(w, jnp.int32(-65536)), F32).astype(BF16)
        gu = (jnp.dot(lo, w_ref[0, 0:HALF, :].astype(BF16), preferred_element_type=F32)
              + jnp.dot(hi, w_ref[0, HALF:D, :].astype(BF16), preferred_element_type=F32)
              + b_ref[0])
        gt = jnp.minimum(gu[:, 0:D], SWIGLU_LIMIT)
        up = jnp.clip(gu[:, D:2 * D], -SWIGLU_LIMIT, SWIGLU_LIMIT)
        act_ref[...] = ((up + 1.0) * gt * jax.nn.sigmoid(SWIGLU_ALPHA * gt)).astype(BF16)

    @pl.when(i >= nt_ref[0])
    def _():
        act_ref[...] = jnp.zeros_like(act_ref)


def _moe_up(te, n_tiles, row_token, xp, w_gu, b_gu):
    return pl.pallas_call(
        _moe_up_kernel,
        out_shape=jax.ShapeDtypeStruct((R_PAD, D), BF16),
        grid_spec=pltpu.PrefetchScalarGridSpec(
            num_scalar_prefetch=3, grid=(MOE_TILES,),
            in_specs=[pl.BlockSpec(memory_space=pltpu.VMEM),
                      pl.BlockSpec((1, D, 2 * D), lambda i, te, nt, rt: (te[i], 0, 0)),
                      pl.BlockSpec((1, 1, 2 * D), lambda i, te, nt, rt: (te[i], 0, 0))],
            out_specs=pl.BlockSpec((TM, D), lambda i, te, nt, rt: (i, 0)),
            scratch_shapes=[pltpu.VMEM((TM, HALF), I32)]),
        compiler_params=_params(("arbitrary",), 58),
        name="moe_up",
    )(te, n_tiles, row_token, xp, w_gu, b_gu.reshape(N_EXPERTS, 1, 2 * D))


def _moe_down_kernel(te_ref, nt_ref, rt_ref, act_ref, gate_ref, w_ref, b_ref, acc_ref, yg_ref):
    i = pl.program_id(1)

    @pl.when(i == 0)
    def _():
        acc_ref[...] = jnp.zeros_like(acc_ref)

    @pl.when(i < nt_ref[0])
    def _():
        y = jnp.dot(act_ref[...], w_ref[0].astype(BF16), preferred_element_type=F32) + b_ref[0]
        yg_ref[...] = y * gate_ref[...]
        base = i * TM

        def scatter(r, carry):
            tok = rt_ref[base + r]
            acc_ref[pl.ds(tok, 1), :] += yg_ref[pl.ds(r, 1), :]
            return carry

        lax.fori_loop(0, TM, scatter, 0, unroll=8)


def _moe_down(te, n_tiles, row_token, act, row_gate, w_dn, b_dn):
    tile = lambda j, i, te, nt, rt: (jnp.minimum(i, jnp.maximum(nt[0] - 1, 0)), 0)
    return pl.pallas_call(
        _moe_down_kernel,
        out_shape=jax.ShapeDtypeStruct((T, D), F32),
        grid_spec=pltpu.PrefetchScalarGridSpec(
            num_scalar_prefetch=3, grid=(2, MOE_TILES),
            in_specs=[pl.BlockSpec((TM, D), tile), pl.BlockSpec((TM, 1), tile),
                      pl.BlockSpec((1, D, HALF), lambda j, i, te, nt, rt: (te[i], 0, j)),
                      pl.BlockSpec((1, 1, HALF), lambda j, i, te, nt, rt: (te[i], 0, j))],
            out_specs=pl.BlockSpec((T, HALF), lambda j, i, te, nt, rt: (0, j)),
            scratch_shapes=[pltpu.VMEM((TM, HALF), F32)]),
        compiler_params=_params(("arbitrary", "arbitrary"), 56),
        name="moe_down",
    )(te, n_tiles, row_token, act, row_gate, w_dn, b_dn.reshape(N_EXPERTS, 1, D))


def _moe(xp, ti, tg, w_gu, b_gu, w_dn, b_dn):
    te, n_tiles, row_token, row_gate = _route_meta(ti, tg)
    act = _moe_up(te, n_tiles, row_token, xp, w_gu, b_gu)
    return _moe_down(te, n_tiles, row_token, act, row_gate, w_dn, b_dn)


def _final_kernel(x_ref, f_ref, mod_ref, o_ref):
    o_ref[...] = x_ref[...] + mod_ref[0][:, 5 * D:6 * D] * f_ref[...]


def _final(x, f, mod):
    row = lambda i: (i, 0)
    return pl.pallas_call(
        _final_kernel,
        out_shape=jax.ShapeDtypeStruct((T, D), F32), grid=(NT,),
        in_specs=[pl.BlockSpec((TM, D), row), pl.BlockSpec((TM, D), row),
                  pl.BlockSpec((1, 1, N_MOD * D), lambda i: (_group(i), 0, 0))],
        out_specs=pl.BlockSpec((TM, D), row),
        compiler_params=_params(("arbitrary",)),
        name="final_residual",
    )(x, f, mod)


def _ones_blockdiag():
    idx = np.arange(HALF) // HEAD
    return jnp.asarray((idx[:, None] == idx[None, :]).astype(np.float32), dtype=BF16)


def _rope_tables():
    pos = jnp.arange(L_SAMPLE)
    rowp = (pos // 64).astype(F32)
    colp = (pos % 64).astype(F32)
    nf = HEAD // 4
    inv = jnp.power(10000.0, -jnp.arange(nf, dtype=F32) / nf)
    ar = rowp[:, None] * inv[None, :]
    ac = colp[:, None] * inv[None, :]
    cos64 = jnp.concatenate([jnp.cos(ar), jnp.cos(ar), jnp.cos(ac), jnp.cos(ac)], axis=1)
    sin64 = jnp.concatenate([-jnp.sin(ar), jnp.sin(ar), -jnp.sin(ac), jnp.sin(ac)], axis=1)
    cos = jnp.tile(cos64, (1, HALF // HEAD))
    sin = jnp.tile(sin64, (1, HALF // HEAD))
    ident = jnp.ones((TM, HALF), F32)
    return (jnp.concatenate([ident, cos], axis=0), jnp.concatenate([jnp.zeros((TM, HALF), F32), sin], axis=0))


def _bd_pairs(s):
    lead = s.shape[:-3]
    s = s.reshape(lead + (4, 2, HEAD, HEAD))
    z = jnp.zeros_like(s[..., 0, :, :])
    top = jnp.concatenate([s[..., 0, :, :], z], axis=-1)
    bot = jnp.concatenate([z, s[..., 1, :, :]], axis=-1)
    return jnp.concatenate([top, bot], axis=-2)


def _bd_unpairs(s):
    a = s[..., 0:HEAD, 0:HEAD]
    b = s[..., HEAD:, HEAD:]
    out = jnp.stack([a, b], axis=-3)
    return out.reshape(s.shape[:-3] + (8, HEAD, HEAD))


def kernel(x_prompt, x_sample, state_rwkv, cache_k_diff, cache_v_diff, state_retention, c, c_ctx, norm_g, ada_w, ada_b, e_w_in, e_w_out, sgu_ln_g, sgu_w, sgu_b, rw_mu, rw_w0, rw_w_up, rw_a0, rw_a_up, rw_g_up, rw_k_k, rw_k_a, rw_r_k, rw_gn_g, rw_gn_b, o_w_in, o_w_out, da_qk_g, da_lam, da_subln_g, ret_gn_g, router_w, router_b, ex_w_gu, ex_b_gu, ex_w_dn, ex_b_dn):
    x = jnp.concatenate([x_prompt.reshape(T_PROMPT, D), x_sample.reshape(T_SAMPLE, D)], axis=0)
    cvec8 = jnp.concatenate([c_ctx[None, :], c, jnp.zeros((3, D), F32)], axis=0)
    mods = _adaln(cvec8, ada_w, ada_b)
    mod0 = mods[0].reshape(8, 1, N_MOD * D)
    mod1 = mods[1].reshape(8, 1, N_MOD * D)
    ones_bd = _ones_blockdiag()
    rw_pad = jnp.pad(router_w, ((0, 0), (0, 0), (0, LANES - N_EXPERTS)))
    rb_pad = jnp.pad(router_b, ((0, 0), (0, LANES - N_EXPERTS))).reshape(2, 1, LANES)

    za, zb = _in_proj(x, norm_g[0, 0], mod0, e_w_in[0].astype(BF16), (2 * HALF, B_COLS))
    bs_full = jnp.repeat(sgu_b[0].T, HEAD, axis=1)
    a_out = _sgu(za, sgu_ln_g[0], sgu_w[0].astype(BF16), bs_full)
    zpad = jnp.zeros((2, HEAD, HALF), F32)
    wup_pad = jnp.concatenate([rw_w_up[0], zpad], axis=1).astype(BF16)
    aup_pad = jnp.concatenate([zpad, rw_a_up[0]], axis=1).astype(BF16)
    r, v, kkn, bonus, g, lw, kt, b = _rwkv_prep(zb, rw_mu[0], rw_k_k[0], rw_k_a[0], rw_r_k[0], rw_w0[0], rw_a0[0],
                                                wup_pad, aup_pad, rw_g_up[0].astype(BF16), ones_bd)
    yp_rw, sfin_p = _rwkv_scan(r, v, kkn, lw, kt, b, 0, N_PROMPT, L_PROMPT)
    s0_bd = _bd_pairs(jnp.moveaxis(state_rwkv[:, 0], 1, 0))
    ys_rw, _ = _rwkv_scan(r, v, kkn, lw, kt, b, T_PROMPT, N_SAMPLE, L_SAMPLE, s0_bd)
    y_rw = jnp.concatenate([yp_rw, ys_rw], axis=1)
    new_rwkv = jnp.moveaxis(_bd_unpairs(sfin_p), 0, 1)[:, None]

    row = lambda i: (i, 0)
    half = pl.BlockSpec((TM, HALF), row)
    dir0 = pl.BlockSpec((1, TM, HALF), lambda i: (0, i, 0))
    dir1 = pl.BlockSpec((1, TM, HALF), lambda i: (1, i, 0))
    y0, xp0, ti0, tg0 = _out_proj(
        True,
        [a_out, y_rw, y_rw, bonus, g, rw_gn_g[0].reshape(1, HALF), rw_gn_b[0].reshape(1, HALF), ones_bd],
        [half, dir0, dir1, half, half, _full((1, HALF)), _full((1, HALF)), _full((HALF, HALF))],
        x, mod0, norm_g[0, 1], e_w_out[0].astype(BF16), rw_pad[0], rb_pad[0])
    f0 = _moe(xp0, ti0, tg0, ex_w_gu[0], ex_b_gu[0], ex_w_dn[0], ex_b_dn[0])

    x1, zc, zr = _in_proj(y0, norm_g[1, 0], mod1, o_w_in[0].astype(BF16), (3 * HALF, 3 * HALF), f=f0, mod_prev=mod0)
    cos_tab, sin_tab = _rope_tables()
    qkg = jnp.tile(da_qk_g[0], (1, HALF // HEAD))
    cq, ck, ck_raw, rqk = _odd_prep(zc, zr, cos_tab, sin_tab, qkg, ones_bd)
    lambda_init = 0.8 - 0.6 * math.exp(-0.3 * 1)
    lv = da_lam[0]
    lam = jnp.exp(jnp.sum(lv[0] * lv[1])) - jnp.exp(jnp.sum(lv[2] * lv[3])) + lambda_init
    c_p = _attn(cq, ck, zc, 0, N_PROMPT, L_PROMPT, lam, da_subln_g[0], 1.0 - lambda_init)
    ctx_k = cache_k_diff[:, 0].reshape(N_SAMPLE, PAST, HALF)
    ctx_v = cache_v_diff[:, 0].reshape(N_SAMPLE, PAST, HALF)
    c_s = _attn(cq, ck, zc, T_PROMPT, N_SAMPLE, L_SAMPLE, lam, da_subln_g[0], 1.0 - lambda_init, ctx_k, ctx_v)
    c_out = jnp.concatenate([c_p, c_s], axis=0)
    o_p, rfin_p = _retention(rqk, zr, 0, N_PROMPT, L_PROMPT)
    sr = jnp.moveaxis(state_retention[:, 0], 1, 0)
    zr0 = jnp.zeros_like(sr)
    s0_ret = jnp.stack([jnp.concatenate([sr[:, :, 0], zr0[:, :, 0]], axis=-2),
                        jnp.concatenate([zr0[:, :, 1], sr[:, :, 1]], axis=-2),
                        jnp.concatenate([sr[:, :, 2], zr0[:, :, 2]], axis=-2),
                        jnp.concatenate([zr0[:, :, 3], sr[:, :, 3]], axis=-2)], axis=2)
    o_s, _ = _retention(rqk, zr, T_PROMPT, N_SAMPLE, L_SAMPLE, s0_ret)
    o_ret = jnp.concatenate([o_p, o_s], axis=1)
    new_ret = jnp.stack([rfin_p[:, :, 0, 0:HEAD], rfin_p[:, :, 1, HEAD:], rfin_p[:, :, 2, 0:HEAD],
                         rfin_p[:, :, 3, HEAD:]], axis=2)
    new_ret = jnp.moveaxis(new_ret, 0, 1)[:, None]

    y1, xp1, ti1, tg1 = _out_proj(
        False,
        [c_out, o_ret, o_ret, zr, ret_gn_g[0].reshape(1, HALF)],
        [half, dir0, dir1, pl.BlockSpec((TM, HALF), lambda i: (i, 2)), _full((1, HALF))],
        x1, mod1, norm_g[1, 1], o_w_out[0].astype(BF16), rw_pad[1], rb_pad[1])
    f1 = _moe(xp1, ti1, tg1, ex_w_gu[1], ex_b_gu[1], ex_w_dn[1], ex_b_dn[1])
    y_fin = _final(y1, f1, mod1)

    new_k = ck_raw[:T_PROMPT].reshape(N_PROMPT, 1, L_PROMPT, 4, LANES)
    new_v = zc[:T_PROMPT, 2 * HALF:3 * HALF].reshape(N_PROMPT, 1, L_PROMPT, 4, LANES)
    return (y_fin[:T_PROMPT].reshape(N_PROMPT, L_PROMPT, D), y_fin[T_PROMPT:].reshape(N_SAMPLE, L_SAMPLE, D),
            new_rwkv, new_k, new_v, new_ret)
```

```python
import functools
import math

import numpy as np
import jax
import jax.numpy as jnp
from jax import lax
from jax.experimental import pallas as pl
from jax.experimental.pallas import tpu as pltpu

F32 = jnp.float32
BF16 = jnp.bfloat16
I32 = jnp.int32
HIGHEST = lax.Precision.HIGHEST

D = 1024
N_PROMPT, L_PROMPT = 16, 256
N_SAMPLE, L_SAMPLE = 4, 1024
N_SEQ = N_PROMPT + N_SAMPLE
PAST = 256
T_PROMPT = N_PROMPT * L_PROMPT
T_SAMPLE = N_SAMPLE * L_SAMPLE
T = T_PROMPT + T_SAMPLE
TM = 256
NT = T // TM
PROMPT_TILES = T_PROMPT // TM
TILES_PER_SAMPLE = L_SAMPLE // TM
N_MOD = 6
HALF = 512
B_COLS = 1792
HEAD = 64
W_DECAY_SCALE = math.exp(-0.5)
RWKV_GN_EPS = 64e-5
RW_CHUNK = 64
RET_CHUNK = 128
RET_EXP = ((5.0, 7.0, 9.0, 11.0), (6.0, 8.0, 10.0, 12.0))
N_EXPERTS = 32
TOP_K = 4
SWIGLU_LIMIT = 7.0
SWIGLU_ALPHA = 1.702
N_ASSIGN = T * TOP_K
MOE_TILES = N_ASSIGN // TM + N_EXPERTS
R_PAD = MOE_TILES * TM
LANES = 128

NN = (((1,), (0,)), ((), ()))
NT_DIMS = (((1,), (1,)), ((), ()))
TN = (((0,), (0,)), ((), ()))


def _group(i):
    return jnp.where(i < PROMPT_TILES, 0, 1 + (i - PROMPT_TILES) // TILES_PER_SAMPLE)


def _mm(a, b, dims=NN, passes=1):
    dg = functools.partial(lax.dot_general, dimension_numbers=dims, preferred_element_type=F32)
    if passes == 1:
        return dg(a.astype(BF16), b.astype(BF16))
    a = a.astype(F32)
    b = b.astype(F32)
    ah = a.astype(BF16)
    al = (a - ah.astype(F32)).astype(BF16)
    bh = b.astype(BF16)
    bl = (b - bh.astype(F32)).astype(BF16)
    return dg(ah, bh) + (dg(ah, bl) + dg(al, bh))


def _group_sum(x, ones_bd):
    xh = x.astype(BF16)
    xl = (x - xh.astype(F32)).astype(BF16)
    return (jnp.dot(xh, ones_bd, preferred_element_type=F32)
            + jnp.dot(xl, ones_bd, preferred_element_type=F32))


def _full(shape):
    nd = len(shape)
    return pl.BlockSpec(shape, lambda *_: (0,) * nd)


def _params(sem, vmem_mb=None):
    kw = {}
    if vmem_mb is not None:
        kw["vmem_limit_bytes"] = vmem_mb * 1024 * 1024
    return pltpu.CompilerParams(dimension_semantics=sem, **kw)


def _seq_tables(chunk):
    blk_f, blk_b, first, last, seq = [], [], [], [], []
    row = 0
    for s in range(N_SEQ):
        n = (L_PROMPT if s < N_PROMPT else L_SAMPLE) // chunk
        base = row // chunk
        for j in range(n):
            blk_f.append(base + j)
            blk_b.append(base + n - 1 - j)
            first.append(int(j == 0))
            last.append(int(j == n - 1))
            seq.append(s)
        row += n * chunk
    return tuple(np.asarray(a, np.int32) for a in (blk_f, blk_b, first, last, seq))


def _adaln_kernel(c_ref, w_ref, b_ref, o_ref):
    c = c_ref[...]
    s = c * jax.nn.sigmoid(c)
    o_ref[0] = jnp.dot(s, w_ref[0], precision=HIGHEST, preferred_element_type=F32) + b_ref[0]


def _adaln(cvec8, ada_w, ada_b):
    depth, _, n = ada_w.shape
    bn = 1536
    return pl.pallas_call(
        _adaln_kernel,
        out_shape=jax.ShapeDtypeStruct((depth, 8, n), F32),
        grid=(depth, n // bn),
        in_specs=[pl.BlockSpec((8, D), lambda l, j: (0, 0)),
                  pl.BlockSpec((1, D, bn), lambda l, j: (l, 0, j)),
                  pl.BlockSpec((1, 1, bn), lambda l, j: (l, 0, j))],
        out_specs=pl.BlockSpec((1, 8, bn), lambda l, j: (l, 0, j)),
        compiler_params=_params(("arbitrary", "arbitrary"), 40),
        name="adaln",
    )(cvec8, ada_w, ada_b.reshape(depth, 1, n))


def _in_kernel(x_ref, g_ref, mod_ref, w_ref, *outs, splits):
    x = x_ref[...]
    mod = mod_ref[0]
    y = x * lax.rsqrt(jnp.mean(x * x, axis=-1, keepdims=True) + 1e-6) * g_ref[...]
    h = (y * (1.0 + mod[:, D:2 * D]) + mod[:, 0:D]).astype(BF16)
    off = 0
    for o_ref, n in zip(outs, splits):
        o_ref[...] = jnp.dot(h, w_ref[:, off:off + n], preferred_element_type=F32)
        off += n


def _in_proj(x, g, mod, w_bf16, splits):
    n = w_bf16.shape[1]
    row = lambda i: (i, 0)
    return pl.pallas_call(
        functools.partial(_in_kernel, splits=splits),
        out_shape=[jax.ShapeDtypeStruct((T, s), F32) for s in splits],
        grid=(NT,),
        in_specs=[pl.BlockSpec((TM, D), row), _full((1, D)),
                  pl.BlockSpec((1, 1, N_MOD * D), lambda i: (_group(i), 0, 0)), _full((D, n))],
        out_specs=[pl.BlockSpec((TM, s), row) for s in splits],
        compiler_params=_params(("arbitrary",), 48),
        name="in_proj",
    )(x, g.reshape(1, D), mod, w_bf16)


def _gelu(x):
    return 0.5 * x * (1.0 + lax.erf(x * (1.0 / math.sqrt(2.0))))


def _sgu_kernel(za_ref, lng_ref, ws_ref, bs_ref, o_ref):
    u = _gelu(za_ref[:, 0:HALF])
    va = _gelu(za_ref[:, HALF:2 * HALF])
    mu = jnp.mean(va, axis=-1, keepdims=True)
    dv = va - mu
    var = jnp.mean(dv * dv, axis=-1, keepdims=True)
    vn = dv * lax.rsqrt(var + 1e-5) * lng_ref[...]
    lane = lax.broadcasted_iota(I32, (LANES, LANES), 1)
    first = lane < HEAD
    for c in range(TM // LANES):
        rows = slice(c * LANES, (c + 1) * LANES)
        for p in range(HALF // LANES):
            cols = slice(p * LANES, (p + 1) * LANES)
            vp = vn[rows, cols]
            s = (jnp.dot(ws_ref[2 * p], jnp.where(first, vp, 0.0).astype(BF16), preferred_element_type=F32)
                 + jnp.dot(ws_ref[2 * p + 1], jnp.where(first, 0.0, vp).astype(BF16), preferred_element_type=F32))
            o_ref[rows, cols] = u[rows, cols] * (s + bs_ref[:, cols])


def _sgu(za, ln_g, w_s_bf16, bs_full):
    return pl.pallas_call(
        _sgu_kernel,
        out_shape=jax.ShapeDtypeStruct((T, HALF), F32),
        grid=(NT,),
        in_specs=[pl.BlockSpec((TM, 2 * HALF), lambda i: (i, 0)), _full((1, HALF)),
                  _full((8, LANES, LANES)), _full((LANES, HALF))],
        out_specs=pl.BlockSpec((TM, HALF), lambda i: (i, 0)),
        compiler_params=_params(("arbitrary",)),
        name="sgu",
    )(za, ln_g.reshape(1, HALF), w_s_bf16, bs_full)


def _rwkv_prep_kernel(zb_ref, zp_ref, zn_ref, mu_ref, kk_ref, ka_ref, rk_ref, w0_ref, a0_ref,
                      wup_ref, aup_ref, gup_ref, ones_ref,
                      r_ref, v_ref, kkn_ref, bonus_ref, g_ref, lw_ref, kt_ref, b_ref):
    i = pl.program_id(0)
    in_sample = i >= PROMPT_TILES
    pos = (i - PROMPT_TILES) % TILES_PER_SAMPLE
    is_first = jnp.logical_or(jnp.logical_not(in_sample), pos == 0)
    is_last = jnp.logical_or(jnp.logical_not(in_sample), pos == TILES_PER_SAMPLE - 1)
    zb = zb_ref[...]
    prev_row = jnp.where(is_first, 0.0, zp_ref[7:8, :])
    next_row = jnp.where(is_last, 0.0, zn_ref[0:1, :])
    rowid = lax.broadcasted_iota(I32, (TM, 1), 0)
    zp = jnp.where(rowid == 0, prev_row, pltpu.roll(zb, 1, 0))
    zn = jnp.where(rowid == TM - 1, next_row, pltpu.roll(zb, TM - 1, 0))
    zs = zb + mu_ref[0:1, :] * (zp - zb) + mu_ref[1:2, :] * (zn - zb)
    r = zs[:, 0:HALF]
    k = zs[:, HALF:2 * HALF]
    v = zs[:, 2 * HALF:3 * HALF]
    wa = zs[:, 3 * HALF:3 * HALF + LANES]
    gd = zs[:, 3 * HALF + LANES:B_COLS]
    ones_bd = ones_ref[...]
    r_ref[...] = r
    v_ref[...] = v
    g_ref[...] = jnp.dot(jax.nn.sigmoid(gd).astype(BF16), gup_ref[...], preferred_element_type=F32)
    kk = k * kk_ref[...]
    kkn = kk / jnp.maximum(jnp.sqrt(_group_sum(kk * kk, ones_bd)), 1e-6)
    kkn_ref[...] = kkn
    bonus_ref[...] = _group_sum(r * k * rk_ref[...], ones_bd) * v
    tw = jnp.tanh(wa).astype(BF16)
    wa16 = wa.astype(BF16)
    for dd in range(2):
        lw_ref[dd] = -W_DECAY_SCALE * jax.nn.sigmoid(
            w0_ref[dd:dd + 1, :] + jnp.dot(tw, wup_ref[dd], preferred_element_type=F32))
        a = jax.nn.sigmoid(a0_ref[dd:dd + 1, :] + jnp.dot(wa16, aup_ref[dd], preferred_element_type=F32))
        kt_ref[dd] = k * (1.0 + (a - 1.0) * ka_ref[...])
        b_ref[dd] = a * kkn


def _rwkv_prep(zb, mu, k_k, k_a, r_k, w0, a0, wup_pad, aup_pad, g_up, ones_bd):
    row = lambda i: (i, 0)
    halo = TM // 8
    one = jax.ShapeDtypeStruct((T, HALF), F32)
    two = jax.ShapeDtypeStruct((2, T, HALF), F32)
    o1 = pl.BlockSpec((TM, HALF), row)
    o2 = pl.BlockSpec((2, TM, HALF), lambda i: (0, i, 0))
    return pl.pallas_call(
        _rwkv_prep_kernel,
        out_shape=[one, one, one, one, one, two, two, two],
        grid=(NT,),
        in_specs=[pl.BlockSpec((TM, B_COLS), row),
                  pl.BlockSpec((8, B_COLS), lambda i: (jnp.maximum(i * halo - 1, 0), 0)),
                  pl.BlockSpec((8, B_COLS), lambda i: (jnp.minimum((i + 1) * halo, T // 8 - 1), 0)),
                  _full((2, B_COLS)), _full((1, HALF)), _full((1, HALF)), _full((1, HALF)),
                  _full((2, HALF)), _full((2, HALF)),
                  _full((2, LANES, HALF)), _full((2, LANES, HALF)), _full((LANES, HALF)),
                  _full((HALF, HALF))],
        out_specs=[o1, o1, o1, o1, o1, o2, o2, o2],
        compiler_params=_params(("arbitrary",), 48),
        name="rwkv_prep",
    )(zb, zb, zb, mu, k_k.reshape(1, HALF), k_a.reshape(1, HALF), r_k.reshape(1, HALF), w0, a0,
      wup_pad, aup_pad, g_up, ones_bd)


def _rwkv_chunks(dirs):
    C = RW_CHUNK
    ti = lax.broadcasted_iota(I32, (C, C), 0)
    tj = lax.broadcasted_iota(I32, (C, C), 1)
    bi = lax.broadcasted_iota(I32, (LANES, LANES), 0)
    bj = lax.broadcasted_iota(I32, (LANES, LANES), 1)
    same = (bi >> 6) == (bj >> 6)
    pi = bi & (C - 1)
    pj = bj & (C - 1)
    eye = (bi == bj).astype(F32)
    h0 = lax.broadcasted_iota(I32, (C, LANES), 1) < HEAD

    def stack(x):
        return jnp.concatenate([jnp.where(h0, x, 0.0), jnp.where(h0, 0.0, x)], axis=0)

    def fold(x):
        return x[0:C] + x[C:2 * C]

    chains = []
    for rev, r, v, kk, lw, kt, b, s_ref, y_ref in dirs:
        tri = jnp.where((tj >= ti) if rev else (tj <= ti), 1.0, 0.0).astype(F32)
        cs = jnp.dot(tri, lw, precision=HIGHEST, preferred_element_type=F32)
        ctot = cs[0:1, :] if rev else cs[C - 1:C, :]
        e_neg = jnp.exp(-cs)
        e_tail = jnp.exp(ctot - cs)
        q1 = kk * jnp.exp(cs - lw)
        k1 = kt * e_neg
        b1 = b * e_neg
        r1 = r * jnp.exp(cs)
        k2 = kt * e_tail
        b2 = b * e_tail
        e_tot = jnp.exp(ctot)
        strict = jnp.logical_and(same, (pj > pi) if rev else (pj < pi))
        incl = jnp.logical_and(same, (pj >= pi) if rev else (pj <= pi))
        for p in range(HALF // LANES):
            cols = slice(p * LANES, (p + 1) * LANES)
            chains.append(dict(p=p, cols=cols, strict=strict, incl=incl, s_ref=s_ref, y_ref=y_ref,
                               q1=q1[:, cols], k1=k1[:, cols], b1=b1[:, cols], r1=r1[:, cols],
                               k2=k2[:, cols], b2=b2[:, cols], v=v[:, cols], e_tot=e_tot[:, cols]))

    for ch in chains:
        lhs = jnp.concatenate([stack(ch["q1"]), stack(ch["r1"])], axis=0)
        rhs = jnp.concatenate([ch["k1"], ch["k1"], ch["b1"], ch["b1"]], axis=0)
        gm = _mm(lhs, rhs, NT_DIMS, 3)
        ch["mk"] = jnp.where(ch["strict"], gm[0:2 * C, 0:2 * C], 0.0)
        ch["mb"] = jnp.where(ch["strict"], gm[0:2 * C, 2 * C:4 * C], 0.0)
        ch["nk"] = jnp.where(ch["incl"], gm[2 * C:4 * C, 0:2 * C], 0.0)
        ch["nb"] = jnp.where(ch["incl"], gm[2 * C:4 * C, 2 * C:4 * C], 0.0)
        ch["tinv"] = eye - jnp.where((pi >> 1) == (pj >> 1), ch["mb"], 0.0)
    size = 2
    while size < C:
        sh = size.bit_length() - 1
        blk = jnp.logical_and((pi >> (sh + 1)) == (pj >> (sh + 1)), (pi >> sh) != (pj >> sh))
        for ch in chains:
            ch["tn"] = _mm(ch["tinv"], jnp.where(blk, ch["mb"], 0.0), NN, 3)
        for ch in chains:
            ch["tinv"] = ch["tinv"] - _mm(ch["tn"], ch["tinv"], NN, 3)
        size *= 2
    for ch in chains:
        vst = stack(ch["v"])
        ch["mkv"] = fold(_mm(ch["mk"], vst, NN, 3))
        ch["nkv"] = fold(_mm(ch["nk"], vst, NN, 3))
        ch["s"] = ch["s_ref"][ch["p"]]
        ch["qr"] = _mm(jnp.concatenate([ch["q1"], ch["r1"]], axis=0), ch["s"], NT_DIMS, 3)
    for ch in chains:
        ch["u"] = fold(_mm(ch["tinv"], stack(ch["mkv"] + ch["qr"][0:C]), NN, 3))
    for ch in chains:
        ch["y_ref"][:, ch["cols"]] = ch["qr"][C:2 * C] + ch["nkv"] - fold(_mm(ch["nb"], stack(ch["u"]), NN, 3))
        upd = _mm(jnp.concatenate([ch["v"], ch["u"]], axis=0),
                  jnp.concatenate([ch["k2"], -ch["b2"]], axis=0), TN, 3)
        ch["s_ref"][ch["p"]] = ch["s"] * ch["e_tot"] + jnp.where(same, upd, 0.0)


def _rwkv_scan_kernel(bf_ref, bb_ref, first_ref, last_ref, seq_ref,
                      rf_ref, vf_ref, kkf_ref, lwf_ref, ktf_ref, bfw_ref,
                      rb_ref, vb_ref, kkb_ref, lwb_ref, ktb_ref, bbw_ref, s0_ref,
                      yf_ref, yb_ref, sfin_ref, s_ref):
    step = pl.program_id(0)

    @pl.when(first_ref[step] == 1)
    def _():
        s_ref[...] = s0_ref[:, 0]

    _rwkv_chunks([
        (False, rf_ref[...], vf_ref[...], kkf_ref[...], lwf_ref[0], ktf_ref[0], bfw_ref[0], s_ref.at[0], yf_ref),
        (True, rb_ref[...], vb_ref[...], kkb_ref[...], lwb_ref[0], ktb_ref[0], bbw_ref[0], s_ref.at[1], yb_ref)])

    @pl.when(last_ref[step] == 1)
    def _():
        sfin_ref[:, 0] = s_ref[...]


def _rwkv_scan(r, v, kk, lw, kt, b, s0_bd):
    C = RW_CHUNK
    tabs = _seq_tables(C)
    fwd = lambda i, bf, bb, fi, la, sq: (bf[i], 0)
    bwd = lambda i, bf, bb, fi, la, sq: (bb[i], 0)
    fwd3 = lambda i, bf, bb, fi, la, sq: (0, bf[i], 0)
    bwd3 = lambda i, bf, bb, fi, la, sq: (1, bb[i], 0)
    st = pl.BlockSpec((2, 1, 4, LANES, LANES), lambda i, bf, bb, fi, la, sq: (0, sq[i], 0, 0, 0))
    one_f, one_b = pl.BlockSpec((C, HALF), fwd), pl.BlockSpec((C, HALF), bwd)
    two_f, two_b = pl.BlockSpec((1, C, HALF), fwd3), pl.BlockSpec((1, C, HALF), bwd3)
    return pl.pallas_call(
        _rwkv_scan_kernel,
        out_shape=[jax.ShapeDtypeStruct((T, HALF), F32), jax.ShapeDtypeStruct((T, HALF), F32),
                   jax.ShapeDtypeStruct((2, N_SEQ, 4, LANES, LANES), F32)],
        grid_spec=pltpu.PrefetchScalarGridSpec(
            num_scalar_prefetch=5, grid=(len(tabs[0]),),
            in_specs=[one_f, one_f, one_f, two_f, two_f, two_f,
                      one_b, one_b, one_b, two_b, two_b, two_b, st],
            out_specs=[one_f, one_b, st],
            scratch_shapes=[pltpu.VMEM((2, 4, LANES, LANES), F32)]),
        compiler_params=_params(("arbitrary",)),
        name="rwkv_scan",
    )(*tabs, r, v, kk, lw, kt, b, r, v, kk, lw, kt, b, s0_bd)


def _rope(x, cos, sin_signed, first16):
    w = x.shape[1]
    partner = jnp.where(first16, pltpu.roll(x, w - 16, 1), pltpu.roll(x, 16, 1))
    return x * cos + partner * sin_signed


def _odd_prep_kernel(zc_ref, zr_ref, cos_ref, sin_ref, qkg_ref, ones_ref, cq_ref, ck_ref, ckraw_ref, rqk_ref):
    ones_bd = ones_ref[...]
    cos = cos_ref[...]
    sin = sin_ref[...]
    lane = lax.broadcasted_iota(I32, (TM, HALF), 1)
    first16 = (lane & 31) < 16
    for idx, (o_ref, raw_ref) in enumerate(((cq_ref, None), (ck_ref, ckraw_ref))):
        x = zc_ref[:, idx * HALF:(idx + 1) * HALF]
        ms = _group_sum(x * x, ones_bd) * (1.0 / HEAD)
        xn = x * lax.rsqrt(ms + 1e-6) * qkg_ref[idx:idx + 1, :]
        if raw_ref is not None:
            raw_ref[...] = xn
        o_ref[...] = _rope(xn, cos, sin, first16)
    rqk = _rope(zr_ref[...], cos, sin, first16)
    rqk_ref[...] = jnp.where(lane < HALF // 2, rqk * (HEAD ** -0.5), rqk)


def _odd_prep(zc, zr, cos_tab, sin_tab, qkg_tiled, ones_bd):
    row = lambda i: (i, 0)
    tab = lambda i: (jnp.where(i < PROMPT_TILES, 0, 1 + (i - PROMPT_TILES) % TILES_PER_SAMPLE), 0)
    one = jax.ShapeDtypeStruct((T, HALF), F32)
    o1 = pl.BlockSpec((TM, HALF), row)
    return pl.pallas_call(
        _odd_prep_kernel,
        out_shape=[one, one, one, one], grid=(NT,),
        in_specs=[pl.BlockSpec((TM, 2 * HALF), row), pl.BlockSpec((TM, HALF), row),
                  pl.BlockSpec((TM, HALF), tab), pl.BlockSpec((TM, HALF), tab),
                  _full((2, HALF)), _full((HALF, HALF))],
        out_specs=[o1, o1, o1, o1],
        compiler_params=_params(("arbitrary",)),
        name="odd_prep",
    )(zc, zr, cos_tab, sin_tab, qkg_tiled, ones_bd)


def _attn_kernel(*refs, has_ctx, one_minus_li):
    if has_ctx:
        q_ref, k_ref, v_ref, kc_ref, vc_ref, lam_ref, sg_ref, _, o_ref = refs
    else:
        q_ref, k_ref, v_ref, lam_ref, sg_ref, _, o_ref = refs
    lam = lam_ref[...]
    lane = lax.broadcasted_iota(I32, (LANES, LANES), 1)
    m0 = lane < HEAD
    scale = HEAD ** -0.5
    for h in range(4):
        cols = slice(h * LANES, (h + 1) * LANES)
        qp = q_ref[:, cols]
        segs = [(k_ref[:, cols], v_ref[:, cols])]
        if has_ctx:
            segs.append((kc_ref[0, :, cols], vc_ref[0, :, cols]))
        outs = []
        for qm in (jnp.where(m0, qp, 0.0), jnp.where(m0, 0.0, qp)):
            qm16 = qm.astype(BF16)
            ss = [lax.dot_general(qm16, ks.astype(BF16), NT_DIMS, preferred_element_type=F32) * scale
                  for ks, _ in segs]
            mx = ss[0].max(axis=-1, keepdims=True)
            for s_ in ss[1:]:
                mx = jnp.maximum(mx, s_.max(axis=-1, keepdims=True))
            ps = [jnp.exp(s_ - mx) for s_ in ss]
            den = ps[0].sum(axis=-1, keepdims=True)
            for p_ in ps[1:]:
                den = den + p_.sum(axis=-1, keepdims=True)
            outs.append([p_ / den for p_ in ps])
        acc = None
        for si, (_, vs) in enumerate(segs):
            amap = outs[0][si] - lam * outs[1][si]
            t = jnp.dot(amap.astype(BF16), vs.astype(BF16), preferred_element_type=F32)
            acc = t if acc is None else acc + t
        nrm = acc * lax.rsqrt(jnp.mean(acc * acc, axis=-1, keepdims=True) + 1e-6) * sg_ref[...]
        o_ref[:, cols] = nrm * one_minus_li


def _attn(cq, ck, zc, row0, n_seq, seq_len, lam, subln_g, one_minus_li, prev, ctx_k=None, ctx_v=None):
    nq = seq_len // LANES
    qb0 = row0 // LANES
    sb0 = row0 // seq_len
    in_specs = [pl.BlockSpec((LANES, HALF), lambda s, q: (qb0 + s * nq + q, 0)),
                pl.BlockSpec((seq_len, HALF), lambda s, q: (sb0 + s, 0)),
                pl.BlockSpec((seq_len, HALF), lambda s, q: (sb0 + s, 2))]
    args = [cq, ck, zc]
    if ctx_k is not None:
        in_specs += [pl.BlockSpec((1, PAST, HALF), lambda s, q: (s, 0, 0))] * 2
        args += [ctx_k, ctx_v]
    in_specs += [_full((1, 1)), _full((1, LANES))]
    args += [lam.reshape(1, 1), subln_g.reshape(1, LANES)]
    in_specs.append(pl.BlockSpec(memory_space=pl.ANY))
    args.append(prev)
    aliases = {len(args) - 1: 0}
    return pl.pallas_call(
        functools.partial(_attn_kernel, has_ctx=ctx_k is not None, one_minus_li=one_minus_li),
        out_shape=jax.ShapeDtypeStruct((T, HALF), F32),
        grid=(n_seq, nq), in_specs=in_specs,
        out_specs=pl.BlockSpec((LANES, HALF), lambda s, q: (qb0 + s * nq + q, 0)),
        input_output_aliases=aliases,
        compiler_params=_params(("arbitrary", "arbitrary"), 48),
        name="diff_attn",
    )(*args)


_LOG_GAMMA = tuple(tuple(float(np.log1p(-np.exp2(-np.float32(e)), dtype=np.float32)) for e in es)
                   for es in RET_EXP)


def _ret_chunk(rev, qk_ref, v_ref, s_ref, o_ref):
    C = RET_CHUNK
    ii = lax.broadcasted_iota(I32, (C, C), 0)
    jj = lax.broadcasted_iota(I32, (C, C), 1)
    mask = (jj > ii) if rev else (jj <= ii)
    dist = jnp.where(mask, (jj - ii) if rev else (ii - jj), 0).astype(F32)
    ri = lax.broadcasted_iota(I32, (C, 1), 0)
    kpow = (ri if rev else (C - 1 - ri)).astype(F32)
    qpow = ((C - ri) if rev else (ri + 1)).astype(F32)
    lane = lax.broadcasted_iota(I32, (C, LANES), 1)
    for h in range(4):
        lg = _LOG_GAMMA[1 if rev else 0][h]
        p = h // 2
        hm = (lane < HEAD) if h % 2 == 0 else (lane >= HEAD)
        qp = jnp.where(hm, qk_ref[:, p * LANES:(p + 1) * LANES], 0.0)
        kp = jnp.where(hm, qk_ref[:, HALF // 2 + p * LANES:HALF // 2 + (p + 1) * LANES], 0.0)
        vh = v_ref[:, h * LANES:(h + 1) * LANES].astype(BF16)
        decay = jnp.where(mask, jnp.exp(lg * dist), 0.0)
        sc = lax.dot_general(qp.astype(BF16), kp.astype(BF16), NT_DIMS, preferred_element_type=F32) * decay
        o_intra = jnp.dot(sc.astype(BF16), vh, preferred_element_type=F32)
        s_prev = s_ref[h]
        o_cross = jnp.dot((qp * jnp.exp(lg * qpow)).astype(BF16), s_prev.astype(BF16),
                          preferred_element_type=F32)
        o_ref[:, h * LANES:(h + 1) * LANES] = o_intra + o_cross
        kv = lax.dot_general((kp * jnp.exp(lg * kpow)).astype(BF16), vh, TN, preferred_element_type=F32)
        s_ref[h] = math.exp(lg * C) * s_prev + kv


def _ret_kernel(bf_ref, bb_ref, first_ref, last_ref, seq_ref,
                qkf_ref, vf_ref, qkb_ref, vb_ref, s0_ref, of_ref, ob_ref, sfin_ref, s_ref):
    step = pl.program_id(0)

    @pl.when(first_ref[step] == 1)
    def _():
        s_ref[...] = s0_ref[:, 0]

    _ret_chunk(False, qkf_ref, vf_ref, s_ref.at[0], of_ref)
    _ret_chunk(True, qkb_ref, vb_ref, s_ref.at[1], ob_ref)

    @pl.when(last_ref[step] == 1)
    def _():
        sfin_ref[:, 0] = s_ref[...]


def _retention(rqk, zr, s0):
    C = RET_CHUNK
    tabs = _seq_tables(C)
    st = pl.BlockSpec((2, 1, 4, LANES, LANES), lambda i, bf, bb, fi, la, sq: (0, sq[i], 0, 0, 0))
    spec = lambda use_b, col: pl.BlockSpec(
        (C, HALF), lambda i, bf, bb, fi, la, sq: ((bb if use_b else bf)[i], col))
    return pl.pallas_call(
        _ret_kernel,
        out_shape=[jax.ShapeDtypeStruct((T, HALF), F32), jax.ShapeDtypeStruct((T, HALF), F32),
                   jax.ShapeDtypeStruct((2, N_SEQ, 4, LANES, LANES), F32)],
        grid_spec=pltpu.PrefetchScalarGridSpec(
            num_scalar_prefetch=5, grid=(len(tabs[0]),),
            in_specs=[spec(False, 0), spec(False, 1), spec(True, 0), spec(True, 1), st],
            out_specs=[spec(False, 0), spec(True, 0), st],
            scratch_shapes=[pltpu.VMEM((2, 4, LANES, LANES), F32)]),
        compiler_params=_params(("arbitrary",)),
        name="retention",
    )(*tabs, rqk, zr, rqk, zr, s0)


def _out_kernel(*refs, even):
    if even:
        (a_ref, yf_ref, yb_ref, bonus_ref, g_ref, gng_ref, gnb_ref, ones_ref,
         x_ref, mod_ref, ng_ref, wo_ref, rw_ref, rb_ref,
         y_ref, xp_ref, ti_ref, tg_ref, rk_ref, cnt_ref, run_ref) = refs
        ones_bd = ones_ref[...]
        ys = yf_ref[...] + yb_ref[...]
        mu = _group_sum(ys, ones_bd) * (1.0 / HEAD)
        dv = ys - mu
        var = _group_sum(dv * dv, ones_bd) * (1.0 / HEAD)
        yn = dv * lax.rsqrt(var + RWKV_GN_EPS) * gng_ref[...] + gnb_ref[...]
        left = a_ref[...]
        right = (yn + bonus_ref[...]) * g_ref[...]
    else:
        (c_ref, of_ref, ob_ref, rg_ref, gng_ref,
         x_ref, mod_ref, ng_ref, wo_ref, rw_ref, rb_ref,
         y_ref, xp_ref, ti_ref, tg_ref, rk_ref, cnt_ref, run_ref) = refs
        left = c_ref[...]
        rg = rg_ref[...]
        gate = rg * jax.nn.sigmoid(rg)
        os_ = of_ref[...] + ob_ref[...]
        parts = []
        for h in range(4):
            oh = os_[:, h * LANES:(h + 1) * LANES]
            mu = jnp.mean(oh, axis=-1, keepdims=True)
            dv = oh - mu
            var = jnp.mean(dv * dv, axis=-1, keepdims=True)
            parts.append(dv * lax.rsqrt(var + 1e-5))
        right = gate * (jnp.concatenate(parts, axis=1) * gng_ref[...])
    mod = mod_ref[0]
    o = (jnp.dot(left.astype(BF16), wo_ref[0:HALF, :], preferred_element_type=F32)
         + jnp.dot(right.astype(BF16), wo_ref[HALF:2 * HALF, :], preferred_element_type=F32))
    y = x_ref[...] + mod[:, 2 * D:3 * D] * o
    y_ref[...] = y
    yn2 = y * lax.rsqrt(jnp.mean(y * y, axis=-1, keepdims=True) + 1e-6) * ng_ref[...]
    t = yn2 * (1.0 + mod[:, 4 * D:5 * D]) + mod[:, 3 * D:4 * D]
    xp_ref[...] = t
    logits = jnp.dot(t, rw_ref[...], precision=HIGHEST, preferred_element_type=F32) + rb_ref[...]
    lane = lax.broadcasted_iota(I32, (TM, LANES), 1)
    neg = jnp.float32(-jnp.inf)
    lg = jnp.where(lane < N_EXPERTS, logits, neg)
    vals, hits = [], []
    for _ in range(TOP_K):
        m = jnp.max(lg, axis=-1, keepdims=True)
        ix = jnp.min(jnp.where(lg == m, lane, LANES), axis=-1, keepdims=True)
        hit = lane == ix
        vals.append(m)
        hits.append((ix, hit))
        lg = jnp.where(hit, neg, lg)
    es = [jnp.exp(vv - vals[0]) for vv in vals]
    den = es[0] + es[1] + es[2] + es[3]

    @pl.when(pl.program_id(0) == 0)
    def _():
        run_ref[...] = jnp.zeros_like(run_ref)

    member = jnp.zeros((TM, LANES), F32)
    for _, hit in hits:
        member = member + jnp.where(hit, 1.0, 0.0)
    ri = lax.broadcasted_iota(I32, (TM, TM), 0)
    ci = lax.broadcasted_iota(I32, (TM, TM), 1)
    before = jnp.where(ci < ri, 1.0, 0.0).astype(BF16)
    seen = run_ref[...] + jnp.dot(before, member.astype(BF16), preferred_element_type=F32)
    ti = jnp.zeros((TM, LANES), I32)
    tg = jnp.zeros((TM, LANES), F32)
    rk = jnp.zeros((TM, LANES), F32)
    for kk, (ix, hit) in enumerate(hits):
        ti = jnp.where(lane == kk, ix, ti)
        tg = jnp.where(lane == kk, es[kk] / den, tg)
        rk = jnp.where(lane == kk, jnp.sum(jnp.where(hit, seen, 0.0), axis=-1, keepdims=True), rk)
    ti_ref[...] = ti
    tg_ref[...] = tg
    rk_ref[...] = rk.astype(I32)
    run_ref[...] = run_ref[...] + jnp.sum(member, axis=0, keepdims=True)
    cnt_ref[...] = run_ref[...]


def _out_proj(even, mix_args, mix_specs, x, mod, norm_g, w_out_bf16, rw_pad, rb_pad):
    row = lambda i: (i, 0)
    modspec = pl.BlockSpec((1, 1, N_MOD * D), lambda i: (_group(i), 0, 0))
    in_specs = list(mix_specs) + [pl.BlockSpec((TM, D), row), modspec, _full((1, D)), _full((D, D)),
                                  _full((D, LANES)), _full((1, LANES))]
    args = list(mix_args) + [x, mod, norm_g.reshape(1, D), w_out_bf16, rw_pad, rb_pad]
    lane_i = jax.ShapeDtypeStruct((T, LANES), I32)
    lane_spec = pl.BlockSpec((TM, LANES), row)
    return pl.pallas_call(
        functools.partial(_out_kernel, even=even),
        out_shape=[jax.ShapeDtypeStruct((T, D), F32), jax.ShapeDtypeStruct((T, D), F32),
                   lane_i, jax.ShapeDtypeStruct((T, LANES), F32), lane_i,
                   jax.ShapeDtypeStruct((1, LANES), F32)],
        grid=(NT,), in_specs=in_specs,
        out_specs=[pl.BlockSpec((TM, D), row), pl.BlockSpec((TM, D), row),
                   lane_spec, lane_spec, lane_spec, _full((1, LANES))],
        scratch_shapes=[pltpu.VMEM((1, LANES), F32)],
        compiler_params=_params(("arbitrary",), 48),
        name="out_proj",
    )(*args)


def _route_kernel(cnt_ref, ti_ref, rk_ref, dest_ref, te_ref, nt_ref):
    cnt = cnt_ref[...].astype(I32)
    ntile = lax.shift_right_logical(cnt + (TM - 1), 8)
    ei = lax.broadcasted_iota(I32, (LANES, LANES), 0)
    ej = lax.broadcasted_iota(I32, (LANES, LANES), 1)
    upto = jnp.where(ei <= ej, 1.0, 0.0).astype(BF16)
    ntile_f = jnp.broadcast_to(ntile.astype(F32), (8, LANES))
    tile_end = jnp.dot(ntile_f.astype(BF16), upto, preferred_element_type=F32)[0:1, :]
    row_start = (tile_end - ntile.astype(F32)) * float(TM)
    lane = lax.broadcasted_iota(I32, (TM, LANES), 1)
    ti = ti_ref[...]
    rk = rk_ref[...]
    dest = jnp.zeros((TM, LANES), F32)
    for k in range(TOP_K):
        hit = lane == ti[:, k:k + 1]
        start = jnp.sum(jnp.where(hit, row_start, 0.0), axis=-1, keepdims=True)
        dest = jnp.where(lane == k, start + rk[:, k:k + 1].astype(F32), dest)
    dest_ref[...] = dest.astype(I32)

    @pl.when(pl.program_id(0) == 0)
    def _():
        lane1 = lax.broadcasted_iota(I32, (1, LANES), 1)
        n_tiles = jnp.max(tile_end, axis=-1, keepdims=True)
        last_e = jnp.max(jnp.where(cnt > 0, lane1, 0), axis=-1, keepdims=True)
        tile = lax.broadcasted_iota(I32, (TM, 1), 0).astype(F32)
        te = jnp.sum(jnp.where(tile_end <= tile, 1, 0), axis=-1, keepdims=True)
        te = jnp.where(tile < n_tiles, te, last_e)
        te_ref[...] = jnp.broadcast_to(te, (TM, LANES)).astype(I32)
        nt_ref[...] = jnp.broadcast_to(n_tiles, (8, LANES)).astype(I32)


def _route(cnt, ti, rk):
    row = lambda i: (i, 0)
    return pl.pallas_call(
        _route_kernel,
        out_shape=[jax.ShapeDtypeStruct((T, LANES), I32), jax.ShapeDtypeStruct((TM, LANES), I32),
                   jax.ShapeDtypeStruct((8, LANES), I32)],
        grid=(NT,),
        in_specs=[_full((1, LANES)), pl.BlockSpec((TM, LANES), row), pl.BlockSpec((TM, LANES), row)],
        out_specs=[pl.BlockSpec((TM, LANES), row), _full((TM, LANES)), _full((8, LANES))],
        compiler_params=_params(("arbitrary",)),
        name="moe_route",
    )(cnt, ti, rk)


def _row_copy(src_ref, src_row, dst_ref, dst_row, sem):
    return pltpu.make_async_copy(src_ref.at[pl.ds(src_row, 1)], dst_ref.at[pl.ds(dst_row, 1)], sem)


def _invert_kernel(dest_ref, owner_ref):
    def clear(r, carry):
        owner_ref[r] = -1
        return carry

    lax.fori_loop(0, R_PAD, clear, 0, unroll=8)

    def place(a, carry):
        owner_ref[dest_ref[a]] = a
        return carry

    lax.fori_loop(0, N_ASSIGN, place, 0, unroll=8)


def _invert(dest_flat):
    return pl.pallas_call(
        _invert_kernel,
        out_shape=jax.ShapeDtypeStruct((R_PAD,), I32),
        in_specs=[pl.BlockSpec(memory_space=pltpu.SMEM)],
        out_specs=pl.BlockSpec(memory_space=pltpu.SMEM),
        name="moe_invert",
    )(dest_flat)


def _wait_tile(src_ref, dst_ref, sem):
    pltpu.make_async_copy(src_ref, dst_ref, sem).wait()


def _expert_weights(i, nt, te_ref, w_ref, wbuf_ref, wsem, group_ref):
    first = jnp.logical_or(i == 0, te_ref[i] != te_ref[jnp.maximum(i - 1, 0)])

    @pl.when(i == 0)
    def _():
        group_ref[0] = 0
        pltpu.make_async_copy(w_ref.at[te_ref[0]], wbuf_ref.at[0], wsem.at[0]).start()

    @pl.when(jnp.logical_and(first, i < nt))
    def _():
        cur = group_ref[0] % 2
        nxt = lax.while_loop(
            lambda j: jnp.logical_and(j < nt, te_ref[jnp.minimum(j, MOE_TILES - 1)] == te_ref[i]),
            lambda j: j + 1, i + 1)

        @pl.when(nxt < nt)
        def _():
            pltpu.make_async_copy(w_ref.at[te_ref[jnp.minimum(nxt, MOE_TILES - 1)]], wbuf_ref.at[1 - cur],
                                  wsem.at[1 - cur]).start()

        pltpu.make_async_copy(w_ref.at[0], wbuf_ref.at[cur], wsem.at[cur]).wait()
        group_ref[0] = group_ref[0] + 1

    return (group_ref[0] + 1) % 2


def _moe_up_kernel(te_ref, nt_ref, owner_ref, x_ref, w_ref, b_ref, act_ref,
                   xg0_ref, xg1_ref, sem, wbuf_ref, wsem, group_ref):
    i = pl.program_id(0)
    slot = i % 2
    wcur = _expert_weights(i, nt_ref[0], te_ref, w_ref, wbuf_ref, wsem, group_ref)
    xg = (xg0_ref, xg1_ref)
    whole = x_ref.at[pl.ds(0, TM)]

    def gather(tile, buf):
        base = tile * TM
        for r in range(TM):
            tok = lax.shift_right_logical(jnp.maximum(owner_ref[base + r], 0), 2)
            _row_copy(x_ref, tok, xg[buf], r, sem.at[buf]).start()

    @pl.when(i == 0)
    def _():
        gather(0, 0)

    for s in range(2):
        @pl.when(slot == s)
        def _():
            _wait_tile(whole, xg[s], sem.at[s])
            gather(jnp.minimum(i + 1, MOE_TILES - 1), 1 - s)
            gu = (jnp.dot(xg[s][...].astype(BF16), wbuf_ref[wcur].astype(BF16), preferred_element_type=F32)
                  + b_ref[0])
            gt = jnp.minimum(gu[:, 0:D], SWIGLU_LIMIT)
            up = jnp.clip(gu[:, D:2 * D], -SWIGLU_LIMIT, SWIGLU_LIMIT)
            act_ref[...] = ((up + 1.0) * gt * jax.nn.sigmoid(SWIGLU_ALPHA * gt)).astype(BF16)

        @pl.when(jnp.logical_and(i == MOE_TILES - 1, slot == s))
        def _():
            _wait_tile(whole, xg[1 - s], sem.at[1 - s])


def _moe_up(te, n_tiles, owner, x, w_gu, b_gu):
    return pl.pallas_call(
        _moe_up_kernel,
        out_shape=jax.ShapeDtypeStruct((R_PAD, D), BF16),
        grid_spec=pltpu.PrefetchScalarGridSpec(
            num_scalar_prefetch=3, grid=(MOE_TILES,),
            in_specs=[pl.BlockSpec(memory_space=pl.ANY), pl.BlockSpec(memory_space=pl.ANY),
                      pl.BlockSpec((1, 1, 2 * D), lambda i, te, nt, ow: (te[i], 0, 0))],
            out_specs=pl.BlockSpec((TM, D), lambda i, te, nt, ow: (i, 0)),
            scratch_shapes=[pltpu.VMEM((TM, D), F32), pltpu.VMEM((TM, D), F32), pltpu.SemaphoreType.DMA((2,)),
                            pltpu.VMEM((2, D, 2 * D), F32), pltpu.SemaphoreType.DMA((2,)),
                            pltpu.SMEM((1,), I32)]),
        compiler_params=_params(("arbitrary",), 56),
        name="moe_up",
    )(te, n_tiles, owner, x, w_gu, b_gu)


def _moe_down_kernel(te_ref, nt_ref, owner_ref, act_ref, w_ref, b_ref, ys_ref, y_ref, sem, wbuf_ref, wsem, group_ref):
    i = pl.program_id(0)
    slot = i % 2
    wcur = _expert_weights(i, nt_ref[0], te_ref, w_ref, wbuf_ref, wsem, group_ref)
    spare = ys_ref.at[pl.ds(TOP_K * T, TM)]

    def scatter(tile, buf):
        base = tile * TM
        for r in range(TM):
            a = owner_ref[base + r]
            dst = jnp.where(a < 0, TOP_K * T + r, jnp.bitwise_and(a, TOP_K - 1) * T + lax.shift_right_logical(a, 2))
            _row_copy(y_ref.at[buf], r, ys_ref, dst, sem).start()

    def project(buf):
        y_ref[buf] = (jnp.dot(act_ref[...], wbuf_ref[wcur].astype(BF16), preferred_element_type=F32) + b_ref[0])

    @pl.when(i == 0)
    def _():
        y_ref[1] = jnp.zeros((TM, D), F32)
        pltpu.make_async_copy(y_ref.at[1], spare, sem).start()
        _wait_tile(y_ref.at[1], spare, sem)
        project(0)

    @pl.when(i >= 2)
    def _():
        _wait_tile(y_ref.at[slot], spare, sem)

    for s in range(2):
        @pl.when(jnp.logical_and(i >= 1, slot == s))
        def _():
            scatter(i - 1, 1 - s)
            project(s)

    @pl.when(i == MOE_TILES - 1)
    def _():
        _wait_tile(y_ref.at[1 - slot], spare, sem)
        scatter(i, slot)
        _wait_tile(y_ref.at[slot], spare, sem)


def _tile_clamped(i, te, nt, ow):
    return (jnp.minimum(i, jnp.maximum(nt[0] - 1, 0)), 0)


def _moe_down(te, n_tiles, owner, act, w_dn, b_dn):
    return pl.pallas_call(
        _moe_down_kernel,
        out_shape=jax.ShapeDtypeStruct((TOP_K * T + TM, D), F32),
        grid_spec=pltpu.PrefetchScalarGridSpec(
            num_scalar_prefetch=3, grid=(MOE_TILES,),
            in_specs=[pl.BlockSpec((TM, D), _tile_clamped), pl.BlockSpec(memory_space=pl.ANY),
                      pl.BlockSpec((1, 1, D), lambda i, te, nt, ow: (te[i], 0, 0))],
            out_specs=pl.BlockSpec(memory_space=pl.ANY),
            scratch_shapes=[pltpu.VMEM((2, TM, D), F32), pltpu.SemaphoreType.DMA(()),
                            pltpu.VMEM((2, D, D), F32), pltpu.SemaphoreType.DMA((2,)),
                            pltpu.SMEM((1,), I32)]),
        compiler_params=_params(("arbitrary",), 40),
        name="moe_down",
    )(te, n_tiles, owner, act, w_dn, b_dn)


def _combine_kernel(x_ref, tg_ref, mod_ref, y0_ref, y1_ref, y2_ref, y3_ref, o_ref):
    tg = tg_ref[...]
    f = tg[:, 0:1] * y0_ref[...]
    for k, y_ref in ((1, y1_ref), (2, y2_ref), (3, y3_ref)):
        f = f + tg[:, k:k + 1] * y_ref[...]
    o_ref[...] = x_ref[...] + mod_ref[0][:, 5 * D:6 * D] * f


def _combine(x, tg, mod, ys):
    row = lambda i: (i, 0)
    slot_spec = lambda k: pl.BlockSpec((TM, D), lambda i: (k * NT + i, 0))
    return pl.pallas_call(
        _combine_kernel,
        out_shape=jax.ShapeDtypeStruct((T, D), F32),
        grid=(NT,),
        in_specs=[pl.BlockSpec((TM, D), row), pl.BlockSpec((TM, LANES), row),
                  pl.BlockSpec((1, 1, N_MOD * D), lambda i: (_group(i), 0, 0)),
                  slot_spec(0), slot_spec(1), slot_spec(2), slot_spec(3)],
        out_specs=pl.BlockSpec((TM, D), row),
        compiler_params=_params(("arbitrary",)),
        name="moe_combine",
    )(x, tg, mod, ys, ys, ys, ys)


def _moe(layer, y, xt, ti, tg, rk, cnt, mod, w_gu, b_gu, w_dn, b_dn):
    dest, te, nt = _route(cnt, ti, rk)
    owner = _invert(dest[:, :TOP_K].reshape(-1))
    te = te[:MOE_TILES, 0] + layer * N_EXPERTS
    n_tiles = nt[0, :1]
    n_all = w_gu.shape[0] * N_EXPERTS
    act = _moe_up(te, n_tiles, owner, xt, w_gu.reshape(n_all, D, 2 * D), b_gu.reshape(n_all, 1, 2 * D))
    ys = _moe_down(te, n_tiles, owner, act, w_dn.reshape(n_all, D, D), b_dn.reshape(n_all, 1, D))
    return _combine(y, tg, mod, ys)


def _ones_blockdiag():
    idx = np.arange(HALF) // HEAD
    return jnp.asarray((idx[:, None] == idx[None, :]).astype(np.float32), dtype=BF16)


def _rope_tables():
    pos = jnp.arange(L_SAMPLE)
    rowp = (pos // 64).astype(F32)
    colp = (pos % 64).astype(F32)
    nf = HEAD // 4
    inv = jnp.power(10000.0, -jnp.arange(nf, dtype=F32) / nf)
    ar = rowp[:, None] * inv[None, :]
    ac = colp[:, None] * inv[None, :]
    cos64 = jnp.concatenate([jnp.cos(ar), jnp.cos(ar), jnp.cos(ac), jnp.cos(ac)], axis=1)
    sin64 = jnp.concatenate([-jnp.sin(ar), jnp.sin(ar), -jnp.sin(ac), jnp.sin(ac)], axis=1)
    cos = jnp.tile(cos64, (1, HALF // HEAD))
    sin = jnp.tile(sin64, (1, HALF // HEAD))
    ident = jnp.ones((TM, HALF), F32)
    return (jnp.concatenate([ident, cos], axis=0), jnp.concatenate([jnp.zeros((TM, HALF), F32), sin], axis=0))


def _bd_pairs(s):
    lead = s.shape[:-3]
    s = s.reshape(lead + (4, 2, HEAD, HEAD))
    z = jnp.zeros_like(s[..., 0, :, :])
    top = jnp.concatenate([s[..., 0, :, :], z], axis=-1)
    bot = jnp.concatenate([z, s[..., 1, :, :]], axis=-1)
    return jnp.concatenate([top, bot], axis=-2)


def _bd_unpairs(s):
    a = s[..., 0:HEAD, 0:HEAD]
    b = s[..., HEAD:, HEAD:]
    out = jnp.stack([a, b], axis=-3)
    return out.reshape(s.shape[:-3] + (8, HEAD, HEAD))


def kernel(x_prompt, x_sample, state_rwkv, cache_k_diff, cache_v_diff, state_retention, c, c_ctx, norm_g, ada_w, ada_b, e_w_in, e_w_out, sgu_ln_g, sgu_w, sgu_b, rw_mu, rw_w0, rw_w_up, rw_a0, rw_a_up, rw_g_up, rw_k_k, rw_k_a, rw_r_k, rw_gn_g, rw_gn_b, o_w_in, o_w_out, da_qk_g, da_lam, da_subln_g, ret_gn_g, router_w, router_b, ex_w_gu, ex_b_gu, ex_w_dn, ex_b_dn):
    x = jnp.concatenate([x_prompt.reshape(T_PROMPT, D), x_sample.reshape(T_SAMPLE, D)], axis=0)
    cvec8 = jnp.concatenate([c_ctx[None, :], c, jnp.zeros((3, D), F32)], axis=0)
    mods = _adaln(cvec8, ada_w, ada_b)
    mod0 = mods[0].reshape(8, 1, N_MOD * D)
    mod1 = mods[1].reshape(8, 1, N_MOD * D)
    ones_bd = _ones_blockdiag()
    rw_pad = jnp.pad(router_w, ((0, 0), (0, 0), (0, LANES - N_EXPERTS)))
    rb_pad = jnp.pad(router_b, ((0, 0), (0, LANES - N_EXPERTS))).reshape(2, 1, LANES)
    row = lambda i: (i, 0)
    half = pl.BlockSpec((TM, HALF), row)

    za, zb = _in_proj(x, norm_g[0, 0], mod0, e_w_in[0].astype(BF16), (2 * HALF, B_COLS))
    bs_full = jnp.repeat(sgu_b[0].T, HEAD, axis=1)
    a_out = _sgu(za, sgu_ln_g[0], sgu_w[0].astype(BF16), bs_full)
    zpad = jnp.zeros((2, HEAD, HALF), F32)
    wup_pad = jnp.concatenate([rw_w_up[0], zpad], axis=1).astype(BF16)
    aup_pad = jnp.concatenate([zpad, rw_a_up[0]], axis=1).astype(BF16)
    r, v, kkn, bonus, g, lw, kt, b = _rwkv_prep(zb, rw_mu[0], rw_k_k[0], rw_k_a[0], rw_r_k[0], rw_w0[0], rw_a0[0],
                                                wup_pad, aup_pad, rw_g_up[0].astype(BF16), ones_bd)
    s0_sample = _bd_pairs(jnp.moveaxis(state_rwkv[:, 0], 1, 0))
    s0_rw = jnp.concatenate([jnp.zeros((2, N_PROMPT, 4, LANES, LANES), F32), s0_sample], axis=1)
    yf_rw, yb_rw, sfin_rw = _rwkv_scan(r, v, kkn, lw, kt, b, s0_rw)
    new_rwkv = jnp.moveaxis(_bd_unpairs(sfin_rw[:, :N_PROMPT]), 0, 1)[:, None]
    y0, xp0, ti0, tg0, rk0, cnt0 = _out_proj(
        True,
        [a_out, yf_rw, yb_rw, bonus, g, rw_gn_g[0].reshape(1, HALF), rw_gn_b[0].reshape(1, HALF), ones_bd],
        [half, half, half, half, half, _full((1, HALF)), _full((1, HALF)), _full((HALF, HALF))],
        x, mod0, norm_g[0, 1], e_w_out[0].astype(BF16), rw_pad[0], rb_pad[0])
    x1 = _moe(0, y0, xp0, ti0, tg0, rk0, cnt0, mod0, ex_w_gu, ex_b_gu, ex_w_dn, ex_b_dn)

    zc, zr = _in_proj(x1, norm_g[1, 0], mod1, o_w_in[0].astype(BF16), (3 * HALF, 3 * HALF))
    cos_tab, sin_tab = _rope_tables()
    qkg = jnp.tile(da_qk_g[0], (1, HALF // HEAD))
    cq, ck, ck_raw, rqk = _odd_prep(zc, zr, cos_tab, sin_tab, qkg, ones_bd)
    lambda_init = 0.8 - 0.6 * math.exp(-0.3 * 1)
    lv = da_lam[0]
    lam = jnp.exp(jnp.sum(lv[0] * lv[1])) - jnp.exp(jnp.sum(lv[2] * lv[3])) + lambda_init
    c_out = _attn(cq, ck, zc, 0, N_PROMPT, L_PROMPT, lam, da_subln_g[0], 1.0 - lambda_init,
                  jnp.zeros((T, HALF), F32))
    ctx_k = cache_k_diff[:, 0].reshape(N_SAMPLE, PAST, HALF)
    ctx_v = cache_v_diff[:, 0].reshape(N_SAMPLE, PAST, HALF)
    c_out = _attn(cq, ck, zc, T_PROMPT, N_SAMPLE, L_SAMPLE, lam, da_subln_g[0], 1.0 - lambda_init,
                  c_out, ctx_k, ctx_v)
    sr = jnp.moveaxis(state_retention[:, 0], 1, 0)
    zr0 = jnp.zeros_like(sr)
    s0_sample = jnp.stack([jnp.concatenate([sr[:, :, 0], zr0[:, :, 0]], axis=-2),
                           jnp.concatenate([zr0[:, :, 1], sr[:, :, 1]], axis=-2),
                           jnp.concatenate([sr[:, :, 2], zr0[:, :, 2]], axis=-2),
                           jnp.concatenate([zr0[:, :, 3], sr[:, :, 3]], axis=-2)], axis=2)
    s0_ret = jnp.concatenate([jnp.zeros((2, N_PROMPT, 4, LANES, LANES), F32), s0_sample], axis=1)
    of_ret, ob_ret, rfin = _retention(rqk, zr, s0_ret)
    rfin_p = rfin[:, :N_PROMPT]
    new_ret = jnp.stack([rfin_p[:, :, 0, 0:HEAD], rfin_p[:, :, 1, HEAD:], rfin_p[:, :, 2, 0:HEAD],
                         rfin_p[:, :, 3, HEAD:]], axis=2)
    new_ret = jnp.moveaxis(new_ret, 0, 1)[:, None]
    y1, xp1, ti1, tg1, rk1, cnt1 = _out_proj(
        False,
        [c_out, of_ret, ob_ret, zr, ret_gn_g[0].reshape(1, HALF)],
        [half, half, half, pl.BlockSpec((TM, HALF), lambda i: (i, 2)), _full((1, HALF))],
        x1, mod1, norm_g[1, 1], o_w_out[0].astype(BF16), rw_pad[1], rb_pad[1])
    y_fin = _moe(1, y1, xp1, ti1, tg1, rk1, cnt1, mod1, ex_w_gu, ex_b_gu, ex_w_dn, ex_b_dn)

    new_k = ck_raw[:T_PROMPT].reshape(N_PROMPT, 1, L_PROMPT, 4, LANES)
    new_v = zc[:T_PROMPT, 2 * HALF:3 * HALF].reshape(N_PROMPT, 1, L_PROMPT, 4, LANES)
    return (y_fin[:T_PROMPT].reshape(N_PROMPT, L_PROMPT, D), y_fin[T_PROMPT:].reshape(N_SAMPLE, L_SAMPLE, D),
            new_rwkv, new_k, new_v, new_ret)
```

```python
import functools
import math

import numpy as np
import jax
import jax.numpy as jnp
from jax import lax
from jax.experimental import pallas as pl
from jax.experimental.pallas import tpu as pltpu

F32 = jnp.float32
BF16 = jnp.bfloat16
I32 = jnp.int32
HIGHEST = lax.Precision.HIGHEST

D = 1024
N_PROMPT, L_PROMPT = 16, 256
N_SAMPLE, L_SAMPLE = 4, 1024
N_SEQ = N_PROMPT + N_SAMPLE
PAST = 256
T_PROMPT = N_PROMPT * L_PROMPT
T_SAMPLE = N_SAMPLE * L_SAMPLE
T = T_PROMPT + T_SAMPLE
TM = 256
NT = T // TM
PROMPT_TILES = T_PROMPT // TM
TILES_PER_SAMPLE = L_SAMPLE // TM
N_MOD = 6
HALF = 512
B_COLS = 1792
HEAD = 64
W_DECAY_SCALE = math.exp(-0.5)
RWKV_GN_EPS = 64e-5
RW_CHUNK = 64
RET_CHUNK = 128
RET_EXP = ((5.0, 7.0, 9.0, 11.0), (6.0, 8.0, 10.0, 12.0))
N_EXPERTS = 32
TOP_K = 4
SWIGLU_LIMIT = 7.0
SWIGLU_ALPHA = 1.702
N_ASSIGN = T * TOP_K
MOE_TILES = N_ASSIGN // TM + N_EXPERTS
R_PAD = MOE_TILES * TM
LANES = 128

NN = (((1,), (0,)), ((), ()))
NT_DIMS = (((1,), (1,)), ((), ()))
TN = (((0,), (0,)), ((), ()))


def _group(i):
    return jnp.where(i < PROMPT_TILES, 0, 1 + (i - PROMPT_TILES) // TILES_PER_SAMPLE)


def _mm(a, b, dims=NN, passes=1):
    dg = functools.partial(lax.dot_general, dimension_numbers=dims, preferred_element_type=F32)
    if passes == 1:
        return dg(a.astype(BF16), b.astype(BF16))
    a = a.astype(F32)
    b = b.astype(F32)
    ah = a.astype(BF16)
    al = (a - ah.astype(F32)).astype(BF16)
    bh = b.astype(BF16)
    bl = (b - bh.astype(F32)).astype(BF16)
    return dg(ah, bh) + (dg(ah, bl) + dg(al, bh))


def _group_sum(x, ones_bd):
    xh = x.astype(BF16)
    xl = (x - xh.astype(F32)).astype(BF16)
    return (jnp.dot(xh, ones_bd, preferred_element_type=F32)
            + jnp.dot(xl, ones_bd, preferred_element_type=F32))


def _full(shape):
    nd = len(shape)
    return pl.BlockSpec(shape, lambda *_: (0,) * nd)


def _params(sem, vmem_mb=None):
    kw = {}
    if vmem_mb is not None:
        kw["vmem_limit_bytes"] = vmem_mb * 1024 * 1024
    return pltpu.CompilerParams(dimension_semantics=sem, **kw)


def _seq_tables(chunk):
    blk_f, blk_b, first, last, seq = [], [], [], [], []
    row = 0
    for s in range(N_SEQ):
        n = (L_PROMPT if s < N_PROMPT else L_SAMPLE) // chunk
        base = row // chunk
        for j in range(n):
            blk_f.append(base + j)
            blk_b.append(base + n - 1 - j)
            first.append(int(j == 0))
            last.append(int(j == n - 1))
            seq.append(s)
        row += n * chunk
    return tuple(np.asarray(a, np.int32) for a in (blk_f, blk_b, first, last, seq))


def _adaln_kernel(c_ref, w_ref, b_ref, o_ref):
    c = c_ref[...]
    s = c * jax.nn.sigmoid(c)
    o_ref[0] = jnp.dot(s, w_ref[0], precision=HIGHEST, preferred_element_type=F32) + b_ref[0]


def _adaln(cvec8, ada_w, ada_b):
    depth, _, n = ada_w.shape
    bn = 1536
    return pl.pallas_call(
        _adaln_kernel,
        out_shape=jax.ShapeDtypeStruct((depth, 8, n), F32),
        grid=(depth, n // bn),
        in_specs=[pl.BlockSpec((8, D), lambda l, j: (0, 0)),
                  pl.BlockSpec((1, D, bn), lambda l, j: (l, 0, j)),
                  pl.BlockSpec((1, 1, bn), lambda l, j: (l, 0, j))],
        out_specs=pl.BlockSpec((1, 8, bn), lambda l, j: (l, 0, j)),
        compiler_params=_params(("arbitrary", "arbitrary"), 40),
        name="adaln",
    )(cvec8, ada_w, ada_b.reshape(depth, 1, n))


def _in_kernel(x_ref, g_ref, mod_ref, w_ref, *outs, splits):
    x = x_ref[...]
    mod = mod_ref[0]
    y = x * lax.rsqrt(jnp.mean(x * x, axis=-1, keepdims=True) + 1e-6) * g_ref[...]
    h = (y * (1.0 + mod[:, D:2 * D]) + mod[:, 0:D]).astype(BF16)
    off = 0
    for o_ref, n in zip(outs, splits):
        o_ref[...] = jnp.dot(h, w_ref[:, off:off + n], preferred_element_type=F32)
        off += n


def _in_proj(x, g, mod, w_bf16, splits):
    n = w_bf16.shape[1]
    row = lambda i: (i, 0)
    return pl.pallas_call(
        functools.partial(_in_kernel, splits=splits),
        out_shape=[jax.ShapeDtypeStruct((T, s), F32) for s in splits],
        grid=(NT,),
        in_specs=[pl.BlockSpec((TM, D), row), _full((1, D)),
                  pl.BlockSpec((1, 1, N_MOD * D), lambda i: (_group(i), 0, 0)), _full((D, n))],
        out_specs=[pl.BlockSpec((TM, s), row) for s in splits],
        compiler_params=_params(("arbitrary",), 48),
        name="in_proj",
    )(x, g.reshape(1, D), mod, w_bf16)


def _gelu(x):
    return 0.5 * x * (1.0 + lax.erf(x * (1.0 / math.sqrt(2.0))))


def _sgu_kernel(za_ref, lng_ref, ws_ref, bs_ref, o_ref):
    u = _gelu(za_ref[:, 0:HALF])
    va = _gelu(za_ref[:, HALF:2 * HALF])
    mu = jnp.mean(va, axis=-1, keepdims=True)
    dv = va - mu
    var = jnp.mean(dv * dv, axis=-1, keepdims=True)
    vn = dv * lax.rsqrt(var + 1e-5) * lng_ref[...]
    lane = lax.broadcasted_iota(I32, (LANES, LANES), 1)
    first = lane < HEAD
    for c in range(TM // LANES):
        rows = slice(c * LANES, (c + 1) * LANES)
        for p in range(HALF // LANES):
            cols = slice(p * LANES, (p + 1) * LANES)
            vp = vn[rows, cols]
            s = (jnp.dot(ws_ref[2 * p], jnp.where(first, vp, 0.0).astype(BF16), preferred_element_type=F32)
                 + jnp.dot(ws_ref[2 * p + 1], jnp.where(first, 0.0, vp).astype(BF16), preferred_element_type=F32))
            o_ref[rows, cols] = u[rows, cols] * (s + bs_ref[:, cols])


def _sgu(za, ln_g, w_s_bf16, bs_full):
    return pl.pallas_call(
        _sgu_kernel,
        out_shape=jax.ShapeDtypeStruct((T, HALF), F32),
        grid=(NT,),
        in_specs=[pl.BlockSpec((TM, 2 * HALF), lambda i: (i, 0)), _full((1, HALF)),
                  _full((8, LANES, LANES)), _full((LANES, HALF))],
        out_specs=pl.BlockSpec((TM, HALF), lambda i: (i, 0)),
        compiler_params=_params(("arbitrary",)),
        name="sgu",
    )(za, ln_g.reshape(1, HALF), w_s_bf16, bs_full)


def _rwkv_prep_kernel(zb_ref, zp_ref, zn_ref, mu_ref, kk_ref, ka_ref, rk_ref, w0_ref, a0_ref,
                      wup_ref, aup_ref, gup_ref, ones_ref,
                      r_ref, v_ref, kkn_ref, bonus_ref, g_ref, lw_ref, kt_ref, b_ref):
    i = pl.program_id(0)
    in_sample = i >= PROMPT_TILES
    pos = (i - PROMPT_TILES) % TILES_PER_SAMPLE
    is_first = jnp.logical_or(jnp.logical_not(in_sample), pos == 0)
    is_last = jnp.logical_or(jnp.logical_not(in_sample), pos == TILES_PER_SAMPLE - 1)
    zb = zb_ref[...]
    prev_row = jnp.where(is_first, 0.0, zp_ref[7:8, :])
    next_row = jnp.where(is_last, 0.0, zn_ref[0:1, :])
    rowid = lax.broadcasted_iota(I32, (TM, 1), 0)
    zp = jnp.where(rowid == 0, prev_row, pltpu.roll(zb, 1, 0))
    zn = jnp.where(rowid == TM - 1, next_row, pltpu.roll(zb, TM - 1, 0))
    zs = zb + mu_ref[0:1, :] * (zp - zb) + mu_ref[1:2, :] * (zn - zb)
    r = zs[:, 0:HALF]
    k = zs[:, HALF:2 * HALF]
    v = zs[:, 2 * HALF:3 * HALF]
    wa = zs[:, 3 * HALF:3 * HALF + LANES]
    gd = zs[:, 3 * HALF + LANES:B_COLS]
    ones_bd = ones_ref[...]
    r_ref[...] = r
    v_ref[...] = v
    g_ref[...] = jnp.dot(jax.nn.sigmoid(gd).astype(BF16), gup_ref[...], preferred_element_type=F32)
    kk = k * kk_ref[...]
    kkn = kk / jnp.maximum(jnp.sqrt(_group_sum(kk * kk, ones_bd)), 1e-6)
    kkn_ref[...] = kkn
    bonus_ref[...] = _group_sum(r * k * rk_ref[...], ones_bd) * v
    tw = jnp.tanh(wa).astype(BF16)
    wa16 = wa.astype(BF16)
    for dd in range(2):
        lw_ref[dd] = -W_DECAY_SCALE * jax.nn.sigmoid(
            w0_ref[dd:dd + 1, :] + jnp.dot(tw, wup_ref[dd], preferred_element_type=F32))
        a = jax.nn.sigmoid(a0_ref[dd:dd + 1, :] + jnp.dot(wa16, aup_ref[dd], preferred_element_type=F32))
        kt_ref[dd] = k * (1.0 + (a - 1.0) * ka_ref[...])
        b_ref[dd] = a * kkn


def _rwkv_prep(zb, mu, k_k, k_a, r_k, w0, a0, wup_pad, aup_pad, g_up, ones_bd):
    row = lambda i: (i, 0)
    halo = TM // 8
    one = jax.ShapeDtypeStruct((T, HALF), F32)
    two = jax.ShapeDtypeStruct((2, T, HALF), F32)
    o1 = pl.BlockSpec((TM, HALF), row)
    o2 = pl.BlockSpec((2, TM, HALF), lambda i: (0, i, 0))
    return pl.pallas_call(
        _rwkv_prep_kernel,
        out_shape=[one, one, one, one, one, two, two, two],
        grid=(NT,),
        in_specs=[pl.BlockSpec((TM, B_COLS), row),
                  pl.BlockSpec((8, B_COLS), lambda i: (jnp.maximum(i * halo - 1, 0), 0)),
                  pl.BlockSpec((8, B_COLS), lambda i: (jnp.minimum((i + 1) * halo, T // 8 - 1), 0)),
                  _full((2, B_COLS)), _full((1, HALF)), _full((1, HALF)), _full((1, HALF)),
                  _full((2, HALF)), _full((2, HALF)),
                  _full((2, LANES, HALF)), _full((2, LANES, HALF)), _full((LANES, HALF)),
                  _full((HALF, HALF))],
        out_specs=[o1, o1, o1, o1, o1, o2, o2, o2],
        compiler_params=_params(("arbitrary",), 48),
        name="rwkv_prep",
    )(zb, zb, zb, mu, k_k.reshape(1, HALF), k_a.reshape(1, HALF), r_k.reshape(1, HALF), w0, a0,
      wup_pad, aup_pad, g_up, ones_bd)


def _rwkv_chunks(dirs):
    C = RW_CHUNK
    ti = lax.broadcasted_iota(I32, (C, C), 0)
    tj = lax.broadcasted_iota(I32, (C, C), 1)
    bi = lax.broadcasted_iota(I32, (LANES, LANES), 0)
    bj = lax.broadcasted_iota(I32, (LANES, LANES), 1)
    same = (bi >> 6) == (bj >> 6)
    pi = bi & (C - 1)
    pj = bj & (C - 1)
    eye = (bi == bj).astype(F32)
    h0 = lax.broadcasted_iota(I32, (C, LANES), 1) < HEAD

    def stack(x):
        return jnp.concatenate([jnp.where(h0, x, 0.0), jnp.where(h0, 0.0, x)], axis=0)

    def fold(x):
        return x[0:C] + x[C:2 * C]

    chains = []
    for rev, r, v, kk, lw, kt, b, s_ref, y_ref in dirs:
        tri = jnp.where((tj >= ti) if rev else (tj <= ti), 1.0, 0.0).astype(F32)
        cs = jnp.dot(tri, lw, precision=HIGHEST, preferred_element_type=F32)
        ctot = cs[0:1, :] if rev else cs[C - 1:C, :]
        e_neg = jnp.exp(-cs)
        e_tail = jnp.exp(ctot - cs)
        q1 = kk * jnp.exp(cs - lw)
        k1 = kt * e_neg
        b1 = b * e_neg
        r1 = r * jnp.exp(cs)
        k2 = kt * e_tail
        b2 = b * e_tail
        e_tot = jnp.exp(ctot)
        strict = jnp.logical_and(same, (pj > pi) if rev else (pj < pi))
        incl = jnp.logical_and(same, (pj >= pi) if rev else (pj <= pi))
        for p in range(HALF // LANES):
            cols = slice(p * LANES, (p + 1) * LANES)
            chains.append(dict(p=p, cols=cols, strict=strict, incl=incl, s_ref=s_ref, y_ref=y_ref,
                               q1=q1[:, cols], k1=k1[:, cols], b1=b1[:, cols], r1=r1[:, cols],
                               k2=k2[:, cols], b2=b2[:, cols], v=v[:, cols], e_tot=e_tot[:, cols]))

    for ch in chains:
        lhs = jnp.concatenate([stack(ch["q1"]), stack(ch["r1"])], axis=0)
        rhs = jnp.concatenate([ch["k1"], ch["k1"], ch["b1"], ch["b1"]], axis=0)
        gm = _mm(lhs, rhs, NT_DIMS, 3)
        ch["mk"] = jnp.where(ch["strict"], gm[0:2 * C, 0:2 * C], 0.0)
        ch["mb"] = jnp.where(ch["strict"], gm[0:2 * C, 2 * C:4 * C], 0.0)
        ch["nk"] = jnp.where(ch["incl"], gm[2 * C:4 * C, 0:2 * C], 0.0)
        ch["nb"] = jnp.where(ch["incl"], gm[2 * C:4 * C, 2 * C:4 * C], 0.0)
        ch["tinv"] = eye - jnp.where((pi >> 1) == (pj >> 1), ch["mb"], 0.0)
    size = 2
    while size < C:
        sh = size.bit_length() - 1
        blk = jnp.logical_and((pi >> (sh + 1)) == (pj >> (sh + 1)), (pi >> sh) != (pj >> sh))
        for ch in chains:
            ch["tn"] = _mm(ch["tinv"], jnp.where(blk, ch["mb"], 0.0), NN, 3)
        for ch in chains:
            ch["tinv"] = ch["tinv"] - _mm(ch["tn"], ch["tinv"], NN, 3)
        size *= 2
    for ch in chains:
        vst = stack(ch["v"])
        ch["mkv"] = fold(_mm(ch["mk"], vst, NN, 3))
        ch["nkv"] = fold(_mm(ch["nk"], vst, NN, 3))
        ch["s"] = ch["s_ref"][ch["p"]]
        ch["qr"] = _mm(jnp.concatenate([ch["q1"], ch["r1"]], axis=0), ch["s"], NT_DIMS, 3)
    for ch in chains:
        ch["u"] = fold(_mm(ch["tinv"], stack(ch["mkv"] + ch["qr"][0:C]), NN, 3))
    for ch in chains:
        ch["y_ref"][:, ch["cols"]] = ch["qr"][C:2 * C] + ch["nkv"] - fold(_mm(ch["nb"], stack(ch["u"]), NN, 3))
        upd = _mm(jnp.concatenate([ch["v"], ch["u"]], axis=0),
                  jnp.concatenate([ch["k2"], -ch["b2"]], axis=0), TN, 3)
        ch["s_ref"][ch["p"]] = ch["s"] * ch["e_tot"] + jnp.where(same, upd, 0.0)


def _rwkv_scan_kernel(bf_ref, bb_ref, first_ref, last_ref, seq_ref,
                      rf_ref, vf_ref, kkf_ref, lwf_ref, ktf_ref, bfw_ref,
                      rb_ref, vb_ref, kkb_ref, lwb_ref, ktb_ref, bbw_ref, s0_ref,
                      yf_ref, yb_ref, sfin_ref, s_ref):
    step = pl.program_id(0)

    @pl.when(first_ref[step] == 1)
    def _():
        s_ref[...] = s0_ref[:, 0]

    _rwkv_chunks([
        (False, rf_ref[...], vf_ref[...], kkf_ref[...], lwf_ref[0], ktf_ref[0], bfw_ref[0], s_ref.at[0], yf_ref),
        (True, rb_ref[...], vb_ref[...], kkb_ref[...], lwb_ref[0], ktb_ref[0], bbw_ref[0], s_ref.at[1], yb_ref)])

    @pl.when(last_ref[step] == 1)
    def _():
        sfin_ref[:, 0] = s_ref[...]


def _rwkv_scan(r, v, kk, lw, kt, b, s0_bd):
    C = RW_CHUNK
    tabs = _seq_tables(C)
    fwd = lambda i, bf, bb, fi, la, sq: (bf[i], 0)
    bwd = lambda i, bf, bb, fi, la, sq: (bb[i], 0)
    fwd3 = lambda i, bf, bb, fi, la, sq: (0, bf[i], 0)
    bwd3 = lambda i, bf, bb, fi, la, sq: (1, bb[i], 0)
    st = pl.BlockSpec((2, 1, 4, LANES, LANES), lambda i, bf, bb, fi, la, sq: (0, sq[i], 0, 0, 0))
    one_f, one_b = pl.BlockSpec((C, HALF), fwd), pl.BlockSpec((C, HALF), bwd)
    two_f, two_b = pl.BlockSpec((1, C, HALF), fwd3), pl.BlockSpec((1, C, HALF), bwd3)
    return pl.pallas_call(
        _rwkv_scan_kernel,
        out_shape=[jax.ShapeDtypeStruct((T, HALF), F32), jax.ShapeDtypeStruct((T, HALF), F32),
                   jax.ShapeDtypeStruct((2, N_SEQ, 4, LANES, LANES), F32)],
        grid_spec=pltpu.PrefetchScalarGridSpec(
            num_scalar_prefetch=5, grid=(len(tabs[0]),),
            in_specs=[one_f, one_f, one_f, two_f, two_f, two_f,
                      one_b, one_b, one_b, two_b, two_b, two_b, st],
            out_specs=[one_f, one_b, st],
            scratch_shapes=[pltpu.VMEM((2, 4, LANES, LANES), F32)]),
        compiler_params=_params(("arbitrary",)),
        name="rwkv_scan",
    )(*tabs, r, v, kk, lw, kt, b, r, v, kk, lw, kt, b, s0_bd)


def _rope(x, cos, sin_signed, first16):
    w = x.shape[1]
    partner = jnp.where(first16, pltpu.roll(x, w - 16, 1), pltpu.roll(x, 16, 1))
    return x * cos + partner * sin_signed


def _odd_prep_kernel(zc_ref, zr_ref, cos_ref, sin_ref, qkg_ref, ones_ref, cq_ref, ck_ref, ckraw_ref, rqk_ref):
    ones_bd = ones_ref[...]
    cos = cos_ref[...]
    sin = sin_ref[...]
    lane = lax.broadcasted_iota(I32, (TM, HALF), 1)
    first16 = (lane & 31) < 16
    for idx, (o_ref, raw_ref) in enumerate(((cq_ref, None), (ck_ref, ckraw_ref))):
        x = zc_ref[:, idx * HALF:(idx + 1) * HALF]
        ms = _group_sum(x * x, ones_bd) * (1.0 / HEAD)
        xn = x * lax.rsqrt(ms + 1e-6) * qkg_ref[idx:idx + 1, :]
        if raw_ref is not None:
            raw_ref[...] = xn
        o_ref[...] = _rope(xn, cos, sin, first16)
    rqk = _rope(zr_ref[...], cos, sin, first16)
    rqk_ref[...] = jnp.where(lane < HALF // 2, rqk * (HEAD ** -0.5), rqk)


def _odd_prep(zc, zr, cos_tab, sin_tab, qkg_tiled, ones_bd):
    row = lambda i: (i, 0)
    tab = lambda i: (jnp.where(i < PROMPT_TILES, 0, 1 + (i - PROMPT_TILES) % TILES_PER_SAMPLE), 0)
    one = jax.ShapeDtypeStruct((T, HALF), F32)
    o1 = pl.BlockSpec((TM, HALF), row)
    return pl.pallas_call(
        _odd_prep_kernel,
        out_shape=[one, one, one, one], grid=(NT,),
        in_specs=[pl.BlockSpec((TM, 2 * HALF), row), pl.BlockSpec((TM, HALF), row),
                  pl.BlockSpec((TM, HALF), tab), pl.BlockSpec((TM, HALF), tab),
                  _full((2, HALF)), _full((HALF, HALF))],
        out_specs=[o1, o1, o1, o1],
        compiler_params=_params(("arbitrary",)),
        name="odd_prep",
    )(zc, zr, cos_tab, sin_tab, qkg_tiled, ones_bd)


def _attn_kernel(*refs, has_ctx, one_minus_li):
    if has_ctx:
        q_ref, k_ref, v_ref, kc_ref, vc_ref, lam_ref, sg_ref, _, o_ref = refs
    else:
        q_ref, k_ref, v_ref, lam_ref, sg_ref, _, o_ref = refs
    lam = lam_ref[...]
    lane = lax.broadcasted_iota(I32, (LANES, LANES), 1)
    m0 = lane < HEAD
    scale = HEAD ** -0.5
    for h in range(4):
        cols = slice(h * LANES, (h + 1) * LANES)
        qp = q_ref[:, cols]
        segs = [(k_ref[:, cols], v_ref[:, cols])]
        if has_ctx:
            segs.append((kc_ref[0, :, cols], vc_ref[0, :, cols]))
        outs = []
        for qm in (jnp.where(m0, qp, 0.0), jnp.where(m0, 0.0, qp)):
            qm16 = qm.astype(BF16)
            ss = [lax.dot_general(qm16, ks.astype(BF16), NT_DIMS, preferred_element_type=F32) * scale
                  for ks, _ in segs]
            mx = ss[0].max(axis=-1, keepdims=True)
            for s_ in ss[1:]:
                mx = jnp.maximum(mx, s_.max(axis=-1, keepdims=True))
            ps = [jnp.exp(s_ - mx) for s_ in ss]
            den = ps[0].sum(axis=-1, keepdims=True)
            for p_ in ps[1:]:
                den = den + p_.sum(axis=-1, keepdims=True)
            outs.append([p_ / den for p_ in ps])
        acc = None
        for si, (_, vs) in enumerate(segs):
            amap = outs[0][si] - lam * outs[1][si]
            t = jnp.dot(amap.astype(BF16), vs.astype(BF16), preferred_element_type=F32)
            acc = t if acc is None else acc + t
        nrm = acc * lax.rsqrt(jnp.mean(acc * acc, axis=-1, keepdims=True) + 1e-6) * sg_ref[...]
        o_ref[:, cols] = nrm * one_minus_li


def _attn(cq, ck, zc, row0, n_seq, seq_len, lam, subln_g, one_minus_li, prev, ctx_k=None, ctx_v=None):
    nq = seq_len // LANES
    qb0 = row0 // LANES
    sb0 = row0 // seq_len
    in_specs = [pl.BlockSpec((LANES, HALF), lambda s, q: (qb0 + s * nq + q, 0)),
                pl.BlockSpec((seq_len, HALF), lambda s, q: (sb0 + s, 0)),
                pl.BlockSpec((seq_len, HALF), lambda s, q: (sb0 + s, 2))]
    args = [cq, ck, zc]
    if ctx_k is not None:
        in_specs += [pl.BlockSpec((1, PAST, HALF), lambda s, q: (s, 0, 0))] * 2
        args += [ctx_k, ctx_v]
    in_specs += [_full((1, 1)), _full((1, LANES))]
    args += [lam.reshape(1, 1), subln_g.reshape(1, LANES)]
    in_specs.append(pl.BlockSpec(memory_space=pl.ANY))
    args.append(prev)
    aliases = {len(args) - 1: 0}
    return pl.pallas_call(
        functools.partial(_attn_kernel, has_ctx=ctx_k is not None, one_minus_li=one_minus_li),
        out_shape=jax.ShapeDtypeStruct((T, HALF), F32),
        grid=(n_seq, nq), in_specs=in_specs,
        out_specs=pl.BlockSpec((LANES, HALF), lambda s, q: (qb0 + s * nq + q, 0)),
        input_output_aliases=aliases,
        compiler_params=_params(("arbitrary", "arbitrary"), 48),
        name="diff_attn",
    )(*args)


_LOG_GAMMA = tuple(tuple(float(np.log1p(-np.exp2(-np.float32(e)), dtype=np.float32)) for e in es)
                   for es in RET_EXP)


def _ret_chunk(rev, qk_ref, v_ref, s_ref, o_ref):
    C = RET_CHUNK
    ii = lax.broadcasted_iota(I32, (C, C), 0)
    jj = lax.broadcasted_iota(I32, (C, C), 1)
    mask = (jj > ii) if rev else (jj <= ii)
    dist = jnp.where(mask, (jj - ii) if rev else (ii - jj), 0).astype(F32)
    ri = lax.broadcasted_iota(I32, (C, 1), 0)
    kpow = (ri if rev else (C - 1 - ri)).astype(F32)
    qpow = ((C - ri) if rev else (ri + 1)).astype(F32)
    lane = lax.broadcasted_iota(I32, (C, LANES), 1)
    for h in range(4):
        lg = _LOG_GAMMA[1 if rev else 0][h]
        p = h // 2
        hm = (lane < HEAD) if h % 2 == 0 else (lane >= HEAD)
        qp = jnp.where(hm, qk_ref[:, p * LANES:(p + 1) * LANES], 0.0)
        kp = jnp.where(hm, qk_ref[:, HALF // 2 + p * LANES:HALF // 2 + (p + 1) * LANES], 0.0)
        vh = v_ref[:, h * LANES:(h + 1) * LANES].astype(BF16)
        decay = jnp.where(mask, jnp.exp(lg * dist), 0.0)
        sc = lax.dot_general(qp.astype(BF16), kp.astype(BF16), NT_DIMS, preferred_element_type=F32) * decay
        o_intra = jnp.dot(sc.astype(BF16), vh, preferred_element_type=F32)
        s_prev = s_ref[h]
        o_cross = jnp.dot((qp * jnp.exp(lg * qpow)).astype(BF16), s_prev.astype(BF16),
                          preferred_element_type=F32)
        o_ref[:, h * LANES:(h + 1) * LANES] = o_intra + o_cross
        kv = lax.dot_general((kp * jnp.exp(lg * kpow)).astype(BF16), vh, TN, preferred_element_type=F32)
        s_ref[h] = math.exp(lg * C) * s_prev + kv


def _ret_kernel(bf_ref, bb_ref, first_ref, last_ref, seq_ref,
                qkf_ref, vf_ref, qkb_ref, vb_ref, s0_ref, of_ref, ob_ref, sfin_ref, s_ref):
    step = pl.program_id(0)

    @pl.when(first_ref[step] == 1)
    def _():
        s_ref[...] = s0_ref[:, 0]

    _ret_chunk(False, qkf_ref, vf_ref, s_ref.at[0], of_ref)
    _ret_chunk(True, qkb_ref, vb_ref, s_ref.at[1], ob_ref)

    @pl.when(last_ref[step] == 1)
    def _():
        sfin_ref[:, 0] = s_ref[...]


def _retention(rqk, zr, s0):
    C = RET_CHUNK
    tabs = _seq_tables(C)
    st = pl.BlockSpec((2, 1, 4, LANES, LANES), lambda i, bf, bb, fi, la, sq: (0, sq[i], 0, 0, 0))
    spec = lambda use_b, col: pl.BlockSpec(
        (C, HALF), lambda i, bf, bb, fi, la, sq: ((bb if use_b else bf)[i], col))
    return pl.pallas_call(
        _ret_kernel,
        out_shape=[jax.ShapeDtypeStruct((T, HALF), F32), jax.ShapeDtypeStruct((T, HALF), F32),
                   jax.ShapeDtypeStruct((2, N_SEQ, 4, LANES, LANES), F32)],
        grid_spec=pltpu.PrefetchScalarGridSpec(
            num_scalar_prefetch=5, grid=(len(tabs[0]),),
            in_specs=[spec(False, 0), spec(False, 1), spec(True, 0), spec(True, 1), st],
            out_specs=[spec(False, 0), spec(True, 0), st],
            scratch_shapes=[pltpu.VMEM((2, 4, LANES, LANES), F32)]),
        compiler_params=_params(("arbitrary",)),
        name="retention",
    )(*tabs, rqk, zr, rqk, zr, s0)


def _out_kernel(*refs, even):
    if even:
        (a_ref, yf_ref, yb_ref, bonus_ref, g_ref, gng_ref, gnb_ref, ones_ref,
         x_ref, mod_ref, ng_ref, wo_ref, rw_ref, rb_ref,
         y_ref, xp_ref, ti_ref, tg_ref, rk_ref, cnt_ref, run_ref) = refs
        ones_bd = ones_ref[...]
        ys = yf_ref[...] + yb_ref[...]
        mu = _group_sum(ys, ones_bd) * (1.0 / HEAD)
        dv = ys - mu
        var = _group_sum(dv * dv, ones_bd) * (1.0 / HEAD)
        yn = dv * lax.rsqrt(var + RWKV_GN_EPS) * gng_ref[...] + gnb_ref[...]
        left = a_ref[...]
        right = (yn + bonus_ref[...]) * g_ref[...]
    else:
        (c_ref, of_ref, ob_ref, rg_ref, gng_ref,
         x_ref, mod_ref, ng_ref, wo_ref, rw_ref, rb_ref,
         y_ref, xp_ref, ti_ref, tg_ref, rk_ref, cnt_ref, run_ref) = refs
        left = c_ref[...]
        rg = rg_ref[...]
        gate = rg * jax.nn.sigmoid(rg)
        os_ = of_ref[...] + ob_ref[...]
        parts = []
        for h in range(4):
            oh = os_[:, h * LANES:(h + 1) * LANES]
            mu = jnp.mean(oh, axis=-1, keepdims=True)
            dv = oh - mu
            var = jnp.mean(dv * dv, axis=-1, keepdims=True)
            parts.append(dv * lax.rsqrt(var + 1e-5))
        right = gate * (jnp.concatenate(parts, axis=1) * gng_ref[...])
    mod = mod_ref[0]
    o = (jnp.dot(left.astype(BF16), wo_ref[0:HALF, :], preferred_element_type=F32)
         + jnp.dot(right.astype(BF16), wo_ref[HALF:2 * HALF, :], preferred_element_type=F32))
    y = x_ref[...] + mod[:, 2 * D:3 * D] * o
    y_ref[...] = y
    yn2 = y * lax.rsqrt(jnp.mean(y * y, axis=-1, keepdims=True) + 1e-6) * ng_ref[...]
    t = yn2 * (1.0 + mod[:, 4 * D:5 * D]) + mod[:, 3 * D:4 * D]
    xp_ref[...] = t
    logits = jnp.dot(t, rw_ref[...], precision=HIGHEST, preferred_element_type=F32) + rb_ref[...]
    lane = lax.broadcasted_iota(I32, (TM, LANES), 1)
    neg = jnp.float32(-jnp.inf)
    lg = jnp.where(lane < N_EXPERTS, logits, neg)
    vals, hits = [], []
    for _ in range(TOP_K):
        m = jnp.max(lg, axis=-1, keepdims=True)
        ix = jnp.min(jnp.where(lg == m, lane, LANES), axis=-1, keepdims=True)
        hit = lane == ix
        vals.append(m)
        hits.append((ix, hit))
        lg = jnp.where(hit, neg, lg)
    es = [jnp.exp(vv - vals[0]) for vv in vals]
    den = es[0] + es[1] + es[2] + es[3]

    @pl.when(pl.program_id(0) == 0)
    def _():
        run_ref[...] = jnp.zeros_like(run_ref)

    member = jnp.zeros((TM, LANES), F32)
    for _, hit in hits:
        member = member + jnp.where(hit, 1.0, 0.0)
    ri = lax.broadcasted_iota(I32, (TM, TM), 0)
    ci = lax.broadcasted_iota(I32, (TM, TM), 1)
    before = jnp.where(ci < ri, 1.0, 0.0).astype(BF16)
    seen = run_ref[...] + jnp.dot(before, member.astype(BF16), preferred_element_type=F32)
    ti = jnp.zeros((TM, LANES), I32)
    tg = jnp.zeros((TM, LANES), F32)
    rk = jnp.zeros((TM, LANES), F32)
    for kk, (ix, hit) in enumerate(hits):
        ti = jnp.where(lane == kk, ix, ti)
        tg = jnp.where(lane == kk, es[kk] / den, tg)
        rk = jnp.where(lane == kk, jnp.sum(jnp.where(hit, seen, 0.0), axis=-1, keepdims=True), rk)
    ti_ref[...] = ti
    tg_ref[...] = tg
    rk_ref[...] = rk.astype(I32)
    run_ref[...] = run_ref[...] + jnp.sum(member, axis=0, keepdims=True)
    cnt_ref[...] = run_ref[...]


def _out_proj(even, mix_args, mix_specs, x, mod, norm_g, w_out_bf16, rw_pad, rb_pad):
    row = lambda i: (i, 0)
    modspec = pl.BlockSpec((1, 1, N_MOD * D), lambda i: (_group(i), 0, 0))
    in_specs = list(mix_specs) + [pl.BlockSpec((TM, D), row), modspec, _full((1, D)), _full((D, D)),
                                  _full((D, LANES)), _full((1, LANES))]
    args = list(mix_args) + [x, mod, norm_g.reshape(1, D), w_out_bf16, rw_pad, rb_pad]
    lane_i = jax.ShapeDtypeStruct((T, LANES), I32)
    lane_spec = pl.BlockSpec((TM, LANES), row)
    return pl.pallas_call(
        functools.partial(_out_kernel, even=even),
        out_shape=[jax.ShapeDtypeStruct((T, D), F32), jax.ShapeDtypeStruct((T, D), F32),
                   lane_i, jax.ShapeDtypeStruct((T, LANES), F32), lane_i,
                   jax.ShapeDtypeStruct((1, LANES), F32)],
        grid=(NT,), in_specs=in_specs,
        out_specs=[pl.BlockSpec((TM, D), row), pl.BlockSpec((TM, D), row),
                   lane_spec, lane_spec, lane_spec, _full((1, LANES))],
        scratch_shapes=[pltpu.VMEM((1, LANES), F32)],
        compiler_params=_params(("arbitrary",), 48),
        name="out_proj",
    )(*args)


def _route_kernel(cnt_ref, ti_ref, rk_ref, dest_ref, te_ref, nt_ref):
    cnt = cnt_ref[...].astype(I32)
    ntile = lax.shift_right_logical(cnt + (TM - 1), 8)
    ei = lax.broadcasted_iota(I32, (LANES, LANES), 0)
    ej = lax.broadcasted_iota(I32, (LANES, LANES), 1)
    upto = jnp.where(ei <= ej, 1.0, 0.0).astype(BF16)
    ntile_f = jnp.broadcast_to(ntile.astype(F32), (8, LANES))
    tile_end = jnp.dot(ntile_f.astype(BF16), upto, preferred_element_type=F32)[0:1, :]
    row_start = (tile_end - ntile.astype(F32)) * float(TM)
    lane = lax.broadcasted_iota(I32, (TM, LANES), 1)
    ti = ti_ref[...]
    rk = rk_ref[...]
    dest = jnp.zeros((TM, LANES), F32)
    for k in range(TOP_K):
        hit = lane == ti[:, k:k + 1]
        start = jnp.sum(jnp.where(hit, row_start, 0.0), axis=-1, keepdims=True)
        dest = jnp.where(lane == k, start + rk[:, k:k + 1].astype(F32), dest)
    dest_ref[...] = dest.astype(I32)

    @pl.when(pl.program_id(0) == 0)
    def _():
        lane1 = lax.broadcasted_iota(I32, (1, LANES), 1)
        n_tiles = jnp.max(tile_end, axis=-1, keepdims=True)
        last_e = jnp.max(jnp.where(cnt > 0, lane1, 0), axis=-1, keepdims=True)
        tile = lax.broadcasted_iota(I32, (TM, 1), 0).astype(F32)
        te = jnp.sum(jnp.where(tile_end <= tile, 1, 0), axis=-1, keepdims=True)
        te = jnp.where(tile < n_tiles, te, last_e)
        te_ref[...] = jnp.broadcast_to(te, (TM, LANES)).astype(I32)
        nt_ref[...] = jnp.broadcast_to(n_tiles, (8, LANES)).astype(I32)


def _route(cnt, ti, rk):
    row = lambda i: (i, 0)
    return pl.pallas_call(
        _route_kernel,
        out_shape=[jax.ShapeDtypeStruct((T, LANES), I32), jax.ShapeDtypeStruct((TM, LANES), I32),
                   jax.ShapeDtypeStruct((8, LANES), I32)],
        grid=(NT,),
        in_specs=[_full((1, LANES)), pl.BlockSpec((TM, LANES), row), pl.BlockSpec((TM, LANES), row)],
        out_specs=[pl.BlockSpec((TM, LANES), row), _full((TM, LANES)), _full((8, LANES))],
        compiler_params=_params(("arbitrary",)),
        name="moe_route",
    )(cnt, ti, rk)


def _row_copy(src_ref, src_row, dst_ref, dst_row, sem):
    return pltpu.make_async_copy(src_ref.at[pl.ds(src_row, 1)], dst_ref.at[pl.ds(dst_row, 1)], sem)


def _invert_kernel(dest_ref, owner_ref):
    def clear(r, carry):
        owner_ref[r] = -1
        return carry

    lax.fori_loop(0, R_PAD, clear, 0, unroll=8)

    def place(a, carry):
        owner_ref[dest_ref[a]] = a
        return carry

    lax.fori_loop(0, N_ASSIGN, place, 0, unroll=8)


def _invert(dest_flat):
    return pl.pallas_call(
        _invert_kernel,
        out_shape=jax.ShapeDtypeStruct((R_PAD,), I32),
        in_specs=[pl.BlockSpec(memory_space=pltpu.SMEM)],
        out_specs=pl.BlockSpec(memory_space=pltpu.SMEM),
        name="moe_invert",
    )(dest_flat)


def _wait_tile(src_ref, dst_ref, sem):
    pltpu.make_async_copy(src_ref, dst_ref, sem).wait()


def _expert_weights(i, nt, te_ref, w_ref, wbuf_ref, wsem, group_ref):
    first = jnp.logical_or(i == 0, te_ref[i] != te_ref[jnp.maximum(i - 1, 0)])

    @pl.when(i == 0)
    def _():
        group_ref[0] = 0
        pltpu.make_async_copy(w_ref.at[te_ref[0]], wbuf_ref.at[0], wsem.at[0]).start()

    @pl.when(jnp.logical_and(first, i < nt))
    def _():
        cur = group_ref[0] % 2
        nxt = lax.while_loop(
            lambda j: jnp.logical_and(j < nt, te_ref[jnp.minimum(j, MOE_TILES - 1)] == te_ref[i]),
            lambda j: j + 1, i + 1)

        @pl.when(nxt < nt)
        def _():
            pltpu.make_async_copy(w_ref.at[te_ref[jnp.minimum(nxt, MOE_TILES - 1)]], wbuf_ref.at[1 - cur],
                                  wsem.at[1 - cur]).start()

        pltpu.make_async_copy(w_ref.at[0], wbuf_ref.at[cur], wsem.at[cur]).wait()
        group_ref[0] = group_ref[0] + 1

    return (group_ref[0] + 1) % 2


def _moe_up_kernel(te_ref, nt_ref, owner_ref, x_ref, w_ref, b_ref, act_ref, xg_ref, sem, wbuf_ref, wsem, group_ref):
    i = pl.program_id(0)
    slot = i % 2
    wcur = _expert_weights(i, nt_ref[0], te_ref, w_ref, wbuf_ref, wsem, group_ref)
    whole = x_ref.at[pl.ds(0, TM)]

    def gather(tile, buf):
        base = tile * TM
        for r in range(TM):
            tok = lax.shift_right_logical(jnp.maximum(owner_ref[base + r], 0), 2)
            _row_copy(x_ref, tok, xg_ref.at[buf], r, sem.at[buf]).start()

    @pl.when(i == 0)
    def _():
        gather(0, 0)

    @pl.when(nt_ref[0] > 0)
    def _():
        _wait_tile(whole, xg_ref.at[slot], sem.at[slot])
        gather(jnp.minimum(i + 1, MOE_TILES - 1), 1 - slot)

    x16 = xg_ref[slot].astype(BF16)
    for gc in (slice(0, HALF), slice(HALF, D)):
        uc = slice(D + gc.start, D + gc.stop)
        g = jnp.dot(x16, wbuf_ref[wcur, :, gc].astype(BF16), preferred_element_type=F32) + b_ref[0, :, gc]
        u = jnp.dot(x16, wbuf_ref[wcur, :, uc].astype(BF16), preferred_element_type=F32) + b_ref[0, :, uc]
        gt = jnp.minimum(g, SWIGLU_LIMIT)
        up = jnp.clip(u, -SWIGLU_LIMIT, SWIGLU_LIMIT)
        act_ref[:, gc] = ((up + 1.0) * gt * jax.nn.sigmoid(SWIGLU_ALPHA * gt)).astype(BF16)

    @pl.when(i == MOE_TILES - 1)
    def _():
        _wait_tile(whole, xg_ref.at[1 - slot], sem.at[1 - slot])


def _moe_up(te, n_tiles, owner, x, w_gu, b_gu):
    return pl.pallas_call(
        _moe_up_kernel,
        out_shape=jax.ShapeDtypeStruct((R_PAD, D), BF16),
        grid_spec=pltpu.PrefetchScalarGridSpec(
            num_scalar_prefetch=3, grid=(MOE_TILES,),
            in_specs=[pl.BlockSpec(memory_space=pl.ANY), pl.BlockSpec(memory_space=pl.ANY),
                      pl.BlockSpec((1, 1, 2 * D), lambda i, te, nt, ow: (te[i], 0, 0))],
            out_specs=pl.BlockSpec((TM, D), lambda i, te, nt, ow: (i, 0)),
            scratch_shapes=[pltpu.VMEM((2, TM, D), F32), pltpu.SemaphoreType.DMA((2,)),
                            pltpu.VMEM((2, D, 2 * D), F32), pltpu.SemaphoreType.DMA((2,)),
                            pltpu.SMEM((1,), I32)]),
        compiler_params=_params(("arbitrary",), 56),
        name="moe_up",
    )(te, n_tiles, owner, x, w_gu, b_gu)


def _moe_down_kernel(te_ref, nt_ref, owner_ref, act_ref, w_ref, b_ref, ys_ref, y_ref, sem, wbuf_ref, wsem, group_ref):
    i = pl.program_id(0)
    wcur = _expert_weights(i, nt_ref[0], te_ref, w_ref, wbuf_ref, wsem, group_ref)

    def spare(buf):
        return ys_ref.at[pl.ds(TOP_K * T + buf * TM, TM)]

    def scatter(tile):
        base = tile * TM
        buf = tile % 3
        pad0 = TOP_K * T + buf * TM
        for r in range(TM):
            a = owner_ref[base + r]
            dst = jnp.where(a < 0, pad0 + r, jnp.bitwise_and(a, TOP_K - 1) * T + lax.shift_right_logical(a, 2))
            _row_copy(y_ref.at[buf], r, ys_ref, dst, sem.at[buf]).start()

    def wait(tile):
        _wait_tile(y_ref.at[tile % 3], spare(0), sem.at[tile % 3])

    @pl.when(i == 0)
    def _():
        y_ref[1] = jnp.zeros((TM, D), F32)
        for buf in range(3):
            pltpu.make_async_copy(y_ref.at[1], spare(buf), sem.at[1]).start()
        for buf in range(3):
            wait(1)

    @pl.when(i >= 1)
    def _():
        scatter(i - 1)

    y_ref[i % 3] = (jnp.dot(act_ref[...], wbuf_ref[wcur].astype(BF16), preferred_element_type=F32) + b_ref[0])

    @pl.when(i >= 2)
    def _():
        wait(i - 2)

    @pl.when(i == MOE_TILES - 1)
    def _():
        scatter(i)
        wait(i - 1)
        wait(i)


def _tile_clamped(i, te, nt, ow):
    return (jnp.minimum(i, jnp.maximum(nt[0] - 1, 0)), 0)


def _moe_down(te, n_tiles, owner, act, w_dn, b_dn):
    return pl.pallas_call(
        _moe_down_kernel,
        out_shape=jax.ShapeDtypeStruct((TOP_K * T + 3 * TM, D), F32),
        grid_spec=pltpu.PrefetchScalarGridSpec(
            num_scalar_prefetch=3, grid=(MOE_TILES,),
            in_specs=[pl.BlockSpec((TM, D), _tile_clamped), pl.BlockSpec(memory_space=pl.ANY),
                      pl.BlockSpec((1, 1, D), lambda i, te, nt, ow: (te[i], 0, 0))],
            out_specs=pl.BlockSpec(memory_space=pl.ANY),
            scratch_shapes=[pltpu.VMEM((3, TM, D), F32), pltpu.SemaphoreType.DMA((3,)),
                            pltpu.VMEM((2, D, D), F32), pltpu.SemaphoreType.DMA((2,)),
                            pltpu.SMEM((1,), I32)]),
        compiler_params=_params(("arbitrary",), 40),
        name="moe_down",
    )(te, n_tiles, owner, act, w_dn, b_dn)


def _combine_kernel(x_ref, tg_ref, mod_ref, y0_ref, y1_ref, y2_ref, y3_ref, o_ref):
    tg = tg_ref[...]
    f = tg[:, 0:1] * y0_ref[...]
    for k, y_ref in ((1, y1_ref), (2, y2_ref), (3, y3_ref)):
        f = f + tg[:, k:k + 1] * y_ref[...]
    o_ref[...] = x_ref[...] + mod_ref[0][:, 5 * D:6 * D] * f


def _combine(x, tg, mod, ys):
    row = lambda i: (i, 0)
    slot_spec = lambda k: pl.BlockSpec((TM, D), lambda i: (k * NT + i, 0))
    return pl.pallas_call(
        _combine_kernel,
        out_shape=jax.ShapeDtypeStruct((T, D), F32),
        grid=(NT,),
        in_specs=[pl.BlockSpec((TM, D), row), pl.BlockSpec((TM, LANES), row),
                  pl.BlockSpec((1, 1, N_MOD * D), lambda i: (_group(i), 0, 0)),
                  slot_spec(0), slot_spec(1), slot_spec(2), slot_spec(3)],
        out_specs=pl.BlockSpec((TM, D), row),
        compiler_params=_params(("arbitrary",)),
        name="moe_combine",
    )(x, tg, mod, ys, ys, ys, ys)


def _moe(layer, y, xt, ti, tg, rk, cnt, mod, w_gu, b_gu, w_dn, b_dn):
    dest, te, nt = _route(cnt, ti, rk)
    owner = _invert(dest[:, :TOP_K].reshape(-1))
    te = te[:MOE_TILES, 0] + layer * N_EXPERTS
    n_tiles = nt[0, :1]
    n_all = w_gu.shape[0] * N_EXPERTS
    act = _moe_up(te, n_tiles, owner, xt, w_gu.reshape(n_all, D, 2 * D), b_gu.reshape(n_all, 1, 2 * D))
    ys = _moe_down(te, n_tiles, owner, act, w_dn.reshape(n_all, D, D), b_dn.reshape(n_all, 1, D))
    return _combine(y, tg, mod, ys)


def _ones_blockdiag():
    idx = np.arange(HALF) // HEAD
    return jnp.asarray((idx[:, None] == idx[None, :]).astype(np.float32), dtype=BF16)


def _rope_tables():
    pos = jnp.arange(L_SAMPLE)
    rowp = (pos // 64).astype(F32)
    colp = (pos % 64).astype(F32)
    nf = HEAD // 4
    inv = jnp.power(10000.0, -jnp.arange(nf, dtype=F32) / nf)
    ar = rowp[:, None] * inv[None, :]
    ac = colp[:, None] * inv[None, :]
    cos64 = jnp.concatenate([jnp.cos(ar), jnp.cos(ar), jnp.cos(ac), jnp.cos(ac)], axis=1)
    sin64 = jnp.concatenate([-jnp.sin(ar), jnp.sin(ar), -jnp.sin(ac), jnp.sin(ac)], axis=1)
    cos = jnp.tile(cos64, (1, HALF // HEAD))
    sin = jnp.tile(sin64, (1, HALF // HEAD))
    ident = jnp.ones((TM, HALF), F32)
    return (jnp.concatenate([ident, cos], axis=0), jnp.concatenate([jnp.zeros((TM, HALF), F32), sin], axis=0))


def _bd_pairs(s):
    lead = s.shape[:-3]
    s = s.reshape(lead + (4, 2, HEAD, HEAD))
    z = jnp.zeros_like(s[..., 0, :, :])
    top = jnp.concatenate([s[..., 0, :, :], z], axis=-1)
    bot = jnp.concatenate([z, s[..., 1, :, :]], axis=-1)
    return jnp.concatenate([top, bot], axis=-2)


def _bd_unpairs(s):
    a = s[..., 0:HEAD, 0:HEAD]
    b = s[..., HEAD:, HEAD:]
    out = jnp.stack([a, b], axis=-3)
    return out.reshape(s.shape[:-3] + (8, HEAD, HEAD))


def kernel(x_prompt, x_sample, state_rwkv, cache_k_diff, cache_v_diff, state_retention, c, c_ctx, norm_g, ada_w, ada_b, e_w_in, e_w_out, sgu_ln_g, sgu_w, sgu_b, rw_mu, rw_w0, rw_w_up, rw_a0, rw_a_up, rw_g_up, rw_k_k, rw_k_a, rw_r_k, rw_gn_g, rw_gn_b, o_w_in, o_w_out, da_qk_g, da_lam, da_subln_g, ret_gn_g, router_w, router_b, ex_w_gu, ex_b_gu, ex_w_dn, ex_b_dn):
    x = jnp.concatenate([x_prompt.reshape(T_PROMPT, D), x_sample.reshape(T_SAMPLE, D)], axis=0)
    cvec8 = jnp.concatenate([c_ctx[None, :], c, jnp.zeros((3, D), F32)], axis=0)
    mods = _adaln(cvec8, ada_w, ada_b)
    mod0 = mods[0].reshape(8, 1, N_MOD * D)
    mod1 = mods[1].reshape(8, 1, N_MOD * D)
    ones_bd = _ones_blockdiag()
    rw_pad = jnp.pad(router_w, ((0, 0), (0, 0), (0, LANES - N_EXPERTS)))
    rb_pad = jnp.pad(router_b, ((0, 0), (0, LANES - N_EXPERTS))).reshape(2, 1, LANES)
    row = lambda i: (i, 0)
    half = pl.BlockSpec((TM, HALF), row)

    za, zb = _in_proj(x, norm_g[0, 0], mod0, e_w_in[0].astype(BF16), (2 * HALF, B_COLS))
    bs_full = jnp.repeat(sgu_b[0].T, HEAD, axis=1)
    a_out = _sgu(za, sgu_ln_g[0], sgu_w[0].astype(BF16), bs_full)
    zpad = jnp.zeros((2, HEAD, HALF), F32)
    wup_pad = jnp.concatenate([rw_w_up[0], zpad], axis=1).astype(BF16)
    aup_pad = jnp.concatenate([zpad, rw_a_up[0]], axis=1).astype(BF16)
    r, v, kkn, bonus, g, lw, kt, b = _rwkv_prep(zb, rw_mu[0], rw_k_k[0], rw_k_a[0], rw_r_k[0], rw_w0[0], rw_a0[0],
                                                wup_pad, aup_pad, rw_g_up[0].astype(BF16), ones_bd)
    s0_sample = _bd_pairs(jnp.moveaxis(state_rwkv[:, 0], 1, 0))
    s0_rw = jnp.concatenate([jnp.zeros((2, N_PROMPT, 4, LANES, LANES), F32), s0_sample], axis=1)
    yf_rw, yb_rw, sfin_rw = _rwkv_scan(r, v, kkn, lw, kt, b, s0_rw)
    new_rwkv = jnp.moveaxis(_bd_unpairs(sfin_rw[:, :N_PROMPT]), 0, 1)[:, None]
    y0, xp0, ti0, tg0, rk0, cnt0 = _out_proj(
        True,
        [a_out, yf_rw, yb_rw, bonus, g, rw_gn_g[0].reshape(1, HALF), rw_gn_b[0].reshape(1, HALF), ones_bd],
        [half, half, half, half, half, _full((1, HALF)), _full((1, HALF)), _full((HALF, HALF))],
        x, mod0, norm_g[0, 1], e_w_out[0].astype(BF16), rw_pad[0], rb_pad[0])
    x1 = _moe(0, y0, xp0, ti0, tg0, rk0, cnt0, mod0, ex_w_gu, ex_b_gu, ex_w_dn, ex_b_dn)

    zc, zr = _in_proj(x1, norm_g[1, 0], mod1, o_w_in[0].astype(BF16), (3 * HALF, 3 * HALF))
    cos_tab, sin_tab = _rope_tables()
    qkg = jnp.tile(da_qk_g[0], (1, HALF // HEAD))
    cq, ck, ck_raw, rqk = _odd_prep(zc, zr, cos_tab, sin_tab, qkg, ones_bd)
    lambda_init = 0.8 - 0.6 * math.exp(-0.3 * 1)
    lv = da_lam[0]
    lam = jnp.exp(jnp.sum(lv[0] * lv[1])) - jnp.exp(jnp.sum(lv[2] * lv[3])) + lambda_init
    c_out = _attn(cq, ck, zc, 0, N_PROMPT, L_PROMPT, lam, da_subln_g[0], 1.0 - lambda_init,
                  jnp.zeros((T, HALF), F32))
    ctx_k = cache_k_diff[:, 0].reshape(N_SAMPLE, PAST, HALF)
    ctx_v = cache_v_diff[:, 0].reshape(N_SAMPLE, PAST, HALF)
    c_out = _attn(cq, ck, zc, T_PROMPT, N_SAMPLE, L_SAMPLE, lam, da_subln_g[0], 1.0 - lambda_init,
                  c_out, ctx_k, ctx_v)
    sr = jnp.moveaxis(state_retention[:, 0], 1, 0)
    zr0 = jnp.zeros_like(sr)
    s0_sample = jnp.stack([jnp.concatenate([sr[:, :, 0], zr0[:, :, 0]], axis=-2),
                           jnp.concatenate([zr0[:, :, 1], sr[:, :, 1]], axis=-2),
                           jnp.concatenate([sr[:, :, 2], zr0[:, :, 2]], axis=-2),
                           jnp.concatenate([zr0[:, :, 3], sr[:, :, 3]], axis=-2)], axis=2)
    s0_ret = jnp.concatenate([jnp.zeros((2, N_PROMPT, 4, LANES, LANES), F32), s0_sample], axis=1)
    of_ret, ob_ret, rfin = _retention(rqk, zr, s0_ret)
    rfin_p = rfin[:, :N_PROMPT]
    new_ret = jnp.stack([rfin_p[:, :, 0, 0:HEAD], rfin_p[:, :, 1, HEAD:], rfin_p[:, :, 2, 0:HEAD],
                         rfin_p[:, :, 3, HEAD:]], axis=2)
    new_ret = jnp.moveaxis(new_ret, 0, 1)[:, None]
    y1, xp1, ti1, tg1, rk1, cnt1 = _out_proj(
        False,
        [c_out, of_ret, ob_ret, zr, ret_gn_g[0].reshape(1, HALF)],
        [half, half, half, pl.BlockSpec((TM, HALF), lambda i: (i, 2)), _full((1, HALF))],
        x1, mod1, norm_g[1, 1], o_w_out[0].astype(BF16), rw_pad[1], rb_pad[1])
    y_fin = _moe(1, y1, xp1, ti1, tg1, rk1, cnt1, mod1, ex_w_gu, ex_b_gu, ex_w_dn, ex_b_dn)

    new_k = ck_raw[:T_PROMPT].reshape(N_PROMPT, 1, L_PROMPT, 4, LANES)
    new_v = zc[:T_PROMPT, 2 * HALF:3 * HALF].reshape(N_PROMPT, 1, L_PROMPT, 4, LANES)
    return (y_fin[:T_PROMPT].reshape(N_PROMPT, L_PROMPT, D), y_fin[T_PROMPT:].reshape(N_SAMPLE, L_SAMPLE, D),
            new_rwkv, new_k, new_v, new_ret)
```

```python
import functools
import math

import numpy as np
import jax
import jax.numpy as jnp
from jax import lax
from jax.experimental import pallas as pl
from jax.experimental.pallas import tpu as pltpu

F32 = jnp.float32
BF16 = jnp.bfloat16
I32 = jnp.int32
HIGHEST = lax.Precision.HIGHEST

D = 1024
N_PROMPT, L_PROMPT = 16, 256
N_SAMPLE, L_SAMPLE = 4, 1024
N_SEQ = N_PROMPT + N_SAMPLE
PAST = 256
T_PROMPT = N_PROMPT * L_PROMPT
T_SAMPLE = N_SAMPLE * L_SAMPLE
T = T_PROMPT + T_SAMPLE
TM = 256
NT = T // TM
PROMPT_TILES = T_PROMPT // TM
TILES_PER_SAMPLE = L_SAMPLE // TM
N_MOD = 6
HALF = 512
B_COLS = 1792
HEAD = 64
W_DECAY_SCALE = math.exp(-0.5)
RWKV_GN_EPS = 64e-5
RW_CHUNK = 64
RET_CHUNK = 128
RET_EXP = ((5.0, 7.0, 9.0, 11.0), (6.0, 8.0, 10.0, 12.0))
N_EXPERTS = 32
TOP_K = 4
SWIGLU_LIMIT = 7.0
SWIGLU_ALPHA = 1.702
N_ASSIGN = T * TOP_K
MOE_TILES = N_ASSIGN // TM + N_EXPERTS
R_PAD = MOE_TILES * TM
LANES = 128

NN = (((1,), (0,)), ((), ()))
NT_DIMS = (((1,), (1,)), ((), ()))
TN = (((0,), (0,)), ((), ()))


def _group(i):
    return jnp.where(i < PROMPT_TILES, 0, 1 + (i - PROMPT_TILES) // TILES_PER_SAMPLE)


def _mm(a, b, dims=NN, passes=1):
    dg = functools.partial(lax.dot_general, dimension_numbers=dims, preferred_element_type=F32)
    if passes == 1:
        return dg(a.astype(BF16), b.astype(BF16))
    a = a.astype(F32)
    b = b.astype(F32)
    ah = a.astype(BF16)
    al = (a - ah.astype(F32)).astype(BF16)
    bh = b.astype(BF16)
    bl = (b - bh.astype(F32)).astype(BF16)
    return dg(ah, bh) + (dg(ah, bl) + dg(al, bh))


def _group_sum(x, ones_bd):
    xh = x.astype(BF16)
    xl = (x - xh.astype(F32)).astype(BF16)
    return (jnp.dot(xh, ones_bd, preferred_element_type=F32)
            + jnp.dot(xl, ones_bd, preferred_element_type=F32))


def _full(shape):
    nd = len(shape)
    return pl.BlockSpec(shape, lambda *_: (0,) * nd)


def _params(sem, vmem_mb=None):
    kw = {}
    if vmem_mb is not None:
        kw["vmem_limit_bytes"] = vmem_mb * 1024 * 1024
    return pltpu.CompilerParams(dimension_semantics=sem, **kw)


def _seq_tables(chunk):
    blk_f, blk_b, first, last, seq = [], [], [], [], []
    row = 0
    for s in range(N_SEQ):
        n = (L_PROMPT if s < N_PROMPT else L_SAMPLE) // chunk
        base = row // chunk
        for j in range(n):
            blk_f.append(base + j)
            blk_b.append(base + n - 1 - j)
            first.append(int(j == 0))
            last.append(int(j == n - 1))
            seq.append(s)
        row += n * chunk
    return tuple(np.asarray(a, np.int32) for a in (blk_f, blk_b, first, last, seq))


def _adaln_kernel(c_ref, w_ref, b_ref, o_ref):
    c = c_ref[...]
    s = c * jax.nn.sigmoid(c)
    o_ref[0] = jnp.dot(s, w_ref[0], precision=HIGHEST, preferred_element_type=F32) + b_ref[0]


def _adaln(cvec8, ada_w, ada_b):
    depth, _, n = ada_w.shape
    bn = 1536
    return pl.pallas_call(
        _adaln_kernel,
        out_shape=jax.ShapeDtypeStruct((depth, 8, n), F32),
        grid=(depth, n // bn),
        in_specs=[pl.BlockSpec((8, D), lambda l, j: (0, 0)),
                  pl.BlockSpec((1, D, bn), lambda l, j: (l, 0, j)),
                  pl.BlockSpec((1, 1, bn), lambda l, j: (l, 0, j))],
        out_specs=pl.BlockSpec((1, 8, bn), lambda l, j: (l, 0, j)),
        compiler_params=_params(("arbitrary", "arbitrary"), 40),
        name="adaln",
    )(cvec8, ada_w, ada_b.reshape(depth, 1, n))


def _in_kernel(x_ref, g_ref, mod_ref, w_ref, *outs, splits):
    x = x_ref[...]
    mod = mod_ref[0]
    y = x * lax.rsqrt(jnp.mean(x * x, axis=-1, keepdims=True) + 1e-6) * g_ref[...]
    h = (y * (1.0 + mod[:, D:2 * D]) + mod[:, 0:D]).astype(BF16)
    off = 0
    for o_ref, n in zip(outs, splits):
        o_ref[...] = jnp.dot(h, w_ref[:, off:off + n], preferred_element_type=F32)
        off += n


def _in_proj(x, g, mod, w_bf16, splits):
    n = w_bf16.shape[1]
    row = lambda i: (i, 0)
    return pl.pallas_call(
        functools.partial(_in_kernel, splits=splits),
        out_shape=[jax.ShapeDtypeStruct((T, s), F32) for s in splits],
        grid=(NT,),
        in_specs=[pl.BlockSpec((TM, D), row), _full((1, D)),
                  pl.BlockSpec((1, 1, N_MOD * D), lambda i: (_group(i), 0, 0)), _full((D, n))],
        out_specs=[pl.BlockSpec((TM, s), row) for s in splits],
        compiler_params=_params(("arbitrary",), 48),
        name="in_proj",
    )(x, g.reshape(1, D), mod, w_bf16)


def _gelu(x):
    return 0.5 * x * (1.0 + lax.erf(x * (1.0 / math.sqrt(2.0))))


def _sgu_kernel(za_ref, lng_ref, ws_ref, bs_ref, o_ref):
    u = _gelu(za_ref[:, 0:HALF])
    va = _gelu(za_ref[:, HALF:2 * HALF])
    mu = jnp.mean(va, axis=-1, keepdims=True)
    dv = va - mu
    var = jnp.mean(dv * dv, axis=-1, keepdims=True)
    vn = dv * lax.rsqrt(var + 1e-5) * lng_ref[...]
    lane = lax.broadcasted_iota(I32, (LANES, LANES), 1)
    first = lane < HEAD
    for c in range(TM // LANES):
        rows = slice(c * LANES, (c + 1) * LANES)
        for p in range(HALF // LANES):
            cols = slice(p * LANES, (p + 1) * LANES)
            vp = vn[rows, cols]
            s = (jnp.dot(ws_ref[2 * p], jnp.where(first, vp, 0.0).astype(BF16), preferred_element_type=F32)
                 + jnp.dot(ws_ref[2 * p + 1], jnp.where(first, 0.0, vp).astype(BF16), preferred_element_type=F32))
            o_ref[rows, cols] = u[rows, cols] * (s + bs_ref[:, cols])


def _sgu(za, ln_g, w_s_bf16, bs_full):
    return pl.pallas_call(
        _sgu_kernel,
        out_shape=jax.ShapeDtypeStruct((T, HALF), F32),
        grid=(NT,),
        in_specs=[pl.BlockSpec((TM, 2 * HALF), lambda i: (i, 0)), _full((1, HALF)),
                  _full((8, LANES, LANES)), _full((LANES, HALF))],
        out_specs=pl.BlockSpec((TM, HALF), lambda i: (i, 0)),
        compiler_params=_params(("arbitrary",)),
        name="sgu",
    )(za, ln_g.reshape(1, HALF), w_s_bf16, bs_full)


def _rwkv_prep_kernel(zb_ref, zp_ref, zn_ref, mu_ref, kk_ref, ka_ref, rk_ref, w0_ref, a0_ref,
                      wup_ref, aup_ref, gup_ref, ones_ref,
                      r_ref, v_ref, kkn_ref, bonus_ref, g_ref, lw_ref, kt_ref, b_ref):
    i = pl.program_id(0)
    in_sample = i >= PROMPT_TILES
    pos = (i - PROMPT_TILES) % TILES_PER_SAMPLE
    is_first = jnp.logical_or(jnp.logical_not(in_sample), pos == 0)
    is_last = jnp.logical_or(jnp.logical_not(in_sample), pos == TILES_PER_SAMPLE - 1)
    zb = zb_ref[...]
    prev_row = jnp.where(is_first, 0.0, zp_ref[7:8, :])
    next_row = jnp.where(is_last, 0.0, zn_ref[0:1, :])
    rowid = lax.broadcasted_iota(I32, (TM, 1), 0)
    zp = jnp.where(rowid == 0, prev_row, pltpu.roll(zb, 1, 0))
    zn = jnp.where(rowid == TM - 1, next_row, pltpu.roll(zb, TM - 1, 0))
    zs = zb + mu_ref[0:1, :] * (zp - zb) + mu_ref[1:2, :] * (zn - zb)
    r = zs[:, 0:HALF]
    k = zs[:, HALF:2 * HALF]
    v = zs[:, 2 * HALF:3 * HALF]
    wa = zs[:, 3 * HALF:3 * HALF + LANES]
    gd = zs[:, 3 * HALF + LANES:B_COLS]
    ones_bd = ones_ref[...]
    r_ref[...] = r
    v_ref[...] = v
    g_ref[...] = jnp.dot(jax.nn.sigmoid(gd).astype(BF16), gup_ref[...], preferred_element_type=F32)
    kk = k * kk_ref[...]
    kkn = kk / jnp.maximum(jnp.sqrt(_group_sum(kk * kk, ones_bd)), 1e-6)
    kkn_ref[...] = kkn
    bonus_ref[...] = _group_sum(r * k * rk_ref[...], ones_bd) * v
    tw = jnp.tanh(wa).astype(BF16)
    wa16 = wa.astype(BF16)
    for dd in range(2):
        lw_ref[dd] = -W_DECAY_SCALE * jax.nn.sigmoid(
            w0_ref[dd:dd + 1, :] + jnp.dot(tw, wup_ref[dd], preferred_element_type=F32))
        a = jax.nn.sigmoid(a0_ref[dd:dd + 1, :] + jnp.dot(wa16, aup_ref[dd], preferred_element_type=F32))
        kt_ref[dd] = k * (1.0 + (a - 1.0) * ka_ref[...])
        b_ref[dd] = a * kkn


def _rwkv_prep(zb, mu, k_k, k_a, r_k, w0, a0, wup_pad, aup_pad, g_up, ones_bd):
    row = lambda i: (i, 0)
    halo = TM // 8
    one = jax.ShapeDtypeStruct((T, HALF), F32)
    two = jax.ShapeDtypeStruct((2, T, HALF), F32)
    o1 = pl.BlockSpec((TM, HALF), row)
    o2 = pl.BlockSpec((2, TM, HALF), lambda i: (0, i, 0))
    return pl.pallas_call(
        _rwkv_prep_kernel,
        out_shape=[one, one, one, one, one, two, two, two],
        grid=(NT,),
        in_specs=[pl.BlockSpec((TM, B_COLS), row),
                  pl.BlockSpec((8, B_COLS), lambda i: (jnp.maximum(i * halo - 1, 0), 0)),
                  pl.BlockSpec((8, B_COLS), lambda i: (jnp.minimum((i + 1) * halo, T // 8 - 1), 0)),
                  _full((2, B_COLS)), _full((1, HALF)), _full((1, HALF)), _full((1, HALF)),
                  _full((2, HALF)), _full((2, HALF)),
                  _full((2, LANES, HALF)), _full((2, LANES, HALF)), _full((LANES, HALF)),
                  _full((HALF, HALF))],
        out_specs=[o1, o1, o1, o1, o1, o2, o2, o2],
        compiler_params=_params(("arbitrary",), 48),
        name="rwkv_prep",
    )(zb, zb, zb, mu, k_k.reshape(1, HALF), k_a.reshape(1, HALF), r_k.reshape(1, HALF), w0, a0,
      wup_pad, aup_pad, g_up, ones_bd)


def _rwkv_chunks(dirs):
    C = RW_CHUNK
    ti = lax.broadcasted_iota(I32, (C, C), 0)
    tj = lax.broadcasted_iota(I32, (C, C), 1)
    bi = lax.broadcasted_iota(I32, (LANES, LANES), 0)
    bj = lax.broadcasted_iota(I32, (LANES, LANES), 1)
    same = (bi >> 6) == (bj >> 6)
    pi = bi & (C - 1)
    pj = bj & (C - 1)
    eye = (bi == bj).astype(F32)
    h0 = lax.broadcasted_iota(I32, (C, LANES), 1) < HEAD

    def stack(x):
        return jnp.concatenate([jnp.where(h0, x, 0.0), jnp.where(h0, 0.0, x)], axis=0)

    def fold(x):
        return x[0:C] + x[C:2 * C]

    chains = []
    for rev, r, v, kk, lw, kt, b, s_ref, y_ref in dirs:
        tri = jnp.where((tj >= ti) if rev else (tj <= ti), 1.0, 0.0).astype(F32)
        cs = jnp.dot(tri, lw, precision=HIGHEST, preferred_element_type=F32)
        ctot = cs[0:1, :] if rev else cs[C - 1:C, :]
        e_neg = jnp.exp(-cs)
        e_tail = jnp.exp(ctot - cs)
        q1 = kk * jnp.exp(cs - lw)
        k1 = kt * e_neg
        b1 = b * e_neg
        r1 = r * jnp.exp(cs)
        k2 = kt * e_tail
        b2 = b * e_tail
        e_tot = jnp.exp(ctot)
        strict = jnp.logical_and(same, (pj > pi) if rev else (pj < pi))
        incl = jnp.logical_and(same, (pj >= pi) if rev else (pj <= pi))
        for p in range(HALF // LANES):
            cols = slice(p * LANES, (p + 1) * LANES)
            chains.append(dict(p=p, cols=cols, strict=strict, incl=incl, s_ref=s_ref, y_ref=y_ref,
                               q1=q1[:, cols], k1=k1[:, cols], b1=b1[:, cols], r1=r1[:, cols],
                               k2=k2[:, cols], b2=b2[:, cols], v=v[:, cols], e_tot=e_tot[:, cols]))

    for ch in chains:
        lhs = jnp.concatenate([stack(ch["q1"]), stack(ch["r1"])], axis=0)
        rhs = jnp.concatenate([ch["k1"], ch["k1"], ch["b1"], ch["b1"]], axis=0)
        gm = _mm(lhs, rhs, NT_DIMS, 3)
        ch["mk"] = jnp.where(ch["strict"], gm[0:2 * C, 0:2 * C], 0.0)
        ch["mb"] = jnp.where(ch["strict"], gm[0:2 * C, 2 * C:4 * C], 0.0)
        ch["nk"] = jnp.where(ch["incl"], gm[2 * C:4 * C, 0:2 * C], 0.0)
        ch["nb"] = jnp.where(ch["incl"], gm[2 * C:4 * C, 2 * C:4 * C], 0.0)
        ch["tinv"] = eye - jnp.where((pi >> 1) == (pj >> 1), ch["mb"], 0.0)
    size = 2
    while size < C:
        sh = size.bit_length() - 1
        blk = jnp.logical_and((pi >> (sh + 1)) == (pj >> (sh + 1)), (pi >> sh) != (pj >> sh))
        for ch in chains:
            ch["tn"] = _mm(ch["tinv"], jnp.where(blk, ch["mb"], 0.0), NN, 3)
        for ch in chains:
            ch["tinv"] = ch["tinv"] - _mm(ch["tn"], ch["tinv"], NN, 3)
        size *= 2
    for ch in chains:
        vst = stack(ch["v"])
        ch["mkv"] = fold(_mm(ch["mk"], vst, NN, 3))
        ch["nkv"] = fold(_mm(ch["nk"], vst, NN, 3))
        ch["s"] = ch["s_ref"][ch["p"]]
        ch["qr"] = _mm(jnp.concatenate([ch["q1"], ch["r1"]], axis=0), ch["s"], NT_DIMS, 3)
    for ch in chains:
        ch["u"] = fold(_mm(ch["tinv"], stack(ch["mkv"] + ch["qr"][0:C]), NN, 3))
    for ch in chains:
        ch["y_ref"][:, ch["cols"]] = ch["qr"][C:2 * C] + ch["nkv"] - fold(_mm(ch["nb"], stack(ch["u"]), NN, 3))
        upd = _mm(jnp.concatenate([ch["v"], ch["u"]], axis=0),
                  jnp.concatenate([ch["k2"], -ch["b2"]], axis=0), TN, 3)
        ch["s_ref"][ch["p"]] = ch["s"] * ch["e_tot"] + jnp.where(same, upd, 0.0)


def _rwkv_scan_kernel(bf_ref, bb_ref, first_ref, last_ref, seq_ref,
                      rf_ref, vf_ref, kkf_ref, lwf_ref, ktf_ref, bfw_ref,
                      rb_ref, vb_ref, kkb_ref, lwb_ref, ktb_ref, bbw_ref, s0_ref,
                      yf_ref, yb_ref, sfin_ref, s_ref):
    step = pl.program_id(0)

    @pl.when(first_ref[step] == 1)
    def _():
        s_ref[...] = s0_ref[:, 0]

    _rwkv_chunks([
        (False, rf_ref[...], vf_ref[...], kkf_ref[...], lwf_ref[0], ktf_ref[0], bfw_ref[0], s_ref.at[0], yf_ref),
        (True, rb_ref[...], vb_ref[...], kkb_ref[...], lwb_ref[0], ktb_ref[0], bbw_ref[0], s_ref.at[1], yb_ref)])

    @pl.when(last_ref[step] == 1)
    def _():
        sfin_ref[:, 0] = s_ref[...]


def _rwkv_scan(r, v, kk, lw, kt, b, s0_bd):
    C = RW_CHUNK
    tabs = _seq_tables(C)
    fwd = lambda i, bf, bb, fi, la, sq: (bf[i], 0)
    bwd = lambda i, bf, bb, fi, la, sq: (bb[i], 0)
    fwd3 = lambda i, bf, bb, fi, la, sq: (0, bf[i], 0)
    bwd3 = lambda i, bf, bb, fi, la, sq: (1, bb[i], 0)
    st = pl.BlockSpec((2, 1, 4, LANES, LANES), lambda i, bf, bb, fi, la, sq: (0, sq[i], 0, 0, 0))
    one_f, one_b = pl.BlockSpec((C, HALF), fwd), pl.BlockSpec((C, HALF), bwd)
    two_f, two_b = pl.BlockSpec((1, C, HALF), fwd3), pl.BlockSpec((1, C, HALF), bwd3)
    return pl.pallas_call(
        _rwkv_scan_kernel,
        out_shape=[jax.ShapeDtypeStruct((T, HALF), F32), jax.ShapeDtypeStruct((T, HALF), F32),
                   jax.ShapeDtypeStruct((2, N_SEQ, 4, LANES, LANES), F32)],
        grid_spec=pltpu.PrefetchScalarGridSpec(
            num_scalar_prefetch=5, grid=(len(tabs[0]),),
            in_specs=[one_f, one_f, one_f, two_f, two_f, two_f,
                      one_b, one_b, one_b, two_b, two_b, two_b, st],
            out_specs=[one_f, one_b, st],
            scratch_shapes=[pltpu.VMEM((2, 4, LANES, LANES), F32)]),
        compiler_params=_params(("arbitrary",)),
        name="rwkv_scan",
    )(*tabs, r, v, kk, lw, kt, b, r, v, kk, lw, kt, b, s0_bd)


def _rope(x, cos, sin_signed, first16):
    w = x.shape[1]
    partner = jnp.where(first16, pltpu.roll(x, w - 16, 1), pltpu.roll(x, 16, 1))
    return x * cos + partner * sin_signed


def _odd_prep_kernel(zc_ref, zr_ref, cos_ref, sin_ref, qkg_ref, ones_ref, cq_ref, ck_ref, ckraw_ref, rqk_ref):
    ones_bd = ones_ref[...]
    cos = cos_ref[...]
    sin = sin_ref[...]
    lane = lax.broadcasted_iota(I32, (TM, HALF), 1)
    first16 = (lane & 31) < 16
    for idx, (o_ref, raw_ref) in enumerate(((cq_ref, None), (ck_ref, ckraw_ref))):
        x = zc_ref[:, idx * HALF:(idx + 1) * HALF]
        ms = _group_sum(x * x, ones_bd) * (1.0 / HEAD)
        xn = x * lax.rsqrt(ms + 1e-6) * qkg_ref[idx:idx + 1, :]
        if raw_ref is not None:
            raw_ref[...] = xn
        o_ref[...] = _rope(xn, cos, sin, first16)
    rqk = _rope(zr_ref[...], cos, sin, first16)
    rqk_ref[...] = jnp.where(lane < HALF // 2, rqk * (HEAD ** -0.5), rqk)


def _odd_prep(zc, zr, cos_tab, sin_tab, qkg_tiled, ones_bd):
    row = lambda i: (i, 0)
    tab = lambda i: (jnp.where(i < PROMPT_TILES, 0, 1 + (i - PROMPT_TILES) % TILES_PER_SAMPLE), 0)
    one = jax.ShapeDtypeStruct((T, HALF), F32)
    o1 = pl.BlockSpec((TM, HALF), row)
    return pl.pallas_call(
        _odd_prep_kernel,
        out_shape=[one, one, one, one], grid=(NT,),
        in_specs=[pl.BlockSpec((TM, 2 * HALF), row), pl.BlockSpec((TM, HALF), row),
                  pl.BlockSpec((TM, HALF), tab), pl.BlockSpec((TM, HALF), tab),
                  _full((2, HALF)), _full((HALF, HALF))],
        out_specs=[o1, o1, o1, o1],
        compiler_params=_params(("arbitrary",)),
        name="odd_prep",
    )(zc, zr, cos_tab, sin_tab, qkg_tiled, ones_bd)


def _attn_kernel(*refs, has_ctx, one_minus_li):
    if has_ctx:
        q_ref, k_ref, v_ref, kc_ref, vc_ref, lam_ref, sg_ref, _, o_ref = refs
    else:
        q_ref, k_ref, v_ref, lam_ref, sg_ref, _, o_ref = refs
    lam = lam_ref[...]
    lane = lax.broadcasted_iota(I32, (LANES, LANES), 1)
    m0 = lane < HEAD
    scale = HEAD ** -0.5
    for h in range(4):
        cols = slice(h * LANES, (h + 1) * LANES)
        qp = q_ref[:, cols]
        segs = [(k_ref[:, cols], v_ref[:, cols])]
        if has_ctx:
            segs.append((kc_ref[0, :, cols], vc_ref[0, :, cols]))
        outs = []
        for qm in (jnp.where(m0, qp, 0.0), jnp.where(m0, 0.0, qp)):
            qm16 = qm.astype(BF16)
            ss = [lax.dot_general(qm16, ks.astype(BF16), NT_DIMS, preferred_element_type=F32) * scale
                  for ks, _ in segs]
            mx = ss[0].max(axis=-1, keepdims=True)
            for s_ in ss[1:]:
                mx = jnp.maximum(mx, s_.max(axis=-1, keepdims=True))
            ps = [jnp.exp(s_ - mx) for s_ in ss]
            den = ps[0].sum(axis=-1, keepdims=True)
            for p_ in ps[1:]:
                den = den + p_.sum(axis=-1, keepdims=True)
            outs.append([p_ / den for p_ in ps])
        acc = None
        for si, (_, vs) in enumerate(segs):
            amap = outs[0][si] - lam * outs[1][si]
            t = jnp.dot(amap.astype(BF16), vs.astype(BF16), preferred_element_type=F32)
            acc = t if acc is None else acc + t
        nrm = acc * lax.rsqrt(jnp.mean(acc * acc, axis=-1, keepdims=True) + 1e-6) * sg_ref[...]
        o_ref[:, cols] = nrm * one_minus_li


def _attn(cq, ck, zc, row0, n_seq, seq_len, lam, subln_g, one_minus_li, prev, ctx_k=None, ctx_v=None):
    nq = seq_len // LANES
    qb0 = row0 // LANES
    sb0 = row0 // seq_len
    in_specs = [pl.BlockSpec((LANES, HALF), lambda s, q: (qb0 + s * nq + q, 0)),
                pl.BlockSpec((seq_len, HALF), lambda s, q: (sb0 + s, 0)),
                pl.BlockSpec((seq_len, HALF), lambda s, q: (sb0 + s, 2))]
    args = [cq, ck, zc]
    if ctx_k is not None:
        in_specs += [pl.BlockSpec((1, PAST, HALF), lambda s, q: (s, 0, 0))] * 2
        args += [ctx_k, ctx_v]
    in_specs += [_full((1, 1)), _full((1, LANES))]
    args += [lam.reshape(1, 1), subln_g.reshape(1, LANES)]
    in_specs.append(pl.BlockSpec(memory_space=pl.ANY))
    args.append(prev)
    aliases = {len(args) - 1: 0}
    return pl.pallas_call(
        functools.partial(_attn_kernel, has_ctx=ctx_k is not None, one_minus_li=one_minus_li),
        out_shape=jax.ShapeDtypeStruct((T, HALF), F32),
        grid=(n_seq, nq), in_specs=in_specs,
        out_specs=pl.BlockSpec((LANES, HALF), lambda s, q: (qb0 + s * nq + q, 0)),
        input_output_aliases=aliases,
        compiler_params=_params(("arbitrary", "arbitrary"), 48),
        name="diff_attn",
    )(*args)


_LOG_GAMMA = tuple(tuple(float(np.log1p(-np.exp2(-np.float32(e)), dtype=np.float32)) for e in es)
                   for es in RET_EXP)


def _ret_chunk(rev, qk_ref, v_ref, s_ref, o_ref):
    C = RET_CHUNK
    ii = lax.broadcasted_iota(I32, (C, C), 0)
    jj = lax.broadcasted_iota(I32, (C, C), 1)
    mask = (jj > ii) if rev else (jj <= ii)
    dist = jnp.where(mask, (jj - ii) if rev else (ii - jj), 0).astype(F32)
    ri = lax.broadcasted_iota(I32, (C, 1), 0)
    kpow = (ri if rev else (C - 1 - ri)).astype(F32)
    qpow = ((C - ri) if rev else (ri + 1)).astype(F32)
    lane = lax.broadcasted_iota(I32, (C, LANES), 1)
    for h in range(4):
        lg = _LOG_GAMMA[1 if rev else 0][h]
        p = h // 2
        hm = (lane < HEAD) if h % 2 == 0 else (lane >= HEAD)
        qp = jnp.where(hm, qk_ref[:, p * LANES:(p + 1) * LANES], 0.0)
        kp = jnp.where(hm, qk_ref[:, HALF // 2 + p * LANES:HALF // 2 + (p + 1) * LANES], 0.0)
        vh = v_ref[:, h * LANES:(h + 1) * LANES].astype(BF16)
        decay = jnp.where(mask, jnp.exp(lg * dist), 0.0)
        sc = lax.dot_general(qp.astype(BF16), kp.astype(BF16), NT_DIMS, preferred_element_type=F32) * decay
        o_intra = jnp.dot(sc.astype(BF16), vh, preferred_element_type=F32)
        s_prev = s_ref[h]
        o_cross = jnp.dot((qp * jnp.exp(lg * qpow)).astype(BF16), s_prev.astype(BF16),
                          preferred_element_type=F32)
        o_ref[:, h * LANES:(h + 1) * LANES] = o_intra + o_cross
        kv = lax.dot_general((kp * jnp.exp(lg * kpow)).astype(BF16), vh, TN, preferred_element_type=F32)
        s_ref[h] = math.exp(lg * C) * s_prev + kv


def _ret_kernel(bf_ref, bb_ref, first_ref, last_ref, seq_ref,
                qkf_ref, vf_ref, qkb_ref, vb_ref, s0_ref, of_ref, ob_ref, sfin_ref, s_ref):
    step = pl.program_id(0)

    @pl.when(first_ref[step] == 1)
    def _():
        s_ref[...] = s0_ref[:, 0]

    _ret_chunk(False, qkf_ref, vf_ref, s_ref.at[0], of_ref)
    _ret_chunk(True, qkb_ref, vb_ref, s_ref.at[1], ob_ref)

    @pl.when(last_ref[step] == 1)
    def _():
        sfin_ref[:, 0] = s_ref[...]


def _retention(rqk, zr, s0):
    C = RET_CHUNK
    tabs = _seq_tables(C)
    st = pl.BlockSpec((2, 1, 4, LANES, LANES), lambda i, bf, bb, fi, la, sq: (0, sq[i], 0, 0, 0))
    spec = lambda use_b, col: pl.BlockSpec(
        (C, HALF), lambda i, bf, bb, fi, la, sq: ((bb if use_b else bf)[i], col))
    return pl.pallas_call(
        _ret_kernel,
        out_shape=[jax.ShapeDtypeStruct((T, HALF), F32), jax.ShapeDtypeStruct((T, HALF), F32),
                   jax.ShapeDtypeStruct((2, N_SEQ, 4, LANES, LANES), F32)],
        grid_spec=pltpu.PrefetchScalarGridSpec(
            num_scalar_prefetch=5, grid=(len(tabs[0]),),
            in_specs=[spec(False, 0), spec(False, 1), spec(True, 0), spec(True, 1), st],
            out_specs=[spec(False, 0), spec(True, 0), st],
            scratch_shapes=[pltpu.VMEM((2, 4, LANES, LANES), F32)]),
        compiler_params=_params(("arbitrary",)),
        name="retention",
    )(*tabs, rqk, zr, rqk, zr, s0)


def _out_kernel(*refs, even):
    if even:
        (a_ref, yf_ref, yb_ref, bonus_ref, g_ref, gng_ref, gnb_ref, ones_ref,
         x_ref, mod_ref, ng_ref, wo_ref, rw_ref, rb_ref,
         y_ref, xp_ref, ti_ref, tg_ref, rk_ref, cnt_ref, run_ref) = refs
        ones_bd = ones_ref[...]
        ys = yf_ref[...] + yb_ref[...]
        mu = _group_sum(ys, ones_bd) * (1.0 / HEAD)
        dv = ys - mu
        var = _group_sum(dv * dv, ones_bd) * (1.0 / HEAD)
        yn = dv * lax.rsqrt(var + RWKV_GN_EPS) * gng_ref[...] + gnb_ref[...]
        left = a_ref[...]
        right = (yn + bonus_ref[...]) * g_ref[...]
    else:
        (c_ref, of_ref, ob_ref, rg_ref, gng_ref,
         x_ref, mod_ref, ng_ref, wo_ref, rw_ref, rb_ref,
         y_ref, xp_ref, ti_ref, tg_ref, rk_ref, cnt_ref, run_ref) = refs
        left = c_ref[...]
        rg = rg_ref[...]
        gate = rg * jax.nn.sigmoid(rg)
        os_ = of_ref[...] + ob_ref[...]
        parts = []
        for h in range(4):
            oh = os_[:, h * LANES:(h + 1) * LANES]
            mu = jnp.mean(oh, axis=-1, keepdims=True)
            dv = oh - mu
            var = jnp.mean(dv * dv, axis=-1, keepdims=True)
            parts.append(dv * lax.rsqrt(var + 1e-5))
        right = gate * (jnp.concatenate(parts, axis=1) * gng_ref[...])
    mod = mod_ref[0]
    o = (jnp.dot(left.astype(BF16), wo_ref[0:HALF, :], preferred_element_type=F32)
         + jnp.dot(right.astype(BF16), wo_ref[HALF:2 * HALF, :], preferred_element_type=F32))
    y = x_ref[...] + mod[:, 2 * D:3 * D] * o
    y_ref[...] = y
    yn2 = y * lax.rsqrt(jnp.mean(y * y, axis=-1, keepdims=True) + 1e-6) * ng_ref[...]
    t = yn2 * (1.0 + mod[:, 4 * D:5 * D]) + mod[:, 3 * D:4 * D]
    xp_ref[...] = t
    logits = jnp.dot(t, rw_ref[...], precision=HIGHEST, preferred_element_type=F32) + rb_ref[...]
    lane = lax.broadcasted_iota(I32, (TM, LANES), 1)
    neg = jnp.float32(-jnp.inf)
    lg = jnp.where(lane < N_EXPERTS, logits, neg)
    vals, hits = [], []
    for _ in range(TOP_K):
        m = jnp.max(lg, axis=-1, keepdims=True)
        ix = jnp.min(jnp.where(lg == m, lane, LANES), axis=-1, keepdims=True)
        hit = lane == ix
        vals.append(m)
        hits.append((ix, hit))
        lg = jnp.where(hit, neg, lg)
    es = [jnp.exp(vv - vals[0]) for vv in vals]
    den = es[0] + es[1] + es[2] + es[3]

    @pl.when(pl.program_id(0) == 0)
    def _():
        run_ref[...] = jnp.zeros_like(run_ref)

    member = jnp.zeros((TM, LANES), F32)
    for _, hit in hits:
        member = member + jnp.where(hit, 1.0, 0.0)
    ri = lax.broadcasted_iota(I32, (TM, TM), 0)
    ci = lax.broadcasted_iota(I32, (TM, TM), 1)
    before = jnp.where(ci < ri, 1.0, 0.0).astype(BF16)
    seen = run_ref[...] + jnp.dot(before, member.astype(BF16), preferred_element_type=F32)
    ti = jnp.zeros((TM, LANES), I32)
    tg = jnp.zeros((TM, LANES), F32)
    rk = jnp.zeros((TM, LANES), F32)
    for kk, (ix, hit) in enumerate(hits):
        ti = jnp.where(lane == kk, ix, ti)
        tg = jnp.where(lane == kk, es[kk] / den, tg)
        rk = jnp.where(lane == kk, jnp.sum(jnp.where(hit, seen, 0.0), axis=-1, keepdims=True), rk)
    ti_ref[...] = ti
    tg_ref[...] = tg
    rk_ref[...] = rk.astype(I32)
    run_ref[...] = run_ref[...] + jnp.sum(member, axis=0, keepdims=True)
    cnt_ref[...] = run_ref[...]


def _out_proj(even, mix_args, mix_specs, x, mod, norm_g, w_out_bf16, rw_pad, rb_pad):
    row = lambda i: (i, 0)
    modspec = pl.BlockSpec((1, 1, N_MOD * D), lambda i: (_group(i), 0, 0))
    in_specs = list(mix_specs) + [pl.BlockSpec((TM, D), row), modspec, _full((1, D)), _full((D, D)),
                                  _full((D, LANES)), _full((1, LANES))]
    args = list(mix_args) + [x, mod, norm_g.reshape(1, D), w_out_bf16, rw_pad, rb_pad]
    lane_i = jax.ShapeDtypeStruct((T, LANES), I32)
    lane_spec = pl.BlockSpec((TM, LANES), row)
    return pl.pallas_call(
        functools.partial(_out_kernel, even=even),
        out_shape=[jax.ShapeDtypeStruct((T, D), F32), jax.ShapeDtypeStruct((T, D), F32),
                   lane_i, jax.ShapeDtypeStruct((T, LANES), F32), lane_i,
                   jax.ShapeDtypeStruct((1, LANES), F32)],
        grid=(NT,), in_specs=in_specs,
        out_specs=[pl.BlockSpec((TM, D), row), pl.BlockSpec((TM, D), row),
                   lane_spec, lane_spec, lane_spec, _full((1, LANES))],
        scratch_shapes=[pltpu.VMEM((1, LANES), F32)],
        compiler_params=_params(("arbitrary",), 48),
        name="out_proj",
    )(*args)


def _route_kernel(cnt_ref, ti_ref, rk_ref, dest_ref, te_ref, nt_ref):
    cnt = cnt_ref[...].astype(I32)
    ntile = lax.shift_right_logical(cnt + (TM - 1), 8)
    ei = lax.broadcasted_iota(I32, (LANES, LANES), 0)
    ej = lax.broadcasted_iota(I32, (LANES, LANES), 1)
    upto = jnp.where(ei <= ej, 1.0, 0.0).astype(BF16)
    ntile_f = jnp.broadcast_to(ntile.astype(F32), (8, LANES))
    tile_end = jnp.dot(ntile_f.astype(BF16), upto, preferred_element_type=F32)[0:1, :]
    row_start = (tile_end - ntile.astype(F32)) * float(TM)
    lane = lax.broadcasted_iota(I32, (TM, LANES), 1)
    ti = ti_ref[...]
    rk = rk_ref[...]
    dest = jnp.zeros((TM, LANES), F32)
    for k in range(TOP_K):
        hit = lane == ti[:, k:k + 1]
        start = jnp.sum(jnp.where(hit, row_start, 0.0), axis=-1, keepdims=True)
        dest = jnp.where(lane == k, start + rk[:, k:k + 1].astype(F32), dest)
    dest_ref[...] = dest.astype(I32)

    @pl.when(pl.program_id(0) == 0)
    def _():
        lane1 = lax.broadcasted_iota(I32, (1, LANES), 1)
        n_tiles = jnp.max(tile_end, axis=-1, keepdims=True)
        last_e = jnp.max(jnp.where(cnt > 0, lane1, 0), axis=-1, keepdims=True)
        tile = lax.broadcasted_iota(I32, (TM, 1), 0).astype(F32)
        te = jnp.sum(jnp.where(tile_end <= tile, 1, 0), axis=-1, keepdims=True)
        te = jnp.where(tile < n_tiles, te, last_e)
        te_ref[...] = jnp.broadcast_to(te, (TM, LANES)).astype(I32)
        nt_ref[...] = jnp.broadcast_to(n_tiles, (8, LANES)).astype(I32)


def _route(cnt, ti, rk):
    row = lambda i: (i, 0)
    return pl.pallas_call(
        _route_kernel,
        out_shape=[jax.ShapeDtypeStruct((T, LANES), I32), jax.ShapeDtypeStruct((TM, LANES), I32),
                   jax.ShapeDtypeStruct((8, LANES), I32)],
        grid=(NT,),
        in_specs=[_full((1, LANES)), pl.BlockSpec((TM, LANES), row), pl.BlockSpec((TM, LANES), row)],
        out_specs=[pl.BlockSpec((TM, LANES), row), _full((TM, LANES)), _full((8, LANES))],
        compiler_params=_params(("arbitrary",)),
        name="moe_route",
    )(cnt, ti, rk)


def _row_copy(src_ref, src_row, dst_ref, dst_row, sem):
    return pltpu.make_async_copy(src_ref.at[pl.ds(src_row, 1)], dst_ref.at[pl.ds(dst_row, 1)], sem)


def _invert_kernel(dest_ref, owner_ref):
    def clear(r, carry):
        owner_ref[r] = -1
        return carry

    lax.fori_loop(0, R_PAD, clear, 0, unroll=8)

    def place(a, carry):
        owner_ref[dest_ref[a]] = a
        return carry

    lax.fori_loop(0, N_ASSIGN, place, 0, unroll=8)


def _invert(dest_flat):
    return pl.pallas_call(
        _invert_kernel,
        out_shape=jax.ShapeDtypeStruct((R_PAD,), I32),
        in_specs=[pl.BlockSpec(memory_space=pltpu.SMEM)],
        out_specs=pl.BlockSpec(memory_space=pltpu.SMEM),
        name="moe_invert",
    )(dest_flat)


def _wait_tile(src_ref, dst_ref, sem):
    pltpu.make_async_copy(src_ref, dst_ref, sem).wait()


def _moe_up_kernel(te_ref, nt_ref, owner_ref, x_ref, w_ref, b_ref, act_ref, xg_ref, sem):
    i = pl.program_id(0)
    slot = i % 2
    whole = x_ref.at[pl.ds(0, TM)]

    def gather(tile, buf):
        base = tile * TM
        for r in range(TM):
            tok = lax.shift_right_logical(jnp.maximum(owner_ref[base + r], 0), 2)
            _row_copy(x_ref, tok, xg_ref.at[buf], r, sem.at[buf]).start()

    @pl.when(i == 0)
    def _():
        gather(0, 0)

    @pl.when(nt_ref[0] > 0)
    def _():
        _wait_tile(whole, xg_ref.at[slot], sem.at[slot])
        gather(jnp.minimum(i + 1, MOE_TILES - 1), 1 - slot)

    x16 = xg_ref[slot].astype(BF16)
    for gc in (slice(0, HALF), slice(HALF, D)):
        uc = slice(D + gc.start, D + gc.stop)
        g = jnp.dot(x16, w_ref[0, :, gc].astype(BF16), preferred_element_type=F32) + b_ref[0, :, gc]
        u = jnp.dot(x16, w_ref[0, :, uc].astype(BF16), preferred_element_type=F32) + b_ref[0, :, uc]
        gt = jnp.minimum(g, SWIGLU_LIMIT)
        up = jnp.clip(u, -SWIGLU_LIMIT, SWIGLU_LIMIT)
        act_ref[:, gc] = ((up + 1.0) * gt * jax.nn.sigmoid(SWIGLU_ALPHA * gt)).astype(BF16)

    @pl.when(i == MOE_TILES - 1)
    def _():
        _wait_tile(whole, xg_ref.at[1 - slot], sem.at[1 - slot])


def _moe_up(te, n_tiles, owner, x, w_gu, b_gu):
    return pl.pallas_call(
        _moe_up_kernel,
        out_shape=jax.ShapeDtypeStruct((R_PAD, D), BF16),
        grid_spec=pltpu.PrefetchScalarGridSpec(
            num_scalar_prefetch=3, grid=(MOE_TILES,),
            in_specs=[pl.BlockSpec(memory_space=pl.ANY),
                      pl.BlockSpec((1, D, 2 * D), lambda i, te, nt, ow: (te[i], 0, 0)),
                      pl.BlockSpec((1, 1, 2 * D), lambda i, te, nt, ow: (te[i], 0, 0))],
            out_specs=pl.BlockSpec((TM, D), lambda i, te, nt, ow: (i, 0)),
            scratch_shapes=[pltpu.VMEM((2, TM, D), F32), pltpu.SemaphoreType.DMA((2,))]),
        compiler_params=_params(("arbitrary",), 56),
        name="moe_up",
    )(te, n_tiles, owner, x, w_gu, b_gu)


def _moe_down_kernel(te_ref, nt_ref, owner_ref, act_ref, w_ref, b_ref, ys_ref, y_ref, sem):
    i = pl.program_id(0)

    def spare(buf):
        return ys_ref.at[pl.ds(TOP_K * T + buf * TM, TM)]

    def scatter(tile):
        base = tile * TM
        buf = tile % 3
        pad0 = TOP_K * T + buf * TM
        for r in range(TM):
            a = owner_ref[base + r]
            dst = jnp.where(a < 0, pad0 + r, jnp.bitwise_and(a, TOP_K - 1) * T + lax.shift_right_logical(a, 2))
            _row_copy(y_ref.at[buf], r, ys_ref, dst, sem.at[buf]).start()

    def wait(tile):
        _wait_tile(y_ref.at[tile % 3], spare(0), sem.at[tile % 3])

    @pl.when(i == 0)
    def _():
        y_ref[1] = jnp.zeros((TM, D), F32)
        for buf in range(3):
            pltpu.make_async_copy(y_ref.at[1], spare(buf), sem.at[1]).start()
        for buf in range(3):
            wait(1)

    @pl.when(i >= 1)
    def _():
        scatter(i - 1)

    y_ref[i % 3] = jnp.dot(act_ref[...], w_ref[0].astype(BF16), preferred_element_type=F32) + b_ref[0]

    @pl.when(i >= 2)
    def _():
        wait(i - 2)

    @pl.when(i == MOE_TILES - 1)
    def _():
        scatter(i)
        wait(i - 1)
        wait(i)


def _tile_clamped(i, te, nt, ow):
    return (jnp.minimum(i, jnp.maximum(nt[0] - 1, 0)), 0)


def _moe_down(te, n_tiles, owner, act, w_dn, b_dn):
    return pl.pallas_call(
        _moe_down_kernel,
        out_shape=jax.ShapeDtypeStruct((TOP_K * T + 3 * TM, D), F32),
        grid_spec=pltpu.PrefetchScalarGridSpec(
            num_scalar_prefetch=3, grid=(MOE_TILES,),
            in_specs=[pl.BlockSpec((TM, D), _tile_clamped),
                      pl.BlockSpec((1, D, D), lambda i, te, nt, ow: (te[i], 0, 0)),
                      pl.BlockSpec((1, 1, D), lambda i, te, nt, ow: (te[i], 0, 0))],
            out_specs=pl.BlockSpec(memory_space=pl.ANY),
            scratch_shapes=[pltpu.VMEM((3, TM, D), F32), pltpu.SemaphoreType.DMA((3,))]),
        compiler_params=_params(("arbitrary",), 40),
        name="moe_down",
    )(te, n_tiles, owner, act, w_dn, b_dn)


def _combine_kernel(x_ref, tg_ref, mod_ref, y0_ref, y1_ref, y2_ref, y3_ref, o_ref):
    tg = tg_ref[...]
    f = tg[:, 0:1] * y0_ref[...]
    for k, y_ref in ((1, y1_ref), (2, y2_ref), (3, y3_ref)):
        f = f + tg[:, k:k + 1] * y_ref[...]
    o_ref[...] = x_ref[...] + mod_ref[0][:, 5 * D:6 * D] * f


def _combine(x, tg, mod, ys):
    row = lambda i: (i, 0)
    slot_spec = lambda k: pl.BlockSpec((TM, D), lambda i: (k * NT + i, 0))
    return pl.pallas_call(
        _combine_kernel,
        out_shape=jax.ShapeDtypeStruct((T, D), F32),
        grid=(NT,),
        in_specs=[pl.BlockSpec((TM, D), row), pl.BlockSpec((TM, LANES), row),
                  pl.BlockSpec((1, 1, N_MOD * D), lambda i: (_group(i), 0, 0)),
                  slot_spec(0), slot_spec(1), slot_spec(2), slot_spec(3)],
        out_specs=pl.BlockSpec((TM, D), row),
        compiler_params=_params(("arbitrary",)),
        name="moe_combine",
    )(x, tg, mod, ys, ys, ys, ys)


def _moe(layer, y, xt, ti, tg, rk, cnt, mod, w_gu, b_gu, w_dn, b_dn):
    dest, te, nt = _route(cnt, ti, rk)
    owner = _invert(dest[:, :TOP_K].reshape(-1))
    te = te[:MOE_TILES, 0] + layer * N_EXPERTS
    n_tiles = nt[0, :1]
    n_all = w_gu.shape[0] * N_EXPERTS
    act = _moe_up(te, n_tiles, owner, xt, w_gu.reshape(n_all, D, 2 * D), b_gu.reshape(n_all, 1, 2 * D))
    ys = _moe_down(te, n_tiles, owner, act, w_dn.reshape(n_all, D, D), b_dn.reshape(n_all, 1, D))
    return _combine(y, tg, mod, ys)


def _ones_blockdiag():
    idx = np.arange(HALF) // HEAD
    return jnp.asarray((idx[:, None] == idx[None, :]).astype(np.float32), dtype=BF16)


def _rope_tables():
    pos = jnp.arange(L_SAMPLE)
    rowp = (pos // 64).astype(F32)
    colp = (pos % 64).astype(F32)
    nf = HEAD // 4
    inv = jnp.power(10000.0, -jnp.arange(nf, dtype=F32) / nf)
    ar = rowp[:, None] * inv[None, :]
    ac = colp[:, None] * inv[None, :]
    cos64 = jnp.concatenate([jnp.cos(ar), jnp.cos(ar), jnp.cos(ac), jnp.cos(ac)], axis=1)
    sin64 = jnp.concatenate([-jnp.sin(ar), jnp.sin(ar), -jnp.sin(ac), jnp.sin(ac)], axis=1)
    cos = jnp.tile(cos64, (1, HALF // HEAD))
    sin = jnp.tile(sin64, (1, HALF // HEAD))
    ident = jnp.ones((TM, HALF), F32)
    return (jnp.concatenate([ident, cos], axis=0), jnp.concatenate([jnp.zeros((TM, HALF), F32), sin], axis=0))


def _bd_pairs(s):
    lead = s.shape[:-3]
    s = s.reshape(lead + (4, 2, HEAD, HEAD))
    z = jnp.zeros_like(s[..., 0, :, :])
    top = jnp.concatenate([s[..., 0, :, :], z], axis=-1)
    bot = jnp.concatenate([z, s[..., 1, :, :]], axis=-1)
    return jnp.concatenate([top, bot], axis=-2)


def _bd_unpairs(s):
    a = s[..., 0:HEAD, 0:HEAD]
    b = s[..., HEAD:, HEAD:]
    out = jnp.stack([a, b], axis=-3)
    return out.reshape(s.shape[:-3] + (8, HEAD, HEAD))


def kernel(x_prompt, x_sample, state_rwkv, cache_k_diff, cache_v_diff, state_retention, c, c_ctx, norm_g, ada_w, ada_b, e_w_in, e_w_out, sgu_ln_g, sgu_w, sgu_b, rw_mu, rw_w0, rw_w_up, rw_a0, rw_a_up, rw_g_up, rw_k_k, rw_k_a, rw_r_k, rw_gn_g, rw_gn_b, o_w_in, o_w_out, da_qk_g, da_lam, da_subln_g, ret_gn_g, router_w, router_b, ex_w_gu, ex_b_gu, ex_w_dn, ex_b_dn):
    x = jnp.concatenate([x_prompt.reshape(T_PROMPT, D), x_sample.reshape(T_SAMPLE, D)], axis=0)
    cvec8 = jnp.concatenate([c_ctx[None, :], c, jnp.zeros((3, D), F32)], axis=0)
    mods = _adaln(cvec8, ada_w, ada_b)
    mod0 = mods[0].reshape(8, 1, N_MOD * D)
    mod1 = mods[1].reshape(8, 1, N_MOD * D)
    ones_bd = _ones_blockdiag()
    rw_pad = jnp.pad(router_w, ((0, 0), (0, 0), (0, LANES - N_EXPERTS)))
    rb_pad = jnp.pad(router_b, ((0, 0), (0, LANES - N_EXPERTS))).reshape(2, 1, LANES)
    row = lambda i: (i, 0)
    half = pl.BlockSpec((TM, HALF), row)

    za, zb = _in_proj(x, norm_g[0, 0], mod0, e_w_in[0].astype(BF16), (2 * HALF, B_COLS))
    bs_full = jnp.repeat(sgu_b[0].T, HEAD, axis=1)
    a_out = _sgu(za, sgu_ln_g[0], sgu_w[0].astype(BF16), bs_full)
    zpad = jnp.zeros((2, HEAD, HALF), F32)
    wup_pad = jnp.concatenate([rw_w_up[0], zpad], axis=1).astype(BF16)
    aup_pad = jnp.concatenate([zpad, rw_a_up[0]], axis=1).astype(BF16)
    r, v, kkn, bonus, g, lw, kt, b = _rwkv_prep(zb, rw_mu[0], rw_k_k[0], rw_k_a[0], rw_r_k[0], rw_w0[0], rw_a0[0],
                                                wup_pad, aup_pad, rw_g_up[0].astype(BF16), ones_bd)
    s0_sample = _bd_pairs(jnp.moveaxis(state_rwkv[:, 0], 1, 0))
    s0_rw = jnp.concatenate([jnp.zeros((2, N_PROMPT, 4, LANES, LANES), F32), s0_sample], axis=1)
    yf_rw, yb_rw, sfin_rw = _rwkv_scan(r, v, kkn, lw, kt, b, s0_rw)
    new_rwkv = jnp.moveaxis(_bd_unpairs(sfin_rw[:, :N_PROMPT]), 0, 1)[:, None]
    y0, xp0, ti0, tg0, rk0, cnt0 = _out_proj(
        True,
        [a_out, yf_rw, yb_rw, bonus, g, rw_gn_g[0].reshape(1, HALF), rw_gn_b[0].reshape(1, HALF), ones_bd],
        [half, half, half, half, half, _full((1, HALF)), _full((1, HALF)), _full((HALF, HALF))],
        x, mod0, norm_g[0, 1], e_w_out[0].astype(BF16), rw_pad[0], rb_pad[0])
    x1 = _moe(0, y0, xp0, ti0, tg0, rk0, cnt0, mod0, ex_w_gu, ex_b_gu, ex_w_dn, ex_b_dn)

    zc, zr = _in_proj(x1, norm_g[1, 0], mod1, o_w_in[0].astype(BF16), (3 * HALF, 3 * HALF))
    cos_tab, sin_tab = _rope_tables()
    qkg = jnp.tile(da_qk_g[0], (1, HALF // HEAD))
    cq, ck, ck_raw, rqk = _odd_prep(zc, zr, cos_tab, sin_tab, qkg, ones_bd)
    lambda_init = 0.8 - 0.6 * math.exp(-0.3 * 1)
    lv = da_lam[0]
    lam = jnp.exp(jnp.sum(lv[0] * lv[1])) - jnp.exp(jnp.sum(lv[2] * lv[3])) + lambda_init
    c_out = _attn(cq, ck, zc, 0, N_PROMPT, L_PROMPT, lam, da_subln_g[0], 1.0 - lambda_init,
                  jnp.zeros((T, HALF), F32))
    ctx_k = cache_k_diff[:, 0].reshape(N_SAMPLE, PAST, HALF)
    ctx_v = cache_v_diff[:, 0].reshape(N_SAMPLE, PAST, HALF)
    c_out = _attn(cq, ck, zc, T_PROMPT, N_SAMPLE, L_SAMPLE, lam, da_subln_g[0], 1.0 - lambda_init,
                  c_out, ctx_k, ctx_v)
    sr = jnp.moveaxis(state_retention[:, 0], 1, 0)
    zr0 = jnp.zeros_like(sr)
    s0_sample = jnp.stack([jnp.concatenate([sr[:, :, 0], zr0[:, :, 0]], axis=-2),
                           jnp.concatenate([zr0[:, :, 1], sr[:, :, 1]], axis=-2),
                           jnp.concatenate([sr[:, :, 2], zr0[:, :, 2]], axis=-2),
                           jnp.concatenate([zr0[:, :, 3], sr[:, :, 3]], axis=-2)], axis=2)
    s0_ret = jnp.concatenate([jnp.zeros((2, N_PROMPT, 4, LANES, LANES), F32), s0_sample], axis=1)
    of_ret, ob_ret, rfin = _retention(rqk, zr, s0_ret)
    rfin_p = rfin[:, :N_PROMPT]
    new_ret = jnp.stack([rfin_p[:, :, 0, 0:HEAD], rfin_p[:, :, 1, HEAD:], rfin_p[:, :, 2, 0:HEAD],
                         rfin_p[:, :, 3, HEAD:]], axis=2)
    new_ret = jnp.moveaxis(new_ret, 0, 1)[:, None]
    y1, xp1, ti1, tg1, rk1, cnt1 = _out_proj(
        False,
        [c_out, of_ret, ob_ret, zr, ret_gn_g[0].reshape(1, HALF)],
        [half, half, half, pl.BlockSpec((TM, HALF), lambda i: (i, 2)), _full((1, HALF))],
        x1, mod1, norm_g[1, 1], o_w_out[0].astype(BF16), rw_pad[1], rb_pad[1])
    y_fin = _moe(1, y1, xp1, ti1, tg1, rk1, cnt1, mod1, ex_w_gu, ex_b_gu, ex_w_dn, ex_b_dn)

    new_k = ck_raw[:T_PROMPT].reshape(N_PROMPT, 1, L_PROMPT, 4, LANES)
    new_v = zc[:T_PROMPT, 2 * HALF:3 * HALF].reshape(N_PROMPT, 1, L_PROMPT, 4, LANES)
    return (y_fin[:T_PROMPT].reshape(N_PROMPT, L_PROMPT, D), y_fin[T_PROMPT:].reshape(N_SAMPLE, L_SAMPLE, D),
            new_rwkv, new_k, new_v, new_ret)
```

```python
import functools
import math

import numpy as np
import jax
import jax.numpy as jnp
from jax import lax
from jax.experimental import pallas as pl
from jax.experimental.pallas import tpu as pltpu

F32 = jnp.float32
BF16 = jnp.bfloat16
I32 = jnp.int32
HIGHEST = lax.Precision.HIGHEST

D = 1024
N_PROMPT, L_PROMPT = 16, 256
N_SAMPLE, L_SAMPLE = 4, 1024
N_SEQ = N_PROMPT + N_SAMPLE
PAST = 256
T_PROMPT = N_PROMPT * L_PROMPT
T_SAMPLE = N_SAMPLE * L_SAMPLE
T = T_PROMPT + T_SAMPLE
TM = 256
NT = T // TM
PROMPT_TILES = T_PROMPT // TM
TILES_PER_SAMPLE = L_SAMPLE // TM
N_MOD = 6
HALF = 512
B_COLS = 1792
HEAD = 64
W_DECAY_SCALE = math.exp(-0.5)
RWKV_GN_EPS = 64e-5
RW_CHUNK = 64
RET_CHUNK = 128
RET_EXP = ((5.0, 7.0, 9.0, 11.0), (6.0, 8.0, 10.0, 12.0))
N_EXPERTS = 32
TOP_K = 4
SWIGLU_LIMIT = 7.0
SWIGLU_ALPHA = 1.702
N_ASSIGN = T * TOP_K
MOE_TILES = N_ASSIGN // TM + N_EXPERTS
R_PAD = MOE_TILES * TM
LANES = 128

NN = (((1,), (0,)), ((), ()))
NT_DIMS = (((1,), (1,)), ((), ()))
TN = (((0,), (0,)), ((), ()))


def _group(i):
    return jnp.where(i < PROMPT_TILES, 0, 1 + (i - PROMPT_TILES) // TILES_PER_SAMPLE)


def _mm(a, b, dims=NN, passes=1):
    dg = functools.partial(lax.dot_general, dimension_numbers=dims, preferred_element_type=F32)
    if passes == 1:
        return dg(a.astype(BF16), b.astype(BF16))
    a = a.astype(F32)
    b = b.astype(F32)
    ah = a.astype(BF16)
    al = (a - ah.astype(F32)).astype(BF16)
    bh = b.astype(BF16)
    bl = (b - bh.astype(F32)).astype(BF16)
    return dg(ah, bh) + (dg(ah, bl) + dg(al, bh))


def _group_sum(x, ones_bd):
    xh = x.astype(BF16)
    xl = (x - xh.astype(F32)).astype(BF16)
    return (jnp.dot(xh, ones_bd, preferred_element_type=F32)
            + jnp.dot(xl, ones_bd, preferred_element_type=F32))


def _full(shape):
    nd = len(shape)
    return pl.BlockSpec(shape, lambda *_: (0,) * nd)


def _params(sem, vmem_mb=None):
    kw = {}
    if vmem_mb is not None:
        kw["vmem_limit_bytes"] = vmem_mb * 1024 * 1024
    return pltpu.CompilerParams(dimension_semantics=sem, **kw)


def _seq_tables(chunk):
    blk_f, blk_b, first, last, seq = [], [], [], [], []
    row = 0
    for s in range(N_SEQ):
        n = (L_PROMPT if s < N_PROMPT else L_SAMPLE) // chunk
        base = row // chunk
        for j in range(n):
            blk_f.append(base + j)
            blk_b.append(base + n - 1 - j)
            first.append(int(j == 0))
            last.append(int(j == n - 1))
            seq.append(s)
        row += n * chunk
    return tuple(np.asarray(a, np.int32) for a in (blk_f, blk_b, first, last, seq))


def _adaln_kernel(c_ref, w_ref, b_ref, o_ref):
    c = c_ref[...]
    s = c * jax.nn.sigmoid(c)
    o_ref[0] = jnp.dot(s, w_ref[0], precision=HIGHEST, preferred_element_type=F32) + b_ref[0]


def _adaln(cvec8, ada_w, ada_b):
    depth, _, n = ada_w.shape
    bn = 1536
    return pl.pallas_call(
        _adaln_kernel,
        out_shape=jax.ShapeDtypeStruct((depth, 8, n), F32),
        grid=(depth, n // bn),
        in_specs=[pl.BlockSpec((8, D), lambda l, j: (0, 0)),
                  pl.BlockSpec((1, D, bn), lambda l, j: (l, 0, j)),
                  pl.BlockSpec((1, 1, bn), lambda l, j: (l, 0, j))],
        out_specs=pl.BlockSpec((1, 8, bn), lambda l, j: (l, 0, j)),
        compiler_params=_params(("arbitrary", "arbitrary"), 40),
        name="adaln",
    )(cvec8, ada_w, ada_b.reshape(depth, 1, n))


def _in_kernel(x_ref, g_ref, mod_ref, w_ref, *outs, splits):
    x = x_ref[...]
    mod = mod_ref[0]
    y = x * lax.rsqrt(jnp.mean(x * x, axis=-1, keepdims=True) + 1e-6) * g_ref[...]
    h = (y * (1.0 + mod[:, D:2 * D]) + mod[:, 0:D]).astype(BF16)
    off = 0
    for o_ref, n in zip(outs, splits):
        o_ref[...] = jnp.dot(h, w_ref[:, off:off + n], preferred_element_type=F32)
        off += n


def _in_proj(x, g, mod, w_bf16, splits):
    n = w_bf16.shape[1]
    row = lambda i: (i, 0)
    return pl.pallas_call(
        functools.partial(_in_kernel, splits=splits),
        out_shape=[jax.ShapeDtypeStruct((T, s), F32) for s in splits],
        grid=(NT,),
        in_specs=[pl.BlockSpec((TM, D), row), _full((1, D)),
                  pl.BlockSpec((1, 1, N_MOD * D), lambda i: (_group(i), 0, 0)), _full((D, n))],
        out_specs=[pl.BlockSpec((TM, s), row) for s in splits],
        compiler_params=_params(("arbitrary",), 48),
        name="in_proj",
    )(x, g.reshape(1, D), mod, w_bf16)


def _gelu(x):
    return 0.5 * x * (1.0 + lax.erf(x * (1.0 / math.sqrt(2.0))))


def _sgu_kernel(za_ref, lng_ref, ws_ref, bs_ref, o_ref):
    u = _gelu(za_ref[:, 0:HALF])
    va = _gelu(za_ref[:, HALF:2 * HALF])
    mu = jnp.mean(va, axis=-1, keepdims=True)
    dv = va - mu
    var = jnp.mean(dv * dv, axis=-1, keepdims=True)
    vn = dv * lax.rsqrt(var + 1e-5) * lng_ref[...]
    lane = lax.broadcasted_iota(I32, (LANES, LANES), 1)
    first = lane < HEAD
    for c in range(TM // LANES):
        rows = slice(c * LANES, (c + 1) * LANES)
        for p in range(HALF // LANES):
            cols = slice(p * LANES, (p + 1) * LANES)
            vp = vn[rows, cols]
            s = (jnp.dot(ws_ref[2 * p], jnp.where(first, vp, 0.0).astype(BF16), preferred_element_type=F32)
                 + jnp.dot(ws_ref[2 * p + 1], jnp.where(first, 0.0, vp).astype(BF16), preferred_element_type=F32))
            o_ref[rows, cols] = u[rows, cols] * (s + bs_ref[:, cols])


def _sgu(za, ln_g, w_s_bf16, bs_full):
    return pl.pallas_call(
        _sgu_kernel,
        out_shape=jax.ShapeDtypeStruct((T, HALF), F32),
        grid=(NT,),
        in_specs=[pl.BlockSpec((TM, 2 * HALF), lambda i: (i, 0)), _full((1, HALF)),
                  _full((8, LANES, LANES)), _full((LANES, HALF))],
        out_specs=pl.BlockSpec((TM, HALF), lambda i: (i, 0)),
        compiler_params=_params(("arbitrary",)),
        name="sgu",
    )(za, ln_g.reshape(1, HALF), w_s_bf16, bs_full)


def _rwkv_prep_kernel(zb_ref, zp_ref, zn_ref, mu_ref, kk_ref, ka_ref, rk_ref, w0_ref, a0_ref,
                      wup_ref, aup_ref, gup_ref, ones_ref,
                      r_ref, v_ref, kkn_ref, bonus_ref, g_ref, lw_ref, kt_ref, b_ref):
    i = pl.program_id(0)
    in_sample = i >= PROMPT_TILES
    pos = (i - PROMPT_TILES) % TILES_PER_SAMPLE
    is_first = jnp.logical_or(jnp.logical_not(in_sample), pos == 0)
    is_last = jnp.logical_or(jnp.logical_not(in_sample), pos == TILES_PER_SAMPLE - 1)
    zb = zb_ref[...]
    prev_row = jnp.where(is_first, 0.0, zp_ref[7:8, :])
    next_row = jnp.where(is_last, 0.0, zn_ref[0:1, :])
    rowid = lax.broadcasted_iota(I32, (TM, 1), 0)
    zp = jnp.where(rowid == 0, prev_row, pltpu.roll(zb, 1, 0))
    zn = jnp.where(rowid == TM - 1, next_row, pltpu.roll(zb, TM - 1, 0))
    zs = zb + mu_ref[0:1, :] * (zp - zb) + mu_ref[1:2, :] * (zn - zb)
    r = zs[:, 0:HALF]
    k = zs[:, HALF:2 * HALF]
    v = zs[:, 2 * HALF:3 * HALF]
    wa = zs[:, 3 * HALF:3 * HALF + LANES]
    gd = zs[:, 3 * HALF + LANES:B_COLS]
    ones_bd = ones_ref[...]
    r_ref[...] = r
    v_ref[...] = v
    g_ref[...] = jnp.dot(jax.nn.sigmoid(gd).astype(BF16), gup_ref[...], preferred_element_type=F32)
    kk = k * kk_ref[...]
    kkn = kk / jnp.maximum(jnp.sqrt(_group_sum(kk * kk, ones_bd)), 1e-6)
    kkn_ref[...] = kkn
    bonus_ref[...] = _group_sum(r * k * rk_ref[...], ones_bd) * v
    tw = jnp.tanh(wa).astype(BF16)
    wa16 = wa.astype(BF16)
    for dd in range(2):
        lw_ref[dd] = -W_DECAY_SCALE * jax.nn.sigmoid(
            w0_ref[dd:dd + 1, :] + jnp.dot(tw, wup_ref[dd], preferred_element_type=F32))
        a = jax.nn.sigmoid(a0_ref[dd:dd + 1, :] + jnp.dot(wa16, aup_ref[dd], preferred_element_type=F32))
        kt_ref[dd] = k * (1.0 + (a - 1.0) * ka_ref[...])
        b_ref[dd] = a * kkn


def _rwkv_prep(zb, mu, k_k, k_a, r_k, w0, a0, wup_pad, aup_pad, g_up, ones_bd):
    row = lambda i: (i, 0)
    halo = TM // 8
    one = jax.ShapeDtypeStruct((T, HALF), F32)
    two = jax.ShapeDtypeStruct((2, T, HALF), F32)
    o1 = pl.BlockSpec((TM, HALF), row)
    o2 = pl.BlockSpec((2, TM, HALF), lambda i: (0, i, 0))
    return pl.pallas_call(
        _rwkv_prep_kernel,
        out_shape=[one, one, one, one, one, two, two, two],
        grid=(NT,),
        in_specs=[pl.BlockSpec((TM, B_COLS), row),
                  pl.BlockSpec((8, B_COLS), lambda i: (jnp.maximum(i * halo - 1, 0), 0)),
                  pl.BlockSpec((8, B_COLS), lambda i: (jnp.minimum((i + 1) * halo, T // 8 - 1), 0)),
                  _full((2, B_COLS)), _full((1, HALF)), _full((1, HALF)), _full((1, HALF)),
                  _full((2, HALF)), _full((2, HALF)),
                  _full((2, LANES, HALF)), _full((2, LANES, HALF)), _full((LANES, HALF)),
                  _full((HALF, HALF))],
        out_specs=[o1, o1, o1, o1, o1, o2, o2, o2],
        compiler_params=_params(("arbitrary",), 48),
        name="rwkv_prep",
    )(zb, zb, zb, mu, k_k.reshape(1, HALF), k_a.reshape(1, HALF), r_k.reshape(1, HALF), w0, a0,
      wup_pad, aup_pad, g_up, ones_bd)


def _rwkv_chunks(dirs):
    C = RW_CHUNK
    ti = lax.broadcasted_iota(I32, (C, C), 0)
    tj = lax.broadcasted_iota(I32, (C, C), 1)
    bi = lax.broadcasted_iota(I32, (LANES, LANES), 0)
    bj = lax.broadcasted_iota(I32, (LANES, LANES), 1)
    same = (bi >> 6) == (bj >> 6)
    pi = bi & (C - 1)
    pj = bj & (C - 1)
    eye = (bi == bj).astype(F32)
    h0 = lax.broadcasted_iota(I32, (C, LANES), 1) < HEAD

    def stack(x):
        return jnp.concatenate([jnp.where(h0, x, 0.0), jnp.where(h0, 0.0, x)], axis=0)

    def fold(x):
        return x[0:C] + x[C:2 * C]

    chains = []
    for rev, r, v, kk, lw, kt, b, s_ref, y_ref in dirs:
        tri = jnp.where((tj >= ti) if rev else (tj <= ti), 1.0, 0.0).astype(F32)
        cs = jnp.dot(tri, lw, precision=HIGHEST, preferred_element_type=F32)
        ctot = cs[0:1, :] if rev else cs[C - 1:C, :]
        e_neg = jnp.exp(-cs)
        e_tail = jnp.exp(ctot - cs)
        q1 = kk * jnp.exp(cs - lw)
        k1 = kt * e_neg
        b1 = b * e_neg
        r1 = r * jnp.exp(cs)
        k2 = kt * e_tail
        b2 = b * e_tail
        e_tot = jnp.exp(ctot)
        strict = jnp.logical_and(same, (pj > pi) if rev else (pj < pi))
        incl = jnp.logical_and(same, (pj >= pi) if rev else (pj <= pi))
        for p in range(HALF // LANES):
            cols = slice(p * LANES, (p + 1) * LANES)
            chains.append(dict(p=p, cols=cols, strict=strict, incl=incl, s_ref=s_ref, y_ref=y_ref,
                               q1=q1[:, cols], k1=k1[:, cols], b1=b1[:, cols], r1=r1[:, cols],
                               k2=k2[:, cols], b2=b2[:, cols], v=v[:, cols], e_tot=e_tot[:, cols]))

    for ch in chains:
        lhs = jnp.concatenate([stack(ch["q1"]), stack(ch["r1"])], axis=0)
        rhs = jnp.concatenate([ch["k1"], ch["k1"], ch["b1"], ch["b1"]], axis=0)
        gm = _mm(lhs, rhs, NT_DIMS, 3)
        ch["mk"] = jnp.where(ch["strict"], gm[0:2 * C, 0:2 * C], 0.0)
        ch["mb"] = jnp.where(ch["strict"], gm[0:2 * C, 2 * C:4 * C], 0.0)
        ch["nk"] = jnp.where(ch["incl"], gm[2 * C:4 * C, 0:2 * C], 0.0)
        ch["nb"] = jnp.where(ch["incl"], gm[2 * C:4 * C, 2 * C:4 * C], 0.0)
        ch["tinv"] = eye - jnp.where((pi >> 1) == (pj >> 1), ch["mb"], 0.0)
    size = 2
    while size < C:
        sh = size.bit_length() - 1
        blk = jnp.logical_and((pi >> (sh + 1)) == (pj >> (sh + 1)), (pi >> sh) != (pj >> sh))
        for ch in chains:
            ch["tn"] = _mm(ch["tinv"], jnp.where(blk, ch["mb"], 0.0), NN, 3)
        for ch in chains:
            ch["tinv"] = ch["tinv"] - _mm(ch["tn"], ch["tinv"], NN, 3)
        size *= 2
    for ch in chains:
        vst = stack(ch["v"])
        ch["mkv"] = fold(_mm(ch["mk"], vst, NN, 3))
        ch["nkv"] = fold(_mm(ch["nk"], vst, NN, 3))
        ch["s"] = ch["s_ref"][ch["p"]]
        ch["qr"] = _mm(jnp.concatenate([ch["q1"], ch["r1"]], axis=0), ch["s"], NT_DIMS, 3)
    for ch in chains:
        ch["u"] = fold(_mm(ch["tinv"], stack(ch["mkv"] + ch["qr"][0:C]), NN, 3))
    for ch in chains:
        ch["y_ref"][:, ch["cols"]] = ch["qr"][C:2 * C] + ch["nkv"] - fold(_mm(ch["nb"], stack(ch["u"]), NN, 3))
        upd = _mm(jnp.concatenate([ch["v"], ch["u"]], axis=0),
                  jnp.concatenate([ch["k2"], -ch["b2"]], axis=0), TN, 3)
        ch["s_ref"][ch["p"]] = ch["s"] * ch["e_tot"] + jnp.where(same, upd, 0.0)


def _rwkv_scan_kernel(bf_ref, bb_ref, first_ref, last_ref, seq_ref,
                      rf_ref, vf_ref, kkf_ref, lwf_ref, ktf_ref, bfw_ref,
                      rb_ref, vb_ref, kkb_ref, lwb_ref, ktb_ref, bbw_ref, s0_ref,
                      yf_ref, yb_ref, sfin_ref, s_ref):
    step = pl.program_id(0)

    @pl.when(first_ref[step] == 1)
    def _():
        s_ref[...] = s0_ref[:, 0]

    _rwkv_chunks([
        (False, rf_ref[...], vf_ref[...], kkf_ref[...], lwf_ref[0], ktf_ref[0], bfw_ref[0], s_ref.at[0], yf_ref),
        (True, rb_ref[...], vb_ref[...], kkb_ref[...], lwb_ref[0], ktb_ref[0], bbw_ref[0], s_ref.at[1], yb_ref)])

    @pl.when(last_ref[step] == 1)
    def _():
        sfin_ref[:, 0] = s_ref[...]


def _rwkv_scan(r, v, kk, lw, kt, b, s0_bd):
    C = RW_CHUNK
    tabs = _seq_tables(C)
    fwd = lambda i, bf, bb, fi, la, sq: (bf[i], 0)
    bwd = lambda i, bf, bb, fi, la, sq: (bb[i], 0)
    fwd3 = lambda i, bf, bb, fi, la, sq: (0, bf[i], 0)
    bwd3 = lambda i, bf, bb, fi, la, sq: (1, bb[i], 0)
    st = pl.BlockSpec((2, 1, 4, LANES, LANES), lambda i, bf, bb, fi, la, sq: (0, sq[i], 0, 0, 0))
    one_f, one_b = pl.BlockSpec((C, HALF), fwd), pl.BlockSpec((C, HALF), bwd)
    two_f, two_b = pl.BlockSpec((1, C, HALF), fwd3), pl.BlockSpec((1, C, HALF), bwd3)
    return pl.pallas_call(
        _rwkv_scan_kernel,
        out_shape=[jax.ShapeDtypeStruct((T, HALF), F32), jax.ShapeDtypeStruct((T, HALF), F32),
                   jax.ShapeDtypeStruct((2, N_SEQ, 4, LANES, LANES), F32)],
        grid_spec=pltpu.PrefetchScalarGridSpec(
            num_scalar_prefetch=5, grid=(len(tabs[0]),),
            in_specs=[one_f, one_f, one_f, two_f, two_f, two_f,
                      one_b, one_b, one_b, two_b, two_b, two_b, st],
            out_specs=[one_f, one_b, st],
            scratch_shapes=[pltpu.VMEM((2, 4, LANES, LANES), F32)]),
        compiler_params=_params(("arbitrary",)),
        name="rwkv_scan",
    )(*tabs, r, v, kk, lw, kt, b, r, v, kk, lw, kt, b, s0_bd)


def _rope(x, cos, sin_signed, first16):
    w = x.shape[1]
    partner = jnp.where(first16, pltpu.roll(x, w - 16, 1), pltpu.roll(x, 16, 1))
    return x * cos + partner * sin_signed


def _odd_prep_kernel(zc_ref, zr_ref, cos_ref, sin_ref, qkg_ref, ones_ref, cq_ref, ck_ref, ckraw_ref, rqk_ref):
    ones_bd = ones_ref[...]
    cos = cos_ref[...]
    sin = sin_ref[...]
    lane = lax.broadcasted_iota(I32, (TM, HALF), 1)
    first16 = (lane & 31) < 16
    for idx, (o_ref, raw_ref) in enumerate(((cq_ref, None), (ck_ref, ckraw_ref))):
        x = zc_ref[:, idx * HALF:(idx + 1) * HALF]
        ms = _group_sum(x * x, ones_bd) * (1.0 / HEAD)
        xn = x * lax.rsqrt(ms + 1e-6) * qkg_ref[idx:idx + 1, :]
        if raw_ref is not None:
            raw_ref[...] = xn
        o_ref[...] = _rope(xn, cos, sin, first16)
    rqk = _rope(zr_ref[...], cos, sin, first16)
    rqk_ref[...] = jnp.where(lane < HALF // 2, rqk * (HEAD ** -0.5), rqk)


def _odd_prep(zc, zr, cos_tab, sin_tab, qkg_tiled, ones_bd):
    row = lambda i: (i, 0)
    tab = lambda i: (jnp.where(i < PROMPT_TILES, 0, 1 + (i - PROMPT_TILES) % TILES_PER_SAMPLE), 0)
    one = jax.ShapeDtypeStruct((T, HALF), F32)
    o1 = pl.BlockSpec((TM, HALF), row)
    return pl.pallas_call(
        _odd_prep_kernel,
        out_shape=[one, one, one, one], grid=(NT,),
        in_specs=[pl.BlockSpec((TM, 2 * HALF), row), pl.BlockSpec((TM, HALF), row),
                  pl.BlockSpec((TM, HALF), tab), pl.BlockSpec((TM, HALF), tab),
                  _full((2, HALF)), _full((HALF, HALF))],
        out_specs=[o1, o1, o1, o1],
        compiler_params=_params(("arbitrary",)),
        name="odd_prep",
    )(zc, zr, cos_tab, sin_tab, qkg_tiled, ones_bd)


def _attn_kernel(*refs, has_ctx, one_minus_li):
    if has_ctx:
        q_ref, k_ref, v_ref, kc_ref, vc_ref, lam_ref, sg_ref, _, o_ref = refs
    else:
        q_ref, k_ref, v_ref, lam_ref, sg_ref, _, o_ref = refs
    lam = lam_ref[...]
    lane = lax.broadcasted_iota(I32, (LANES, LANES), 1)
    m0 = lane < HEAD
    scale = HEAD ** -0.5
    for h in range(4):
        cols = slice(h * LANES, (h + 1) * LANES)
        qp = q_ref[:, cols]
        segs = [(k_ref[:, cols], v_ref[:, cols])]
        if has_ctx:
            segs.append((kc_ref[0, :, cols], vc_ref[0, :, cols]))
        outs = []
        for qm in (jnp.where(m0, qp, 0.0), jnp.where(m0, 0.0, qp)):
            qm16 = qm.astype(BF16)
            ss = [lax.dot_general(qm16, ks.astype(BF16), NT_DIMS, preferred_element_type=F32) * scale
                  for ks, _ in segs]
            mx = ss[0].max(axis=-1, keepdims=True)
            for s_ in ss[1:]:
                mx = jnp.maximum(mx, s_.max(axis=-1, keepdims=True))
            ps = [jnp.exp(s_ - mx) for s_ in ss]
            den = ps[0].sum(axis=-1, keepdims=True)
            for p_ in ps[1:]:
                den = den + p_.sum(axis=-1, keepdims=True)
            outs.append([p_ / den for p_ in ps])
        acc = None
        for si, (_, vs) in enumerate(segs):
            amap = outs[0][si] - lam * outs[1][si]
            t = jnp.dot(amap.astype(BF16), vs.astype(BF16), preferred_element_type=F32)
            acc = t if acc is None else acc + t
        nrm = acc * lax.rsqrt(jnp.mean(acc * acc, axis=-1, keepdims=True) + 1e-6) * sg_ref[...]
        o_ref[:, cols] = nrm * one_minus_li


def _attn(cq, ck, zc, row0, n_seq, seq_len, lam, subln_g, one_minus_li, prev, ctx_k=None, ctx_v=None):
    nq = seq_len // LANES
    qb0 = row0 // LANES
    sb0 = row0 // seq_len
    in_specs = [pl.BlockSpec((LANES, HALF), lambda s, q: (qb0 + s * nq + q, 0)),
                pl.BlockSpec((seq_len, HALF), lambda s, q: (sb0 + s, 0)),
                pl.BlockSpec((seq_len, HALF), lambda s, q: (sb0 + s, 2))]
    args = [cq, ck, zc]
    if ctx_k is not None:
        in_specs += [pl.BlockSpec((1, PAST, HALF), lambda s, q: (s, 0, 0))] * 2
        args += [ctx_k, ctx_v]
    in_specs += [_full((1, 1)), _full((1, LANES))]
    args += [lam.reshape(1, 1), subln_g.reshape(1, LANES)]
    in_specs.append(pl.BlockSpec(memory_space=pl.ANY))
    args.append(prev)
    aliases = {len(args) - 1: 0}
    return pl.pallas_call(
        functools.partial(_attn_kernel, has_ctx=ctx_k is not None, one_minus_li=one_minus_li),
        out_shape=jax.ShapeDtypeStruct((T, HALF), F32),
        grid=(n_seq, nq), in_specs=in_specs,
        out_specs=pl.BlockSpec((LANES, HALF), lambda s, q: (qb0 + s * nq + q, 0)),
        input_output_aliases=aliases,
        compiler_params=_params(("arbitrary", "arbitrary"), 48),
        name="diff_attn",
    )(*args)


_LOG_GAMMA = tuple(tuple(float(np.log1p(-np.exp2(-np.float32(e)), dtype=np.float32)) for e in es)
                   for es in RET_EXP)


def _ret_chunk(rev, qk_ref, v_ref, s_ref, o_ref):
    C = RET_CHUNK
    ii = lax.broadcasted_iota(I32, (C, C), 0)
    jj = lax.broadcasted_iota(I32, (C, C), 1)
    mask = (jj > ii) if rev else (jj <= ii)
    dist = jnp.where(mask, (jj - ii) if rev else (ii - jj), 0).astype(F32)
    ri = lax.broadcasted_iota(I32, (C, 1), 0)
    kpow = (ri if rev else (C - 1 - ri)).astype(F32)
    qpow = ((C - ri) if rev else (ri + 1)).astype(F32)
    lane = lax.broadcasted_iota(I32, (C, LANES), 1)
    for h in range(4):
        lg = _LOG_GAMMA[1 if rev else 0][h]
        p = h // 2
        hm = (lane < HEAD) if h % 2 == 0 else (lane >= HEAD)
        qp = jnp.where(hm, qk_ref[:, p * LANES:(p + 1) * LANES], 0.0)
        kp = jnp.where(hm, qk_ref[:, HALF // 2 + p * LANES:HALF // 2 + (p + 1) * LANES], 0.0)
        vh = v_ref[:, h * LANES:(h + 1) * LANES].astype(BF16)
        decay = jnp.where(mask, jnp.exp(lg * dist), 0.0)
        sc = lax.dot_general(qp.astype(BF16), kp.astype(BF16), NT_DIMS, preferred_element_type=F32) * decay
        o_intra = jnp.dot(sc.astype(BF16), vh, preferred_element_type=F32)
        s_prev = s_ref[h]
        o_cross = jnp.dot((qp * jnp.exp(lg * qpow)).astype(BF16), s_prev.astype(BF16),
                          preferred_element_type=F32)
        o_ref[:, h * LANES:(h + 1) * LANES] = o_intra + o_cross
        kv = lax.dot_general((kp * jnp.exp(lg * kpow)).astype(BF16), vh, TN, preferred_element_type=F32)
        s_ref[h] = math.exp(lg * C) * s_prev + kv


def _ret_kernel(bf_ref, bb_ref, first_ref, last_ref, seq_ref,
                qkf_ref, vf_ref, qkb_ref, vb_ref, s0_ref, of_ref, ob_ref, sfin_ref, s_ref):
    step = pl.program_id(0)

    @pl.when(first_ref[step] == 1)
    def _():
        s_ref[...] = s0_ref[:, 0]

    _ret_chunk(False, qkf_ref, vf_ref, s_ref.at[0], of_ref)
    _ret_chunk(True, qkb_ref, vb_ref, s_ref.at[1], ob_ref)

    @pl.when(last_ref[step] == 1)
    def _():
        sfin_ref[:, 0] = s_ref[...]


def _retention(rqk, zr, s0):
    C = RET_CHUNK
    tabs = _seq_tables(C)
    st = pl.BlockSpec((2, 1, 4, LANES, LANES), lambda i, bf, bb, fi, la, sq: (0, sq[i], 0, 0, 0))
    spec = lambda use_b, col: pl.BlockSpec(
        (C, HALF), lambda i, bf, bb, fi, la, sq: ((bb if use_b else bf)[i], col))
    return pl.pallas_call(
        _ret_kernel,
        out_shape=[jax.ShapeDtypeStruct((T, HALF), F32), jax.ShapeDtypeStruct((T, HALF), F32),
                   jax.ShapeDtypeStruct((2, N_SEQ, 4, LANES, LANES), F32)],
        grid_spec=pltpu.PrefetchScalarGridSpec(
            num_scalar_prefetch=5, grid=(len(tabs[0]),),
            in_specs=[spec(False, 0), spec(False, 1), spec(True, 0), spec(True, 1), st],
            out_specs=[spec(False, 0), spec(True, 0), st],
            scratch_shapes=[pltpu.VMEM((2, 4, LANES, LANES), F32)]),
        compiler_params=_params(("arbitrary",)),
        name="retention",
    )(*tabs, rqk, zr, rqk, zr, s0)


def _out_kernel(*refs, even):
    if even:
        (a_ref, yf_ref, yb_ref, bonus_ref, g_ref, gng_ref, gnb_ref, ones_ref,
         x_ref, mod_ref, ng_ref, wo_ref, rw_ref, rb_ref,
         y_ref, xp_ref, ti_ref, tg_ref, rk_ref, cnt_ref, run_ref) = refs
        ones_bd = ones_ref[...]
        ys = yf_ref[...] + yb_ref[...]
        mu = _group_sum(ys, ones_bd) * (1.0 / HEAD)
        dv = ys - mu
        var = _group_sum(dv * dv, ones_bd) * (1.0 / HEAD)
        yn = dv * lax.rsqrt(var + RWKV_GN_EPS) * gng_ref[...] + gnb_ref[...]
        left = a_ref[...]
        right = (yn + bonus_ref[...]) * g_ref[...]
    else:
        (c_ref, of_ref, ob_ref, rg_ref, gng_ref,
         x_ref, mod_ref, ng_ref, wo_ref, rw_ref, rb_ref,
         y_ref, xp_ref, ti_ref, tg_ref, rk_ref, cnt_ref, run_ref) = refs
        left = c_ref[...]
        rg = rg_ref[...]
        gate = rg * jax.nn.sigmoid(rg)
        os_ = of_ref[...] + ob_ref[...]
        parts = []
        for h in range(4):
            oh = os_[:, h * LANES:(h + 1) * LANES]
            mu = jnp.mean(oh, axis=-1, keepdims=True)
            dv = oh - mu
            var = jnp.mean(dv * dv, axis=-1, keepdims=True)
            parts.append(dv * lax.rsqrt(var + 1e-5))
        right = gate * (jnp.concatenate(parts, axis=1) * gng_ref[...])
    mod = mod_ref[0]
    o = (jnp.dot(left.astype(BF16), wo_ref[0:HALF, :], preferred_element_type=F32)
         + jnp.dot(right.astype(BF16), wo_ref[HALF:2 * HALF, :], preferred_element_type=F32))
    y = x_ref[...] + mod[:, 2 * D:3 * D] * o
    y_ref[...] = y
    yn2 = y * lax.rsqrt(jnp.mean(y * y, axis=-1, keepdims=True) + 1e-6) * ng_ref[...]
    t = yn2 * (1.0 + mod[:, 4 * D:5 * D]) + mod[:, 3 * D:4 * D]
    xp_ref[...] = t
    logits = jnp.dot(t, rw_ref[...], precision=HIGHEST, preferred_element_type=F32) + rb_ref[...]
    lane = lax.broadcasted_iota(I32, (TM, LANES), 1)
    neg = jnp.float32(-jnp.inf)
    lg = jnp.where(lane < N_EXPERTS, logits, neg)
    vals, hits = [], []
    for _ in range(TOP_K):
        m = jnp.max(lg, axis=-1, keepdims=True)
        ix = jnp.min(jnp.where(lg == m, lane, LANES), axis=-1, keepdims=True)
        hit = lane == ix
        vals.append(m)
        hits.append((ix, hit))
        lg = jnp.where(hit, neg, lg)
    es = [jnp.exp(vv - vals[0]) for vv in vals]
    den = es[0] + es[1] + es[2] + es[3]

    @pl.when(pl.program_id(0) == 0)
    def _():
        run_ref[...] = jnp.zeros_like(run_ref)

    member = jnp.zeros((TM, LANES), F32)
    for _, hit in hits:
        member = member + jnp.where(hit, 1.0, 0.0)
    ri = lax.broadcasted_iota(I32, (TM, TM), 0)
    ci = lax.broadcasted_iota(I32, (TM, TM), 1)
    before = jnp.where(ci < ri, 1.0, 0.0).astype(BF16)
    seen = run_ref[...] + jnp.dot(before, member.astype(BF16), preferred_element_type=F32)
    ti = jnp.zeros((TM, LANES), I32)
    tg = jnp.zeros((TM, LANES), F32)
    rk = jnp.zeros((TM, LANES), F32)
    for kk, (ix, hit) in enumerate(hits):
        ti = jnp.where(lane == kk, ix, ti)
        tg = jnp.where(lane == kk, es[kk] / den, tg)
        rk = jnp.where(lane == kk, jnp.sum(jnp.where(hit, seen, 0.0), axis=-1, keepdims=True), rk)
    ti_ref[...] = ti
    tg_ref[...] = tg
    rk_ref[...] = rk.astype(I32)
    run_ref[...] = run_ref[...] + jnp.sum(member, axis=0, keepdims=True)
    cnt_ref[...] = run_ref[...]


def _out_proj(even, mix_args, mix_specs, x, mod, norm_g, w_out_bf16, rw_pad, rb_pad):
    row = lambda i: (i, 0)
    modspec = pl.BlockSpec((1, 1, N_MOD * D), lambda i: (_group(i), 0, 0))
    in_specs = list(mix_specs) + [pl.BlockSpec((TM, D), row), modspec, _full((1, D)), _full((D, D)),
                                  _full((D, LANES)), _full((1, LANES))]
    args = list(mix_args) + [x, mod, norm_g.reshape(1, D), w_out_bf16, rw_pad, rb_pad]
    lane_i = jax.ShapeDtypeStruct((T, LANES), I32)
    lane_spec = pl.BlockSpec((TM, LANES), row)
    return pl.pallas_call(
        functools.partial(_out_kernel, even=even),
        out_shape=[jax.ShapeDtypeStruct((T, D), F32), jax.ShapeDtypeStruct((T, D), F32),
                   lane_i, jax.ShapeDtypeStruct((T, LANES), F32), lane_i,
                   jax.ShapeDtypeStruct((1, LANES), F32)],
        grid=(NT,), in_specs=in_specs,
        out_specs=[pl.BlockSpec((TM, D), row), pl.BlockSpec((TM, D), row),
                   lane_spec, lane_spec, lane_spec, _full((1, LANES))],
        scratch_shapes=[pltpu.VMEM((1, LANES), F32)],
        compiler_params=_params(("arbitrary",), 48),
        name="out_proj",
    )(*args)


def _route_kernel(cnt_ref, ti_ref, rk_ref, dest_ref, te_ref, nt_ref):
    cnt = cnt_ref[...].astype(I32)
    ntile = lax.shift_right_logical(cnt + (TM - 1), 8)
    ei = lax.broadcasted_iota(I32, (LANES, LANES), 0)
    ej = lax.broadcasted_iota(I32, (LANES, LANES), 1)
    upto = jnp.where(ei <= ej, 1.0, 0.0).astype(BF16)
    ntile_f = jnp.broadcast_to(ntile.astype(F32), (8, LANES))
    tile_end = jnp.dot(ntile_f.astype(BF16), upto, preferred_element_type=F32)[0:1, :]
    row_start = (tile_end - ntile.astype(F32)) * float(TM)
    lane = lax.broadcasted_iota(I32, (TM, LANES), 1)
    ti = ti_ref[...]
    rk = rk_ref[...]
    dest = jnp.zeros((TM, LANES), F32)
    for k in range(TOP_K):
        hit = lane == ti[:, k:k + 1]
        start = jnp.sum(jnp.where(hit, row_start, 0.0), axis=-1, keepdims=True)
        dest = jnp.where(lane == k, start + rk[:, k:k + 1].astype(F32), dest)
    dest_ref[...] = dest.astype(I32)

    @pl.when(pl.program_id(0) == 0)
    def _():
        lane1 = lax.broadcasted_iota(I32, (1, LANES), 1)
        n_tiles = jnp.max(tile_end, axis=-1, keepdims=True)
        last_e = jnp.max(jnp.where(cnt > 0, lane1, 0), axis=-1, keepdims=True)
        tile = lax.broadcasted_iota(I32, (TM, 1), 0).astype(F32)
        te = jnp.sum(jnp.where(tile_end <= tile, 1, 0), axis=-1, keepdims=True)
        te = jnp.where(tile < n_tiles, te, last_e)
        te_ref[...] = jnp.broadcast_to(te, (TM, LANES)).astype(I32)
        first_row = lax.broadcasted_iota(I32, (8, LANES), 0) == 0
        nt_ref[...] = jnp.where(first_row, n_tiles, tile_end).astype(I32)


def _route(cnt, ti, rk):
    row = lambda i: (i, 0)
    return pl.pallas_call(
        _route_kernel,
        out_shape=[jax.ShapeDtypeStruct((T, LANES), I32), jax.ShapeDtypeStruct((TM, LANES), I32),
                   jax.ShapeDtypeStruct((8, LANES), I32)],
        grid=(NT,),
        in_specs=[_full((1, LANES)), pl.BlockSpec((TM, LANES), row), pl.BlockSpec((TM, LANES), row)],
        out_specs=[pl.BlockSpec((TM, LANES), row), _full((TM, LANES)), _full((8, LANES))],
        compiler_params=_params(("arbitrary",)),
        name="moe_route",
    )(cnt, ti, rk)


def _row_copy(src_ref, src_row, dst_ref, dst_row, sem):
    return pltpu.make_async_copy(src_ref.at[pl.ds(src_row, 1)], dst_ref.at[pl.ds(dst_row, 1)], sem)


def _wait_tiles(n, src_ref, dst_ref, sem):
    for _ in range(n):
        pltpu.make_async_copy(src_ref, dst_ref, sem).wait()


def _dispatch_kernel(dest_ref, tend_ref, x_ref, xs_ref, zero_ref, sem):
    i = pl.program_id(0)

    @pl.when(i == 0)
    def _():
        zero_ref[...] = jnp.zeros_like(zero_ref)

        def last_tile(e, fn):
            end = tend_ref[e]
            begin = tend_ref[e - 1] if e > 0 else 0

            @pl.when(end > begin)
            def _():
                fn(pltpu.make_async_copy(zero_ref, xs_ref.at[pl.ds((end - 1) * TM, TM)], sem))

        def unused_tile(j):
            return pltpu.make_async_copy(zero_ref, xs_ref.at[pl.ds(j * TM, TM)], sem)

        def start_unused(j, carry):
            unused_tile(j).start()
            return carry

        def wait_unused(j, carry):
            unused_tile(j).wait()
            return carry

        n_used = tend_ref[N_EXPERTS - 1]
        for e in range(N_EXPERTS):
            last_tile(e, lambda c: c.start())
        lax.fori_loop(n_used, MOE_TILES, start_unused, 0)
        for e in range(N_EXPERTS):
            last_tile(e, lambda c: c.wait())
        lax.fori_loop(n_used, MOE_TILES, wait_unused, 0)

    base = i * (TM * TOP_K)

    def start(r, carry):
        for k in range(TOP_K):
            _row_copy(x_ref, r, xs_ref, dest_ref[base + r * TOP_K + k], sem).start()
        return carry

    lax.fori_loop(0, TM, start, 0, unroll=4)
    _wait_tiles(TOP_K, x_ref, xs_ref.at[pl.ds(0, TM)], sem)


def _dispatch(dest_flat, tile_end, xt):
    return pl.pallas_call(
        _dispatch_kernel,
        out_shape=jax.ShapeDtypeStruct((R_PAD, D), F32),
        grid_spec=pltpu.PrefetchScalarGridSpec(
            num_scalar_prefetch=2, grid=(NT,),
            in_specs=[pl.BlockSpec((TM, D), lambda i, d, te: (i, 0))],
            out_specs=pl.BlockSpec(memory_space=pl.ANY),
            scratch_shapes=[pltpu.VMEM((TM, D), F32), pltpu.SemaphoreType.DMA(())]),
        compiler_params=_params(("arbitrary",)),
        name="moe_dispatch",
    )(dest_flat, tile_end, xt)


def _moe_up_kernel(te_ref, nt_ref, xs_ref, w_ref, b_ref, act_ref):
    i = pl.program_id(0)

    @pl.when(i < nt_ref[0])
    def _():
        x16 = xs_ref[...].astype(BF16)
        for gc in (slice(0, HALF), slice(HALF, D)):
            uc = slice(D + gc.start, D + gc.stop)
            g = jnp.dot(x16, w_ref[0, :, gc].astype(BF16), preferred_element_type=F32) + b_ref[0, :, gc]
            u = jnp.dot(x16, w_ref[0, :, uc].astype(BF16), preferred_element_type=F32) + b_ref[0, :, uc]
            gt = jnp.minimum(g, SWIGLU_LIMIT)
            up = jnp.clip(u, -SWIGLU_LIMIT, SWIGLU_LIMIT)
            act_ref[:, gc] = ((up + 1.0) * gt * jax.nn.sigmoid(SWIGLU_ALPHA * gt)).astype(BF16)

    @pl.when(i >= nt_ref[0])
    def _():
        act_ref[...] = jnp.zeros_like(act_ref)


def _tile_clamped(i, te, nt):
    return (jnp.minimum(i, jnp.maximum(nt[0] - 1, 0)), 0)


def _moe_up(te, n_tiles, xs, w_gu, b_gu):
    return pl.pallas_call(
        _moe_up_kernel,
        out_shape=jax.ShapeDtypeStruct((R_PAD, D), BF16),
        grid_spec=pltpu.PrefetchScalarGridSpec(
            num_scalar_prefetch=2, grid=(MOE_TILES,),
            in_specs=[pl.BlockSpec((TM, D), _tile_clamped),
                      pl.BlockSpec((1, D, 2 * D), lambda i, te, nt: (te[i], 0, 0)),
                      pl.BlockSpec((1, 1, 2 * D), lambda i, te, nt: (te[i], 0, 0))],
            out_specs=pl.BlockSpec((TM, D), lambda i, te, nt: (i, 0))),
        compiler_params=_params(("arbitrary",), 48),
        name="moe_up",
    )(te, n_tiles, xs, w_gu, b_gu)


def _moe_down_kernel(te_ref, nt_ref, act_ref, w_ref, b_ref, y_ref):
    i = pl.program_id(0)

    @pl.when(i < nt_ref[0])
    def _():
        y_ref[...] = jnp.dot(act_ref[...], w_ref[0].astype(BF16), preferred_element_type=F32) + b_ref[0]

    @pl.when(i >= nt_ref[0])
    def _():
        y_ref[...] = jnp.zeros_like(y_ref)


def _moe_down(te, n_tiles, act, w_dn, b_dn):
    return pl.pallas_call(
        _moe_down_kernel,
        out_shape=jax.ShapeDtypeStruct((R_PAD, D), F32),
        grid_spec=pltpu.PrefetchScalarGridSpec(
            num_scalar_prefetch=2, grid=(MOE_TILES,),
            in_specs=[pl.BlockSpec((TM, D), _tile_clamped),
                      pl.BlockSpec((1, D, D), lambda i, te, nt: (te[i], 0, 0)),
                      pl.BlockSpec((1, 1, D), lambda i, te, nt: (te[i], 0, 0))],
            out_specs=pl.BlockSpec((TM, D), lambda i, te, nt: (i, 0))),
        compiler_params=_params(("arbitrary",), 40),
        name="moe_down",
    )(te, n_tiles, act, w_dn, b_dn)


def _combine_kernel(dest_ref, x_ref, tg_ref, mod_ref, ys_ref, o_ref, buf_ref, sem):
    i = pl.program_id(0)
    slot = i % 2

    def gather(tile, b):
        base = tile * (TM * TOP_K)

        def body(r, carry):
            for k in range(TOP_K):
                _row_copy(ys_ref, dest_ref[base + r * TOP_K + k], buf_ref.at[b, k], r, sem.at[b]).start()
            return carry

        lax.fori_loop(0, TM, body, 0, unroll=4)

    @pl.when(i == 0)
    def _():
        gather(0, 0)

    @pl.when(i + 1 < NT)
    def _():
        gather(i + 1, 1 - slot)

    _wait_tiles(TOP_K, ys_ref.at[pl.ds(0, TM)], buf_ref.at[slot, 0], sem.at[slot])
    tg = tg_ref[...]
    f = tg[:, 0:1] * buf_ref[slot, 0]
    for k in range(1, TOP_K):
        f = f + tg[:, k:k + 1] * buf_ref[slot, k]
    o_ref[...] = x_ref[...] + mod_ref[0][:, 5 * D:6 * D] * f


def _combine(dest_flat, x, tg, mod, ys):
    row = lambda i, d: (i, 0)
    return pl.pallas_call(
        _combine_kernel,
        out_shape=jax.ShapeDtypeStruct((T, D), F32),
        grid_spec=pltpu.PrefetchScalarGridSpec(
            num_scalar_prefetch=1, grid=(NT,),
            in_specs=[pl.BlockSpec((TM, D), row), pl.BlockSpec((TM, LANES), row),
                      pl.BlockSpec((1, 1, N_MOD * D), lambda i, d: (_group(i), 0, 0)),
                      pl.BlockSpec(memory_space=pl.ANY)],
            out_specs=pl.BlockSpec((TM, D), row),
            scratch_shapes=[pltpu.VMEM((2, TOP_K, TM, D), F32), pltpu.SemaphoreType.DMA((2,))]),
        compiler_params=_params(("arbitrary",), 40),
        name="moe_combine",
    )(dest_flat, x, tg, mod, ys)


def _moe(layer, y, xt, ti, tg, rk, cnt, mod, w_gu, b_gu, w_dn, b_dn):
    dest, te, nt = _route(cnt, ti, rk)
    dest_flat = dest[:, :TOP_K].reshape(-1)
    te = te[:MOE_TILES, 0] + layer * N_EXPERTS
    n_tiles = nt[0, :1]
    xs = _dispatch(dest_flat, nt[1, :N_EXPERTS], xt)
    n_all = w_gu.shape[0] * N_EXPERTS
    act = _moe_up(te, n_tiles, xs, w_gu.reshape(n_all, D, 2 * D), b_gu.reshape(n_all, 1, 2 * D))
    ys = _moe_down(te, n_tiles, act, w_dn.reshape(n_all, D, D), b_dn.reshape(n_all, 1, D))
    return _combine(dest_flat, y, tg, mod, ys)


def _ones_blockdiag():
    idx = np.arange(HALF) // HEAD
    return jnp.asarray((idx[:, None] == idx[None, :]).astype(np.float32), dtype=BF16)


def _rope_tables():
    pos = jnp.arange(L_SAMPLE)
    rowp = (pos // 64).astype(F32)
    colp = (pos % 64).astype(F32)
    nf = HEAD // 4
    inv = jnp.power(10000.0, -jnp.arange(nf, dtype=F32) / nf)
    ar = rowp[:, None] * inv[None, :]
    ac = colp[:, None] * inv[None, :]
    cos64 = jnp.concatenate([jnp.cos(ar), jnp.cos(ar), jnp.cos(ac), jnp.cos(ac)], axis=1)
    sin64 = jnp.concatenate([-jnp.sin(ar), jnp.sin(ar), -jnp.sin(ac), jnp.sin(ac)], axis=1)
    cos = jnp.tile(cos64, (1, HALF // HEAD))
    sin = jnp.tile(sin64, (1, HALF // HEAD))
    ident = jnp.ones((TM, HALF), F32)
    return (jnp.concatenate([ident, cos], axis=0), jnp.concatenate([jnp.zeros((TM, HALF), F32), sin], axis=0))


def _bd_pairs(s):
    lead = s.shape[:-3]
    s = s.reshape(lead + (4, 2, HEAD, HEAD))
    z = jnp.zeros_like(s[..., 0, :, :])
    top = jnp.concatenate([s[..., 0, :, :], z], axis=-1)
    bot = jnp.concatenate([z, s[..., 1, :, :]], axis=-1)
    return jnp.concatenate([top, bot], axis=-2)


def _bd_unpairs(s):
    a = s[..., 0:HEAD, 0:HEAD]
    b = s[..., HEAD:, HEAD:]
    out = jnp.stack([a, b], axis=-3)
    return out.reshape(s.shape[:-3] + (8, HEAD, HEAD))


def kernel(x_prompt, x_sample, state_rwkv, cache_k_diff, cache_v_diff, state_retention, c, c_ctx, norm_g, ada_w, ada_b, e_w_in, e_w_out, sgu_ln_g, sgu_w, sgu_b, rw_mu, rw_w0, rw_w_up, rw_a0, rw_a_up, rw_g_up, rw_k_k, rw_k_a, rw_r_k, rw_gn_g, rw_gn_b, o_w_in, o_w_out, da_qk_g, da_lam, da_subln_g, ret_gn_g, router_w, router_b, ex_w_gu, ex_b_gu, ex_w_dn, ex_b_dn):
    x = jnp.concatenate([x_prompt.reshape(T_PROMPT, D), x_sample.reshape(T_SAMPLE, D)], axis=0)
    cvec8 = jnp.concatenate([c_ctx[None, :], c, jnp.zeros((3, D), F32)], axis=0)
    mods = _adaln(cvec8, ada_w, ada_b)
    mod0 = mods[0].reshape(8, 1, N_MOD * D)
    mod1 = mods[1].reshape(8, 1, N_MOD * D)
    ones_bd = _ones_blockdiag()
    rw_pad = jnp.pad(router_w, ((0, 0), (0, 0), (0, LANES - N_EXPERTS)))
    rb_pad = jnp.pad(router_b, ((0, 0), (0, LANES - N_EXPERTS))).reshape(2, 1, LANES)
    row = lambda i: (i, 0)
    half = pl.BlockSpec((TM, HALF), row)

    za, zb = _in_proj(x, norm_g[0, 0], mod0, e_w_in[0].astype(BF16), (2 * HALF, B_COLS))
    bs_full = jnp.repeat(sgu_b[0].T, HEAD, axis=1)
    a_out = _sgu(za, sgu_ln_g[0], sgu_w[0].astype(BF16), bs_full)
    zpad = jnp.zeros((2, HEAD, HALF), F32)
    wup_pad = jnp.concatenate([rw_w_up[0], zpad], axis=1).astype(BF16)
    aup_pad = jnp.concatenate([zpad, rw_a_up[0]], axis=1).astype(BF16)
    r, v, kkn, bonus, g, lw, kt, b = _rwkv_prep(zb, rw_mu[0], rw_k_k[0], rw_k_a[0], rw_r_k[0], rw_w0[0], rw_a0[0],
                                                wup_pad, aup_pad, rw_g_up[0].astype(BF16), ones_bd)
    s0_sample = _bd_pairs(jnp.moveaxis(state_rwkv[:, 0], 1, 0))
    s0_rw = jnp.concatenate([jnp.zeros((2, N_PROMPT, 4, LANES, LANES), F32), s0_sample], axis=1)
    yf_rw, yb_rw, sfin_rw = _rwkv_scan(r, v, kkn, lw, kt, b, s0_rw)
    new_rwkv = jnp.moveaxis(_bd_unpairs(sfin_rw[:, :N_PROMPT]), 0, 1)[:, None]
    y0, xp0, ti0, tg0, rk0, cnt0 = _out_proj(
        True,
        [a_out, yf_rw, yb_rw, bonus, g, rw_gn_g[0].reshape(1, HALF), rw_gn_b[0].reshape(1, HALF), ones_bd],
        [half, half, half, half, half, _full((1, HALF)), _full((1, HALF)), _full((HALF, HALF))],
        x, mod0, norm_g[0, 1], e_w_out[0].astype(BF16), rw_pad[0], rb_pad[0])
    x1 = _moe(0, y0, xp0, ti0, tg0, rk0, cnt0, mod0, ex_w_gu, ex_b_gu, ex_w_dn, ex_b_dn)

    zc, zr = _in_proj(x1, norm_g[1, 0], mod1, o_w_in[0].astype(BF16), (3 * HALF, 3 * HALF))
    cos_tab, sin_tab = _rope_tables()
    qkg = jnp.tile(da_qk_g[0], (1, HALF // HEAD))
    cq, ck, ck_raw, rqk = _odd_prep(zc, zr, cos_tab, sin_tab, qkg, ones_bd)
    lambda_init = 0.8 - 0.6 * math.exp(-0.3 * 1)
    lv = da_lam[0]
    lam = jnp.exp(jnp.sum(lv[0] * lv[1])) - jnp.exp(jnp.sum(lv[2] * lv[3])) + lambda_init
    c_out = _attn(cq, ck, zc, 0, N_PROMPT, L_PROMPT, lam, da_subln_g[0], 1.0 - lambda_init,
                  jnp.zeros((T, HALF), F32))
    ctx_k = cache_k_diff[:, 0].reshape(N_SAMPLE, PAST, HALF)
    ctx_v = cache_v_diff[:, 0].reshape(N_SAMPLE, PAST, HALF)
    c_out = _attn(cq, ck, zc, T_PROMPT, N_SAMPLE, L_SAMPLE, lam, da_subln_g[0], 1.0 - lambda_init,
                  c_out, ctx_k, ctx_v)
    sr = jnp.moveaxis(state_retention[:, 0], 1, 0)
    zr0 = jnp.zeros_like(sr)
    s0_sample = jnp.stack([jnp.concatenate([sr[:, :, 0], zr0[:, :, 0]], axis=-2),
                           jnp.concatenate([zr0[:, :, 1], sr[:, :, 1]], axis=-2),
                           jnp.concatenate([sr[:, :, 2], zr0[:, :, 2]], axis=-2),
                           jnp.concatenate([zr0[:, :, 3], sr[:, :, 3]], axis=-2)], axis=2)
    s0_ret = jnp.concatenate([jnp.zeros((2, N_PROMPT, 4, LANES, LANES), F32), s0_sample], axis=1)
    of_ret, ob_ret, rfin = _retention(rqk, zr, s0_ret)
    rfin_p = rfin[:, :N_PROMPT]
    new_ret = jnp.stack([rfin_p[:, :, 0, 0:HEAD], rfin_p[:, :, 1, HEAD:], rfin_p[:, :, 2, 0:HEAD],
                         rfin_p[:, :, 3, HEAD:]], axis=2)
    new_ret = jnp.moveaxis(new_ret, 0, 1)[:, None]
    y1, xp1, ti1, tg1, rk1, cnt1 = _out_proj(
        False,
        [c_out, of_ret, ob_ret, zr, ret_gn_g[0].reshape(1, HALF)],
        [half, half, half, pl.BlockSpec((TM, HALF), lambda i: (i, 2)), _full((1, HALF))],
        x1, mod1, norm_g[1, 1], o_w_out[0].astype(BF16), rw_pad[1], rb_pad[1])
    y_fin = _moe(1, y1, xp1, ti1, tg1, rk1, cnt1, mod1, ex_w_gu, ex_b_gu, ex_w_dn, ex_b_dn)

    new_k = ck_raw[:T_PROMPT].reshape(N_PROMPT, 1, L_PROMPT, 4, LANES)
    new_v = zc[:T_PROMPT, 2 * HALF:3 * HALF].reshape(N_PROMPT, 1, L_PROMPT, 4, LANES)
    return (y_fin[:T_PROMPT].reshape(N_PROMPT, L_PROMPT, D), y_fin[T_PROMPT:].reshape(N_SAMPLE, L_SAMPLE, D),
            new_rwkv, new_k, new_v, new_ret)
```

```python
import functools
import math

import numpy as np
import jax
import jax.numpy as jnp
from jax import lax
from jax.experimental import pallas as pl
from jax.experimental.pallas import tpu as pltpu

F32 = jnp.float32
BF16 = jnp.bfloat16
I32 = jnp.int32
HIGHEST = lax.Precision.HIGHEST

D = 1024
N_PROMPT, L_PROMPT = 16, 256
N_SAMPLE, L_SAMPLE = 4, 1024
N_SEQ = N_PROMPT + N_SAMPLE
PAST = 256
T_PROMPT = N_PROMPT * L_PROMPT
T_SAMPLE = N_SAMPLE * L_SAMPLE
T = T_PROMPT + T_SAMPLE
TM = 256
NT = T // TM
PROMPT_TILES = T_PROMPT // TM
TILES_PER_SAMPLE = L_SAMPLE // TM
N_MOD = 6
HALF = 512
B_COLS = 1792
HEAD = 64
W_DECAY_SCALE = math.exp(-0.5)
RWKV_GN_EPS = 64e-5
RW_CHUNK = 64
RET_CHUNK = 128
RET_EXP = ((5.0, 7.0, 9.0, 11.0), (6.0, 8.0, 10.0, 12.0))
N_EXPERTS = 32
TOP_K = 4
SWIGLU_LIMIT = 7.0
SWIGLU_ALPHA = 1.702
N_ASSIGN = T * TOP_K
TMX = 512
MOE_TILES = N_ASSIGN // TMX + N_EXPERTS
R_PAD = MOE_TILES * TMX
LANES = 128

NN = (((1,), (0,)), ((), ()))
NT_DIMS = (((1,), (1,)), ((), ()))
TN = (((0,), (0,)), ((), ()))


def _group(i):
    return jnp.where(i < PROMPT_TILES, 0, 1 + (i - PROMPT_TILES) // TILES_PER_SAMPLE)


def _mm(a, b, dims=NN, passes=1):
    dg = functools.partial(lax.dot_general, dimension_numbers=dims, preferred_element_type=F32)
    if passes == 1:
        return dg(a.astype(BF16), b.astype(BF16))
    a = a.astype(F32)
    b = b.astype(F32)
    ah = a.astype(BF16)
    al = (a - ah.astype(F32)).astype(BF16)
    bh = b.astype(BF16)
    bl = (b - bh.astype(F32)).astype(BF16)
    if dims[0][0] == (1,):
        m = a.shape[0]
        both = dg(jnp.concatenate([ah, al], axis=0), bh)
        return both[0:m] + (dg(ah, bl) + both[m:2 * m])
    return dg(ah, bh) + (dg(ah, bl) + dg(al, bh))


def _group_sum(x, ones_bd):
    xh = x.astype(BF16)
    xl = (x - xh.astype(F32)).astype(BF16)
    return (jnp.dot(xh, ones_bd, preferred_element_type=F32)
            + jnp.dot(xl, ones_bd, preferred_element_type=F32))


def _full(shape):
    nd = len(shape)
    return pl.BlockSpec(shape, lambda *_: (0,) * nd)


def _params(sem, vmem_mb=None):
    kw = {}
    if vmem_mb is not None:
        kw["vmem_limit_bytes"] = vmem_mb * 1024 * 1024
    return pltpu.CompilerParams(dimension_semantics=sem, **kw)


def _seq_tables(chunk):
    blk_f, blk_b, first, last, seq = [], [], [], [], []
    row = 0
    for s in range(N_SEQ):
        n = (L_PROMPT if s < N_PROMPT else L_SAMPLE) // chunk
        base = row // chunk
        for j in range(n):
            blk_f.append(base + j)
            blk_b.append(base + n - 1 - j)
            first.append(int(j == 0))
            last.append(int(j == n - 1))
            seq.append(s)
        row += n * chunk
    return tuple(np.asarray(a, np.int32) for a in (blk_f, blk_b, first, last, seq))


def _adaln_kernel(c_ref, w_ref, b_ref, o_ref):
    c = c_ref[...]
    s = c * jax.nn.sigmoid(c)
    o_ref[0] = jnp.dot(s, w_ref[0], precision=HIGHEST, preferred_element_type=F32) + b_ref[0]


def _adaln(cvec8, ada_w, ada_b):
    depth, _, n = ada_w.shape
    bn = 1536
    return pl.pallas_call(
        _adaln_kernel,
        out_shape=jax.ShapeDtypeStruct((depth, 8, n), F32),
        grid=(depth, n // bn),
        in_specs=[pl.BlockSpec((8, D), lambda l, j: (0, 0)),
                  pl.BlockSpec((1, D, bn), lambda l, j: (l, 0, j)),
                  pl.BlockSpec((1, 1, bn), lambda l, j: (l, 0, j))],
        out_specs=pl.BlockSpec((1, 8, bn), lambda l, j: (l, 0, j)),
        compiler_params=_params(("arbitrary", "arbitrary"), 40),
        name="adaln",
    )(cvec8, ada_w, ada_b.reshape(depth, 1, n))


def _in_kernel(x_ref, g_ref, mod_ref, w_ref, *outs, splits):
    x = x_ref[...]
    mod = mod_ref[0]
    y = x * lax.rsqrt(jnp.mean(x * x, axis=-1, keepdims=True) + 1e-6) * g_ref[...]
    h = (y * (1.0 + mod[:, D:2 * D]) + mod[:, 0:D]).astype(BF16)
    off = 0
    for o_ref, n in zip(outs, splits):
        o_ref[...] = jnp.dot(h, w_ref[:, off:off + n], preferred_element_type=F32)
        off += n


def _in_proj(x, g, mod, w_bf16, splits):
    n = w_bf16.shape[1]
    row = lambda i: (i, 0)
    return pl.pallas_call(
        functools.partial(_in_kernel, splits=splits),
        out_shape=[jax.ShapeDtypeStruct((T, s), F32) for s in splits],
        grid=(NT,),
        in_specs=[pl.BlockSpec((TM, D), row), _full((1, D)),
                  pl.BlockSpec((1, 1, N_MOD * D), lambda i: (_group(i), 0, 0)), _full((D, n))],
        out_specs=[pl.BlockSpec((TM, s), row) for s in splits],
        compiler_params=_params(("arbitrary",), 48),
        name="in_proj",
    )(x, g.reshape(1, D), mod, w_bf16)


def _gelu(x):
    return 0.5 * x * (1.0 + lax.erf(x * (1.0 / math.sqrt(2.0))))


def _sgu_kernel(za_ref, lng_ref, ws_ref, bs_ref, o_ref):
    u = _gelu(za_ref[:, 0:HALF])
    va = _gelu(za_ref[:, HALF:2 * HALF])
    mu = jnp.mean(va, axis=-1, keepdims=True)
    dv = va - mu
    var = jnp.mean(dv * dv, axis=-1, keepdims=True)
    vn = dv * lax.rsqrt(var + 1e-5) * lng_ref[...]
    lane = lax.broadcasted_iota(I32, (LANES, LANES), 1)
    first = lane < HEAD
    for c in range(TM // LANES):
        rows = slice(c * LANES, (c + 1) * LANES)
        for p in range(HALF // LANES):
            cols = slice(p * LANES, (p + 1) * LANES)
            vp = vn[rows, cols]
            s = (jnp.dot(ws_ref[2 * p], jnp.where(first, vp, 0.0).astype(BF16), preferred_element_type=F32)
                 + jnp.dot(ws_ref[2 * p + 1], jnp.where(first, 0.0, vp).astype(BF16), preferred_element_type=F32))
            o_ref[rows, cols] = u[rows, cols] * (s + bs_ref[:, cols])


def _sgu(za, ln_g, w_s_bf16, bs_full):
    return pl.pallas_call(
        _sgu_kernel,
        out_shape=jax.ShapeDtypeStruct((T, HALF), F32),
        grid=(NT,),
        in_specs=[pl.BlockSpec((TM, 2 * HALF), lambda i: (i, 0)), _full((1, HALF)),
                  _full((8, LANES, LANES)), _full((LANES, HALF))],
        out_specs=pl.BlockSpec((TM, HALF), lambda i: (i, 0)),
        compiler_params=_params(("arbitrary",)),
        name="sgu",
    )(za, ln_g.reshape(1, HALF), w_s_bf16, bs_full)


def _rwkv_prep_kernel(zb_ref, zp_ref, zn_ref, mu_ref, kk_ref, ka_ref, rk_ref, w0_ref, a0_ref,
                      wup_ref, aup_ref, gup_ref, ones_ref,
                      r_ref, v_ref, kkn_ref, bonus_ref, g_ref, lw_ref, kt_ref, b_ref):
    i = pl.program_id(0)
    in_sample = i >= PROMPT_TILES
    pos = (i - PROMPT_TILES) % TILES_PER_SAMPLE
    is_first = jnp.logical_or(jnp.logical_not(in_sample), pos == 0)
    is_last = jnp.logical_or(jnp.logical_not(in_sample), pos == TILES_PER_SAMPLE - 1)
    zb = zb_ref[...]
    prev_row = jnp.where(is_first, 0.0, zp_ref[7:8, :])
    next_row = jnp.where(is_last, 0.0, zn_ref[0:1, :])
    rowid = lax.broadcasted_iota(I32, (TM, 1), 0)
    zp = jnp.where(rowid == 0, prev_row, pltpu.roll(zb, 1, 0))
    zn = jnp.where(rowid == TM - 1, next_row, pltpu.roll(zb, TM - 1, 0))
    zs = zb + mu_ref[0:1, :] * (zp - zb) + mu_ref[1:2, :] * (zn - zb)
    r = zs[:, 0:HALF]
    k = zs[:, HALF:2 * HALF]
    v = zs[:, 2 * HALF:3 * HALF]
    wa = zs[:, 3 * HALF:3 * HALF + LANES]
    gd = zs[:, 3 * HALF + LANES:B_COLS]
    ones_bd = ones_ref[...]
    r_ref[...] = r
    v_ref[...] = v
    g_ref[...] = jnp.dot(jax.nn.sigmoid(gd).astype(BF16), gup_ref[...], preferred_element_type=F32)
    kk = k * kk_ref[...]
    kkn = kk / jnp.maximum(jnp.sqrt(_group_sum(kk * kk, ones_bd)), 1e-6)
    kkn_ref[...] = kkn
    bonus_ref[...] = _group_sum(r * k * rk_ref[...], ones_bd) * v
    tw = jnp.tanh(wa).astype(BF16)
    wa16 = wa.astype(BF16)
    for dd in range(2):
        lw_ref[dd] = -W_DECAY_SCALE * jax.nn.sigmoid(
            w0_ref[dd:dd + 1, :] + jnp.dot(tw, wup_ref[dd], preferred_element_type=F32))
        a = jax.nn.sigmoid(a0_ref[dd:dd + 1, :] + jnp.dot(wa16, aup_ref[dd], preferred_element_type=F32))
        kt_ref[dd] = k * (1.0 + (a - 1.0) * ka_ref[...])
        b_ref[dd] = a * kkn


def _rwkv_prep(zb, mu, k_k, k_a, r_k, w0, a0, wup_pad, aup_pad, g_up, ones_bd):
    row = lambda i: (i, 0)
    halo = TM // 8
    one = jax.ShapeDtypeStruct((T, HALF), F32)
    two = jax.ShapeDtypeStruct((2, T, HALF), F32)
    o1 = pl.BlockSpec((TM, HALF), row)
    o2 = pl.BlockSpec((2, TM, HALF), lambda i: (0, i, 0))
    return pl.pallas_call(
        _rwkv_prep_kernel,
        out_shape=[one, one, one, one, one, two, two, two],
        grid=(NT,),
        in_specs=[pl.BlockSpec((TM, B_COLS), row),
                  pl.BlockSpec((8, B_COLS), lambda i: (jnp.maximum(i * halo - 1, 0), 0)),
                  pl.BlockSpec((8, B_COLS), lambda i: (jnp.minimum((i + 1) * halo, T // 8 - 1), 0)),
                  _full((2, B_COLS)), _full((1, HALF)), _full((1, HALF)), _full((1, HALF)),
                  _full((2, HALF)), _full((2, HALF)),
                  _full((2, LANES, HALF)), _full((2, LANES, HALF)), _full((LANES, HALF)),
                  _full((HALF, HALF))],
        out_specs=[o1, o1, o1, o1, o1, o2, o2, o2],
        compiler_params=_params(("arbitrary",), 48),
        name="rwkv_prep",
    )(zb, zb, zb, mu, k_k.reshape(1, HALF), k_a.reshape(1, HALF), r_k.reshape(1, HALF), w0, a0,
      wup_pad, aup_pad, g_up, ones_bd)


def _rwkv_chunks(dirs):
    C = RW_CHUNK
    ti = lax.broadcasted_iota(I32, (C, C), 0)
    tj = lax.broadcasted_iota(I32, (C, C), 1)
    bi = lax.broadcasted_iota(I32, (LANES, LANES), 0)
    bj = lax.broadcasted_iota(I32, (LANES, LANES), 1)
    same = (bi >> 6) == (bj >> 6)
    pi = bi & (C - 1)
    pj = bj & (C - 1)
    eye = (bi == bj).astype(F32)
    h0 = lax.broadcasted_iota(I32, (C, LANES), 1) < HEAD

    def stack(x):
        return jnp.concatenate([jnp.where(h0, x, 0.0), jnp.where(h0, 0.0, x)], axis=0)

    def fold(x):
        return x[0:C] + x[C:2 * C]

    chains = []
    for rev, r, v, kk, lw, kt, b, s_ref, y_ref in dirs:
        tri = jnp.where((tj >= ti) if rev else (tj <= ti), 1.0, 0.0).astype(F32)
        cs = jnp.dot(tri, lw, precision=HIGHEST, preferred_element_type=F32)
        ctot = cs[0:1, :] if rev else cs[C - 1:C, :]
        e_neg = jnp.exp(-cs)
        e_tail = jnp.exp(ctot - cs)
        q1 = kk * jnp.exp(cs - lw)
        k1 = kt * e_neg
        b1 = b * e_neg
        r1 = r * jnp.exp(cs)
        k2 = kt * e_tail
        b2 = b * e_tail
        e_tot = jnp.exp(ctot)
        strict = jnp.logical_and(same, (pj > pi) if rev else (pj < pi))
        incl = jnp.logical_and(same, (pj >= pi) if rev else (pj <= pi))
        for p in range(HALF // LANES):
            cols = slice(p * LANES, (p + 1) * LANES)
            chains.append(dict(p=p, cols=cols, strict=strict, incl=incl, s_ref=s_ref, y_ref=y_ref,
                               q1=q1[:, cols], k1=k1[:, cols], b1=b1[:, cols], r1=r1[:, cols],
                               k2=k2[:, cols], b2=b2[:, cols], v=v[:, cols], e_tot=e_tot[:, cols]))

    for ch in chains:
        lhs = jnp.concatenate([stack(ch["q1"]), stack(ch["r1"])], axis=0)
        rhs = jnp.concatenate([ch["k1"], ch["k1"], ch["b1"], ch["b1"]], axis=0)
        gm = _mm(lhs, rhs, NT_DIMS, 3)
        ch["mk"] = jnp.where(ch["strict"], gm[0:2 * C, 0:2 * C], 0.0)
        ch["mb"] = jnp.where(ch["strict"], gm[0:2 * C, 2 * C:4 * C], 0.0)
        ch["nk"] = jnp.where(ch["incl"], gm[2 * C:4 * C, 0:2 * C], 0.0)
        ch["nb"] = jnp.where(ch["incl"], gm[2 * C:4 * C, 2 * C:4 * C], 0.0)
        ch["tinv"] = eye - jnp.where((pi >> 1) == (pj >> 1), ch["mb"], 0.0)
    size = 2
    while size < C:
        sh = size.bit_length() - 1
        blk = jnp.logical_and((pi >> (sh + 1)) == (pj >> (sh + 1)), (pi >> sh) != (pj >> sh))
        for ch in chains:
            ch["tn"] = _mm(ch["tinv"], jnp.where(blk, ch["mb"], 0.0), NN, 3)
        for ch in chains:
            ch["tinv"] = ch["tinv"] - _mm(ch["tn"], ch["tinv"], NN, 3)
        size *= 2
    for ch in chains:
        vst = stack(ch["v"])
        ch["mkv"] = fold(_mm(ch["mk"], vst, NN, 3))
        ch["nkv"] = fold(_mm(ch["nk"], vst, NN, 3))
        ch["s"] = ch["s_ref"][ch["p"]]
        ch["qr"] = _mm(jnp.concatenate([ch["q1"], ch["r1"]], axis=0), ch["s"], NT_DIMS, 3)
    for ch in chains:
        ch["u"] = fold(_mm(ch["tinv"], stack(ch["mkv"] + ch["qr"][0:C]), NN, 3))
    for ch in chains:
        ch["y_ref"][:, ch["cols"]] = ch["qr"][C:2 * C] + ch["nkv"] - fold(_mm(ch["nb"], stack(ch["u"]), NN, 3))
        upd = _mm(jnp.concatenate([ch["v"], ch["u"]], axis=0),
                  jnp.concatenate([ch["k2"], -ch["b2"]], axis=0), TN, 3)
        ch["s_ref"][ch["p"]] = ch["s"] * ch["e_tot"] + jnp.where(same, upd, 0.0)


def _rwkv_scan_kernel(bf_ref, bb_ref, first_ref, last_ref, seq_ref,
                      rf_ref, vf_ref, kkf_ref, lwf_ref, ktf_ref, bfw_ref,
                      rb_ref, vb_ref, kkb_ref, lwb_ref, ktb_ref, bbw_ref, s0_ref,
                      yf_ref, yb_ref, sfin_ref, s_ref):
    step = pl.program_id(0)

    @pl.when(first_ref[step] == 1)
    def _():
        s_ref[...] = s0_ref[:, 0]

    _rwkv_chunks([
        (False, rf_ref[...], vf_ref[...], kkf_ref[...], lwf_ref[0], ktf_ref[0], bfw_ref[0], s_ref.at[0], yf_ref),
        (True, rb_ref[...], vb_ref[...], kkb_ref[...], lwb_ref[0], ktb_ref[0], bbw_ref[0], s_ref.at[1], yb_ref)])

    @pl.when(last_ref[step] == 1)
    def _():
        sfin_ref[:, 0] = s_ref[...]


def _rwkv_scan(r, v, kk, lw, kt, b, s0_bd):
    C = RW_CHUNK
    tabs = _seq_tables(C)
    fwd = lambda i, bf, bb, fi, la, sq: (bf[i], 0)
    bwd = lambda i, bf, bb, fi, la, sq: (bb[i], 0)
    fwd3 = lambda i, bf, bb, fi, la, sq: (0, bf[i], 0)
    bwd3 = lambda i, bf, bb, fi, la, sq: (1, bb[i], 0)
    st = pl.BlockSpec((2, 1, 4, LANES, LANES), lambda i, bf, bb, fi, la, sq: (0, sq[i], 0, 0, 0))
    one_f, one_b = pl.BlockSpec((C, HALF), fwd), pl.BlockSpec((C, HALF), bwd)
    two_f, two_b = pl.BlockSpec((1, C, HALF), fwd3), pl.BlockSpec((1, C, HALF), bwd3)
    return pl.pallas_call(
        _rwkv_scan_kernel,
        out_shape=[jax.ShapeDtypeStruct((T, HALF), F32), jax.ShapeDtypeStruct((T, HALF), F32),
                   jax.ShapeDtypeStruct((2, N_SEQ, 4, LANES, LANES), F32)],
        grid_spec=pltpu.PrefetchScalarGridSpec(
            num_scalar_prefetch=5, grid=(len(tabs[0]),),
            in_specs=[one_f, one_f, one_f, two_f, two_f, two_f,
                      one_b, one_b, one_b, two_b, two_b, two_b, st],
            out_specs=[one_f, one_b, st],
            scratch_shapes=[pltpu.VMEM((2, 4, LANES, LANES), F32)]),
        compiler_params=_params(("arbitrary",)),
        name="rwkv_scan",
    )(*tabs, r, v, kk, lw, kt, b, r, v, kk, lw, kt, b, s0_bd)


def _rope(x, cos, sin_signed, first16):
    w = x.shape[1]
    partner = jnp.where(first16, pltpu.roll(x, w - 16, 1), pltpu.roll(x, 16, 1))
    return x * cos + partner * sin_signed


def _odd_prep_kernel(zc_ref, zr_ref, cos_ref, sin_ref, qkg_ref, ones_ref, cq_ref, ck_ref, ckraw_ref, rqk_ref):
    ones_bd = ones_ref[...]
    cos = cos_ref[...]
    sin = sin_ref[...]
    lane = lax.broadcasted_iota(I32, (TM, HALF), 1)
    first16 = (lane & 31) < 16
    for idx, (o_ref, raw_ref) in enumerate(((cq_ref, None), (ck_ref, ckraw_ref))):
        x = zc_ref[:, idx * HALF:(idx + 1) * HALF]
        ms = _group_sum(x * x, ones_bd) * (1.0 / HEAD)
        xn = x * lax.rsqrt(ms + 1e-6) * qkg_ref[idx:idx + 1, :]
        if raw_ref is not None:
            raw_ref[...] = xn
        o_ref[...] = _rope(xn, cos, sin, first16)
    rqk = _rope(zr_ref[...], cos, sin, first16)
    rqk_ref[...] = jnp.where(lane < HALF // 2, rqk * (HEAD ** -0.5), rqk)


def _odd_prep(zc, zr, cos_tab, sin_tab, qkg_tiled, ones_bd):
    row = lambda i: (i, 0)
    tab = lambda i: (jnp.where(i < PROMPT_TILES, 0, 1 + (i - PROMPT_TILES) % TILES_PER_SAMPLE), 0)
    one = jax.ShapeDtypeStruct((T, HALF), F32)
    o1 = pl.BlockSpec((TM, HALF), row)
    return pl.pallas_call(
        _odd_prep_kernel,
        out_shape=[one, one, one, one], grid=(NT,),
        in_specs=[pl.BlockSpec((TM, 2 * HALF), row), pl.BlockSpec((TM, HALF), row),
                  pl.BlockSpec((TM, HALF), tab), pl.BlockSpec((TM, HALF), tab),
                  _full((2, HALF)), _full((HALF, HALF))],
        out_specs=[o1, o1, o1, o1],
        compiler_params=_params(("arbitrary",)),
        name="odd_prep",
    )(zc, zr, cos_tab, sin_tab, qkg_tiled, ones_bd)


def _attn_kernel(*refs, has_ctx, one_minus_li):
    if has_ctx:
        q_ref, k_ref, v_ref, kc_ref, vc_ref, lam_ref, sg_ref, _, o_ref = refs
    else:
        q_ref, k_ref, v_ref, lam_ref, sg_ref, _, o_ref = refs
    lam = lam_ref[...]
    lane = lax.broadcasted_iota(I32, (LANES, LANES), 1)
    m0 = lane < HEAD
    scale = HEAD ** -0.5
    for h in range(4):
        cols = slice(h * LANES, (h + 1) * LANES)
        qp = q_ref[:, cols]
        segs = [(k_ref[:, cols], v_ref[:, cols])]
        if has_ctx:
            segs.append((kc_ref[0, :, cols], vc_ref[0, :, cols]))
        outs = []
        for qm in (jnp.where(m0, qp, 0.0), jnp.where(m0, 0.0, qp)):
            qm16 = qm.astype(BF16)
            ss = [lax.dot_general(qm16, ks.astype(BF16), NT_DIMS, preferred_element_type=F32) * scale
                  for ks, _ in segs]
            mx = ss[0].max(axis=-1, keepdims=True)
            for s_ in ss[1:]:
                mx = jnp.maximum(mx, s_.max(axis=-1, keepdims=True))
            ps = [jnp.exp(s_ - mx) for s_ in ss]
            den = ps[0].sum(axis=-1, keepdims=True)
            for p_ in ps[1:]:
                den = den + p_.sum(axis=-1, keepdims=True)
            outs.append([p_ / den for p_ in ps])
        acc = None
        for si, (_, vs) in enumerate(segs):
            amap = outs[0][si] - lam * outs[1][si]
            t = jnp.dot(amap.astype(BF16), vs.astype(BF16), preferred_element_type=F32)
            acc = t if acc is None else acc + t
        nrm = acc * lax.rsqrt(jnp.mean(acc * acc, axis=-1, keepdims=True) + 1e-6) * sg_ref[...]
        o_ref[:, cols] = nrm * one_minus_li


def _attn(cq, ck, zc, row0, n_seq, seq_len, lam, subln_g, one_minus_li, prev, ctx_k=None, ctx_v=None):
    nq = seq_len // LANES
    qb0 = row0 // LANES
    sb0 = row0 // seq_len
    in_specs = [pl.BlockSpec((LANES, HALF), lambda s, q: (qb0 + s * nq + q, 0)),
                pl.BlockSpec((seq_len, HALF), lambda s, q: (sb0 + s, 0)),
                pl.BlockSpec((seq_len, HALF), lambda s, q: (sb0 + s, 2))]
    args = [cq, ck, zc]
    if ctx_k is not None:
        in_specs += [pl.BlockSpec((1, PAST, HALF), lambda s, q: (s, 0, 0))] * 2
        args += [ctx_k, ctx_v]
    in_specs += [_full((1, 1)), _full((1, LANES))]
    args += [lam.reshape(1, 1), subln_g.reshape(1, LANES)]
    in_specs.append(pl.BlockSpec(memory_space=pl.ANY))
    args.append(prev)
    aliases = {len(args) - 1: 0}
    return pl.pallas_call(
        functools.partial(_attn_kernel, has_ctx=ctx_k is not None, one_minus_li=one_minus_li),
        out_shape=jax.ShapeDtypeStruct((T, HALF), F32),
        grid=(n_seq, nq), in_specs=in_specs,
        out_specs=pl.BlockSpec((LANES, HALF), lambda s, q: (qb0 + s * nq + q, 0)),
        input_output_aliases=aliases,
        compiler_params=_params(("arbitrary", "arbitrary"), 48),
        name="diff_attn",
    )(*args)


_LOG_GAMMA = tuple(tuple(float(np.log1p(-np.exp2(-np.float32(e)), dtype=np.float32)) for e in es)
                   for es in RET_EXP)


def _ret_chunk(rev, qk_ref, v_ref, s_ref, o_ref):
    C = RET_CHUNK
    ii = lax.broadcasted_iota(I32, (C, C), 0)
    jj = lax.broadcasted_iota(I32, (C, C), 1)
    mask = (jj > ii) if rev else (jj <= ii)
    dist = jnp.where(mask, (jj - ii) if rev else (ii - jj), 0).astype(F32)
    ri = lax.broadcasted_iota(I32, (C, 1), 0)
    kpow = (ri if rev else (C - 1 - ri)).astype(F32)
    qpow = ((C - ri) if rev else (ri + 1)).astype(F32)
    lane = lax.broadcasted_iota(I32, (C, LANES), 1)
    for h in range(4):
        lg = _LOG_GAMMA[1 if rev else 0][h]
        p = h // 2
        hm = (lane < HEAD) if h % 2 == 0 else (lane >= HEAD)
        qp = jnp.where(hm, qk_ref[:, p * LANES:(p + 1) * LANES], 0.0)
        kp = jnp.where(hm, qk_ref[:, HALF // 2 + p * LANES:HALF // 2 + (p + 1) * LANES], 0.0)
        vh = v_ref[:, h * LANES:(h + 1) * LANES].astype(BF16)
        decay = jnp.where(mask, jnp.exp(lg * dist), 0.0)
        sc = lax.dot_general(qp.astype(BF16), kp.astype(BF16), NT_DIMS, preferred_element_type=F32) * decay
        o_intra = jnp.dot(sc.astype(BF16), vh, preferred_element_type=F32)
        s_prev = s_ref[h]
        o_cross = jnp.dot((qp * jnp.exp(lg * qpow)).astype(BF16), s_prev.astype(BF16),
                          preferred_element_type=F32)
        o_ref[:, h * LANES:(h + 1) * LANES] = o_intra + o_cross
        kv = lax.dot_general((kp * jnp.exp(lg * kpow)).astype(BF16), vh, TN, preferred_element_type=F32)
        s_ref[h] = math.exp(lg * C) * s_prev + kv


def _ret_kernel(bf_ref, bb_ref, first_ref, last_ref, seq_ref,
                qkf_ref, vf_ref, qkb_ref, vb_ref, s0_ref, of_ref, ob_ref, sfin_ref, s_ref):
    step = pl.program_id(0)

    @pl.when(first_ref[step] == 1)
    def _():
        s_ref[...] = s0_ref[:, 0]

    _ret_chunk(False, qkf_ref, vf_ref, s_ref.at[0], of_ref)
    _ret_chunk(True, qkb_ref, vb_ref, s_ref.at[1], ob_ref)

    @pl.when(last_ref[step] == 1)
    def _():
        sfin_ref[:, 0] = s_ref[...]


def _retention(rqk, zr, s0):
    C = RET_CHUNK
    tabs = _seq_tables(C)
    st = pl.BlockSpec((2, 1, 4, LANES, LANES), lambda i, bf, bb, fi, la, sq: (0, sq[i], 0, 0, 0))
    spec = lambda use_b, col: pl.BlockSpec(
        (C, HALF), lambda i, bf, bb, fi, la, sq: ((bb if use_b else bf)[i], col))
    return pl.pallas_call(
        _ret_kernel,
        out_shape=[jax.ShapeDtypeStruct((T, HALF), F32), jax.ShapeDtypeStruct((T, HALF), F32),
                   jax.ShapeDtypeStruct((2, N_SEQ, 4, LANES, LANES), F32)],
        grid_spec=pltpu.PrefetchScalarGridSpec(
            num_scalar_prefetch=5, grid=(len(tabs[0]),),
            in_specs=[spec(False, 0), spec(False, 1), spec(True, 0), spec(True, 1), st],
            out_specs=[spec(False, 0), spec(True, 0), st],
            scratch_shapes=[pltpu.VMEM((2, 4, LANES, LANES), F32)]),
        compiler_params=_params(("arbitrary",)),
        name="retention",
    )(*tabs, rqk, zr, rqk, zr, s0)


def _out_kernel(*refs, even):
    if even:
        (a_ref, yf_ref, yb_ref, bonus_ref, g_ref, gng_ref, gnb_ref, ones_ref,
         x_ref, mod_ref, ng_ref, wo_ref, rw_ref, rb_ref,
         y_ref, xp_ref, ti_ref, tg_ref, rk_ref, cnt_ref, run_ref) = refs
        ones_bd = ones_ref[...]
        ys = yf_ref[...] + yb_ref[...]
        mu = _group_sum(ys, ones_bd) * (1.0 / HEAD)
        dv = ys - mu
        var = _group_sum(dv * dv, ones_bd) * (1.0 / HEAD)
        yn = dv * lax.rsqrt(var + RWKV_GN_EPS) * gng_ref[...] + gnb_ref[...]
        left = a_ref[...]
        right = (yn + bonus_ref[...]) * g_ref[...]
    else:
        (c_ref, of_ref, ob_ref, rg_ref, gng_ref,
         x_ref, mod_ref, ng_ref, wo_ref, rw_ref, rb_ref,
         y_ref, xp_ref, ti_ref, tg_ref, rk_ref, cnt_ref, run_ref) = refs
        left = c_ref[...]
        rg = rg_ref[...]
        gate = rg * jax.nn.sigmoid(rg)
        os_ = of_ref[...] + ob_ref[...]
        parts = []
        for h in range(4):
            oh = os_[:, h * LANES:(h + 1) * LANES]
            mu = jnp.mean(oh, axis=-1, keepdims=True)
            dv = oh - mu
            var = jnp.mean(dv * dv, axis=-1, keepdims=True)
            parts.append(dv * lax.rsqrt(var + 1e-5))
        right = gate * (jnp.concatenate(parts, axis=1) * gng_ref[...])
    mod = mod_ref[0]
    o = (jnp.dot(left.astype(BF16), wo_ref[0:HALF, :], preferred_element_type=F32)
         + jnp.dot(right.astype(BF16), wo_ref[HALF:2 * HALF, :], preferred_element_type=F32))
    y = x_ref[...] + mod[:, 2 * D:3 * D] * o
    y_ref[...] = y
    yn2 = y * lax.rsqrt(jnp.mean(y * y, axis=-1, keepdims=True) + 1e-6) * ng_ref[...]
    t = yn2 * (1.0 + mod[:, 4 * D:5 * D]) + mod[:, 3 * D:4 * D]
    xp_ref[...] = t
    logits = _mm(t, rw_ref[...], NN, 3) + rb_ref[...]
    lane = lax.broadcasted_iota(I32, (TM, LANES), 1)
    neg = jnp.float32(-jnp.inf)
    lg = jnp.where(lane < N_EXPERTS, logits, neg)
    vals, hits = [], []
    for _ in range(TOP_K):
        m = jnp.max(lg, axis=-1, keepdims=True)
        ix = jnp.min(jnp.where(lg == m, lane, LANES), axis=-1, keepdims=True)
        hit = lane == ix
        vals.append(m)
        hits.append((ix, hit))
        lg = jnp.where(hit, neg, lg)
    es = [jnp.exp(vv - vals[0]) for vv in vals]
    den = es[0] + es[1] + es[2] + es[3]

    @pl.when(pl.program_id(0) == 0)
    def _():
        run_ref[...] = jnp.zeros_like(run_ref)

    member = jnp.zeros((TM, LANES), F32)
    for _, hit in hits:
        member = member + jnp.where(hit, 1.0, 0.0)
    ri = lax.broadcasted_iota(I32, (TM, TM), 0)
    ci = lax.broadcasted_iota(I32, (TM, TM), 1)
    before = jnp.where(ci < ri, 1.0, 0.0).astype(BF16)
    seen = run_ref[...] + jnp.dot(before, member.astype(BF16), preferred_element_type=F32)
    ti = jnp.zeros((TM, LANES), I32)
    tg = jnp.zeros((TM, LANES), F32)
    rk = jnp.zeros((TM, LANES), F32)
    for kk, (ix, hit) in enumerate(hits):
        ti = jnp.where(lane == kk, ix, ti)
        tg = jnp.where(lane == kk, es[kk] / den, tg)
        rk = jnp.where(lane == kk, jnp.sum(jnp.where(hit, seen, 0.0), axis=-1, keepdims=True), rk)
    ti_ref[...] = ti
    tg_ref[...] = tg
    rk_ref[...] = rk.astype(I32)
    run_ref[...] = run_ref[...] + jnp.sum(member, axis=0, keepdims=True)
    cnt_ref[...] = run_ref[...]


def _out_proj(even, mix_args, mix_specs, x, mod, norm_g, w_out_bf16, rw_pad, rb_pad):
    row = lambda i: (i, 0)
    modspec = pl.BlockSpec((1, 1, N_MOD * D), lambda i: (_group(i), 0, 0))
    in_specs = list(mix_specs) + [pl.BlockSpec((TM, D), row), modspec, _full((1, D)), _full((D, D)),
                                  _full((D, LANES)), _full((1, LANES))]
    args = list(mix_args) + [x, mod, norm_g.reshape(1, D), w_out_bf16, rw_pad, rb_pad]
    lane_i = jax.ShapeDtypeStruct((T, LANES), I32)
    lane_spec = pl.BlockSpec((TM, LANES), row)
    return pl.pallas_call(
        functools.partial(_out_kernel, even=even),
        out_shape=[jax.ShapeDtypeStruct((T, D), F32), jax.ShapeDtypeStruct((T, D), F32),
                   lane_i, jax.ShapeDtypeStruct((T, LANES), F32), lane_i,
                   jax.ShapeDtypeStruct((1, LANES), F32)],
        grid=(NT,), in_specs=in_specs,
        out_specs=[pl.BlockSpec((TM, D), row), pl.BlockSpec((TM, D), row),
                   lane_spec, lane_spec, lane_spec, _full((1, LANES))],
        scratch_shapes=[pltpu.VMEM((1, LANES), F32)],
        compiler_params=_params(("arbitrary",), 48),
        name="out_proj",
    )(*args)


def _route_kernel(cnt_ref, ti_ref, rk_ref, dest_ref, te_ref, nt_ref):
    cnt = cnt_ref[...].astype(I32)
    ntile = lax.shift_right_logical(cnt + (TMX - 1), TMX.bit_length() - 1)
    ei = lax.broadcasted_iota(I32, (LANES, LANES), 0)
    ej = lax.broadcasted_iota(I32, (LANES, LANES), 1)
    upto = jnp.where(ei <= ej, 1.0, 0.0).astype(BF16)
    ntile_f = jnp.broadcast_to(ntile.astype(F32), (8, LANES))
    tile_end = jnp.dot(ntile_f.astype(BF16), upto, preferred_element_type=F32)[0:1, :]
    row_start = (tile_end - ntile.astype(F32)) * float(TMX)
    lane = lax.broadcasted_iota(I32, (TM, LANES), 1)
    ti = ti_ref[...]
    rk = rk_ref[...]
    dest = jnp.zeros((TM, LANES), F32)
    for k in range(TOP_K):
        hit = lane == ti[:, k:k + 1]
        start = jnp.sum(jnp.where(hit, row_start, 0.0), axis=-1, keepdims=True)
        dest = jnp.where(lane == k, start + rk[:, k:k + 1].astype(F32), dest)
    dest_ref[...] = dest.astype(I32)

    @pl.when(pl.program_id(0) == 0)
    def _():
        lane1 = lax.broadcasted_iota(I32, (1, LANES), 1)
        n_tiles = jnp.max(tile_end, axis=-1, keepdims=True)
        last_e = jnp.max(jnp.where(cnt > 0, lane1, 0), axis=-1, keepdims=True)
        tile = lax.broadcasted_iota(I32, (TM, 1), 0).astype(F32)
        te = jnp.sum(jnp.where(tile_end <= tile, 1, 0), axis=-1, keepdims=True)
        te = jnp.where(tile < n_tiles, te, last_e)
        te_ref[...] = jnp.broadcast_to(te, (TM, LANES)).astype(I32)
        first_row = lax.broadcasted_iota(I32, (8, LANES), 0) == 0
        nt_ref[...] = jnp.where(first_row, n_tiles, tile_end).astype(I32)


def _route(cnt, ti, rk):
    row = lambda i: (i, 0)
    return pl.pallas_call(
        _route_kernel,
        out_shape=[jax.ShapeDtypeStruct((T, LANES), I32), jax.ShapeDtypeStruct((TM, LANES), I32),
                   jax.ShapeDtypeStruct((8, LANES), I32)],
        grid=(NT,),
        in_specs=[_full((1, LANES)), pl.BlockSpec((TM, LANES), row), pl.BlockSpec((TM, LANES), row)],
        out_specs=[pl.BlockSpec((TM, LANES), row), _full((TM, LANES)), _full((8, LANES))],
        compiler_params=_params(("arbitrary",)),
        name="moe_route",
    )(cnt, ti, rk)


def _row_copy(src_ref, src_row, dst_ref, dst_row, sem):
    return pltpu.make_async_copy(src_ref.at[pl.ds(src_row, 1)], dst_ref.at[pl.ds(dst_row, 1)], sem)


def _wait_tiles(n, src_ref, dst_ref, sem):
    for _ in range(n):
        pltpu.make_async_copy(src_ref, dst_ref, sem).wait()


def _dispatch_kernel(dest_ref, tend_ref, x_ref, xs_ref, zero_ref, sem):
    i = pl.program_id(0)

    @pl.when(i == 0)
    def _():
        zero_ref[...] = jnp.zeros_like(zero_ref)

        def last_tile(e, fn):
            end = tend_ref[e]
            begin = tend_ref[e - 1] if e > 0 else 0

            @pl.when(end > begin)
            def _():
                fn(pltpu.make_async_copy(zero_ref, xs_ref.at[pl.ds((end - 1) * TMX, TMX)], sem))

        def unused_tile(j):
            return pltpu.make_async_copy(zero_ref, xs_ref.at[pl.ds(j * TMX, TMX)], sem)

        def start_unused(j, carry):
            unused_tile(j).start()
            return carry

        def wait_unused(j, carry):
            unused_tile(j).wait()
            return carry

        n_used = tend_ref[N_EXPERTS - 1]
        for e in range(N_EXPERTS):
            last_tile(e, lambda c: c.start())
        lax.fori_loop(n_used, MOE_TILES, start_unused, 0)
        for e in range(N_EXPERTS):
            last_tile(e, lambda c: c.wait())
        lax.fori_loop(n_used, MOE_TILES, wait_unused, 0)

    base = i * (TM * TOP_K)

    def start(r, carry):
        for k in range(TOP_K):
            _row_copy(x_ref, r, xs_ref, dest_ref[base + r * TOP_K + k], sem).start()
        return carry

    lax.fori_loop(0, TM, start, 0, unroll=4)
    _wait_tiles(TOP_K, x_ref, xs_ref.at[pl.ds(0, TM)], sem)


def _dispatch(dest_flat, tile_end, xt):
    return pl.pallas_call(
        _dispatch_kernel,
        out_shape=jax.ShapeDtypeStruct((R_PAD, D), F32),
        grid_spec=pltpu.PrefetchScalarGridSpec(
            num_scalar_prefetch=2, grid=(NT,),
            in_specs=[pl.BlockSpec((TM, D), lambda i, d, te: (i, 0))],
            out_specs=pl.BlockSpec(memory_space=pl.ANY),
            scratch_shapes=[pltpu.VMEM((TMX, D), F32), pltpu.SemaphoreType.DMA(())]),
        compiler_params=_params(("arbitrary",)),
        name="moe_dispatch",
    )(dest_flat, tile_end, xt)


def _moe_up_kernel(te_ref, nt_ref, xs_ref, w_ref, b_ref, act_ref):
    i = pl.program_id(0)

    @pl.when(i < nt_ref[0])
    def _():
        x16 = xs_ref[...].astype(BF16)
        for gc in (slice(0, HALF), slice(HALF, D)):
            uc = slice(D + gc.start, D + gc.stop)
            g = jnp.dot(x16, w_ref[0, :, gc].astype(BF16), preferred_element_type=F32) + b_ref[0, :, gc]
            u = jnp.dot(x16, w_ref[0, :, uc].astype(BF16), preferred_element_type=F32) + b_ref[0, :, uc]
            gt = jnp.minimum(g, SWIGLU_LIMIT)
            up = jnp.clip(u, -SWIGLU_LIMIT, SWIGLU_LIMIT)
            act_ref[:, gc] = ((up + 1.0) * gt * jax.nn.sigmoid(SWIGLU_ALPHA * gt)).astype(BF16)

    @pl.when(i >= nt_ref[0])
    def _():
        act_ref[...] = jnp.zeros_like(act_ref)


def _tile_clamped(i, te, nt):
    return (jnp.minimum(i, jnp.maximum(nt[0] - 1, 0)), 0)


def _moe_up(te, n_tiles, xs, w_gu, b_gu):
    return pl.pallas_call(
        _moe_up_kernel,
        out_shape=jax.ShapeDtypeStruct((R_PAD, D), BF16),
        grid_spec=pltpu.PrefetchScalarGridSpec(
            num_scalar_prefetch=2, grid=(MOE_TILES,),
            in_specs=[pl.BlockSpec((TMX, D), _tile_clamped),
                      pl.BlockSpec((1, D, 2 * D), lambda i, te, nt: (te[i], 0, 0)),
                      pl.BlockSpec((1, 1, 2 * D), lambda i, te, nt: (te[i], 0, 0))],
            out_specs=pl.BlockSpec((TMX, D), lambda i, te, nt: (i, 0))),
        compiler_params=_params(("arbitrary",), 56),
        name="moe_up",
    )(te, n_tiles, xs, w_gu, b_gu)


def _moe_down_kernel(te_ref, nt_ref, act_ref, w_ref, b_ref, y_ref):
    i = pl.program_id(0)

    @pl.when(i < nt_ref[0])
    def _():
        y_ref[...] = jnp.dot(act_ref[...], w_ref[0].astype(BF16), preferred_element_type=F32) + b_ref[0]

    @pl.when(i >= nt_ref[0])
    def _():
        y_ref[...] = jnp.zeros_like(y_ref)


def _moe_down(te, n_tiles, act, w_dn, b_dn):
    return pl.pallas_call(
        _moe_down_kernel,
        out_shape=jax.ShapeDtypeStruct((R_PAD, D), F32),
        grid_spec=pltpu.PrefetchScalarGridSpec(
            num_scalar_prefetch=2, grid=(MOE_TILES,),
            in_specs=[pl.BlockSpec((TMX, D), _tile_clamped),
                      pl.BlockSpec((1, D, D), lambda i, te, nt: (te[i], 0, 0)),
                      pl.BlockSpec((1, 1, D), lambda i, te, nt: (te[i], 0, 0))],
            out_specs=pl.BlockSpec((TMX, D), lambda i, te, nt: (i, 0))),
        compiler_params=_params(("arbitrary",), 40),
        name="moe_down",
    )(te, n_tiles, act, w_dn, b_dn)


def _combine_kernel(dest_ref, x_ref, tg_ref, mod_ref, ys_ref, o_ref, buf_ref, sem):
    i = pl.program_id(0)
    slot = i % 2

    def gather(tile, b):
        base = tile * (TM * TOP_K)

        def body(r, carry):
            for k in range(TOP_K):
                _row_copy(ys_ref, dest_ref[base + r * TOP_K + k], buf_ref.at[b, k], r, sem.at[b]).start()
            return carry

        lax.fori_loop(0, TM, body, 0, unroll=4)

    @pl.when(i == 0)
    def _():
        gather(0, 0)

    @pl.when(i + 1 < NT)
    def _():
        gather(i + 1, 1 - slot)

    _wait_tiles(TOP_K, ys_ref.at[pl.ds(0, TM)], buf_ref.at[slot, 0], sem.at[slot])
    tg = tg_ref[...]
    f = tg[:, 0:1] * buf_ref[slot, 0]
    for k in range(1, TOP_K):
        f = f + tg[:, k:k + 1] * buf_ref[slot, k]
    o_ref[...] = x_ref[...] + mod_ref[0][:, 5 * D:6 * D] * f


def _combine(dest_flat, x, tg, mod, ys):
    row = lambda i, d: (i, 0)
    return pl.pallas_call(
        _combine_kernel,
        out_shape=jax.ShapeDtypeStruct((T, D), F32),
        grid_spec=pltpu.PrefetchScalarGridSpec(
            num_scalar_prefetch=1, grid=(NT,),
            in_specs=[pl.BlockSpec((TM, D), row), pl.BlockSpec((TM, LANES), row),
                      pl.BlockSpec((1, 1, N_MOD * D), lambda i, d: (_group(i), 0, 0)),
                      pl.BlockSpec(memory_space=pl.ANY)],
            out_specs=pl.BlockSpec((TM, D), row),
            scratch_shapes=[pltpu.VMEM((2, TOP_K, TM, D), F32), pltpu.SemaphoreType.DMA((2,))]),
        compiler_params=_params(("arbitrary",), 40),
        name="moe_combine",
    )(dest_flat, x, tg, mod, ys)


def _moe(layer, y, xt, ti, tg, rk, cnt, mod, w_gu, b_gu, w_dn, b_dn):
    dest, te, nt = _route(cnt, ti, rk)
    dest_flat = dest[:, :TOP_K].reshape(-1)
    te = te[:MOE_TILES, 0] + layer * N_EXPERTS
    n_tiles = nt[0, :1]
    xs = _dispatch(dest_flat, nt[1, :N_EXPERTS], xt)
    n_all = w_gu.shape[0] * N_EXPERTS
    act = _moe_up(te, n_tiles, xs, w_gu.reshape(n_all, D, 2 * D), b_gu.reshape(n_all, 1, 2 * D))
    ys = _moe_down(te, n_tiles, act, w_dn.reshape(n_all, D, D), b_dn.reshape(n_all, 1, D))
    return _combine(dest_flat, y, tg, mod, ys)


def _ones_blockdiag():
    idx = np.arange(HALF) // HEAD
    return jnp.asarray((idx[:, None] == idx[None, :]).astype(np.float32), dtype=BF16)


def _rope_tables():
    pos = jnp.arange(L_SAMPLE)
    rowp = (pos // 64).astype(F32)
    colp = (pos % 64).astype(F32)
    nf = HEAD // 4
    inv = jnp.power(10000.0, -jnp.arange(nf, dtype=F32) / nf)
    ar = rowp[:, None] * inv[None, :]
    ac = colp[:, None] * inv[None, :]
    cos64 = jnp.concatenate([jnp.cos(ar), jnp.cos(ar), jnp.cos(ac), jnp.cos(ac)], axis=1)
    sin64 = jnp.concatenate([-jnp.sin(ar), jnp.sin(ar), -jnp.sin(ac), jnp.sin(ac)], axis=1)
    cos = jnp.tile(cos64, (1, HALF // HEAD))
    sin = jnp.tile(sin64, (1, HALF // HEAD))
    ident = jnp.ones((TM, HALF), F32)
    return (jnp.concatenate([ident, cos], axis=0), jnp.concatenate([jnp.zeros((TM, HALF), F32), sin], axis=0))


def _bd_pairs(s):
    lead = s.shape[:-3]
    s = s.reshape(lead + (4, 2, HEAD, HEAD))
    z = jnp.zeros_like(s[..., 0, :, :])
    top = jnp.concatenate([s[..., 0, :, :], z], axis=-1)
    bot = jnp.concatenate([z, s[..., 1, :, :]], axis=-1)
    return jnp.concatenate([top, bot], axis=-2)


def _bd_unpairs(s):
    a = s[..., 0:HEAD, 0:HEAD]
    b = s[..., HEAD:, HEAD:]
    out = jnp.stack([a, b], axis=-3)
    return out.reshape(s.shape[:-3] + (8, HEAD, HEAD))


def kernel(x_prompt, x_sample, state_rwkv, cache_k_diff, cache_v_diff, state_retention, c, c_ctx, norm_g, ada_w, ada_b, e_w_in, e_w_out, sgu_ln_g, sgu_w, sgu_b, rw_mu, rw_w0, rw_w_up, rw_a0, rw_a_up, rw_g_up, rw_k_k, rw_k_a, rw_r_k, rw_gn_g, rw_gn_b, o_w_in, o_w_out, da_qk_g, da_lam, da_subln_g, ret_gn_g, router_w, router_b, ex_w_gu, ex_b_gu, ex_w_dn, ex_b_dn):
    x = jnp.concatenate([x_prompt.reshape(T_PROMPT, D), x_sample.reshape(T_SAMPLE, D)], axis=0)
    cvec8 = jnp.concatenate([c_ctx[None, :], c, jnp.zeros((3, D), F32)], axis=0)
    mods = _adaln(cvec8, ada_w, ada_b)
    mod0 = mods[0].reshape(8, 1, N_MOD * D)
    mod1 = mods[1].reshape(8, 1, N_MOD * D)
    ones_bd = _ones_blockdiag()
    rw_pad = jnp.pad(router_w, ((0, 0), (0, 0), (0, LANES - N_EXPERTS)))
    rb_pad = jnp.pad(router_b, ((0, 0), (0, LANES - N_EXPERTS))).reshape(2, 1, LANES)
    row = lambda i: (i, 0)
    half = pl.BlockSpec((TM, HALF), row)

    za, zb = _in_proj(x, norm_g[0, 0], mod0, e_w_in[0].astype(BF16), (2 * HALF, B_COLS))
    bs_full = jnp.repeat(sgu_b[0].T, HEAD, axis=1)
    a_out = _sgu(za, sgu_ln_g[0], sgu_w[0].astype(BF16), bs_full)
    zpad = jnp.zeros((2, HEAD, HALF), F32)
    wup_pad = jnp.concatenate([rw_w_up[0], zpad], axis=1).astype(BF16)
    aup_pad = jnp.concatenate([zpad, rw_a_up[0]], axis=1).astype(BF16)
    r, v, kkn, bonus, g, lw, kt, b = _rwkv_prep(zb, rw_mu[0], rw_k_k[0], rw_k_a[0], rw_r_k[0], rw_w0[0], rw_a0[0],
                                                wup_pad, aup_pad, rw_g_up[0].astype(BF16), ones_bd)
    s0_sample = _bd_pairs(jnp.moveaxis(state_rwkv[:, 0], 1, 0))
    s0_rw = jnp.concatenate([jnp.zeros((2, N_PROMPT, 4, LANES, LANES), F32), s0_sample], axis=1)
    yf_rw, yb_rw, sfin_rw = _rwkv_scan(r, v, kkn, lw, kt, b, s0_rw)
    new_rwkv = jnp.moveaxis(_bd_unpairs(sfin_rw[:, :N_PROMPT]), 0, 1)[:, None]
    y0, xp0, ti0, tg0, rk0, cnt0 = _out_proj(
        True,
        [a_out, yf_rw, yb_rw, bonus, g, rw_gn_g[0].reshape(1, HALF), rw_gn_b[0].reshape(1, HALF), ones_bd],
        [half, half, half, half, half, _full((1, HALF)), _full((1, HALF)), _full((HALF, HALF))],
        x, mod0, norm_g[0, 1], e_w_out[0].astype(BF16), rw_pad[0], rb_pad[0])
    x1 = _moe(0, y0, xp0, ti0, tg0, rk0, cnt0, mod0, ex_w_gu, ex_b_gu, ex_w_dn, ex_b_dn)

    zc, zr = _in_proj(x1, norm_g[1, 0], mod1, o_w_in[0].astype(BF16), (3 * HALF, 3 * HALF))
    cos_tab, sin_tab = _rope_tables()
    qkg = jnp.tile(da_qk_g[0], (1, HALF // HEAD))
    cq, ck, ck_raw, rqk = _odd_prep(zc, zr, cos_tab, sin_tab, qkg, ones_bd)
    lambda_init = 0.8 - 0.6 * math.exp(-0.3 * 1)
    lv = da_lam[0]
    lam = jnp.exp(jnp.sum(lv[0] * lv[1])) - jnp.exp(jnp.sum(lv[2] * lv[3])) + lambda_init
    c_out = _attn(cq, ck, zc, 0, N_PROMPT, L_PROMPT, lam, da_subln_g[0], 1.0 - lambda_init,
                  jnp.zeros((T, HALF), F32))
    ctx_k = cache_k_diff[:, 0].reshape(N_SAMPLE, PAST, HALF)
    ctx_v = cache_v_diff[:, 0].reshape(N_SAMPLE, PAST, HALF)
    c_out = _attn(cq, ck, zc, T_PROMPT, N_SAMPLE, L_SAMPLE, lam, da_subln_g[0], 1.0 - lambda_init,
                  c_out, ctx_k, ctx_v)
    sr = jnp.moveaxis(state_retention[:, 0], 1, 0)
    zr0 = jnp.zeros_like(sr)
    s0_sample = jnp.stack([jnp.concatenate([sr[:, :, 0], zr0[:, :, 0]], axis=-2),
                           jnp.concatenate([zr0[:, :, 1], sr[:, :, 1]], axis=-2),
                           jnp.concatenate([sr[:, :, 2], zr0[:, :, 2]], axis=-2),
                           jnp.concatenate([zr0[:, :, 3], sr[:, :, 3]], axis=-2)], axis=2)
    s0_ret = jnp.concatenate([jnp.zeros((2, N_PROMPT, 4, LANES, LANES), F32), s0_sample], axis=1)
    of_ret, ob_ret, rfin = _retention(rqk, zr, s0_ret)
    rfin_p = rfin[:, :N_PROMPT]
    new_ret = jnp.stack([rfin_p[:, :, 0, 0:HEAD], rfin_p[:, :, 1, HEAD:], rfin_p[:, :, 2, 0:HEAD],
                         rfin_p[:, :, 3, HEAD:]], axis=2)
    new_ret = jnp.moveaxis(new_ret, 0, 1)[:, None]
    y1, xp1, ti1, tg1, rk1, cnt1 = _out_proj(
        False,
        [c_out, of_ret, ob_ret, zr, ret_gn_g[0].reshape(1, HALF)],
        [half, half, half, pl.BlockSpec((TM, HALF), lambda i: (i, 2)), _full((1, HALF))],
        x1, mod1, norm_g[1, 1], o_w_out[0].astype(BF16), rw_pad[1], rb_pad[1])
    y_fin = _moe(1, y1, xp1, ti1, tg1, rk1, cnt1, mod1, ex_w_gu, ex_b_gu, ex_w_dn, ex_b_dn)

    new_k = ck_raw[:T_PROMPT].reshape(N_PROMPT, 1, L_PROMPT, 4, LANES)
    new_v = zc[:T_PROMPT, 2 * HALF:3 * HALF].reshape(N_PROMPT, 1, L_PROMPT, 4, LANES)
    return (y_fin[:T_PROMPT].reshape(N_PROMPT, L_PROMPT, D), y_fin[T_PROMPT:].reshape(N_SAMPLE, L_SAMPLE, D),
            new_rwkv, new_k, new_v, new_ret)
```

```python
import functools
import math

import numpy as np
import jax
import jax.numpy as jnp
from jax import lax
from jax.experimental import pallas as pl
from jax.experimental.pallas import tpu as pltpu

F32 = jnp.float32
BF16 = jnp.bfloat16
I32 = jnp.int32
HIGHEST = lax.Precision.HIGHEST

D = 1024
N_PROMPT, L_PROMPT = 16, 256
N_SAMPLE, L_SAMPLE = 4, 1024
N_SEQ = N_PROMPT + N_SAMPLE
PAST = 256
T_PROMPT = N_PROMPT * L_PROMPT
T_SAMPLE = N_SAMPLE * L_SAMPLE
T = T_PROMPT + T_SAMPLE
TM = 256
NT = T // TM
PROMPT_TILES = T_PROMPT // TM
TILES_PER_SAMPLE = L_SAMPLE // TM
N_MOD = 6
HALF = 512
B_COLS = 1792
HEAD = 64
W_DECAY_SCALE = math.exp(-0.5)
RWKV_GN_EPS = 64e-5
RW_CHUNK = 64
RET_CHUNK = 128
RET_EXP = ((5.0, 7.0, 9.0, 11.0), (6.0, 8.0, 10.0, 12.0))
N_EXPERTS = 32
TOP_K = 4
SWIGLU_LIMIT = 7.0
SWIGLU_ALPHA = 1.702
N_ASSIGN = T * TOP_K
TMX = 512
MOE_TILES = N_ASSIGN // TMX + N_EXPERTS
R_PAD = MOE_TILES * TMX
LANES = 128

NN = (((1,), (0,)), ((), ()))
NT_DIMS = (((1,), (1,)), ((), ()))
TN = (((0,), (0,)), ((), ()))


def _group(i):
    return jnp.where(i < PROMPT_TILES, 0, 1 + (i - PROMPT_TILES) // TILES_PER_SAMPLE)


def _mm(a, b, dims=NN, passes=1):
    dg = functools.partial(lax.dot_general, dimension_numbers=dims, preferred_element_type=F32)
    if passes == 1:
        return dg(a.astype(BF16), b.astype(BF16))
    a = a.astype(F32)
    b = b.astype(F32)
    ah = a.astype(BF16)
    al = (a - ah.astype(F32)).astype(BF16)
    bh = b.astype(BF16)
    bl = (b - bh.astype(F32)).astype(BF16)
    if dims[0][0] == (1,):
        m = a.shape[0]
        both = dg(jnp.concatenate([ah, al], axis=0), bh)
        return both[0:m] + (dg(ah, bl) + both[m:2 * m])
    return dg(ah, bh) + (dg(ah, bl) + dg(al, bh))


def _group_sum(x, ones_bd):
    xh = x.astype(BF16)
    xl = (x - xh.astype(F32)).astype(BF16)
    return (jnp.dot(xh, ones_bd, preferred_element_type=F32)
            + jnp.dot(xl, ones_bd, preferred_element_type=F32))


def _full(shape):
    nd = len(shape)
    return pl.BlockSpec(shape, lambda *_: (0,) * nd)


def _params(sem, vmem_mb=None):
    kw = {}
    if vmem_mb is not None:
        kw["vmem_limit_bytes"] = vmem_mb * 1024 * 1024
    return pltpu.CompilerParams(dimension_semantics=sem, **kw)


def _seq_tables(chunk):
    blk_f, blk_b, first, last, seq = [], [], [], [], []
    row = 0
    for s in range(N_SEQ):
        n = (L_PROMPT if s < N_PROMPT else L_SAMPLE) // chunk
        base = row // chunk
        for j in range(n):
            blk_f.append(base + j)
            blk_b.append(base + n - 1 - j)
            first.append(int(j == 0))
            last.append(int(j == n - 1))
            seq.append(s)
        row += n * chunk
    return tuple(np.asarray(a, np.int32) for a in (blk_f, blk_b, first, last, seq))


def _adaln_kernel(c_ref, w_ref, b_ref, o_ref):
    c = c_ref[...]
    s = c * jax.nn.sigmoid(c)
    o_ref[0] = jnp.dot(s, w_ref[0], precision=HIGHEST, preferred_element_type=F32) + b_ref[0]


def _adaln(cvec8, ada_w, ada_b):
    depth, _, n = ada_w.shape
    bn = 1536
    return pl.pallas_call(
        _adaln_kernel,
        out_shape=jax.ShapeDtypeStruct((depth, 8, n), F32),
        grid=(depth, n // bn),
        in_specs=[pl.BlockSpec((8, D), lambda l, j: (0, 0)),
                  pl.BlockSpec((1, D, bn), lambda l, j: (l, 0, j)),
                  pl.BlockSpec((1, 1, bn), lambda l, j: (l, 0, j))],
        out_specs=pl.BlockSpec((1, 8, bn), lambda l, j: (l, 0, j)),
        compiler_params=_params(("arbitrary", "arbitrary"), 40),
        name="adaln",
    )(cvec8, ada_w, ada_b.reshape(depth, 1, n))


def _in_kernel(x_ref, g_ref, mod_ref, w_ref, *outs, splits):
    x = x_ref[...]
    mod = mod_ref[0]
    y = x * lax.rsqrt(jnp.mean(x * x, axis=-1, keepdims=True) + 1e-6) * g_ref[...]
    h = (y * (1.0 + mod[:, D:2 * D]) + mod[:, 0:D]).astype(BF16)
    off = 0
    for o_ref, n in zip(outs, splits):
        o_ref[...] = jnp.dot(h, w_ref[:, off:off + n], preferred_element_type=F32)
        off += n


def _in_proj(x, g, mod, w_bf16, splits):
    n = w_bf16.shape[1]
    row = lambda i: (i, 0)
    return pl.pallas_call(
        functools.partial(_in_kernel, splits=splits),
        out_shape=[jax.ShapeDtypeStruct((T, s), F32) for s in splits],
        grid=(NT,),
        in_specs=[pl.BlockSpec((TM, D), row), _full((1, D)),
                  pl.BlockSpec((1, 1, N_MOD * D), lambda i: (_group(i), 0, 0)), _full((D, n))],
        out_specs=[pl.BlockSpec((TM, s), row) for s in splits],
        compiler_params=_params(("arbitrary",), 48),
        name="in_proj",
    )(x, g.reshape(1, D), mod, w_bf16)


def _gelu(x):
    return 0.5 * x * (1.0 + lax.erf(x * (1.0 / math.sqrt(2.0))))


def _sgu_kernel(za_ref, lng_ref, ws_ref, bs_ref, o_ref):
    u = _gelu(za_ref[:, 0:HALF])
    va = _gelu(za_ref[:, HALF:2 * HALF])
    mu = jnp.mean(va, axis=-1, keepdims=True)
    dv = va - mu
    var = jnp.mean(dv * dv, axis=-1, keepdims=True)
    vn = dv * lax.rsqrt(var + 1e-5) * lng_ref[...]
    lane = lax.broadcasted_iota(I32, (LANES, LANES), 1)
    first = lane < HEAD
    for c in range(TM // LANES):
        rows = slice(c * LANES, (c + 1) * LANES)
        for p in range(HALF // LANES):
            cols = slice(p * LANES, (p + 1) * LANES)
            vp = vn[rows, cols]
            s = (jnp.dot(ws_ref[2 * p], jnp.where(first, vp, 0.0).astype(BF16), preferred_element_type=F32)
                 + jnp.dot(ws_ref[2 * p + 1], jnp.where(first, 0.0, vp).astype(BF16), preferred_element_type=F32))
            o_ref[rows, cols] = u[rows, cols] * (s + bs_ref[:, cols])


def _sgu(za, ln_g, w_s_bf16, bs_full):
    return pl.pallas_call(
        _sgu_kernel,
        out_shape=jax.ShapeDtypeStruct((T, HALF), F32),
        grid=(NT,),
        in_specs=[pl.BlockSpec((TM, 2 * HALF), lambda i: (i, 0)), _full((1, HALF)),
                  _full((8, LANES, LANES)), _full((LANES, HALF))],
        out_specs=pl.BlockSpec((TM, HALF), lambda i: (i, 0)),
        compiler_params=_params(("arbitrary",)),
        name="sgu",
    )(za, ln_g.reshape(1, HALF), w_s_bf16, bs_full)


def _rwkv_prep_kernel(zb_ref, zp_ref, zn_ref, mu_ref, kk_ref, ka_ref, rk_ref, w0_ref, a0_ref,
                      wup_ref, aup_ref, gup_ref, ones_ref,
                      r_ref, v_ref, kkn_ref, bonus_ref, g_ref, lw_ref, kt_ref, b_ref):
    i = pl.program_id(0)
    in_sample = i >= PROMPT_TILES
    pos = (i - PROMPT_TILES) % TILES_PER_SAMPLE
    is_first = jnp.logical_or(jnp.logical_not(in_sample), pos == 0)
    is_last = jnp.logical_or(jnp.logical_not(in_sample), pos == TILES_PER_SAMPLE - 1)
    zb = zb_ref[...]
    prev_row = jnp.where(is_first, 0.0, zp_ref[7:8, :])
    next_row = jnp.where(is_last, 0.0, zn_ref[0:1, :])
    rowid = lax.broadcasted_iota(I32, (TM, 1), 0)
    zp = jnp.where(rowid == 0, prev_row, pltpu.roll(zb, 1, 0))
    zn = jnp.where(rowid == TM - 1, next_row, pltpu.roll(zb, TM - 1, 0))
    zs = zb + mu_ref[0:1, :] * (zp - zb) + mu_ref[1:2, :] * (zn - zb)
    r = zs[:, 0:HALF]
    k = zs[:, HALF:2 * HALF]
    v = zs[:, 2 * HALF:3 * HALF]
    wa = zs[:, 3 * HALF:3 * HALF + LANES]
    gd = zs[:, 3 * HALF + LANES:B_COLS]
    ones_bd = ones_ref[...]
    r_ref[...] = r
    v_ref[...] = v
    g_ref[...] = jnp.dot(jax.nn.sigmoid(gd).astype(BF16), gup_ref[...], preferred_element_type=F32)
    kk = k * kk_ref[...]
    kkn = kk / jnp.maximum(jnp.sqrt(_group_sum(kk * kk, ones_bd)), 1e-6)
    kkn_ref[...] = kkn
    bonus_ref[...] = _group_sum(r * k * rk_ref[...], ones_bd) * v
    tw = jnp.tanh(wa).astype(BF16)
    wa16 = wa.astype(BF16)
    for dd in range(2):
        lw_ref[dd] = -W_DECAY_SCALE * jax.nn.sigmoid(
            w0_ref[dd:dd + 1, :] + jnp.dot(tw, wup_ref[dd], preferred_element_type=F32))
        a = jax.nn.sigmoid(a0_ref[dd:dd + 1, :] + jnp.dot(wa16, aup_ref[dd], preferred_element_type=F32))
        kt_ref[dd] = k * (1.0 + (a - 1.0) * ka_ref[...])
        b_ref[dd] = a * kkn


def _rwkv_prep(zb, mu, k_k, k_a, r_k, w0, a0, wup_pad, aup_pad, g_up, ones_bd):
    row = lambda i: (i, 0)
    halo = TM // 8
    one = jax.ShapeDtypeStruct((T, HALF), F32)
    two = jax.ShapeDtypeStruct((2, T, HALF), F32)
    o1 = pl.BlockSpec((TM, HALF), row)
    o2 = pl.BlockSpec((2, TM, HALF), lambda i: (0, i, 0))
    return pl.pallas_call(
        _rwkv_prep_kernel,
        out_shape=[one, one, one, one, one, two, two, two],
        grid=(NT,),
        in_specs=[pl.BlockSpec((TM, B_COLS), row),
                  pl.BlockSpec((8, B_COLS), lambda i: (jnp.maximum(i * halo - 1, 0), 0)),
                  pl.BlockSpec((8, B_COLS), lambda i: (jnp.minimum((i + 1) * halo, T // 8 - 1), 0)),
                  _full((2, B_COLS)), _full((1, HALF)), _full((1, HALF)), _full((1, HALF)),
                  _full((2, HALF)), _full((2, HALF)),
                  _full((2, LANES, HALF)), _full((2, LANES, HALF)), _full((LANES, HALF)),
                  _full((HALF, HALF))],
        out_specs=[o1, o1, o1, o1, o1, o2, o2, o2],
        compiler_params=_params(("arbitrary",), 48),
        name="rwkv_prep",
    )(zb, zb, zb, mu, k_k.reshape(1, HALF), k_a.reshape(1, HALF), r_k.reshape(1, HALF), w0, a0,
      wup_pad, aup_pad, g_up, ones_bd)


def _rwkv_chunks(dirs):
    C = RW_CHUNK
    ti = lax.broadcasted_iota(I32, (C, C), 0)
    tj = lax.broadcasted_iota(I32, (C, C), 1)
    bi = lax.broadcasted_iota(I32, (LANES, LANES), 0)
    bj = lax.broadcasted_iota(I32, (LANES, LANES), 1)
    same = (bi >> 6) == (bj >> 6)
    pi = bi & (C - 1)
    pj = bj & (C - 1)
    eye = (bi == bj).astype(F32)
    h0 = lax.broadcasted_iota(I32, (C, LANES), 1) < HEAD

    def stack(x):
        return jnp.concatenate([jnp.where(h0, x, 0.0), jnp.where(h0, 0.0, x)], axis=0)

    def fold(x):
        return x[0:C] + x[C:2 * C]

    chains = []
    for rev, r, v, kk, lw, kt, b, s_ref, y_ref in dirs:
        tri = jnp.where((tj >= ti) if rev else (tj <= ti), 1.0, 0.0).astype(F32)
        p1 = lw.astype(BF16)
        r1 = lw - p1.astype(F32)
        p2 = r1.astype(BF16)
        p3 = (r1 - p2.astype(F32)).astype(BF16)
        cs3 = jnp.dot(tri.astype(BF16), jnp.concatenate([p1, p2, p3], axis=1), preferred_element_type=F32)
        cs = cs3[:, 0:HALF] + (cs3[:, HALF:2 * HALF] + cs3[:, 2 * HALF:3 * HALF])
        ctot = cs[0:1, :] if rev else cs[C - 1:C, :]
        e_neg = jnp.exp(-cs)
        e_tail = jnp.exp(ctot - cs)
        q1 = kk * jnp.exp(cs - lw)
        k1 = kt * e_neg
        b1 = b * e_neg
        r1 = r * jnp.exp(cs)
        k2 = kt * e_tail
        b2 = b * e_tail
        e_tot = jnp.exp(ctot)
        strict = jnp.logical_and(same, (pj > pi) if rev else (pj < pi))
        incl = jnp.logical_and(same, (pj >= pi) if rev else (pj <= pi))
        for p in range(HALF // LANES):
            cols = slice(p * LANES, (p + 1) * LANES)
            chains.append(dict(p=p, cols=cols, strict=strict, incl=incl, s_ref=s_ref, y_ref=y_ref,
                               q1=q1[:, cols], k1=k1[:, cols], b1=b1[:, cols], r1=r1[:, cols],
                               k2=k2[:, cols], b2=b2[:, cols], v=v[:, cols], e_tot=e_tot[:, cols]))

    for ch in chains:
        lhs = jnp.concatenate([stack(ch["q1"]), stack(ch["r1"])], axis=0)
        rhs = jnp.concatenate([ch["k1"], ch["k1"], ch["b1"], ch["b1"]], axis=0)
        gm = _mm(lhs, rhs, NT_DIMS, 3)
        ch["mk"] = jnp.where(ch["strict"], gm[0:2 * C, 0:2 * C], 0.0)
        ch["mb"] = jnp.where(ch["strict"], gm[0:2 * C, 2 * C:4 * C], 0.0)
        ch["nk"] = jnp.where(ch["incl"], gm[2 * C:4 * C, 0:2 * C], 0.0)
        ch["nb"] = jnp.where(ch["incl"], gm[2 * C:4 * C, 2 * C:4 * C], 0.0)
        ch["tinv"] = eye - jnp.where((pi >> 1) == (pj >> 1), ch["mb"], 0.0)
    size = 2
    while size < C:
        sh = size.bit_length() - 1
        blk = jnp.logical_and((pi >> (sh + 1)) == (pj >> (sh + 1)), (pi >> sh) != (pj >> sh))
        for ch in chains:
            ch["tn"] = _mm(ch["tinv"], jnp.where(blk, ch["mb"], 0.0), NN, 3)
        for ch in chains:
            ch["tinv"] = ch["tinv"] - _mm(ch["tn"], ch["tinv"], NN, 3)
        size *= 2
    for ch in chains:
        vst = stack(ch["v"])
        ch["mkv"] = fold(_mm(ch["mk"], vst, NN, 3))
        ch["nkv"] = fold(_mm(ch["nk"], vst, NN, 3))
        ch["s"] = ch["s_ref"][ch["p"]]
        ch["qr"] = _mm(jnp.concatenate([ch["q1"], ch["r1"]], axis=0), ch["s"], NT_DIMS, 3)
    for ch in chains:
        ch["u"] = fold(_mm(ch["tinv"], stack(ch["mkv"] + ch["qr"][0:C]), NN, 3))
    for ch in chains:
        ch["y_ref"][:, ch["cols"]] = ch["qr"][C:2 * C] + ch["nkv"] - fold(_mm(ch["nb"], stack(ch["u"]), NN, 3))
        upd = _mm(jnp.concatenate([ch["v"], ch["u"]], axis=0),
                  jnp.concatenate([ch["k2"], -ch["b2"]], axis=0), TN, 3)
        ch["s_ref"][ch["p"]] = ch["s"] * ch["e_tot"] + jnp.where(same, upd, 0.0)


def _rwkv_scan_kernel(bf_ref, bb_ref, first_ref, last_ref, seq_ref,
                      rf_ref, vf_ref, kkf_ref, lwf_ref, ktf_ref, bfw_ref,
                      rb_ref, vb_ref, kkb_ref, lwb_ref, ktb_ref, bbw_ref, s0_ref,
                      yf_ref, yb_ref, sfin_ref, s_ref):
    step = pl.program_id(0)

    @pl.when(first_ref[step] == 1)
    def _():
        s_ref[...] = s0_ref[:, 0]

    _rwkv_chunks([
        (False, rf_ref[...], vf_ref[...], kkf_ref[...], lwf_ref[0], ktf_ref[0], bfw_ref[0], s_ref.at[0], yf_ref),
        (True, rb_ref[...], vb_ref[...], kkb_ref[...], lwb_ref[0], ktb_ref[0], bbw_ref[0], s_ref.at[1], yb_ref)])

    @pl.when(last_ref[step] == 1)
    def _():
        sfin_ref[:, 0] = s_ref[...]


def _rwkv_scan(r, v, kk, lw, kt, b, s0_bd):
    C = RW_CHUNK
    tabs = _seq_tables(C)
    fwd = lambda i, bf, bb, fi, la, sq: (bf[i], 0)
    bwd = lambda i, bf, bb, fi, la, sq: (bb[i], 0)
    fwd3 = lambda i, bf, bb, fi, la, sq: (0, bf[i], 0)
    bwd3 = lambda i, bf, bb, fi, la, sq: (1, bb[i], 0)
    st = pl.BlockSpec((2, 1, 4, LANES, LANES), lambda i, bf, bb, fi, la, sq: (0, sq[i], 0, 0, 0))
    one_f, one_b = pl.BlockSpec((C, HALF), fwd), pl.BlockSpec((C, HALF), bwd)
    two_f, two_b = pl.BlockSpec((1, C, HALF), fwd3), pl.BlockSpec((1, C, HALF), bwd3)
    return pl.pallas_call(
        _rwkv_scan_kernel,
        out_shape=[jax.ShapeDtypeStruct((T, HALF), F32), jax.ShapeDtypeStruct((T, HALF), F32),
                   jax.ShapeDtypeStruct((2, N_SEQ, 4, LANES, LANES), F32)],
        grid_spec=pltpu.PrefetchScalarGridSpec(
            num_scalar_prefetch=5, grid=(len(tabs[0]),),
            in_specs=[one_f, one_f, one_f, two_f, two_f, two_f,
                      one_b, one_b, one_b, two_b, two_b, two_b, st],
            out_specs=[one_f, one_b, st],
            scratch_shapes=[pltpu.VMEM((2, 4, LANES, LANES), F32)]),
        compiler_params=_params(("arbitrary",)),
        name="rwkv_scan",
    )(*tabs, r, v, kk, lw, kt, b, r, v, kk, lw, kt, b, s0_bd)


def _rope(x, cos, sin_signed, first16):
    w = x.shape[1]
    partner = jnp.where(first16, pltpu.roll(x, w - 16, 1), pltpu.roll(x, 16, 1))
    return x * cos + partner * sin_signed


def _odd_prep_kernel(zc_ref, zr_ref, cos_ref, sin_ref, qkg_ref, ones_ref, cq_ref, ck_ref, ckraw_ref, rqk_ref):
    ones_bd = ones_ref[...]
    cos = cos_ref[...]
    sin = sin_ref[...]
    lane = lax.broadcasted_iota(I32, (TM, HALF), 1)
    first16 = (lane & 31) < 16
    for idx, (o_ref, raw_ref) in enumerate(((cq_ref, None), (ck_ref, ckraw_ref))):
        x = zc_ref[:, idx * HALF:(idx + 1) * HALF]
        ms = _group_sum(x * x, ones_bd) * (1.0 / HEAD)
        xn = x * lax.rsqrt(ms + 1e-6) * qkg_ref[idx:idx + 1, :]
        if raw_ref is not None:
            raw_ref[...] = xn
        o_ref[...] = _rope(xn, cos, sin, first16)
    rqk = _rope(zr_ref[...], cos, sin, first16)
    rqk_ref[...] = jnp.where(lane < HALF // 2, rqk * (HEAD ** -0.5), rqk)


def _odd_prep(zc, zr, cos_tab, sin_tab, qkg_tiled, ones_bd):
    row = lambda i: (i, 0)
    tab = lambda i: (jnp.where(i < PROMPT_TILES, 0, 1 + (i - PROMPT_TILES) % TILES_PER_SAMPLE), 0)
    one = jax.ShapeDtypeStruct((T, HALF), F32)
    o1 = pl.BlockSpec((TM, HALF), row)
    return pl.pallas_call(
        _odd_prep_kernel,
        out_shape=[one, one, one, one], grid=(NT,),
        in_specs=[pl.BlockSpec((TM, 2 * HALF), row), pl.BlockSpec((TM, HALF), row),
                  pl.BlockSpec((TM, HALF), tab), pl.BlockSpec((TM, HALF), tab),
                  _full((2, HALF)), _full((HALF, HALF))],
        out_specs=[o1, o1, o1, o1],
        compiler_params=_params(("arbitrary",)),
        name="odd_prep",
    )(zc, zr, cos_tab, sin_tab, qkg_tiled, ones_bd)


def _attn_kernel(*refs, has_ctx, one_minus_li):
    if has_ctx:
        q_ref, k_ref, v_ref, kc_ref, vc_ref, lam_ref, sg_ref, _, o_ref = refs
    else:
        q_ref, k_ref, v_ref, lam_ref, sg_ref, _, o_ref = refs
    lam = lam_ref[...]
    lane = lax.broadcasted_iota(I32, (LANES, LANES), 1)
    m0 = lane < HEAD
    scale = HEAD ** -0.5
    for h in range(4):
        cols = slice(h * LANES, (h + 1) * LANES)
        qp = q_ref[:, cols]
        segs = [(k_ref[:, cols], v_ref[:, cols])]
        if has_ctx:
            segs.append((kc_ref[0, :, cols], vc_ref[0, :, cols]))
        outs = []
        for qm in (jnp.where(m0, qp, 0.0), jnp.where(m0, 0.0, qp)):
            qm16 = qm.astype(BF16)
            ss = [lax.dot_general(qm16, ks.astype(BF16), NT_DIMS, preferred_element_type=F32) * scale
                  for ks, _ in segs]
            mx = ss[0].max(axis=-1, keepdims=True)
            for s_ in ss[1:]:
                mx = jnp.maximum(mx, s_.max(axis=-1, keepdims=True))
            ps = [jnp.exp(s_ - mx) for s_ in ss]
            den = ps[0].sum(axis=-1, keepdims=True)
            for p_ in ps[1:]:
                den = den + p_.sum(axis=-1, keepdims=True)
            outs.append([p_ / den for p_ in ps])
        acc = None
        for si, (_, vs) in enumerate(segs):
            amap = outs[0][si] - lam * outs[1][si]
            t = jnp.dot(amap.astype(BF16), vs.astype(BF16), preferred_element_type=F32)
            acc = t if acc is None else acc + t
        nrm = acc * lax.rsqrt(jnp.mean(acc * acc, axis=-1, keepdims=True) + 1e-6) * sg_ref[...]
        o_ref[:, cols] = nrm * one_minus_li


def _attn(cq, ck, zc, row0, n_seq, seq_len, lam, subln_g, one_minus_li, prev, ctx_k=None, ctx_v=None):
    nq = seq_len // LANES
    qb0 = row0 // LANES
    sb0 = row0 // seq_len
    in_specs = [pl.BlockSpec((LANES, HALF), lambda s, q: (qb0 + s * nq + q, 0)),
                pl.BlockSpec((seq_len, HALF), lambda s, q: (sb0 + s, 0)),
                pl.BlockSpec((seq_len, HALF), lambda s, q: (sb0 + s, 2))]
    args = [cq, ck, zc]
    if ctx_k is not None:
        in_specs += [pl.BlockSpec((1, PAST, HALF), lambda s, q: (s, 0, 0))] * 2
        args += [ctx_k, ctx_v]
    in_specs += [_full((1, 1)), _full((1, LANES))]
    args += [lam.reshape(1, 1), subln_g.reshape(1, LANES)]
    in_specs.append(pl.BlockSpec(memory_space=pl.ANY))
    args.append(prev)
    aliases = {len(args) - 1: 0}
    return pl.pallas_call(
        functools.partial(_attn_kernel, has_ctx=ctx_k is not None, one_minus_li=one_minus_li),
        out_shape=jax.ShapeDtypeStruct((T, HALF), F32),
        grid=(n_seq, nq), in_specs=in_specs,
        out_specs=pl.BlockSpec((LANES, HALF), lambda s, q: (qb0 + s * nq + q, 0)),
        input_output_aliases=aliases,
        compiler_params=_params(("arbitrary", "arbitrary"), 48),
        name="diff_attn",
    )(*args)


_LOG_GAMMA = tuple(tuple(float(np.log1p(-np.exp2(-np.float32(e)), dtype=np.float32)) for e in es)
                   for es in RET_EXP)


def _ret_chunk(rev, qk_ref, v_ref, s_ref, o_ref):
    C = RET_CHUNK
    ii = lax.broadcasted_iota(I32, (C, C), 0)
    jj = lax.broadcasted_iota(I32, (C, C), 1)
    mask = (jj > ii) if rev else (jj <= ii)
    dist = jnp.where(mask, (jj - ii) if rev else (ii - jj), 0).astype(F32)
    ri = lax.broadcasted_iota(I32, (C, 1), 0)
    kpow = (ri if rev else (C - 1 - ri)).astype(F32)
    qpow = ((C - ri) if rev else (ri + 1)).astype(F32)
    lane = lax.broadcasted_iota(I32, (C, LANES), 1)
    for h in range(4):
        lg = _LOG_GAMMA[1 if rev else 0][h]
        p = h // 2
        hm = (lane < HEAD) if h % 2 == 0 else (lane >= HEAD)
        qp = jnp.where(hm, qk_ref[:, p * LANES:(p + 1) * LANES], 0.0)
        kp = jnp.where(hm, qk_ref[:, HALF // 2 + p * LANES:HALF // 2 + (p + 1) * LANES], 0.0)
        vh = v_ref[:, h * LANES:(h + 1) * LANES].astype(BF16)
        decay = jnp.where(mask, jnp.exp(lg * dist), 0.0)
        sc = lax.dot_general(qp.astype(BF16), kp.astype(BF16), NT_DIMS, preferred_element_type=F32) * decay
        o_intra = jnp.dot(sc.astype(BF16), vh, preferred_element_type=F32)
        s_prev = s_ref[h]
        o_cross = jnp.dot((qp * jnp.exp(lg * qpow)).astype(BF16), s_prev.astype(BF16),
                          preferred_element_type=F32)
        o_ref[:, h * LANES:(h + 1) * LANES] = o_intra + o_cross
        kv = lax.dot_general((kp * jnp.exp(lg * kpow)).astype(BF16), vh, TN, preferred_element_type=F32)
        s_ref[h] = math.exp(lg * C) * s_prev + kv


def _ret_kernel(bf_ref, bb_ref, first_ref, last_ref, seq_ref,
                qkf_ref, vf_ref, qkb_ref, vb_ref, s0_ref, of_ref, ob_ref, sfin_ref, s_ref):
    step = pl.program_id(0)

    @pl.when(first_ref[step] == 1)
    def _():
        s_ref[...] = s0_ref[:, 0]

    _ret_chunk(False, qkf_ref, vf_ref, s_ref.at[0], of_ref)
    _ret_chunk(True, qkb_ref, vb_ref, s_ref.at[1], ob_ref)

    @pl.when(last_ref[step] == 1)
    def _():
        sfin_ref[:, 0] = s_ref[...]


def _retention(rqk, zr, s0):
    C = RET_CHUNK
    tabs = _seq_tables(C)
    st = pl.BlockSpec((2, 1, 4, LANES, LANES), lambda i, bf, bb, fi, la, sq: (0, sq[i], 0, 0, 0))
    spec = lambda use_b, col: pl.BlockSpec(
        (C, HALF), lambda i, bf, bb, fi, la, sq: ((bb if use_b else bf)[i], col))
    return pl.pallas_call(
        _ret_kernel,
        out_shape=[jax.ShapeDtypeStruct((T, HALF), F32), jax.ShapeDtypeStruct((T, HALF), F32),
                   jax.ShapeDtypeStruct((2, N_SEQ, 4, LANES, LANES), F32)],
        grid_spec=pltpu.PrefetchScalarGridSpec(
            num_scalar_prefetch=5, grid=(len(tabs[0]),),
            in_specs=[spec(False, 0), spec(False, 1), spec(True, 0), spec(True, 1), st],
            out_specs=[spec(False, 0), spec(True, 0), st],
            scratch_shapes=[pltpu.VMEM((2, 4, LANES, LANES), F32)]),
        compiler_params=_params(("arbitrary",)),
        name="retention",
    )(*tabs, rqk, zr, rqk, zr, s0)


def _out_kernel(*refs, even):
    if even:
        (a_ref, yf_ref, yb_ref, bonus_ref, g_ref, gng_ref, gnb_ref, ones_ref,
         x_ref, mod_ref, ng_ref, wo_ref, rw_ref, rb_ref,
         y_ref, xp_ref, ti_ref, tg_ref, rk_ref, cnt_ref, run_ref) = refs
        ones_bd = ones_ref[...]
        ys = yf_ref[...] + yb_ref[...]
        mu = _group_sum(ys, ones_bd) * (1.0 / HEAD)
        dv = ys - mu
        var = _group_sum(dv * dv, ones_bd) * (1.0 / HEAD)
        yn = dv * lax.rsqrt(var + RWKV_GN_EPS) * gng_ref[...] + gnb_ref[...]
        left = a_ref[...]
        right = (yn + bonus_ref[...]) * g_ref[...]
    else:
        (c_ref, of_ref, ob_ref, rg_ref, gng_ref,
         x_ref, mod_ref, ng_ref, wo_ref, rw_ref, rb_ref,
         y_ref, xp_ref, ti_ref, tg_ref, rk_ref, cnt_ref, run_ref) = refs
        left = c_ref[...]
        rg = rg_ref[...]
        gate = rg * jax.nn.sigmoid(rg)
        os_ = of_ref[...] + ob_ref[...]
        parts = []
        for h in range(4):
            oh = os_[:, h * LANES:(h + 1) * LANES]
            mu = jnp.mean(oh, axis=-1, keepdims=True)
            dv = oh - mu
            var = jnp.mean(dv * dv, axis=-1, keepdims=True)
            parts.append(dv * lax.rsqrt(var + 1e-5))
        right = gate * (jnp.concatenate(parts, axis=1) * gng_ref[...])
    mod = mod_ref[0]
    o = (jnp.dot(left.astype(BF16), wo_ref[0:HALF, :], preferred_element_type=F32)
         + jnp.dot(right.astype(BF16), wo_ref[HALF:2 * HALF, :], preferred_element_type=F32))
    y = x_ref[...] + mod[:, 2 * D:3 * D] * o
    y_ref[...] = y
    yn2 = y * lax.rsqrt(jnp.mean(y * y, axis=-1, keepdims=True) + 1e-6) * ng_ref[...]
    t = yn2 * (1.0 + mod[:, 4 * D:5 * D]) + mod[:, 3 * D:4 * D]
    xp_ref[...] = t
    logits = _mm(t, rw_ref[...], NN, 3) + rb_ref[...]
    lane = lax.broadcasted_iota(I32, (TM, LANES), 1)
    neg = jnp.float32(-jnp.inf)
    lg = jnp.where(lane < N_EXPERTS, logits, neg)
    vals, hits = [], []
    for _ in range(TOP_K):
        m = jnp.max(lg, axis=-1, keepdims=True)
        ix = jnp.min(jnp.where(lg == m, lane, LANES), axis=-1, keepdims=True)
        hit = lane == ix
        vals.append(m)
        hits.append((ix, hit))
        lg = jnp.where(hit, neg, lg)
    es = [jnp.exp(vv - vals[0]) for vv in vals]
    den = es[0] + es[1] + es[2] + es[3]

    @pl.when(pl.program_id(0) == 0)
    def _():
        run_ref[...] = jnp.zeros_like(run_ref)

    member = jnp.zeros((TM, LANES), F32)
    for _, hit in hits:
        member = member + jnp.where(hit, 1.0, 0.0)
    ri = lax.broadcasted_iota(I32, (TM, TM), 0)
    ci = lax.broadcasted_iota(I32, (TM, TM), 1)
    before = jnp.where(ci < ri, 1.0, 0.0).astype(BF16)
    seen = run_ref[...] + jnp.dot(before, member.astype(BF16), preferred_element_type=F32)
    ti = jnp.zeros((TM, LANES), I32)
    tg = jnp.zeros((TM, LANES), F32)
    rk = jnp.zeros((TM, LANES), F32)
    for kk, (ix, hit) in enumerate(hits):
        ti = jnp.where(lane == kk, ix, ti)
        tg = jnp.where(lane == kk, es[kk] / den, tg)
        rk = jnp.where(lane == kk, jnp.sum(jnp.where(hit, seen, 0.0), axis=-1, keepdims=True), rk)
    ti_ref[...] = ti
    tg_ref[...] = tg
    rk_ref[...] = rk.astype(I32)
    run_ref[...] = run_ref[...] + jnp.sum(member, axis=0, keepdims=True)
    cnt_ref[...] = run_ref[...]


def _out_proj(even, mix_args, mix_specs, x, mod, norm_g, w_out_bf16, rw_pad, rb_pad):
    row = lambda i: (i, 0)
    modspec = pl.BlockSpec((1, 1, N_MOD * D), lambda i: (_group(i), 0, 0))
    in_specs = list(mix_specs) + [pl.BlockSpec((TM, D), row), modspec, _full((1, D)), _full((D, D)),
                                  _full((D, LANES)), _full((1, LANES))]
    args = list(mix_args) + [x, mod, norm_g.reshape(1, D), w_out_bf16, rw_pad, rb_pad]
    lane_i = jax.ShapeDtypeStruct((T, LANES), I32)
    lane_spec = pl.BlockSpec((TM, LANES), row)
    return pl.pallas_call(
        functools.partial(_out_kernel, even=even),
        out_shape=[jax.ShapeDtypeStruct((T, D), F32), jax.ShapeDtypeStruct((T, D), F32),
                   lane_i, jax.ShapeDtypeStruct((T, LANES), F32), lane_i,
                   jax.ShapeDtypeStruct((1, LANES), F32)],
        grid=(NT,), in_specs=in_specs,
        out_specs=[pl.BlockSpec((TM, D), row), pl.BlockSpec((TM, D), row),
                   lane_spec, lane_spec, lane_spec, _full((1, LANES))],
        scratch_shapes=[pltpu.VMEM((1, LANES), F32)],
        compiler_params=_params(("arbitrary",), 48),
        name="out_proj",
    )(*args)


def _route_kernel(cnt_ref, ti_ref, rk_ref, dest_ref, te_ref, nt_ref):
    cnt = cnt_ref[...].astype(I32)
    ntile = lax.shift_right_logical(cnt + (TMX - 1), TMX.bit_length() - 1)
    ei = lax.broadcasted_iota(I32, (LANES, LANES), 0)
    ej = lax.broadcasted_iota(I32, (LANES, LANES), 1)
    upto = jnp.where(ei <= ej, 1.0, 0.0).astype(BF16)
    ntile_f = jnp.broadcast_to(ntile.astype(F32), (8, LANES))
    tile_end = jnp.dot(ntile_f.astype(BF16), upto, preferred_element_type=F32)[0:1, :]
    row_start = (tile_end - ntile.astype(F32)) * float(TMX)
    lane = lax.broadcasted_iota(I32, (TM, LANES), 1)
    ti = ti_ref[...]
    rk = rk_ref[...]
    dest = jnp.zeros((TM, LANES), F32)
    for k in range(TOP_K):
        hit = lane == ti[:, k:k + 1]
        start = jnp.sum(jnp.where(hit, row_start, 0.0), axis=-1, keepdims=True)
        dest = jnp.where(lane == k, start + rk[:, k:k + 1].astype(F32), dest)
    dest_ref[...] = dest.astype(I32)

    @pl.when(pl.program_id(0) == 0)
    def _():
        lane1 = lax.broadcasted_iota(I32, (1, LANES), 1)
        n_tiles = jnp.max(tile_end, axis=-1, keepdims=True)
        last_e = jnp.max(jnp.where(cnt > 0, lane1, 0), axis=-1, keepdims=True)
        tile = lax.broadcasted_iota(I32, (TM, 1), 0).astype(F32)
        te = jnp.sum(jnp.where(tile_end <= tile, 1, 0), axis=-1, keepdims=True)
        te = jnp.where(tile < n_tiles, te, last_e)
        te_ref[...] = jnp.broadcast_to(te, (TM, LANES)).astype(I32)
        first_row = lax.broadcasted_iota(I32, (8, LANES), 0) == 0
        nt_ref[...] = jnp.where(first_row, n_tiles, tile_end).astype(I32)


def _route(cnt, ti, rk):
    row = lambda i: (i, 0)
    return pl.pallas_call(
        _route_kernel,
        out_shape=[jax.ShapeDtypeStruct((T, LANES), I32), jax.ShapeDtypeStruct((TM, LANES), I32),
                   jax.ShapeDtypeStruct((8, LANES), I32)],
        grid=(NT,),
        in_specs=[_full((1, LANES)), pl.BlockSpec((TM, LANES), row), pl.BlockSpec((TM, LANES), row)],
        out_specs=[pl.BlockSpec((TM, LANES), row), _full((TM, LANES)), _full((8, LANES))],
        compiler_params=_params(("arbitrary",)),
        name="moe_route",
    )(cnt, ti, rk)


def _row_copy(src_ref, src_row, dst_ref, dst_row, sem):
    return pltpu.make_async_copy(src_ref.at[pl.ds(src_row, 1)], dst_ref.at[pl.ds(dst_row, 1)], sem)


def _wait_tiles(n, src_ref, dst_ref, sem):
    for _ in range(n):
        pltpu.make_async_copy(src_ref, dst_ref, sem).wait()


def _dispatch_kernel(dest_ref, tend_ref, x_ref, xs_ref, zero_ref, sem):
    i = pl.program_id(0)

    @pl.when(i == 0)
    def _():
        zero_ref[...] = jnp.zeros_like(zero_ref)

        def last_tile(e, fn):
            end = tend_ref[e]
            begin = tend_ref[e - 1] if e > 0 else 0

            @pl.when(end > begin)
            def _():
                fn(pltpu.make_async_copy(zero_ref, xs_ref.at[pl.ds((end - 1) * TMX, TMX)], sem))

        def unused_tile(j):
            return pltpu.make_async_copy(zero_ref, xs_ref.at[pl.ds(j * TMX, TMX)], sem)

        def start_unused(j, carry):
            unused_tile(j).start()
            return carry

        def wait_unused(j, carry):
            unused_tile(j).wait()
            return carry

        n_used = tend_ref[N_EXPERTS - 1]
        for e in range(N_EXPERTS):
            last_tile(e, lambda c: c.start())
        lax.fori_loop(n_used, MOE_TILES, start_unused, 0)
        for e in range(N_EXPERTS):
            last_tile(e, lambda c: c.wait())
        lax.fori_loop(n_used, MOE_TILES, wait_unused, 0)

    base = i * (TM * TOP_K)

    def start(r, carry):
        for k in range(TOP_K):
            _row_copy(x_ref, r, xs_ref, dest_ref[base + r * TOP_K + k], sem).start()
        return carry

    lax.fori_loop(0, TM, start, 0, unroll=4)
    _wait_tiles(TOP_K, x_ref, xs_ref.at[pl.ds(0, TM)], sem)


def _dispatch(dest_flat, tile_end, xt):
    return pl.pallas_call(
        _dispatch_kernel,
        out_shape=jax.ShapeDtypeStruct((R_PAD, D), F32),
        grid_spec=pltpu.PrefetchScalarGridSpec(
            num_scalar_prefetch=2, grid=(NT,),
            in_specs=[pl.BlockSpec((TM, D), lambda i, d, te: (i, 0))],
            out_specs=pl.BlockSpec(memory_space=pl.ANY),
            scratch_shapes=[pltpu.VMEM((TMX, D), F32), pltpu.SemaphoreType.DMA(())]),
        compiler_params=_params(("arbitrary",)),
        name="moe_dispatch",
    )(dest_flat, tile_end, xt)


def _moe_up_kernel(te_ref, nt_ref, xs_ref, wg0_ref, wg1_ref, wu0_ref, wu1_ref, b_ref, act_ref):
    i = pl.program_id(0)

    @pl.when(i < nt_ref[0])
    def _():
        x16 = xs_ref[...].astype(BF16)
        for h, (wg_ref, wu_ref) in enumerate(((wg0_ref, wu0_ref), (wg1_ref, wu1_ref))):
            gc = slice(h * HALF, (h + 1) * HALF)
            uc = slice(D + h * HALF, D + (h + 1) * HALF)
            g = jnp.dot(x16, wg_ref[0].astype(BF16), preferred_element_type=F32) + b_ref[0, :, gc]
            u = jnp.dot(x16, wu_ref[0].astype(BF16), preferred_element_type=F32) + b_ref[0, :, uc]
            gt = jnp.minimum(g, SWIGLU_LIMIT)
            up = jnp.clip(u, -SWIGLU_LIMIT, SWIGLU_LIMIT)
            act_ref[:, gc] = ((up + 1.0) * gt * jax.nn.sigmoid(SWIGLU_ALPHA * gt)).astype(BF16)

    @pl.when(i >= nt_ref[0])
    def _():
        act_ref[...] = jnp.zeros_like(act_ref)


def _tile_clamped(i, te, nt):
    return (jnp.minimum(i, jnp.maximum(nt[0] - 1, 0)), 0)


def _moe_up(te, n_tiles, xs, w_gu, b_gu):
    wcols = lambda j: pl.BlockSpec((1, D, HALF), lambda i, te, nt: (te[i], 0, j))
    return pl.pallas_call(
        _moe_up_kernel,
        out_shape=jax.ShapeDtypeStruct((R_PAD, D), BF16),
        grid_spec=pltpu.PrefetchScalarGridSpec(
            num_scalar_prefetch=2, grid=(MOE_TILES,),
            in_specs=[pl.BlockSpec((TMX, D), _tile_clamped),
                      wcols(0), wcols(1), wcols(2), wcols(3),
                      pl.BlockSpec((1, 1, 2 * D), lambda i, te, nt: (te[i], 0, 0))],
            out_specs=pl.BlockSpec((TMX, D), lambda i, te, nt: (i, 0))),
        compiler_params=_params(("arbitrary",), 56),
        name="moe_up",
    )(te, n_tiles, xs, w_gu, w_gu, w_gu, w_gu, b_gu)


def _moe_down_kernel(te_ref, nt_ref, act_ref, w0_ref, w1_ref, b_ref, y_ref):
    i = pl.program_id(0)

    @pl.when(i < nt_ref[0])
    def _():
        a = act_ref[...]
        for h, w_ref in enumerate((w0_ref, w1_ref)):
            cols = slice(h * HALF, (h + 1) * HALF)
            y_ref[:, cols] = jnp.dot(a, w_ref[0].astype(BF16), preferred_element_type=F32) + b_ref[0, :, cols]

    @pl.when(i >= nt_ref[0])
    def _():
        y_ref[...] = jnp.zeros_like(y_ref)


def _moe_down(te, n_tiles, act, w_dn, b_dn):
    return pl.pallas_call(
        _moe_down_kernel,
        out_shape=jax.ShapeDtypeStruct((R_PAD, D), F32),
        grid_spec=pltpu.PrefetchScalarGridSpec(
            num_scalar_prefetch=2, grid=(MOE_TILES,),
            in_specs=[pl.BlockSpec((TMX, D), _tile_clamped),
                      pl.BlockSpec((1, D, HALF), lambda i, te, nt: (te[i], 0, 0)),
                      pl.BlockSpec((1, D, HALF), lambda i, te, nt: (te[i], 0, 1)),
                      pl.BlockSpec((1, 1, D), lambda i, te, nt: (te[i], 0, 0))],
            out_specs=pl.BlockSpec((TMX, D), lambda i, te, nt: (i, 0))),
        compiler_params=_params(("arbitrary",), 40),
        name="moe_down",
    )(te, n_tiles, act, w_dn, w_dn, b_dn)


def _combine_kernel(dest_ref, x_ref, tg_ref, mod_ref, ys_ref, o_ref, buf_ref, sem):
    i = pl.program_id(0)
    slot = i % 2

    def gather(tile, b):
        base = tile * (TM * TOP_K)

        def body(r, carry):
            for k in range(TOP_K):
                _row_copy(ys_ref, dest_ref[base + r * TOP_K + k], buf_ref.at[b, k], r, sem.at[b]).start()
            return carry

        lax.fori_loop(0, TM, body, 0, unroll=4)

    @pl.when(i == 0)
    def _():
        gather(0, 0)

    @pl.when(i + 1 < NT)
    def _():
        gather(i + 1, 1 - slot)

    _wait_tiles(TOP_K, ys_ref.at[pl.ds(0, TM)], buf_ref.at[slot, 0], sem.at[slot])
    tg = tg_ref[...]
    f = tg[:, 0:1] * buf_ref[slot, 0]
    for k in range(1, TOP_K):
        f = f + tg[:, k:k + 1] * buf_ref[slot, k]
    o_ref[...] = x_ref[...] + mod_ref[0][:, 5 * D:6 * D] * f


def _combine(dest_flat, x, tg, mod, ys):
    row = lambda i, d: (i, 0)
    return pl.pallas_call(
        _combine_kernel,
        out_shape=jax.ShapeDtypeStruct((T, D), F32),
        grid_spec=pltpu.PrefetchScalarGridSpec(
            num_scalar_prefetch=1, grid=(NT,),
            in_specs=[pl.BlockSpec((TM, D), row), pl.BlockSpec((TM, LANES), row),
                      pl.BlockSpec((1, 1, N_MOD * D), lambda i, d: (_group(i), 0, 0)),
                      pl.BlockSpec(memory_space=pl.ANY)],
            out_specs=pl.BlockSpec((TM, D), row),
            scratch_shapes=[pltpu.VMEM((2, TOP_K, TM, D), F32), pltpu.SemaphoreType.DMA((2,))]),
        compiler_params=_params(("arbitrary",), 40),
        name="moe_combine",
    )(dest_flat, x, tg, mod, ys)


def _moe(layer, y, xt, ti, tg, rk, cnt, mod, w_gu, b_gu, w_dn, b_dn):
    dest, te, nt = _route(cnt, ti, rk)
    dest_flat = dest[:, :TOP_K].reshape(-1)
    te = te[:MOE_TILES, 0] + layer * N_EXPERTS
    n_tiles = nt[0, :1]
    xs = _dispatch(dest_flat, nt[1, :N_EXPERTS], xt)
    n_all = w_gu.shape[0] * N_EXPERTS
    act = _moe_up(te, n_tiles, xs, w_gu.reshape(n_all, D, 2 * D), b_gu.reshape(n_all, 1, 2 * D))
    ys = _moe_down(te, n_tiles, act, w_dn.reshape(n_all, D, D), b_dn.reshape(n_all, 1, D))
    return _combine(dest_flat, y, tg, mod, ys)


def _ones_blockdiag():
    idx = np.arange(HALF) // HEAD
    return jnp.asarray((idx[:, None] == idx[None, :]).astype(np.float32), dtype=BF16)


def _rope_tables():
    pos = jnp.arange(L_SAMPLE)
    rowp = (pos // 64).astype(F32)
    colp = (pos % 64).astype(F32)
    nf = HEAD // 4
    inv = jnp.power(10000.0, -jnp.arange(nf, dtype=F32) / nf)
    ar = rowp[:, None] * inv[None, :]
    ac = colp[:, None] * inv[None, :]
    cos64 = jnp.concatenate([jnp.cos(ar), jnp.cos(ar), jnp.cos(ac), jnp.cos(ac)], axis=1)
    sin64 = jnp.concatenate([-jnp.sin(ar), jnp.sin(ar), -jnp.sin(ac), jnp.sin(ac)], axis=1)
    cos = jnp.tile(cos64, (1, HALF // HEAD))
    sin = jnp.tile(sin64, (1, HALF // HEAD))
    ident = jnp.ones((TM, HALF), F32)
    return (jnp.concatenate([ident, cos], axis=0), jnp.concatenate([jnp.zeros((TM, HALF), F32), sin], axis=0))


def _bd_pairs(s):
    lead = s.shape[:-3]
    s = s.reshape(lead + (4, 2, HEAD, HEAD))
    z = jnp.zeros_like(s[..., 0, :, :])
    top = jnp.concatenate([s[..., 0, :, :], z], axis=-1)
    bot = jnp.concatenate([z, s[..., 1, :, :]], axis=-1)
    return jnp.concatenate([top, bot], axis=-2)


def _bd_unpairs(s):
    a = s[..., 0:HEAD, 0:HEAD]
    b = s[..., HEAD:, HEAD:]
    out = jnp.stack([a, b], axis=-3)
    return out.reshape(s.shape[:-3] + (8, HEAD, HEAD))


def kernel(x_prompt, x_sample, state_rwkv, cache_k_diff, cache_v_diff, state_retention, c, c_ctx, norm_g, ada_w, ada_b, e_w_in, e_w_out, sgu_ln_g, sgu_w, sgu_b, rw_mu, rw_w0, rw_w_up, rw_a0, rw_a_up, rw_g_up, rw_k_k, rw_k_a, rw_r_k, rw_gn_g, rw_gn_b, o_w_in, o_w_out, da_qk_g, da_lam, da_subln_g, ret_gn_g, router_w, router_b, ex_w_gu, ex_b_gu, ex_w_dn, ex_b_dn):
    x = jnp.concatenate([x_prompt.reshape(T_PROMPT, D), x_sample.reshape(T_SAMPLE, D)], axis=0)
    cvec8 = jnp.concatenate([c_ctx[None, :], c, jnp.zeros((3, D), F32)], axis=0)
    mods = _adaln(cvec8, ada_w, ada_b)
    mod0 = mods[0].reshape(8, 1, N_MOD * D)
    mod1 = mods[1].reshape(8, 1, N_MOD * D)
    ones_bd = _ones_blockdiag()
    rw_pad = jnp.pad(router_w, ((0, 0), (0, 0), (0, LANES - N_EXPERTS)))
    rb_pad = jnp.pad(router_b, ((0, 0), (0, LANES - N_EXPERTS))).reshape(2, 1, LANES)
    row = lambda i: (i, 0)
    half = pl.BlockSpec((TM, HALF), row)

    za, zb = _in_proj(x, norm_g[0, 0], mod0, e_w_in[0].astype(BF16), (2 * HALF, B_COLS))
    bs_full = jnp.repeat(sgu_b[0].T, HEAD, axis=1)
    a_out = _sgu(za, sgu_ln_g[0], sgu_w[0].astype(BF16), bs_full)
    zpad = jnp.zeros((2, HEAD, HALF), F32)
    wup_pad = jnp.concatenate([rw_w_up[0], zpad], axis=1).astype(BF16)
    aup_pad = jnp.concatenate([zpad, rw_a_up[0]], axis=1).astype(BF16)
    r, v, kkn, bonus, g, lw, kt, b = _rwkv_prep(zb, rw_mu[0], rw_k_k[0], rw_k_a[0], rw_r_k[0], rw_w0[0], rw_a0[0],
                                                wup_pad, aup_pad, rw_g_up[0].astype(BF16), ones_bd)
    s0_sample = _bd_pairs(jnp.moveaxis(state_rwkv[:, 0], 1, 0))
    s0_rw = jnp.concatenate([jnp.zeros((2, N_PROMPT, 4, LANES, LANES), F32), s0_sample], axis=1)
    yf_rw, yb_rw, sfin_rw = _rwkv_scan(r, v, kkn, lw, kt, b, s0_rw)
    new_rwkv = jnp.moveaxis(_bd_unpairs(sfin_rw[:, :N_PROMPT]), 0, 1)[:, None]
    y0, xp0, ti0, tg0, rk0, cnt0 = _out_proj(
        True,
        [a_out, yf_rw, yb_rw, bonus, g, rw_gn_g[0].reshape(1, HALF), rw_gn_b[0].reshape(1, HALF), ones_bd],
        [half, half, half, half, half, _full((1, HALF)), _full((1, HALF)), _full((HALF, HALF))],
        x, mod0, norm_g[0, 1], e_w_out[0].astype(BF16), rw_pad[0], rb_pad[0])
    x1 = _moe(0, y0, xp0, ti0, tg0, rk0, cnt0, mod0, ex_w_gu, ex_b_gu, ex_w_dn, ex_b_dn)

    zc, zr = _in_proj(x1, norm_g[1, 0], mod1, o_w_in[0].astype(BF16), (3 * HALF, 3 * HALF))
    cos_tab, sin_tab = _rope_tables()
    qkg = jnp.tile(da_qk_g[0], (1, HALF // HEAD))
    cq, ck, ck_raw, rqk = _odd_prep(zc, zr, cos_tab, sin_tab, qkg, ones_bd)
    lambda_init = 0.8 - 0.6 * math.exp(-0.3 * 1)
    lv = da_lam[0]
    lam = jnp.exp(jnp.sum(lv[0] * lv[1])) - jnp.exp(jnp.sum(lv[2] * lv[3])) + lambda_init
    c_out = _attn(cq, ck, zc, 0, N_PROMPT, L_PROMPT, lam, da_subln_g[0], 1.0 - lambda_init,
                  jnp.zeros((T, HALF), F32))
    ctx_k = cache_k_diff[:, 0].reshape(N_SAMPLE, PAST, HALF)
    ctx_v = cache_v_diff[:, 0].reshape(N_SAMPLE, PAST, HALF)
    c_out = _attn(cq, ck, zc, T_PROMPT, N_SAMPLE, L_SAMPLE, lam, da_subln_g[0], 1.0 - lambda_init,
                  c_out, ctx_k, ctx_v)
    sr = jnp.moveaxis(state_retention[:, 0], 1, 0)
    zr0 = jnp.zeros_like(sr)
    s0_sample = jnp.stack([jnp.concatenate([sr[:, :, 0], zr0[:, :, 0]], axis=-2),
                           jnp.concatenate([zr0[:, :, 1], sr[:, :, 1]], axis=-2),
                           jnp.concatenate([sr[:, :, 2], zr0[:, :, 2]], axis=-2),
                           jnp.concatenate([zr0[:, :, 3], sr[:, :, 3]], axis=-2)], axis=2)
    s0_ret = jnp.concatenate([jnp.zeros((2, N_PROMPT, 4, LANES, LANES), F32), s0_sample], axis=1)
    of_ret, ob_ret, rfin = _retention(rqk, zr, s0_ret)
    rfin_p = rfin[:, :N_PROMPT]
    new_ret = jnp.stack([rfin_p[:, :, 0, 0:HEAD], rfin_p[:, :, 1, HEAD:], rfin_p[:, :, 2, 0:HEAD],
                         rfin_p[:, :, 3, HEAD:]], axis=2)
    new_ret = jnp.moveaxis(new_ret, 0, 1)[:, None]
    y1, xp1, ti1, tg1, rk1, cnt1 = _out_proj(
        False,
        [c_out, of_ret, ob_ret, zr, ret_gn_g[0].reshape(1, HALF)],
        [half, half, half, pl.BlockSpec((TM, HALF), lambda i: (i, 2)), _full((1, HALF))],
        x1, mod1, norm_g[1, 1], o_w_out[0].astype(BF16), rw_pad[1], rb_pad[1])
    y_fin = _moe(1, y1, xp1, ti1, tg1, rk1, cnt1, mod1, ex_w_gu, ex_b_gu, ex_w_dn, ex_b_dn)

    new_k = ck_raw[:T_PROMPT].reshape(N_PROMPT, 1, L_PROMPT, 4, LANES)
    new_v = zc[:T_PROMPT, 2 * HALF:3 * HALF].reshape(N_PROMPT, 1, L_PROMPT, 4, LANES)
    return (y_fin[:T_PROMPT].reshape(N_PROMPT, L_PROMPT, D), y_fin[T_PROMPT:].reshape(N_SAMPLE, L_SAMPLE, D),
            new_rwkv, new_k, new_v, new_ret)
```

```python
import functools
import math

import numpy as np
import jax
import jax.numpy as jnp
from jax import lax
from jax.experimental import pallas as pl
from jax.experimental.pallas import tpu as pltpu

F32 = jnp.float32
BF16 = jnp.bfloat16
I32 = jnp.int32
HIGHEST = lax.Precision.HIGHEST

D = 1024
N_PROMPT, L_PROMPT = 16, 256
N_SAMPLE, L_SAMPLE = 4, 1024
N_SEQ = N_PROMPT + N_SAMPLE
PAST = 256
T_PROMPT = N_PROMPT * L_PROMPT
T_SAMPLE = N_SAMPLE * L_SAMPLE
T = T_PROMPT + T_SAMPLE
TM = 256
NT = T // TM
PROMPT_TILES = T_PROMPT // TM
TILES_PER_SAMPLE = L_SAMPLE // TM
N_MOD = 6
HALF = 512
B_COLS = 1792
HEAD = 64
W_DECAY_SCALE = math.exp(-0.5)
RWKV_GN_EPS = 64e-5
RW_CHUNK = 64
RET_CHUNK = 128
RET_EXP = ((5.0, 7.0, 9.0, 11.0), (6.0, 8.0, 10.0, 12.0))
N_EXPERTS = 32
TOP_K = 4
SWIGLU_LIMIT = 7.0
SWIGLU_ALPHA = 1.702
N_ASSIGN = T * TOP_K
TMX = 512
MOE_TILES = N_ASSIGN // TMX + N_EXPERTS
R_PAD = MOE_TILES * TMX
LANES = 128

NN = (((1,), (0,)), ((), ()))
NT_DIMS = (((1,), (1,)), ((), ()))
TN = (((0,), (0,)), ((), ()))


def _group(i):
    return jnp.where(i < PROMPT_TILES, 0, 1 + (i - PROMPT_TILES) // TILES_PER_SAMPLE)


def _mm(a, b, dims=NN, passes=1):
    dg = functools.partial(lax.dot_general, dimension_numbers=dims, preferred_element_type=F32)
    if passes == 1:
        return dg(a.astype(BF16), b.astype(BF16))
    a = a.astype(F32)
    b = b.astype(F32)
    ah = a.astype(BF16)
    al = (a - ah.astype(F32)).astype(BF16)
    bh = b.astype(BF16)
    bl = (b - bh.astype(F32)).astype(BF16)
    if dims[0][0] == (1,):
        m = a.shape[0]
        both = dg(jnp.concatenate([ah, al], axis=0), bh)
        return both[0:m] + (dg(ah, bl) + both[m:2 * m])
    return dg(ah, bh) + (dg(ah, bl) + dg(al, bh))


def _group_sum(x, ones_bd):
    xh = x.astype(BF16)
    xl = (x - xh.astype(F32)).astype(BF16)
    return (jnp.dot(xh, ones_bd, preferred_element_type=F32)
            + jnp.dot(xl, ones_bd, preferred_element_type=F32))


def _full(shape):
    nd = len(shape)
    return pl.BlockSpec(shape, lambda *_: (0,) * nd)


def _params(sem, vmem_mb=None):
    kw = {}
    if vmem_mb is not None:
        kw["vmem_limit_bytes"] = vmem_mb * 1024 * 1024
    return pltpu.CompilerParams(dimension_semantics=sem, **kw)


def _seq_tables(chunk):
    blk_f, blk_b, first, last, seq = [], [], [], [], []
    row = 0
    for s in range(N_SEQ):
        n = (L_PROMPT if s < N_PROMPT else L_SAMPLE) // chunk
        base = row // chunk
        for j in range(n):
            blk_f.append(base + j)
            blk_b.append(base + n - 1 - j)
            first.append(int(j == 0))
            last.append(int(j == n - 1))
            seq.append(s)
        row += n * chunk
    return tuple(np.asarray(a, np.int32) for a in (blk_f, blk_b, first, last, seq))


def _adaln_kernel(c_ref, w_ref, b_ref, o_ref):
    c = c_ref[...]
    s = c * jax.nn.sigmoid(c)
    o_ref[0] = jnp.dot(s, w_ref[0], precision=HIGHEST, preferred_element_type=F32) + b_ref[0]


def _adaln(cvec8, ada_w, ada_b):
    depth, _, n = ada_w.shape
    bn = 1536
    return pl.pallas_call(
        _adaln_kernel,
        out_shape=jax.ShapeDtypeStruct((depth, 8, n), F32),
        grid=(depth, n // bn),
        in_specs=[pl.BlockSpec((8, D), lambda l, j: (0, 0)),
                  pl.BlockSpec((1, D, bn), lambda l, j: (l, 0, j)),
                  pl.BlockSpec((1, 1, bn), lambda l, j: (l, 0, j))],
        out_specs=pl.BlockSpec((1, 8, bn), lambda l, j: (l, 0, j)),
        compiler_params=_params(("arbitrary", "arbitrary"), 40),
        name="adaln",
    )(cvec8, ada_w, ada_b.reshape(depth, 1, n))


def _in_kernel(x_ref, g_ref, mod_ref, w_ref, *outs, splits):
    x = x_ref[...]
    mod = mod_ref[0]
    y = x * lax.rsqrt(jnp.mean(x * x, axis=-1, keepdims=True) + 1e-6) * g_ref[...]
    h = (y * (1.0 + mod[:, D:2 * D]) + mod[:, 0:D]).astype(BF16)
    off = 0
    for o_ref, n in zip(outs, splits):
        o_ref[...] = jnp.dot(h, w_ref[:, off:off + n], preferred_element_type=F32)
        off += n


def _in_proj(x, g, mod, w_bf16, splits):
    n = w_bf16.shape[1]
    row = lambda i: (i, 0)
    return pl.pallas_call(
        functools.partial(_in_kernel, splits=splits),
        out_shape=[jax.ShapeDtypeStruct((T, s), F32) for s in splits],
        grid=(NT,),
        in_specs=[pl.BlockSpec((TM, D), row), _full((1, D)),
                  pl.BlockSpec((1, 1, N_MOD * D), lambda i: (_group(i), 0, 0)), _full((D, n))],
        out_specs=[pl.BlockSpec((TM, s), row) for s in splits],
        compiler_params=_params(("arbitrary",), 48),
        name="in_proj",
    )(x, g.reshape(1, D), mod, w_bf16)


def _gelu(x):
    return 0.5 * x * (1.0 + lax.erf(x * (1.0 / math.sqrt(2.0))))


def _sgu_kernel(za_ref, lng_ref, ws_ref, bs_ref, o_ref):
    u = _gelu(za_ref[:, 0:HALF])
    va = _gelu(za_ref[:, HALF:2 * HALF])
    mu = jnp.mean(va, axis=-1, keepdims=True)
    dv = va - mu
    var = jnp.mean(dv * dv, axis=-1, keepdims=True)
    vn = dv * lax.rsqrt(var + 1e-5) * lng_ref[...]
    lane = lax.broadcasted_iota(I32, (LANES, LANES), 1)
    first = lane < HEAD
    for c in range(TM // LANES):
        rows = slice(c * LANES, (c + 1) * LANES)
        for p in range(HALF // LANES):
            cols = slice(p * LANES, (p + 1) * LANES)
            vp = vn[rows, cols]
            s = (jnp.dot(ws_ref[2 * p], jnp.where(first, vp, 0.0).astype(BF16), preferred_element_type=F32)
                 + jnp.dot(ws_ref[2 * p + 1], jnp.where(first, 0.0, vp).astype(BF16), preferred_element_type=F32))
            o_ref[rows, cols] = u[rows, cols] * (s + bs_ref[:, cols])


def _sgu(za, ln_g, w_s_bf16, bs_full):
    return pl.pallas_call(
        _sgu_kernel,
        out_shape=jax.ShapeDtypeStruct((T, HALF), F32),
        grid=(NT,),
        in_specs=[pl.BlockSpec((TM, 2 * HALF), lambda i: (i, 0)), _full((1, HALF)),
                  _full((8, LANES, LANES)), _full((LANES, HALF))],
        out_specs=pl.BlockSpec((TM, HALF), lambda i: (i, 0)),
        compiler_params=_params(("arbitrary",)),
        name="sgu",
    )(za, ln_g.reshape(1, HALF), w_s_bf16, bs_full)


def _rwkv_prep_kernel(zb_ref, zp_ref, zn_ref, mu_ref, kk_ref, ka_ref, rk_ref, w0_ref, a0_ref,
                      wup_ref, aup_ref, gup_ref, ones_ref,
                      r_ref, v_ref, kkn_ref, bonus_ref, g_ref, lw_ref, kt_ref, b_ref):
    i = pl.program_id(0)
    in_sample = i >= PROMPT_TILES
    pos = (i - PROMPT_TILES) % TILES_PER_SAMPLE
    is_first = jnp.logical_or(jnp.logical_not(in_sample), pos == 0)
    is_last = jnp.logical_or(jnp.logical_not(in_sample), pos == TILES_PER_SAMPLE - 1)
    zb = zb_ref[...]
    prev_row = jnp.where(is_first, 0.0, zp_ref[7:8, :])
    next_row = jnp.where(is_last, 0.0, zn_ref[0:1, :])
    rowid = lax.broadcasted_iota(I32, (TM, 1), 0)
    zp = jnp.where(rowid == 0, prev_row, pltpu.roll(zb, 1, 0))
    zn = jnp.where(rowid == TM - 1, next_row, pltpu.roll(zb, TM - 1, 0))
    zs = zb + mu_ref[0:1, :] * (zp - zb) + mu_ref[1:2, :] * (zn - zb)
    r = zs[:, 0:HALF]
    k = zs[:, HALF:2 * HALF]
    v = zs[:, 2 * HALF:3 * HALF]
    wa = zs[:, 3 * HALF:3 * HALF + LANES]
    gd = zs[:, 3 * HALF + LANES:B_COLS]
    ones_bd = ones_ref[...]
    r_ref[...] = r
    v_ref[...] = v
    g_ref[...] = jnp.dot(jax.nn.sigmoid(gd).astype(BF16), gup_ref[...], preferred_element_type=F32)
    kk = k * kk_ref[...]
    kkn = kk / jnp.maximum(jnp.sqrt(_group_sum(kk * kk, ones_bd)), 1e-6)
    kkn_ref[...] = kkn
    bonus_ref[...] = _group_sum(r * k * rk_ref[...], ones_bd) * v
    tw = jnp.tanh(wa).astype(BF16)
    wa16 = wa.astype(BF16)
    for dd in range(2):
        lw_ref[dd] = -W_DECAY_SCALE * jax.nn.sigmoid(
            w0_ref[dd:dd + 1, :] + jnp.dot(tw, wup_ref[dd], preferred_element_type=F32))
        a = jax.nn.sigmoid(a0_ref[dd:dd + 1, :] + jnp.dot(wa16, aup_ref[dd], preferred_element_type=F32))
        kt_ref[dd] = k * (1.0 + (a - 1.0) * ka_ref[...])
        b_ref[dd] = a * kkn


def _rwkv_prep(zb, mu, k_k, k_a, r_k, w0, a0, wup_pad, aup_pad, g_up, ones_bd):
    row = lambda i: (i, 0)
    halo = TM // 8
    one = jax.ShapeDtypeStruct((T, HALF), F32)
    two = jax.ShapeDtypeStruct((2, T, HALF), F32)
    o1 = pl.BlockSpec((TM, HALF), row)
    o2 = pl.BlockSpec((2, TM, HALF), lambda i: (0, i, 0))
    return pl.pallas_call(
        _rwkv_prep_kernel,
        out_shape=[one, one, one, one, one, two, two, two],
        grid=(NT,),
        in_specs=[pl.BlockSpec((TM, B_COLS), row),
                  pl.BlockSpec((8, B_COLS), lambda i: (jnp.maximum(i * halo - 1, 0), 0)),
                  pl.BlockSpec((8, B_COLS), lambda i: (jnp.minimum((i + 1) * halo, T // 8 - 1), 0)),
                  _full((2, B_COLS)), _full((1, HALF)), _full((1, HALF)), _full((1, HALF)),
                  _full((2, HALF)), _full((2, HALF)),
                  _full((2, LANES, HALF)), _full((2, LANES, HALF)), _full((LANES, HALF)),
                  _full((HALF, HALF))],
        out_specs=[o1, o1, o1, o1, o1, o2, o2, o2],
        compiler_params=_params(("arbitrary",), 48),
        name="rwkv_prep",
    )(zb, zb, zb, mu, k_k.reshape(1, HALF), k_a.reshape(1, HALF), r_k.reshape(1, HALF), w0, a0,
      wup_pad, aup_pad, g_up, ones_bd)


def _rwkv_chunks(dirs):
    C = RW_CHUNK
    ti = lax.broadcasted_iota(I32, (C, C), 0)
    tj = lax.broadcasted_iota(I32, (C, C), 1)
    bi = lax.broadcasted_iota(I32, (LANES, LANES), 0)
    bj = lax.broadcasted_iota(I32, (LANES, LANES), 1)
    same = (bi >> 6) == (bj >> 6)
    pi = bi & (C - 1)
    pj = bj & (C - 1)
    eye = (bi == bj).astype(F32)
    h0 = lax.broadcasted_iota(I32, (C, LANES), 1) < HEAD

    def stack(x):
        return jnp.concatenate([jnp.where(h0, x, 0.0), jnp.where(h0, 0.0, x)], axis=0)

    def fold(x):
        return x[0:C] + x[C:2 * C]

    chains = []
    for rev, r, v, kk, lw, kt, b, s_ref, y_ref in dirs:
        tri = jnp.where((tj >= ti) if rev else (tj <= ti), 1.0, 0.0).astype(F32)
        p1 = lw.astype(BF16)
        r1 = lw - p1.astype(F32)
        p2 = r1.astype(BF16)
        p3 = (r1 - p2.astype(F32)).astype(BF16)
        cs3 = jnp.dot(tri.astype(BF16), jnp.concatenate([p1, p2, p3], axis=1), preferred_element_type=F32)
        cs = cs3[:, 0:HALF] + (cs3[:, HALF:2 * HALF] + cs3[:, 2 * HALF:3 * HALF])
        ctot = cs[0:1, :] if rev else cs[C - 1:C, :]
        e_neg = jnp.exp(-cs)
        e_tail = jnp.exp(ctot - cs)
        q1 = kk * jnp.exp(cs - lw)
        k1 = kt * e_neg
        b1 = b * e_neg
        r1 = r * jnp.exp(cs)
        k2 = kt * e_tail
        b2 = b * e_tail
        e_tot = jnp.exp(ctot)
        strict = jnp.logical_and(same, (pj > pi) if rev else (pj < pi))
        incl = jnp.logical_and(same, (pj >= pi) if rev else (pj <= pi))
        for p in range(HALF // LANES):
            cols = slice(p * LANES, (p + 1) * LANES)
            chains.append(dict(p=p, cols=cols, strict=strict, incl=incl, s_ref=s_ref, y_ref=y_ref,
                               q1=q1[:, cols], k1=k1[:, cols], b1=b1[:, cols], r1=r1[:, cols],
                               k2=k2[:, cols], b2=b2[:, cols], v=v[:, cols], e_tot=e_tot[:, cols]))

    for ch in chains:
        lhs = jnp.concatenate([stack(ch["q1"]), stack(ch["r1"])], axis=0)
        rhs = jnp.concatenate([ch["k1"], ch["k1"], ch["b1"], ch["b1"]], axis=0)
        gm = _mm(lhs, rhs, NT_DIMS, 3)
        ch["mk"] = jnp.where(ch["strict"], gm[0:2 * C, 0:2 * C], 0.0)
        ch["mb"] = jnp.where(ch["strict"], gm[0:2 * C, 2 * C:4 * C], 0.0)
        ch["nk"] = jnp.where(ch["incl"], gm[2 * C:4 * C, 0:2 * C], 0.0)
        ch["nb"] = jnp.where(ch["incl"], gm[2 * C:4 * C, 2 * C:4 * C], 0.0)
        ch["tinv"] = eye - jnp.where((pi >> 1) == (pj >> 1), ch["mb"], 0.0)
    size = 2
    while size < C:
        sh = size.bit_length() - 1
        blk = jnp.logical_and((pi >> (sh + 1)) == (pj >> (sh + 1)), (pi >> sh) != (pj >> sh))
        for ch in chains:
            ch["tn"] = _mm(ch["tinv"], jnp.where(blk, ch["mb"], 0.0), NN, 3)
        for ch in chains:
            ch["tinv"] = ch["tinv"] - _mm(ch["tn"], ch["tinv"], NN, 3)
        size *= 2
    for ch in chains:
        vst = stack(ch["v"])
        ch["mkv"] = fold(_mm(ch["mk"], vst, NN, 3))
        ch["nkv"] = fold(_mm(ch["nk"], vst, NN, 3))
        ch["s"] = ch["s_ref"][ch["p"]]
        ch["qr"] = _mm(jnp.concatenate([ch["q1"], ch["r1"]], axis=0), ch["s"], NT_DIMS, 3)
    for ch in chains:
        ch["u"] = fold(_mm(ch["tinv"], stack(ch["mkv"] + ch["qr"][0:C]), NN, 3))
    for ch in chains:
        ch["y_ref"][:, ch["cols"]] = ch["qr"][C:2 * C] + ch["nkv"] - fold(_mm(ch["nb"], stack(ch["u"]), NN, 3))
        upd = _mm(jnp.concatenate([ch["v"], ch["u"]], axis=0),
                  jnp.concatenate([ch["k2"], -ch["b2"]], axis=0), TN, 3)
        ch["s_ref"][ch["p"]] = ch["s"] * ch["e_tot"] + jnp.where(same, upd, 0.0)


def _rwkv_scan_kernel(bf_ref, bb_ref, first_ref, last_ref, seq_ref,
                      rf_ref, vf_ref, kkf_ref, lwf_ref, ktf_ref, bfw_ref,
                      rb_ref, vb_ref, kkb_ref, lwb_ref, ktb_ref, bbw_ref, s0_ref,
                      yf_ref, yb_ref, sfin_ref, s_ref):
    step = pl.program_id(0)

    @pl.when(first_ref[step] == 1)
    def _():
        s_ref[...] = s0_ref[:, 0]

    _rwkv_chunks([
        (False, rf_ref[...], vf_ref[...], kkf_ref[...], lwf_ref[0], ktf_ref[0], bfw_ref[0], s_ref.at[0], yf_ref),
        (True, rb_ref[...], vb_ref[...], kkb_ref[...], lwb_ref[0], ktb_ref[0], bbw_ref[0], s_ref.at[1], yb_ref)])

    @pl.when(last_ref[step] == 1)
    def _():
        sfin_ref[:, 0] = s_ref[...]


def _rwkv_scan(r, v, kk, lw, kt, b, s0_bd):
    C = RW_CHUNK
    tabs = _seq_tables(C)
    fwd = lambda i, bf, bb, fi, la, sq: (bf[i], 0)
    bwd = lambda i, bf, bb, fi, la, sq: (bb[i], 0)
    fwd3 = lambda i, bf, bb, fi, la, sq: (0, bf[i], 0)
    bwd3 = lambda i, bf, bb, fi, la, sq: (1, bb[i], 0)
    st = pl.BlockSpec((2, 1, 4, LANES, LANES), lambda i, bf, bb, fi, la, sq: (0, sq[i], 0, 0, 0))
    one_f, one_b = pl.BlockSpec((C, HALF), fwd), pl.BlockSpec((C, HALF), bwd)
    two_f, two_b = pl.BlockSpec((1, C, HALF), fwd3), pl.BlockSpec((1, C, HALF), bwd3)
    return pl.pallas_call(
        _rwkv_scan_kernel,
        out_shape=[jax.ShapeDtypeStruct((T, HALF), F32), jax.ShapeDtypeStruct((T, HALF), F32),
                   jax.ShapeDtypeStruct((2, N_SEQ, 4, LANES, LANES), F32)],
        grid_spec=pltpu.PrefetchScalarGridSpec(
            num_scalar_prefetch=5, grid=(len(tabs[0]),),
            in_specs=[one_f, one_f, one_f, two_f, two_f, two_f,
                      one_b, one_b, one_b, two_b, two_b, two_b, st],
            out_specs=[one_f, one_b, st],
            scratch_shapes=[pltpu.VMEM((2, 4, LANES, LANES), F32)]),
        compiler_params=_params(("arbitrary",)),
        name="rwkv_scan",
    )(*tabs, r, v, kk, lw, kt, b, r, v, kk, lw, kt, b, s0_bd)


def _rope(x, cos, sin_signed, first16):
    w = x.shape[1]
    partner = jnp.where(first16, pltpu.roll(x, w - 16, 1), pltpu.roll(x, 16, 1))
    return x * cos + partner * sin_signed


def _odd_prep_kernel(zc_ref, zr_ref, cos_ref, sin_ref, qkg_ref, ones_ref, cq_ref, ck_ref, ckraw_ref, rqk_ref):
    ones_bd = ones_ref[...]
    cos = cos_ref[...]
    sin = sin_ref[...]
    lane = lax.broadcasted_iota(I32, (TM, HALF), 1)
    first16 = (lane & 31) < 16
    for idx, (o_ref, raw_ref) in enumerate(((cq_ref, None), (ck_ref, ckraw_ref))):
        x = zc_ref[:, idx * HALF:(idx + 1) * HALF]
        ms = _group_sum(x * x, ones_bd) * (1.0 / HEAD)
        xn = x * lax.rsqrt(ms + 1e-6) * qkg_ref[idx:idx + 1, :]
        if raw_ref is not None:
            raw_ref[...] = xn
        o_ref[...] = _rope(xn, cos, sin, first16)
    rqk = _rope(zr_ref[...], cos, sin, first16)
    rqk_ref[...] = jnp.where(lane < HALF // 2, rqk * (HEAD ** -0.5), rqk)


def _odd_prep(zc, zr, cos_tab, sin_tab, qkg_tiled, ones_bd):
    row = lambda i: (i, 0)
    tab = lambda i: (jnp.where(i < PROMPT_TILES, 0, 1 + (i - PROMPT_TILES) % TILES_PER_SAMPLE), 0)
    one = jax.ShapeDtypeStruct((T, HALF), F32)
    o1 = pl.BlockSpec((TM, HALF), row)
    return pl.pallas_call(
        _odd_prep_kernel,
        out_shape=[one, one, one, one], grid=(NT,),
        in_specs=[pl.BlockSpec((TM, 2 * HALF), row), pl.BlockSpec((TM, HALF), row),
                  pl.BlockSpec((TM, HALF), tab), pl.BlockSpec((TM, HALF), tab),
                  _full((2, HALF)), _full((HALF, HALF))],
        out_specs=[o1, o1, o1, o1],
        compiler_params=_params(("arbitrary",)),
        name="odd_prep",
    )(zc, zr, cos_tab, sin_tab, qkg_tiled, ones_bd)


def _attn_kernel(*refs, has_ctx, one_minus_li):
    if has_ctx:
        q_ref, k_ref, v_ref, kc_ref, vc_ref, lam_ref, sg_ref, _, o_ref = refs
    else:
        q_ref, k_ref, v_ref, lam_ref, sg_ref, _, o_ref = refs
    lam = lam_ref[...]
    lane = lax.broadcasted_iota(I32, (LANES, LANES), 1)
    m0 = lane < HEAD
    scale = HEAD ** -0.5
    for h in range(4):
        cols = slice(h * LANES, (h + 1) * LANES)
        qp = q_ref[:, cols]
        segs = [(k_ref[:, cols], v_ref[:, cols])]
        if has_ctx:
            segs.append((kc_ref[0, :, cols], vc_ref[0, :, cols]))
        outs = []
        for qm in (jnp.where(m0, qp, 0.0), jnp.where(m0, 0.0, qp)):
            qm16 = qm.astype(BF16)
            ss = [lax.dot_general(qm16, ks.astype(BF16), NT_DIMS, preferred_element_type=F32) * scale
                  for ks, _ in segs]
            mx = ss[0].max(axis=-1, keepdims=True)
            for s_ in ss[1:]:
                mx = jnp.maximum(mx, s_.max(axis=-1, keepdims=True))
            ps = [jnp.exp(s_ - mx) for s_ in ss]
            den = ps[0].sum(axis=-1, keepdims=True)
            for p_ in ps[1:]:
                den = den + p_.sum(axis=-1, keepdims=True)
            outs.append([p_ / den for p_ in ps])
        acc = None
        for si, (_, vs) in enumerate(segs):
            amap = outs[0][si] - lam * outs[1][si]
            t = jnp.dot(amap.astype(BF16), vs.astype(BF16), preferred_element_type=F32)
            acc = t if acc is None else acc + t
        nrm = acc * lax.rsqrt(jnp.mean(acc * acc, axis=-1, keepdims=True) + 1e-6) * sg_ref[...]
        o_ref[:, cols] = nrm * one_minus_li


def _attn(cq, ck, zc, row0, n_seq, seq_len, lam, subln_g, one_minus_li, prev, ctx_k=None, ctx_v=None):
    nq = seq_len // LANES
    qb0 = row0 // LANES
    sb0 = row0 // seq_len
    in_specs = [pl.BlockSpec((LANES, HALF), lambda s, q: (qb0 + s * nq + q, 0)),
                pl.BlockSpec((seq_len, HALF), lambda s, q: (sb0 + s, 0)),
                pl.BlockSpec((seq_len, HALF), lambda s, q: (sb0 + s, 2))]
    args = [cq, ck, zc]
    if ctx_k is not None:
        in_specs += [pl.BlockSpec((1, PAST, HALF), lambda s, q: (s, 0, 0))] * 2
        args += [ctx_k, ctx_v]
    in_specs += [_full((1, 1)), _full((1, LANES))]
    args += [lam.reshape(1, 1), subln_g.reshape(1, LANES)]
    in_specs.append(pl.BlockSpec(memory_space=pl.ANY))
    args.append(prev)
    aliases = {len(args) - 1: 0}
    return pl.pallas_call(
        functools.partial(_attn_kernel, has_ctx=ctx_k is not None, one_minus_li=one_minus_li),
        out_shape=jax.ShapeDtypeStruct((T, HALF), F32),
        grid=(n_seq, nq), in_specs=in_specs,
        out_specs=pl.BlockSpec((LANES, HALF), lambda s, q: (qb0 + s * nq + q, 0)),
        input_output_aliases=aliases,
        compiler_params=_params(("arbitrary", "arbitrary"), 48),
        name="diff_attn",
    )(*args)


_LOG_GAMMA = tuple(tuple(float(np.log1p(-np.exp2(-np.float32(e)), dtype=np.float32)) for e in es)
                   for es in RET_EXP)


def _ret_chunk(rev, qk_ref, v_ref, s_ref, o_ref):
    C = RET_CHUNK
    ii = lax.broadcasted_iota(I32, (C, C), 0)
    jj = lax.broadcasted_iota(I32, (C, C), 1)
    mask = (jj > ii) if rev else (jj <= ii)
    dist = jnp.where(mask, (jj - ii) if rev else (ii - jj), 0).astype(F32)
    ri = lax.broadcasted_iota(I32, (C, 1), 0)
    kpow = (ri if rev else (C - 1 - ri)).astype(F32)
    qpow = ((C - ri) if rev else (ri + 1)).astype(F32)
    lane = lax.broadcasted_iota(I32, (C, LANES), 1)
    for h in range(4):
        lg = _LOG_GAMMA[1 if rev else 0][h]
        p = h // 2
        hm = (lane < HEAD) if h % 2 == 0 else (lane >= HEAD)
        qp = jnp.where(hm, qk_ref[:, p * LANES:(p + 1) * LANES], 0.0)
        kp = jnp.where(hm, qk_ref[:, HALF // 2 + p * LANES:HALF // 2 + (p + 1) * LANES], 0.0)
        vh = v_ref[:, h * LANES:(h + 1) * LANES].astype(BF16)
        decay = jnp.where(mask, jnp.exp(lg * dist), 0.0)
        sc = lax.dot_general(qp.astype(BF16), kp.astype(BF16), NT_DIMS, preferred_element_type=F32) * decay
        o_intra = jnp.dot(sc.astype(BF16), vh, preferred_element_type=F32)
        s_prev = s_ref[h]
        o_cross = jnp.dot((qp * jnp.exp(lg * qpow)).astype(BF16), s_prev.astype(BF16),
                          preferred_element_type=F32)
        o_ref[:, h * LANES:(h + 1) * LANES] = o_intra + o_cross
        kv = lax.dot_general((kp * jnp.exp(lg * kpow)).astype(BF16), vh, TN, preferred_element_type=F32)
        s_ref[h] = math.exp(lg * C) * s_prev + kv


def _ret_kernel(bf_ref, bb_ref, first_ref, last_ref, seq_ref,
                qkf_ref, vf_ref, qkb_ref, vb_ref, s0_ref, of_ref, ob_ref, sfin_ref, s_ref):
    step = pl.program_id(0)

    @pl.when(first_ref[step] == 1)
    def _():
        s_ref[...] = s0_ref[:, 0]

    _ret_chunk(False, qkf_ref, vf_ref, s_ref.at[0], of_ref)
    _ret_chunk(True, qkb_ref, vb_ref, s_ref.at[1], ob_ref)

    @pl.when(last_ref[step] == 1)
    def _():
        sfin_ref[:, 0] = s_ref[...]


def _retention(rqk, zr, s0):
    C = RET_CHUNK
    tabs = _seq_tables(C)
    st = pl.BlockSpec((2, 1, 4, LANES, LANES), lambda i, bf, bb, fi, la, sq: (0, sq[i], 0, 0, 0))
    spec = lambda use_b, col: pl.BlockSpec(
        (C, HALF), lambda i, bf, bb, fi, la, sq: ((bb if use_b else bf)[i], col))
    return pl.pallas_call(
        _ret_kernel,
        out_shape=[jax.ShapeDtypeStruct((T, HALF), F32), jax.ShapeDtypeStruct((T, HALF), F32),
                   jax.ShapeDtypeStruct((2, N_SEQ, 4, LANES, LANES), F32)],
        grid_spec=pltpu.PrefetchScalarGridSpec(
            num_scalar_prefetch=5, grid=(len(tabs[0]),),
            in_specs=[spec(False, 0), spec(False, 1), spec(True, 0), spec(True, 1), st],
            out_specs=[spec(False, 0), spec(True, 0), st],
            scratch_shapes=[pltpu.VMEM((2, 4, LANES, LANES), F32)]),
        compiler_params=_params(("arbitrary",)),
        name="retention",
    )(*tabs, rqk, zr, rqk, zr, s0)


def _out_kernel(*refs, even):
    if even:
        (a_ref, yf_ref, yb_ref, bonus_ref, g_ref, gng_ref, gnb_ref, ones_ref,
         x_ref, mod_ref, ng_ref, wo_ref, rw_ref, rb_ref,
         y_ref, xp_ref, ti_ref, tg_ref, rk_ref, cnt_ref, run_ref) = refs
        ones_bd = ones_ref[...]
        ys = yf_ref[...] + yb_ref[...]
        mu = _group_sum(ys, ones_bd) * (1.0 / HEAD)
        dv = ys - mu
        var = _group_sum(dv * dv, ones_bd) * (1.0 / HEAD)
        yn = dv * lax.rsqrt(var + RWKV_GN_EPS) * gng_ref[...] + gnb_ref[...]
        left = a_ref[...]
        right = (yn + bonus_ref[...]) * g_ref[...]
    else:
        (c_ref, of_ref, ob_ref, rg_ref, gng_ref,
         x_ref, mod_ref, ng_ref, wo_ref, rw_ref, rb_ref,
         y_ref, xp_ref, ti_ref, tg_ref, rk_ref, cnt_ref, run_ref) = refs
        left = c_ref[...]
        rg = rg_ref[...]
        gate = rg * jax.nn.sigmoid(rg)
        os_ = of_ref[...] + ob_ref[...]
        parts = []
        for h in range(4):
            oh = os_[:, h * LANES:(h + 1) * LANES]
            mu = jnp.mean(oh, axis=-1, keepdims=True)
            dv = oh - mu
            var = jnp.mean(dv * dv, axis=-1, keepdims=True)
            parts.append(dv * lax.rsqrt(var + 1e-5))
        right = gate * (jnp.concatenate(parts, axis=1) * gng_ref[...])
    mod = mod_ref[0]
    o = (jnp.dot(left.astype(BF16), wo_ref[0:HALF, :], preferred_element_type=F32)
         + jnp.dot(right.astype(BF16), wo_ref[HALF:2 * HALF, :], preferred_element_type=F32))
    y = x_ref[...] + mod[:, 2 * D:3 * D] * o
    y_ref[...] = y
    yn2 = y * lax.rsqrt(jnp.mean(y * y, axis=-1, keepdims=True) + 1e-6) * ng_ref[...]
    t = yn2 * (1.0 + mod[:, 4 * D:5 * D]) + mod[:, 3 * D:4 * D]
    xp_ref[...] = t
    logits = _mm(t, rw_ref[...], NN, 3) + rb_ref[...]
    lane = lax.broadcasted_iota(I32, (TM, LANES), 1)
    neg = jnp.float32(-jnp.inf)
    lg = jnp.where(lane < N_EXPERTS, logits, neg)
    vals, hits = [], []
    for _ in range(TOP_K):
        m = jnp.max(lg, axis=-1, keepdims=True)
        ix = jnp.min(jnp.where(lg == m, lane, LANES), axis=-1, keepdims=True)
        hit = lane == ix
        vals.append(m)
        hits.append((ix, hit))
        lg = jnp.where(hit, neg, lg)
    es = [jnp.exp(vv - vals[0]) for vv in vals]
    den = es[0] + es[1] + es[2] + es[3]

    @pl.when(pl.program_id(0) == 0)
    def _():
        run_ref[...] = jnp.zeros_like(run_ref)

    member = jnp.zeros((TM, LANES), F32)
    for _, hit in hits:
        member = member + jnp.where(hit, 1.0, 0.0)
    ri = lax.broadcasted_iota(I32, (TM, TM), 0)
    ci = lax.broadcasted_iota(I32, (TM, TM), 1)
    before = jnp.where(ci < ri, 1.0, 0.0).astype(BF16)
    seen = run_ref[...] + jnp.dot(before, member.astype(BF16), preferred_element_type=F32)
    ti = jnp.zeros((TM, LANES), I32)
    tg = jnp.zeros((TM, LANES), F32)
    rk = jnp.zeros((TM, LANES), F32)
    for kk, (ix, hit) in enumerate(hits):
        ti = jnp.where(lane == kk, ix, ti)
        tg = jnp.where(lane == kk, es[kk] / den, tg)
        rk = jnp.where(lane == kk, jnp.sum(jnp.where(hit, seen, 0.0), axis=-1, keepdims=True), rk)
    ti_ref[...] = ti
    tg_ref[...] = tg
    rk_ref[...] = rk.astype(I32)
    run_ref[...] = run_ref[...] + jnp.sum(member, axis=0, keepdims=True)
    cnt_ref[...] = run_ref[...]


def _out_proj(even, mix_args, mix_specs, x, mod, norm_g, w_out_bf16, rw_pad, rb_pad):
    row = lambda i: (i, 0)
    modspec = pl.BlockSpec((1, 1, N_MOD * D), lambda i: (_group(i), 0, 0))
    in_specs = list(mix_specs) + [pl.BlockSpec((TM, D), row), modspec, _full((1, D)), _full((D, D)),
                                  _full((D, LANES)), _full((1, LANES))]
    args = list(mix_args) + [x, mod, norm_g.reshape(1, D), w_out_bf16, rw_pad, rb_pad]
    lane_i = jax.ShapeDtypeStruct((T, LANES), I32)
    lane_spec = pl.BlockSpec((TM, LANES), row)
    return pl.pallas_call(
        functools.partial(_out_kernel, even=even),
        out_shape=[jax.ShapeDtypeStruct((T, D), F32), jax.ShapeDtypeStruct((T, D), F32),
                   lane_i, jax.ShapeDtypeStruct((T, LANES), F32), lane_i,
                   jax.ShapeDtypeStruct((1, LANES), F32)],
        grid=(NT,), in_specs=in_specs,
        out_specs=[pl.BlockSpec((TM, D), row), pl.BlockSpec((TM, D), row),
                   lane_spec, lane_spec, lane_spec, _full((1, LANES))],
        scratch_shapes=[pltpu.VMEM((1, LANES), F32)],
        compiler_params=_params(("arbitrary",), 48),
        name="out_proj",
    )(*args)


def _route_kernel(cnt_ref, ti_ref, rk_ref, dest_ref, te_ref, nt_ref):
    cnt = cnt_ref[...].astype(I32)
    ntile = lax.shift_right_logical(cnt + (TMX - 1), TMX.bit_length() - 1)
    ei = lax.broadcasted_iota(I32, (LANES, LANES), 0)
    ej = lax.broadcasted_iota(I32, (LANES, LANES), 1)
    upto = jnp.where(ei <= ej, 1.0, 0.0).astype(BF16)
    ntile_f = jnp.broadcast_to(ntile.astype(F32), (8, LANES))
    tile_end = jnp.dot(ntile_f.astype(BF16), upto, preferred_element_type=F32)[0:1, :]
    row_start = (tile_end - ntile.astype(F32)) * float(TMX)
    lane = lax.broadcasted_iota(I32, (TM, LANES), 1)
    ti = ti_ref[...]
    rk = rk_ref[...]
    dest = jnp.zeros((TM, LANES), F32)
    for k in range(TOP_K):
        hit = lane == ti[:, k:k + 1]
        start = jnp.sum(jnp.where(hit, row_start, 0.0), axis=-1, keepdims=True)
        dest = jnp.where(lane == k, start + rk[:, k:k + 1].astype(F32), dest)
    dest_ref[...] = dest.astype(I32)

    @pl.when(pl.program_id(0) == 0)
    def _():
        lane1 = lax.broadcasted_iota(I32, (1, LANES), 1)
        n_tiles = jnp.max(tile_end, axis=-1, keepdims=True)
        last_e = jnp.max(jnp.where(cnt > 0, lane1, 0), axis=-1, keepdims=True)
        tile = lax.broadcasted_iota(I32, (TM, 1), 0).astype(F32)
        te = jnp.sum(jnp.where(tile_end <= tile, 1, 0), axis=-1, keepdims=True)
        te = jnp.where(tile < n_tiles, te, last_e)
        te_ref[...] = jnp.broadcast_to(te, (TM, LANES)).astype(I32)
        first_row = lax.broadcasted_iota(I32, (8, LANES), 0) == 0
        nt_ref[...] = jnp.where(first_row, n_tiles, tile_end).astype(I32)


def _route(cnt, ti, rk):
    row = lambda i: (i, 0)
    return pl.pallas_call(
        _route_kernel,
        out_shape=[jax.ShapeDtypeStruct((T, LANES), I32), jax.ShapeDtypeStruct((TM, LANES), I32),
                   jax.ShapeDtypeStruct((8, LANES), I32)],
        grid=(NT,),
        in_specs=[_full((1, LANES)), pl.BlockSpec((TM, LANES), row), pl.BlockSpec((TM, LANES), row)],
        out_specs=[pl.BlockSpec((TM, LANES), row), _full((TM, LANES)), _full((8, LANES))],
        compiler_params=_params(("arbitrary",)),
        name="moe_route",
    )(cnt, ti, rk)


def _row_copy(src_ref, src_row, dst_ref, dst_row, sem):
    return pltpu.make_async_copy(src_ref.at[pl.ds(src_row, 1)], dst_ref.at[pl.ds(dst_row, 1)], sem)


def _wait_tiles(n, src_ref, dst_ref, sem):
    for _ in range(n):
        pltpu.make_async_copy(src_ref, dst_ref, sem).wait()


def _dispatch_kernel(dest_ref, tend_ref, x_ref, xs_ref, zero_ref, sem):
    i = pl.program_id(0)

    @pl.when(i == 0)
    def _():
        zero_ref[...] = jnp.zeros_like(zero_ref)

        def last_tile(e, fn):
            end = tend_ref[e]
            begin = tend_ref[e - 1] if e > 0 else 0

            @pl.when(end > begin)
            def _():
                fn(pltpu.make_async_copy(zero_ref, xs_ref.at[pl.ds((end - 1) * TMX, TMX)], sem))

        def unused_tile(j):
            return pltpu.make_async_copy(zero_ref, xs_ref.at[pl.ds(j * TMX, TMX)], sem)

        def start_unused(j, carry):
            unused_tile(j).start()
            return carry

        def wait_unused(j, carry):
            unused_tile(j).wait()
            return carry

        n_used = tend_ref[N_EXPERTS - 1]
        for e in range(N_EXPERTS):
            last_tile(e, lambda c: c.start())
        lax.fori_loop(n_used, MOE_TILES, start_unused, 0)
        for e in range(N_EXPERTS):
            last_tile(e, lambda c: c.wait())
        lax.fori_loop(n_used, MOE_TILES, wait_unused, 0)

    base = i * (TM * TOP_K)

    def start(r, carry):
        for k in range(TOP_K):
            _row_copy(x_ref, r, xs_ref, dest_ref[base + r * TOP_K + k], sem).start()
        return carry

    lax.fori_loop(0, TM, start, 0, unroll=4)
    _wait_tiles(TOP_K, x_ref, xs_ref.at[pl.ds(0, TM)], sem)


def _dispatch(dest_flat, tile_end, xt):
    return pl.pallas_call(
        _dispatch_kernel,
        out_shape=jax.ShapeDtypeStruct((R_PAD, D), F32),
        grid_spec=pltpu.PrefetchScalarGridSpec(
            num_scalar_prefetch=2, grid=(NT,),
            in_specs=[pl.BlockSpec((TM, D), lambda i, d, te: (i, 0))],
            out_specs=pl.BlockSpec(memory_space=pl.ANY),
            scratch_shapes=[pltpu.VMEM((TMX, D), F32), pltpu.SemaphoreType.DMA(())]),
        compiler_params=_params(("arbitrary",)),
        name="moe_dispatch",
    )(dest_flat, tile_end, xt)


W_PARTS = 4


def _expert_weights(i, nt, te_ref, w_ref, wbuf_ref, wsem, w16_ref, group_ref):
    rows = w_ref.shape[1] // W_PARTS

    def fetch(e, buf):
        return [pltpu.make_async_copy(w_ref.at[e, pl.ds(p * rows, rows)], wbuf_ref.at[buf, pl.ds(p * rows, rows)],
                                      wsem.at[buf]) for p in range(W_PARTS)]

    @pl.when(i == 0)
    def _():
        group_ref[0] = 0
        for c in fetch(te_ref[0], 0):
            c.start()

    first = jnp.logical_or(i == 0, te_ref[i] != te_ref[jnp.maximum(i - 1, 0)])

    @pl.when(jnp.logical_and(first, i < nt))
    def _():
        cur = group_ref[0] % 2
        nxt = lax.while_loop(
            lambda j: jnp.logical_and(j < nt, te_ref[jnp.minimum(j, MOE_TILES - 1)] == te_ref[i]),
            lambda j: j + 1, i + 1)

        @pl.when(nxt < nt)
        def _():
            for c in fetch(te_ref[jnp.minimum(nxt, MOE_TILES - 1)], 1 - cur):
                c.start()

        for c in fetch(0, cur):
            c.wait()
        w16_ref[...] = wbuf_ref[cur].astype(BF16)
        group_ref[0] = group_ref[0] + 1


def _moe_up_kernel(te_ref, nt_ref, xs_ref, w_ref, b_ref, act_ref, wbuf_ref, wsem, w16_ref, group_ref):
    i = pl.program_id(0)
    _expert_weights(i, nt_ref[0], te_ref, w_ref, wbuf_ref, wsem, w16_ref, group_ref)

    @pl.when(i < nt_ref[0])
    def _():
        x16 = xs_ref[...].astype(BF16)
        for h in range(2):
            gc = slice(h * HALF, (h + 1) * HALF)
            uc = slice(D + h * HALF, D + (h + 1) * HALF)
            g = jnp.dot(x16, w16_ref[:, gc], preferred_element_type=F32) + b_ref[0, :, gc]
            u = jnp.dot(x16, w16_ref[:, uc], preferred_element_type=F32) + b_ref[0, :, uc]
            gt = jnp.minimum(g, SWIGLU_LIMIT)
            up = jnp.clip(u, -SWIGLU_LIMIT, SWIGLU_LIMIT)
            act_ref[:, gc] = ((up + 1.0) * gt * jax.nn.sigmoid(SWIGLU_ALPHA * gt)).astype(BF16)

    @pl.when(i >= nt_ref[0])
    def _():
        act_ref[...] = jnp.zeros_like(act_ref)


def _tile_clamped(i, te, nt):
    return (jnp.minimum(i, jnp.maximum(nt[0] - 1, 0)), 0)


def _weight_scratch(n_out):
    return [pltpu.VMEM((2, D, n_out), F32), pltpu.SemaphoreType.DMA((2,)), pltpu.VMEM((D, n_out), BF16),
            pltpu.SMEM((1,), I32)]


def _moe_up(te, n_tiles, xs, w_gu, b_gu):
    return pl.pallas_call(
        _moe_up_kernel,
        out_shape=jax.ShapeDtypeStruct((R_PAD, D), BF16),
        grid_spec=pltpu.PrefetchScalarGridSpec(
            num_scalar_prefetch=2, grid=(MOE_TILES,),
            in_specs=[pl.BlockSpec((TMX, D), _tile_clamped), pl.BlockSpec(memory_space=pl.ANY),
                      pl.BlockSpec((1, 1, 2 * D), lambda i, te, nt: (te[i], 0, 0))],
            out_specs=pl.BlockSpec((TMX, D), lambda i, te, nt: (i, 0)),
            scratch_shapes=_weight_scratch(2 * D)),
        compiler_params=_params(("arbitrary",), 56),
        name="moe_up",
    )(te, n_tiles, xs, w_gu, b_gu)


def _moe_down_kernel(te_ref, nt_ref, act_ref, w_ref, b_ref, y_ref, wbuf_ref, wsem, w16_ref, group_ref):
    i = pl.program_id(0)
    _expert_weights(i, nt_ref[0], te_ref, w_ref, wbuf_ref, wsem, w16_ref, group_ref)

    @pl.when(i < nt_ref[0])
    def _():
        y_ref[...] = jnp.dot(act_ref[...], w16_ref[...], preferred_element_type=F32) + b_ref[0]

    @pl.when(i >= nt_ref[0])
    def _():
        y_ref[...] = jnp.zeros_like(y_ref)


def _moe_down(te, n_tiles, act, w_dn, b_dn):
    return pl.pallas_call(
        _moe_down_kernel,
        out_shape=jax.ShapeDtypeStruct((R_PAD, D), F32),
        grid_spec=pltpu.PrefetchScalarGridSpec(
            num_scalar_prefetch=2, grid=(MOE_TILES,),
            in_specs=[pl.BlockSpec((TMX, D), _tile_clamped), pl.BlockSpec(memory_space=pl.ANY),
                      pl.BlockSpec((1, 1, D), lambda i, te, nt: (te[i], 0, 0))],
            out_specs=pl.BlockSpec((TMX, D), lambda i, te, nt: (i, 0)),
            scratch_shapes=_weight_scratch(D)),
        compiler_params=_params(("arbitrary",), 40),
        name="moe_down",
    )(te, n_tiles, act, w_dn, b_dn)


def _combine_kernel(dest_ref, x_ref, tg_ref, mod_ref, ys_ref, o_ref, buf_ref, sem):
    i = pl.program_id(0)
    slot = i % 2

    def gather(tile, b):
        base = tile * (TM * TOP_K)

        def body(r, carry):
            for k in range(TOP_K):
                _row_copy(ys_ref, dest_ref[base + r * TOP_K + k], buf_ref.at[b, k], r, sem.at[b]).start()
            return carry

        lax.fori_loop(0, TM, body, 0, unroll=4)

    @pl.when(i == 0)
    def _():
        gather(0, 0)

    @pl.when(i + 1 < NT)
    def _():
        gather(i + 1, 1 - slot)

    _wait_tiles(TOP_K, ys_ref.at[pl.ds(0, TM)], buf_ref.at[slot, 0], sem.at[slot])
    tg = tg_ref[...]
    f = tg[:, 0:1] * buf_ref[slot, 0]
    for k in range(1, TOP_K):
        f = f + tg[:, k:k + 1] * buf_ref[slot, k]
    o_ref[...] = x_ref[...] + mod_ref[0][:, 5 * D:6 * D] * f


def _combine(dest_flat, x, tg, mod, ys):
    row = lambda i, d: (i, 0)
    return pl.pallas_call(
        _combine_kernel,
        out_shape=jax.ShapeDtypeStruct((T, D), F32),
        grid_spec=pltpu.PrefetchScalarGridSpec(
            num_scalar_prefetch=1, grid=(NT,),
            in_specs=[pl.BlockSpec((TM, D), row), pl.BlockSpec((TM, LANES), row),
                      pl.BlockSpec((1, 1, N_MOD * D), lambda i, d: (_group(i), 0, 0)),
                      pl.BlockSpec(memory_space=pl.ANY)],
            out_specs=pl.BlockSpec((TM, D), row),
            scratch_shapes=[pltpu.VMEM((2, TOP_K, TM, D), F32), pltpu.SemaphoreType.DMA((2,))]),
        compiler_params=_params(("arbitrary",), 40),
        name="moe_combine",
    )(dest_flat, x, tg, mod, ys)


def _moe(layer, y, xt, ti, tg, rk, cnt, mod, w_gu, b_gu, w_dn, b_dn):
    dest, te, nt = _route(cnt, ti, rk)
    dest_flat = dest[:, :TOP_K].reshape(-1)
    te = te[:MOE_TILES, 0] + layer * N_EXPERTS
    n_tiles = nt[0, :1]
    xs = _dispatch(dest_flat, nt[1, :N_EXPERTS], xt)
    n_all = w_gu.shape[0] * N_EXPERTS
    act = _moe_up(te, n_tiles, xs, w_gu.reshape(n_all, D, 2 * D), b_gu.reshape(n_all, 1, 2 * D))
    ys = _moe_down(te, n_tiles, act, w_dn.reshape(n_all, D, D), b_dn.reshape(n_all, 1, D))
    return _combine(dest_flat, y, tg, mod, ys)


def _ones_blockdiag():
    idx = np.arange(HALF) // HEAD
    return jnp.asarray((idx[:, None] == idx[None, :]).astype(np.float32), dtype=BF16)


def _rope_tables():
    pos = jnp.arange(L_SAMPLE)
    rowp = (pos // 64).astype(F32)
    colp = (pos % 64).astype(F32)
    nf = HEAD // 4
    inv = jnp.power(10000.0, -jnp.arange(nf, dtype=F32) / nf)
    ar = rowp[:, None] * inv[None, :]
    ac = colp[:, None] * inv[None, :]
    cos64 = jnp.concatenate([jnp.cos(ar), jnp.cos(ar), jnp.cos(ac), jnp.cos(ac)], axis=1)
    sin64 = jnp.concatenate([-jnp.sin(ar), jnp.sin(ar), -jnp.sin(ac), jnp.sin(ac)], axis=1)
    cos = jnp.tile(cos64, (1, HALF // HEAD))
    sin = jnp.tile(sin64, (1, HALF // HEAD))
    ident = jnp.ones((TM, HALF), F32)
    return (jnp.concatenate([ident, cos], axis=0), jnp.concatenate([jnp.zeros((TM, HALF), F32), sin], axis=0))


def _bd_pairs(s):
    lead = s.shape[:-3]
    s = s.reshape(lead + (4, 2, HEAD, HEAD))
    z = jnp.zeros_like(s[..., 0, :, :])
    top = jnp.concatenate([s[..., 0, :, :], z], axis=-1)
    bot = jnp.concatenate([z, s[..., 1, :, :]], axis=-1)
    return jnp.concatenate([top, bot], axis=-2)


def _bd_unpairs(s):
    a = s[..., 0:HEAD, 0:HEAD]
    b = s[..., HEAD:, HEAD:]
    out = jnp.stack([a, b], axis=-3)
    return out.reshape(s.shape[:-3] + (8, HEAD, HEAD))


def kernel(x_prompt, x_sample, state_rwkv, cache_k_diff, cache_v_diff, state_retention, c, c_ctx, norm_g, ada_w, ada_b, e_w_in, e_w_out, sgu_ln_g, sgu_w, sgu_b, rw_mu, rw_w0, rw_w_up, rw_a0, rw_a_up, rw_g_up, rw_k_k, rw_k_a, rw_r_k, rw_gn_g, rw_gn_b, o_w_in, o_w_out, da_qk_g, da_lam, da_subln_g, ret_gn_g, router_w, router_b, ex_w_gu, ex_b_gu, ex_w_dn, ex_b_dn):
    x = jnp.concatenate([x_prompt.reshape(T_PROMPT, D), x_sample.reshape(T_SAMPLE, D)], axis=0)
    cvec8 = jnp.concatenate([c_ctx[None, :], c, jnp.zeros((3, D), F32)], axis=0)
    mods = _adaln(cvec8, ada_w, ada_b)
    mod0 = mods[0].reshape(8, 1, N_MOD * D)
    mod1 = mods[1].reshape(8, 1, N_MOD * D)
    ones_bd = _ones_blockdiag()
    rw_pad = jnp.pad(router_w, ((0, 0), (0, 0), (0, LANES - N_EXPERTS)))
    rb_pad = jnp.pad(router_b, ((0, 0), (0, LANES - N_EXPERTS))).reshape(2, 1, LANES)
    row = lambda i: (i, 0)
    half = pl.BlockSpec((TM, HALF), row)

    za, zb = _in_proj(x, norm_g[0, 0], mod0, e_w_in[0].astype(BF16), (2 * HALF, B_COLS))
    bs_full = jnp.repeat(sgu_b[0].T, HEAD, axis=1)
    a_out = _sgu(za, sgu_ln_g[0], sgu_w[0].astype(BF16), bs_full)
    zpad = jnp.zeros((2, HEAD, HALF), F32)
    wup_pad = jnp.concatenate([rw_w_up[0], zpad], axis=1).astype(BF16)
    aup_pad = jnp.concatenate([zpad, rw_a_up[0]], axis=1).astype(BF16)
    r, v, kkn, bonus, g, lw, kt, b = _rwkv_prep(zb, rw_mu[0], rw_k_k[0], rw_k_a[0], rw_r_k[0], rw_w0[0], rw_a0[0],
                                                wup_pad, aup_pad, rw_g_up[0].astype(BF16), ones_bd)
    s0_sample = _bd_pairs(jnp.moveaxis(state_rwkv[:, 0], 1, 0))
    s0_rw = jnp.concatenate([jnp.zeros((2, N_PROMPT, 4, LANES, LANES), F32), s0_sample], axis=1)
    yf_rw, yb_rw, sfin_rw = _rwkv_scan(r, v, kkn, lw, kt, b, s0_rw)
    new_rwkv = jnp.moveaxis(_bd_unpairs(sfin_rw[:, :N_PROMPT]), 0, 1)[:, None]
    y0, xp0, ti0, tg0, rk0, cnt0 = _out_proj(
        True,
        [a_out, yf_rw, yb_rw, bonus, g, rw_gn_g[0].reshape(1, HALF), rw_gn_b[0].reshape(1, HALF), ones_bd],
        [half, half, half, half, half, _full((1, HALF)), _full((1, HALF)), _full((HALF, HALF))],
        x, mod0, norm_g[0, 1], e_w_out[0].astype(BF16), rw_pad[0], rb_pad[0])
    x1 = _moe(0, y0, xp0, ti0, tg0, rk0, cnt0, mod0, ex_w_gu, ex_b_gu, ex_w_dn, ex_b_dn)

    zc, zr = _in_proj(x1, norm_g[1, 0], mod1, o_w_in[0].astype(BF16), (3 * HALF, 3 * HALF))
    cos_tab, sin_tab = _rope_tables()
    qkg = jnp.tile(da_qk_g[0], (1, HALF // HEAD))
    cq, ck, ck_raw, rqk = _odd_prep(zc, zr, cos_tab, sin_tab, qkg, ones_bd)
    lambda_init = 0.8 - 0.6 * math.exp(-0.3 * 1)
    lv = da_lam[0]
    lam = jnp.exp(jnp.sum(lv[0] * lv[1])) - jnp.exp(jnp.sum(lv[2] * lv[3])) + lambda_init
    c_out = _attn(cq, ck, zc, 0, N_PROMPT, L_PROMPT, lam, da_subln_g[0], 1.0 - lambda_init,
                  jnp.zeros((T, HALF), F32))
    ctx_k = cache_k_diff[:, 0].reshape(N_SAMPLE, PAST, HALF)
    ctx_v = cache_v_diff[:, 0].reshape(N_SAMPLE, PAST, HALF)
    c_out = _attn(cq, ck, zc, T_PROMPT, N_SAMPLE, L_SAMPLE, lam, da_subln_g[0], 1.0 - lambda_init,
                  c_out, ctx_k, ctx_v)
    sr = jnp.moveaxis(state_retention[:, 0], 1, 0)
    zr0 = jnp.zeros_like(sr)
    s0_sample = jnp.stack([jnp.concatenate([sr[:, :, 0], zr0[:, :, 0]], axis=-2),
                           jnp.concatenate([zr0[:, :, 1], sr[:, :, 1]], axis=-2),
                           jnp.concatenate([sr[:, :, 2], zr0[:, :, 2]], axis=-2),
                           jnp.concatenate([zr0[:, :, 3], sr[:, :, 3]], axis=-2)], axis=2)
    s0_ret = jnp.concatenate([jnp.zeros((2, N_PROMPT, 4, LANES, LANES), F32), s0_sample], axis=1)
    of_ret, ob_ret, rfin = _retention(rqk, zr, s0_ret)
    rfin_p = rfin[:, :N_PROMPT]
    new_ret = jnp.stack([rfin_p[:, :, 0, 0:HEAD], rfin_p[:, :, 1, HEAD:], rfin_p[:, :, 2, 0:HEAD],
                         rfin_p[:, :, 3, HEAD:]], axis=2)
    new_ret = jnp.moveaxis(new_ret, 0, 1)[:, None]
    y1, xp1, ti1, tg1, rk1, cnt1 = _out_proj(
        False,
        [c_out, of_ret, ob_ret, zr, ret_gn_g[0].reshape(1, HALF)],
        [half, half, half, pl.BlockSpec((TM, HALF), lambda i: (i, 2)), _full((1, HALF))],
        x1, mod1, norm_g[1, 1], o_w_out[0].astype(BF16), rw_pad[1], rb_pad[1])
    y_fin = _moe(1, y1, xp1, ti1, tg1, rk1, cnt1, mod1, ex_w_gu, ex_b_gu, ex_w_dn, ex_b_dn)

    new_k = ck_raw[:T_PROMPT].reshape(N_PROMPT, 1, L_PROMPT, 4, LANES)
    new_v = zc[:T_PROMPT, 2 * HALF:3 * HALF].reshape(N_PROMPT, 1, L_PROMPT, 4, LANES)
    return (y_fin[:T_PROMPT].reshape(N_PROMPT, L_PROMPT, D), y_fin[T_PROMPT:].reshape(N_SAMPLE, L_SAMPLE, D),
            new_rwkv, new_k, new_v, new_ret)
```

```python
import functools
import math

import numpy as np
import jax
import jax.numpy as jnp
from jax import lax
from jax.experimental import pallas as pl
from jax.experimental.pallas import tpu as pltpu

F32 = jnp.float32
BF16 = jnp.bfloat16
I32 = jnp.int32

D = 1024
N_PROMPT, L_PROMPT = 16, 256
N_SAMPLE, L_SAMPLE = 4, 1024
N_SEQ = N_PROMPT + N_SAMPLE
PAST = 256
T_PROMPT = N_PROMPT * L_PROMPT
T_SAMPLE = N_SAMPLE * L_SAMPLE
T = T_PROMPT + T_SAMPLE
TM = 256
NT = T // TM
PROMPT_TILES = T_PROMPT // TM
TILES_PER_SAMPLE = L_SAMPLE // TM
N_MOD = 6
HALF = 512
B_COLS = 1792
HEAD = 64
W_DECAY_SCALE = math.exp(-0.5)
RWKV_GN_EPS = 64e-5
RW_CHUNK = 64
RET_CHUNK = 128
RET_EXP = ((5.0, 7.0, 9.0, 11.0), (6.0, 8.0, 10.0, 12.0))
N_EXPERTS = 32
TOP_K = 4
SWIGLU_LIMIT = 7.0
SWIGLU_ALPHA = 1.702
N_ASSIGN = T * TOP_K
TMX = 512
MOE_TILES = N_ASSIGN // TMX + N_EXPERTS
R_PAD = MOE_TILES * TMX
LANES = 128

NN = (((1,), (0,)), ((), ()))
NT_DIMS = (((1,), (1,)), ((), ()))
TN = (((0,), (0,)), ((), ()))


def _group(i):
    return jnp.where(i < PROMPT_TILES, 0, 1 + (i - PROMPT_TILES) // TILES_PER_SAMPLE)


def _mm(a, b, dims=NN, passes=1):
    dg = functools.partial(lax.dot_general, dimension_numbers=dims, preferred_element_type=F32)
    if passes == 1:
        return dg(a.astype(BF16), b.astype(BF16))
    a = a.astype(F32)
    b = b.astype(F32)
    ah = a.astype(BF16)
    al = (a - ah.astype(F32)).astype(BF16)
    bh = b.astype(BF16)
    if passes == 2:
        assert dims[0][0] == (1,)
        m = a.shape[0]
        both = dg(jnp.concatenate([ah, al], axis=0), bh)
        return both[0:m] + both[m:2 * m]
    bl = (b - bh.astype(F32)).astype(BF16)
    if dims[0][0] == (1,):
        m = a.shape[0]
        both = dg(jnp.concatenate([ah, al], axis=0), bh)
        return both[0:m] + (dg(ah, bl) + both[m:2 * m])
    return dg(ah, bh) + (dg(ah, bl) + dg(al, bh))


def _group_sum(x, ones_bd):
    xh = x.astype(BF16)
    xl = (x - xh.astype(F32)).astype(BF16)
    return (jnp.dot(xh, ones_bd, preferred_element_type=F32)
            + jnp.dot(xl, ones_bd, preferred_element_type=F32))


def _full(shape):
    nd = len(shape)
    return pl.BlockSpec(shape, lambda *_: (0,) * nd)


def _params(sem, vmem_mb=None):
    kw = {}
    if vmem_mb is not None:
        kw["vmem_limit_bytes"] = vmem_mb * 1024 * 1024
    return pltpu.CompilerParams(dimension_semantics=sem, **kw)


def _seq_tables(chunk):
    blk_f, blk_b, first, last, seq = [], [], [], [], []
    row = 0
    for s in range(N_SEQ):
        n = (L_PROMPT if s < N_PROMPT else L_SAMPLE) // chunk
        base = row // chunk
        for j in range(n):
            blk_f.append(base + j)
            blk_b.append(base + n - 1 - j)
            first.append(int(j == 0))
            last.append(int(j == n - 1))
            seq.append(s)
        row += n * chunk
    return tuple(np.asarray(a, np.int32) for a in (blk_f, blk_b, first, last, seq))


def _adaln_kernel(c_ref, w_ref, b_ref, o_ref):
    c = c_ref[...]
    s = c * jax.nn.sigmoid(c)
    o_ref[0] = _mm(s, w_ref[0], NN, 3) + b_ref[0]


def _adaln(cvec8, ada_w, ada_b):
    depth, _, n = ada_w.shape
    bn = 1536
    return pl.pallas_call(
        _adaln_kernel,
        out_shape=jax.ShapeDtypeStruct((depth, 8, n), F32),
        grid=(depth, n // bn),
        in_specs=[pl.BlockSpec((8, D), lambda l, j: (0, 0)),
                  pl.BlockSpec((1, D, bn), lambda l, j: (l, 0, j)),
                  pl.BlockSpec((1, 1, bn), lambda l, j: (l, 0, j))],
        out_specs=pl.BlockSpec((1, 8, bn), lambda l, j: (l, 0, j)),
        compiler_params=_params(("arbitrary", "arbitrary"), 40),
        name="adaln",
    )(cvec8, ada_w, ada_b.reshape(depth, 1, n))


def _in_kernel(x_ref, g_ref, mod_ref, w_ref, *outs, splits):
    x = x_ref[...]
    mod = mod_ref[0]
    y = x * lax.rsqrt(jnp.mean(x * x, axis=-1, keepdims=True) + 1e-6) * g_ref[...]
    h = (y * (1.0 + mod[:, D:2 * D]) + mod[:, 0:D]).astype(BF16)
    off = 0
    for o_ref, n in zip(outs, splits):
        o_ref[...] = jnp.dot(h, w_ref[:, off:off + n], preferred_element_type=F32)
        off += n


def _in_proj(x, g, mod, w_bf16, splits):
    n = w_bf16.shape[1]
    row = lambda i: (i, 0)
    return pl.pallas_call(
        functools.partial(_in_kernel, splits=splits),
        out_shape=[jax.ShapeDtypeStruct((T, s), F32) for s in splits],
        grid=(NT,),
        in_specs=[pl.BlockSpec((TM, D), row), _full((1, D)),
                  pl.BlockSpec((1, 1, N_MOD * D), lambda i: (_group(i), 0, 0)), _full((D, n))],
        out_specs=[pl.BlockSpec((TM, s), row) for s in splits],
        compiler_params=_params(("arbitrary",), 48),
        name="in_proj",
    )(x, g.reshape(1, D), mod, w_bf16)


def _gelu(x):
    return 0.5 * x * (1.0 + lax.erf(x * (1.0 / math.sqrt(2.0))))


def _sgu_kernel(za_ref, lng_ref, ws_ref, bs_ref, o_ref):
    u = _gelu(za_ref[:, 0:HALF])
    va = _gelu(za_ref[:, HALF:2 * HALF])
    mu = jnp.mean(va, axis=-1, keepdims=True)
    dv = va - mu
    var = jnp.mean(dv * dv, axis=-1, keepdims=True)
    vn = dv * lax.rsqrt(var + 1e-5) * lng_ref[...]
    lane = lax.broadcasted_iota(I32, (LANES, LANES), 1)
    first = lane < HEAD
    for c in range(TM // LANES):
        rows = slice(c * LANES, (c + 1) * LANES)
        for p in range(HALF // LANES):
            cols = slice(p * LANES, (p + 1) * LANES)
            vp = vn[rows, cols]
            s = (jnp.dot(ws_ref[2 * p], jnp.where(first, vp, 0.0).astype(BF16), preferred_element_type=F32)
                 + jnp.dot(ws_ref[2 * p + 1], jnp.where(first, 0.0, vp).astype(BF16), preferred_element_type=F32))
            o_ref[rows, cols] = u[rows, cols] * (s + bs_ref[:, cols])


def _sgu(za, ln_g, w_s_bf16, bs_full):
    return pl.pallas_call(
        _sgu_kernel,
        out_shape=jax.ShapeDtypeStruct((T, HALF), F32),
        grid=(NT,),
        in_specs=[pl.BlockSpec((TM, 2 * HALF), lambda i: (i, 0)), _full((1, HALF)),
                  _full((8, LANES, LANES)), _full((LANES, HALF))],
        out_specs=pl.BlockSpec((TM, HALF), lambda i: (i, 0)),
        compiler_params=_params(("arbitrary",)),
        name="sgu",
    )(za, ln_g.reshape(1, HALF), w_s_bf16, bs_full)


def _rwkv_prep_kernel(zb_ref, zp_ref, zn_ref, mu_ref, kk_ref, ka_ref, rk_ref, w0_ref, a0_ref,
                      wup_ref, aup_ref, gup_ref, ones_ref,
                      r_ref, v_ref, kkn_ref, bonus_ref, g_ref, lw_ref, kt_ref, b_ref):
    i = pl.program_id(0)
    in_sample = i >= PROMPT_TILES
    pos = (i - PROMPT_TILES) % TILES_PER_SAMPLE
    is_first = jnp.logical_or(jnp.logical_not(in_sample), pos == 0)
    is_last = jnp.logical_or(jnp.logical_not(in_sample), pos == TILES_PER_SAMPLE - 1)
    zb = zb_ref[...]
    prev_row = jnp.where(is_first, 0.0, zp_ref[7:8, :])
    next_row = jnp.where(is_last, 0.0, zn_ref[0:1, :])
    rowid = lax.broadcasted_iota(I32, (TM, 1), 0)
    zp = jnp.where(rowid == 0, prev_row, pltpu.roll(zb, 1, 0))
    zn = jnp.where(rowid == TM - 1, next_row, pltpu.roll(zb, TM - 1, 0))
    zs = zb + mu_ref[0:1, :] * (zp - zb) + mu_ref[1:2, :] * (zn - zb)
    r = zs[:, 0:HALF]
    k = zs[:, HALF:2 * HALF]
    v = zs[:, 2 * HALF:3 * HALF]
    wa = zs[:, 3 * HALF:3 * HALF + LANES]
    gd = zs[:, 3 * HALF + LANES:B_COLS]
    ones_bd = ones_ref[...]
    r_ref[...] = r
    v_ref[...] = v
    g_ref[...] = jnp.dot(jax.nn.sigmoid(gd).astype(BF16), gup_ref[...], preferred_element_type=F32)
    kk = k * kk_ref[...]
    kkn = kk / jnp.maximum(jnp.sqrt(_group_sum(kk * kk, ones_bd)), 1e-6)
    kkn_ref[...] = kkn
    bonus_ref[...] = _group_sum(r * k * rk_ref[...], ones_bd) * v
    tw = jnp.tanh(wa).astype(BF16)
    wa16 = wa.astype(BF16)
    for dd in range(2):
        lw_ref[dd] = -W_DECAY_SCALE * jax.nn.sigmoid(
            w0_ref[dd:dd + 1, :] + jnp.dot(tw, wup_ref[dd], preferred_element_type=F32))
        a = jax.nn.sigmoid(a0_ref[dd:dd + 1, :] + jnp.dot(wa16, aup_ref[dd], preferred_element_type=F32))
        kt_ref[dd] = k * (1.0 + (a - 1.0) * ka_ref[...])
        b_ref[dd] = a * kkn


def _rwkv_prep(zb, mu, k_k, k_a, r_k, w0, a0, wup_pad, aup_pad, g_up, ones_bd):
    row = lambda i: (i, 0)
    halo = TM // 8
    one = jax.ShapeDtypeStruct((T, HALF), F32)
    two = jax.ShapeDtypeStruct((2, T, HALF), F32)
    o1 = pl.BlockSpec((TM, HALF), row)
    o2 = pl.BlockSpec((2, TM, HALF), lambda i: (0, i, 0))
    return pl.pallas_call(
        _rwkv_prep_kernel,
        out_shape=[one, one, one, one, one, two, two, two],
        grid=(NT,),
        in_specs=[pl.BlockSpec((TM, B_COLS), row),
                  pl.BlockSpec((8, B_COLS), lambda i: (jnp.maximum(i * halo - 1, 0), 0)),
                  pl.BlockSpec((8, B_COLS), lambda i: (jnp.minimum((i + 1) * halo, T // 8 - 1), 0)),
                  _full((2, B_COLS)), _full((1, HALF)), _full((1, HALF)), _full((1, HALF)),
                  _full((2, HALF)), _full((2, HALF)),
                  _full((2, LANES, HALF)), _full((2, LANES, HALF)), _full((LANES, HALF)),
                  _full((HALF, HALF))],
        out_specs=[o1, o1, o1, o1, o1, o2, o2, o2],
        compiler_params=_params(("arbitrary",), 48),
        name="rwkv_prep",
    )(zb, zb, zb, mu, k_k.reshape(1, HALF), k_a.reshape(1, HALF), r_k.reshape(1, HALF), w0, a0,
      wup_pad, aup_pad, g_up, ones_bd)


def _rwkv_chunks(dirs):
    C = RW_CHUNK
    ti = lax.broadcasted_iota(I32, (C, C), 0)
    tj = lax.broadcasted_iota(I32, (C, C), 1)
    bi = lax.broadcasted_iota(I32, (LANES, LANES), 0)
    bj = lax.broadcasted_iota(I32, (LANES, LANES), 1)
    same = (bi >> 6) == (bj >> 6)
    pi = bi & (C - 1)
    pj = bj & (C - 1)
    eye = (bi == bj).astype(F32)
    h0 = lax.broadcasted_iota(I32, (C, LANES), 1) < HEAD

    def stack(x):
        return jnp.concatenate([jnp.where(h0, x, 0.0), jnp.where(h0, 0.0, x)], axis=0)

    def fold(x):
        return x[0:C] + x[C:2 * C]

    chains = []
    for rev, r, v, kk, lw, kt, b, s_ref, y_ref in dirs:
        tri = jnp.where((tj >= ti) if rev else (tj <= ti), 1.0, 0.0).astype(F32)
        p1 = lw.astype(BF16)
        r1 = lw - p1.astype(F32)
        p2 = r1.astype(BF16)
        p3 = (r1 - p2.astype(F32)).astype(BF16)
        cs3 = jnp.dot(tri.astype(BF16), jnp.concatenate([p1, p2, p3], axis=1), preferred_element_type=F32)
        cs = cs3[:, 0:HALF] + (cs3[:, HALF:2 * HALF] + cs3[:, 2 * HALF:3 * HALF])
        ctot = cs[0:1, :] if rev else cs[C - 1:C, :]
        e_neg = jnp.exp(-cs)
        e_tail = jnp.exp(ctot - cs)
        q1 = kk * jnp.exp(cs - lw)
        k1 = kt * e_neg
        b1 = b * e_neg
        r1 = r * jnp.exp(cs)
        k2 = kt * e_tail
        b2 = b * e_tail
        e_tot = jnp.exp(ctot)
        strict = jnp.logical_and(same, (pj > pi) if rev else (pj < pi))
        incl = jnp.logical_and(same, (pj >= pi) if rev else (pj <= pi))
        for p in range(HALF // LANES):
            cols = slice(p * LANES, (p + 1) * LANES)
            chains.append(dict(p=p, cols=cols, strict=strict, incl=incl, s_ref=s_ref, y_ref=y_ref,
                               q1=q1[:, cols], k1=k1[:, cols], b1=b1[:, cols], r1=r1[:, cols],
                               k2=k2[:, cols], b2=b2[:, cols], v=v[:, cols], e_tot=e_tot[:, cols]))

    for ch in chains:
        lhs = jnp.concatenate([stack(ch["q1"]), stack(ch["r1"])], axis=0)
        rhs = jnp.concatenate([ch["k1"], ch["k1"], ch["b1"], ch["b1"]], axis=0)
        gm = _mm(lhs, rhs, NT_DIMS, 3)
        ch["mk"] = jnp.where(ch["strict"], gm[0:2 * C, 0:2 * C], 0.0)
        ch["mb"] = jnp.where(ch["strict"], gm[0:2 * C, 2 * C:4 * C], 0.0)
        ch["nk"] = jnp.where(ch["incl"], gm[2 * C:4 * C, 0:2 * C], 0.0)
        ch["nb"] = jnp.where(ch["incl"], gm[2 * C:4 * C, 2 * C:4 * C], 0.0)
        ch["tinv"] = eye - jnp.where((pi >> 1) == (pj >> 1), ch["mb"], 0.0)
    size = 2
    while size < C:
        sh = size.bit_length() - 1
        blk = jnp.logical_and((pi >> (sh + 1)) == (pj >> (sh + 1)), (pi >> sh) != (pj >> sh))
        for ch in chains:
            ch["tn"] = _mm(ch["tinv"], jnp.where(blk, ch["mb"], 0.0), NN, 2)
        for ch in chains:
            ch["tinv"] = ch["tinv"] - _mm(ch["tn"], ch["tinv"], NN, 3)
        size *= 2
    for ch in chains:
        vst = stack(ch["v"])
        ch["mkv"] = fold(_mm(ch["mk"], vst, NN, 2))
        ch["nkv"] = fold(_mm(ch["nk"], vst, NN, 2))
        ch["s"] = ch["s_ref"][ch["p"]]
        ch["qr"] = _mm(jnp.concatenate([ch["q1"], ch["r1"]], axis=0), ch["s"], NT_DIMS, 3)
    for ch in chains:
        ch["u"] = fold(_mm(ch["tinv"], stack(ch["mkv"] + ch["qr"][0:C]), NN, 3))
    for ch in chains:
        ch["y_ref"][:, ch["cols"]] = ch["qr"][C:2 * C] + ch["nkv"] - fold(_mm(ch["nb"], stack(ch["u"]), NN, 2))
        upd = _mm(jnp.concatenate([ch["v"], ch["u"]], axis=0),
                  jnp.concatenate([ch["k2"], -ch["b2"]], axis=0), TN, 3)
        ch["s_ref"][ch["p"]] = ch["s"] * ch["e_tot"] + jnp.where(same, upd, 0.0)


def _rwkv_scan_kernel(bf_ref, bb_ref, first_ref, last_ref, seq_ref,
                      rf_ref, vf_ref, kkf_ref, lwf_ref, ktf_ref, bfw_ref,
                      rb_ref, vb_ref, kkb_ref, lwb_ref, ktb_ref, bbw_ref, s0_ref,
                      yf_ref, yb_ref, sfin_ref, s_ref):
    step = pl.program_id(0)

    @pl.when(first_ref[step] == 1)
    def _():
        s_ref[...] = s0_ref[:, 0]

    _rwkv_chunks([
        (False, rf_ref[...], vf_ref[...], kkf_ref[...], lwf_ref[0], ktf_ref[0], bfw_ref[0], s_ref.at[0], yf_ref),
        (True, rb_ref[...], vb_ref[...], kkb_ref[...], lwb_ref[0], ktb_ref[0], bbw_ref[0], s_ref.at[1], yb_ref)])

    @pl.when(last_ref[step] == 1)
    def _():
        sfin_ref[:, 0] = s_ref[...]


def _rwkv_scan(r, v, kk, lw, kt, b, s0_bd):
    C = RW_CHUNK
    tabs = _seq_tables(C)
    fwd = lambda i, bf, bb, fi, la, sq: (bf[i], 0)
    bwd = lambda i, bf, bb, fi, la, sq: (bb[i], 0)
    fwd3 = lambda i, bf, bb, fi, la, sq: (0, bf[i], 0)
    bwd3 = lambda i, bf, bb, fi, la, sq: (1, bb[i], 0)
    st = pl.BlockSpec((2, 1, 4, LANES, LANES), lambda i, bf, bb, fi, la, sq: (0, sq[i], 0, 0, 0))
    one_f, one_b = pl.BlockSpec((C, HALF), fwd), pl.BlockSpec((C, HALF), bwd)
    two_f, two_b = pl.BlockSpec((1, C, HALF), fwd3), pl.BlockSpec((1, C, HALF), bwd3)
    return pl.pallas_call(
        _rwkv_scan_kernel,
        out_shape=[jax.ShapeDtypeStruct((T, HALF), F32), jax.ShapeDtypeStruct((T, HALF), F32),
                   jax.ShapeDtypeStruct((2, N_SEQ, 4, LANES, LANES), F32)],
        grid_spec=pltpu.PrefetchScalarGridSpec(
            num_scalar_prefetch=5, grid=(len(tabs[0]),),
            in_specs=[one_f, one_f, one_f, two_f, two_f, two_f,
                      one_b, one_b, one_b, two_b, two_b, two_b, st],
            out_specs=[one_f, one_b, st],
            scratch_shapes=[pltpu.VMEM((2, 4, LANES, LANES), F32)]),
        compiler_params=_params(("arbitrary",)),
        name="rwkv_scan",
    )(*tabs, r, v, kk, lw, kt, b, r, v, kk, lw, kt, b, s0_bd)


def _rope(x, cos, sin_signed, first16):
    w = x.shape[1]
    partner = jnp.where(first16, pltpu.roll(x, w - 16, 1), pltpu.roll(x, 16, 1))
    return x * cos + partner * sin_signed


def _odd_prep_kernel(zc_ref, zr_ref, cos_ref, sin_ref, qkg_ref, ones_ref, cq_ref, ck_ref, ckraw_ref, rqk_ref):
    ones_bd = ones_ref[...]
    cos = cos_ref[...]
    sin = sin_ref[...]
    lane = lax.broadcasted_iota(I32, (TM, HALF), 1)
    first16 = (lane & 31) < 16
    for idx, (o_ref, raw_ref) in enumerate(((cq_ref, None), (ck_ref, ckraw_ref))):
        x = zc_ref[:, idx * HALF:(idx + 1) * HALF]
        ms = _group_sum(x * x, ones_bd) * (1.0 / HEAD)
        xn = x * lax.rsqrt(ms + 1e-6) * qkg_ref[idx:idx + 1, :]
        if raw_ref is not None:
            raw_ref[...] = xn
        o_ref[...] = _rope(xn, cos, sin, first16)
    rqk = _rope(zr_ref[...], cos, sin, first16)
    rqk_ref[...] = jnp.where(lane < HALF // 2, rqk * (HEAD ** -0.5), rqk)


def _odd_prep(zc, zr, cos_tab, sin_tab, qkg_tiled, ones_bd):
    row = lambda i: (i, 0)
    tab = lambda i: (jnp.where(i < PROMPT_TILES, 0, 1 + (i - PROMPT_TILES) % TILES_PER_SAMPLE), 0)
    one = jax.ShapeDtypeStruct((T, HALF), F32)
    o1 = pl.BlockSpec((TM, HALF), row)
    return pl.pallas_call(
        _odd_prep_kernel,
        out_shape=[one, one, one, one], grid=(NT,),
        in_specs=[pl.BlockSpec((TM, 2 * HALF), row), pl.BlockSpec((TM, HALF), row),
                  pl.BlockSpec((TM, HALF), tab), pl.BlockSpec((TM, HALF), tab),
                  _full((2, HALF)), _full((HALF, HALF))],
        out_specs=[o1, o1, o1, o1],
        compiler_params=_params(("arbitrary",)),
        name="odd_prep",
    )(zc, zr, cos_tab, sin_tab, qkg_tiled, ones_bd)


def _attn_kernel(*refs, has_ctx, one_minus_li):
    if has_ctx:
        q_ref, k_ref, v_ref, kc_ref, vc_ref, lam_ref, sg_ref, _, o_ref = refs
    else:
        q_ref, k_ref, v_ref, lam_ref, sg_ref, _, o_ref = refs
    lam = lam_ref[...]
    lane = lax.broadcasted_iota(I32, (LANES, LANES), 1)
    m0 = lane < HEAD
    scale = HEAD ** -0.5
    for h in range(4):
        cols = slice(h * LANES, (h + 1) * LANES)
        qp = q_ref[:, cols]
        segs = [(k_ref[:, cols], v_ref[:, cols])]
        if has_ctx:
            segs.append((kc_ref[0, :, cols], vc_ref[0, :, cols]))
        outs = []
        for qm in (jnp.where(m0, qp, 0.0), jnp.where(m0, 0.0, qp)):
            qm16 = qm.astype(BF16)
            ss = [lax.dot_general(qm16, ks.astype(BF16), NT_DIMS, preferred_element_type=F32) * scale
                  for ks, _ in segs]
            mx = ss[0].max(axis=-1, keepdims=True)
            for s_ in ss[1:]:
                mx = jnp.maximum(mx, s_.max(axis=-1, keepdims=True))
            ps = [jnp.exp(s_ - mx) for s_ in ss]
            den = ps[0].sum(axis=-1, keepdims=True)
            for p_ in ps[1:]:
                den = den + p_.sum(axis=-1, keepdims=True)
            outs.append([p_ / den for p_ in ps])
        acc = None
        for si, (_, vs) in enumerate(segs):
            amap = outs[0][si] - lam * outs[1][si]
            t = jnp.dot(amap.astype(BF16), vs.astype(BF16), preferred_element_type=F32)
            acc = t if acc is None else acc + t
        nrm = acc * lax.rsqrt(jnp.mean(acc * acc, axis=-1, keepdims=True) + 1e-6) * sg_ref[...]
        o_ref[:, cols] = nrm * one_minus_li


def _attn(cq, ck, zc, row0, n_seq, seq_len, lam, subln_g, one_minus_li, prev, ctx_k=None, ctx_v=None):
    nq = seq_len // LANES
    qb0 = row0 // LANES
    sb0 = row0 // seq_len
    in_specs = [pl.BlockSpec((LANES, HALF), lambda s, q: (qb0 + s * nq + q, 0)),
                pl.BlockSpec((seq_len, HALF), lambda s, q: (sb0 + s, 0)),
                pl.BlockSpec((seq_len, HALF), lambda s, q: (sb0 + s, 2))]
    args = [cq, ck, zc]
    if ctx_k is not None:
        in_specs += [pl.BlockSpec((1, PAST, HALF), lambda s, q: (s, 0, 0))] * 2
        args += [ctx_k, ctx_v]
    in_specs += [_full((1, 1)), _full((1, LANES))]
    args += [lam.reshape(1, 1), subln_g.reshape(1, LANES)]
    in_specs.append(pl.BlockSpec(memory_space=pl.ANY))
    args.append(prev)
    aliases = {len(args) - 1: 0}
    return pl.pallas_call(
        functools.partial(_attn_kernel, has_ctx=ctx_k is not None, one_minus_li=one_minus_li),
        out_shape=jax.ShapeDtypeStruct((T, HALF), F32),
        grid=(n_seq, nq), in_specs=in_specs,
        out_specs=pl.BlockSpec((LANES, HALF), lambda s, q: (qb0 + s * nq + q, 0)),
        input_output_aliases=aliases,
        compiler_params=_params(("arbitrary", "arbitrary"), 48),
        name="diff_attn",
    )(*args)


_LOG_GAMMA = tuple(tuple(float(np.log1p(-np.exp2(-np.float32(e)), dtype=np.float32)) for e in es)
                   for es in RET_EXP)


def _ret_chunks(dirs):
    C = RET_CHUNK
    ii = lax.broadcasted_iota(I32, (C, C), 0)
    jj = lax.broadcasted_iota(I32, (C, C), 1)
    ri = lax.broadcasted_iota(I32, (C, 1), 0)
    lane = lax.broadcasted_iota(I32, (C, LANES), 1)
    chains = []
    for rev, qk_ref, v_ref, s_ref, o_ref in dirs:
        mask = (jj > ii) if rev else (jj <= ii)
        dist = jnp.where(mask, (jj - ii) if rev else (ii - jj), 0).astype(F32)
        kpow = (ri if rev else (C - 1 - ri)).astype(F32)
        qpow = ((C - ri) if rev else (ri + 1)).astype(F32)
        for h in range(4):
            lg = _LOG_GAMMA[1 if rev else 0][h]
            p = h // 2
            hm = (lane < HEAD) if h % 2 == 0 else (lane >= HEAD)
            qp = jnp.where(hm, qk_ref[:, p * LANES:(p + 1) * LANES], 0.0)
            kp = jnp.where(hm, qk_ref[:, HALF // 2 + p * LANES:HALF // 2 + (p + 1) * LANES], 0.0)
            chains.append(dict(
                h=h, lg=lg, s_ref=s_ref, o_ref=o_ref, q16=qp.astype(BF16), k16=kp.astype(BF16),
                qw16=(qp * jnp.exp(lg * qpow)).astype(BF16), kw16=(kp * jnp.exp(lg * kpow)).astype(BF16),
                v16=v_ref[:, h * LANES:(h + 1) * LANES].astype(BF16),
                decay=jnp.where(mask, jnp.exp(lg * dist), 0.0)))
    for ch in chains:
        ch["sc"] = (lax.dot_general(ch["q16"], ch["k16"], NT_DIMS, preferred_element_type=F32)
                    * ch["decay"]).astype(BF16)
        ch["s"] = ch["s_ref"][ch["h"]]
    for ch in chains:
        ch["o"] = (jnp.dot(ch["sc"], ch["v16"], preferred_element_type=F32)
                   + jnp.dot(ch["qw16"], ch["s"].astype(BF16), preferred_element_type=F32))
        ch["kv"] = lax.dot_general(ch["kw16"], ch["v16"], TN, preferred_element_type=F32)
    for ch in chains:
        h = ch["h"]
        ch["o_ref"][:, h * LANES:(h + 1) * LANES] = ch["o"]
        ch["s_ref"][h] = math.exp(ch["lg"] * C) * ch["s"] + ch["kv"]


def _ret_kernel(bf_ref, bb_ref, first_ref, last_ref, seq_ref,
                qkf_ref, vf_ref, qkb_ref, vb_ref, s0_ref, of_ref, ob_ref, sfin_ref, s_ref):
    step = pl.program_id(0)

    @pl.when(first_ref[step] == 1)
    def _():
        s_ref[...] = s0_ref[:, 0]

    _ret_chunks([(False, qkf_ref, vf_ref, s_ref.at[0], of_ref), (True, qkb_ref, vb_ref, s_ref.at[1], ob_ref)])

    @pl.when(last_ref[step] == 1)
    def _():
        sfin_ref[:, 0] = s_ref[...]


def _retention(rqk, zr, s0):
    C = RET_CHUNK
    tabs = _seq_tables(C)
    st = pl.BlockSpec((2, 1, 4, LANES, LANES), lambda i, bf, bb, fi, la, sq: (0, sq[i], 0, 0, 0))
    spec = lambda use_b, col: pl.BlockSpec(
        (C, HALF), lambda i, bf, bb, fi, la, sq: ((bb if use_b else bf)[i], col))
    return pl.pallas_call(
        _ret_kernel,
        out_shape=[jax.ShapeDtypeStruct((T, HALF), F32), jax.ShapeDtypeStruct((T, HALF), F32),
                   jax.ShapeDtypeStruct((2, N_SEQ, 4, LANES, LANES), F32)],
        grid_spec=pltpu.PrefetchScalarGridSpec(
            num_scalar_prefetch=5, grid=(len(tabs[0]),),
            in_specs=[spec(False, 0), spec(False, 1), spec(True, 0), spec(True, 1), st],
            out_specs=[spec(False, 0), spec(True, 0), st],
            scratch_shapes=[pltpu.VMEM((2, 4, LANES, LANES), F32)]),
        compiler_params=_params(("arbitrary",)),
        name="retention",
    )(*tabs, rqk, zr, rqk, zr, s0)


def _out_kernel(*refs, even):
    if even:
        (a_ref, yf_ref, yb_ref, bonus_ref, g_ref, gng_ref, gnb_ref, ones_ref,
         x_ref, mod_ref, ng_ref, wo_ref, rw_ref, rb_ref,
         y_ref, xp_ref, ti_ref, tg_ref, rk_ref, cnt_ref, run_ref) = refs
        ones_bd = ones_ref[...]
        ys = yf_ref[...] + yb_ref[...]
        mu = _group_sum(ys, ones_bd) * (1.0 / HEAD)
        dv = ys - mu
        var = _group_sum(dv * dv, ones_bd) * (1.0 / HEAD)
        yn = dv * lax.rsqrt(var + RWKV_GN_EPS) * gng_ref[...] + gnb_ref[...]
        left = a_ref[...]
        right = (yn + bonus_ref[...]) * g_ref[...]
    else:
        (c_ref, of_ref, ob_ref, rg_ref, gng_ref,
         x_ref, mod_ref, ng_ref, wo_ref, rw_ref, rb_ref,
         y_ref, xp_ref, ti_ref, tg_ref, rk_ref, cnt_ref, run_ref) = refs
        left = c_ref[...]
        rg = rg_ref[...]
        gate = rg * jax.nn.sigmoid(rg)
        os_ = of_ref[...] + ob_ref[...]
        parts = []
        for h in range(4):
            oh = os_[:, h * LANES:(h + 1) * LANES]
            mu = jnp.mean(oh, axis=-1, keepdims=True)
            dv = oh - mu
            var = jnp.mean(dv * dv, axis=-1, keepdims=True)
            parts.append(dv * lax.rsqrt(var + 1e-5))
        right = gate * (jnp.concatenate(parts, axis=1) * gng_ref[...])
    mod = mod_ref[0]
    o = (jnp.dot(left.astype(BF16), wo_ref[0:HALF, :], preferred_element_type=F32)
         + jnp.dot(right.astype(BF16), wo_ref[HALF:2 * HALF, :], preferred_element_type=F32))
    y = x_ref[...] + mod[:, 2 * D:3 * D] * o
    y_ref[...] = y
    yn2 = y * lax.rsqrt(jnp.mean(y * y, axis=-1, keepdims=True) + 1e-6) * ng_ref[...]
    t = yn2 * (1.0 + mod[:, 4 * D:5 * D]) + mod[:, 3 * D:4 * D]
    xp_ref[...] = t
    logits = _mm(t, rw_ref[...], NN, 3) + rb_ref[...]
    lane = lax.broadcasted_iota(I32, (TM, LANES), 1)
    neg = jnp.float32(-jnp.inf)
    lg = jnp.where(lane < N_EXPERTS, logits, neg)
    vals, hits = [], []
    for _ in range(TOP_K):
        m = jnp.max(lg, axis=-1, keepdims=True)
        ix = jnp.min(jnp.where(lg == m, lane, LANES), axis=-1, keepdims=True)
        hit = lane == ix
        vals.append(m)
        hits.append((ix, hit))
        lg = jnp.where(hit, neg, lg)
    es = [jnp.exp(vv - vals[0]) for vv in vals]
    den = es[0] + es[1] + es[2] + es[3]

    @pl.when(pl.program_id(0) == 0)
    def _():
        run_ref[...] = jnp.zeros_like(run_ref)

    member = jnp.zeros((TM, LANES), F32)
    for _, hit in hits:
        member = member + jnp.where(hit, 1.0, 0.0)
    ri = lax.broadcasted_iota(I32, (TM, TM), 0)
    ci = lax.broadcasted_iota(I32, (TM, TM), 1)
    before = jnp.where(ci < ri, 1.0, 0.0).astype(BF16)
    seen = run_ref[...] + jnp.dot(before, member.astype(BF16), preferred_element_type=F32)
    ti = jnp.zeros((TM, LANES), I32)
    tg = jnp.zeros((TM, LANES), F32)
    rk = jnp.zeros((TM, LANES), F32)
    for kk, (ix, hit) in enumerate(hits):
        ti = jnp.where(lane == kk, ix, ti)
        tg = jnp.where(lane == kk, es[kk] / den, tg)
        rk = jnp.where(lane == kk, jnp.sum(jnp.where(hit, seen, 0.0), axis=-1, keepdims=True), rk)
    ti_ref[...] = ti
    tg_ref[...] = tg
    rk_ref[...] = rk.astype(I32)
    run_ref[...] = run_ref[...] + jnp.sum(member, axis=0, keepdims=True)
    cnt_ref[...] = run_ref[...]


def _out_proj(even, mix_args, mix_specs, x, mod, norm_g, w_out_bf16, rw_pad, rb_pad):
    row = lambda i: (i, 0)
    modspec = pl.BlockSpec((1, 1, N_MOD * D), lambda i: (_group(i), 0, 0))
    in_specs = list(mix_specs) + [pl.BlockSpec((TM, D), row), modspec, _full((1, D)), _full((D, D)),
                                  _full((D, LANES)), _full((1, LANES))]
    args = list(mix_args) + [x, mod, norm_g.reshape(1, D), w_out_bf16, rw_pad, rb_pad]
    lane_i = jax.ShapeDtypeStruct((T, LANES), I32)
    lane_spec = pl.BlockSpec((TM, LANES), row)
    return pl.pallas_call(
        functools.partial(_out_kernel, even=even),
        out_shape=[jax.ShapeDtypeStruct((T, D), F32), jax.ShapeDtypeStruct((T, D), F32),
                   lane_i, jax.ShapeDtypeStruct((T, LANES), F32), lane_i,
                   jax.ShapeDtypeStruct((1, LANES), F32)],
        grid=(NT,), in_specs=in_specs,
        out_specs=[pl.BlockSpec((TM, D), row), pl.BlockSpec((TM, D), row),
                   lane_spec, lane_spec, lane_spec, _full((1, LANES))],
        scratch_shapes=[pltpu.VMEM((1, LANES), F32)],
        compiler_params=_params(("arbitrary",), 48),
        name="out_proj",
    )(*args)


def _route_kernel(cnt_ref, ti_ref, rk_ref, dest_ref, te_ref, nt_ref):
    cnt = cnt_ref[...].astype(I32)
    ntile = lax.shift_right_logical(cnt + (TMX - 1), TMX.bit_length() - 1)
    ei = lax.broadcasted_iota(I32, (LANES, LANES), 0)
    ej = lax.broadcasted_iota(I32, (LANES, LANES), 1)
    upto = jnp.where(ei <= ej, 1.0, 0.0).astype(BF16)
    ntile_f = jnp.broadcast_to(ntile.astype(F32), (8, LANES))
    tile_end = jnp.dot(ntile_f.astype(BF16), upto, preferred_element_type=F32)[0:1, :]
    row_start = (tile_end - ntile.astype(F32)) * float(TMX)
    lane = lax.broadcasted_iota(I32, (TM, LANES), 1)
    ti = ti_ref[...]
    rk = rk_ref[...]
    dest = jnp.zeros((TM, LANES), F32)
    for k in range(TOP_K):
        hit = lane == ti[:, k:k + 1]
        start = jnp.sum(jnp.where(hit, row_start, 0.0), axis=-1, keepdims=True)
        dest = jnp.where(lane == k, start + rk[:, k:k + 1].astype(F32), dest)
    dest_ref[...] = dest.astype(I32)

    @pl.when(pl.program_id(0) == 0)
    def _():
        lane1 = lax.broadcasted_iota(I32, (1, LANES), 1)
        n_tiles = jnp.max(tile_end, axis=-1, keepdims=True)
        last_e = jnp.max(jnp.where(cnt > 0, lane1, 0), axis=-1, keepdims=True)
        tile = lax.broadcasted_iota(I32, (TM, 1), 0).astype(F32)
        te = jnp.sum(jnp.where(tile_end <= tile, 1, 0), axis=-1, keepdims=True)
        te = jnp.where(tile < n_tiles, te, last_e)
        te_ref[...] = jnp.broadcast_to(te, (TM, LANES)).astype(I32)
        first_row = lax.broadcasted_iota(I32, (8, LANES), 0) == 0
        nt_ref[...] = jnp.where(first_row, n_tiles, tile_end).astype(I32)


def _route(cnt, ti, rk):
    row = lambda i: (i, 0)
    return pl.pallas_call(
        _route_kernel,
        out_shape=[jax.ShapeDtypeStruct((T, LANES), I32), jax.ShapeDtypeStruct((TM, LANES), I32),
                   jax.ShapeDtypeStruct((8, LANES), I32)],
        grid=(NT,),
        in_specs=[_full((1, LANES)), pl.BlockSpec((TM, LANES), row), pl.BlockSpec((TM, LANES), row)],
        out_specs=[pl.BlockSpec((TM, LANES), row), _full((TM, LANES)), _full((8, LANES))],
        compiler_params=_params(("arbitrary",)),
        name="moe_route",
    )(cnt, ti, rk)


def _row_copy(src_ref, src_row, dst_ref, dst_row, sem):
    return pltpu.make_async_copy(src_ref.at[pl.ds(src_row, 1)], dst_ref.at[pl.ds(dst_row, 1)], sem)


def _wait_tiles(n, src_ref, dst_ref, sem):
    for _ in range(n):
        pltpu.make_async_copy(src_ref, dst_ref, sem).wait()


def _dispatch_kernel(dest_ref, tend_ref, x_ref, xs_ref, zero_ref, sem):
    i = pl.program_id(0)

    @pl.when(i == 0)
    def _():
        zero_ref[...] = jnp.zeros_like(zero_ref)

        def last_tile(e, fn):
            end = tend_ref[e]
            begin = tend_ref[e - 1] if e > 0 else 0

            @pl.when(end > begin)
            def _():
                fn(pltpu.make_async_copy(zero_ref, xs_ref.at[pl.ds((end - 1) * TMX, TMX)], sem))

        def unused_tile(j):
            return pltpu.make_async_copy(zero_ref, xs_ref.at[pl.ds(j * TMX, TMX)], sem)

        def start_unused(j, carry):
            unused_tile(j).start()
            return carry

        def wait_unused(j, carry):
            unused_tile(j).wait()
            return carry

        n_used = tend_ref[N_EXPERTS - 1]
        for e in range(N_EXPERTS):
            last_tile(e, lambda c: c.start())
        lax.fori_loop(n_used, MOE_TILES, start_unused, 0)
        for e in range(N_EXPERTS):
            last_tile(e, lambda c: c.wait())
        lax.fori_loop(n_used, MOE_TILES, wait_unused, 0)

    base = i * (TM * TOP_K)

    def start(r, carry):
        for k in range(TOP_K):
            _row_copy(x_ref, r, xs_ref, dest_ref[base + r * TOP_K + k], sem).start()
        return carry

    lax.fori_loop(0, TM, start, 0, unroll=4)
    _wait_tiles(TOP_K, x_ref, xs_ref.at[pl.ds(0, TM)], sem)


def _dispatch(dest_flat, tile_end, xt):
    return pl.pallas_call(
        _dispatch_kernel,
        out_shape=jax.ShapeDtypeStruct((R_PAD, D), F32),
        grid_spec=pltpu.PrefetchScalarGridSpec(
            num_scalar_prefetch=2, grid=(NT,),
            in_specs=[pl.BlockSpec((TM, D), lambda i, d, te: (i, 0))],
            out_specs=pl.BlockSpec(memory_space=pl.ANY),
            scratch_shapes=[pltpu.VMEM((TMX, D), F32), pltpu.SemaphoreType.DMA(())]),
        compiler_params=_params(("arbitrary",)),
        name="moe_dispatch",
    )(dest_flat, tile_end, xt)


W_PARTS = 4


def _expert_weights(i, nt, te_ref, w_ref, wbuf_ref, wsem, w16_ref, group_ref):
    rows = w_ref.shape[1] // W_PARTS

    def fetch(e, buf):
        return [pltpu.make_async_copy(w_ref.at[e, pl.ds(p * rows, rows)], wbuf_ref.at[buf, pl.ds(p * rows, rows)],
                                      wsem.at[buf]) for p in range(W_PARTS)]

    @pl.when(i == 0)
    def _():
        group_ref[0] = 0
        for c in fetch(te_ref[0], 0):
            c.start()

    first = jnp.logical_or(i == 0, te_ref[i] != te_ref[jnp.maximum(i - 1, 0)])

    @pl.when(jnp.logical_and(first, i < nt))
    def _():
        cur = group_ref[0] % 2
        nxt = lax.while_loop(
            lambda j: jnp.logical_and(j < nt, te_ref[jnp.minimum(j, MOE_TILES - 1)] == te_ref[i]),
            lambda j: j + 1, i + 1)

        @pl.when(nxt < nt)
        def _():
            for c in fetch(te_ref[jnp.minimum(nxt, MOE_TILES - 1)], 1 - cur):
                c.start()

        for c in fetch(0, cur):
            c.wait()
        w16_ref[...] = wbuf_ref[cur].astype(BF16)
        group_ref[0] = group_ref[0] + 1


def _moe_up_kernel(te_ref, nt_ref, xs_ref, w_ref, b_ref, act_ref, wbuf_ref, wsem, w16_ref, group_ref):
    i = pl.program_id(0)
    _expert_weights(i, nt_ref[0], te_ref, w_ref, wbuf_ref, wsem, w16_ref, group_ref)

    @pl.when(i < nt_ref[0])
    def _():
        x16 = xs_ref[...].astype(BF16)
        for h in range(2):
            gc = slice(h * HALF, (h + 1) * HALF)
            uc = slice(D + h * HALF, D + (h + 1) * HALF)
            g = jnp.dot(x16, w16_ref[:, gc], preferred_element_type=F32) + b_ref[0, :, gc]
            u = jnp.dot(x16, w16_ref[:, uc], preferred_element_type=F32) + b_ref[0, :, uc]
            gt = jnp.minimum(g, SWIGLU_LIMIT)
            up = jnp.clip(u, -SWIGLU_LIMIT, SWIGLU_LIMIT)
            act_ref[:, gc] = ((up + 1.0) * gt * jax.nn.sigmoid(SWIGLU_ALPHA * gt)).astype(BF16)

    @pl.when(i >= nt_ref[0])
    def _():
        act_ref[...] = jnp.zeros_like(act_ref)


def _tile_clamped(i, te, nt):
    return (jnp.minimum(i, jnp.maximum(nt[0] - 1, 0)), 0)


def _weight_scratch(n_out):
    return [pltpu.VMEM((2, D, n_out), F32), pltpu.SemaphoreType.DMA((2,)), pltpu.VMEM((D, n_out), BF16),
            pltpu.SMEM((1,), I32)]


def _moe_up(te, n_tiles, xs, w_gu, b_gu):
    return pl.pallas_call(
        _moe_up_kernel,
        out_shape=jax.ShapeDtypeStruct((R_PAD, D), BF16),
        grid_spec=pltpu.PrefetchScalarGridSpec(
            num_scalar_prefetch=2, grid=(MOE_TILES,),
            in_specs=[pl.BlockSpec((TMX, D), _tile_clamped), pl.BlockSpec(memory_space=pl.ANY),
                      pl.BlockSpec((1, 1, 2 * D), lambda i, te, nt: (te[i], 0, 0))],
            out_specs=pl.BlockSpec((TMX, D), lambda i, te, nt: (i, 0)),
            scratch_shapes=_weight_scratch(2 * D)),
        compiler_params=_params(("arbitrary",), 56),
        name="moe_up",
    )(te, n_tiles, xs, w_gu, b_gu)


def _moe_down_kernel(te_ref, nt_ref, act_ref, w_ref, b_ref, y_ref, wbuf_ref, wsem, w16_ref, group_ref):
    i = pl.program_id(0)
    _expert_weights(i, nt_ref[0], te_ref, w_ref, wbuf_ref, wsem, w16_ref, group_ref)

    @pl.when(i < nt_ref[0])
    def _():
        y_ref[...] = jnp.dot(act_ref[...], w16_ref[...], preferred_element_type=F32) + b_ref[0]

    @pl.when(i >= nt_ref[0])
    def _():
        y_ref[...] = jnp.zeros_like(y_ref)


def _moe_down(te, n_tiles, act, w_dn, b_dn):
    return pl.pallas_call(
        _moe_down_kernel,
        out_shape=jax.ShapeDtypeStruct((R_PAD, D), F32),
        grid_spec=pltpu.PrefetchScalarGridSpec(
            num_scalar_prefetch=2, grid=(MOE_TILES,),
            in_specs=[pl.BlockSpec((TMX, D), _tile_clamped), pl.BlockSpec(memory_space=pl.ANY),
                      pl.BlockSpec((1, 1, D), lambda i, te, nt: (te[i], 0, 0))],
            out_specs=pl.BlockSpec((TMX, D), lambda i, te, nt: (i, 0)),
            scratch_shapes=_weight_scratch(D)),
        compiler_params=_params(("arbitrary",), 40),
        name="moe_down",
    )(te, n_tiles, act, w_dn, b_dn)


def _combine_kernel(dest_ref, x_ref, tg_ref, mod_ref, ys_ref, o_ref, buf_ref, sem):
    i = pl.program_id(0)
    slot = i % 2

    def gather(tile, b):
        base = tile * (TM * TOP_K)

        def body(r, carry):
            for k in range(TOP_K):
                _row_copy(ys_ref, dest_ref[base + r * TOP_K + k], buf_ref.at[b, k], r, sem.at[b]).start()
            return carry

        lax.fori_loop(0, TM, body, 0, unroll=4)

    @pl.when(i == 0)
    def _():
        gather(0, 0)

    @pl.when(i + 1 < NT)
    def _():
        gather(i + 1, 1 - slot)

    _wait_tiles(TOP_K, ys_ref.at[pl.ds(0, TM)], buf_ref.at[slot, 0], sem.at[slot])
    tg = tg_ref[...]
    f = tg[:, 0:1] * buf_ref[slot, 0]
    for k in range(1, TOP_K):
        f = f + tg[:, k:k + 1] * buf_ref[slot, k]
    o_ref[...] = x_ref[...] + mod_ref[0][:, 5 * D:6 * D] * f


def _combine(dest_flat, x, tg, mod, ys):
    row = lambda i, d: (i, 0)
    return pl.pallas_call(
        _combine_kernel,
        out_shape=jax.ShapeDtypeStruct((T, D), F32),
        grid_spec=pltpu.PrefetchScalarGridSpec(
            num_scalar_prefetch=1, grid=(NT,),
            in_specs=[pl.BlockSpec((TM, D), row), pl.BlockSpec((TM, LANES), row),
                      pl.BlockSpec((1, 1, N_MOD * D), lambda i, d: (_group(i), 0, 0)),
                      pl.BlockSpec(memory_space=pl.ANY)],
            out_specs=pl.BlockSpec((TM, D), row),
            scratch_shapes=[pltpu.VMEM((2, TOP_K, TM, D), F32), pltpu.SemaphoreType.DMA((2,))]),
        compiler_params=_params(("arbitrary",), 40),
        name="moe_combine",
    )(dest_flat, x, tg, mod, ys)


def _moe(layer, y, xt, ti, tg, rk, cnt, mod, w_gu, b_gu, w_dn, b_dn):
    dest, te, nt = _route(cnt, ti, rk)
    dest_flat = dest[:, :TOP_K].reshape(-1)
    te = te[:MOE_TILES, 0] + layer * N_EXPERTS
    n_tiles = nt[0, :1]
    xs = _dispatch(dest_flat, nt[1, :N_EXPERTS], xt)
    n_all = w_gu.shape[0] * N_EXPERTS
    act = _moe_up(te, n_tiles, xs, w_gu.reshape(n_all, D, 2 * D), b_gu.reshape(n_all, 1, 2 * D))
    ys = _moe_down(te, n_tiles, act, w_dn.reshape(n_all, D, D), b_dn.reshape(n_all, 1, D))
    return _combine(dest_flat, y, tg, mod, ys)


def _ones_blockdiag():
    idx = np.arange(HALF) // HEAD
    return jnp.asarray((idx[:, None] == idx[None, :]).astype(np.float32), dtype=BF16)


def _rope_tables():
    pos = jnp.arange(L_SAMPLE)
    rowp = (pos // 64).astype(F32)
    colp = (pos % 64).astype(F32)
    nf = HEAD // 4
    inv = jnp.power(10000.0, -jnp.arange(nf, dtype=F32) / nf)
    ar = rowp[:, None] * inv[None, :]
    ac = colp[:, None] * inv[None, :]
    cos64 = jnp.concatenate([jnp.cos(ar), jnp.cos(ar), jnp.cos(ac), jnp.cos(ac)], axis=1)
    sin64 = jnp.concatenate([-jnp.sin(ar), jnp.sin(ar), -jnp.sin(ac), jnp.sin(ac)], axis=1)
    cos = jnp.tile(cos64, (1, HALF // HEAD))
    sin = jnp.tile(sin64, (1, HALF // HEAD))
    ident = jnp.ones((TM, HALF), F32)
    return (jnp.concatenate([ident, cos], axis=0), jnp.concatenate([jnp.zeros((TM, HALF), F32), sin], axis=0))


def _bd_pairs(s):
    lead = s.shape[:-3]
    s = s.reshape(lead + (4, 2, HEAD, HEAD))
    z = jnp.zeros_like(s[..., 0, :, :])
    top = jnp.concatenate([s[..., 0, :, :], z], axis=-1)
    bot = jnp.concatenate([z, s[..., 1, :, :]], axis=-1)
    return jnp.concatenate([top, bot], axis=-2)


def _bd_unpairs(s):
    a = s[..., 0:HEAD, 0:HEAD]
    b = s[..., HEAD:, HEAD:]
    out = jnp.stack([a, b], axis=-3)
    return out.reshape(s.shape[:-3] + (8, HEAD, HEAD))


def kernel(x_prompt, x_sample, state_rwkv, cache_k_diff, cache_v_diff, state_retention, c, c_ctx, norm_g, ada_w, ada_b, e_w_in, e_w_out, sgu_ln_g, sgu_w, sgu_b, rw_mu, rw_w0, rw_w_up, rw_a0, rw_a_up, rw_g_up, rw_k_k, rw_k_a, rw_r_k, rw_gn_g, rw_gn_b, o_w_in, o_w_out, da_qk_g, da_lam, da_subln_g, ret_gn_g, router_w, router_b, ex_w_gu, ex_b_gu, ex_w_dn, ex_b_dn):
    x = jnp.concatenate([x_prompt.reshape(T_PROMPT, D), x_sample.reshape(T_SAMPLE, D)], axis=0)
    cvec8 = jnp.concatenate([c_ctx[None, :], c, jnp.zeros((3, D), F32)], axis=0)
    mods = _adaln(cvec8, ada_w, ada_b)
    mod0 = mods[0].reshape(8, 1, N_MOD * D)
    mod1 = mods[1].reshape(8, 1, N_MOD * D)
    ones_bd = _ones_blockdiag()
    rw_pad = jnp.pad(router_w, ((0, 0), (0, 0), (0, LANES - N_EXPERTS)))
    rb_pad = jnp.pad(router_b, ((0, 0), (0, LANES - N_EXPERTS))).reshape(2, 1, LANES)
    row = lambda i: (i, 0)
    half = pl.BlockSpec((TM, HALF), row)

    za, zb = _in_proj(x, norm_g[0, 0], mod0, e_w_in[0].astype(BF16), (2 * HALF, B_COLS))
    bs_full = jnp.repeat(sgu_b[0].T, HEAD, axis=1)
    a_out = _sgu(za, sgu_ln_g[0], sgu_w[0].astype(BF16), bs_full)
    zpad = jnp.zeros((2, HEAD, HALF), F32)
    wup_pad = jnp.concatenate([rw_w_up[0], zpad], axis=1).astype(BF16)
    aup_pad = jnp.concatenate([zpad, rw_a_up[0]], axis=1).astype(BF16)
    r, v, kkn, bonus, g, lw, kt, b = _rwkv_prep(zb, rw_mu[0], rw_k_k[0], rw_k_a[0], rw_r_k[0], rw_w0[0], rw_a0[0],
                                                wup_pad, aup_pad, rw_g_up[0].astype(BF16), ones_bd)
    s0_sample = _bd_pairs(jnp.moveaxis(state_rwkv[:, 0], 1, 0))
    s0_rw = jnp.concatenate([jnp.zeros((2, N_PROMPT, 4, LANES, LANES), F32), s0_sample], axis=1)
    yf_rw, yb_rw, sfin_rw = _rwkv_scan(r, v, kkn, lw, kt, b, s0_rw)
    new_rwkv = jnp.moveaxis(_bd_unpairs(sfin_rw[:, :N_PROMPT]), 0, 1)[:, None]
    y0, xp0, ti0, tg0, rk0, cnt0 = _out_proj(
        True,
        [a_out, yf_rw, yb_rw, bonus, g, rw_gn_g[0].reshape(1, HALF), rw_gn_b[0].reshape(1, HALF), ones_bd],
        [half, half, half, half, half, _full((1, HALF)), _full((1, HALF)), _full((HALF, HALF))],
        x, mod0, norm_g[0, 1], e_w_out[0].astype(BF16), rw_pad[0], rb_pad[0])
    x1 = _moe(0, y0, xp0, ti0, tg0, rk0, cnt0, mod0, ex_w_gu, ex_b_gu, ex_w_dn, ex_b_dn)

    zc, zr = _in_proj(x1, norm_g[1, 0], mod1, o_w_in[0].astype(BF16), (3 * HALF, 3 * HALF))
    cos_tab, sin_tab = _rope_tables()
    qkg = jnp.tile(da_qk_g[0], (1, HALF // HEAD))
    cq, ck, ck_raw, rqk = _odd_prep(zc, zr, cos_tab, sin_tab, qkg, ones_bd)
    lambda_init = 0.8 - 0.6 * math.exp(-0.3 * 1)
    lv = da_lam[0]
    lam = jnp.exp(jnp.sum(lv[0] * lv[1])) - jnp.exp(jnp.sum(lv[2] * lv[3])) + lambda_init
    c_out = _attn(cq, ck, zc, 0, N_PROMPT, L_PROMPT, lam, da_subln_g[0], 1.0 - lambda_init,
                  jnp.zeros((T, HALF), F32))
    ctx_k = cache_k_diff[:, 0].reshape(N_SAMPLE, PAST, HALF)
    ctx_v = cache_v_diff[:, 0].reshape(N_SAMPLE, PAST, HALF)
    c_out = _attn(cq, ck, zc, T_PROMPT, N_SAMPLE, L_SAMPLE, lam, da_subln_g[0], 1.0 - lambda_init,
                  c_out, ctx_k, ctx_v)
    sr = jnp.moveaxis(state_retention[:, 0], 1, 0)
    zr0 = jnp.zeros_like(sr)
    s0_sample = jnp.stack([jnp.concatenate([sr[:, :, 0], zr0[:, :, 0]], axis=-2),
                           jnp.concatenate([zr0[:, :, 1], sr[:, :, 1]], axis=-2),
                           jnp.concatenate([sr[:, :, 2], zr0[:, :, 2]], axis=-2),
                           jnp.concatenate([zr0[:, :, 3], sr[:, :, 3]], axis=-2)], axis=2)
    s0_ret = jnp.concatenate([jnp.zeros((2, N_PROMPT, 4, LANES, LANES), F32), s0_sample], axis=1)
    of_ret, ob_ret, rfin = _retention(rqk, zr, s0_ret)
    rfin_p = rfin[:, :N_PROMPT]
    new_ret = jnp.stack([rfin_p[:, :, 0, 0:HEAD], rfin_p[:, :, 1, HEAD:], rfin_p[:, :, 2, 0:HEAD],
                         rfin_p[:, :, 3, HEAD:]], axis=2)
    new_ret = jnp.moveaxis(new_ret, 0, 1)[:, None]
    y1, xp1, ti1, tg1, rk1, cnt1 = _out_proj(
        False,
        [c_out, of_ret, ob_ret, zr, ret_gn_g[0].reshape(1, HALF)],
        [half, half, half, pl.BlockSpec((TM, HALF), lambda i: (i, 2)), _full((1, HALF))],
        x1, mod1, norm_g[1, 1], o_w_out[0].astype(BF16), rw_pad[1], rb_pad[1])
    y_fin = _moe(1, y1, xp1, ti1, tg1, rk1, cnt1, mod1, ex_w_gu, ex_b_gu, ex_w_dn, ex_b_dn)

    new_k = ck_raw[:T_PROMPT].reshape(N_PROMPT, 1, L_PROMPT, 4, LANES)
    new_v = zc[:T_PROMPT, 2 * HALF:3 * HALF].reshape(N_PROMPT, 1, L_PROMPT, 4, LANES)
    return (y_fin[:T_PROMPT].reshape(N_PROMPT, L_PROMPT, D), y_fin[T_PROMPT:].reshape(N_SAMPLE, L_SAMPLE, D),
            new_rwkv, new_k, new_v, new_ret)
```

```python
import functools
import math

import numpy as np
import jax
import jax.numpy as jnp
from jax import lax
from jax.experimental import pallas as pl
from jax.experimental.pallas import tpu as pltpu

F32 = jnp.float32
BF16 = jnp.bfloat16
I32 = jnp.int32

D = 1024
N_PROMPT, L_PROMPT = 16, 256
N_SAMPLE, L_SAMPLE = 4, 1024
N_SEQ = N_PROMPT + N_SAMPLE
PAST = 256
T_PROMPT = N_PROMPT * L_PROMPT
T_SAMPLE = N_SAMPLE * L_SAMPLE
T = T_PROMPT + T_SAMPLE
TM = 256
NT = T // TM
PROMPT_TILES = T_PROMPT // TM
TILES_PER_SAMPLE = L_SAMPLE // TM
N_MOD = 6
HALF = 512
B_COLS = 1792
HEAD = 64
W_DECAY_SCALE = math.exp(-0.5)
RWKV_GN_EPS = 64e-5
RW_CHUNK = 64
RET_CHUNK = 128
RET_EXP = ((5.0, 7.0, 9.0, 11.0), (6.0, 8.0, 10.0, 12.0))
N_EXPERTS = 32
TOP_K = 4
SWIGLU_LIMIT = 7.0
SWIGLU_ALPHA = 1.702
N_ASSIGN = T * TOP_K
TMX = 512
MOE_TILES = N_ASSIGN // TMX + N_EXPERTS
R_PAD = MOE_TILES * TMX
LANES = 128

NN = (((1,), (0,)), ((), ()))
NT_DIMS = (((1,), (1,)), ((), ()))
TN = (((0,), (0,)), ((), ()))


def _group(i):
    return jnp.where(i < PROMPT_TILES, 0, 1 + (i - PROMPT_TILES) // TILES_PER_SAMPLE)


def _mm(a, b, dims=NN, passes=1):
    dg = functools.partial(lax.dot_general, dimension_numbers=dims, preferred_element_type=F32)
    if passes == 1:
        return dg(a.astype(BF16), b.astype(BF16))
    a = a.astype(F32)
    b = b.astype(F32)
    ah = a.astype(BF16)
    al = (a - ah.astype(F32)).astype(BF16)
    bh = b.astype(BF16)
    if passes == 2:
        assert dims[0][0] == (1,)
        m = a.shape[0]
        both = dg(jnp.concatenate([ah, al], axis=0), bh)
        return both[0:m] + both[m:2 * m]
    bl = (b - bh.astype(F32)).astype(BF16)
    if dims[0][0] == (1,):
        m = a.shape[0]
        both = dg(jnp.concatenate([ah, al], axis=0), bh)
        return both[0:m] + (dg(ah, bl) + both[m:2 * m])
    return dg(ah, bh) + (dg(ah, bl) + dg(al, bh))


def _group_sum(x, ones_bd):
    xh = x.astype(BF16)
    xl = (x - xh.astype(F32)).astype(BF16)
    return (jnp.dot(xh, ones_bd, preferred_element_type=F32)
            + jnp.dot(xl, ones_bd, preferred_element_type=F32))


def _full(shape):
    nd = len(shape)
    return pl.BlockSpec(shape, lambda *_: (0,) * nd)


def _params(sem, vmem_mb=None):
    kw = {}
    if vmem_mb is not None:
        kw["vmem_limit_bytes"] = vmem_mb * 1024 * 1024
    return pltpu.CompilerParams(dimension_semantics=sem, **kw)


def _seq_tables(chunk):
    blk_f, blk_b, first, last, seq = [], [], [], [], []
    row = 0
    for s in range(N_SEQ):
        n = (L_PROMPT if s < N_PROMPT else L_SAMPLE) // chunk
        base = row // chunk
        for j in range(n):
            blk_f.append(base + j)
            blk_b.append(base + n - 1 - j)
            first.append(int(j == 0))
            last.append(int(j == n - 1))
            seq.append(s)
        row += n * chunk
    return tuple(np.asarray(a, np.int32) for a in (blk_f, blk_b, first, last, seq))


def _adaln_kernel(c_ref, w_ref, b_ref, o_ref):
    c = c_ref[...]
    s = c * jax.nn.sigmoid(c)
    o_ref[0] = _mm(s, w_ref[0], NN, 3) + b_ref[0]


def _adaln(cvec8, ada_w, ada_b):
    depth, _, n = ada_w.shape
    bn = 1536
    return pl.pallas_call(
        _adaln_kernel,
        out_shape=jax.ShapeDtypeStruct((depth, 8, n), F32),
        grid=(depth, n // bn),
        in_specs=[pl.BlockSpec((8, D), lambda l, j: (0, 0)),
                  pl.BlockSpec((1, D, bn), lambda l, j: (l, 0, j)),
                  pl.BlockSpec((1, 1, bn), lambda l, j: (l, 0, j))],
        out_specs=pl.BlockSpec((1, 8, bn), lambda l, j: (l, 0, j)),
        compiler_params=_params(("arbitrary", "arbitrary"), 40),
        name="adaln",
    )(cvec8, ada_w, ada_b.reshape(depth, 1, n))


def _in_kernel(x_ref, g_ref, mod_ref, w_ref, *outs, splits):
    x = x_ref[...]
    mod = mod_ref[0]
    y = x * lax.rsqrt(jnp.mean(x * x, axis=-1, keepdims=True) + 1e-6) * g_ref[...]
    h = (y * (1.0 + mod[:, D:2 * D]) + mod[:, 0:D]).astype(BF16)
    off = 0
    for o_ref, n in zip(outs, splits):
        o_ref[...] = jnp.dot(h, w_ref[:, off:off + n], preferred_element_type=F32)
        off += n


def _in_proj(x, g, mod, w_bf16, splits):
    n = w_bf16.shape[1]
    row = lambda i: (i, 0)
    return pl.pallas_call(
        functools.partial(_in_kernel, splits=splits),
        out_shape=[jax.ShapeDtypeStruct((T, s), F32) for s in splits],
        grid=(NT,),
        in_specs=[pl.BlockSpec((TM, D), row), _full((1, D)),
                  pl.BlockSpec((1, 1, N_MOD * D), lambda i: (_group(i), 0, 0)), _full((D, n))],
        out_specs=[pl.BlockSpec((TM, s), row) for s in splits],
        compiler_params=_params(("arbitrary",), 48),
        name="in_proj",
    )(x, g.reshape(1, D), mod, w_bf16)


def _gelu(x):
    return 0.5 * x * (1.0 + lax.erf(x * (1.0 / math.sqrt(2.0))))


def _sgu_kernel(za_ref, lng_ref, ws_ref, bs_ref, o_ref):
    u = _gelu(za_ref[:, 0:HALF])
    va = _gelu(za_ref[:, HALF:2 * HALF])
    mu = jnp.mean(va, axis=-1, keepdims=True)
    dv = va - mu
    var = jnp.mean(dv * dv, axis=-1, keepdims=True)
    vn = dv * lax.rsqrt(var + 1e-5) * lng_ref[...]
    lane = lax.broadcasted_iota(I32, (LANES, LANES), 1)
    first = lane < HEAD
    for c in range(TM // LANES):
        rows = slice(c * LANES, (c + 1) * LANES)
        for p in range(HALF // LANES):
            cols = slice(p * LANES, (p + 1) * LANES)
            vp = vn[rows, cols]
            s = (jnp.dot(ws_ref[2 * p], jnp.where(first, vp, 0.0).astype(BF16), preferred_element_type=F32)
                 + jnp.dot(ws_ref[2 * p + 1], jnp.where(first, 0.0, vp).astype(BF16), preferred_element_type=F32))
            o_ref[rows, cols] = u[rows, cols] * (s + bs_ref[:, cols])


def _sgu(za, ln_g, w_s_bf16, bs_full):
    return pl.pallas_call(
        _sgu_kernel,
        out_shape=jax.ShapeDtypeStruct((T, HALF), F32),
        grid=(NT,),
        in_specs=[pl.BlockSpec((TM, 2 * HALF), lambda i: (i, 0)), _full((1, HALF)),
                  _full((8, LANES, LANES)), _full((LANES, HALF))],
        out_specs=pl.BlockSpec((TM, HALF), lambda i: (i, 0)),
        compiler_params=_params(("arbitrary",)),
        name="sgu",
    )(za, ln_g.reshape(1, HALF), w_s_bf16, bs_full)


def _rwkv_prep_kernel(zb_ref, zp_ref, zn_ref, mu_ref, kk_ref, ka_ref, rk_ref, w0_ref, a0_ref,
                      wup_ref, aup_ref, gup_ref, ones_ref,
                      r_ref, v_ref, kkn_ref, bonus_ref, g_ref, lw_ref, kt_ref, b_ref):
    i = pl.program_id(0)
    in_sample = i >= PROMPT_TILES
    pos = (i - PROMPT_TILES) % TILES_PER_SAMPLE
    is_first = jnp.logical_or(jnp.logical_not(in_sample), pos == 0)
    is_last = jnp.logical_or(jnp.logical_not(in_sample), pos == TILES_PER_SAMPLE - 1)
    zb = zb_ref[...]
    prev_row = jnp.where(is_first, 0.0, zp_ref[7:8, :])
    next_row = jnp.where(is_last, 0.0, zn_ref[0:1, :])
    rowid = lax.broadcasted_iota(I32, (TM, 1), 0)
    zp = jnp.where(rowid == 0, prev_row, pltpu.roll(zb, 1, 0))
    zn = jnp.where(rowid == TM - 1, next_row, pltpu.roll(zb, TM - 1, 0))
    zs = zb + mu_ref[0:1, :] * (zp - zb) + mu_ref[1:2, :] * (zn - zb)
    r = zs[:, 0:HALF]
    k = zs[:, HALF:2 * HALF]
    v = zs[:, 2 * HALF:3 * HALF]
    wa = zs[:, 3 * HALF:3 * HALF + LANES]
    gd = zs[:, 3 * HALF + LANES:B_COLS]
    ones_bd = ones_ref[...]
    r_ref[...] = r
    v_ref[...] = v
    g_ref[...] = jnp.dot(jax.nn.sigmoid(gd).astype(BF16), gup_ref[...], preferred_element_type=F32)
    kk = k * kk_ref[...]
    kkn = kk / jnp.maximum(jnp.sqrt(_group_sum(kk * kk, ones_bd)), 1e-6)
    kkn_ref[...] = kkn
    bonus_ref[...] = _group_sum(r * k * rk_ref[...], ones_bd) * v
    tw = jnp.tanh(wa).astype(BF16)
    wa16 = wa.astype(BF16)
    for dd in range(2):
        lw_ref[dd] = -W_DECAY_SCALE * jax.nn.sigmoid(
            w0_ref[dd:dd + 1, :] + jnp.dot(tw, wup_ref[dd], preferred_element_type=F32))
        a = jax.nn.sigmoid(a0_ref[dd:dd + 1, :] + jnp.dot(wa16, aup_ref[dd], preferred_element_type=F32))
        kt_ref[dd] = k * (1.0 + (a - 1.0) * ka_ref[...])
        b_ref[dd] = a * kkn


def _rwkv_prep(zb, mu, k_k, k_a, r_k, w0, a0, wup_pad, aup_pad, g_up, ones_bd):
    row = lambda i: (i, 0)
    halo = TM // 8
    one = jax.ShapeDtypeStruct((T, HALF), F32)
    two = jax.ShapeDtypeStruct((2, T, HALF), F32)
    o1 = pl.BlockSpec((TM, HALF), row)
    o2 = pl.BlockSpec((2, TM, HALF), lambda i: (0, i, 0))
    return pl.pallas_call(
        _rwkv_prep_kernel,
        out_shape=[one, one, one, one, one, two, two, two],
        grid=(NT,),
        in_specs=[pl.BlockSpec((TM, B_COLS), row),
                  pl.BlockSpec((8, B_COLS), lambda i: (jnp.maximum(i * halo - 1, 0), 0)),
                  pl.BlockSpec((8, B_COLS), lambda i: (jnp.minimum((i + 1) * halo, T // 8 - 1), 0)),
                  _full((2, B_COLS)), _full((1, HALF)), _full((1, HALF)), _full((1, HALF)),
                  _full((2, HALF)), _full((2, HALF)),
                  _full((2, LANES, HALF)), _full((2, LANES, HALF)), _full((LANES, HALF)),
                  _full((HALF, HALF))],
        out_specs=[o1, o1, o1, o1, o1, o2, o2, o2],
        compiler_params=_params(("arbitrary",), 48),
        name="rwkv_prep",
    )(zb, zb, zb, mu, k_k.reshape(1, HALF), k_a.reshape(1, HALF), r_k.reshape(1, HALF), w0, a0,
      wup_pad, aup_pad, g_up, ones_bd)


def _rwkv_chunks(dirs):
    C = RW_CHUNK
    ti = lax.broadcasted_iota(I32, (C, C), 0)
    tj = lax.broadcasted_iota(I32, (C, C), 1)
    bi = lax.broadcasted_iota(I32, (LANES, LANES), 0)
    bj = lax.broadcasted_iota(I32, (LANES, LANES), 1)
    same = (bi >> 6) == (bj >> 6)
    pi = bi & (C - 1)
    pj = bj & (C - 1)
    eye = (bi == bj).astype(F32)
    h0 = lax.broadcasted_iota(I32, (C, LANES), 1) < HEAD

    def stack(x):
        return jnp.concatenate([jnp.where(h0, x, 0.0), jnp.where(h0, 0.0, x)], axis=0)

    def fold(x):
        return x[0:C] + x[C:2 * C]

    chains = []
    for rev, r, v, kk, lw, kt, b, s_ref, y_ref in dirs:
        tri = jnp.where((tj >= ti) if rev else (tj <= ti), 1.0, 0.0).astype(F32)
        p1 = lw.astype(BF16)
        r1 = lw - p1.astype(F32)
        p2 = r1.astype(BF16)
        p3 = (r1 - p2.astype(F32)).astype(BF16)
        cs3 = jnp.dot(tri.astype(BF16), jnp.concatenate([p1, p2, p3], axis=1), preferred_element_type=F32)
        cs = cs3[:, 0:HALF] + (cs3[:, HALF:2 * HALF] + cs3[:, 2 * HALF:3 * HALF])
        ctot = cs[0:1, :] if rev else cs[C - 1:C, :]
        e_neg = jnp.exp(-cs)
        e_tail = jnp.exp(ctot - cs)
        q1 = kk * jnp.exp(cs - lw)
        k1 = kt * e_neg
        b1 = b * e_neg
        r1 = r * jnp.exp(cs)
        k2 = kt * e_tail
        b2 = b * e_tail
        e_tot = jnp.exp(ctot)
        strict = jnp.logical_and(same, (pj > pi) if rev else (pj < pi))
        incl = jnp.logical_and(same, (pj >= pi) if rev else (pj <= pi))
        for p in range(HALF // LANES):
            cols = slice(p * LANES, (p + 1) * LANES)
            chains.append(dict(p=p, cols=cols, strict=strict, incl=incl, s_ref=s_ref, y_ref=y_ref,
                               q1=q1[:, cols], k1=k1[:, cols], b1=b1[:, cols], r1=r1[:, cols],
                               k2=k2[:, cols], b2=b2[:, cols], v=v[:, cols], e_tot=e_tot[:, cols]))

    for ch in chains:
        lhs = jnp.concatenate([stack(ch["q1"]), stack(ch["r1"])], axis=0)
        rhs = jnp.concatenate([ch["k1"], ch["k1"], ch["b1"], ch["b1"]], axis=0)
        gm = _mm(lhs, rhs, NT_DIMS, 2)
        ch["mk"] = jnp.where(ch["strict"], gm[0:2 * C, 0:2 * C], 0.0)
        ch["mb"] = jnp.where(ch["strict"], gm[0:2 * C, 2 * C:4 * C], 0.0)
        ch["nk"] = jnp.where(ch["incl"], gm[2 * C:4 * C, 0:2 * C], 0.0)
        ch["nb"] = jnp.where(ch["incl"], gm[2 * C:4 * C, 2 * C:4 * C], 0.0)
        ch["tinv"] = eye - jnp.where((pi >> 1) == (pj >> 1), ch["mb"], 0.0)
    size = 2
    while size < C:
        sh = size.bit_length() - 1
        blk = jnp.logical_and((pi >> (sh + 1)) == (pj >> (sh + 1)), (pi >> sh) != (pj >> sh))
        for ch in chains:
            ch["tn"] = _mm(ch["tinv"], jnp.where(blk, ch["mb"], 0.0), NN, 2)
        for ch in chains:
            ch["tinv"] = ch["tinv"] - _mm(ch["tn"], ch["tinv"], NN, 2)
        size *= 2
    for ch in chains:
        vst = stack(ch["v"])
        ch["mkv"] = fold(_mm(ch["mk"], vst, NN, 2))
        ch["nkv"] = fold(_mm(ch["nk"], vst, NN, 1))
        ch["s"] = ch["s_ref"][ch["p"]]
        ch["qr"] = _mm(jnp.concatenate([ch["q1"], ch["r1"]], axis=0), ch["s"], NT_DIMS, 2)
    for ch in chains:
        ch["u"] = fold(_mm(ch["tinv"], stack(ch["mkv"] + ch["qr"][0:C]), NN, 2))
    for ch in chains:
        ch["y_ref"][:, ch["cols"]] = ch["qr"][C:2 * C] + ch["nkv"] - fold(_mm(ch["nb"], stack(ch["u"]), NN, 1))
        upd = _mm(jnp.concatenate([ch["v"], ch["u"]], axis=0),
                  jnp.concatenate([ch["k2"], -ch["b2"]], axis=0), TN, 3)
        ch["s_ref"][ch["p"]] = ch["s"] * ch["e_tot"] + jnp.where(same, upd, 0.0)


def _rwkv_scan_kernel(bf_ref, bb_ref, first_ref, last_ref, seq_ref,
                      rf_ref, vf_ref, kkf_ref, lwf_ref, ktf_ref, bfw_ref,
                      rb_ref, vb_ref, kkb_ref, lwb_ref, ktb_ref, bbw_ref, s0_ref,
                      yf_ref, yb_ref, sfin_ref, s_ref):
    step = pl.program_id(0)

    @pl.when(first_ref[step] == 1)
    def _():
        s_ref[...] = s0_ref[:, 0]

    _rwkv_chunks([
        (False, rf_ref[...], vf_ref[...], kkf_ref[...], lwf_ref[0], ktf_ref[0], bfw_ref[0], s_ref.at[0], yf_ref),
        (True, rb_ref[...], vb_ref[...], kkb_ref[...], lwb_ref[0], ktb_ref[0], bbw_ref[0], s_ref.at[1], yb_ref)])

    @pl.when(last_ref[step] == 1)
    def _():
        sfin_ref[:, 0] = s_ref[...]


def _rwkv_scan(r, v, kk, lw, kt, b, s0_bd):
    C = RW_CHUNK
    tabs = _seq_tables(C)
    fwd = lambda i, bf, bb, fi, la, sq: (bf[i], 0)
    bwd = lambda i, bf, bb, fi, la, sq: (bb[i], 0)
    fwd3 = lambda i, bf, bb, fi, la, sq: (0, bf[i], 0)
    bwd3 = lambda i, bf, bb, fi, la, sq: (1, bb[i], 0)
    st = pl.BlockSpec((2, 1, 4, LANES, LANES), lambda i, bf, bb, fi, la, sq: (0, sq[i], 0, 0, 0))
    one_f, one_b = pl.BlockSpec((C, HALF), fwd), pl.BlockSpec((C, HALF), bwd)
    two_f, two_b = pl.BlockSpec((1, C, HALF), fwd3), pl.BlockSpec((1, C, HALF), bwd3)
    return pl.pallas_call(
        _rwkv_scan_kernel,
        out_shape=[jax.ShapeDtypeStruct((T, HALF), F32), jax.ShapeDtypeStruct((T, HALF), F32),
                   jax.ShapeDtypeStruct((2, N_SEQ, 4, LANES, LANES), F32)],
        grid_spec=pltpu.PrefetchScalarGridSpec(
            num_scalar_prefetch=5, grid=(len(tabs[0]),),
            in_specs=[one_f, one_f, one_f, two_f, two_f, two_f,
                      one_b, one_b, one_b, two_b, two_b, two_b, st],
            out_specs=[one_f, one_b, st],
            scratch_shapes=[pltpu.VMEM((2, 4, LANES, LANES), F32)]),
        compiler_params=_params(("arbitrary",)),
        name="rwkv_scan",
    )(*tabs, r, v, kk, lw, kt, b, r, v, kk, lw, kt, b, s0_bd)


def _rope(x, cos, sin_signed, first16):
    w = x.shape[1]
    partner = jnp.where(first16, pltpu.roll(x, w - 16, 1), pltpu.roll(x, 16, 1))
    return x * cos + partner * sin_signed


def _odd_prep_kernel(zc_ref, zr_ref, cos_ref, sin_ref, qkg_ref, ones_ref, cq_ref, ck_ref, ckraw_ref, rqk_ref):
    ones_bd = ones_ref[...]
    cos = cos_ref[...]
    sin = sin_ref[...]
    lane = lax.broadcasted_iota(I32, (TM, HALF), 1)
    first16 = (lane & 31) < 16
    for idx, (o_ref, raw_ref) in enumerate(((cq_ref, None), (ck_ref, ckraw_ref))):
        x = zc_ref[:, idx * HALF:(idx + 1) * HALF]
        ms = _group_sum(x * x, ones_bd) * (1.0 / HEAD)
        xn = x * lax.rsqrt(ms + 1e-6) * qkg_ref[idx:idx + 1, :]
        if raw_ref is not None:
            raw_ref[...] = xn
        o_ref[...] = _rope(xn, cos, sin, first16)
    rqk = _rope(zr_ref[...], cos, sin, first16)
    rqk_ref[...] = jnp.where(lane < HALF // 2, rqk * (HEAD ** -0.5), rqk)


def _odd_prep(zc, zr, cos_tab, sin_tab, qkg_tiled, ones_bd):
    row = lambda i: (i, 0)
    tab = lambda i: (jnp.where(i < PROMPT_TILES, 0, 1 + (i - PROMPT_TILES) % TILES_PER_SAMPLE), 0)
    one = jax.ShapeDtypeStruct((T, HALF), F32)
    o1 = pl.BlockSpec((TM, HALF), row)
    return pl.pallas_call(
        _odd_prep_kernel,
        out_shape=[one, one, one, one], grid=(NT,),
        in_specs=[pl.BlockSpec((TM, 2 * HALF), row), pl.BlockSpec((TM, HALF), row),
                  pl.BlockSpec((TM, HALF), tab), pl.BlockSpec((TM, HALF), tab),
                  _full((2, HALF)), _full((HALF, HALF))],
        out_specs=[o1, o1, o1, o1],
        compiler_params=_params(("arbitrary",)),
        name="odd_prep",
    )(zc, zr, cos_tab, sin_tab, qkg_tiled, ones_bd)


def _attn_kernel(*refs, has_ctx, one_minus_li):
    if has_ctx:
        q_ref, k_ref, v_ref, kc_ref, vc_ref, lam_ref, sg_ref, _, o_ref = refs
    else:
        q_ref, k_ref, v_ref, lam_ref, sg_ref, _, o_ref = refs
    lam = lam_ref[...]
    lane = lax.broadcasted_iota(I32, (LANES, LANES), 1)
    m0 = lane < HEAD
    scale = HEAD ** -0.5
    for h in range(4):
        cols = slice(h * LANES, (h + 1) * LANES)
        qp = q_ref[:, cols]
        segs = [(k_ref[:, cols], v_ref[:, cols])]
        if has_ctx:
            segs.append((kc_ref[0, :, cols], vc_ref[0, :, cols]))
        outs = []
        for qm in (jnp.where(m0, qp, 0.0), jnp.where(m0, 0.0, qp)):
            qm16 = qm.astype(BF16)
            ss = [lax.dot_general(qm16, ks.astype(BF16), NT_DIMS, preferred_element_type=F32) * scale
                  for ks, _ in segs]
            mx = ss[0].max(axis=-1, keepdims=True)
            for s_ in ss[1:]:
                mx = jnp.maximum(mx, s_.max(axis=-1, keepdims=True))
            ps = [jnp.exp(s_ - mx) for s_ in ss]
            den = ps[0].sum(axis=-1, keepdims=True)
            for p_ in ps[1:]:
                den = den + p_.sum(axis=-1, keepdims=True)
            outs.append([p_ / den for p_ in ps])
        acc = None
        for si, (_, vs) in enumerate(segs):
            amap = outs[0][si] - lam * outs[1][si]
            t = jnp.dot(amap.astype(BF16), vs.astype(BF16), preferred_element_type=F32)
            acc = t if acc is None else acc + t
        nrm = acc * lax.rsqrt(jnp.mean(acc * acc, axis=-1, keepdims=True) + 1e-6) * sg_ref[...]
        o_ref[:, cols] = nrm * one_minus_li


def _attn(cq, ck, zc, row0, n_seq, seq_len, lam, subln_g, one_minus_li, prev, ctx_k=None, ctx_v=None):
    nq = seq_len // LANES
    qb0 = row0 // LANES
    sb0 = row0 // seq_len
    in_specs = [pl.BlockSpec((LANES, HALF), lambda s, q: (qb0 + s * nq + q, 0)),
                pl.BlockSpec((seq_len, HALF), lambda s, q: (sb0 + s, 0)),
                pl.BlockSpec((seq_len, HALF), lambda s, q: (sb0 + s, 2))]
    args = [cq, ck, zc]
    if ctx_k is not None:
        in_specs += [pl.BlockSpec((1, PAST, HALF), lambda s, q: (s, 0, 0))] * 2
        args += [ctx_k, ctx_v]
    in_specs += [_full((1, 1)), _full((1, LANES))]
    args += [lam.reshape(1, 1), subln_g.reshape(1, LANES)]
    in_specs.append(pl.BlockSpec(memory_space=pl.ANY))
    args.append(prev)
    aliases = {len(args) - 1: 0}
    return pl.pallas_call(
        functools.partial(_attn_kernel, has_ctx=ctx_k is not None, one_minus_li=one_minus_li),
        out_shape=jax.ShapeDtypeStruct((T, HALF), F32),
        grid=(n_seq, nq), in_specs=in_specs,
        out_specs=pl.BlockSpec((LANES, HALF), lambda s, q: (qb0 + s * nq + q, 0)),
        input_output_aliases=aliases,
        compiler_params=_params(("arbitrary", "arbitrary"), 48),
        name="diff_attn",
    )(*args)


_LOG_GAMMA = tuple(tuple(float(np.log1p(-np.exp2(-np.float32(e)), dtype=np.float32)) for e in es)
                   for es in RET_EXP)


def _ret_chunks(dirs):
    C = RET_CHUNK
    ii = lax.broadcasted_iota(I32, (C, C), 0)
    jj = lax.broadcasted_iota(I32, (C, C), 1)
    ri = lax.broadcasted_iota(I32, (C, 1), 0)
    lane = lax.broadcasted_iota(I32, (C, LANES), 1)
    chains = []
    for rev, qk_ref, v_ref, s_ref, o_ref in dirs:
        mask = (jj > ii) if rev else (jj <= ii)
        dist = jnp.where(mask, (jj - ii) if rev else (ii - jj), 0).astype(F32)
        kpow = (ri if rev else (C - 1 - ri)).astype(F32)
        qpow = ((C - ri) if rev else (ri + 1)).astype(F32)
        for h in range(4):
            lg = _LOG_GAMMA[1 if rev else 0][h]
            p = h // 2
            hm = (lane < HEAD) if h % 2 == 0 else (lane >= HEAD)
            qp = jnp.where(hm, qk_ref[:, p * LANES:(p + 1) * LANES], 0.0)
            kp = jnp.where(hm, qk_ref[:, HALF // 2 + p * LANES:HALF // 2 + (p + 1) * LANES], 0.0)
            chains.append(dict(
                h=h, lg=lg, s_ref=s_ref, o_ref=o_ref, q16=qp.astype(BF16), k16=kp.astype(BF16),
                qw16=(qp * jnp.exp(lg * qpow)).astype(BF16), kw16=(kp * jnp.exp(lg * kpow)).astype(BF16),
                v16=v_ref[:, h * LANES:(h + 1) * LANES].astype(BF16),
                decay=jnp.where(mask, jnp.exp(lg * dist), 0.0)))
    for ch in chains:
        ch["sc"] = (lax.dot_general(ch["q16"], ch["k16"], NT_DIMS, preferred_element_type=F32)
                    * ch["decay"]).astype(BF16)
        ch["s"] = ch["s_ref"][ch["h"]]
    for ch in chains:
        ch["o"] = (jnp.dot(ch["sc"], ch["v16"], preferred_element_type=F32)
                   + jnp.dot(ch["qw16"], ch["s"].astype(BF16), preferred_element_type=F32))
        ch["kv"] = lax.dot_general(ch["kw16"], ch["v16"], TN, preferred_element_type=F32)
    for ch in chains:
        h = ch["h"]
        ch["o_ref"][:, h * LANES:(h + 1) * LANES] = ch["o"]
        ch["s_ref"][h] = math.exp(ch["lg"] * C) * ch["s"] + ch["kv"]


def _ret_kernel(bf_ref, bb_ref, first_ref, last_ref, seq_ref,
                qkf_ref, vf_ref, qkb_ref, vb_ref, s0_ref, of_ref, ob_ref, sfin_ref, s_ref):
    step = pl.program_id(0)

    @pl.when(first_ref[step] == 1)
    def _():
        s_ref[...] = s0_ref[:, 0]

    _ret_chunks([(False, qkf_ref, vf_ref, s_ref.at[0], of_ref), (True, qkb_ref, vb_ref, s_ref.at[1], ob_ref)])

    @pl.when(last_ref[step] == 1)
    def _():
        sfin_ref[:, 0] = s_ref[...]


def _retention(rqk, zr, s0):
    C = RET_CHUNK
    tabs = _seq_tables(C)
    st = pl.BlockSpec((2, 1, 4, LANES, LANES), lambda i, bf, bb, fi, la, sq: (0, sq[i], 0, 0, 0))
    spec = lambda use_b, col: pl.BlockSpec(
        (C, HALF), lambda i, bf, bb, fi, la, sq: ((bb if use_b else bf)[i], col))
    return pl.pallas_call(
        _ret_kernel,
        out_shape=[jax.ShapeDtypeStruct((T, HALF), F32), jax.ShapeDtypeStruct((T, HALF), F32),
                   jax.ShapeDtypeStruct((2, N_SEQ, 4, LANES, LANES), F32)],
        grid_spec=pltpu.PrefetchScalarGridSpec(
            num_scalar_prefetch=5, grid=(len(tabs[0]),),
            in_specs=[spec(False, 0), spec(False, 1), spec(True, 0), spec(True, 1), st],
            out_specs=[spec(False, 0), spec(True, 0), st],
            scratch_shapes=[pltpu.VMEM((2, 4, LANES, LANES), F32)]),
        compiler_params=_params(("arbitrary",)),
        name="retention",
    )(*tabs, rqk, zr, rqk, zr, s0)


def _out_kernel(*refs, even):
    if even:
        (a_ref, yf_ref, yb_ref, bonus_ref, g_ref, gng_ref, gnb_ref, ones_ref,
         x_ref, mod_ref, ng_ref, wo_ref, rw_ref, rb_ref,
         y_ref, xp_ref, ti_ref, tg_ref, rk_ref, cnt_ref, run_ref) = refs
        ones_bd = ones_ref[...]
        ys = yf_ref[...] + yb_ref[...]
        mu = _group_sum(ys, ones_bd) * (1.0 / HEAD)
        dv = ys - mu
        var = _group_sum(dv * dv, ones_bd) * (1.0 / HEAD)
        yn = dv * lax.rsqrt(var + RWKV_GN_EPS) * gng_ref[...] + gnb_ref[...]
        left = a_ref[...]
        right = (yn + bonus_ref[...]) * g_ref[...]
    else:
        (c_ref, of_ref, ob_ref, rg_ref, gng_ref,
         x_ref, mod_ref, ng_ref, wo_ref, rw_ref, rb_ref,
         y_ref, xp_ref, ti_ref, tg_ref, rk_ref, cnt_ref, run_ref) = refs
        left = c_ref[...]
        rg = rg_ref[...]
        gate = rg * jax.nn.sigmoid(rg)
        os_ = of_ref[...] + ob_ref[...]
        parts = []
        for h in range(4):
            oh = os_[:, h * LANES:(h + 1) * LANES]
            mu = jnp.mean(oh, axis=-1, keepdims=True)
            dv = oh - mu
            var = jnp.mean(dv * dv, axis=-1, keepdims=True)
            parts.append(dv * lax.rsqrt(var + 1e-5))
        right = gate * (jnp.concatenate(parts, axis=1) * gng_ref[...])
    mod = mod_ref[0]
    o = (jnp.dot(left.astype(BF16), wo_ref[0:HALF, :], preferred_element_type=F32)
         + jnp.dot(right.astype(BF16), wo_ref[HALF:2 * HALF, :], preferred_element_type=F32))
    y = x_ref[...] + mod[:, 2 * D:3 * D] * o
    y_ref[...] = y
    yn2 = y * lax.rsqrt(jnp.mean(y * y, axis=-1, keepdims=True) + 1e-6) * ng_ref[...]
    t = yn2 * (1.0 + mod[:, 4 * D:5 * D]) + mod[:, 3 * D:4 * D]
    xp_ref[...] = t
    logits = _mm(t, rw_ref[...], NN, 3) + rb_ref[...]
    lane = lax.broadcasted_iota(I32, (TM, LANES), 1)
    neg = jnp.float32(-jnp.inf)
    lg = jnp.where(lane < N_EXPERTS, logits, neg)
    vals, hits = [], []
    for _ in range(TOP_K):
        m = jnp.max(lg, axis=-1, keepdims=True)
        ix = jnp.min(jnp.where(lg == m, lane, LANES), axis=-1, keepdims=True)
        hit = lane == ix
        vals.append(m)
        hits.append((ix, hit))
        lg = jnp.where(hit, neg, lg)
    es = [jnp.exp(vv - vals[0]) for vv in vals]
    den = es[0] + es[1] + es[2] + es[3]

    @pl.when(pl.program_id(0) == 0)
    def _():
        run_ref[...] = jnp.zeros_like(run_ref)

    member = jnp.zeros((TM, LANES), F32)
    for _, hit in hits:
        member = member + jnp.where(hit, 1.0, 0.0)
    ri = lax.broadcasted_iota(I32, (TM, TM), 0)
    ci = lax.broadcasted_iota(I32, (TM, TM), 1)
    before = jnp.where(ci < ri, 1.0, 0.0).astype(BF16)
    seen = run_ref[...] + jnp.dot(before, member.astype(BF16), preferred_element_type=F32)
    ti = jnp.zeros((TM, LANES), I32)
    tg = jnp.zeros((TM, LANES), F32)
    rk = jnp.zeros((TM, LANES), F32)
    for kk, (ix, hit) in enumerate(hits):
        ti = jnp.where(lane == kk, ix, ti)
        tg = jnp.where(lane == kk, es[kk] / den, tg)
        rk = jnp.where(lane == kk, jnp.sum(jnp.where(hit, seen, 0.0), axis=-1, keepdims=True), rk)
    ti_ref[...] = ti
    tg_ref[...] = tg
    rk_ref[...] = rk.astype(I32)
    run_ref[...] = run_ref[...] + jnp.sum(member, axis=0, keepdims=True)
    cnt_ref[...] = run_ref[...]


def _out_proj(even, mix_args, mix_specs, x, mod, norm_g, w_out_bf16, rw_pad, rb_pad):
    row = lambda i: (i, 0)
    modspec = pl.BlockSpec((1, 1, N_MOD * D), lambda i: (_group(i), 0, 0))
    in_specs = list(mix_specs) + [pl.BlockSpec((TM, D), row), modspec, _full((1, D)), _full((D, D)),
                                  _full((D, LANES)), _full((1, LANES))]
    args = list(mix_args) + [x, mod, norm_g.reshape(1, D), w_out_bf16, rw_pad, rb_pad]
    lane_i = jax.ShapeDtypeStruct((T, LANES), I32)
    lane_spec = pl.BlockSpec((TM, LANES), row)
    return pl.pallas_call(
        functools.partial(_out_kernel, even=even),
        out_shape=[jax.ShapeDtypeStruct((T, D), F32), jax.ShapeDtypeStruct((T, D), F32),
                   lane_i, jax.ShapeDtypeStruct((T, LANES), F32), lane_i,
                   jax.ShapeDtypeStruct((1, LANES), F32)],
        grid=(NT,), in_specs=in_specs,
        out_specs=[pl.BlockSpec((TM, D), row), pl.BlockSpec((TM, D), row),
                   lane_spec, lane_spec, lane_spec, _full((1, LANES))],
        scratch_shapes=[pltpu.VMEM((1, LANES), F32)],
        compiler_params=_params(("arbitrary",), 48),
        name="out_proj",
    )(*args)


def _route_kernel(cnt_ref, ti_ref, rk_ref, dest_ref, te_ref, nt_ref):
    cnt = cnt_ref[...].astype(I32)
    ntile = lax.shift_right_logical(cnt + (TMX - 1), TMX.bit_length() - 1)
    ei = lax.broadcasted_iota(I32, (LANES, LANES), 0)
    ej = lax.broadcasted_iota(I32, (LANES, LANES), 1)
    upto = jnp.where(ei <= ej, 1.0, 0.0).astype(BF16)
    ntile_f = jnp.broadcast_to(ntile.astype(F32), (8, LANES))
    tile_end = jnp.dot(ntile_f.astype(BF16), upto, preferred_element_type=F32)[0:1, :]
    row_start = (tile_end - ntile.astype(F32)) * float(TMX)
    lane = lax.broadcasted_iota(I32, (TM, LANES), 1)
    ti = ti_ref[...]
    rk = rk_ref[...]
    dest = jnp.zeros((TM, LANES), F32)
    for k in range(TOP_K):
        hit = lane == ti[:, k:k + 1]
        start = jnp.sum(jnp.where(hit, row_start, 0.0), axis=-1, keepdims=True)
        dest = jnp.where(lane == k, start + rk[:, k:k + 1].astype(F32), dest)
    dest_ref[...] = dest.astype(I32)

    @pl.when(pl.program_id(0) == 0)
    def _():
        lane1 = lax.broadcasted_iota(I32, (1, LANES), 1)
        n_tiles = jnp.max(tile_end, axis=-1, keepdims=True)
        last_e = jnp.max(jnp.where(cnt > 0, lane1, 0), axis=-1, keepdims=True)
        tile = lax.broadcasted_iota(I32, (TM, 1), 0).astype(F32)
        te = jnp.sum(jnp.where(tile_end <= tile, 1, 0), axis=-1, keepdims=True)
        te = jnp.where(tile < n_tiles, te, last_e)
        te_ref[...] = jnp.broadcast_to(te, (TM, LANES)).astype(I32)
        first_row = lax.broadcasted_iota(I32, (8, LANES), 0) == 0
        nt_ref[...] = jnp.where(first_row, n_tiles, tile_end).astype(I32)


def _route(cnt, ti, rk):
    row = lambda i: (i, 0)
    return pl.pallas_call(
        _route_kernel,
        out_shape=[jax.ShapeDtypeStruct((T, LANES), I32), jax.ShapeDtypeStruct((TM, LANES), I32),
                   jax.ShapeDtypeStruct((8, LANES), I32)],
        grid=(NT,),
        in_specs=[_full((1, LANES)), pl.BlockSpec((TM, LANES), row), pl.BlockSpec((TM, LANES), row)],
        out_specs=[pl.BlockSpec((TM, LANES), row), _full((TM, LANES)), _full((8, LANES))],
        compiler_params=_params(("arbitrary",)),
        name="moe_route",
    )(cnt, ti, rk)


def _row_copy(src_ref, src_row, dst_ref, dst_row, sem):
    return pltpu.make_async_copy(src_ref.at[pl.ds(src_row, 1)], dst_ref.at[pl.ds(dst_row, 1)], sem)


def _wait_tiles(n, src_ref, dst_ref, sem):
    for _ in range(n):
        pltpu.make_async_copy(src_ref, dst_ref, sem).wait()


def _dispatch_kernel(dest_ref, tend_ref, x_ref, xs_ref, zero_ref, sem):
    i = pl.program_id(0)

    @pl.when(i == 0)
    def _():
        zero_ref[...] = jnp.zeros_like(zero_ref)

        def last_tile(e, fn):
            end = tend_ref[e]
            begin = tend_ref[e - 1] if e > 0 else 0

            @pl.when(end > begin)
            def _():
                fn(pltpu.make_async_copy(zero_ref, xs_ref.at[pl.ds((end - 1) * TMX, TMX)], sem))

        def unused_tile(j):
            return pltpu.make_async_copy(zero_ref, xs_ref.at[pl.ds(j * TMX, TMX)], sem)

        def start_unused(j, carry):
            unused_tile(j).start()
            return carry

        def wait_unused(j, carry):
            unused_tile(j).wait()
            return carry

        n_used = tend_ref[N_EXPERTS - 1]
        for e in range(N_EXPERTS):
            last_tile(e, lambda c: c.start())
        lax.fori_loop(n_used, MOE_TILES, start_unused, 0)
        for e in range(N_EXPERTS):
            last_tile(e, lambda c: c.wait())
        lax.fori_loop(n_used, MOE_TILES, wait_unused, 0)

    base = i * (TM * TOP_K)

    def start(r, carry):
        for k in range(TOP_K):
            _row_copy(x_ref, r, xs_ref, dest_ref[base + r * TOP_K + k], sem).start()
        return carry

    lax.fori_loop(0, TM, start, 0, unroll=4)
    _wait_tiles(TOP_K, x_ref, xs_ref.at[pl.ds(0, TM)], sem)


def _dispatch(dest_flat, tile_end, xt):
    return pl.pallas_call(
        _dispatch_kernel,
        out_shape=jax.ShapeDtypeStruct((R_PAD, D), F32),
        grid_spec=pltpu.PrefetchScalarGridSpec(
            num_scalar_prefetch=2, grid=(NT,),
            in_specs=[pl.BlockSpec((TM, D), lambda i, d, te: (i, 0))],
            out_specs=pl.BlockSpec(memory_space=pl.ANY),
            scratch_shapes=[pltpu.VMEM((TMX, D), F32), pltpu.SemaphoreType.DMA(())]),
        compiler_params=_params(("arbitrary",)),
        name="moe_dispatch",
    )(dest_flat, tile_end, xt)


W_PARTS = 4


def _expert_weights(i, nt, te_ref, w_ref, wbuf_ref, wsem, w16_ref, group_ref):
    rows = w_ref.shape[1] // W_PARTS

    def fetch(e, buf):
        return [pltpu.make_async_copy(w_ref.at[e, pl.ds(p * rows, rows)], wbuf_ref.at[buf, pl.ds(p * rows, rows)],
                                      wsem.at[buf]) for p in range(W_PARTS)]

    @pl.when(i == 0)
    def _():
        group_ref[0] = 0
        for c in fetch(te_ref[0], 0):
            c.start()

    first = jnp.logical_or(i == 0, te_ref[i] != te_ref[jnp.maximum(i - 1, 0)])

    @pl.when(jnp.logical_and(first, i < nt))
    def _():
        cur = group_ref[0] % 2
        nxt = lax.while_loop(
            lambda j: jnp.logical_and(j < nt, te_ref[jnp.minimum(j, MOE_TILES - 1)] == te_ref[i]),
            lambda j: j + 1, i + 1)

        @pl.when(nxt < nt)
        def _():
            for c in fetch(te_ref[jnp.minimum(nxt, MOE_TILES - 1)], 1 - cur):
                c.start()

        for c in fetch(0, cur):
            c.wait()
        w16_ref[...] = wbuf_ref[cur].astype(BF16)
        group_ref[0] = group_ref[0] + 1


def _experts_kernel(te_ref, nt_ref, xs_ref, wgu_ref, bgu_ref, wdn_ref, bdn_ref, y_ref,
                    gu_buf, gu_sem, gu16_ref, gu_group, dn_buf, dn_sem, dn16_ref, dn_group):
    i = pl.program_id(0)
    nt = nt_ref[0]
    _expert_weights(i, nt, te_ref, wgu_ref, gu_buf, gu_sem, gu16_ref, gu_group)
    _expert_weights(i, nt, te_ref, wdn_ref, dn_buf, dn_sem, dn16_ref, dn_group)

    @pl.when(i < nt)
    def _():
        x16 = xs_ref[...].astype(BF16)
        y = bdn_ref[0]
        for h in range(2):
            gc = slice(h * HALF, (h + 1) * HALF)
            uc = slice(D + h * HALF, D + (h + 1) * HALF)
            g = jnp.dot(x16, gu16_ref[:, gc], preferred_element_type=F32) + bgu_ref[0, :, gc]
            u = jnp.dot(x16, gu16_ref[:, uc], preferred_element_type=F32) + bgu_ref[0, :, uc]
            gt = jnp.minimum(g, SWIGLU_LIMIT)
            up = jnp.clip(u, -SWIGLU_LIMIT, SWIGLU_LIMIT)
            act = ((up + 1.0) * gt * jax.nn.sigmoid(SWIGLU_ALPHA * gt)).astype(BF16)
            y = y + jnp.dot(act, dn16_ref[gc, :], preferred_element_type=F32)
        y_ref[...] = y

    @pl.when(i >= nt)
    def _():
        y_ref[...] = jnp.zeros_like(y_ref)


def _tile_clamped(i, te, nt):
    return (jnp.minimum(i, jnp.maximum(nt[0] - 1, 0)), 0)


def _weight_scratch(n_out):
    return [pltpu.VMEM((2, D, n_out), F32), pltpu.SemaphoreType.DMA((2,)), pltpu.VMEM((D, n_out), BF16),
            pltpu.SMEM((1,), I32)]


def _experts(te, n_tiles, xs, w_gu, b_gu, w_dn, b_dn):
    return pl.pallas_call(
        _experts_kernel,
        out_shape=jax.ShapeDtypeStruct((R_PAD, D), F32),
        grid_spec=pltpu.PrefetchScalarGridSpec(
            num_scalar_prefetch=2, grid=(MOE_TILES,),
            in_specs=[pl.BlockSpec((TMX, D), _tile_clamped), pl.BlockSpec(memory_space=pl.ANY),
                      pl.BlockSpec((1, 1, 2 * D), lambda i, te, nt: (te[i], 0, 0)),
                      pl.BlockSpec(memory_space=pl.ANY),
                      pl.BlockSpec((1, 1, D), lambda i, te, nt: (te[i], 0, 0))],
            out_specs=pl.BlockSpec((TMX, D), lambda i, te, nt: (i, 0)),
            scratch_shapes=_weight_scratch(2 * D) + _weight_scratch(D)),
        compiler_params=_params(("arbitrary",), 58),
        name="moe_experts",
    )(te, n_tiles, xs, w_gu, b_gu, w_dn, b_dn)


def _combine_kernel(dest_ref, x_ref, tg_ref, mod_ref, ys_ref, o_ref, buf_ref, sem):
    i = pl.program_id(0)
    slot = i % 2

    def gather(tile, b):
        base = tile * (TM * TOP_K)

        def body(r, carry):
            for k in range(TOP_K):
                _row_copy(ys_ref, dest_ref[base + r * TOP_K + k], buf_ref.at[b, k], r, sem.at[b]).start()
            return carry

        lax.fori_loop(0, TM, body, 0, unroll=4)

    @pl.when(i == 0)
    def _():
        gather(0, 0)

    @pl.when(i + 1 < NT)
    def _():
        gather(i + 1, 1 - slot)

    _wait_tiles(TOP_K, ys_ref.at[pl.ds(0, TM)], buf_ref.at[slot, 0], sem.at[slot])
    tg = tg_ref[...]
    f = tg[:, 0:1] * buf_ref[slot, 0]
    for k in range(1, TOP_K):
        f = f + tg[:, k:k + 1] * buf_ref[slot, k]
    o_ref[...] = x_ref[...] + mod_ref[0][:, 5 * D:6 * D] * f


def _combine(dest_flat, x, tg, mod, ys):
    row = lambda i, d: (i, 0)
    return pl.pallas_call(
        _combine_kernel,
        out_shape=jax.ShapeDtypeStruct((T, D), F32),
        grid_spec=pltpu.PrefetchScalarGridSpec(
            num_scalar_prefetch=1, grid=(NT,),
            in_specs=[pl.BlockSpec((TM, D), row), pl.BlockSpec((TM, LANES), row),
                      pl.BlockSpec((1, 1, N_MOD * D), lambda i, d: (_group(i), 0, 0)),
                      pl.BlockSpec(memory_space=pl.ANY)],
            out_specs=pl.BlockSpec((TM, D), row),
            scratch_shapes=[pltpu.VMEM((2, TOP_K, TM, D), F32), pltpu.SemaphoreType.DMA((2,))]),
        compiler_params=_params(("arbitrary",), 40),
        name="moe_combine",
    )(dest_flat, x, tg, mod, ys)


def _moe(layer, y, xt, ti, tg, rk, cnt, mod, w_gu, b_gu, w_dn, b_dn):
    dest, te, nt = _route(cnt, ti, rk)
    dest_flat = dest[:, :TOP_K].reshape(-1)
    te = te[:MOE_TILES, 0] + layer * N_EXPERTS
    n_tiles = nt[0, :1]
    xs = _dispatch(dest_flat, nt[1, :N_EXPERTS], xt)
    n_all = w_gu.shape[0] * N_EXPERTS
    ys = _experts(te, n_tiles, xs, w_gu.reshape(n_all, D, 2 * D), b_gu.reshape(n_all, 1, 2 * D),
                  w_dn.reshape(n_all, D, D), b_dn.reshape(n_all, 1, D))
    return _combine(dest_flat, y, tg, mod, ys)


def _ones_blockdiag():
    idx = np.arange(HALF) // HEAD
    return jnp.asarray((idx[:, None] == idx[None, :]).astype(np.float32), dtype=BF16)


def _rope_tables():
    pos = jnp.arange(L_SAMPLE)
    rowp = (pos // 64).astype(F32)
    colp = (pos % 64).astype(F32)
    nf = HEAD // 4
    inv = jnp.power(10000.0, -jnp.arange(nf, dtype=F32) / nf)
    ar = rowp[:, None] * inv[None, :]
    ac = colp[:, None] * inv[None, :]
    cos64 = jnp.concatenate([jnp.cos(ar), jnp.cos(ar), jnp.cos(ac), jnp.cos(ac)], axis=1)
    sin64 = jnp.concatenate([-jnp.sin(ar), jnp.sin(ar), -jnp.sin(ac), jnp.sin(ac)], axis=1)
    cos = jnp.tile(cos64, (1, HALF // HEAD))
    sin = jnp.tile(sin64, (1, HALF // HEAD))
    ident = jnp.ones((TM, HALF), F32)
    return (jnp.concatenate([ident, cos], axis=0), jnp.concatenate([jnp.zeros((TM, HALF), F32), sin], axis=0))


def _bd_pairs(s):
    lead = s.shape[:-3]
    s = s.reshape(lead + (4, 2, HEAD, HEAD))
    z = jnp.zeros_like(s[..., 0, :, :])
    top = jnp.concatenate([s[..., 0, :, :], z], axis=-1)
    bot = jnp.concatenate([z, s[..., 1, :, :]], axis=-1)
    return jnp.concatenate([top, bot], axis=-2)


def _bd_unpairs(s):
    a = s[..., 0:HEAD, 0:HEAD]
    b = s[..., HEAD:, HEAD:]
    out = jnp.stack([a, b], axis=-3)
    return out.reshape(s.shape[:-3] + (8, HEAD, HEAD))


def kernel(x_prompt, x_sample, state_rwkv, cache_k_diff, cache_v_diff, state_retention, c, c_ctx, norm_g, ada_w, ada_b, e_w_in, e_w_out, sgu_ln_g, sgu_w, sgu_b, rw_mu, rw_w0, rw_w_up, rw_a0, rw_a_up, rw_g_up, rw_k_k, rw_k_a, rw_r_k, rw_gn_g, rw_gn_b, o_w_in, o_w_out, da_qk_g, da_lam, da_subln_g, ret_gn_g, router_w, router_b, ex_w_gu, ex_b_gu, ex_w_dn, ex_b_dn):
    x = jnp.concatenate([x_prompt.reshape(T_PROMPT, D), x_sample.reshape(T_SAMPLE, D)], axis=0)
    cvec8 = jnp.concatenate([c_ctx[None, :], c, jnp.zeros((3, D), F32)], axis=0)
    mods = _adaln(cvec8, ada_w, ada_b)
    mod0 = mods[0].reshape(8, 1, N_MOD * D)
    mod1 = mods[1].reshape(8, 1, N_MOD * D)
    ones_bd = _ones_blockdiag()
    rw_pad = jnp.pad(router_w, ((0, 0), (0, 0), (0, LANES - N_EXPERTS)))
    rb_pad = jnp.pad(router_b, ((0, 0), (0, LANES - N_EXPERTS))).reshape(2, 1, LANES)
    row = lambda i: (i, 0)
    half = pl.BlockSpec((TM, HALF), row)

    za, zb = _in_proj(x, norm_g[0, 0], mod0, e_w_in[0].astype(BF16), (2 * HALF, B_COLS))
    bs_full = jnp.repeat(sgu_b[0].T, HEAD, axis=1)
    a_out = _sgu(za, sgu_ln_g[0], sgu_w[0].astype(BF16), bs_full)
    zpad = jnp.zeros((2, HEAD, HALF), F32)
    wup_pad = jnp.concatenate([rw_w_up[0], zpad], axis=1).astype(BF16)
    aup_pad = jnp.concatenate([zpad, rw_a_up[0]], axis=1).astype(BF16)
    r, v, kkn, bonus, g, lw, kt, b = _rwkv_prep(zb, rw_mu[0], rw_k_k[0], rw_k_a[0], rw_r_k[0], rw_w0[0], rw_a0[0],
                                                wup_pad, aup_pad, rw_g_up[0].astype(BF16), ones_bd)
    s0_sample = _bd_pairs(jnp.moveaxis(state_rwkv[:, 0], 1, 0))
    s0_rw = jnp.concatenate([jnp.zeros((2, N_PROMPT, 4, LANES, LANES), F32), s0_sample], axis=1)
    yf_rw, yb_rw, sfin_rw = _rwkv_scan(r, v, kkn, lw, kt, b, s0_rw)
    new_rwkv = jnp.moveaxis(_bd_unpairs(sfin_rw[:, :N_PROMPT]), 0, 1)[:, None]
    y0, xp0, ti0, tg0, rk0, cnt0 = _out_proj(
        True,
        [a_out, yf_rw, yb_rw, bonus, g, rw_gn_g[0].reshape(1, HALF), rw_gn_b[0].reshape(1, HALF), ones_bd],
        [half, half, half, half, half, _full((1, HALF)), _full((1, HALF)), _full((HALF, HALF))],
        x, mod0, norm_g[0, 1], e_w_out[0].astype(BF16), rw_pad[0], rb_pad[0])
    x1 = _moe(0, y0, xp0, ti0, tg0, rk0, cnt0, mod0, ex_w_gu, ex_b_gu, ex_w_dn, ex_b_dn)

    zc, zr = _in_proj(x1, norm_g[1, 0], mod1, o_w_in[0].astype(BF16), (3 * HALF, 3 * HALF))
    cos_tab, sin_tab = _rope_tables()
    qkg = jnp.tile(da_qk_g[0], (1, HALF // HEAD))
    cq, ck, ck_raw, rqk = _odd_prep(zc, zr, cos_tab, sin_tab, qkg, ones_bd)
    lambda_init = 0.8 - 0.6 * math.exp(-0.3 * 1)
    lv = da_lam[0]
    lam = jnp.exp(jnp.sum(lv[0] * lv[1])) - jnp.exp(jnp.sum(lv[2] * lv[3])) + lambda_init
    c_out = _attn(cq, ck, zc, 0, N_PROMPT, L_PROMPT, lam, da_subln_g[0], 1.0 - lambda_init,
                  jnp.zeros((T, HALF), F32))
    ctx_k = cache_k_diff[:, 0].reshape(N_SAMPLE, PAST, HALF)
    ctx_v = cache_v_diff[:, 0].reshape(N_SAMPLE, PAST, HALF)
    c_out = _attn(cq, ck, zc, T_PROMPT, N_SAMPLE, L_SAMPLE, lam, da_subln_g[0], 1.0 - lambda_init,
                  c_out, ctx_k, ctx_v)
    sr = jnp.moveaxis(state_retention[:, 0], 1, 0)
    zr0 = jnp.zeros_like(sr)
    s0_sample = jnp.stack([jnp.concatenate([sr[:, :, 0], zr0[:, :, 0]], axis=-2),
                           jnp.concatenate([zr0[:, :, 1], sr[:, :, 1]], axis=-2),
                           jnp.concatenate([sr[:, :, 2], zr0[:, :, 2]], axis=-2),
                           jnp.concatenate([zr0[:, :, 3], sr[:, :, 3]], axis=-2)], axis=2)
    s0_ret = jnp.concatenate([jnp.zeros((2, N_PROMPT, 4, LANES, LANES), F32), s0_sample], axis=1)
    of_ret, ob_ret, rfin = _retention(rqk, zr, s0_ret)
    rfin_p = rfin[:, :N_PROMPT]
    new_ret = jnp.stack([rfin_p[:, :, 0, 0:HEAD], rfin_p[:, :, 1, HEAD:], rfin_p[:, :, 2, 0:HEAD],
                         rfin_p[:, :, 3, HEAD:]], axis=2)
    new_ret = jnp.moveaxis(new_ret, 0, 1)[:, None]
    y1, xp1, ti1, tg1, rk1, cnt1 = _out_proj(
        False,
        [c_out, of_ret, ob_ret, zr, ret_gn_g[0].reshape(1, HALF)],
        [half, half, half, pl.BlockSpec((TM, HALF), lambda i: (i, 2)), _full((1, HALF))],
        x1, mod1, norm_g[1, 1], o_w_out[0].astype(BF16), rw_pad[1], rb_pad[1])
    y_fin = _moe(1, y1, xp1, ti1, tg1, rk1, cnt1, mod1, ex_w_gu, ex_b_gu, ex_w_dn, ex_b_dn)

    new_k = ck_raw[:T_PROMPT].reshape(N_PROMPT, 1, L_PROMPT, 4, LANES)
    new_v = zc[:T_PROMPT, 2 * HALF:3 * HALF].reshape(N_PROMPT, 1, L_PROMPT, 4, LANES)
    return (y_fin[:T_PROMPT].reshape(N_PROMPT, L_PROMPT, D), y_fin[T_PROMPT:].reshape(N_SAMPLE, L_SAMPLE, D),
            new_rwkv, new_k, new_v, new_ret)
```

```python
import functools
import math

import numpy as np
import jax
import jax.numpy as jnp
from jax import lax
from jax.experimental import pallas as pl
from jax.experimental.pallas import tpu as pltpu

F32 = jnp.float32
BF16 = jnp.bfloat16
I32 = jnp.int32

D = 1024
N_PROMPT, L_PROMPT = 16, 256
N_SAMPLE, L_SAMPLE = 4, 1024
N_SEQ = N_PROMPT + N_SAMPLE
PAST = 256
T_PROMPT = N_PROMPT * L_PROMPT
T_SAMPLE = N_SAMPLE * L_SAMPLE
T = T_PROMPT + T_SAMPLE
TM = 256
NT = T // TM
PROMPT_TILES = T_PROMPT // TM
TILES_PER_SAMPLE = L_SAMPLE // TM
N_MOD = 6
HALF = 512
B_COLS = 1792
HEAD = 64
W_DECAY_SCALE = math.exp(-0.5)
RWKV_GN_EPS = 64e-5
RW_CHUNK = 64
RET_CHUNK = 128
RET_EXP = ((5.0, 7.0, 9.0, 11.0), (6.0, 8.0, 10.0, 12.0))
N_EXPERTS = 32
TOP_K = 4
SWIGLU_LIMIT = 7.0
SWIGLU_ALPHA = 1.702
N_ASSIGN = T * TOP_K
TMX = 512
MOE_TILES = N_ASSIGN // TMX + N_EXPERTS
R_PAD = MOE_TILES * TMX
LANES = 128

NN = (((1,), (0,)), ((), ()))
NT_DIMS = (((1,), (1,)), ((), ()))
TN = (((0,), (0,)), ((), ()))


def _group(i):
    return jnp.where(i < PROMPT_TILES, 0, 1 + (i - PROMPT_TILES) // TILES_PER_SAMPLE)


def _mm(a, b, dims=NN, passes=1):
    dg = functools.partial(lax.dot_general, dimension_numbers=dims, preferred_element_type=F32)
    if passes == 1:
        return dg(a.astype(BF16), b.astype(BF16))
    a = a.astype(F32)
    b = b.astype(F32)
    ah = a.astype(BF16)
    al = (a - ah.astype(F32)).astype(BF16)
    bh = b.astype(BF16)
    if passes == 2:
        assert dims[0][0] == (1,)
        m = a.shape[0]
        both = dg(jnp.concatenate([ah, al], axis=0), bh)
        return both[0:m] + both[m:2 * m]
    bl = (b - bh.astype(F32)).astype(BF16)
    if dims[0][0] == (1,):
        m = a.shape[0]
        both = dg(jnp.concatenate([ah, al], axis=0), bh)
        return both[0:m] + (dg(ah, bl) + both[m:2 * m])
    return dg(ah, bh) + (dg(ah, bl) + dg(al, bh))


def _group_sum(x, ones_bd):
    xh = x.astype(BF16)
    xl = (x - xh.astype(F32)).astype(BF16)
    return (jnp.dot(xh, ones_bd, preferred_element_type=F32)
            + jnp.dot(xl, ones_bd, preferred_element_type=F32))


def _full(shape):
    nd = len(shape)
    return pl.BlockSpec(shape, lambda *_: (0,) * nd)


def _params(sem, vmem_mb=None):
    kw = {}
    if vmem_mb is not None:
        kw["vmem_limit_bytes"] = vmem_mb * 1024 * 1024
    return pltpu.CompilerParams(dimension_semantics=sem, **kw)


def _seq_tables(chunk):
    blk_f, blk_b, first, last, seq = [], [], [], [], []
    row = 0
    for s in range(N_SEQ):
        n = (L_PROMPT if s < N_PROMPT else L_SAMPLE) // chunk
        base = row // chunk
        for j in range(n):
            blk_f.append(base + j)
            blk_b.append(base + n - 1 - j)
            first.append(int(j == 0))
            last.append(int(j == n - 1))
            seq.append(s)
        row += n * chunk
    return tuple(np.asarray(a, np.int32) for a in (blk_f, blk_b, first, last, seq))


def _adaln_kernel(c_ref, w_ref, b_ref, o_ref):
    c = c_ref[...]
    s = c * jax.nn.sigmoid(c)
    o_ref[0] = _mm(s, w_ref[0], NN, 3) + b_ref[0]


def _adaln(cvec8, ada_w, ada_b):
    depth, _, n = ada_w.shape
    bn = 1536
    return pl.pallas_call(
        _adaln_kernel,
        out_shape=jax.ShapeDtypeStruct((depth, 8, n), F32),
        grid=(depth, n // bn),
        in_specs=[pl.BlockSpec((8, D), lambda l, j: (0, 0)),
                  pl.BlockSpec((1, D, bn), lambda l, j: (l, 0, j)),
                  pl.BlockSpec((1, 1, bn), lambda l, j: (l, 0, j))],
        out_specs=pl.BlockSpec((1, 8, bn), lambda l, j: (l, 0, j)),
        compiler_params=_params(("arbitrary", "arbitrary"), 40),
        name="adaln",
    )(cvec8, ada_w, ada_b.reshape(depth, 1, n))


def _in_kernel(x_ref, g_ref, mod_ref, w_ref, *outs, splits):
    x = x_ref[...]
    mod = mod_ref[0]
    y = x * lax.rsqrt(jnp.mean(x * x, axis=-1, keepdims=True) + 1e-6) * g_ref[...]
    h = (y * (1.0 + mod[:, D:2 * D]) + mod[:, 0:D]).astype(BF16)
    off = 0
    for o_ref, n in zip(outs, splits):
        o_ref[...] = jnp.dot(h, w_ref[:, off:off + n], preferred_element_type=F32)
        off += n


def _in_proj(x, g, mod, w_bf16, splits):
    n = w_bf16.shape[1]
    row = lambda i: (i, 0)
    return pl.pallas_call(
        functools.partial(_in_kernel, splits=splits),
        out_shape=[jax.ShapeDtypeStruct((T, s), F32) for s in splits],
        grid=(NT,),
        in_specs=[pl.BlockSpec((TM, D), row), _full((1, D)),
                  pl.BlockSpec((1, 1, N_MOD * D), lambda i: (_group(i), 0, 0)), _full((D, n))],
        out_specs=[pl.BlockSpec((TM, s), row) for s in splits],
        compiler_params=_params(("arbitrary",), 48),
        name="in_proj",
    )(x, g.reshape(1, D), mod, w_bf16)


def _gelu(x):
    return 0.5 * x * (1.0 + lax.erf(x * (1.0 / math.sqrt(2.0))))


def _sgu_kernel(za_ref, lng_ref, ws_ref, bs_ref, o_ref):
    u = _gelu(za_ref[:, 0:HALF])
    va = _gelu(za_ref[:, HALF:2 * HALF])
    mu = jnp.mean(va, axis=-1, keepdims=True)
    dv = va - mu
    var = jnp.mean(dv * dv, axis=-1, keepdims=True)
    vn = dv * lax.rsqrt(var + 1e-5) * lng_ref[...]
    lane = lax.broadcasted_iota(I32, (LANES, LANES), 1)
    first = lane < HEAD
    for c in range(TM // LANES):
        rows = slice(c * LANES, (c + 1) * LANES)
        for p in range(HALF // LANES):
            cols = slice(p * LANES, (p + 1) * LANES)
            vp = vn[rows, cols]
            s = (jnp.dot(ws_ref[2 * p], jnp.where(first, vp, 0.0).astype(BF16), preferred_element_type=F32)
                 + jnp.dot(ws_ref[2 * p + 1], jnp.where(first, 0.0, vp).astype(BF16), preferred_element_type=F32))
            o_ref[rows, cols] = u[rows, cols] * (s + bs_ref[:, cols])


def _sgu(za, ln_g, w_s_bf16, bs_full):
    return pl.pallas_call(
        _sgu_kernel,
        out_shape=jax.ShapeDtypeStruct((T, HALF), F32),
        grid=(NT,),
        in_specs=[pl.BlockSpec((TM, 2 * HALF), lambda i: (i, 0)), _full((1, HALF)),
                  _full((8, LANES, LANES)), _full((LANES, HALF))],
        out_specs=pl.BlockSpec((TM, HALF), lambda i: (i, 0)),
        compiler_params=_params(("arbitrary",)),
        name="sgu",
    )(za, ln_g.reshape(1, HALF), w_s_bf16, bs_full)


def _rwkv_prep_kernel(zb_ref, zp_ref, zn_ref, mu_ref, kk_ref, ka_ref, rk_ref, w0_ref, a0_ref,
                      wup_ref, aup_ref, gup_ref, ones_ref,
                      r_ref, v_ref, kkn_ref, bonus_ref, g_ref, lw_ref, kt_ref, b_ref):
    i = pl.program_id(0)
    in_sample = i >= PROMPT_TILES
    pos = (i - PROMPT_TILES) % TILES_PER_SAMPLE
    is_first = jnp.logical_or(jnp.logical_not(in_sample), pos == 0)
    is_last = jnp.logical_or(jnp.logical_not(in_sample), pos == TILES_PER_SAMPLE - 1)
    zb = zb_ref[...]
    prev_row = jnp.where(is_first, 0.0, zp_ref[7:8, :])
    next_row = jnp.where(is_last, 0.0, zn_ref[0:1, :])
    rowid = lax.broadcasted_iota(I32, (TM, 1), 0)
    zp = jnp.where(rowid == 0, prev_row, pltpu.roll(zb, 1, 0))
    zn = jnp.where(rowid == TM - 1, next_row, pltpu.roll(zb, TM - 1, 0))
    zs = zb + mu_ref[0:1, :] * (zp - zb) + mu_ref[1:2, :] * (zn - zb)
    r = zs[:, 0:HALF]
    k = zs[:, HALF:2 * HALF]
    v = zs[:, 2 * HALF:3 * HALF]
    wa = zs[:, 3 * HALF:3 * HALF + LANES]
    gd = zs[:, 3 * HALF + LANES:B_COLS]
    ones_bd = ones_ref[...]
    r_ref[...] = r
    v_ref[...] = v
    g_ref[...] = jnp.dot(jax.nn.sigmoid(gd).astype(BF16), gup_ref[...], preferred_element_type=F32)
    kk = k * kk_ref[...]
    kkn = kk / jnp.maximum(jnp.sqrt(_group_sum(kk * kk, ones_bd)), 1e-6)
    kkn_ref[...] = kkn
    bonus_ref[...] = _group_sum(r * k * rk_ref[...], ones_bd) * v
    tw = jnp.tanh(wa).astype(BF16)
    wa16 = wa.astype(BF16)
    for dd in range(2):
        lw_ref[dd] = -W_DECAY_SCALE * jax.nn.sigmoid(
            w0_ref[dd:dd + 1, :] + jnp.dot(tw, wup_ref[dd], preferred_element_type=F32))
        a = jax.nn.sigmoid(a0_ref[dd:dd + 1, :] + jnp.dot(wa16, aup_ref[dd], preferred_element_type=F32))
        kt_ref[dd] = k * (1.0 + (a - 1.0) * ka_ref[...])
        b_ref[dd] = a * kkn


def _rwkv_prep(zb, mu, k_k, k_a, r_k, w0, a0, wup_pad, aup_pad, g_up, ones_bd):
    row = lambda i: (i, 0)
    halo = TM // 8
    one = jax.ShapeDtypeStruct((T, HALF), F32)
    two = jax.ShapeDtypeStruct((2, T, HALF), F32)
    o1 = pl.BlockSpec((TM, HALF), row)
    o2 = pl.BlockSpec((2, TM, HALF), lambda i: (0, i, 0))
    return pl.pallas_call(
        _rwkv_prep_kernel,
        out_shape=[one, one, one, one, one, two, two, two],
        grid=(NT,),
        in_specs=[pl.BlockSpec((TM, B_COLS), row),
                  pl.BlockSpec((8, B_COLS), lambda i: (jnp.maximum(i * halo - 1, 0), 0)),
                  pl.BlockSpec((8, B_COLS), lambda i: (jnp.minimum((i + 1) * halo, T // 8 - 1), 0)),
                  _full((2, B_COLS)), _full((1, HALF)), _full((1, HALF)), _full((1, HALF)),
                  _full((2, HALF)), _full((2, HALF)),
                  _full((2, LANES, HALF)), _full((2, LANES, HALF)), _full((LANES, HALF)),
                  _full((HALF, HALF))],
        out_specs=[o1, o1, o1, o1, o1, o2, o2, o2],
        compiler_params=_params(("arbitrary",), 48),
        name="rwkv_prep",
    )(zb, zb, zb, mu, k_k.reshape(1, HALF), k_a.reshape(1, HALF), r_k.reshape(1, HALF), w0, a0,
      wup_pad, aup_pad, g_up, ones_bd)


def _rwkv_chunks(dirs):
    C = RW_CHUNK
    ti = lax.broadcasted_iota(I32, (C, C), 0)
    tj = lax.broadcasted_iota(I32, (C, C), 1)
    bi = lax.broadcasted_iota(I32, (LANES, LANES), 0)
    bj = lax.broadcasted_iota(I32, (LANES, LANES), 1)
    same = (bi >> 6) == (bj >> 6)
    pi = bi & (C - 1)
    pj = bj & (C - 1)
    eye = (bi == bj).astype(F32)
    h0 = lax.broadcasted_iota(I32, (C, LANES), 1) < HEAD

    def stack(x):
        return jnp.concatenate([jnp.where(h0, x, 0.0), jnp.where(h0, 0.0, x)], axis=0)

    def fold(x):
        return x[0:C] + x[C:2 * C]

    chains = []
    for rev, r, v, kk, lw, kt, b, s_ref, y_ref in dirs:
        tri = jnp.where((tj >= ti) if rev else (tj <= ti), 1.0, 0.0).astype(F32)
        p1 = lw.astype(BF16)
        r1 = lw - p1.astype(F32)
        p2 = r1.astype(BF16)
        p3 = (r1 - p2.astype(F32)).astype(BF16)
        cs3 = jnp.dot(tri.astype(BF16), jnp.concatenate([p1, p2, p3], axis=1), preferred_element_type=F32)
        cs = cs3[:, 0:HALF] + (cs3[:, HALF:2 * HALF] + cs3[:, 2 * HALF:3 * HALF])
        ctot = cs[0:1, :] if rev else cs[C - 1:C, :]
        e_neg = jnp.exp(-cs)
        e_tail = jnp.exp(ctot - cs)
        q1 = kk * jnp.exp(cs - lw)
        k1 = kt * e_neg
        b1 = b * e_neg
        r1 = r * jnp.exp(cs)
        k2 = kt * e_tail
        b2 = b * e_tail
        e_tot = jnp.exp(ctot)
        strict = jnp.logical_and(same, (pj > pi) if rev else (pj < pi))
        incl = jnp.logical_and(same, (pj >= pi) if rev else (pj <= pi))
        for p in range(HALF // LANES):
            cols = slice(p * LANES, (p + 1) * LANES)
            chains.append(dict(p=p, cols=cols, strict=strict, incl=incl, s_ref=s_ref, y_ref=y_ref,
                               q1=q1[:, cols], k1=k1[:, cols], b1=b1[:, cols], r1=r1[:, cols],
                               k2=k2[:, cols], b2=b2[:, cols], v=v[:, cols], e_tot=e_tot[:, cols]))

    for ch in chains:
        lhs = jnp.concatenate([stack(ch["q1"]), stack(ch["r1"])], axis=0)
        rhs = jnp.concatenate([ch["k1"], ch["k1"], ch["b1"], ch["b1"]], axis=0)
        gm = _mm(lhs, rhs, NT_DIMS, 2)
        ch["mk"] = jnp.where(ch["strict"], gm[0:2 * C, 0:2 * C], 0.0)
        ch["mb"] = jnp.where(ch["strict"], gm[0:2 * C, 2 * C:4 * C], 0.0)
        ch["nk"] = jnp.where(ch["incl"], gm[2 * C:4 * C, 0:2 * C], 0.0)
        ch["nb"] = jnp.where(ch["incl"], gm[2 * C:4 * C, 2 * C:4 * C], 0.0)
        ch["tinv"] = eye - jnp.where((pi >> 1) == (pj >> 1), ch["mb"], 0.0)
    size = 2
    while size < C:
        sh = size.bit_length() - 1
        blk = jnp.logical_and((pi >> (sh + 1)) == (pj >> (sh + 1)), (pi >> sh) != (pj >> sh))
        for ch in chains:
            ch["tn"] = _mm(ch["tinv"], jnp.where(blk, ch["mb"], 0.0), NN, 2)
        for ch in chains:
            ch["tinv"] = ch["tinv"] - _mm(ch["tn"], ch["tinv"], NN, 2)
        size *= 2
    for ch in chains:
        vst = stack(ch["v"])
        ch["mkv"] = fold(_mm(ch["mk"], vst, NN, 2))
        ch["nkv"] = fold(_mm(ch["nk"], vst, NN, 1))
        ch["s"] = ch["s_ref"][ch["p"]]
        ch["qr"] = _mm(jnp.concatenate([ch["q1"], ch["r1"]], axis=0), ch["s"], NT_DIMS, 2)
    for ch in chains:
        ch["u"] = fold(_mm(ch["tinv"], stack(ch["mkv"] + ch["qr"][0:C]), NN, 2))
    for ch in chains:
        ch["y_ref"][:, ch["cols"]] = ch["qr"][C:2 * C] + ch["nkv"] - fold(_mm(ch["nb"], stack(ch["u"]), NN, 1))
        upd = _mm(jnp.concatenate([ch["v"], ch["u"]], axis=0),
                  jnp.concatenate([ch["k2"], -ch["b2"]], axis=0), TN, 3)
        ch["s_ref"][ch["p"]] = ch["s"] * ch["e_tot"] + jnp.where(same, upd, 0.0)


def _rwkv_scan_kernel(bf_ref, bb_ref, first_ref, last_ref, seq_ref,
                      rf_ref, vf_ref, kkf_ref, lwf_ref, ktf_ref, bfw_ref,
                      rb_ref, vb_ref, kkb_ref, lwb_ref, ktb_ref, bbw_ref, s0_ref,
                      yf_ref, yb_ref, sfin_ref, s_ref):
    step = pl.program_id(0)

    @pl.when(first_ref[step] == 1)
    def _():
        s_ref[...] = s0_ref[:, 0]

    _rwkv_chunks([
        (False, rf_ref[...], vf_ref[...], kkf_ref[...], lwf_ref[0], ktf_ref[0], bfw_ref[0], s_ref.at[0], yf_ref),
        (True, rb_ref[...], vb_ref[...], kkb_ref[...], lwb_ref[0], ktb_ref[0], bbw_ref[0], s_ref.at[1], yb_ref)])

    @pl.when(last_ref[step] == 1)
    def _():
        sfin_ref[:, 0] = s_ref[...]


def _rwkv_scan(r, v, kk, lw, kt, b, s0_bd):
    C = RW_CHUNK
    tabs = _seq_tables(C)
    fwd = lambda i, bf, bb, fi, la, sq: (bf[i], 0)
    bwd = lambda i, bf, bb, fi, la, sq: (bb[i], 0)
    fwd3 = lambda i, bf, bb, fi, la, sq: (0, bf[i], 0)
    bwd3 = lambda i, bf, bb, fi, la, sq: (1, bb[i], 0)
    st = pl.BlockSpec((2, 1, 4, LANES, LANES), lambda i, bf, bb, fi, la, sq: (0, sq[i], 0, 0, 0))
    one_f, one_b = pl.BlockSpec((C, HALF), fwd), pl.BlockSpec((C, HALF), bwd)
    two_f, two_b = pl.BlockSpec((1, C, HALF), fwd3), pl.BlockSpec((1, C, HALF), bwd3)
    return pl.pallas_call(
        _rwkv_scan_kernel,
        out_shape=[jax.ShapeDtypeStruct((T, HALF), F32), jax.ShapeDtypeStruct((T, HALF), F32),
                   jax.ShapeDtypeStruct((2, N_SEQ, 4, LANES, LANES), F32)],
        grid_spec=pltpu.PrefetchScalarGridSpec(
            num_scalar_prefetch=5, grid=(len(tabs[0]),),
            in_specs=[one_f, one_f, one_f, two_f, two_f, two_f,
                      one_b, one_b, one_b, two_b, two_b, two_b, st],
            out_specs=[one_f, one_b, st],
            scratch_shapes=[pltpu.VMEM((2, 4, LANES, LANES), F32)]),
        compiler_params=_params(("arbitrary",)),
        name="rwkv_scan",
    )(*tabs, r, v, kk, lw, kt, b, r, v, kk, lw, kt, b, s0_bd)


def _rope(x, cos, sin_signed, first16):
    w = x.shape[1]
    partner = jnp.where(first16, pltpu.roll(x, w - 16, 1), pltpu.roll(x, 16, 1))
    return x * cos + partner * sin_signed


def _odd_prep_kernel(zc_ref, zr_ref, cos_ref, sin_ref, qkg_ref, ones_ref, cq_ref, ck_ref, ckraw_ref, rqk_ref):
    ones_bd = ones_ref[...]
    cos = cos_ref[...]
    sin = sin_ref[...]
    lane = lax.broadcasted_iota(I32, (TM, HALF), 1)
    first16 = (lane & 31) < 16
    for idx, (o_ref, raw_ref) in enumerate(((cq_ref, None), (ck_ref, ckraw_ref))):
        x = zc_ref[:, idx * HALF:(idx + 1) * HALF]
        ms = _group_sum(x * x, ones_bd) * (1.0 / HEAD)
        xn = x * lax.rsqrt(ms + 1e-6) * qkg_ref[idx:idx + 1, :]
        if raw_ref is not None:
            raw_ref[...] = xn
        o_ref[...] = _rope(xn, cos, sin, first16)
    rqk = _rope(zr_ref[...], cos, sin, first16)
    rqk_ref[...] = jnp.where(lane < HALF // 2, rqk * (HEAD ** -0.5), rqk)


def _odd_prep(zc, zr, cos_tab, sin_tab, qkg_tiled, ones_bd):
    row = lambda i: (i, 0)
    tab = lambda i: (jnp.where(i < PROMPT_TILES, 0, 1 + (i - PROMPT_TILES) % TILES_PER_SAMPLE), 0)
    one = jax.ShapeDtypeStruct((T, HALF), F32)
    o1 = pl.BlockSpec((TM, HALF), row)
    return pl.pallas_call(
        _odd_prep_kernel,
        out_shape=[one, one, one, one], grid=(NT,),
        in_specs=[pl.BlockSpec((TM, 2 * HALF), row), pl.BlockSpec((TM, HALF), row),
                  pl.BlockSpec((TM, HALF), tab), pl.BlockSpec((TM, HALF), tab),
                  _full((2, HALF)), _full((HALF, HALF))],
        out_specs=[o1, o1, o1, o1],
        compiler_params=_params(("arbitrary",)),
        name="odd_prep",
    )(zc, zr, cos_tab, sin_tab, qkg_tiled, ones_bd)


def _attn_kernel(*refs, has_ctx, one_minus_li):
    if has_ctx:
        q_ref, k_ref, v_ref, kc_ref, vc_ref, lam_ref, sg_ref, _, o_ref = refs
    else:
        q_ref, k_ref, v_ref, lam_ref, sg_ref, _, o_ref = refs
    lam = lam_ref[...]
    lane = lax.broadcasted_iota(I32, (LANES, LANES), 1)
    m0 = lane < HEAD
    scale = HEAD ** -0.5
    for h in range(4):
        cols = slice(h * LANES, (h + 1) * LANES)
        qp = q_ref[:, cols]
        segs = [(k_ref[:, cols], v_ref[:, cols])]
        if has_ctx:
            segs.append((kc_ref[0, :, cols], vc_ref[0, :, cols]))
        outs = []
        for qm in (jnp.where(m0, qp, 0.0), jnp.where(m0, 0.0, qp)):
            qm16 = qm.astype(BF16)
            ss = [lax.dot_general(qm16, ks.astype(BF16), NT_DIMS, preferred_element_type=F32) * scale
                  for ks, _ in segs]
            mx = ss[0].max(axis=-1, keepdims=True)
            for s_ in ss[1:]:
                mx = jnp.maximum(mx, s_.max(axis=-1, keepdims=True))
            ps = [jnp.exp(s_ - mx) for s_ in ss]
            den = ps[0].sum(axis=-1, keepdims=True)
            for p_ in ps[1:]:
                den = den + p_.sum(axis=-1, keepdims=True)
            outs.append([p_ / den for p_ in ps])
        acc = None
        for si, (_, vs) in enumerate(segs):
            amap = outs[0][si] - lam * outs[1][si]
            t = jnp.dot(amap.astype(BF16), vs.astype(BF16), preferred_element_type=F32)
            acc = t if acc is None else acc + t
        nrm = acc * lax.rsqrt(jnp.mean(acc * acc, axis=-1, keepdims=True) + 1e-6) * sg_ref[...]
        o_ref[:, cols] = nrm * one_minus_li


def _attn(cq, ck, zc, row0, n_seq, seq_len, lam, subln_g, one_minus_li, prev, ctx_k=None, ctx_v=None):
    nq = seq_len // LANES
    qb0 = row0 // LANES
    sb0 = row0 // seq_len
    in_specs = [pl.BlockSpec((LANES, HALF), lambda s, q: (qb0 + s * nq + q, 0)),
                pl.BlockSpec((seq_len, HALF), lambda s, q: (sb0 + s, 0)),
                pl.BlockSpec((seq_len, HALF), lambda s, q: (sb0 + s, 2))]
    args = [cq, ck, zc]
    if ctx_k is not None:
        in_specs += [pl.BlockSpec((1, PAST, HALF), lambda s, q: (s, 0, 0))] * 2
        args += [ctx_k, ctx_v]
    in_specs += [_full((1, 1)), _full((1, LANES))]
    args += [lam.reshape(1, 1), subln_g.reshape(1, LANES)]
    in_specs.append(pl.BlockSpec(memory_space=pl.ANY))
    args.append(prev)
    aliases = {len(args) - 1: 0}
    return pl.pallas_call(
        functools.partial(_attn_kernel, has_ctx=ctx_k is not None, one_minus_li=one_minus_li),
        out_shape=jax.ShapeDtypeStruct((T, HALF), F32),
        grid=(n_seq, nq), in_specs=in_specs,
        out_specs=pl.BlockSpec((LANES, HALF), lambda s, q: (qb0 + s * nq + q, 0)),
        input_output_aliases=aliases,
        compiler_params=_params(("arbitrary", "arbitrary"), 48),
        name="diff_attn",
    )(*args)


_LOG_GAMMA = tuple(tuple(float(np.log1p(-np.exp2(-np.float32(e)), dtype=np.float32)) for e in es)
                   for es in RET_EXP)


def _ret_chunks(dirs):
    C = RET_CHUNK
    ii = lax.broadcasted_iota(I32, (C, C), 0)
    jj = lax.broadcasted_iota(I32, (C, C), 1)
    ri = lax.broadcasted_iota(I32, (C, 1), 0)
    lane = lax.broadcasted_iota(I32, (C, LANES), 1)
    chains = []
    for rev, qk_ref, v_ref, s_ref, o_ref in dirs:
        mask = (jj > ii) if rev else (jj <= ii)
        dist = jnp.where(mask, (jj - ii) if rev else (ii - jj), 0).astype(F32)
        kpow = (ri if rev else (C - 1 - ri)).astype(F32)
        qpow = ((C - ri) if rev else (ri + 1)).astype(F32)
        for h in range(4):
            lg = _LOG_GAMMA[1 if rev else 0][h]
            p = h // 2
            hm = (lane < HEAD) if h % 2 == 0 else (lane >= HEAD)
            qp = jnp.where(hm, qk_ref[:, p * LANES:(p + 1) * LANES], 0.0)
            kp = jnp.where(hm, qk_ref[:, HALF // 2 + p * LANES:HALF // 2 + (p + 1) * LANES], 0.0)
            chains.append(dict(
                h=h, lg=lg, s_ref=s_ref, o_ref=o_ref, q16=qp.astype(BF16), k16=kp.astype(BF16),
                qw16=(qp * jnp.exp(lg * qpow)).astype(BF16), kw16=(kp * jnp.exp(lg * kpow)).astype(BF16),
                v16=v_ref[:, h * LANES:(h + 1) * LANES].astype(BF16),
                decay=jnp.where(mask, jnp.exp(lg * dist), 0.0)))
    for ch in chains:
        ch["sc"] = (lax.dot_general(ch["q16"], ch["k16"], NT_DIMS, preferred_element_type=F32)
                    * ch["decay"]).astype(BF16)
        ch["s"] = ch["s_ref"][ch["h"]]
    for ch in chains:
        ch["o"] = (jnp.dot(ch["sc"], ch["v16"], preferred_element_type=F32)
                   + jnp.dot(ch["qw16"], ch["s"].astype(BF16), preferred_element_type=F32))
        ch["kv"] = lax.dot_general(ch["kw16"], ch["v16"], TN, preferred_element_type=F32)
    for ch in chains:
        h = ch["h"]
        ch["o_ref"][:, h * LANES:(h + 1) * LANES] = ch["o"]
        ch["s_ref"][h] = math.exp(ch["lg"] * C) * ch["s"] + ch["kv"]


def _ret_kernel(bf_ref, bb_ref, first_ref, last_ref, seq_ref,
                qkf_ref, vf_ref, qkb_ref, vb_ref, s0_ref, of_ref, ob_ref, sfin_ref, s_ref):
    step = pl.program_id(0)

    @pl.when(first_ref[step] == 1)
    def _():
        s_ref[...] = s0_ref[:, 0]

    _ret_chunks([(False, qkf_ref, vf_ref, s_ref.at[0], of_ref), (True, qkb_ref, vb_ref, s_ref.at[1], ob_ref)])

    @pl.when(last_ref[step] == 1)
    def _():
        sfin_ref[:, 0] = s_ref[...]


def _retention(rqk, zr, s0):
    C = RET_CHUNK
    tabs = _seq_tables(C)
    st = pl.BlockSpec((2, 1, 4, LANES, LANES), lambda i, bf, bb, fi, la, sq: (0, sq[i], 0, 0, 0))
    spec = lambda use_b, col: pl.BlockSpec(
        (C, HALF), lambda i, bf, bb, fi, la, sq: ((bb if use_b else bf)[i], col))
    return pl.pallas_call(
        _ret_kernel,
        out_shape=[jax.ShapeDtypeStruct((T, HALF), F32), jax.ShapeDtypeStruct((T, HALF), F32),
                   jax.ShapeDtypeStruct((2, N_SEQ, 4, LANES, LANES), F32)],
        grid_spec=pltpu.PrefetchScalarGridSpec(
            num_scalar_prefetch=5, grid=(len(tabs[0]),),
            in_specs=[spec(False, 0), spec(False, 1), spec(True, 0), spec(True, 1), st],
            out_specs=[spec(False, 0), spec(True, 0), st],
            scratch_shapes=[pltpu.VMEM((2, 4, LANES, LANES), F32)]),
        compiler_params=_params(("arbitrary",)),
        name="retention",
    )(*tabs, rqk, zr, rqk, zr, s0)


def _out_kernel(*refs, even):
    if even:
        (a_ref, yf_ref, yb_ref, bonus_ref, g_ref, gng_ref, gnb_ref, ones_ref,
         x_ref, mod_ref, ng_ref, wo_ref, rw_ref, rb_ref,
         y_ref, xp_ref, ti_ref, tg_ref, rk_ref, cnt_ref, run_ref) = refs
        ones_bd = ones_ref[...]
        ys = yf_ref[...] + yb_ref[...]
        mu = _group_sum(ys, ones_bd) * (1.0 / HEAD)
        dv = ys - mu
        var = _group_sum(dv * dv, ones_bd) * (1.0 / HEAD)
        yn = dv * lax.rsqrt(var + RWKV_GN_EPS) * gng_ref[...] + gnb_ref[...]
        left = a_ref[...]
        right = (yn + bonus_ref[...]) * g_ref[...]
    else:
        (c_ref, of_ref, ob_ref, rg_ref, gng_ref,
         x_ref, mod_ref, ng_ref, wo_ref, rw_ref, rb_ref,
         y_ref, xp_ref, ti_ref, tg_ref, rk_ref, cnt_ref, run_ref) = refs
        left = c_ref[...]
        rg = rg_ref[...]
        gate = rg * jax.nn.sigmoid(rg)
        os_ = of_ref[...] + ob_ref[...]
        parts = []
        for h in range(4):
            oh = os_[:, h * LANES:(h + 1) * LANES]
            mu = jnp.mean(oh, axis=-1, keepdims=True)
            dv = oh - mu
            var = jnp.mean(dv * dv, axis=-1, keepdims=True)
            parts.append(dv * lax.rsqrt(var + 1e-5))
        right = gate * (jnp.concatenate(parts, axis=1) * gng_ref[...])
    mod = mod_ref[0]
    o = (jnp.dot(left.astype(BF16), wo_ref[0:HALF, :], preferred_element_type=F32)
         + jnp.dot(right.astype(BF16), wo_ref[HALF:2 * HALF, :], preferred_element_type=F32))
    y = x_ref[...] + mod[:, 2 * D:3 * D] * o
    y_ref[...] = y
    yn2 = y * lax.rsqrt(jnp.mean(y * y, axis=-1, keepdims=True) + 1e-6) * ng_ref[...]
    t = yn2 * (1.0 + mod[:, 4 * D:5 * D]) + mod[:, 3 * D:4 * D]
    for c in range(D // LANES):
        xp_ref[:, c, :] = t[:, c * LANES:(c + 1) * LANES]
    logits = _mm(t, rw_ref[...], NN, 3) + rb_ref[...]
    lane = lax.broadcasted_iota(I32, (TM, LANES), 1)
    neg = jnp.float32(-jnp.inf)
    lg = jnp.where(lane < N_EXPERTS, logits, neg)
    vals, hits = [], []
    for _ in range(TOP_K):
        m = jnp.max(lg, axis=-1, keepdims=True)
        ix = jnp.min(jnp.where(lg == m, lane, LANES), axis=-1, keepdims=True)
        hit = lane == ix
        vals.append(m)
        hits.append((ix, hit))
        lg = jnp.where(hit, neg, lg)
    es = [jnp.exp(vv - vals[0]) for vv in vals]
    den = es[0] + es[1] + es[2] + es[3]

    @pl.when(pl.program_id(0) == 0)
    def _():
        run_ref[...] = jnp.zeros_like(run_ref)

    member = jnp.zeros((TM, LANES), F32)
    for _, hit in hits:
        member = member + jnp.where(hit, 1.0, 0.0)
    ri = lax.broadcasted_iota(I32, (TM, TM), 0)
    ci = lax.broadcasted_iota(I32, (TM, TM), 1)
    before = jnp.where(ci < ri, 1.0, 0.0).astype(BF16)
    seen = run_ref[...] + jnp.dot(before, member.astype(BF16), preferred_element_type=F32)
    ti = jnp.zeros((TM, LANES), I32)
    tg = jnp.zeros((TM, LANES), F32)
    rk = jnp.zeros((TM, LANES), F32)
    for kk, (ix, hit) in enumerate(hits):
        ti = jnp.where(lane == kk, ix, ti)
        tg = jnp.where(lane == kk, es[kk] / den, tg)
        rk = jnp.where(lane == kk, jnp.sum(jnp.where(hit, seen, 0.0), axis=-1, keepdims=True), rk)
    ti_ref[...] = ti
    tg_ref[...] = tg
    rk_ref[...] = rk.astype(I32)
    run_ref[...] = run_ref[...] + jnp.sum(member, axis=0, keepdims=True)
    cnt_ref[...] = run_ref[...]


def _out_proj(even, mix_args, mix_specs, x, mod, norm_g, w_out_bf16, rw_pad, rb_pad):
    row = lambda i: (i, 0)
    modspec = pl.BlockSpec((1, 1, N_MOD * D), lambda i: (_group(i), 0, 0))
    in_specs = list(mix_specs) + [pl.BlockSpec((TM, D), row), modspec, _full((1, D)), _full((D, D)),
                                  _full((D, LANES)), _full((1, LANES))]
    args = list(mix_args) + [x, mod, norm_g.reshape(1, D), w_out_bf16, rw_pad, rb_pad]
    lane_i = jax.ShapeDtypeStruct((T, LANES), I32)
    lane_spec = pl.BlockSpec((TM, LANES), row)
    return pl.pallas_call(
        functools.partial(_out_kernel, even=even),
        out_shape=[jax.ShapeDtypeStruct((T, D), F32), jax.ShapeDtypeStruct((T, D // LANES, LANES), F32),
                   lane_i, jax.ShapeDtypeStruct((T, LANES), F32), lane_i,
                   jax.ShapeDtypeStruct((1, LANES), F32)],
        grid=(NT,), in_specs=in_specs,
        out_specs=[pl.BlockSpec((TM, D), row), pl.BlockSpec((TM, D // LANES, LANES), lambda i: (i, 0, 0)),
                   lane_spec, lane_spec, lane_spec, _full((1, LANES))],
        scratch_shapes=[pltpu.VMEM((1, LANES), F32)],
        compiler_params=_params(("arbitrary",), 48),
        name="out_proj",
    )(*args)


def _route_kernel(cnt_ref, ti_ref, rk_ref, dest_ref, te_ref, nt_ref):
    cnt = cnt_ref[...].astype(I32)
    ntile = lax.shift_right_logical(cnt + (TMX - 1), TMX.bit_length() - 1)
    ei = lax.broadcasted_iota(I32, (LANES, LANES), 0)
    ej = lax.broadcasted_iota(I32, (LANES, LANES), 1)
    upto = jnp.where(ei <= ej, 1.0, 0.0).astype(BF16)
    ntile_f = jnp.broadcast_to(ntile.astype(F32), (8, LANES))
    tile_end = jnp.dot(ntile_f.astype(BF16), upto, preferred_element_type=F32)[0:1, :]
    row_start = (tile_end - ntile.astype(F32)) * float(TMX)
    lane = lax.broadcasted_iota(I32, (TM, LANES), 1)
    ti = ti_ref[...]
    rk = rk_ref[...]
    dest = jnp.zeros((TM, LANES), F32)
    for k in range(TOP_K):
        hit = lane == ti[:, k:k + 1]
        start = jnp.sum(jnp.where(hit, row_start, 0.0), axis=-1, keepdims=True)
        dest = jnp.where(lane == k, start + rk[:, k:k + 1].astype(F32), dest)
    dest_ref[...] = dest.astype(I32)

    @pl.when(pl.program_id(0) == 0)
    def _():
        lane1 = lax.broadcasted_iota(I32, (1, LANES), 1)
        n_tiles = jnp.max(tile_end, axis=-1, keepdims=True)
        last_e = jnp.max(jnp.where(cnt > 0, lane1, 0), axis=-1, keepdims=True)
        tile = lax.broadcasted_iota(I32, (TM, 1), 0).astype(F32)
        te = jnp.sum(jnp.where(tile_end <= tile, 1, 0), axis=-1, keepdims=True)
        te = jnp.where(tile < n_tiles, te, last_e)
        te_ref[...] = jnp.broadcast_to(te, (TM, LANES)).astype(I32)
        first_row = lax.broadcasted_iota(I32, (8, LANES), 0) == 0
        nt_ref[...] = jnp.where(first_row, n_tiles, tile_end).astype(I32)


def _route(cnt, ti, rk):
    row = lambda i: (i, 0)
    return pl.pallas_call(
        _route_kernel,
        out_shape=[jax.ShapeDtypeStruct((T, LANES), I32), jax.ShapeDtypeStruct((TM, LANES), I32),
                   jax.ShapeDtypeStruct((8, LANES), I32)],
        grid=(NT,),
        in_specs=[_full((1, LANES)), pl.BlockSpec((TM, LANES), row), pl.BlockSpec((TM, LANES), row)],
        out_specs=[pl.BlockSpec((TM, LANES), row), _full((TM, LANES)), _full((8, LANES))],
        compiler_params=_params(("arbitrary",)),
        name="moe_route",
    )(cnt, ti, rk)


def _row_copy(src_ref, src_row, dst_ref, dst_row, sem):
    return pltpu.make_async_copy(src_ref.at[pl.ds(src_row, 1)], dst_ref.at[pl.ds(dst_row, 1)], sem)


def _wait_tiles(n, src_ref, dst_ref, sem):
    for _ in range(n):
        pltpu.make_async_copy(src_ref, dst_ref, sem).wait()


def _dispatch_kernel(dest_ref, tend_ref, x_ref, xs_ref, zero_ref, sem):
    i = pl.program_id(0)

    @pl.when(i == 0)
    def _():
        zero_ref[...] = jnp.zeros_like(zero_ref)

        def last_tile(e, fn):
            end = tend_ref[e]
            begin = tend_ref[e - 1] if e > 0 else 0

            @pl.when(end > begin)
            def _():
                fn(pltpu.make_async_copy(zero_ref, xs_ref.at[pl.ds((end - 1) * TMX, TMX)], sem))

        def unused_tile(j):
            return pltpu.make_async_copy(zero_ref, xs_ref.at[pl.ds(j * TMX, TMX)], sem)

        def start_unused(j, carry):
            unused_tile(j).start()
            return carry

        def wait_unused(j, carry):
            unused_tile(j).wait()
            return carry

        n_used = tend_ref[N_EXPERTS - 1]
        for e in range(N_EXPERTS):
            last_tile(e, lambda c: c.start())
        lax.fori_loop(n_used, MOE_TILES, start_unused, 0)
        for e in range(N_EXPERTS):
            last_tile(e, lambda c: c.wait())
        lax.fori_loop(n_used, MOE_TILES, wait_unused, 0)

    base = i * (TM * TOP_K)

    def start(r, carry):
        for k in range(TOP_K):
            _row_copy(x_ref, r, xs_ref, dest_ref[base + r * TOP_K + k], sem).start()
        return carry

    lax.fori_loop(0, TM, start, 0, unroll=4)
    _wait_tiles(TOP_K, x_ref, xs_ref.at[pl.ds(0, TM)], sem)


def _dispatch(dest_flat, tile_end, xt):
    return pl.pallas_call(
        _dispatch_kernel,
        out_shape=jax.ShapeDtypeStruct((R_PAD, D // LANES, LANES), F32),
        grid_spec=pltpu.PrefetchScalarGridSpec(
            num_scalar_prefetch=2, grid=(NT,),
            in_specs=[pl.BlockSpec((TM, D // LANES, LANES), lambda i, d, te: (i, 0, 0))],
            out_specs=pl.BlockSpec(memory_space=pl.ANY),
            scratch_shapes=[pltpu.VMEM((TMX, D // LANES, LANES), F32), pltpu.SemaphoreType.DMA(())]),
        compiler_params=_params(("arbitrary",)),
        name="moe_dispatch",
    )(dest_flat, tile_end, xt)


W_PARTS = 4


def _expert_weights(i, nt, te_ref, w_ref, wbuf_ref, wsem, w16_ref, group_ref):
    rows = w_ref.shape[1] // W_PARTS

    def fetch(e, buf):
        return [pltpu.make_async_copy(w_ref.at[e, pl.ds(p * rows, rows)], wbuf_ref.at[buf, pl.ds(p * rows, rows)],
                                      wsem.at[buf]) for p in range(W_PARTS)]

    @pl.when(i == 0)
    def _():
        group_ref[0] = 0
        for c in fetch(te_ref[0], 0):
            c.start()

    first = jnp.logical_or(i == 0, te_ref[i] != te_ref[jnp.maximum(i - 1, 0)])

    @pl.when(jnp.logical_and(first, i < nt))
    def _():
        cur = group_ref[0] % 2
        nxt = lax.while_loop(
            lambda j: jnp.logical_and(j < nt, te_ref[jnp.minimum(j, MOE_TILES - 1)] == te_ref[i]),
            lambda j: j + 1, i + 1)

        @pl.when(nxt < nt)
        def _():
            for c in fetch(te_ref[jnp.minimum(nxt, MOE_TILES - 1)], 1 - cur):
                c.start()

        for c in fetch(0, cur):
            c.wait()
        w16_ref[...] = wbuf_ref[cur].astype(BF16)
        group_ref[0] = group_ref[0] + 1


def _experts_kernel(te_ref, nt_ref, xs_ref, wgu_ref, bgu_ref, wdn_ref, bdn_ref, y_ref,
                    gu_buf, gu_sem, gu16_ref, gu_group, dn_buf, dn_sem, dn16_ref, dn_group):
    i = pl.program_id(0)
    nt = nt_ref[0]
    _expert_weights(i, nt, te_ref, wgu_ref, gu_buf, gu_sem, gu16_ref, gu_group)
    _expert_weights(i, nt, te_ref, wdn_ref, dn_buf, dn_sem, dn16_ref, dn_group)

    @pl.when(i < nt)
    def _():
        x16 = jnp.concatenate([xs_ref[:, c, :] for c in range(D // LANES)], axis=1).astype(BF16)
        y = bdn_ref[0]
        for h in range(2):
            gc = slice(h * HALF, (h + 1) * HALF)
            uc = slice(D + h * HALF, D + (h + 1) * HALF)
            g = jnp.dot(x16, gu16_ref[:, gc], preferred_element_type=F32) + bgu_ref[0, :, gc]
            u = jnp.dot(x16, gu16_ref[:, uc], preferred_element_type=F32) + bgu_ref[0, :, uc]
            gt = jnp.minimum(g, SWIGLU_LIMIT)
            up = jnp.clip(u, -SWIGLU_LIMIT, SWIGLU_LIMIT)
            act = ((up + 1.0) * gt * jax.nn.sigmoid(SWIGLU_ALPHA * gt)).astype(BF16)
            y = y + jnp.dot(act, dn16_ref[gc, :], preferred_element_type=F32)
        y_ref[...] = y

    @pl.when(i >= nt)
    def _():
        y_ref[...] = jnp.zeros_like(y_ref)


def _tile_clamped(i, te, nt):
    return (jnp.minimum(i, jnp.maximum(nt[0] - 1, 0)), 0)


def _weight_scratch(n_out):
    return [pltpu.VMEM((2, D, n_out), F32), pltpu.SemaphoreType.DMA((2,)), pltpu.VMEM((D, n_out), BF16),
            pltpu.SMEM((1,), I32)]


def _experts(te, n_tiles, xs, w_gu, b_gu, w_dn, b_dn):
    return pl.pallas_call(
        _experts_kernel,
        out_shape=jax.ShapeDtypeStruct((R_PAD, D), F32),
        grid_spec=pltpu.PrefetchScalarGridSpec(
            num_scalar_prefetch=2, grid=(MOE_TILES,),
            in_specs=[pl.BlockSpec((TMX, D // LANES, LANES), lambda i, te, nt: _tile_clamped(i, te, nt) + (0,)),
                      pl.BlockSpec(memory_space=pl.ANY),
                      pl.BlockSpec((1, 1, 2 * D), lambda i, te, nt: (te[i], 0, 0)),
                      pl.BlockSpec(memory_space=pl.ANY),
                      pl.BlockSpec((1, 1, D), lambda i, te, nt: (te[i], 0, 0))],
            out_specs=pl.BlockSpec((TMX, D), lambda i, te, nt: (i, 0)),
            scratch_shapes=_weight_scratch(2 * D) + _weight_scratch(D)),
        compiler_params=_params(("arbitrary",), 58),
        name="moe_experts",
    )(te, n_tiles, xs, w_gu, b_gu, w_dn, b_dn)


def _combine_kernel(dest_ref, x_ref, tg_ref, mod_ref, ys_ref, o_ref, buf_ref, sem):
    i = pl.program_id(0)
    slot = i % 2

    def gather(tile, b):
        base = tile * (TM * TOP_K)

        def body(r, carry):
            for k in range(TOP_K):
                _row_copy(ys_ref, dest_ref[base + r * TOP_K + k], buf_ref.at[b, k], r, sem.at[b]).start()
            return carry

        lax.fori_loop(0, TM, body, 0, unroll=4)

    @pl.when(i == 0)
    def _():
        gather(0, 0)

    @pl.when(i + 1 < NT)
    def _():
        gather(i + 1, 1 - slot)

    _wait_tiles(TOP_K, ys_ref.at[pl.ds(0, TM)], buf_ref.at[slot, 0], sem.at[slot])
    tg = tg_ref[...]
    f = tg[:, 0:1] * buf_ref[slot, 0]
    for k in range(1, TOP_K):
        f = f + tg[:, k:k + 1] * buf_ref[slot, k]
    o_ref[...] = x_ref[...] + mod_ref[0][:, 5 * D:6 * D] * f


def _combine(dest_flat, x, tg, mod, ys):
    row = lambda i, d: (i, 0)
    return pl.pallas_call(
        _combine_kernel,
        out_shape=jax.ShapeDtypeStruct((T, D), F32),
        grid_spec=pltpu.PrefetchScalarGridSpec(
            num_scalar_prefetch=1, grid=(NT,),
            in_specs=[pl.BlockSpec((TM, D), row), pl.BlockSpec((TM, LANES), row),
                      pl.BlockSpec((1, 1, N_MOD * D), lambda i, d: (_group(i), 0, 0)),
                      pl.BlockSpec(memory_space=pl.ANY)],
            out_specs=pl.BlockSpec((TM, D), row),
            scratch_shapes=[pltpu.VMEM((2, TOP_K, TM, D), F32), pltpu.SemaphoreType.DMA((2,))]),
        compiler_params=_params(("arbitrary",), 40),
        name="moe_combine",
    )(dest_flat, x, tg, mod, ys)


def _moe(layer, y, xt, ti, tg, rk, cnt, mod, w_gu, b_gu, w_dn, b_dn):
    dest, te, nt = _route(cnt, ti, rk)
    dest_flat = dest[:, :TOP_K].reshape(-1)
    te = te[:MOE_TILES, 0] + layer * N_EXPERTS
    n_tiles = nt[0, :1]
    xs = _dispatch(dest_flat, nt[1, :N_EXPERTS], xt)
    n_all = w_gu.shape[0] * N_EXPERTS
    ys = _experts(te, n_tiles, xs, w_gu.reshape(n_all, D, 2 * D), b_gu.reshape(n_all, 1, 2 * D),
                  w_dn.reshape(n_all, D, D), b_dn.reshape(n_all, 1, D))
    return _combine(dest_flat, y, tg, mod, ys)


def _ones_blockdiag():
    idx = np.arange(HALF) // HEAD
    return jnp.asarray((idx[:, None] == idx[None, :]).astype(np.float32), dtype=BF16)


def _rope_tables():
    pos = jnp.arange(L_SAMPLE)
    rowp = (pos // 64).astype(F32)
    colp = (pos % 64).astype(F32)
    nf = HEAD // 4
    inv = jnp.power(10000.0, -jnp.arange(nf, dtype=F32) / nf)
    ar = rowp[:, None] * inv[None, :]
    ac = colp[:, None] * inv[None, :]
    cos64 = jnp.concatenate([jnp.cos(ar), jnp.cos(ar), jnp.cos(ac), jnp.cos(ac)], axis=1)
    sin64 = jnp.concatenate([-jnp.sin(ar), jnp.sin(ar), -jnp.sin(ac), jnp.sin(ac)], axis=1)
    cos = jnp.tile(cos64, (1, HALF // HEAD))
    sin = jnp.tile(sin64, (1, HALF // HEAD))
    ident = jnp.ones((TM, HALF), F32)
    return (jnp.concatenate([ident, cos], axis=0), jnp.concatenate([jnp.zeros((TM, HALF), F32), sin], axis=0))


def _bd_pairs(s):
    lead = s.shape[:-3]
    s = s.reshape(lead + (4, 2, HEAD, HEAD))
    z = jnp.zeros_like(s[..., 0, :, :])
    top = jnp.concatenate([s[..., 0, :, :], z], axis=-1)
    bot = jnp.concatenate([z, s[..., 1, :, :]], axis=-1)
    return jnp.concatenate([top, bot], axis=-2)


def _bd_unpairs(s):
    a = s[..., 0:HEAD, 0:HEAD]
    b = s[..., HEAD:, HEAD:]
    out = jnp.stack([a, b], axis=-3)
    return out.reshape(s.shape[:-3] + (8, HEAD, HEAD))


def kernel(x_prompt, x_sample, state_rwkv, cache_k_diff, cache_v_diff, state_retention, c, c_ctx, norm_g, ada_w, ada_b, e_w_in, e_w_out, sgu_ln_g, sgu_w, sgu_b, rw_mu, rw_w0, rw_w_up, rw_a0, rw_a_up, rw_g_up, rw_k_k, rw_k_a, rw_r_k, rw_gn_g, rw_gn_b, o_w_in, o_w_out, da_qk_g, da_lam, da_subln_g, ret_gn_g, router_w, router_b, ex_w_gu, ex_b_gu, ex_w_dn, ex_b_dn):
    x = jnp.concatenate([x_prompt.reshape(T_PROMPT, D), x_sample.reshape(T_SAMPLE, D)], axis=0)
    cvec8 = jnp.concatenate([c_ctx[None, :], c, jnp.zeros((3, D), F32)], axis=0)
    mods = _adaln(cvec8, ada_w, ada_b)
    mod0 = mods[0].reshape(8, 1, N_MOD * D)
    mod1 = mods[1].reshape(8, 1, N_MOD * D)
    ones_bd = _ones_blockdiag()
    rw_pad = jnp.pad(router_w, ((0, 0), (0, 0), (0, LANES - N_EXPERTS)))
    rb_pad = jnp.pad(router_b, ((0, 0), (0, LANES - N_EXPERTS))).reshape(2, 1, LANES)
    row = lambda i: (i, 0)
    half = pl.BlockSpec((TM, HALF), row)

    za, zb = _in_proj(x, norm_g[0, 0], mod0, e_w_in[0].astype(BF16), (2 * HALF, B_COLS))
    bs_full = jnp.repeat(sgu_b[0].T, HEAD, axis=1)
    a_out = _sgu(za, sgu_ln_g[0], sgu_w[0].astype(BF16), bs_full)
    zpad = jnp.zeros((2, HEAD, HALF), F32)
    wup_pad = jnp.concatenate([rw_w_up[0], zpad], axis=1).astype(BF16)
    aup_pad = jnp.concatenate([zpad, rw_a_up[0]], axis=1).astype(BF16)
    r, v, kkn, bonus, g, lw, kt, b = _rwkv_prep(zb, rw_mu[0], rw_k_k[0], rw_k_a[0], rw_r_k[0], rw_w0[0], rw_a0[0],
                                                wup_pad, aup_pad, rw_g_up[0].astype(BF16), ones_bd)
    s0_sample = _bd_pairs(jnp.moveaxis(state_rwkv[:, 0], 1, 0))
    s0_rw = jnp.concatenate([jnp.zeros((2, N_PROMPT, 4, LANES, LANES), F32), s0_sample], axis=1)
    yf_rw, yb_rw, sfin_rw = _rwkv_scan(r, v, kkn, lw, kt, b, s0_rw)
    new_rwkv = jnp.moveaxis(_bd_unpairs(sfin_rw[:, :N_PROMPT]), 0, 1)[:, None]
    y0, xp0, ti0, tg0, rk0, cnt0 = _out_proj(
        True,
        [a_out, yf_rw, yb_rw, bonus, g, rw_gn_g[0].reshape(1, HALF), rw_gn_b[0].reshape(1, HALF), ones_bd],
        [half, half, half, half, half, _full((1, HALF)), _full((1, HALF)), _full((HALF, HALF))],
        x, mod0, norm_g[0, 1], e_w_out[0].astype(BF16), rw_pad[0], rb_pad[0])
    x1 = _moe(0, y0, xp0, ti0, tg0, rk0, cnt0, mod0, ex_w_gu, ex_b_gu, ex_w_dn, ex_b_dn)

    zc, zr = _in_proj(x1, norm_g[1, 0], mod1, o_w_in[0].astype(BF16), (3 * HALF, 3 * HALF))
    cos_tab, sin_tab = _rope_tables()
    qkg = jnp.tile(da_qk_g[0], (1, HALF // HEAD))
    cq, ck, ck_raw, rqk = _odd_prep(zc, zr, cos_tab, sin_tab, qkg, ones_bd)
    lambda_init = 0.8 - 0.6 * math.exp(-0.3 * 1)
    lv = da_lam[0]
    lam = jnp.exp(jnp.sum(lv[0] * lv[1])) - jnp.exp(jnp.sum(lv[2] * lv[3])) + lambda_init
    c_out = _attn(cq, ck, zc, 0, N_PROMPT, L_PROMPT, lam, da_subln_g[0], 1.0 - lambda_init,
                  jnp.zeros((T, HALF), F32))
    ctx_k = cache_k_diff[:, 0].reshape(N_SAMPLE, PAST, HALF)
    ctx_v = cache_v_diff[:, 0].reshape(N_SAMPLE, PAST, HALF)
    c_out = _attn(cq, ck, zc, T_PROMPT, N_SAMPLE, L_SAMPLE, lam, da_subln_g[0], 1.0 - lambda_init,
                  c_out, ctx_k, ctx_v)
    sr = jnp.moveaxis(state_retention[:, 0], 1, 0)
    zr0 = jnp.zeros_like(sr)
    s0_sample = jnp.stack([jnp.concatenate([sr[:, :, 0], zr0[:, :, 0]], axis=-2),
                           jnp.concatenate([zr0[:, :, 1], sr[:, :, 1]], axis=-2),
                           jnp.concatenate([sr[:, :, 2], zr0[:, :, 2]], axis=-2),
                           jnp.concatenate([zr0[:, :, 3], sr[:, :, 3]], axis=-2)], axis=2)
    s0_ret = jnp.concatenate([jnp.zeros((2, N_PROMPT, 4, LANES, LANES), F32), s0_sample], axis=1)
    of_ret, ob_ret, rfin = _retention(rqk, zr, s0_ret)
    rfin_p = rfin[:, :N_PROMPT]
    new_ret = jnp.stack([rfin_p[:, :, 0, 0:HEAD], rfin_p[:, :, 1, HEAD:], rfin_p[:, :, 2, 0:HEAD],
                         rfin_p[:, :, 3, HEAD:]], axis=2)
    new_ret = jnp.moveaxis(new_ret, 0, 1)[:, None]
    y1, xp1, ti1, tg1, rk1, cnt1 = _out_proj(
        False,
        [c_out, of_ret, ob_ret, zr, ret_gn_g[0].reshape(1, HALF)],
        [half, half, half, pl.BlockSpec((TM, HALF), lambda i: (i, 2)), _full((1, HALF))],
        x1, mod1, norm_g[1, 1], o_w_out[0].astype(BF16), rw_pad[1], rb_pad[1])
    y_fin = _moe(1, y1, xp1, ti1, tg1, rk1, cnt1, mod1, ex_w_gu, ex_b_gu, ex_w_dn, ex_b_dn)

    new_k = ck_raw[:T_PROMPT].reshape(N_PROMPT, 1, L_PROMPT, 4, LANES)
    new_v = zc[:T_PROMPT, 2 * HALF:3 * HALF].reshape(N_PROMPT, 1, L_PROMPT, 4, LANES)
    return (y_fin[:T_PROMPT].reshape(N_PROMPT, L_PROMPT, D), y_fin[T_PROMPT:].reshape(N_SAMPLE, L_SAMPLE, D),
            new_rwkv, new_k, new_v, new_ret)
```

```python
import functools
import math

import numpy as np
import jax
import jax.numpy as jnp
from jax import lax
from jax.experimental import pallas as pl
from jax.experimental.pallas import tpu as pltpu

F32 = jnp.float32
BF16 = jnp.bfloat16
I32 = jnp.int32

D = 1024
N_PROMPT, L_PROMPT = 16, 256
N_SAMPLE, L_SAMPLE = 4, 1024
N_SEQ = N_PROMPT + N_SAMPLE
PAST = 256
T_PROMPT = N_PROMPT * L_PROMPT
T_SAMPLE = N_SAMPLE * L_SAMPLE
T = T_PROMPT + T_SAMPLE
TM = 256
NT = T // TM
PROMPT_TILES = T_PROMPT // TM
TILES_PER_SAMPLE = L_SAMPLE // TM
N_MOD = 6
HALF = 512
B_COLS = 1792
HEAD = 64
W_DECAY_SCALE = math.exp(-0.5)
RWKV_GN_EPS = 64e-5
RW_CHUNK = 64
RET_CHUNK = 128
RET_EXP = ((5.0, 7.0, 9.0, 11.0), (6.0, 8.0, 10.0, 12.0))
N_EXPERTS = 32
TOP_K = 4
SWIGLU_LIMIT = 7.0
SWIGLU_ALPHA = 1.702
N_ASSIGN = T * TOP_K
TMX = 512
MOE_TILES = N_ASSIGN // TMX + N_EXPERTS
R_PAD = MOE_TILES * TMX
LANES = 128

NN = (((1,), (0,)), ((), ()))
NT_DIMS = (((1,), (1,)), ((), ()))
TN = (((0,), (0,)), ((), ()))


def _group(i):
    return jnp.where(i < PROMPT_TILES, 0, 1 + (i - PROMPT_TILES) // TILES_PER_SAMPLE)


def _mm(a, b, dims=NN, passes=1):
    dg = functools.partial(lax.dot_general, dimension_numbers=dims, preferred_element_type=F32)
    if passes == 1:
        return dg(a.astype(BF16), b.astype(BF16))
    a = a.astype(F32)
    b = b.astype(F32)
    ah = a.astype(BF16)
    al = (a - ah.astype(F32)).astype(BF16)
    bh = b.astype(BF16)
    if passes == 2:
        assert dims[0][0] == (1,)
        m = a.shape[0]
        both = dg(jnp.concatenate([ah, al], axis=0), bh)
        return both[0:m] + both[m:2 * m]
    bl = (b - bh.astype(F32)).astype(BF16)
    if dims[0][0] == (1,):
        m = a.shape[0]
        both = dg(jnp.concatenate([ah, al], axis=0), bh)
        return both[0:m] + (dg(ah, bl) + both[m:2 * m])
    return dg(ah, bh) + (dg(ah, bl) + dg(al, bh))


def _group_sum(x, ones_bd):
    xh = x.astype(BF16)
    xl = (x - xh.astype(F32)).astype(BF16)
    return (jnp.dot(xh, ones_bd, preferred_element_type=F32)
            + jnp.dot(xl, ones_bd, preferred_element_type=F32))


def _full(shape):
    nd = len(shape)
    return pl.BlockSpec(shape, lambda *_: (0,) * nd)


def _params(sem, vmem_mb=None):
    kw = {}
    if vmem_mb is not None:
        kw["vmem_limit_bytes"] = vmem_mb * 1024 * 1024
    return pltpu.CompilerParams(dimension_semantics=sem, **kw)


def _seq_tables(chunk):
    blk_f, blk_b, first, last, seq = [], [], [], [], []
    row = 0
    for s in range(N_SEQ):
        n = (L_PROMPT if s < N_PROMPT else L_SAMPLE) // chunk
        base = row // chunk
        for j in range(n):
            blk_f.append(base + j)
            blk_b.append(base + n - 1 - j)
            first.append(int(j == 0))
            last.append(int(j == n - 1))
            seq.append(s)
        row += n * chunk
    return tuple(np.asarray(a, np.int32) for a in (blk_f, blk_b, first, last, seq))


def _adaln_kernel(c_ref, w_ref, b_ref, o_ref):
    c = c_ref[...]
    s = c * jax.nn.sigmoid(c)
    o_ref[0] = _mm(s, w_ref[0], NN, 3) + b_ref[0]


def _adaln(cvec8, ada_w, ada_b):
    depth, _, n = ada_w.shape
    bn = 1536
    return pl.pallas_call(
        _adaln_kernel,
        out_shape=jax.ShapeDtypeStruct((depth, 8, n), F32),
        grid=(depth, n // bn),
        in_specs=[pl.BlockSpec((8, D), lambda l, j: (0, 0)),
                  pl.BlockSpec((1, D, bn), lambda l, j: (l, 0, j)),
                  pl.BlockSpec((1, 1, bn), lambda l, j: (l, 0, j))],
        out_specs=pl.BlockSpec((1, 8, bn), lambda l, j: (l, 0, j)),
        compiler_params=_params(("arbitrary", "arbitrary"), 40),
        name="adaln",
    )(cvec8, ada_w, ada_b.reshape(depth, 1, n))


def _in_kernel(x_ref, g_ref, mod_ref, w_ref, *outs, splits):
    x = x_ref[...]
    mod = mod_ref[0]
    y = x * lax.rsqrt(jnp.mean(x * x, axis=-1, keepdims=True) + 1e-6) * g_ref[...]
    h = (y * (1.0 + mod[:, D:2 * D]) + mod[:, 0:D]).astype(BF16)
    off = 0
    for o_ref, n in zip(outs, splits):
        o_ref[...] = jnp.dot(h, w_ref[:, off:off + n], preferred_element_type=F32)
        off += n


def _in_proj(x, g, mod, w_bf16, splits):
    n = w_bf16.shape[1]
    row = lambda i: (i, 0)
    return pl.pallas_call(
        functools.partial(_in_kernel, splits=splits),
        out_shape=[jax.ShapeDtypeStruct((T, s), F32) for s in splits],
        grid=(NT,),
        in_specs=[pl.BlockSpec((TM, D), row), _full((1, D)),
                  pl.BlockSpec((1, 1, N_MOD * D), lambda i: (_group(i), 0, 0)), _full((D, n))],
        out_specs=[pl.BlockSpec((TM, s), row) for s in splits],
        compiler_params=_params(("arbitrary",), 48),
        name="in_proj",
    )(x, g.reshape(1, D), mod, w_bf16)


def _gelu(x):
    return 0.5 * x * (1.0 + lax.erf(x * (1.0 / math.sqrt(2.0))))


def _sgu_kernel(za_ref, lng_ref, ws_ref, bs_ref, o_ref):
    u = _gelu(za_ref[:, 0:HALF])
    va = _gelu(za_ref[:, HALF:2 * HALF])
    mu = jnp.mean(va, axis=-1, keepdims=True)
    dv = va - mu
    var = jnp.mean(dv * dv, axis=-1, keepdims=True)
    vn = dv * lax.rsqrt(var + 1e-5) * lng_ref[...]
    lane = lax.broadcasted_iota(I32, (LANES, LANES), 1)
    first = lane < HEAD
    for c in range(TM // LANES):
        rows = slice(c * LANES, (c + 1) * LANES)
        for p in range(HALF // LANES):
            cols = slice(p * LANES, (p + 1) * LANES)
            vp = vn[rows, cols]
            s = (jnp.dot(ws_ref[2 * p], jnp.where(first, vp, 0.0).astype(BF16), preferred_element_type=F32)
                 + jnp.dot(ws_ref[2 * p + 1], jnp.where(first, 0.0, vp).astype(BF16), preferred_element_type=F32))
            o_ref[rows, cols] = u[rows, cols] * (s + bs_ref[:, cols])


def _sgu(za, ln_g, w_s_bf16, bs_full):
    return pl.pallas_call(
        _sgu_kernel,
        out_shape=jax.ShapeDtypeStruct((T, HALF), F32),
        grid=(NT,),
        in_specs=[pl.BlockSpec((TM, 2 * HALF), lambda i: (i, 0)), _full((1, HALF)),
                  _full((8, LANES, LANES)), _full((LANES, HALF))],
        out_specs=pl.BlockSpec((TM, HALF), lambda i: (i, 0)),
        compiler_params=_params(("arbitrary",)),
        name="sgu",
    )(za, ln_g.reshape(1, HALF), w_s_bf16, bs_full)


def _rwkv_prep_kernel(zb_ref, zp_ref, zn_ref, mu_ref, kk_ref, ka_ref, rk_ref, w0_ref, a0_ref,
                      wup_ref, aup_ref, gup_ref, ones_ref,
                      r_ref, v_ref, kkn_ref, bonus_ref, g_ref, lw_ref, kt_ref, b_ref):
    i = pl.program_id(0)
    in_sample = i >= PROMPT_TILES
    pos = (i - PROMPT_TILES) % TILES_PER_SAMPLE
    is_first = jnp.logical_or(jnp.logical_not(in_sample), pos == 0)
    is_last = jnp.logical_or(jnp.logical_not(in_sample), pos == TILES_PER_SAMPLE - 1)
    zb = zb_ref[...]
    prev_row = jnp.where(is_first, 0.0, zp_ref[7:8, :])
    next_row = jnp.where(is_last, 0.0, zn_ref[0:1, :])
    rowid = lax.broadcasted_iota(I32, (TM, 1), 0)
    zp = jnp.where(rowid == 0, prev_row, pltpu.roll(zb, 1, 0))
    zn = jnp.where(rowid == TM - 1, next_row, pltpu.roll(zb, TM - 1, 0))
    zs = zb + mu_ref[0:1, :] * (zp - zb) + mu_ref[1:2, :] * (zn - zb)
    r = zs[:, 0:HALF]
    k = zs[:, HALF:2 * HALF]
    v = zs[:, 2 * HALF:3 * HALF]
    wa = zs[:, 3 * HALF:3 * HALF + LANES]
    gd = zs[:, 3 * HALF + LANES:B_COLS]
    ones_bd = ones_ref[...]
    r_ref[...] = r
    v_ref[...] = v
    g_ref[...] = jnp.dot(jax.nn.sigmoid(gd).astype(BF16), gup_ref[...], preferred_element_type=F32)
    kk = k * kk_ref[...]
    kkn = kk / jnp.maximum(jnp.sqrt(_group_sum(kk * kk, ones_bd)), 1e-6)
    kkn_ref[...] = kkn
    bonus_ref[...] = _group_sum(r * k * rk_ref[...], ones_bd) * v
    tw = jnp.tanh(wa).astype(BF16)
    wa16 = wa.astype(BF16)
    for dd in range(2):
        lw_ref[dd] = -W_DECAY_SCALE * jax.nn.sigmoid(
            w0_ref[dd:dd + 1, :] + jnp.dot(tw, wup_ref[dd], preferred_element_type=F32))
        a = jax.nn.sigmoid(a0_ref[dd:dd + 1, :] + jnp.dot(wa16, aup_ref[dd], preferred_element_type=F32))
        kt_ref[dd] = k * (1.0 + (a - 1.0) * ka_ref[...])
        b_ref[dd] = a * kkn


def _rwkv_prep(zb, mu, k_k, k_a, r_k, w0, a0, wup_pad, aup_pad, g_up, ones_bd):
    row = lambda i: (i, 0)
    halo = TM // 8
    one = jax.ShapeDtypeStruct((T, HALF), F32)
    two = jax.ShapeDtypeStruct((2, T, HALF), F32)
    o1 = pl.BlockSpec((TM, HALF), row)
    o2 = pl.BlockSpec((2, TM, HALF), lambda i: (0, i, 0))
    return pl.pallas_call(
        _rwkv_prep_kernel,
        out_shape=[one, one, one, one, one, two, two, two],
        grid=(NT,),
        in_specs=[pl.BlockSpec((TM, B_COLS), row),
                  pl.BlockSpec((8, B_COLS), lambda i: (jnp.maximum(i * halo - 1, 0), 0)),
                  pl.BlockSpec((8, B_COLS), lambda i: (jnp.minimum((i + 1) * halo, T // 8 - 1), 0)),
                  _full((2, B_COLS)), _full((1, HALF)), _full((1, HALF)), _full((1, HALF)),
                  _full((2, HALF)), _full((2, HALF)),
                  _full((2, LANES, HALF)), _full((2, LANES, HALF)), _full((LANES, HALF)),
                  _full((HALF, HALF))],
        out_specs=[o1, o1, o1, o1, o1, o2, o2, o2],
        compiler_params=_params(("arbitrary",), 48),
        name="rwkv_prep",
    )(zb, zb, zb, mu, k_k.reshape(1, HALF), k_a.reshape(1, HALF), r_k.reshape(1, HALF), w0, a0,
      wup_pad, aup_pad, g_up, ones_bd)


def _rwkv_chunks(dirs):
    C = RW_CHUNK
    ti = lax.broadcasted_iota(I32, (C, C), 0)
    tj = lax.broadcasted_iota(I32, (C, C), 1)
    bi = lax.broadcasted_iota(I32, (LANES, LANES), 0)
    bj = lax.broadcasted_iota(I32, (LANES, LANES), 1)
    same = (bi >> 6) == (bj >> 6)
    pi = bi & (C - 1)
    pj = bj & (C - 1)
    eye = (bi == bj).astype(F32)
    h0 = lax.broadcasted_iota(I32, (C, LANES), 1) < HEAD

    def stack(x):
        return jnp.concatenate([jnp.where(h0, x, 0.0), jnp.where(h0, 0.0, x)], axis=0)

    def fold(x):
        return x[0:C] + x[C:2 * C]

    chains = []
    for rev, r, v, kk, lw, kt, b, s_ref, y_ref in dirs:
        tri = jnp.where((tj >= ti) if rev else (tj <= ti), 1.0, 0.0).astype(F32)
        p1 = lw.astype(BF16)
        r1 = lw - p1.astype(F32)
        p2 = r1.astype(BF16)
        p3 = (r1 - p2.astype(F32)).astype(BF16)
        cs3 = jnp.dot(tri.astype(BF16), jnp.concatenate([p1, p2, p3], axis=1), preferred_element_type=F32)
        cs = cs3[:, 0:HALF] + (cs3[:, HALF:2 * HALF] + cs3[:, 2 * HALF:3 * HALF])
        ctot = cs[0:1, :] if rev else cs[C - 1:C, :]
        e_neg = jnp.exp(-cs)
        e_tail = jnp.exp(ctot - cs)
        q1 = kk * jnp.exp(cs - lw)
        k1 = kt * e_neg
        b1 = b * e_neg
        r1 = r * jnp.exp(cs)
        k2 = kt * e_tail
        b2 = b * e_tail
        e_tot = jnp.exp(ctot)
        strict = jnp.logical_and(same, (pj > pi) if rev else (pj < pi))
        incl = jnp.logical_and(same, (pj >= pi) if rev else (pj <= pi))
        for p in range(HALF // LANES):
            cols = slice(p * LANES, (p + 1) * LANES)
            chains.append(dict(p=p, cols=cols, strict=strict, incl=incl, s_ref=s_ref, y_ref=y_ref,
                               q1=q1[:, cols], k1=k1[:, cols], b1=b1[:, cols], r1=r1[:, cols],
                               k2=k2[:, cols], b2=b2[:, cols], v=v[:, cols], e_tot=e_tot[:, cols]))

    for ch in chains:
        lhs = jnp.concatenate([stack(ch["q1"]), stack(ch["r1"])], axis=0)
        rhs = jnp.concatenate([ch["k1"], ch["k1"], ch["b1"], ch["b1"]], axis=0)
        gm = _mm(lhs, rhs, NT_DIMS, 2)
        ch["mk"] = jnp.where(ch["strict"], gm[0:2 * C, 0:2 * C], 0.0)
        ch["mb"] = jnp.where(ch["strict"], gm[0:2 * C, 2 * C:4 * C], 0.0)
        ch["nk"] = jnp.where(ch["incl"], gm[2 * C:4 * C, 0:2 * C], 0.0)
        ch["nb"] = jnp.where(ch["incl"], gm[2 * C:4 * C, 2 * C:4 * C], 0.0)
        ch["tinv"] = eye - jnp.where((pi >> 1) == (pj >> 1), ch["mb"], 0.0)
    size = 2
    while size < C:
        sh = size.bit_length() - 1
        blk = jnp.logical_and((pi >> (sh + 1)) == (pj >> (sh + 1)), (pi >> sh) != (pj >> sh))
        for ch in chains:
            ch["tn"] = _mm(ch["tinv"], jnp.where(blk, ch["mb"], 0.0), NN, 2)
        for ch in chains:
            ch["tinv"] = ch["tinv"] - _mm(ch["tn"], ch["tinv"], NN, 2)
        size *= 2
    for ch in chains:
        vst = stack(ch["v"])
        ch["mkv"] = fold(_mm(ch["mk"], vst, NN, 2))
        ch["nkv"] = fold(_mm(ch["nk"], vst, NN, 1))
        ch["s"] = ch["s_ref"][ch["p"]]
        ch["qr"] = _mm(jnp.concatenate([ch["q1"], ch["r1"]], axis=0), ch["s"], NT_DIMS, 2)
    for ch in chains:
        ch["u"] = fold(_mm(ch["tinv"], stack(ch["mkv"] + ch["qr"][0:C]), NN, 2))
    for ch in chains:
        ch["y_ref"][:, ch["cols"]] = ch["qr"][C:2 * C] + ch["nkv"] - fold(_mm(ch["nb"], stack(ch["u"]), NN, 1))
        upd = _mm(jnp.concatenate([ch["v"], ch["u"]], axis=0),
                  jnp.concatenate([ch["k2"], -ch["b2"]], axis=0), TN, 3)
        ch["s_ref"][ch["p"]] = ch["s"] * ch["e_tot"] + jnp.where(same, upd, 0.0)


def _rwkv_scan_kernel(bf_ref, bb_ref, first_ref, last_ref, seq_ref,
                      rf_ref, vf_ref, kkf_ref, lwf_ref, ktf_ref, bfw_ref,
                      rb_ref, vb_ref, kkb_ref, lwb_ref, ktb_ref, bbw_ref, s0_ref,
                      yf_ref, yb_ref, sfin_ref, s_ref):
    step = pl.program_id(0)

    @pl.when(first_ref[step] == 1)
    def _():
        s_ref[...] = s0_ref[:, 0]

    _rwkv_chunks([
        (False, rf_ref[...], vf_ref[...], kkf_ref[...], lwf_ref[0], ktf_ref[0], bfw_ref[0], s_ref.at[0], yf_ref),
        (True, rb_ref[...], vb_ref[...], kkb_ref[...], lwb_ref[0], ktb_ref[0], bbw_ref[0], s_ref.at[1], yb_ref)])

    @pl.when(last_ref[step] == 1)
    def _():
        sfin_ref[:, 0] = s_ref[...]


def _rwkv_scan(r, v, kk, lw, kt, b, s0_bd):
    C = RW_CHUNK
    tabs = _seq_tables(C)
    fwd = lambda i, bf, bb, fi, la, sq: (bf[i], 0)
    bwd = lambda i, bf, bb, fi, la, sq: (bb[i], 0)
    fwd3 = lambda i, bf, bb, fi, la, sq: (0, bf[i], 0)
    bwd3 = lambda i, bf, bb, fi, la, sq: (1, bb[i], 0)
    st = pl.BlockSpec((2, 1, 4, LANES, LANES), lambda i, bf, bb, fi, la, sq: (0, sq[i], 0, 0, 0))
    one_f, one_b = pl.BlockSpec((C, HALF), fwd), pl.BlockSpec((C, HALF), bwd)
    two_f, two_b = pl.BlockSpec((1, C, HALF), fwd3), pl.BlockSpec((1, C, HALF), bwd3)
    return pl.pallas_call(
        _rwkv_scan_kernel,
        out_shape=[jax.ShapeDtypeStruct((T, HALF), F32), jax.ShapeDtypeStruct((T, HALF), F32),
                   jax.ShapeDtypeStruct((2, N_SEQ, 4, LANES, LANES), F32)],
        grid_spec=pltpu.PrefetchScalarGridSpec(
            num_scalar_prefetch=5, grid=(len(tabs[0]),),
            in_specs=[one_f, one_f, one_f, two_f, two_f, two_f,
                      one_b, one_b, one_b, two_b, two_b, two_b, st],
            out_specs=[one_f, one_b, st],
            scratch_shapes=[pltpu.VMEM((2, 4, LANES, LANES), F32)]),
        compiler_params=_params(("arbitrary",)),
        name="rwkv_scan",
    )(*tabs, r, v, kk, lw, kt, b, r, v, kk, lw, kt, b, s0_bd)


def _rope(x, cos, sin_signed, first16):
    w = x.shape[1]
    partner = jnp.where(first16, pltpu.roll(x, w - 16, 1), pltpu.roll(x, 16, 1))
    return x * cos + partner * sin_signed


def _odd_prep_kernel(zc_ref, zr_ref, cos_ref, sin_ref, qkg_ref, ones_ref, cq_ref, ck_ref, ckraw_ref, rqk_ref):
    ones_bd = ones_ref[...]
    cos = cos_ref[...]
    sin = sin_ref[...]
    lane = lax.broadcasted_iota(I32, (TM, HALF), 1)
    first16 = (lane & 31) < 16
    for idx, (o_ref, raw_ref) in enumerate(((cq_ref, None), (ck_ref, ckraw_ref))):
        x = zc_ref[:, idx * HALF:(idx + 1) * HALF]
        ms = _group_sum(x * x, ones_bd) * (1.0 / HEAD)
        xn = x * lax.rsqrt(ms + 1e-6) * qkg_ref[idx:idx + 1, :]
        if raw_ref is not None:
            raw_ref[...] = xn
        o_ref[...] = _rope(xn, cos, sin, first16)
    rqk = _rope(zr_ref[...], cos, sin, first16)
    rqk_ref[...] = jnp.where(lane < HALF // 2, rqk * (HEAD ** -0.5), rqk)


def _odd_prep(zc, zr, cos_tab, sin_tab, qkg_tiled, ones_bd):
    row = lambda i: (i, 0)
    tab = lambda i: (jnp.where(i < PROMPT_TILES, 0, 1 + (i - PROMPT_TILES) % TILES_PER_SAMPLE), 0)
    one = jax.ShapeDtypeStruct((T, HALF), F32)
    o1 = pl.BlockSpec((TM, HALF), row)
    return pl.pallas_call(
        _odd_prep_kernel,
        out_shape=[one, one, one, one], grid=(NT,),
        in_specs=[pl.BlockSpec((TM, 2 * HALF), row), pl.BlockSpec((TM, HALF), row),
                  pl.BlockSpec((TM, HALF), tab), pl.BlockSpec((TM, HALF), tab),
                  _full((2, HALF)), _full((HALF, HALF))],
        out_specs=[o1, o1, o1, o1],
        compiler_params=_params(("arbitrary",)),
        name="odd_prep",
    )(zc, zr, cos_tab, sin_tab, qkg_tiled, ones_bd)


def _attn_kernel(*refs, has_ctx, one_minus_li):
    if has_ctx:
        q_ref, k_ref, v_ref, kc_ref, vc_ref, lam_ref, sg_ref, _, o_ref = refs
    else:
        q_ref, k_ref, v_ref, lam_ref, sg_ref, _, o_ref = refs
    lam = lam_ref[...]
    lane = lax.broadcasted_iota(I32, (LANES, LANES), 1)
    m0 = lane < HEAD
    scale = HEAD ** -0.5
    for h in range(4):
        cols = slice(h * LANES, (h + 1) * LANES)
        qp = q_ref[:, cols]
        segs = [(k_ref[:, cols], v_ref[:, cols])]
        if has_ctx:
            segs.append((kc_ref[0, :, cols], vc_ref[0, :, cols]))
        outs = []
        for qm in (jnp.where(m0, qp, 0.0), jnp.where(m0, 0.0, qp)):
            qm16 = qm.astype(BF16)
            ss = [lax.dot_general(qm16, ks.astype(BF16), NT_DIMS, preferred_element_type=F32) * scale
                  for ks, _ in segs]
            mx = ss[0].max(axis=-1, keepdims=True)
            for s_ in ss[1:]:
                mx = jnp.maximum(mx, s_.max(axis=-1, keepdims=True))
            ps = [jnp.exp(s_ - mx) for s_ in ss]
            den = ps[0].sum(axis=-1, keepdims=True)
            for p_ in ps[1:]:
                den = den + p_.sum(axis=-1, keepdims=True)
            outs.append([p_ / den for p_ in ps])
        acc = None
        for si, (_, vs) in enumerate(segs):
            amap = outs[0][si] - lam * outs[1][si]
            t = jnp.dot(amap.astype(BF16), vs.astype(BF16), preferred_element_type=F32)
            acc = t if acc is None else acc + t
        nrm = acc * lax.rsqrt(jnp.mean(acc * acc, axis=-1, keepdims=True) + 1e-6) * sg_ref[...]
        o_ref[:, cols] = nrm * one_minus_li


def _attn(cq, ck, zc, row0, n_seq, seq_len, lam, subln_g, one_minus_li, prev, ctx_k=None, ctx_v=None):
    nq = seq_len // LANES
    qb0 = row0 // LANES
    sb0 = row0 // seq_len
    in_specs = [pl.BlockSpec((LANES, HALF), lambda s, q: (qb0 + s * nq + q, 0)),
                pl.BlockSpec((seq_len, HALF), lambda s, q: (sb0 + s, 0)),
                pl.BlockSpec((seq_len, HALF), lambda s, q: (sb0 + s, 2))]
    args = [cq, ck, zc]
    if ctx_k is not None:
        in_specs += [pl.BlockSpec((1, PAST, HALF), lambda s, q: (s, 0, 0))] * 2
        args += [ctx_k, ctx_v]
    in_specs += [_full((1, 1)), _full((1, LANES))]
    args += [lam.reshape(1, 1), subln_g.reshape(1, LANES)]
    in_specs.append(pl.BlockSpec(memory_space=pl.ANY))
    args.append(prev)
    aliases = {len(args) - 1: 0}
    return pl.pallas_call(
        functools.partial(_attn_kernel, has_ctx=ctx_k is not None, one_minus_li=one_minus_li),
        out_shape=jax.ShapeDtypeStruct((T, HALF), F32),
        grid=(n_seq, nq), in_specs=in_specs,
        out_specs=pl.BlockSpec((LANES, HALF), lambda s, q: (qb0 + s * nq + q, 0)),
        input_output_aliases=aliases,
        compiler_params=_params(("arbitrary", "arbitrary"), 48),
        name="diff_attn",
    )(*args)


_LOG_GAMMA = tuple(tuple(float(np.log1p(-np.exp2(-np.float32(e)), dtype=np.float32)) for e in es)
                   for es in RET_EXP)


def _ret_chunks(dirs):
    C = RET_CHUNK
    ii = lax.broadcasted_iota(I32, (C, C), 0)
    jj = lax.broadcasted_iota(I32, (C, C), 1)
    ri = lax.broadcasted_iota(I32, (C, 1), 0)
    lane = lax.broadcasted_iota(I32, (C, LANES), 1)
    chains = []
    for rev, qk_ref, v_ref, s_ref, o_ref in dirs:
        mask = (jj > ii) if rev else (jj <= ii)
        dist = jnp.where(mask, (jj - ii) if rev else (ii - jj), 0).astype(F32)
        kpow = (ri if rev else (C - 1 - ri)).astype(F32)
        qpow = ((C - ri) if rev else (ri + 1)).astype(F32)
        for h in range(4):
            lg = _LOG_GAMMA[1 if rev else 0][h]
            p = h // 2
            hm = (lane < HEAD) if h % 2 == 0 else (lane >= HEAD)
            qp = jnp.where(hm, qk_ref[:, p * LANES:(p + 1) * LANES], 0.0)
            kp = jnp.where(hm, qk_ref[:, HALF // 2 + p * LANES:HALF // 2 + (p + 1) * LANES], 0.0)
            chains.append(dict(
                h=h, lg=lg, s_ref=s_ref, o_ref=o_ref, q16=qp.astype(BF16), k16=kp.astype(BF16),
                qw16=(qp * jnp.exp(lg * qpow)).astype(BF16), kw16=(kp * jnp.exp(lg * kpow)).astype(BF16),
                v16=v_ref[:, h * LANES:(h + 1) * LANES].astype(BF16),
                decay=jnp.where(mask, jnp.exp(lg * dist), 0.0)))
    for ch in chains:
        ch["sc"] = (lax.dot_general(ch["q16"], ch["k16"], NT_DIMS, preferred_element_type=F32)
                    * ch["decay"]).astype(BF16)
        ch["s"] = ch["s_ref"][ch["h"]]
    for ch in chains:
        ch["o"] = (jnp.dot(ch["sc"], ch["v16"], preferred_element_type=F32)
                   + jnp.dot(ch["qw16"], ch["s"].astype(BF16), preferred_element_type=F32))
        ch["kv"] = lax.dot_general(ch["kw16"], ch["v16"], TN, preferred_element_type=F32)
    for ch in chains:
        h = ch["h"]
        ch["o_ref"][:, h * LANES:(h + 1) * LANES] = ch["o"]
        ch["s_ref"][h] = math.exp(ch["lg"] * C) * ch["s"] + ch["kv"]


def _ret_kernel(bf_ref, bb_ref, first_ref, last_ref, seq_ref,
                qkf_ref, vf_ref, qkb_ref, vb_ref, s0_ref, of_ref, ob_ref, sfin_ref, s_ref):
    step = pl.program_id(0)

    @pl.when(first_ref[step] == 1)
    def _():
        s_ref[...] = s0_ref[:, 0]

    _ret_chunks([(False, qkf_ref, vf_ref, s_ref.at[0], of_ref), (True, qkb_ref, vb_ref, s_ref.at[1], ob_ref)])

    @pl.when(last_ref[step] == 1)
    def _():
        sfin_ref[:, 0] = s_ref[...]


def _retention(rqk, zr, s0):
    C = RET_CHUNK
    tabs = _seq_tables(C)
    st = pl.BlockSpec((2, 1, 4, LANES, LANES), lambda i, bf, bb, fi, la, sq: (0, sq[i], 0, 0, 0))
    spec = lambda use_b, col: pl.BlockSpec(
        (C, HALF), lambda i, bf, bb, fi, la, sq: ((bb if use_b else bf)[i], col))
    return pl.pallas_call(
        _ret_kernel,
        out_shape=[jax.ShapeDtypeStruct((T, HALF), F32), jax.ShapeDtypeStruct((T, HALF), F32),
                   jax.ShapeDtypeStruct((2, N_SEQ, 4, LANES, LANES), F32)],
        grid_spec=pltpu.PrefetchScalarGridSpec(
            num_scalar_prefetch=5, grid=(len(tabs[0]),),
            in_specs=[spec(False, 0), spec(False, 1), spec(True, 0), spec(True, 1), st],
            out_specs=[spec(False, 0), spec(True, 0), st],
            scratch_shapes=[pltpu.VMEM((2, 4, LANES, LANES), F32)]),
        compiler_params=_params(("arbitrary",)),
        name="retention",
    )(*tabs, rqk, zr, rqk, zr, s0)


def _out_kernel(*refs, even):
    if even:
        (a_ref, yf_ref, yb_ref, bonus_ref, g_ref, gng_ref, gnb_ref, ones_ref,
         x_ref, mod_ref, ng_ref, wo_ref, rw_ref, rb_ref,
         y_ref, xp_ref, ti_ref, tg_ref, rk_ref, cnt_ref, run_ref) = refs
        ones_bd = ones_ref[...]
        ys = yf_ref[...] + yb_ref[...]
        mu = _group_sum(ys, ones_bd) * (1.0 / HEAD)
        dv = ys - mu
        var = _group_sum(dv * dv, ones_bd) * (1.0 / HEAD)
        yn = dv * lax.rsqrt(var + RWKV_GN_EPS) * gng_ref[...] + gnb_ref[...]
        left = a_ref[...]
        right = (yn + bonus_ref[...]) * g_ref[...]
    else:
        (c_ref, of_ref, ob_ref, rg_ref, gng_ref,
         x_ref, mod_ref, ng_ref, wo_ref, rw_ref, rb_ref,
         y_ref, xp_ref, ti_ref, tg_ref, rk_ref, cnt_ref, run_ref) = refs
        left = c_ref[...]
        rg = rg_ref[...]
        gate = rg * jax.nn.sigmoid(rg)
        os_ = of_ref[...] + ob_ref[...]
        parts = []
        for h in range(4):
            oh = os_[:, h * LANES:(h + 1) * LANES]
            mu = jnp.mean(oh, axis=-1, keepdims=True)
            dv = oh - mu
            var = jnp.mean(dv * dv, axis=-1, keepdims=True)
            parts.append(dv * lax.rsqrt(var + 1e-5))
        right = gate * (jnp.concatenate(parts, axis=1) * gng_ref[...])
    mod = mod_ref[0]
    o = (jnp.dot(left.astype(BF16), wo_ref[0:HALF, :], preferred_element_type=F32)
         + jnp.dot(right.astype(BF16), wo_ref[HALF:2 * HALF, :], preferred_element_type=F32))
    y = x_ref[...] + mod[:, 2 * D:3 * D] * o
    y_ref[...] = y
    yn2 = y * lax.rsqrt(jnp.mean(y * y, axis=-1, keepdims=True) + 1e-6) * ng_ref[...]
    t = yn2 * (1.0 + mod[:, 4 * D:5 * D]) + mod[:, 3 * D:4 * D]
    xp_ref[...] = t
    logits = _mm(t, rw_ref[...], NN, 3) + rb_ref[...]
    lane = lax.broadcasted_iota(I32, (TM, LANES), 1)
    neg = jnp.float32(-jnp.inf)
    lg = jnp.where(lane < N_EXPERTS, logits, neg)
    vals, hits = [], []
    for _ in range(TOP_K):
        m = jnp.max(lg, axis=-1, keepdims=True)
        ix = jnp.min(jnp.where(lg == m, lane, LANES), axis=-1, keepdims=True)
        hit = lane == ix
        vals.append(m)
        hits.append((ix, hit))
        lg = jnp.where(hit, neg, lg)
    es = [jnp.exp(vv - vals[0]) for vv in vals]
    den = es[0] + es[1] + es[2] + es[3]

    @pl.when(pl.program_id(0) == 0)
    def _():
        run_ref[...] = jnp.zeros_like(run_ref)

    member = jnp.zeros((TM, LANES), F32)
    for _, hit in hits:
        member = member + jnp.where(hit, 1.0, 0.0)
    ri = lax.broadcasted_iota(I32, (TM, TM), 0)
    ci = lax.broadcasted_iota(I32, (TM, TM), 1)
    before = jnp.where(ci < ri, 1.0, 0.0).astype(BF16)
    seen = run_ref[...] + jnp.dot(before, member.astype(BF16), preferred_element_type=F32)
    ti = jnp.zeros((TM, LANES), I32)
    tg = jnp.zeros((TM, LANES), F32)
    rk = jnp.zeros((TM, LANES), F32)
    for kk, (ix, hit) in enumerate(hits):
        ti = jnp.where(lane == kk, ix, ti)
        tg = jnp.where(lane == kk, es[kk] / den, tg)
        rk = jnp.where(lane == kk, jnp.sum(jnp.where(hit, seen, 0.0), axis=-1, keepdims=True), rk)
    ti_ref[...] = ti
    tg_ref[...] = tg
    rk_ref[...] = rk.astype(I32)
    run_ref[...] = run_ref[...] + jnp.sum(member, axis=0, keepdims=True)
    cnt_ref[...] = run_ref[...]


def _out_proj(even, mix_args, mix_specs, x, mod, norm_g, w_out_bf16, rw_pad, rb_pad):
    row = lambda i: (i, 0)
    modspec = pl.BlockSpec((1, 1, N_MOD * D), lambda i: (_group(i), 0, 0))
    in_specs = list(mix_specs) + [pl.BlockSpec((TM, D), row), modspec, _full((1, D)), _full((D, D)),
                                  _full((D, LANES)), _full((1, LANES))]
    args = list(mix_args) + [x, mod, norm_g.reshape(1, D), w_out_bf16, rw_pad, rb_pad]
    lane_i = jax.ShapeDtypeStruct((T, LANES), I32)
    lane_spec = pl.BlockSpec((TM, LANES), row)
    return pl.pallas_call(
        functools.partial(_out_kernel, even=even),
        out_shape=[jax.ShapeDtypeStruct((T, D), F32), jax.ShapeDtypeStruct((T, D), F32),
                   lane_i, jax.ShapeDtypeStruct((T, LANES), F32), lane_i,
                   jax.ShapeDtypeStruct((1, LANES), F32)],
        grid=(NT,), in_specs=in_specs,
        out_specs=[pl.BlockSpec((TM, D), row), pl.BlockSpec((TM, D), row),
                   lane_spec, lane_spec, lane_spec, _full((1, LANES))],
        scratch_shapes=[pltpu.VMEM((1, LANES), F32)],
        compiler_params=_params(("arbitrary",), 48),
        name="out_proj",
    )(*args)


def _route_kernel(cnt_ref, ti_ref, rk_ref, dest_ref, te_ref, nt_ref):
    cnt = cnt_ref[...].astype(I32)
    ntile = lax.shift_right_logical(cnt + (TMX - 1), TMX.bit_length() - 1)
    ei = lax.broadcasted_iota(I32, (LANES, LANES), 0)
    ej = lax.broadcasted_iota(I32, (LANES, LANES), 1)
    upto = jnp.where(ei <= ej, 1.0, 0.0).astype(BF16)
    ntile_f = jnp.broadcast_to(ntile.astype(F32), (8, LANES))
    tile_end = jnp.dot(ntile_f.astype(BF16), upto, preferred_element_type=F32)[0:1, :]
    row_start = (tile_end - ntile.astype(F32)) * float(TMX)
    lane = lax.broadcasted_iota(I32, (TM, LANES), 1)
    ti = ti_ref[...]
    rk = rk_ref[...]
    dest = jnp.zeros((TM, LANES), F32)
    for k in range(TOP_K):
        hit = lane == ti[:, k:k + 1]
        start = jnp.sum(jnp.where(hit, row_start, 0.0), axis=-1, keepdims=True)
        dest = jnp.where(lane == k, start + rk[:, k:k + 1].astype(F32), dest)
    dest_ref[...] = dest.astype(I32)

    @pl.when(pl.program_id(0) == 0)
    def _():
        lane1 = lax.broadcasted_iota(I32, (1, LANES), 1)
        n_tiles = jnp.max(tile_end, axis=-1, keepdims=True)
        last_e = jnp.max(jnp.where(cnt > 0, lane1, 0), axis=-1, keepdims=True)
        tile = lax.broadcasted_iota(I32, (TM, 1), 0).astype(F32)
        te = jnp.sum(jnp.where(tile_end <= tile, 1, 0), axis=-1, keepdims=True)
        te = jnp.where(tile < n_tiles, te, last_e)
        te_ref[...] = jnp.broadcast_to(te, (TM, LANES)).astype(I32)
        first_row = lax.broadcasted_iota(I32, (8, LANES), 0) == 0
        nt_ref[...] = jnp.where(first_row, n_tiles, tile_end).astype(I32)


def _route(cnt, ti, rk):
    row = lambda i: (i, 0)
    return pl.pallas_call(
        _route_kernel,
        out_shape=[jax.ShapeDtypeStruct((T, LANES), I32), jax.ShapeDtypeStruct((TM, LANES), I32),
                   jax.ShapeDtypeStruct((8, LANES), I32)],
        grid=(NT,),
        in_specs=[_full((1, LANES)), pl.BlockSpec((TM, LANES), row), pl.BlockSpec((TM, LANES), row)],
        out_specs=[pl.BlockSpec((TM, LANES), row), _full((TM, LANES)), _full((8, LANES))],
        compiler_params=_params(("arbitrary",)),
        name="moe_route",
    )(cnt, ti, rk)


def _start_row_copy(src_ref, src_row, dst_ref, dst_row, sem, queue):
    pltpu.async_copy(src_ref.at[pl.ds(src_row, 1)], dst_ref.at[pl.ds(dst_row, 1)], sem, priority=queue)


def _wait_tiles(n, src_ref, dst_ref, sem):
    for _ in range(n):
        pltpu.make_async_copy(src_ref, dst_ref, sem).wait()


def _dispatch_kernel(dest_ref, tend_ref, x_ref, xs_ref, zero_ref, sem):
    i = pl.program_id(0)

    @pl.when(i == 0)
    def _():
        zero_ref[...] = jnp.zeros_like(zero_ref)

        def last_tile(e, fn):
            end = tend_ref[e]
            begin = tend_ref[e - 1] if e > 0 else 0

            @pl.when(end > begin)
            def _():
                fn(pltpu.make_async_copy(zero_ref, xs_ref.at[pl.ds((end - 1) * TMX, TMX)], sem))

        def unused_tile(j):
            return pltpu.make_async_copy(zero_ref, xs_ref.at[pl.ds(j * TMX, TMX)], sem)

        def start_unused(j, carry):
            unused_tile(j).start()
            return carry

        def wait_unused(j, carry):
            unused_tile(j).wait()
            return carry

        n_used = tend_ref[N_EXPERTS - 1]
        for e in range(N_EXPERTS):
            last_tile(e, lambda c: c.start())
        lax.fori_loop(n_used, MOE_TILES, start_unused, 0)
        for e in range(N_EXPERTS):
            last_tile(e, lambda c: c.wait())
        lax.fori_loop(n_used, MOE_TILES, wait_unused, 0)

    base = i * (TM * TOP_K)

    def start(r, carry):
        for k in range(TOP_K):
            _start_row_copy(x_ref, r, xs_ref, dest_ref[base + r * TOP_K + k], sem, k % 2)
        return carry

    lax.fori_loop(0, TM, start, 0, unroll=4)
    _wait_tiles(TOP_K, x_ref, xs_ref.at[pl.ds(0, TM)], sem)


def _dispatch(dest_flat, tile_end, xt):
    return pl.pallas_call(
        _dispatch_kernel,
        out_shape=jax.ShapeDtypeStruct((R_PAD, D), F32),
        grid_spec=pltpu.PrefetchScalarGridSpec(
            num_scalar_prefetch=2, grid=(NT,),
            in_specs=[pl.BlockSpec((TM, D), lambda i, d, te: (i, 0))],
            out_specs=pl.BlockSpec(memory_space=pl.ANY),
            scratch_shapes=[pltpu.VMEM((TMX, D), F32), pltpu.SemaphoreType.DMA(())]),
        compiler_params=_params(("arbitrary",)),
        name="moe_dispatch",
    )(dest_flat, tile_end, xt)


W_PARTS = 4


def _expert_weights(i, nt, te_ref, w_ref, wbuf_ref, wsem, w16_ref, group_ref):
    rows = w_ref.shape[1] // W_PARTS

    def fetch(e, buf):
        return [pltpu.make_async_copy(w_ref.at[e, pl.ds(p * rows, rows)], wbuf_ref.at[buf, pl.ds(p * rows, rows)],
                                      wsem.at[buf]) for p in range(W_PARTS)]

    @pl.when(i == 0)
    def _():
        group_ref[0] = 0
        for c in fetch(te_ref[0], 0):
            c.start()

    first = jnp.logical_or(i == 0, te_ref[i] != te_ref[jnp.maximum(i - 1, 0)])

    @pl.when(jnp.logical_and(first, i < nt))
    def _():
        cur = group_ref[0] % 2
        nxt = lax.while_loop(
            lambda j: jnp.logical_and(j < nt, te_ref[jnp.minimum(j, MOE_TILES - 1)] == te_ref[i]),
            lambda j: j + 1, i + 1)

        @pl.when(nxt < nt)
        def _():
            for c in fetch(te_ref[jnp.minimum(nxt, MOE_TILES - 1)], 1 - cur):
                c.start()

        for c in fetch(0, cur):
            c.wait()
        w16_ref[...] = wbuf_ref[cur].astype(BF16)
        group_ref[0] = group_ref[0] + 1


def _experts_kernel(te_ref, nt_ref, xs_ref, wgu_ref, bgu_ref, wdn_ref, bdn_ref, y_ref,
                    gu_buf, gu_sem, gu16_ref, gu_group, dn_buf, dn_sem, dn16_ref, dn_group):
    i = pl.program_id(0)
    nt = nt_ref[0]
    _expert_weights(i, nt, te_ref, wgu_ref, gu_buf, gu_sem, gu16_ref, gu_group)
    _expert_weights(i, nt, te_ref, wdn_ref, dn_buf, dn_sem, dn16_ref, dn_group)

    @pl.when(i < nt)
    def _():
        x16 = xs_ref[...].astype(BF16)
        y = bdn_ref[0]
        for h in range(2):
            gc = slice(h * HALF, (h + 1) * HALF)
            uc = slice(D + h * HALF, D + (h + 1) * HALF)
            g = jnp.dot(x16, gu16_ref[:, gc], preferred_element_type=F32) + bgu_ref[0, :, gc]
            u = jnp.dot(x16, gu16_ref[:, uc], preferred_element_type=F32) + bgu_ref[0, :, uc]
            gt = jnp.minimum(g, SWIGLU_LIMIT)
            up = jnp.clip(u, -SWIGLU_LIMIT, SWIGLU_LIMIT)
            act = ((up + 1.0) * gt * jax.nn.sigmoid(SWIGLU_ALPHA * gt)).astype(BF16)
            y = y + jnp.dot(act, dn16_ref[gc, :], preferred_element_type=F32)
        y_ref[...] = y

    @pl.when(i >= nt)
    def _():
        y_ref[...] = jnp.zeros_like(y_ref)


def _tile_clamped(i, te, nt):
    return (jnp.minimum(i, jnp.maximum(nt[0] - 1, 0)), 0)


def _weight_scratch(n_out):
    return [pltpu.VMEM((2, D, n_out), F32), pltpu.SemaphoreType.DMA((2,)), pltpu.VMEM((D, n_out), BF16),
            pltpu.SMEM((1,), I32)]


def _experts(te, n_tiles, xs, w_gu, b_gu, w_dn, b_dn):
    return pl.pallas_call(
        _experts_kernel,
        out_shape=jax.ShapeDtypeStruct((R_PAD, D), F32),
        grid_spec=pltpu.PrefetchScalarGridSpec(
            num_scalar_prefetch=2, grid=(MOE_TILES,),
            in_specs=[pl.BlockSpec((TMX, D), _tile_clamped), pl.BlockSpec(memory_space=pl.ANY),
                      pl.BlockSpec((1, 1, 2 * D), lambda i, te, nt: (te[i], 0, 0)),
                      pl.BlockSpec(memory_space=pl.ANY),
                      pl.BlockSpec((1, 1, D), lambda i, te, nt: (te[i], 0, 0))],
            out_specs=pl.BlockSpec((TMX, D), lambda i, te, nt: (i, 0)),
            scratch_shapes=_weight_scratch(2 * D) + _weight_scratch(D)),
        compiler_params=_params(("arbitrary",), 58),
        name="moe_experts",
    )(te, n_tiles, xs, w_gu, b_gu, w_dn, b_dn)


def _combine_kernel(dest_ref, x_ref, tg_ref, mod_ref, ys_ref, o_ref, buf_ref, sem):
    i = pl.program_id(0)
    slot = i % 2

    def gather(tile, b):
        base = tile * (TM * TOP_K)

        def body(r, carry):
            for k in range(TOP_K):
                _start_row_copy(ys_ref, dest_ref[base + r * TOP_K + k], buf_ref.at[b, k], r, sem.at[b], k % 2)
            return carry

        lax.fori_loop(0, TM, body, 0, unroll=4)

    @pl.when(i == 0)
    def _():
        gather(0, 0)

    @pl.when(i + 1 < NT)
    def _():
        gather(i + 1, 1 - slot)

    _wait_tiles(TOP_K, ys_ref.at[pl.ds(0, TM)], buf_ref.at[slot, 0], sem.at[slot])
    tg = tg_ref[...]
    f = tg[:, 0:1] * buf_ref[slot, 0]
    for k in range(1, TOP_K):
        f = f + tg[:, k:k + 1] * buf_ref[slot, k]
    o_ref[...] = x_ref[...] + mod_ref[0][:, 5 * D:6 * D] * f


def _combine(dest_flat, x, tg, mod, ys):
    row = lambda i, d: (i, 0)
    return pl.pallas_call(
        _combine_kernel,
        out_shape=jax.ShapeDtypeStruct((T, D), F32),
        grid_spec=pltpu.PrefetchScalarGridSpec(
            num_scalar_prefetch=1, grid=(NT,),
            in_specs=[pl.BlockSpec((TM, D), row), pl.BlockSpec((TM, LANES), row),
                      pl.BlockSpec((1, 1, N_MOD * D), lambda i, d: (_group(i), 0, 0)),
                      pl.BlockSpec(memory_space=pl.ANY)],
            out_specs=pl.BlockSpec((TM, D), row),
            scratch_shapes=[pltpu.VMEM((2, TOP_K, TM, D), F32), pltpu.SemaphoreType.DMA((2,))]),
        compiler_params=_params(("arbitrary",), 40),
        name="moe_combine",
    )(dest_flat, x, tg, mod, ys)


def _moe(layer, y, xt, ti, tg, rk, cnt, mod, w_gu, b_gu, w_dn, b_dn):
    dest, te, nt = _route(cnt, ti, rk)
    dest_flat = dest[:, :TOP_K].reshape(-1)
    te = te[:MOE_TILES, 0] + layer * N_EXPERTS
    n_tiles = nt[0, :1]
    xs = _dispatch(dest_flat, nt[1, :N_EXPERTS], xt)
    n_all = w_gu.shape[0] * N_EXPERTS
    ys = _experts(te, n_tiles, xs, w_gu.reshape(n_all, D, 2 * D), b_gu.reshape(n_all, 1, 2 * D),
                  w_dn.reshape(n_all, D, D), b_dn.reshape(n_all, 1, D))
    return _combine(dest_flat, y, tg, mod, ys)


def _ones_blockdiag():
    idx = np.arange(HALF) // HEAD
    return jnp.asarray((idx[:, None] == idx[None, :]).astype(np.float32), dtype=BF16)


def _rope_tables():
    pos = jnp.arange(L_SAMPLE)
    rowp = (pos // 64).astype(F32)
    colp = (pos % 64).astype(F32)
    nf = HEAD // 4
    inv = jnp.power(10000.0, -jnp.arange(nf, dtype=F32) / nf)
    ar = rowp[:, None] * inv[None, :]
    ac = colp[:, None] * inv[None, :]
    cos64 = jnp.concatenate([jnp.cos(ar), jnp.cos(ar), jnp.cos(ac), jnp.cos(ac)], axis=1)
    sin64 = jnp.concatenate([-jnp.sin(ar), jnp.sin(ar), -jnp.sin(ac), jnp.sin(ac)], axis=1)
    cos = jnp.tile(cos64, (1, HALF // HEAD))
    sin = jnp.tile(sin64, (1, HALF // HEAD))
    ident = jnp.ones((TM, HALF), F32)
    return (jnp.concatenate([ident, cos], axis=0), jnp.concatenate([jnp.zeros((TM, HALF), F32), sin], axis=0))


def _bd_pairs(s):
    lead = s.shape[:-3]
    s = s.reshape(lead + (4, 2, HEAD, HEAD))
    z = jnp.zeros_like(s[..., 0, :, :])
    top = jnp.concatenate([s[..., 0, :, :], z], axis=-1)
    bot = jnp.concatenate([z, s[..., 1, :, :]], axis=-1)
    return jnp.concatenate([top, bot], axis=-2)


def _bd_unpairs(s):
    a = s[..., 0:HEAD, 0:HEAD]
    b = s[..., HEAD:, HEAD:]
    out = jnp.stack([a, b], axis=-3)
    return out.reshape(s.shape[:-3] + (8, HEAD, HEAD))


def kernel(x_prompt, x_sample, state_rwkv, cache_k_diff, cache_v_diff, state_retention, c, c_ctx, norm_g, ada_w, ada_b, e_w_in, e_w_out, sgu_ln_g, sgu_w, sgu_b, rw_mu, rw_w0, rw_w_up, rw_a0, rw_a_up, rw_g_up, rw_k_k, rw_k_a, rw_r_k, rw_gn_g, rw_gn_b, o_w_in, o_w_out, da_qk_g, da_lam, da_subln_g, ret_gn_g, router_w, router_b, ex_w_gu, ex_b_gu, ex_w_dn, ex_b_dn):
    x = jnp.concatenate([x_prompt.reshape(T_PROMPT, D), x_sample.reshape(T_SAMPLE, D)], axis=0)
    cvec8 = jnp.concatenate([c_ctx[None, :], c, jnp.zeros((3, D), F32)], axis=0)
    mods = _adaln(cvec8, ada_w, ada_b)
    mod0 = mods[0].reshape(8, 1, N_MOD * D)
    mod1 = mods[1].reshape(8, 1, N_MOD * D)
    ones_bd = _ones_blockdiag()
    rw_pad = jnp.pad(router_w, ((0, 0), (0, 0), (0, LANES - N_EXPERTS)))
    rb_pad = jnp.pad(router_b, ((0, 0), (0, LANES - N_EXPERTS))).reshape(2, 1, LANES)
    row = lambda i: (i, 0)
    half = pl.BlockSpec((TM, HALF), row)

    za, zb = _in_proj(x, norm_g[0, 0], mod0, e_w_in[0].astype(BF16), (2 * HALF, B_COLS))
    bs_full = jnp.repeat(sgu_b[0].T, HEAD, axis=1)
    a_out = _sgu(za, sgu_ln_g[0], sgu_w[0].astype(BF16), bs_full)
    zpad = jnp.zeros((2, HEAD, HALF), F32)
    wup_pad = jnp.concatenate([rw_w_up[0], zpad], axis=1).astype(BF16)
    aup_pad = jnp.concatenate([zpad, rw_a_up[0]], axis=1).astype(BF16)
    r, v, kkn, bonus, g, lw, kt, b = _rwkv_prep(zb, rw_mu[0], rw_k_k[0], rw_k_a[0], rw_r_k[0], rw_w0[0], rw_a0[0],
                                                wup_pad, aup_pad, rw_g_up[0].astype(BF16), ones_bd)
    s0_sample = _bd_pairs(jnp.moveaxis(state_rwkv[:, 0], 1, 0))
    s0_rw = jnp.concatenate([jnp.zeros((2, N_PROMPT, 4, LANES, LANES), F32), s0_sample], axis=1)
    yf_rw, yb_rw, sfin_rw = _rwkv_scan(r, v, kkn, lw, kt, b, s0_rw)
    new_rwkv = jnp.moveaxis(_bd_unpairs(sfin_rw[:, :N_PROMPT]), 0, 1)[:, None]
    y0, xp0, ti0, tg0, rk0, cnt0 = _out_proj(
        True,
        [a_out, yf_rw, yb_rw, bonus, g, rw_gn_g[0].reshape(1, HALF), rw_gn_b[0].reshape(1, HALF), ones_bd],
        [half, half, half, half, half, _full((1, HALF)), _full((1, HALF)), _full((HALF, HALF))],
        x, mod0, norm_g[0, 1], e_w_out[0].astype(BF16), rw_pad[0], rb_pad[0])
    x1 = _moe(0, y0, xp0, ti0, tg0, rk0, cnt0, mod0, ex_w_gu, ex_b_gu, ex_w_dn, ex_b_dn)

    zc, zr = _in_proj(x1, norm_g[1, 0], mod1, o_w_in[0].astype(BF16), (3 * HALF, 3 * HALF))
    cos_tab, sin_tab = _rope_tables()
    qkg = jnp.tile(da_qk_g[0], (1, HALF // HEAD))
    cq, ck, ck_raw, rqk = _odd_prep(zc, zr, cos_tab, sin_tab, qkg, ones_bd)
    lambda_init = 0.8 - 0.6 * math.exp(-0.3 * 1)
    lv = da_lam[0]
    lam = jnp.exp(jnp.sum(lv[0] * lv[1])) - jnp.exp(jnp.sum(lv[2] * lv[3])) + lambda_init
    c_out = _attn(cq, ck, zc, 0, N_PROMPT, L_PROMPT, lam, da_subln_g[0], 1.0 - lambda_init,
                  jnp.zeros((T, HALF), F32))
    ctx_k = cache_k_diff[:, 0].reshape(N_SAMPLE, PAST, HALF)
    ctx_v = cache_v_diff[:, 0].reshape(N_SAMPLE, PAST, HALF)
    c_out = _attn(cq, ck, zc, T_PROMPT, N_SAMPLE, L_SAMPLE, lam, da_subln_g[0], 1.0 - lambda_init,
                  c_out, ctx_k, ctx_v)
    sr = jnp.moveaxis(state_retention[:, 0], 1, 0)
    zr0 = jnp.zeros_like(sr)
    s0_sample = jnp.stack([jnp.concatenate([sr[:, :, 0], zr0[:, :, 0]], axis=-2),
                           jnp.concatenate([zr0[:, :, 1], sr[:, :, 1]], axis=-2),
                           jnp.concatenate([sr[:, :, 2], zr0[:, :, 2]], axis=-2),
                           jnp.concatenate([zr0[:, :, 3], sr[:, :, 3]], axis=-2)], axis=2)
    s0_ret = jnp.concatenate([jnp.zeros((2, N_PROMPT, 4, LANES, LANES), F32), s0_sample], axis=1)
    of_ret, ob_ret, rfin = _retention(rqk, zr, s0_ret)
    rfin_p = rfin[:, :N_PROMPT]
    new_ret = jnp.stack([rfin_p[:, :, 0, 0:HEAD], rfin_p[:, :, 1, HEAD:], rfin_p[:, :, 2, 0:HEAD],
                         rfin_p[:, :, 3, HEAD:]], axis=2)
    new_ret = jnp.moveaxis(new_ret, 0, 1)[:, None]
    y1, xp1, ti1, tg1, rk1, cnt1 = _out_proj(
        False,
        [c_out, of_ret, ob_ret, zr, ret_gn_g[0].reshape(1, HALF)],
        [half, half, half, pl.BlockSpec((TM, HALF), lambda i: (i, 2)), _full((1, HALF))],
        x1, mod1, norm_g[1, 1], o_w_out[0].astype(BF16), rw_pad[1], rb_pad[1])
    y_fin = _moe(1, y1, xp1, ti1, tg1, rk1, cnt1, mod1, ex_w_gu, ex_b_gu, ex_w_dn, ex_b_dn)

    new_k = ck_raw[:T_PROMPT].reshape(N_PROMPT, 1, L_PROMPT, 4, LANES)
    new_v = zc[:T_PROMPT, 2 * HALF:3 * HALF].reshape(N_PROMPT, 1, L_PROMPT, 4, LANES)
    return (y_fin[:T_PROMPT].reshape(N_PROMPT, L_PROMPT, D), y_fin[T_PROMPT:].reshape(N_SAMPLE, L_SAMPLE, D),
            new_rwkv, new_k, new_v, new_ret)
```

```python
import functools
import math

import numpy as np
import jax
import jax.numpy as jnp
from jax import lax
from jax.experimental import pallas as pl
from jax.experimental.pallas import tpu as pltpu

F32 = jnp.float32
BF16 = jnp.bfloat16
I32 = jnp.int32

D = 1024
N_PROMPT, L_PROMPT = 16, 256
N_SAMPLE, L_SAMPLE = 4, 1024
N_SEQ = N_PROMPT + N_SAMPLE
PAST = 256
T_PROMPT = N_PROMPT * L_PROMPT
T_SAMPLE = N_SAMPLE * L_SAMPLE
T = T_PROMPT + T_SAMPLE
TM = 256
NT = T // TM
PROMPT_TILES = T_PROMPT // TM
TILES_PER_SAMPLE = L_SAMPLE // TM
N_MOD = 6
HALF = 512
B_COLS = 1792
HEAD = 64
W_DECAY_SCALE = math.exp(-0.5)
RWKV_GN_EPS = 64e-5
RW_CHUNK = 64
RET_CHUNK = 128
RET_EXP = ((5.0, 7.0, 9.0, 11.0), (6.0, 8.0, 10.0, 12.0))
N_EXPERTS = 32
TOP_K = 4
SWIGLU_LIMIT = 7.0
SWIGLU_ALPHA = 1.702
N_ASSIGN = T * TOP_K
TMX = 512
MOE_TILES = N_ASSIGN // TMX + N_EXPERTS
R_PAD = MOE_TILES * TMX
LANES = 128

NN = (((1,), (0,)), ((), ()))
NT_DIMS = (((1,), (1,)), ((), ()))
TN = (((0,), (0,)), ((), ()))


def _group(i):
    return jnp.where(i < PROMPT_TILES, 0, 1 + (i - PROMPT_TILES) // TILES_PER_SAMPLE)


def _mm(a, b, dims=NN, passes=1):
    dg = functools.partial(lax.dot_general, dimension_numbers=dims, preferred_element_type=F32)
    if passes == 1:
        return dg(a.astype(BF16), b.astype(BF16))
    a = a.astype(F32)
    b = b.astype(F32)
    ah = a.astype(BF16)
    al = (a - ah.astype(F32)).astype(BF16)
    bh = b.astype(BF16)
    if passes == 2:
        assert dims[0][0] == (1,)
        m = a.shape[0]
        both = dg(jnp.concatenate([ah, al], axis=0), bh)
        return both[0:m] + both[m:2 * m]
    bl = (b - bh.astype(F32)).astype(BF16)
    if dims[0][0] == (1,):
        m = a.shape[0]
        both = dg(jnp.concatenate([ah, al], axis=0), bh)
        return both[0:m] + (dg(ah, bl) + both[m:2 * m])
    return dg(ah, bh) + (dg(ah, bl) + dg(al, bh))


def _group_sum(x, ones_bd):
    xh = x.astype(BF16)
    xl = (x - xh.astype(F32)).astype(BF16)
    return (jnp.dot(xh, ones_bd, preferred_element_type=F32)
            + jnp.dot(xl, ones_bd, preferred_element_type=F32))


def _full(shape):
    nd = len(shape)
    return pl.BlockSpec(shape, lambda *_: (0,) * nd)


def _params(sem, vmem_mb=None):
    kw = {}
    if vmem_mb is not None:
        kw["vmem_limit_bytes"] = vmem_mb * 1024 * 1024
    return pltpu.CompilerParams(dimension_semantics=sem, **kw)


def _seq_tables(chunk):
    blk_f, blk_b, first, last, seq = [], [], [], [], []
    row = 0
    for s in range(N_SEQ):
        n = (L_PROMPT if s < N_PROMPT else L_SAMPLE) // chunk
        base = row // chunk
        for j in range(n):
            blk_f.append(base + j)
            blk_b.append(base + n - 1 - j)
            first.append(int(j == 0))
            last.append(int(j == n - 1))
            seq.append(s)
        row += n * chunk
    return tuple(np.asarray(a, np.int32) for a in (blk_f, blk_b, first, last, seq))


def _adaln_kernel(c_ref, w_ref, b_ref, o_ref):
    c = c_ref[...]
    s = c * jax.nn.sigmoid(c)
    o_ref[0] = _mm(s, w_ref[0], NN, 3) + b_ref[0]


def _adaln(cvec8, ada_w, ada_b):
    depth, _, n = ada_w.shape
    bn = 1536
    return pl.pallas_call(
        _adaln_kernel,
        out_shape=jax.ShapeDtypeStruct((depth, 8, n), F32),
        grid=(depth, n // bn),
        in_specs=[pl.BlockSpec((8, D), lambda l, j: (0, 0)),
                  pl.BlockSpec((1, D, bn), lambda l, j: (l, 0, j)),
                  pl.BlockSpec((1, 1, bn), lambda l, j: (l, 0, j))],
        out_specs=pl.BlockSpec((1, 8, bn), lambda l, j: (l, 0, j)),
        compiler_params=_params(("arbitrary", "arbitrary"), 40),
        name="adaln",
    )(cvec8, ada_w, ada_b.reshape(depth, 1, n))


def _in_kernel(x_ref, g_ref, mod_ref, w_ref, *outs, splits):
    x = x_ref[...]
    mod = mod_ref[0]
    y = x * lax.rsqrt(jnp.mean(x * x, axis=-1, keepdims=True) + 1e-6) * g_ref[...]
    h = (y * (1.0 + mod[:, D:2 * D]) + mod[:, 0:D]).astype(BF16)
    off = 0
    for o_ref, n in zip(outs, splits):
        o_ref[...] = jnp.dot(h, w_ref[:, off:off + n], preferred_element_type=F32)
        off += n


def _in_proj(x, g, mod, w_bf16, splits):
    n = w_bf16.shape[1]
    row = lambda i: (i, 0)
    return pl.pallas_call(
        functools.partial(_in_kernel, splits=splits),
        out_shape=[jax.ShapeDtypeStruct((T, s), F32) for s in splits],
        grid=(NT,),
        in_specs=[pl.BlockSpec((TM, D), row), _full((1, D)),
                  pl.BlockSpec((1, 1, N_MOD * D), lambda i: (_group(i), 0, 0)), _full((D, n))],
        out_specs=[pl.BlockSpec((TM, s), row) for s in splits],
        compiler_params=_params(("arbitrary",), 48),
        name="in_proj",
    )(x, g.reshape(1, D), mod, w_bf16)


def _gelu(x):
    return 0.5 * x * (1.0 + lax.erf(x * (1.0 / math.sqrt(2.0))))


def _sgu_kernel(za_ref, lng_ref, ws_ref, bs_ref, o_ref):
    u = _gelu(za_ref[:, 0:HALF])
    va = _gelu(za_ref[:, HALF:2 * HALF])
    mu = jnp.mean(va, axis=-1, keepdims=True)
    dv = va - mu
    var = jnp.mean(dv * dv, axis=-1, keepdims=True)
    vn = dv * lax.rsqrt(var + 1e-5) * lng_ref[...]
    lane = lax.broadcasted_iota(I32, (LANES, LANES), 1)
    first = lane < HEAD
    for c in range(TM // LANES):
        rows = slice(c * LANES, (c + 1) * LANES)
        for p in range(HALF // LANES):
            cols = slice(p * LANES, (p + 1) * LANES)
            vp = vn[rows, cols]
            s = (jnp.dot(ws_ref[2 * p], jnp.where(first, vp, 0.0).astype(BF16), preferred_element_type=F32)
                 + jnp.dot(ws_ref[2 * p + 1], jnp.where(first, 0.0, vp).astype(BF16), preferred_element_type=F32))
            o_ref[rows, cols] = u[rows, cols] * (s + bs_ref[:, cols])


def _sgu(za, ln_g, w_s_bf16, bs_full):
    return pl.pallas_call(
        _sgu_kernel,
        out_shape=jax.ShapeDtypeStruct((T, HALF), F32),
        grid=(NT,),
        in_specs=[pl.BlockSpec((TM, 2 * HALF), lambda i: (i, 0)), _full((1, HALF)),
                  _full((8, LANES, LANES)), _full((LANES, HALF))],
        out_specs=pl.BlockSpec((TM, HALF), lambda i: (i, 0)),
        compiler_params=_params(("arbitrary",)),
        name="sgu",
    )(za, ln_g.reshape(1, HALF), w_s_bf16, bs_full)


def _rwkv_prep_kernel(zb_ref, zp_ref, zn_ref, mu_ref, kk_ref, ka_ref, rk_ref, w0_ref, a0_ref,
                      wup_ref, aup_ref, gup_ref, ones_ref,
                      r_ref, v_ref, kkn_ref, bonus_ref, g_ref, lw_ref, kt_ref, b_ref):
    i = pl.program_id(0)
    in_sample = i >= PROMPT_TILES
    pos = (i - PROMPT_TILES) % TILES_PER_SAMPLE
    is_first = jnp.logical_or(jnp.logical_not(in_sample), pos == 0)
    is_last = jnp.logical_or(jnp.logical_not(in_sample), pos == TILES_PER_SAMPLE - 1)
    zb = zb_ref[...]
    prev_row = jnp.where(is_first, 0.0, zp_ref[7:8, :])
    next_row = jnp.where(is_last, 0.0, zn_ref[0:1, :])
    rowid = lax.broadcasted_iota(I32, (TM, 1), 0)
    zp = jnp.where(rowid == 0, prev_row, pltpu.roll(zb, 1, 0))
    zn = jnp.where(rowid == TM - 1, next_row, pltpu.roll(zb, TM - 1, 0))
    zs = zb + mu_ref[0:1, :] * (zp - zb) + mu_ref[1:2, :] * (zn - zb)
    r = zs[:, 0:HALF]
    k = zs[:, HALF:2 * HALF]
    v = zs[:, 2 * HALF:3 * HALF]
    wa = zs[:, 3 * HALF:3 * HALF + LANES]
    gd = zs[:, 3 * HALF + LANES:B_COLS]
    ones_bd = ones_ref[...]
    r_ref[...] = r
    v_ref[...] = v
    g_ref[...] = jnp.dot(jax.nn.sigmoid(gd).astype(BF16), gup_ref[...], preferred_element_type=F32)
    kk = k * kk_ref[...]
    kkn = kk / jnp.maximum(jnp.sqrt(_group_sum(kk * kk, ones_bd)), 1e-6)
    kkn_ref[...] = kkn
    bonus_ref[...] = _group_sum(r * k * rk_ref[...], ones_bd) * v
    tw = jnp.tanh(wa).astype(BF16)
    wa16 = wa.astype(BF16)
    for dd in range(2):
        lw_ref[dd] = -W_DECAY_SCALE * jax.nn.sigmoid(
            w0_ref[dd:dd + 1, :] + jnp.dot(tw, wup_ref[dd], preferred_element_type=F32))
        a = jax.nn.sigmoid(a0_ref[dd:dd + 1, :] + jnp.dot(wa16, aup_ref[dd], preferred_element_type=F32))
        kt_ref[dd] = k * (1.0 + (a - 1.0) * ka_ref[...])
        b_ref[dd] = a * kkn


def _rwkv_prep(zb, mu, k_k, k_a, r_k, w0, a0, wup_pad, aup_pad, g_up, ones_bd):
    row = lambda i: (i, 0)
    halo = TM // 8
    one = jax.ShapeDtypeStruct((T, HALF), F32)
    two = jax.ShapeDtypeStruct((2, T, HALF), F32)
    o1 = pl.BlockSpec((TM, HALF), row)
    o2 = pl.BlockSpec((2, TM, HALF), lambda i: (0, i, 0))
    return pl.pallas_call(
        _rwkv_prep_kernel,
        out_shape=[one, one, one, one, one, two, two, two],
        grid=(NT,),
        in_specs=[pl.BlockSpec((TM, B_COLS), row),
                  pl.BlockSpec((8, B_COLS), lambda i: (jnp.maximum(i * halo - 1, 0), 0)),
                  pl.BlockSpec((8, B_COLS), lambda i: (jnp.minimum((i + 1) * halo, T // 8 - 1), 0)),
                  _full((2, B_COLS)), _full((1, HALF)), _full((1, HALF)), _full((1, HALF)),
                  _full((2, HALF)), _full((2, HALF)),
                  _full((2, LANES, HALF)), _full((2, LANES, HALF)), _full((LANES, HALF)),
                  _full((HALF, HALF))],
        out_specs=[o1, o1, o1, o1, o1, o2, o2, o2],
        compiler_params=_params(("arbitrary",), 48),
        name="rwkv_prep",
    )(zb, zb, zb, mu, k_k.reshape(1, HALF), k_a.reshape(1, HALF), r_k.reshape(1, HALF), w0, a0,
      wup_pad, aup_pad, g_up, ones_bd)


def _rwkv_chunks(dirs):
    C = RW_CHUNK
    ti = lax.broadcasted_iota(I32, (C, C), 0)
    tj = lax.broadcasted_iota(I32, (C, C), 1)
    bi = lax.broadcasted_iota(I32, (LANES, LANES), 0)
    bj = lax.broadcasted_iota(I32, (LANES, LANES), 1)
    same = (bi >> 6) == (bj >> 6)
    pi = bi & (C - 1)
    pj = bj & (C - 1)
    eye = (bi == bj).astype(F32)
    h0 = lax.broadcasted_iota(I32, (C, LANES), 1) < HEAD

    def stack(x):
        return jnp.concatenate([jnp.where(h0, x, 0.0), jnp.where(h0, 0.0, x)], axis=0)

    def fold(x):
        return x[0:C] + x[C:2 * C]

    chains = []
    for rev, r, v, kk, lw, kt, b, s_ref, y_ref in dirs:
        tri = jnp.where((tj >= ti) if rev else (tj <= ti), 1.0, 0.0).astype(F32)
        p1 = lw.astype(BF16)
        r1 = lw - p1.astype(F32)
        p2 = r1.astype(BF16)
        p3 = (r1 - p2.astype(F32)).astype(BF16)
        cs3 = jnp.dot(tri.astype(BF16), jnp.concatenate([p1, p2, p3], axis=1), preferred_element_type=F32)
        cs = cs3[:, 0:HALF] + (cs3[:, HALF:2 * HALF] + cs3[:, 2 * HALF:3 * HALF])
        ctot = cs[0:1, :] if rev else cs[C - 1:C, :]
        e_neg = jnp.exp(-cs)
        e_tail = jnp.exp(ctot - cs)
        q1 = kk * jnp.exp(cs - lw)
        k1 = kt * e_neg
        b1 = b * e_neg
        r1 = r * jnp.exp(cs)
        k2 = kt * e_tail
        b2 = b * e_tail
        e_tot = jnp.exp(ctot)
        strict = jnp.logical_and(same, (pj > pi) if rev else (pj < pi))
        incl = jnp.logical_and(same, (pj >= pi) if rev else (pj <= pi))
        for p in range(HALF // LANES):
            cols = slice(p * LANES, (p + 1) * LANES)
            chains.append(dict(p=p, cols=cols, strict=strict, incl=incl, s_ref=s_ref, y_ref=y_ref,
                               q1=q1[:, cols], k1=k1[:, cols], b1=b1[:, cols], r1=r1[:, cols],
                               k2=k2[:, cols], b2=b2[:, cols], v=v[:, cols], e_tot=e_tot[:, cols]))

    for ch in chains:
        lhs = jnp.concatenate([stack(ch["q1"]), stack(ch["r1"])], axis=0)
        rhs = jnp.concatenate([ch["k1"], ch["k1"], ch["b1"], ch["b1"]], axis=0)
        gm = _mm(lhs, rhs, NT_DIMS, 2)
        ch["mk"] = jnp.where(ch["strict"], gm[0:2 * C, 0:2 * C], 0.0)
        ch["mb"] = jnp.where(ch["strict"], gm[0:2 * C, 2 * C:4 * C], 0.0)
        ch["nk"] = jnp.where(ch["incl"], gm[2 * C:4 * C, 0:2 * C], 0.0)
        ch["nb"] = jnp.where(ch["incl"], gm[2 * C:4 * C, 2 * C:4 * C], 0.0)
        ch["tinv"] = eye - jnp.where((pi >> 1) == (pj >> 1), ch["mb"], 0.0)
    size = 2
    while size < C:
        sh = size.bit_length() - 1
        blk = jnp.logical_and((pi >> (sh + 1)) == (pj >> (sh + 1)), (pi >> sh) != (pj >> sh))
        for ch in chains:
            ch["tn"] = _mm(ch["tinv"], jnp.where(blk, ch["mb"], 0.0), NN, 1)
        for ch in chains:
            ch["tinv"] = ch["tinv"] - _mm(ch["tn"], ch["tinv"], NN, 2)
        size *= 2
    for ch in chains:
        vst = stack(ch["v"])
        ch["mkv"] = fold(_mm(ch["mk"], vst, NN, 2))
        ch["nkv"] = fold(_mm(ch["nk"], vst, NN, 1))
        ch["s"] = ch["s_ref"][ch["p"]]
        ch["qr"] = _mm(jnp.concatenate([ch["q1"], ch["r1"]], axis=0), ch["s"], NT_DIMS, 2)
    for ch in chains:
        ch["u"] = fold(_mm(ch["tinv"], stack(ch["mkv"] + ch["qr"][0:C]), NN, 2))
    for ch in chains:
        ch["y_ref"][:, ch["cols"]] = ch["qr"][C:2 * C] + ch["nkv"] - fold(_mm(ch["nb"], stack(ch["u"]), NN, 1))
        upd = _mm(jnp.concatenate([ch["v"], ch["u"]], axis=0),
                  jnp.concatenate([ch["k2"], -ch["b2"]], axis=0), TN, 3)
        ch["s_ref"][ch["p"]] = ch["s"] * ch["e_tot"] + jnp.where(same, upd, 0.0)


def _rwkv_scan_kernel(bf_ref, bb_ref, first_ref, last_ref, seq_ref,
                      rf_ref, vf_ref, kkf_ref, lwf_ref, ktf_ref, bfw_ref,
                      rb_ref, vb_ref, kkb_ref, lwb_ref, ktb_ref, bbw_ref, s0_ref,
                      yf_ref, yb_ref, sfin_ref, s_ref):
    step = pl.program_id(0)

    @pl.when(first_ref[step] == 1)
    def _():
        s_ref[...] = s0_ref[:, 0]

    _rwkv_chunks([
        (False, rf_ref[...], vf_ref[...], kkf_ref[...], lwf_ref[0], ktf_ref[0], bfw_ref[0], s_ref.at[0], yf_ref),
        (True, rb_ref[...], vb_ref[...], kkb_ref[...], lwb_ref[0], ktb_ref[0], bbw_ref[0], s_ref.at[1], yb_ref)])

    @pl.when(last_ref[step] == 1)
    def _():
        sfin_ref[:, 0] = s_ref[...]


def _rwkv_scan(r, v, kk, lw, kt, b, s0_bd):
    C = RW_CHUNK
    tabs = _seq_tables(C)
    fwd = lambda i, bf, bb, fi, la, sq: (bf[i], 0)
    bwd = lambda i, bf, bb, fi, la, sq: (bb[i], 0)
    fwd3 = lambda i, bf, bb, fi, la, sq: (0, bf[i], 0)
    bwd3 = lambda i, bf, bb, fi, la, sq: (1, bb[i], 0)
    st = pl.BlockSpec((2, 1, 4, LANES, LANES), lambda i, bf, bb, fi, la, sq: (0, sq[i], 0, 0, 0))
    one_f, one_b = pl.BlockSpec((C, HALF), fwd), pl.BlockSpec((C, HALF), bwd)
    two_f, two_b = pl.BlockSpec((1, C, HALF), fwd3), pl.BlockSpec((1, C, HALF), bwd3)
    return pl.pallas_call(
        _rwkv_scan_kernel,
        out_shape=[jax.ShapeDtypeStruct((T, HALF), F32), jax.ShapeDtypeStruct((T, HALF), F32),
                   jax.ShapeDtypeStruct((2, N_SEQ, 4, LANES, LANES), F32)],
        grid_spec=pltpu.PrefetchScalarGridSpec(
            num_scalar_prefetch=5, grid=(len(tabs[0]),),
            in_specs=[one_f, one_f, one_f, two_f, two_f, two_f,
                      one_b, one_b, one_b, two_b, two_b, two_b, st],
            out_specs=[one_f, one_b, st],
            scratch_shapes=[pltpu.VMEM((2, 4, LANES, LANES), F32)]),
        compiler_params=_params(("arbitrary",)),
        name="rwkv_scan",
    )(*tabs, r, v, kk, lw, kt, b, r, v, kk, lw, kt, b, s0_bd)


def _rope(x, cos, sin_signed, first16):
    w = x.shape[1]
    partner = jnp.where(first16, pltpu.roll(x, w - 16, 1), pltpu.roll(x, 16, 1))
    return x * cos + partner * sin_signed


def _odd_prep_kernel(zc_ref, zr_ref, cos_ref, sin_ref, qkg_ref, ones_ref, cq_ref, ck_ref, ckraw_ref, rqk_ref):
    ones_bd = ones_ref[...]
    cos = cos_ref[...]
    sin = sin_ref[...]
    lane = lax.broadcasted_iota(I32, (TM, HALF), 1)
    first16 = (lane & 31) < 16
    for idx, (o_ref, raw_ref) in enumerate(((cq_ref, None), (ck_ref, ckraw_ref))):
        x = zc_ref[:, idx * HALF:(idx + 1) * HALF]
        ms = _group_sum(x * x, ones_bd) * (1.0 / HEAD)
        xn = x * lax.rsqrt(ms + 1e-6) * qkg_ref[idx:idx + 1, :]
        if raw_ref is not None:
            raw_ref[...] = xn
        o_ref[...] = _rope(xn, cos, sin, first16)
    rqk = _rope(zr_ref[...], cos, sin, first16)
    rqk_ref[...] = jnp.where(lane < HALF // 2, rqk * (HEAD ** -0.5), rqk)


def _odd_prep(zc, zr, cos_tab, sin_tab, qkg_tiled, ones_bd):
    row = lambda i: (i, 0)
    tab = lambda i: (jnp.where(i < PROMPT_TILES, 0, 1 + (i - PROMPT_TILES) % TILES_PER_SAMPLE), 0)
    one = jax.ShapeDtypeStruct((T, HALF), F32)
    o1 = pl.BlockSpec((TM, HALF), row)
    return pl.pallas_call(
        _odd_prep_kernel,
        out_shape=[one, one, one, one], grid=(NT,),
        in_specs=[pl.BlockSpec((TM, 2 * HALF), row), pl.BlockSpec((TM, HALF), row),
                  pl.BlockSpec((TM, HALF), tab), pl.BlockSpec((TM, HALF), tab),
                  _full((2, HALF)), _full((HALF, HALF))],
        out_specs=[o1, o1, o1, o1],
        compiler_params=_params(("arbitrary",)),
        name="odd_prep",
    )(zc, zr, cos_tab, sin_tab, qkg_tiled, ones_bd)


def _attn_kernel(*refs, has_ctx, one_minus_li):
    if has_ctx:
        q_ref, k_ref, v_ref, kc_ref, vc_ref, lam_ref, sg_ref, _, o_ref = refs
    else:
        q_ref, k_ref, v_ref, lam_ref, sg_ref, _, o_ref = refs
    lam = lam_ref[...]
    lane = lax.broadcasted_iota(I32, (LANES, LANES), 1)
    m0 = lane < HEAD
    scale = HEAD ** -0.5
    for h in range(4):
        cols = slice(h * LANES, (h + 1) * LANES)
        qp = q_ref[:, cols]
        segs = [(k_ref[:, cols], v_ref[:, cols])]
        if has_ctx:
            segs.append((kc_ref[0, :, cols], vc_ref[0, :, cols]))
        outs = []
        for qm in (jnp.where(m0, qp, 0.0), jnp.where(m0, 0.0, qp)):
            qm16 = qm.astype(BF16)
            ss = [lax.dot_general(qm16, ks.astype(BF16), NT_DIMS, preferred_element_type=F32) * scale
                  for ks, _ in segs]
            mx = ss[0].max(axis=-1, keepdims=True)
            for s_ in ss[1:]:
                mx = jnp.maximum(mx, s_.max(axis=-1, keepdims=True))
            ps = [jnp.exp(s_ - mx) for s_ in ss]
            den = ps[0].sum(axis=-1, keepdims=True)
            for p_ in ps[1:]:
                den = den + p_.sum(axis=-1, keepdims=True)
            outs.append([p_ / den for p_ in ps])
        acc = None
        for si, (_, vs) in enumerate(segs):
            amap = outs[0][si] - lam * outs[1][si]
            t = jnp.dot(amap.astype(BF16), vs.astype(BF16), preferred_element_type=F32)
            acc = t if acc is None else acc + t
        nrm = acc * lax.rsqrt(jnp.mean(acc * acc, axis=-1, keepdims=True) + 1e-6) * sg_ref[...]
        o_ref[:, cols] = nrm * one_minus_li


def _attn(cq, ck, zc, row0, n_seq, seq_len, lam, subln_g, one_minus_li, prev, ctx_k=None, ctx_v=None):
    nq = seq_len // LANES
    qb0 = row0 // LANES
    sb0 = row0 // seq_len
    in_specs = [pl.BlockSpec((LANES, HALF), lambda s, q: (qb0 + s * nq + q, 0)),
                pl.BlockSpec((seq_len, HALF), lambda s, q: (sb0 + s, 0)),
                pl.BlockSpec((seq_len, HALF), lambda s, q: (sb0 + s, 2))]
    args = [cq, ck, zc]
    if ctx_k is not None:
        in_specs += [pl.BlockSpec((1, PAST, HALF), lambda s, q: (s, 0, 0))] * 2
        args += [ctx_k, ctx_v]
    in_specs += [_full((1, 1)), _full((1, LANES))]
    args += [lam.reshape(1, 1), subln_g.reshape(1, LANES)]
    in_specs.append(pl.BlockSpec(memory_space=pl.ANY))
    args.append(prev)
    aliases = {len(args) - 1: 0}
    return pl.pallas_call(
        functools.partial(_attn_kernel, has_ctx=ctx_k is not None, one_minus_li=one_minus_li),
        out_shape=jax.ShapeDtypeStruct((T, HALF), F32),
        grid=(n_seq, nq), in_specs=in_specs,
        out_specs=pl.BlockSpec((LANES, HALF), lambda s, q: (qb0 + s * nq + q, 0)),
        input_output_aliases=aliases,
        compiler_params=_params(("arbitrary", "arbitrary"), 48),
        name="diff_attn",
    )(*args)


_LOG_GAMMA = tuple(tuple(float(np.log1p(-np.exp2(-np.float32(e)), dtype=np.float32)) for e in es)
                   for es in RET_EXP)


def _ret_chunks(dirs):
    C = RET_CHUNK
    ii = lax.broadcasted_iota(I32, (C, C), 0)
    jj = lax.broadcasted_iota(I32, (C, C), 1)
    ri = lax.broadcasted_iota(I32, (C, 1), 0)
    lane = lax.broadcasted_iota(I32, (C, LANES), 1)
    chains = []
    for rev, qk_ref, v_ref, s_ref, o_ref in dirs:
        mask = (jj > ii) if rev else (jj <= ii)
        dist = jnp.where(mask, (jj - ii) if rev else (ii - jj), 0).astype(F32)
        kpow = (ri if rev else (C - 1 - ri)).astype(F32)
        qpow = ((C - ri) if rev else (ri + 1)).astype(F32)
        for h in range(4):
            lg = _LOG_GAMMA[1 if rev else 0][h]
            p = h // 2
            hm = (lane < HEAD) if h % 2 == 0 else (lane >= HEAD)
            qp = jnp.where(hm, qk_ref[:, p * LANES:(p + 1) * LANES], 0.0)
            kp = jnp.where(hm, qk_ref[:, HALF // 2 + p * LANES:HALF // 2 + (p + 1) * LANES], 0.0)
            chains.append(dict(
                h=h, lg=lg, s_ref=s_ref, o_ref=o_ref, q16=qp.astype(BF16), k16=kp.astype(BF16),
                qw16=(qp * jnp.exp(lg * qpow)).astype(BF16), kw16=(kp * jnp.exp(lg * kpow)).astype(BF16),
                v16=v_ref[:, h * LANES:(h + 1) * LANES].astype(BF16),
                decay=jnp.where(mask, jnp.exp(lg * dist), 0.0)))
    for ch in chains:
        ch["sc"] = (lax.dot_general(ch["q16"], ch["k16"], NT_DIMS, preferred_element_type=F32)
                    * ch["decay"]).astype(BF16)
        ch["s"] = ch["s_ref"][ch["h"]]
    for ch in chains:
        ch["o"] = (jnp.dot(ch["sc"], ch["v16"], preferred_element_type=F32)
                   + jnp.dot(ch["qw16"], ch["s"].astype(BF16), preferred_element_type=F32))
        ch["kv"] = lax.dot_general(ch["kw16"], ch["v16"], TN, preferred_element_type=F32)
    for ch in chains:
        h = ch["h"]
        ch["o_ref"][:, h * LANES:(h + 1) * LANES] = ch["o"]
        ch["s_ref"][h] = math.exp(ch["lg"] * C) * ch["s"] + ch["kv"]


def _ret_kernel(bf_ref, bb_ref, first_ref, last_ref, seq_ref,
                qkf_ref, vf_ref, qkb_ref, vb_ref, s0_ref, of_ref, ob_ref, sfin_ref, s_ref):
    step = pl.program_id(0)

    @pl.when(first_ref[step] == 1)
    def _():
        s_ref[...] = s0_ref[:, 0]

    _ret_chunks([(False, qkf_ref, vf_ref, s_ref.at[0], of_ref), (True, qkb_ref, vb_ref, s_ref.at[1], ob_ref)])

    @pl.when(last_ref[step] == 1)
    def _():
        sfin_ref[:, 0] = s_ref[...]


def _retention(rqk, zr, s0):
    C = RET_CHUNK
    tabs = _seq_tables(C)
    st = pl.BlockSpec((2, 1, 4, LANES, LANES), lambda i, bf, bb, fi, la, sq: (0, sq[i], 0, 0, 0))
    spec = lambda use_b, col: pl.BlockSpec(
        (C, HALF), lambda i, bf, bb, fi, la, sq: ((bb if use_b else bf)[i], col))
    return pl.pallas_call(
        _ret_kernel,
        out_shape=[jax.ShapeDtypeStruct((T, HALF), F32), jax.ShapeDtypeStruct((T, HALF), F32),
                   jax.ShapeDtypeStruct((2, N_SEQ, 4, LANES, LANES), F32)],
        grid_spec=pltpu.PrefetchScalarGridSpec(
            num_scalar_prefetch=5, grid=(len(tabs[0]),),
            in_specs=[spec(False, 0), spec(False, 1), spec(True, 0), spec(True, 1), st],
            out_specs=[spec(False, 0), spec(True, 0), st],
            scratch_shapes=[pltpu.VMEM((2, 4, LANES, LANES), F32)]),
        compiler_params=_params(("arbitrary",)),
        name="retention",
    )(*tabs, rqk, zr, rqk, zr, s0)


def _out_kernel(*refs, even):
    if even:
        (a_ref, yf_ref, yb_ref, bonus_ref, g_ref, gng_ref, gnb_ref, ones_ref,
         x_ref, mod_ref, ng_ref, wo_ref, rw_ref, rb_ref,
         y_ref, xp_ref, ti_ref, tg_ref, rk_ref, cnt_ref, run_ref) = refs
        ones_bd = ones_ref[...]
        ys = yf_ref[...] + yb_ref[...]
        mu = _group_sum(ys, ones_bd) * (1.0 / HEAD)
        dv = ys - mu
        var = _group_sum(dv * dv, ones_bd) * (1.0 / HEAD)
        yn = dv * lax.rsqrt(var + RWKV_GN_EPS) * gng_ref[...] + gnb_ref[...]
        left = a_ref[...]
        right = (yn + bonus_ref[...]) * g_ref[...]
    else:
        (c_ref, of_ref, ob_ref, rg_ref, gng_ref,
         x_ref, mod_ref, ng_ref, wo_ref, rw_ref, rb_ref,
         y_ref, xp_ref, ti_ref, tg_ref, rk_ref, cnt_ref, run_ref) = refs
        left = c_ref[...]
        rg = rg_ref[...]
        gate = rg * jax.nn.sigmoid(rg)
        os_ = of_ref[...] + ob_ref[...]
        parts = []
        for h in range(4):
            oh = os_[:, h * LANES:(h + 1) * LANES]
            mu = jnp.mean(oh, axis=-1, keepdims=True)
            dv = oh - mu
            var = jnp.mean(dv * dv, axis=-1, keepdims=True)
            parts.append(dv * lax.rsqrt(var + 1e-5))
        right = gate * (jnp.concatenate(parts, axis=1) * gng_ref[...])
    mod = mod_ref[0]
    o = (jnp.dot(left.astype(BF16), wo_ref[0:HALF, :], preferred_element_type=F32)
         + jnp.dot(right.astype(BF16), wo_ref[HALF:2 * HALF, :], preferred_element_type=F32))
    y = x_ref[...] + mod[:, 2 * D:3 * D] * o
    y_ref[...] = y
    yn2 = y * lax.rsqrt(jnp.mean(y * y, axis=-1, keepdims=True) + 1e-6) * ng_ref[...]
    t = yn2 * (1.0 + mod[:, 4 * D:5 * D]) + mod[:, 3 * D:4 * D]
    xp_ref[...] = t
    logits = _mm(t, rw_ref[...], NN, 3) + rb_ref[...]
    lane = lax.broadcasted_iota(I32, (TM, LANES), 1)
    neg = jnp.float32(-jnp.inf)
    lg = jnp.where(lane < N_EXPERTS, logits, neg)
    vals, hits = [], []
    for _ in range(TOP_K):
        m = jnp.max(lg, axis=-1, keepdims=True)
        ix = jnp.min(jnp.where(lg == m, lane, LANES), axis=-1, keepdims=True)
        hit = lane == ix
        vals.append(m)
        hits.append((ix, hit))
        lg = jnp.where(hit, neg, lg)
    es = [jnp.exp(vv - vals[0]) for vv in vals]
    den = es[0] + es[1] + es[2] + es[3]

    @pl.when(pl.program_id(0) == 0)
    def _():
        run_ref[...] = jnp.zeros_like(run_ref)

    member = jnp.zeros((TM, LANES), F32)
    for _, hit in hits:
        member = member + jnp.where(hit, 1.0, 0.0)
    ri = lax.broadcasted_iota(I32, (TM, TM), 0)
    ci = lax.broadcasted_iota(I32, (TM, TM), 1)
    before = jnp.where(ci < ri, 1.0, 0.0).astype(BF16)
    seen = run_ref[...] + jnp.dot(before, member.astype(BF16), preferred_element_type=F32)
    ti = jnp.zeros((TM, LANES), I32)
    tg = jnp.zeros((TM, LANES), F32)
    rk = jnp.zeros((TM, LANES), F32)
    for kk, (ix, hit) in enumerate(hits):
        ti = jnp.where(lane == kk, ix, ti)
        tg = jnp.where(lane == kk, es[kk] / den, tg)
        rk = jnp.where(lane == kk, jnp.sum(jnp.where(hit, seen, 0.0), axis=-1, keepdims=True), rk)
    ti_ref[...] = ti
    tg_ref[...] = tg
    rk_ref[...] = rk.astype(I32)
    run_ref[...] = run_ref[...] + jnp.sum(member, axis=0, keepdims=True)
    cnt_ref[...] = run_ref[...]


def _out_proj(even, mix_args, mix_specs, x, mod, norm_g, w_out_bf16, rw_pad, rb_pad):
    row = lambda i: (i, 0)
    modspec = pl.BlockSpec((1, 1, N_MOD * D), lambda i: (_group(i), 0, 0))
    in_specs = list(mix_specs) + [pl.BlockSpec((TM, D), row), modspec, _full((1, D)), _full((D, D)),
                                  _full((D, LANES)), _full((1, LANES))]
    args = list(mix_args) + [x, mod, norm_g.reshape(1, D), w_out_bf16, rw_pad, rb_pad]
    lane_i = jax.ShapeDtypeStruct((T, LANES), I32)
    lane_spec = pl.BlockSpec((TM, LANES), row)
    return pl.pallas_call(
        functools.partial(_out_kernel, even=even),
        out_shape=[jax.ShapeDtypeStruct((T, D), F32), jax.ShapeDtypeStruct((T, D), F32),
                   lane_i, jax.ShapeDtypeStruct((T, LANES), F32), lane_i,
                   jax.ShapeDtypeStruct((1, LANES), F32)],
        grid=(NT,), in_specs=in_specs,
        out_specs=[pl.BlockSpec((TM, D), row), pl.BlockSpec((TM, D), row),
                   lane_spec, lane_spec, lane_spec, _full((1, LANES))],
        scratch_shapes=[pltpu.VMEM((1, LANES), F32)],
        compiler_params=_params(("arbitrary",), 48),
        name="out_proj",
    )(*args)


def _route_kernel(cnt_ref, ti_ref, rk_ref, dest_ref, te_ref, nt_ref):
    cnt = cnt_ref[...].astype(I32)
    ntile = lax.shift_right_logical(cnt + (TMX - 1), TMX.bit_length() - 1)
    ei = lax.broadcasted_iota(I32, (LANES, LANES), 0)
    ej = lax.broadcasted_iota(I32, (LANES, LANES), 1)
    upto = jnp.where(ei <= ej, 1.0, 0.0).astype(BF16)
    ntile_f = jnp.broadcast_to(ntile.astype(F32), (8, LANES))
    tile_end = jnp.dot(ntile_f.astype(BF16), upto, preferred_element_type=F32)[0:1, :]
    row_start = (tile_end - ntile.astype(F32)) * float(TMX)
    lane = lax.broadcasted_iota(I32, (TM, LANES), 1)
    ti = ti_ref[...]
    rk = rk_ref[...]
    spread = jnp.zeros((TM, LANES), F32)
    for k in range(TOP_K):
        hit = lane == ti[:, k:k + 1]
        start = jnp.sum(jnp.where(hit, row_start, 0.0), axis=-1, keepdims=True)
        spread = jnp.where((lane & (TOP_K - 1)) == k, start + rk[:, k:k + 1].astype(F32), spread)
    tok = lax.broadcasted_iota(I32, (TM, LANES), 0)
    keep = (tok & (LANES // TOP_K - 1)) == lax.shift_right_logical(lane, 2)
    flat = jnp.where(keep, spread, 0.0).reshape(TM * TOP_K // LANES, LANES // TOP_K, LANES).sum(axis=1)
    dest_ref[...] = flat.astype(I32)

    @pl.when(pl.program_id(0) == 0)
    def _():
        lane1 = lax.broadcasted_iota(I32, (1, LANES), 1)
        n_tiles = jnp.max(tile_end, axis=-1, keepdims=True)
        last_e = jnp.max(jnp.where(cnt > 0, lane1, 0), axis=-1, keepdims=True)
        tile = lax.broadcasted_iota(I32, (TM, 1), 0).astype(F32)
        te = jnp.sum(jnp.where(tile_end <= tile, 1, 0), axis=-1, keepdims=True)
        te = jnp.where(tile < n_tiles, te, last_e)
        te_ref[...] = jnp.broadcast_to(te, (TM, LANES)).astype(I32)
        first_row = lax.broadcasted_iota(I32, (8, LANES), 0) == 0
        nt_ref[...] = jnp.where(first_row, n_tiles, tile_end).astype(I32)


def _route(cnt, ti, rk):
    row = lambda i: (i, 0)
    return pl.pallas_call(
        _route_kernel,
        out_shape=[jax.ShapeDtypeStruct((N_ASSIGN // LANES, LANES), I32), jax.ShapeDtypeStruct((TM, LANES), I32),
                   jax.ShapeDtypeStruct((8, LANES), I32)],
        grid=(NT,),
        in_specs=[_full((1, LANES)), pl.BlockSpec((TM, LANES), row), pl.BlockSpec((TM, LANES), row)],
        out_specs=[pl.BlockSpec((TM * TOP_K // LANES, LANES), row), _full((TM, LANES)), _full((8, LANES))],
        compiler_params=_params(("arbitrary",)),
        name="moe_route",
    )(cnt, ti, rk)


def _start_row_copy(src_ref, src_row, dst_ref, dst_row, sem, queue):
    pltpu.async_copy(src_ref.at[pl.ds(src_row, 1)], dst_ref.at[pl.ds(dst_row, 1)], sem, priority=queue)


def _wait_tiles(n, src_ref, dst_ref, sem):
    for _ in range(n):
        pltpu.make_async_copy(src_ref, dst_ref, sem).wait()


def _dispatch_kernel(dest_ref, tend_ref, x_ref, xs_ref, zero_ref, sem):
    i = pl.program_id(0)

    @pl.when(i == 0)
    def _():
        zero_ref[...] = jnp.zeros_like(zero_ref)

        def last_tile(e, fn):
            end = tend_ref[e]
            begin = tend_ref[e - 1] if e > 0 else 0

            @pl.when(end > begin)
            def _():
                fn(pltpu.make_async_copy(zero_ref, xs_ref.at[pl.ds((end - 1) * TMX, TMX)], sem))

        def unused_tile(j):
            return pltpu.make_async_copy(zero_ref, xs_ref.at[pl.ds(j * TMX, TMX)], sem)

        def start_unused(j, carry):
            unused_tile(j).start()
            return carry

        def wait_unused(j, carry):
            unused_tile(j).wait()
            return carry

        n_used = tend_ref[N_EXPERTS - 1]
        for e in range(N_EXPERTS):
            last_tile(e, lambda c: c.start())
        lax.fori_loop(n_used, MOE_TILES, start_unused, 0)
        for e in range(N_EXPERTS):
            last_tile(e, lambda c: c.wait())
        lax.fori_loop(n_used, MOE_TILES, wait_unused, 0)

    base = i * (TM * TOP_K)

    def start(r, carry):
        for k in range(TOP_K):
            _start_row_copy(x_ref, r, xs_ref, dest_ref[base + r * TOP_K + k], sem, k % 2)
        return carry

    lax.fori_loop(0, TM, start, 0, unroll=4)
    _wait_tiles(TOP_K, x_ref, xs_ref.at[pl.ds(0, TM)], sem)


def _dispatch(dest_flat, tile_end, xt):
    return pl.pallas_call(
        _dispatch_kernel,
        out_shape=jax.ShapeDtypeStruct((R_PAD, D), F32),
        grid_spec=pltpu.PrefetchScalarGridSpec(
            num_scalar_prefetch=2, grid=(NT,),
            in_specs=[pl.BlockSpec((TM, D), lambda i, d, te: (i, 0))],
            out_specs=pl.BlockSpec(memory_space=pl.ANY),
            scratch_shapes=[pltpu.VMEM((TMX, D), F32), pltpu.SemaphoreType.DMA(())]),
        compiler_params=_params(("arbitrary",)),
        name="moe_dispatch",
    )(dest_flat, tile_end, xt)


W_PARTS = 8


def _expert_weights(i, nt, te_ref, w_ref, wbuf_ref, wsem, w16_ref, group_ref):
    rows = w_ref.shape[1] // W_PARTS

    def fetch(e, buf):
        return [pltpu.make_async_copy(w_ref.at[e, pl.ds(p * rows, rows)], wbuf_ref.at[buf, pl.ds(p * rows, rows)],
                                      wsem.at[buf]) for p in range(W_PARTS)]

    @pl.when(i == 0)
    def _():
        group_ref[0] = 0
        for c in fetch(te_ref[0], 0):
            c.start()

    first = jnp.logical_or(i == 0, te_ref[i] != te_ref[jnp.maximum(i - 1, 0)])

    @pl.when(jnp.logical_and(first, i < nt))
    def _():
        cur = group_ref[0] % 2
        nxt = lax.while_loop(
            lambda j: jnp.logical_and(j < nt, te_ref[jnp.minimum(j, MOE_TILES - 1)] == te_ref[i]),
            lambda j: j + 1, i + 1)

        @pl.when(nxt < nt)
        def _():
            for c in fetch(te_ref[jnp.minimum(nxt, MOE_TILES - 1)], 1 - cur):
                c.start()

        for c in fetch(0, cur):
            c.wait()
        w16_ref[...] = wbuf_ref[cur].astype(BF16)
        group_ref[0] = group_ref[0] + 1


def _experts_kernel(te_ref, nt_ref, xs_ref, wgu_ref, bgu_ref, wdn_ref, bdn_ref, y_ref,
                    gu_buf, gu_sem, gu16_ref, gu_group, dn_buf, dn_sem, dn16_ref, dn_group):
    i = pl.program_id(0)
    nt = nt_ref[0]
    _expert_weights(i, nt, te_ref, wgu_ref, gu_buf, gu_sem, gu16_ref, gu_group)
    _expert_weights(i, nt, te_ref, wdn_ref, dn_buf, dn_sem, dn16_ref, dn_group)

    @pl.when(i < nt)
    def _():
        x16 = xs_ref[...].astype(BF16)
        y = bdn_ref[0]
        for h in range(2):
            gc = slice(h * HALF, (h + 1) * HALF)
            uc = slice(D + h * HALF, D + (h + 1) * HALF)
            g = jnp.dot(x16, gu16_ref[:, gc], preferred_element_type=F32) + bgu_ref[0, :, gc]
            u = jnp.dot(x16, gu16_ref[:, uc], preferred_element_type=F32) + bgu_ref[0, :, uc]
            gt = jnp.minimum(g, SWIGLU_LIMIT)
            up = jnp.clip(u, -SWIGLU_LIMIT, SWIGLU_LIMIT)
            act = ((up + 1.0) * gt * jax.nn.sigmoid(SWIGLU_ALPHA * gt)).astype(BF16)
            y = y + jnp.dot(act, dn16_ref[gc, :], preferred_element_type=F32)
        y_ref[...] = y

    @pl.when(i >= nt)
    def _():
        y_ref[...] = jnp.zeros_like(y_ref)


def _tile_clamped(i, te, nt):
    return (jnp.minimum(i, jnp.maximum(nt[0] - 1, 0)), 0)


def _weight_scratch(n_out):
    return [pltpu.VMEM((2, D, n_out), F32), pltpu.SemaphoreType.DMA((2,)), pltpu.VMEM((D, n_out), BF16),
            pltpu.SMEM((1,), I32)]


def _experts(te, n_tiles, xs, w_gu, b_gu, w_dn, b_dn):
    return pl.pallas_call(
        _experts_kernel,
        out_shape=jax.ShapeDtypeStruct((R_PAD, D), F32),
        grid_spec=pltpu.PrefetchScalarGridSpec(
            num_scalar_prefetch=2, grid=(MOE_TILES,),
            in_specs=[pl.BlockSpec((TMX, D), _tile_clamped), pl.BlockSpec(memory_space=pl.ANY),
                      pl.BlockSpec((1, 1, 2 * D), lambda i, te, nt: (te[i], 0, 0)),
                      pl.BlockSpec(memory_space=pl.ANY),
                      pl.BlockSpec((1, 1, D), lambda i, te, nt: (te[i], 0, 0))],
            out_specs=pl.BlockSpec((TMX, D), lambda i, te, nt: (i, 0)),
            scratch_shapes=_weight_scratch(2 * D) + _weight_scratch(D)),
        compiler_params=_params(("arbitrary",), 58),
        name="moe_experts",
    )(te, n_tiles, xs, w_gu, b_gu, w_dn, b_dn)


def _combine_kernel(dest_ref, x_ref, tg_ref, mod_ref, ys_ref, o_ref, buf_ref, sem):
    i = pl.program_id(0)
    slot = i % 2

    def gather(tile, b):
        base = tile * (TM * TOP_K)

        def body(r, carry):
            for k in range(TOP_K):
                _start_row_copy(ys_ref, dest_ref[base + r * TOP_K + k], buf_ref.at[b, k], r, sem.at[b], k % 2)
            return carry

        lax.fori_loop(0, TM, body, 0, unroll=4)

    @pl.when(i == 0)
    def _():
        gather(0, 0)

    @pl.when(i + 1 < NT)
    def _():
        gather(i + 1, 1 - slot)

    _wait_tiles(TOP_K, ys_ref.at[pl.ds(0, TM)], buf_ref.at[slot, 0], sem.at[slot])
    tg = tg_ref[...]
    f = tg[:, 0:1] * buf_ref[slot, 0]
    for k in range(1, TOP_K):
        f = f + tg[:, k:k + 1] * buf_ref[slot, k]
    o_ref[...] = x_ref[...] + mod_ref[0][:, 5 * D:6 * D] * f


def _combine(dest_flat, x, tg, mod, ys):
    row = lambda i, d: (i, 0)
    return pl.pallas_call(
        _combine_kernel,
        out_shape=jax.ShapeDtypeStruct((T, D), F32),
        grid_spec=pltpu.PrefetchScalarGridSpec(
            num_scalar_prefetch=1, grid=(NT,),
            in_specs=[pl.BlockSpec((TM, D), row), pl.BlockSpec((TM, LANES), row),
                      pl.BlockSpec((1, 1, N_MOD * D), lambda i, d: (_group(i), 0, 0)),
                      pl.BlockSpec(memory_space=pl.ANY)],
            out_specs=pl.BlockSpec((TM, D), row),
            scratch_shapes=[pltpu.VMEM((2, TOP_K, TM, D), F32), pltpu.SemaphoreType.DMA((2,))]),
        compiler_params=_params(("arbitrary",), 40),
        name="moe_combine",
    )(dest_flat, x, tg, mod, ys)


def _moe(layer, y, xt, ti, tg, rk, cnt, mod, w_gu, b_gu, w_dn, b_dn):
    dest, te, nt = _route(cnt, ti, rk)
    dest_flat = dest.reshape(-1)
    te = te[:MOE_TILES, 0] + layer * N_EXPERTS
    n_tiles = nt[0, :1]
    xs = _dispatch(dest_flat, nt[1, :N_EXPERTS], xt)
    n_all = w_gu.shape[0] * N_EXPERTS
    ys = _experts(te, n_tiles, xs, w_gu.reshape(n_all, D, 2 * D), b_gu.reshape(n_all, 1, 2 * D),
                  w_dn.reshape(n_all, D, D), b_dn.reshape(n_all, 1, D))
    return _combine(dest_flat, y, tg, mod, ys)


def _ones_blockdiag():
    idx = np.arange(HALF) // HEAD
    return jnp.asarray((idx[:, None] == idx[None, :]).astype(np.float32), dtype=BF16)


def _rope_tables():
    pos = jnp.arange(L_SAMPLE)
    rowp = (pos // 64).astype(F32)
    colp = (pos % 64).astype(F32)
    nf = HEAD // 4
    inv = jnp.power(10000.0, -jnp.arange(nf, dtype=F32) / nf)
    ar = rowp[:, None] * inv[None, :]
    ac = colp[:, None] * inv[None, :]
    cos64 = jnp.concatenate([jnp.cos(ar), jnp.cos(ar), jnp.cos(ac), jnp.cos(ac)], axis=1)
    sin64 = jnp.concatenate([-jnp.sin(ar), jnp.sin(ar), -jnp.sin(ac), jnp.sin(ac)], axis=1)
    cos = jnp.tile(cos64, (1, HALF // HEAD))
    sin = jnp.tile(sin64, (1, HALF // HEAD))
    ident = jnp.ones((TM, HALF), F32)
    return (jnp.concatenate([ident, cos], axis=0), jnp.concatenate([jnp.zeros((TM, HALF), F32), sin], axis=0))


def _bd_pairs(s):
    lead = s.shape[:-3]
    s = s.reshape(lead + (4, 2, HEAD, HEAD))
    z = jnp.zeros_like(s[..., 0, :, :])
    top = jnp.concatenate([s[..., 0, :, :], z], axis=-1)
    bot = jnp.concatenate([z, s[..., 1, :, :]], axis=-1)
    return jnp.concatenate([top, bot], axis=-2)


def _bd_unpairs(s):
    a = s[..., 0:HEAD, 0:HEAD]
    b = s[..., HEAD:, HEAD:]
    out = jnp.stack([a, b], axis=-3)
    return out.reshape(s.shape[:-3] + (8, HEAD, HEAD))


def kernel(x_prompt, x_sample, state_rwkv, cache_k_diff, cache_v_diff, state_retention, c, c_ctx, norm_g, ada_w, ada_b, e_w_in, e_w_out, sgu_ln_g, sgu_w, sgu_b, rw_mu, rw_w0, rw_w_up, rw_a0, rw_a_up, rw_g_up, rw_k_k, rw_k_a, rw_r_k, rw_gn_g, rw_gn_b, o_w_in, o_w_out, da_qk_g, da_lam, da_subln_g, ret_gn_g, router_w, router_b, ex_w_gu, ex_b_gu, ex_w_dn, ex_b_dn):
    x = jnp.concatenate([x_prompt.reshape(T_PROMPT, D), x_sample.reshape(T_SAMPLE, D)], axis=0)
    cvec8 = jnp.concatenate([c_ctx[None, :], c, jnp.zeros((3, D), F32)], axis=0)
    mods = _adaln(cvec8, ada_w, ada_b)
    mod0 = mods[0].reshape(8, 1, N_MOD * D)
    mod1 = mods[1].reshape(8, 1, N_MOD * D)
    ones_bd = _ones_blockdiag()
    rw_pad = jnp.pad(router_w, ((0, 0), (0, 0), (0, LANES - N_EXPERTS)))
    rb_pad = jnp.pad(router_b, ((0, 0), (0, LANES - N_EXPERTS))).reshape(2, 1, LANES)
    row = lambda i: (i, 0)
    half = pl.BlockSpec((TM, HALF), row)

    za, zb = _in_proj(x, norm_g[0, 0], mod0, e_w_in[0].astype(BF16), (2 * HALF, B_COLS))
    bs_full = jnp.repeat(sgu_b[0].T, HEAD, axis=1)
    a_out = _sgu(za, sgu_ln_g[0], sgu_w[0].astype(BF16), bs_full)
    zpad = jnp.zeros((2, HEAD, HALF), F32)
    wup_pad = jnp.concatenate([rw_w_up[0], zpad], axis=1).astype(BF16)
    aup_pad = jnp.concatenate([zpad, rw_a_up[0]], axis=1).astype(BF16)
    r, v, kkn, bonus, g, lw, kt, b = _rwkv_prep(zb, rw_mu[0], rw_k_k[0], rw_k_a[0], rw_r_k[0], rw_w0[0], rw_a0[0],
                                                wup_pad, aup_pad, rw_g_up[0].astype(BF16), ones_bd)
    s0_sample = _bd_pairs(jnp.moveaxis(state_rwkv[:, 0], 1, 0))
    s0_rw = jnp.concatenate([jnp.zeros((2, N_PROMPT, 4, LANES, LANES), F32), s0_sample], axis=1)
    yf_rw, yb_rw, sfin_rw = _rwkv_scan(r, v, kkn, lw, kt, b, s0_rw)
    new_rwkv = jnp.moveaxis(_bd_unpairs(sfin_rw[:, :N_PROMPT]), 0, 1)[:, None]
    y0, xp0, ti0, tg0, rk0, cnt0 = _out_proj(
        True,
        [a_out, yf_rw, yb_rw, bonus, g, rw_gn_g[0].reshape(1, HALF), rw_gn_b[0].reshape(1, HALF), ones_bd],
        [half, half, half, half, half, _full((1, HALF)), _full((1, HALF)), _full((HALF, HALF))],
        x, mod0, norm_g[0, 1], e_w_out[0].astype(BF16), rw_pad[0], rb_pad[0])
    x1 = _moe(0, y0, xp0, ti0, tg0, rk0, cnt0, mod0, ex_w_gu, ex_b_gu, ex_w_dn, ex_b_dn)

    zc, zr = _in_proj(x1, norm_g[1, 0], mod1, o_w_in[0].astype(BF16), (3 * HALF, 3 * HALF))
    cos_tab, sin_tab = _rope_tables()
    qkg = jnp.tile(da_qk_g[0], (1, HALF // HEAD))
    cq, ck, ck_raw, rqk = _odd_prep(zc, zr, cos_tab, sin_tab, qkg, ones_bd)
    lambda_init = 0.8 - 0.6 * math.exp(-0.3 * 1)
    lv = da_lam[0]
    lam = jnp.exp(jnp.sum(lv[0] * lv[1])) - jnp.exp(jnp.sum(lv[2] * lv[3])) + lambda_init
    c_out = _attn(cq, ck, zc, 0, N_PROMPT, L_PROMPT, lam, da_subln_g[0], 1.0 - lambda_init,
                  jnp.zeros((T, HALF), F32))
    ctx_k = cache_k_diff[:, 0].reshape(N_SAMPLE, PAST, HALF)
    ctx_v = cache_v_diff[:, 0].reshape(N_SAMPLE, PAST, HALF)
    c_out = _attn(cq, ck, zc, T_PROMPT, N_SAMPLE, L_SAMPLE, lam, da_subln_g[0], 1.0 - lambda_init,
                  c_out, ctx_k, ctx_v)
    sr = jnp.moveaxis(state_retention[:, 0], 1, 0)
    zr0 = jnp.zeros_like(sr)
    s0_sample = jnp.stack([jnp.concatenate([sr[:, :, 0], zr0[:, :, 0]], axis=-2),
                           jnp.concatenate([zr0[:, :, 1], sr[:, :, 1]], axis=-2),
                           jnp.concatenate([sr[:, :, 2], zr0[:, :, 2]], axis=-2),
                           jnp.concatenate([zr0[:, :, 3], sr[:, :, 3]], axis=-2)], axis=2)
    s0_ret = jnp.concatenate([jnp.zeros((2, N_PROMPT, 4, LANES, LANES), F32), s0_sample], axis=1)
    of_ret, ob_ret, rfin = _retention(rqk, zr, s0_ret)
    rfin_p = rfin[:, :N_PROMPT]
    new_ret = jnp.stack([rfin_p[:, :, 0, 0:HEAD], rfin_p[:, :, 1, HEAD:], rfin_p[:, :, 2, 0:HEAD],
                         rfin_p[:, :, 3, HEAD:]], axis=2)
    new_ret = jnp.moveaxis(new_ret, 0, 1)[:, None]
    y1, xp1, ti1, tg1, rk1, cnt1 = _out_proj(
        False,
        [c_out, of_ret, ob_ret, zr, ret_gn_g[0].reshape(1, HALF)],
        [half, half, half, pl.BlockSpec((TM, HALF), lambda i: (i, 2)), _full((1, HALF))],
        x1, mod1, norm_g[1, 1], o_w_out[0].astype(BF16), rw_pad[1], rb_pad[1])
    y_fin = _moe(1, y1, xp1, ti1, tg1, rk1, cnt1, mod1, ex_w_gu, ex_b_gu, ex_w_dn, ex_b_dn)

    new_k = ck_raw[:T_PROMPT].reshape(N_PROMPT, 1, L_PROMPT, 4, LANES)
    new_v = zc[:T_PROMPT, 2 * HALF:3 * HALF].reshape(N_PROMPT, 1, L_PROMPT, 4, LANES)
    return (y_fin[:T_PROMPT].reshape(N_PROMPT, L_PROMPT, D), y_fin[T_PROMPT:].reshape(N_SAMPLE, L_SAMPLE, D),
            new_rwkv, new_k, new_v, new_ret)
```

```python
import functools
import math

import numpy as np
import jax
import jax.numpy as jnp
from jax import lax
from jax.experimental import pallas as pl
from jax.experimental.pallas import tpu as pltpu

F32 = jnp.float32
BF16 = jnp.bfloat16
I32 = jnp.int32

D = 1024
N_PROMPT, L_PROMPT = 16, 256
N_SAMPLE, L_SAMPLE = 4, 1024
N_SEQ = N_PROMPT + N_SAMPLE
PAST = 256
T_PROMPT = N_PROMPT * L_PROMPT
T_SAMPLE = N_SAMPLE * L_SAMPLE
T = T_PROMPT + T_SAMPLE
TM = 256
NT = T // TM
PROMPT_TILES = T_PROMPT // TM
TILES_PER_SAMPLE = L_SAMPLE // TM
N_MOD = 6
HALF = 512
B_COLS = 1792
HEAD = 64
W_DECAY_SCALE = math.exp(-0.5)
RWKV_GN_EPS = 64e-5
RW_CHUNK = 64
RW_STEP = 128
RET_CHUNK = 128
RET_EXP = ((5.0, 7.0, 9.0, 11.0), (6.0, 8.0, 10.0, 12.0))
N_EXPERTS = 32
TOP_K = 4
SWIGLU_LIMIT = 7.0
SWIGLU_ALPHA = 1.702
N_ASSIGN = T * TOP_K
TMX = 512
MOE_TILES = N_ASSIGN // TMX + N_EXPERTS
R_PAD = MOE_TILES * TMX
LANES = 128

NN = (((1,), (0,)), ((), ()))
NT_DIMS = (((1,), (1,)), ((), ()))
TN = (((0,), (0,)), ((), ()))


def _group(i):
    return jnp.where(i < PROMPT_TILES, 0, 1 + (i - PROMPT_TILES) // TILES_PER_SAMPLE)


def _mm(a, b, dims=NN, passes=1):
    dg = functools.partial(lax.dot_general, dimension_numbers=dims, preferred_element_type=F32)
    if passes == 1:
        return dg(a.astype(BF16), b.astype(BF16))
    a = a.astype(F32)
    b = b.astype(F32)
    ah = a.astype(BF16)
    al = (a - ah.astype(F32)).astype(BF16)
    bh = b.astype(BF16)
    if passes == 2:
        assert dims[0][0] == (1,)
        m = a.shape[0]
        both = dg(jnp.concatenate([ah, al], axis=0), bh)
        return both[0:m] + both[m:2 * m]
    bl = (b - bh.astype(F32)).astype(BF16)
    if dims[0][0] == (1,):
        m = a.shape[0]
        both = dg(jnp.concatenate([ah, al], axis=0), bh)
        return both[0:m] + (dg(ah, bl) + both[m:2 * m])
    return dg(ah, bh) + (dg(ah, bl) + dg(al, bh))


def _group_sum(x, ones_bd):
    xh = x.astype(BF16)
    xl = (x - xh.astype(F32)).astype(BF16)
    return (jnp.dot(xh, ones_bd, preferred_element_type=F32)
            + jnp.dot(xl, ones_bd, preferred_element_type=F32))


def _full(shape):
    nd = len(shape)
    return pl.BlockSpec(shape, lambda *_: (0,) * nd)


def _params(sem, vmem_mb=None):
    kw = {}
    if vmem_mb is not None:
        kw["vmem_limit_bytes"] = vmem_mb * 1024 * 1024
    return pltpu.CompilerParams(dimension_semantics=sem, **kw)


def _seq_tables(chunk):
    blk_f, blk_b, first, last, seq = [], [], [], [], []
    row = 0
    for s in range(N_SEQ):
        n = (L_PROMPT if s < N_PROMPT else L_SAMPLE) // chunk
        base = row // chunk
        for j in range(n):
            blk_f.append(base + j)
            blk_b.append(base + n - 1 - j)
            first.append(int(j == 0))
            last.append(int(j == n - 1))
            seq.append(s)
        row += n * chunk
    return tuple(np.asarray(a, np.int32) for a in (blk_f, blk_b, first, last, seq))


def _adaln_kernel(c_ref, w_ref, b_ref, o_ref):
    c = c_ref[...]
    s = c * jax.nn.sigmoid(c)
    o_ref[0] = _mm(s, w_ref[0], NN, 3) + b_ref[0]


def _adaln(cvec8, ada_w, ada_b):
    depth, _, n = ada_w.shape
    bn = 1536
    return pl.pallas_call(
        _adaln_kernel,
        out_shape=jax.ShapeDtypeStruct((depth, 8, n), F32),
        grid=(depth, n // bn),
        in_specs=[pl.BlockSpec((8, D), lambda l, j: (0, 0)),
                  pl.BlockSpec((1, D, bn), lambda l, j: (l, 0, j)),
                  pl.BlockSpec((1, 1, bn), lambda l, j: (l, 0, j))],
        out_specs=pl.BlockSpec((1, 8, bn), lambda l, j: (l, 0, j)),
        compiler_params=_params(("arbitrary", "arbitrary"), 40),
        name="adaln",
    )(cvec8, ada_w, ada_b.reshape(depth, 1, n))


def _in_kernel(x_ref, g_ref, mod_ref, w_ref, *outs, splits):
    x = x_ref[...]
    mod = mod_ref[0]
    y = x * lax.rsqrt(jnp.mean(x * x, axis=-1, keepdims=True) + 1e-6) * g_ref[...]
    h = (y * (1.0 + mod[:, D:2 * D]) + mod[:, 0:D]).astype(BF16)
    off = 0
    for o_ref, n in zip(outs, splits):
        o_ref[...] = jnp.dot(h, w_ref[:, off:off + n], preferred_element_type=F32)
        off += n


def _in_proj(x, g, mod, w_bf16, splits):
    n = w_bf16.shape[1]
    row = lambda i: (i, 0)
    return pl.pallas_call(
        functools.partial(_in_kernel, splits=splits),
        out_shape=[jax.ShapeDtypeStruct((T, s), F32) for s in splits],
        grid=(NT,),
        in_specs=[pl.BlockSpec((TM, D), row), _full((1, D)),
                  pl.BlockSpec((1, 1, N_MOD * D), lambda i: (_group(i), 0, 0)), _full((D, n))],
        out_specs=[pl.BlockSpec((TM, s), row) for s in splits],
        compiler_params=_params(("arbitrary",), 48),
        name="in_proj",
    )(x, g.reshape(1, D), mod, w_bf16)


def _gelu(x):
    return 0.5 * x * (1.0 + lax.erf(x * (1.0 / math.sqrt(2.0))))


def _sgu_kernel(za_ref, lng_ref, ws_ref, bs_ref, o_ref):
    u = _gelu(za_ref[:, 0:HALF])
    va = _gelu(za_ref[:, HALF:2 * HALF])
    mu = jnp.mean(va, axis=-1, keepdims=True)
    dv = va - mu
    var = jnp.mean(dv * dv, axis=-1, keepdims=True)
    vn = dv * lax.rsqrt(var + 1e-5) * lng_ref[...]
    lane = lax.broadcasted_iota(I32, (LANES, LANES), 1)
    first = lane < HEAD
    for c in range(TM // LANES):
        rows = slice(c * LANES, (c + 1) * LANES)
        for p in range(HALF // LANES):
            cols = slice(p * LANES, (p + 1) * LANES)
            vp = vn[rows, cols]
            s = (jnp.dot(ws_ref[2 * p], jnp.where(first, vp, 0.0).astype(BF16), preferred_element_type=F32)
                 + jnp.dot(ws_ref[2 * p + 1], jnp.where(first, 0.0, vp).astype(BF16), preferred_element_type=F32))
            o_ref[rows, cols] = u[rows, cols] * (s + bs_ref[:, cols])


def _sgu(za, ln_g, w_s_bf16, bs_full):
    return pl.pallas_call(
        _sgu_kernel,
        out_shape=jax.ShapeDtypeStruct((T, HALF), F32),
        grid=(NT,),
        in_specs=[pl.BlockSpec((TM, 2 * HALF), lambda i: (i, 0)), _full((1, HALF)),
                  _full((8, LANES, LANES)), _full((LANES, HALF))],
        out_specs=pl.BlockSpec((TM, HALF), lambda i: (i, 0)),
        compiler_params=_params(("arbitrary",)),
        name="sgu",
    )(za, ln_g.reshape(1, HALF), w_s_bf16, bs_full)


def _rwkv_prep_kernel(zb_ref, zp_ref, zn_ref, mu_ref, kk_ref, ka_ref, rk_ref, w0_ref, a0_ref,
                      wup_ref, aup_ref, gup_ref, ones_ref,
                      r_ref, v_ref, kkn_ref, bonus_ref, g_ref, lw_ref, kt_ref, b_ref):
    i = pl.program_id(0)
    in_sample = i >= PROMPT_TILES
    pos = (i - PROMPT_TILES) % TILES_PER_SAMPLE
    is_first = jnp.logical_or(jnp.logical_not(in_sample), pos == 0)
    is_last = jnp.logical_or(jnp.logical_not(in_sample), pos == TILES_PER_SAMPLE - 1)
    zb = zb_ref[...]
    prev_row = jnp.where(is_first, 0.0, zp_ref[7:8, :])
    next_row = jnp.where(is_last, 0.0, zn_ref[0:1, :])
    rowid = lax.broadcasted_iota(I32, (TM, 1), 0)
    zp = jnp.where(rowid == 0, prev_row, pltpu.roll(zb, 1, 0))
    zn = jnp.where(rowid == TM - 1, next_row, pltpu.roll(zb, TM - 1, 0))
    zs = zb + mu_ref[0:1, :] * (zp - zb) + mu_ref[1:2, :] * (zn - zb)
    r = zs[:, 0:HALF]
    k = zs[:, HALF:2 * HALF]
    v = zs[:, 2 * HALF:3 * HALF]
    wa = zs[:, 3 * HALF:3 * HALF + LANES]
    gd = zs[:, 3 * HALF + LANES:B_COLS]
    ones_bd = ones_ref[...]
    r_ref[...] = r
    v_ref[...] = v
    g_ref[...] = jnp.dot(jax.nn.sigmoid(gd).astype(BF16), gup_ref[...], preferred_element_type=F32)
    kk = k * kk_ref[...]
    kkn = kk / jnp.maximum(jnp.sqrt(_group_sum(kk * kk, ones_bd)), 1e-6)
    kkn_ref[...] = kkn
    bonus_ref[...] = _group_sum(r * k * rk_ref[...], ones_bd) * v
    tw = jnp.tanh(wa).astype(BF16)
    wa16 = wa.astype(BF16)
    for dd in range(2):
        lw_ref[dd] = -W_DECAY_SCALE * jax.nn.sigmoid(
            w0_ref[dd:dd + 1, :] + jnp.dot(tw, wup_ref[dd], preferred_element_type=F32))
        a = jax.nn.sigmoid(a0_ref[dd:dd + 1, :] + jnp.dot(wa16, aup_ref[dd], preferred_element_type=F32))
        kt_ref[dd] = k * (1.0 + (a - 1.0) * ka_ref[...])
        b_ref[dd] = a * kkn


def _rwkv_prep(zb, mu, k_k, k_a, r_k, w0, a0, wup_pad, aup_pad, g_up, ones_bd):
    row = lambda i: (i, 0)
    halo = TM // 8
    one = jax.ShapeDtypeStruct((T, HALF), F32)
    two = jax.ShapeDtypeStruct((2, T, HALF), F32)
    o1 = pl.BlockSpec((TM, HALF), row)
    o2 = pl.BlockSpec((2, TM, HALF), lambda i: (0, i, 0))
    return pl.pallas_call(
        _rwkv_prep_kernel,
        out_shape=[one, one, one, one, one, two, two, two],
        grid=(NT,),
        in_specs=[pl.BlockSpec((TM, B_COLS), row),
                  pl.BlockSpec((8, B_COLS), lambda i: (jnp.maximum(i * halo - 1, 0), 0)),
                  pl.BlockSpec((8, B_COLS), lambda i: (jnp.minimum((i + 1) * halo, T // 8 - 1), 0)),
                  _full((2, B_COLS)), _full((1, HALF)), _full((1, HALF)), _full((1, HALF)),
                  _full((2, HALF)), _full((2, HALF)),
                  _full((2, LANES, HALF)), _full((2, LANES, HALF)), _full((LANES, HALF)),
                  _full((HALF, HALF))],
        out_specs=[o1, o1, o1, o1, o1, o2, o2, o2],
        compiler_params=_params(("arbitrary",), 48),
        name="rwkv_prep",
    )(zb, zb, zb, mu, k_k.reshape(1, HALF), k_a.reshape(1, HALF), r_k.reshape(1, HALF), w0, a0,
      wup_pad, aup_pad, g_up, ones_bd)


def _rwkv_chunks(dirs):
    C = RW_CHUNK
    ti = lax.broadcasted_iota(I32, (C, C), 0)
    tj = lax.broadcasted_iota(I32, (C, C), 1)
    bi = lax.broadcasted_iota(I32, (LANES, LANES), 0)
    bj = lax.broadcasted_iota(I32, (LANES, LANES), 1)
    same = (bi >> 6) == (bj >> 6)
    pi = bi & (C - 1)
    pj = bj & (C - 1)
    eye = (bi == bj).astype(F32)
    h0 = lax.broadcasted_iota(I32, (C, LANES), 1) < HEAD

    def stack(x):
        return jnp.concatenate([jnp.where(h0, x, 0.0), jnp.where(h0, 0.0, x)], axis=0)

    def fold(x):
        return x[0:C] + x[C:2 * C]

    chains = []
    for rev, r, v, kk, lw, kt, b, s_ref, y_ref in dirs:
        tri = jnp.where((tj >= ti) if rev else (tj <= ti), 1.0, 0.0).astype(F32)
        p1 = lw.astype(BF16)
        r1 = lw - p1.astype(F32)
        p2 = r1.astype(BF16)
        p3 = (r1 - p2.astype(F32)).astype(BF16)
        cs3 = jnp.dot(tri.astype(BF16), jnp.concatenate([p1, p2, p3], axis=1), preferred_element_type=F32)
        cs = cs3[:, 0:HALF] + (cs3[:, HALF:2 * HALF] + cs3[:, 2 * HALF:3 * HALF])
        ctot = cs[0:1, :] if rev else cs[C - 1:C, :]
        e_neg = jnp.exp(-cs)
        e_tail = jnp.exp(ctot - cs)
        q1 = kk * jnp.exp(cs - lw)
        k1 = kt * e_neg
        b1 = b * e_neg
        r1 = r * jnp.exp(cs)
        k2 = kt * e_tail
        b2 = b * e_tail
        e_tot = jnp.exp(ctot)
        strict = jnp.logical_and(same, (pj > pi) if rev else (pj < pi))
        incl = jnp.logical_and(same, (pj >= pi) if rev else (pj <= pi))
        for p in range(HALF // LANES):
            cols = slice(p * LANES, (p + 1) * LANES)
            chains.append(dict(p=p, cols=cols, strict=strict, incl=incl, s_ref=s_ref, y_ref=y_ref,
                               q1=q1[:, cols], k1=k1[:, cols], b1=b1[:, cols], r1=r1[:, cols],
                               k2=k2[:, cols], b2=b2[:, cols], v=v[:, cols], e_tot=e_tot[:, cols]))

    for ch in chains:
        lhs = jnp.concatenate([stack(ch["q1"]), stack(ch["r1"])], axis=0)
        rhs = jnp.concatenate([ch["k1"], ch["k1"], ch["b1"], ch["b1"]], axis=0)
        gm = _mm(lhs, rhs, NT_DIMS, 2)
        ch["mk"] = jnp.where(ch["strict"], gm[0:2 * C, 0:2 * C], 0.0)
        ch["mb"] = jnp.where(ch["strict"], gm[0:2 * C, 2 * C:4 * C], 0.0)
        ch["nk"] = jnp.where(ch["incl"], gm[2 * C:4 * C, 0:2 * C], 0.0)
        ch["nb"] = jnp.where(ch["incl"], gm[2 * C:4 * C, 2 * C:4 * C], 0.0)
        ch["tinv"] = eye - jnp.where((pi >> 1) == (pj >> 1), ch["mb"], 0.0)
    size = 2
    while size < C:
        sh = size.bit_length() - 1
        blk = jnp.logical_and((pi >> (sh + 1)) == (pj >> (sh + 1)), (pi >> sh) != (pj >> sh))
        for ch in chains:
            ch["tn"] = _mm(ch["tinv"], jnp.where(blk, ch["mb"], 0.0), NN, 1)
        for ch in chains:
            ch["tinv"] = ch["tinv"] - _mm(ch["tn"], ch["tinv"], NN, 2)
        size *= 2
    for ch in chains:
        vst = stack(ch["v"])
        ch["mkv"] = fold(_mm(ch["mk"], vst, NN, 2))
        ch["nkv"] = fold(_mm(ch["nk"], vst, NN, 1))
        ch["s"] = ch["s_ref"][ch["p"]]
        ch["qr"] = _mm(jnp.concatenate([ch["q1"], ch["r1"]], axis=0), ch["s"], NT_DIMS, 2)
    for ch in chains:
        ch["u"] = fold(_mm(ch["tinv"], stack(ch["mkv"] + ch["qr"][0:C]), NN, 2))
    for ch in chains:
        ch["y_ref"][:, ch["cols"]] = ch["qr"][C:2 * C] + ch["nkv"] - fold(_mm(ch["nb"], stack(ch["u"]), NN, 1))
        upd = _mm(jnp.concatenate([ch["v"], ch["u"]], axis=0),
                  jnp.concatenate([ch["k2"], -ch["b2"]], axis=0), TN, 3)
        ch["s_ref"][ch["p"]] = ch["s"] * ch["e_tot"] + jnp.where(same, upd, 0.0)


def _rwkv_scan_kernel(bf_ref, bb_ref, first_ref, last_ref, seq_ref,
                      rf_ref, vf_ref, kkf_ref, lwf_ref, ktf_ref, bfw_ref,
                      rb_ref, vb_ref, kkb_ref, lwb_ref, ktb_ref, bbw_ref, s0_ref,
                      yf_ref, yb_ref, sfin_ref, s_ref):
    step = pl.program_id(0)

    @pl.when(first_ref[step] == 1)
    def _():
        s_ref[...] = s0_ref[:, 0]

    C = RW_CHUNK
    for sub in range(RW_STEP // C):
        f = pl.ds(sub * C, C)
        b = pl.ds(RW_STEP - (sub + 1) * C, C)
        _rwkv_chunks([
            (False, rf_ref[f, :], vf_ref[f, :], kkf_ref[f, :], lwf_ref[0, f, :], ktf_ref[0, f, :], bfw_ref[0, f, :],
             s_ref.at[0], yf_ref.at[f]),
            (True, rb_ref[b, :], vb_ref[b, :], kkb_ref[b, :], lwb_ref[0, b, :], ktb_ref[0, b, :], bbw_ref[0, b, :],
             s_ref.at[1], yb_ref.at[b])])

    @pl.when(last_ref[step] == 1)
    def _():
        sfin_ref[:, 0] = s_ref[...]


def _rwkv_scan(r, v, kk, lw, kt, b, s0_bd):
    C = RW_STEP
    tabs = _seq_tables(C)
    fwd = lambda i, bf, bb, fi, la, sq: (bf[i], 0)
    bwd = lambda i, bf, bb, fi, la, sq: (bb[i], 0)
    fwd3 = lambda i, bf, bb, fi, la, sq: (0, bf[i], 0)
    bwd3 = lambda i, bf, bb, fi, la, sq: (1, bb[i], 0)
    st = pl.BlockSpec((2, 1, 4, LANES, LANES), lambda i, bf, bb, fi, la, sq: (0, sq[i], 0, 0, 0))
    one_f, one_b = pl.BlockSpec((C, HALF), fwd), pl.BlockSpec((C, HALF), bwd)
    two_f, two_b = pl.BlockSpec((1, C, HALF), fwd3), pl.BlockSpec((1, C, HALF), bwd3)
    return pl.pallas_call(
        _rwkv_scan_kernel,
        out_shape=[jax.ShapeDtypeStruct((T, HALF), F32), jax.ShapeDtypeStruct((T, HALF), F32),
                   jax.ShapeDtypeStruct((2, N_SEQ, 4, LANES, LANES), F32)],
        grid_spec=pltpu.PrefetchScalarGridSpec(
            num_scalar_prefetch=5, grid=(len(tabs[0]),),
            in_specs=[one_f, one_f, one_f, two_f, two_f, two_f,
                      one_b, one_b, one_b, two_b, two_b, two_b, st],
            out_specs=[one_f, one_b, st],
            scratch_shapes=[pltpu.VMEM((2, 4, LANES, LANES), F32)]),
        compiler_params=_params(("arbitrary",)),
        name="rwkv_scan",
    )(*tabs, r, v, kk, lw, kt, b, r, v, kk, lw, kt, b, s0_bd)


def _rope(x, cos, sin_signed, first16):
    w = x.shape[1]
    partner = jnp.where(first16, pltpu.roll(x, w - 16, 1), pltpu.roll(x, 16, 1))
    return x * cos + partner * sin_signed


def _odd_prep_kernel(zc_ref, zr_ref, cos_ref, sin_ref, qkg_ref, ones_ref, cq_ref, ck_ref, ckraw_ref, rqk_ref):
    ones_bd = ones_ref[...]
    cos = cos_ref[...]
    sin = sin_ref[...]
    lane = lax.broadcasted_iota(I32, (TM, HALF), 1)
    first16 = (lane & 31) < 16
    for idx, (o_ref, raw_ref) in enumerate(((cq_ref, None), (ck_ref, ckraw_ref))):
        x = zc_ref[:, idx * HALF:(idx + 1) * HALF]
        ms = _group_sum(x * x, ones_bd) * (1.0 / HEAD)
        xn = x * lax.rsqrt(ms + 1e-6) * qkg_ref[idx:idx + 1, :]
        if raw_ref is not None:
            raw_ref[...] = xn
        o_ref[...] = _rope(xn, cos, sin, first16)
    rqk = _rope(zr_ref[...], cos, sin, first16)
    rqk_ref[...] = jnp.where(lane < HALF // 2, rqk * (HEAD ** -0.5), rqk)


def _odd_prep(zc, zr, cos_tab, sin_tab, qkg_tiled, ones_bd):
    row = lambda i: (i, 0)
    tab = lambda i: (jnp.where(i < PROMPT_TILES, 0, 1 + (i - PROMPT_TILES) % TILES_PER_SAMPLE), 0)
    one = jax.ShapeDtypeStruct((T, HALF), F32)
    o1 = pl.BlockSpec((TM, HALF), row)
    return pl.pallas_call(
        _odd_prep_kernel,
        out_shape=[one, one, one, one], grid=(NT,),
        in_specs=[pl.BlockSpec((TM, 2 * HALF), row), pl.BlockSpec((TM, HALF), row),
                  pl.BlockSpec((TM, HALF), tab), pl.BlockSpec((TM, HALF), tab),
                  _full((2, HALF)), _full((HALF, HALF))],
        out_specs=[o1, o1, o1, o1],
        compiler_params=_params(("arbitrary",)),
        name="odd_prep",
    )(zc, zr, cos_tab, sin_tab, qkg_tiled, ones_bd)


def _attn_kernel(*refs, has_ctx, one_minus_li):
    if has_ctx:
        q_ref, k_ref, v_ref, kc_ref, vc_ref, lam_ref, sg_ref, _, o_ref = refs
    else:
        q_ref, k_ref, v_ref, lam_ref, sg_ref, _, o_ref = refs
    lam = lam_ref[...]
    lane = lax.broadcasted_iota(I32, (LANES, LANES), 1)
    m0 = lane < HEAD
    scale = HEAD ** -0.5
    for h in range(4):
        cols = slice(h * LANES, (h + 1) * LANES)
        qp = q_ref[:, cols]
        segs = [(k_ref[:, cols], v_ref[:, cols])]
        if has_ctx:
            segs.append((kc_ref[0, :, cols], vc_ref[0, :, cols]))
        outs = []
        for qm in (jnp.where(m0, qp, 0.0), jnp.where(m0, 0.0, qp)):
            qm16 = qm.astype(BF16)
            ss = [lax.dot_general(qm16, ks.astype(BF16), NT_DIMS, preferred_element_type=F32) * scale
                  for ks, _ in segs]
            mx = ss[0].max(axis=-1, keepdims=True)
            for s_ in ss[1:]:
                mx = jnp.maximum(mx, s_.max(axis=-1, keepdims=True))
            ps = [jnp.exp(s_ - mx) for s_ in ss]
            den = ps[0].sum(axis=-1, keepdims=True)
            for p_ in ps[1:]:
                den = den + p_.sum(axis=-1, keepdims=True)
            outs.append([p_ / den for p_ in ps])
        acc = None
        for si, (_, vs) in enumerate(segs):
            amap = outs[0][si] - lam * outs[1][si]
            t = jnp.dot(amap.astype(BF16), vs.astype(BF16), preferred_element_type=F32)
            acc = t if acc is None else acc + t
        nrm = acc * lax.rsqrt(jnp.mean(acc * acc, axis=-1, keepdims=True) + 1e-6) * sg_ref[...]
        o_ref[:, cols] = nrm * one_minus_li


def _attn(cq, ck, zc, row0, n_seq, seq_len, lam, subln_g, one_minus_li, prev, ctx_k=None, ctx_v=None):
    nq = seq_len // LANES
    qb0 = row0 // LANES
    sb0 = row0 // seq_len
    in_specs = [pl.BlockSpec((LANES, HALF), lambda s, q: (qb0 + s * nq + q, 0)),
                pl.BlockSpec((seq_len, HALF), lambda s, q: (sb0 + s, 0)),
                pl.BlockSpec((seq_len, HALF), lambda s, q: (sb0 + s, 2))]
    args = [cq, ck, zc]
    if ctx_k is not None:
        in_specs += [pl.BlockSpec((1, PAST, HALF), lambda s, q: (s, 0, 0))] * 2
        args += [ctx_k, ctx_v]
    in_specs += [_full((1, 1)), _full((1, LANES))]
    args += [lam.reshape(1, 1), subln_g.reshape(1, LANES)]
    in_specs.append(pl.BlockSpec(memory_space=pl.ANY))
    args.append(prev)
    aliases = {len(args) - 1: 0}
    return pl.pallas_call(
        functools.partial(_attn_kernel, has_ctx=ctx_k is not None, one_minus_li=one_minus_li),
        out_shape=jax.ShapeDtypeStruct((T, HALF), F32),
        grid=(n_seq, nq), in_specs=in_specs,
        out_specs=pl.BlockSpec((LANES, HALF), lambda s, q: (qb0 + s * nq + q, 0)),
        input_output_aliases=aliases,
        compiler_params=_params(("arbitrary", "arbitrary"), 48),
        name="diff_attn",
    )(*args)


_LOG_GAMMA = tuple(tuple(float(np.log1p(-np.exp2(-np.float32(e)), dtype=np.float32)) for e in es)
                   for es in RET_EXP)


def _ret_chunks(dirs):
    C = RET_CHUNK
    ii = lax.broadcasted_iota(I32, (C, C), 0)
    jj = lax.broadcasted_iota(I32, (C, C), 1)
    ri = lax.broadcasted_iota(I32, (C, 1), 0)
    lane = lax.broadcasted_iota(I32, (C, LANES), 1)
    chains = []
    for rev, qk_ref, v_ref, s_ref, o_ref in dirs:
        mask = (jj > ii) if rev else (jj <= ii)
        dist = jnp.where(mask, (jj - ii) if rev else (ii - jj), 0).astype(F32)
        kpow = (ri if rev else (C - 1 - ri)).astype(F32)
        qpow = ((C - ri) if rev else (ri + 1)).astype(F32)
        for h in range(4):
            lg = _LOG_GAMMA[1 if rev else 0][h]
            p = h // 2
            hm = (lane < HEAD) if h % 2 == 0 else (lane >= HEAD)
            qp = jnp.where(hm, qk_ref[:, p * LANES:(p + 1) * LANES], 0.0)
            kp = jnp.where(hm, qk_ref[:, HALF // 2 + p * LANES:HALF // 2 + (p + 1) * LANES], 0.0)
            chains.append(dict(
                h=h, lg=lg, s_ref=s_ref, o_ref=o_ref, q16=qp.astype(BF16), k16=kp.astype(BF16),
                qw16=(qp * jnp.exp(lg * qpow)).astype(BF16), kw16=(kp * jnp.exp(lg * kpow)).astype(BF16),
                v16=v_ref[:, h * LANES:(h + 1) * LANES].astype(BF16),
                decay=jnp.where(mask, jnp.exp(lg * dist), 0.0)))
    for ch in chains:
        ch["sc"] = (lax.dot_general(ch["q16"], ch["k16"], NT_DIMS, preferred_element_type=F32)
                    * ch["decay"]).astype(BF16)
        ch["s"] = ch["s_ref"][ch["h"]]
    for ch in chains:
        ch["o"] = (jnp.dot(ch["sc"], ch["v16"], preferred_element_type=F32)
                   + jnp.dot(ch["qw16"], ch["s"].astype(BF16), preferred_element_type=F32))
        ch["kv"] = lax.dot_general(ch["kw16"], ch["v16"], TN, preferred_element_type=F32)
    for ch in chains:
        h = ch["h"]
        ch["o_ref"][:, h * LANES:(h + 1) * LANES] = ch["o"]
        ch["s_ref"][h] = math.exp(ch["lg"] * C) * ch["s"] + ch["kv"]


def _ret_kernel(bf_ref, bb_ref, first_ref, last_ref, seq_ref,
                qkf_ref, vf_ref, qkb_ref, vb_ref, s0_ref, of_ref, ob_ref, sfin_ref, s_ref):
    step = pl.program_id(0)

    @pl.when(first_ref[step] == 1)
    def _():
        s_ref[...] = s0_ref[:, 0]

    _ret_chunks([(False, qkf_ref, vf_ref, s_ref.at[0], of_ref), (True, qkb_ref, vb_ref, s_ref.at[1], ob_ref)])

    @pl.when(last_ref[step] == 1)
    def _():
        sfin_ref[:, 0] = s_ref[...]


def _retention(rqk, zr, s0):
    C = RET_CHUNK
    tabs = _seq_tables(C)
    st = pl.BlockSpec((2, 1, 4, LANES, LANES), lambda i, bf, bb, fi, la, sq: (0, sq[i], 0, 0, 0))
    spec = lambda use_b, col: pl.BlockSpec(
        (C, HALF), lambda i, bf, bb, fi, la, sq: ((bb if use_b else bf)[i], col))
    return pl.pallas_call(
        _ret_kernel,
        out_shape=[jax.ShapeDtypeStruct((T, HALF), F32), jax.ShapeDtypeStruct((T, HALF), F32),
                   jax.ShapeDtypeStruct((2, N_SEQ, 4, LANES, LANES), F32)],
        grid_spec=pltpu.PrefetchScalarGridSpec(
            num_scalar_prefetch=5, grid=(len(tabs[0]),),
            in_specs=[spec(False, 0), spec(False, 1), spec(True, 0), spec(True, 1), st],
            out_specs=[spec(False, 0), spec(True, 0), st],
            scratch_shapes=[pltpu.VMEM((2, 4, LANES, LANES), F32)]),
        compiler_params=_params(("arbitrary",)),
        name="retention",
    )(*tabs, rqk, zr, rqk, zr, s0)


def _out_kernel(*refs, even):
    if even:
        (a_ref, yf_ref, yb_ref, bonus_ref, g_ref, gng_ref, gnb_ref, ones_ref,
         x_ref, mod_ref, ng_ref, wo_ref, rw_ref, rb_ref,
         y_ref, xp_ref, ti_ref, tg_ref, rk_ref, cnt_ref, run_ref) = refs
        ones_bd = ones_ref[...]
        ys = yf_ref[...] + yb_ref[...]
        mu = _group_sum(ys, ones_bd) * (1.0 / HEAD)
        dv = ys - mu
        var = _group_sum(dv * dv, ones_bd) * (1.0 / HEAD)
        yn = dv * lax.rsqrt(var + RWKV_GN_EPS) * gng_ref[...] + gnb_ref[...]
        left = a_ref[...]
        right = (yn + bonus_ref[...]) * g_ref[...]
    else:
        (c_ref, of_ref, ob_ref, rg_ref, gng_ref,
         x_ref, mod_ref, ng_ref, wo_ref, rw_ref, rb_ref,
         y_ref, xp_ref, ti_ref, tg_ref, rk_ref, cnt_ref, run_ref) = refs
        left = c_ref[...]
        rg = rg_ref[...]
        gate = rg * jax.nn.sigmoid(rg)
        os_ = of_ref[...] + ob_ref[...]
        parts = []
        for h in range(4):
            oh = os_[:, h * LANES:(h + 1) * LANES]
            mu = jnp.mean(oh, axis=-1, keepdims=True)
            dv = oh - mu
            var = jnp.mean(dv * dv, axis=-1, keepdims=True)
            parts.append(dv * lax.rsqrt(var + 1e-5))
        right = gate * (jnp.concatenate(parts, axis=1) * gng_ref[...])
    mod = mod_ref[0]
    o = (jnp.dot(left.astype(BF16), wo_ref[0:HALF, :], preferred_element_type=F32)
         + jnp.dot(right.astype(BF16), wo_ref[HALF:2 * HALF, :], preferred_element_type=F32))
    y = x_ref[...] + mod[:, 2 * D:3 * D] * o
    y_ref[...] = y
    yn2 = y * lax.rsqrt(jnp.mean(y * y, axis=-1, keepdims=True) + 1e-6) * ng_ref[...]
    t = yn2 * (1.0 + mod[:, 4 * D:5 * D]) + mod[:, 3 * D:4 * D]
    xp_ref[...] = t
    logits = _mm(t, rw_ref[...], NN, 3) + rb_ref[...]
    lane = lax.broadcasted_iota(I32, (TM, LANES), 1)
    neg = jnp.float32(-jnp.inf)
    lg = jnp.where(lane < N_EXPERTS, logits, neg)
    vals, hits = [], []
    for _ in range(TOP_K):
        m = jnp.max(lg, axis=-1, keepdims=True)
        ix = jnp.min(jnp.where(lg == m, lane, LANES), axis=-1, keepdims=True)
        hit = lane == ix
        vals.append(m)
        hits.append((ix, hit))
        lg = jnp.where(hit, neg, lg)
    es = [jnp.exp(vv - vals[0]) for vv in vals]
    den = es[0] + es[1] + es[2] + es[3]

    @pl.when(pl.program_id(0) == 0)
    def _():
        run_ref[...] = jnp.zeros_like(run_ref)

    member = jnp.zeros((TM, LANES), F32)
    for _, hit in hits:
        member = member + jnp.where(hit, 1.0, 0.0)
    ri = lax.broadcasted_iota(I32, (TM, TM), 0)
    ci = lax.broadcasted_iota(I32, (TM, TM), 1)
    before = jnp.where(ci < ri, 1.0, 0.0).astype(BF16)
    seen = run_ref[...] + jnp.dot(before, member.astype(BF16), preferred_element_type=F32)
    ti = jnp.zeros((TM, LANES), I32)
    tg = jnp.zeros((TM, LANES), F32)
    rk = jnp.zeros((TM, LANES), F32)
    for kk, (ix, hit) in enumerate(hits):
        ti = jnp.where(lane == kk, ix, ti)
        tg = jnp.where(lane == kk, es[kk] / den, tg)
        rk = jnp.where(lane == kk, jnp.sum(jnp.where(hit, seen, 0.0), axis=-1, keepdims=True), rk)
    ti_ref[...] = ti
    tg_ref[...] = tg
    rk_ref[...] = rk.astype(I32)
    run_ref[...] = run_ref[...] + jnp.sum(member, axis=0, keepdims=True)
    cnt_ref[...] = run_ref[...]


def _out_proj(even, mix_args, mix_specs, x, mod, norm_g, w_out_bf16, rw_pad, rb_pad):
    row = lambda i: (i, 0)
    modspec = pl.BlockSpec((1, 1, N_MOD * D), lambda i: (_group(i), 0, 0))
    in_specs = list(mix_specs) + [pl.BlockSpec((TM, D), row), modspec, _full((1, D)), _full((D, D)),
                                  _full((D, LANES)), _full((1, LANES))]
    args = list(mix_args) + [x, mod, norm_g.reshape(1, D), w_out_bf16, rw_pad, rb_pad]
    lane_i = jax.ShapeDtypeStruct((T, LANES), I32)
    lane_spec = pl.BlockSpec((TM, LANES), row)
    return pl.pallas_call(
        functools.partial(_out_kernel, even=even),
        out_shape=[jax.ShapeDtypeStruct((T, D), F32), jax.ShapeDtypeStruct((T, D), F32),
                   lane_i, jax.ShapeDtypeStruct((T, LANES), F32), lane_i,
                   jax.ShapeDtypeStruct((1, LANES), F32)],
        grid=(NT,), in_specs=in_specs,
        out_specs=[pl.BlockSpec((TM, D), row), pl.BlockSpec((TM, D), row),
                   lane_spec, lane_spec, lane_spec, _full((1, LANES))],
        scratch_shapes=[pltpu.VMEM((1, LANES), F32)],
        compiler_params=_params(("arbitrary",), 48),
        name="out_proj",
    )(*args)


def _route_kernel(cnt_ref, ti_ref, rk_ref, dest_ref, te_ref, nt_ref):
    cnt = cnt_ref[...].astype(I32)
    ntile = lax.shift_right_logical(cnt + (TMX - 1), TMX.bit_length() - 1)
    ei = lax.broadcasted_iota(I32, (LANES, LANES), 0)
    ej = lax.broadcasted_iota(I32, (LANES, LANES), 1)
    upto = jnp.where(ei <= ej, 1.0, 0.0).astype(BF16)
    ntile_f = jnp.broadcast_to(ntile.astype(F32), (8, LANES))
    tile_end = jnp.dot(ntile_f.astype(BF16), upto, preferred_element_type=F32)[0:1, :]
    row_start = (tile_end - ntile.astype(F32)) * float(TMX)
    lane = lax.broadcasted_iota(I32, (TM, LANES), 1)
    ti = ti_ref[...]
    rk = rk_ref[...]
    spread = jnp.zeros((TM, LANES), F32)
    for k in range(TOP_K):
        hit = lane == ti[:, k:k + 1]
        start = jnp.sum(jnp.where(hit, row_start, 0.0), axis=-1, keepdims=True)
        spread = jnp.where((lane & (TOP_K - 1)) == k, start + rk[:, k:k + 1].astype(F32), spread)
    tok = lax.broadcasted_iota(I32, (TM, LANES), 0)
    keep = (tok & (LANES // TOP_K - 1)) == lax.shift_right_logical(lane, 2)
    flat = jnp.where(keep, spread, 0.0).reshape(TM * TOP_K // LANES, LANES // TOP_K, LANES).sum(axis=1)
    dest_ref[...] = flat.astype(I32)

    @pl.when(pl.program_id(0) == 0)
    def _():
        lane1 = lax.broadcasted_iota(I32, (1, LANES), 1)
        n_tiles = jnp.max(tile_end, axis=-1, keepdims=True)
        last_e = jnp.max(jnp.where(cnt > 0, lane1, 0), axis=-1, keepdims=True)
        tile = lax.broadcasted_iota(I32, (TM, 1), 0).astype(F32)
        te = jnp.sum(jnp.where(tile_end <= tile, 1, 0), axis=-1, keepdims=True)
        te = jnp.where(tile < n_tiles, te, last_e)
        te_ref[...] = jnp.broadcast_to(te, (TM, LANES)).astype(I32)
        first_row = lax.broadcasted_iota(I32, (8, LANES), 0) == 0
        nt_ref[...] = jnp.where(first_row, n_tiles, tile_end).astype(I32)


def _route(cnt, ti, rk):
    row = lambda i: (i, 0)
    return pl.pallas_call(
        _route_kernel,
        out_shape=[jax.ShapeDtypeStruct((N_ASSIGN // LANES, LANES), I32), jax.ShapeDtypeStruct((TM, LANES), I32),
                   jax.ShapeDtypeStruct((8, LANES), I32)],
        grid=(NT,),
        in_specs=[_full((1, LANES)), pl.BlockSpec((TM, LANES), row), pl.BlockSpec((TM, LANES), row)],
        out_specs=[pl.BlockSpec((TM * TOP_K // LANES, LANES), row), _full((TM, LANES)), _full((8, LANES))],
        compiler_params=_params(("arbitrary",)),
        name="moe_route",
    )(cnt, ti, rk)


def _start_row_copy(src_ref, src_row, dst_ref, dst_row, sem, queue):
    pltpu.async_copy(src_ref.at[pl.ds(src_row, 1)], dst_ref.at[pl.ds(dst_row, 1)], sem, priority=queue)


def _wait_tiles(n, src_ref, dst_ref, sem):
    for _ in range(n):
        pltpu.make_async_copy(src_ref, dst_ref, sem).wait()


def _dispatch_kernel(dest_ref, tend_ref, x_ref, xs_ref, zero_ref, sem):
    i = pl.program_id(0)

    @pl.when(i == 0)
    def _():
        zero_ref[...] = jnp.zeros_like(zero_ref)

        def last_tile(e, fn):
            end = tend_ref[e]
            begin = tend_ref[e - 1] if e > 0 else 0

            @pl.when(end > begin)
            def _():
                fn(pltpu.make_async_copy(zero_ref, xs_ref.at[pl.ds((end - 1) * TMX, TMX)], sem))

        def unused_tile(j):
            return pltpu.make_async_copy(zero_ref, xs_ref.at[pl.ds(j * TMX, TMX)], sem)

        def start_unused(j, carry):
            unused_tile(j).start()
            return carry

        def wait_unused(j, carry):
            unused_tile(j).wait()
            return carry

        n_used = tend_ref[N_EXPERTS - 1]
        for e in range(N_EXPERTS):
            last_tile(e, lambda c: c.start())
        lax.fori_loop(n_used, MOE_TILES, start_unused, 0)
        for e in range(N_EXPERTS):
            last_tile(e, lambda c: c.wait())
        lax.fori_loop(n_used, MOE_TILES, wait_unused, 0)

    base = i * (TM * TOP_K)

    def start(r, carry):
        for k in range(TOP_K):
            _start_row_copy(x_ref, r, xs_ref, dest_ref[base + r * TOP_K + k], sem, k % 2)
        return carry

    lax.fori_loop(0, TM, start, 0, unroll=4)
    _wait_tiles(TOP_K, x_ref, xs_ref.at[pl.ds(0, TM)], sem)


def _dispatch(dest_flat, tile_end, xt):
    return pl.pallas_call(
        _dispatch_kernel,
        out_shape=jax.ShapeDtypeStruct((R_PAD, D), F32),
        grid_spec=pltpu.PrefetchScalarGridSpec(
            num_scalar_prefetch=2, grid=(NT,),
            in_specs=[pl.BlockSpec((TM, D), lambda i, d, te: (i, 0))],
            out_specs=pl.BlockSpec(memory_space=pl.ANY),
            scratch_shapes=[pltpu.VMEM((TMX, D), F32), pltpu.SemaphoreType.DMA(())]),
        compiler_params=_params(("arbitrary",)),
        name="moe_dispatch",
    )(dest_flat, tile_end, xt)


W_PARTS = 8


def _expert_weights(i, nt, te_ref, w_ref, wbuf_ref, wsem, w16_ref, group_ref):
    rows = w_ref.shape[1] // W_PARTS

    def fetch(e, buf):
        return [pltpu.make_async_copy(w_ref.at[e, pl.ds(p * rows, rows)], wbuf_ref.at[buf, pl.ds(p * rows, rows)],
                                      wsem.at[buf]) for p in range(W_PARTS)]

    @pl.when(i == 0)
    def _():
        group_ref[0] = 0
        for c in fetch(te_ref[0], 0):
            c.start()

    first = jnp.logical_or(i == 0, te_ref[i] != te_ref[jnp.maximum(i - 1, 0)])

    @pl.when(jnp.logical_and(first, i < nt))
    def _():
        cur = group_ref[0] % 2
        nxt = lax.while_loop(
            lambda j: jnp.logical_and(j < nt, te_ref[jnp.minimum(j, MOE_TILES - 1)] == te_ref[i]),
            lambda j: j + 1, i + 1)

        @pl.when(nxt < nt)
        def _():
            for c in fetch(te_ref[jnp.minimum(nxt, MOE_TILES - 1)], 1 - cur):
                c.start()

        for c in fetch(0, cur):
            c.wait()
        w16_ref[...] = wbuf_ref[cur].astype(BF16)
        group_ref[0] = group_ref[0] + 1


def _experts_kernel(te_ref, nt_ref, xs_ref, wgu_ref, bgu_ref, wdn_ref, bdn_ref, y_ref,
                    gu_buf, gu_sem, gu16_ref, gu_group, dn_buf, dn_sem, dn16_ref, dn_group):
    i = pl.program_id(0)
    nt = nt_ref[0]
    _expert_weights(i, nt, te_ref, wgu_ref, gu_buf, gu_sem, gu16_ref, gu_group)
    _expert_weights(i, nt, te_ref, wdn_ref, dn_buf, dn_sem, dn16_ref, dn_group)

    @pl.when(i < nt)
    def _():
        x16 = xs_ref[...].astype(BF16)
        y = bdn_ref[0]
        for h in range(2):
            gc = slice(h * HALF, (h + 1) * HALF)
            uc = slice(D + h * HALF, D + (h + 1) * HALF)
            g = jnp.dot(x16, gu16_ref[:, gc], preferred_element_type=F32) + bgu_ref[0, :, gc]
            u = jnp.dot(x16, gu16_ref[:, uc], preferred_element_type=F32) + bgu_ref[0, :, uc]
            gt = jnp.minimum(g, SWIGLU_LIMIT)
            up = jnp.clip(u, -SWIGLU_LIMIT, SWIGLU_LIMIT)
            act = ((up + 1.0) * gt * jax.nn.sigmoid(SWIGLU_ALPHA * gt)).astype(BF16)
            y = y + jnp.dot(act, dn16_ref[gc, :], preferred_element_type=F32)
        y_ref[...] = y

    @pl.when(i >= nt)
    def _():
        y_ref[...] = jnp.zeros_like(y_ref)


def _tile_clamped(i, te, nt):
    return (jnp.minimum(i, jnp.maximum(nt[0] - 1, 0)), 0)


def _weight_scratch(n_out):
    return [pltpu.VMEM((2, D, n_out), F32), pltpu.SemaphoreType.DMA((2,)), pltpu.VMEM((D, n_out), BF16),
            pltpu.SMEM((1,), I32)]


def _experts(te, n_tiles, xs, w_gu, b_gu, w_dn, b_dn):
    return pl.pallas_call(
        _experts_kernel,
        out_shape=jax.ShapeDtypeStruct((R_PAD, D), F32),
        grid_spec=pltpu.PrefetchScalarGridSpec(
            num_scalar_prefetch=2, grid=(MOE_TILES,),
            in_specs=[pl.BlockSpec((TMX, D), _tile_clamped), pl.BlockSpec(memory_space=pl.ANY),
                      pl.BlockSpec((1, 1, 2 * D), lambda i, te, nt: (te[i], 0, 0)),
                      pl.BlockSpec(memory_space=pl.ANY),
                      pl.BlockSpec((1, 1, D), lambda i, te, nt: (te[i], 0, 0))],
            out_specs=pl.BlockSpec((TMX, D), lambda i, te, nt: (i, 0)),
            scratch_shapes=_weight_scratch(2 * D) + _weight_scratch(D)),
        compiler_params=_params(("arbitrary",), 58),
        name="moe_experts",
    )(te, n_tiles, xs, w_gu, b_gu, w_dn, b_dn)


def _combine_kernel(dest_ref, x_ref, tg_ref, mod_ref, ys_ref, *rest):
    *o_refs, buf_ref, sem = rest
    i = pl.program_id(0)
    slot = i % 2

    def gather(tile, b):
        base = tile * (TM * TOP_K)

        def body(r, carry):
            for k in range(TOP_K):
                _start_row_copy(ys_ref, dest_ref[base + r * TOP_K + k], buf_ref.at[b, k], r, sem.at[b], k % 2)
            return carry

        lax.fori_loop(0, TM, body, 0, unroll=4)

    @pl.when(i == 0)
    def _():
        gather(0, 0)

    @pl.when(i + 1 < NT)
    def _():
        gather(i + 1, 1 - slot)

    _wait_tiles(TOP_K, ys_ref.at[pl.ds(0, TM)], buf_ref.at[slot, 0], sem.at[slot])
    tg = tg_ref[...]
    f = tg[:, 0:1] * buf_ref[slot, 0]
    for k in range(1, TOP_K):
        f = f + tg[:, k:k + 1] * buf_ref[slot, k]
    res = x_ref[...] + mod_ref[0][:, 5 * D:6 * D] * f
    if len(o_refs) == 1:
        o_refs[0][...] = res
    else:
        @pl.when(i < PROMPT_TILES)
        def _():
            o_refs[0][...] = res

        @pl.when(i >= PROMPT_TILES)
        def _():
            o_refs[1][...] = res


def _combine(dest_flat, x, tg, mod, ys, split=False):
    row = lambda i, d: (i, 0)
    if split:
        out_shape = [jax.ShapeDtypeStruct((T_PROMPT, D), F32), jax.ShapeDtypeStruct((T_SAMPLE, D), F32)]
        out_specs = [pl.BlockSpec((TM, D), lambda i, d: (jnp.minimum(i, PROMPT_TILES - 1), 0)),
                     pl.BlockSpec((TM, D), lambda i, d: (jnp.maximum(i - PROMPT_TILES, 0), 0))]
    else:
        out_shape = jax.ShapeDtypeStruct((T, D), F32)
        out_specs = pl.BlockSpec((TM, D), row)
    return pl.pallas_call(
        _combine_kernel,
        out_shape=out_shape,
        grid_spec=pltpu.PrefetchScalarGridSpec(
            num_scalar_prefetch=1, grid=(NT,),
            in_specs=[pl.BlockSpec((TM, D), row), pl.BlockSpec((TM, LANES), row),
                      pl.BlockSpec((1, 1, N_MOD * D), lambda i, d: (_group(i), 0, 0)),
                      pl.BlockSpec(memory_space=pl.ANY)],
            out_specs=out_specs,
            scratch_shapes=[pltpu.VMEM((2, TOP_K, TM, D), F32), pltpu.SemaphoreType.DMA((2,))]),
        compiler_params=_params(("arbitrary",), 40),
        name="moe_combine",
    )(dest_flat, x, tg, mod, ys)


def _moe(layer, y, xt, ti, tg, rk, cnt, mod, w_gu, b_gu, w_dn, b_dn):
    dest, te, nt = _route(cnt, ti, rk)
    dest_flat = dest.reshape(-1)
    te = te[:MOE_TILES, 0] + layer * N_EXPERTS
    n_tiles = nt[0, :1]
    xs = _dispatch(dest_flat, nt[1, :N_EXPERTS], xt)
    n_all = w_gu.shape[0] * N_EXPERTS
    ys = _experts(te, n_tiles, xs, w_gu.reshape(n_all, D, 2 * D), b_gu.reshape(n_all, 1, 2 * D),
                  w_dn.reshape(n_all, D, D), b_dn.reshape(n_all, 1, D))
    return _combine(dest_flat, y, tg, mod, ys, split=layer == 1)


def _ones_blockdiag():
    idx = np.arange(HALF) // HEAD
    return jnp.asarray((idx[:, None] == idx[None, :]).astype(np.float32), dtype=BF16)


def _rope_tables():
    pos = jnp.arange(L_SAMPLE)
    rowp = (pos // 64).astype(F32)
    colp = (pos % 64).astype(F32)
    nf = HEAD // 4
    inv = jnp.power(10000.0, -jnp.arange(nf, dtype=F32) / nf)
    ar = rowp[:, None] * inv[None, :]
    ac = colp[:, None] * inv[None, :]
    cos64 = jnp.concatenate([jnp.cos(ar), jnp.cos(ar), jnp.cos(ac), jnp.cos(ac)], axis=1)
    sin64 = jnp.concatenate([-jnp.sin(ar), jnp.sin(ar), -jnp.sin(ac), jnp.sin(ac)], axis=1)
    cos = jnp.tile(cos64, (1, HALF // HEAD))
    sin = jnp.tile(sin64, (1, HALF // HEAD))
    ident = jnp.ones((TM, HALF), F32)
    return (jnp.concatenate([ident, cos], axis=0), jnp.concatenate([jnp.zeros((TM, HALF), F32), sin], axis=0))


def _bd_pairs(s):
    lead = s.shape[:-3]
    s = s.reshape(lead + (4, 2, HEAD, HEAD))
    z = jnp.zeros_like(s[..., 0, :, :])
    top = jnp.concatenate([s[..., 0, :, :], z], axis=-1)
    bot = jnp.concatenate([z, s[..., 1, :, :]], axis=-1)
    return jnp.concatenate([top, bot], axis=-2)


def _bd_unpairs(s):
    a = s[..., 0:HEAD, 0:HEAD]
    b = s[..., HEAD:, HEAD:]
    out = jnp.stack([a, b], axis=-3)
    return out.reshape(s.shape[:-3] + (8, HEAD, HEAD))


def kernel(x_prompt, x_sample, state_rwkv, cache_k_diff, cache_v_diff, state_retention, c, c_ctx, norm_g, ada_w, ada_b, e_w_in, e_w_out, sgu_ln_g, sgu_w, sgu_b, rw_mu, rw_w0, rw_w_up, rw_a0, rw_a_up, rw_g_up, rw_k_k, rw_k_a, rw_r_k, rw_gn_g, rw_gn_b, o_w_in, o_w_out, da_qk_g, da_lam, da_subln_g, ret_gn_g, router_w, router_b, ex_w_gu, ex_b_gu, ex_w_dn, ex_b_dn):
    x = jnp.concatenate([x_prompt.reshape(T_PROMPT, D), x_sample.reshape(T_SAMPLE, D)], axis=0)
    cvec8 = jnp.concatenate([c_ctx[None, :], c, jnp.zeros((3, D), F32)], axis=0)
    mods = _adaln(cvec8, ada_w, ada_b)
    mod0 = mods[0].reshape(8, 1, N_MOD * D)
    mod1 = mods[1].reshape(8, 1, N_MOD * D)
    ones_bd = _ones_blockdiag()
    rw_pad = jnp.pad(router_w, ((0, 0), (0, 0), (0, LANES - N_EXPERTS)))
    rb_pad = jnp.pad(router_b, ((0, 0), (0, LANES - N_EXPERTS))).reshape(2, 1, LANES)
    row = lambda i: (i, 0)
    half = pl.BlockSpec((TM, HALF), row)

    za, zb = _in_proj(x, norm_g[0, 0], mod0, e_w_in[0].astype(BF16), (2 * HALF, B_COLS))
    bs_full = jnp.repeat(sgu_b[0].T, HEAD, axis=1)
    a_out = _sgu(za, sgu_ln_g[0], sgu_w[0].astype(BF16), bs_full)
    zpad = jnp.zeros((2, HEAD, HALF), F32)
    wup_pad = jnp.concatenate([rw_w_up[0], zpad], axis=1).astype(BF16)
    aup_pad = jnp.concatenate([zpad, rw_a_up[0]], axis=1).astype(BF16)
    r, v, kkn, bonus, g, lw, kt, b = _rwkv_prep(zb, rw_mu[0], rw_k_k[0], rw_k_a[0], rw_r_k[0], rw_w0[0], rw_a0[0],
                                                wup_pad, aup_pad, rw_g_up[0].astype(BF16), ones_bd)
    s0_sample = _bd_pairs(jnp.moveaxis(state_rwkv[:, 0], 1, 0))
    s0_rw = jnp.concatenate([jnp.zeros((2, N_PROMPT, 4, LANES, LANES), F32), s0_sample], axis=1)
    yf_rw, yb_rw, sfin_rw = _rwkv_scan(r, v, kkn, lw, kt, b, s0_rw)
    new_rwkv = jnp.moveaxis(_bd_unpairs(sfin_rw[:, :N_PROMPT]), 0, 1)[:, None]
    y0, xp0, ti0, tg0, rk0, cnt0 = _out_proj(
        True,
        [a_out, yf_rw, yb_rw, bonus, g, rw_gn_g[0].reshape(1, HALF), rw_gn_b[0].reshape(1, HALF), ones_bd],
        [half, half, half, half, half, _full((1, HALF)), _full((1, HALF)), _full((HALF, HALF))],
        x, mod0, norm_g[0, 1], e_w_out[0].astype(BF16), rw_pad[0], rb_pad[0])
    x1 = _moe(0, y0, xp0, ti0, tg0, rk0, cnt0, mod0, ex_w_gu, ex_b_gu, ex_w_dn, ex_b_dn)

    zc, zr = _in_proj(x1, norm_g[1, 0], mod1, o_w_in[0].astype(BF16), (3 * HALF, 3 * HALF))
    cos_tab, sin_tab = _rope_tables()
    qkg = jnp.tile(da_qk_g[0], (1, HALF // HEAD))
    cq, ck, ck_raw, rqk = _odd_prep(zc, zr, cos_tab, sin_tab, qkg, ones_bd)
    lambda_init = 0.8 - 0.6 * math.exp(-0.3 * 1)
    lv = da_lam[0]
    lam = jnp.exp(jnp.sum(lv[0] * lv[1])) - jnp.exp(jnp.sum(lv[2] * lv[3])) + lambda_init
    c_out = _attn(cq, ck, zc, 0, N_PROMPT, L_PROMPT, lam, da_subln_g[0], 1.0 - lambda_init,
                  jnp.zeros((T, HALF), F32))
    ctx_k = cache_k_diff[:, 0].reshape(N_SAMPLE, PAST, HALF)
    ctx_v = cache_v_diff[:, 0].reshape(N_SAMPLE, PAST, HALF)
    c_out = _attn(cq, ck, zc, T_PROMPT, N_SAMPLE, L_SAMPLE, lam, da_subln_g[0], 1.0 - lambda_init,
                  c_out, ctx_k, ctx_v)
    sr = jnp.moveaxis(state_retention[:, 0], 1, 0)
    zr0 = jnp.zeros_like(sr)
    s0_sample = jnp.stack([jnp.concatenate([sr[:, :, 0], zr0[:, :, 0]], axis=-2),
                           jnp.concatenate([zr0[:, :, 1], sr[:, :, 1]], axis=-2),
                           jnp.concatenate([sr[:, :, 2], zr0[:, :, 2]], axis=-2),
                           jnp.concatenate([zr0[:, :, 3], sr[:, :, 3]], axis=-2)], axis=2)
    s0_ret = jnp.concatenate([jnp.zeros((2, N_PROMPT, 4, LANES, LANES), F32), s0_sample], axis=1)
    of_ret, ob_ret, rfin = _retention(rqk, zr, s0_ret)
    rfin_p = rfin[:, :N_PROMPT]
    new_ret = jnp.stack([rfin_p[:, :, 0, 0:HEAD], rfin_p[:, :, 1, HEAD:], rfin_p[:, :, 2, 0:HEAD],
                         rfin_p[:, :, 3, HEAD:]], axis=2)
    new_ret = jnp.moveaxis(new_ret, 0, 1)[:, None]
    y1, xp1, ti1, tg1, rk1, cnt1 = _out_proj(
        False,
        [c_out, of_ret, ob_ret, zr, ret_gn_g[0].reshape(1, HALF)],
        [half, half, half, pl.BlockSpec((TM, HALF), lambda i: (i, 2)), _full((1, HALF))],
        x1, mod1, norm_g[1, 1], o_w_out[0].astype(BF16), rw_pad[1], rb_pad[1])
    y_prompt, y_sample = _moe(1, y1, xp1, ti1, tg1, rk1, cnt1, mod1, ex_w_gu, ex_b_gu, ex_w_dn, ex_b_dn)

    new_k = ck_raw[:T_PROMPT].reshape(N_PROMPT, 1, L_PROMPT, 4, LANES)
    new_v = zc[:T_PROMPT, 2 * HALF:3 * HALF].reshape(N_PROMPT, 1, L_PROMPT, 4, LANES)
    return (y_prompt.reshape(N_PROMPT, L_PROMPT, D), y_sample.reshape(N_SAMPLE, L_SAMPLE, D),
            new_rwkv, new_k, new_v, new_ret)
```

```python
import functools
import math

import numpy as np
import jax
import jax.numpy as jnp
from jax import lax
from jax.experimental import pallas as pl
from jax.experimental.pallas import tpu as pltpu

F32 = jnp.float32
BF16 = jnp.bfloat16
I32 = jnp.int32

D = 1024
N_PROMPT, L_PROMPT = 16, 256
N_SAMPLE, L_SAMPLE = 4, 1024
N_SEQ = N_PROMPT + N_SAMPLE
PAST = 256
T_PROMPT = N_PROMPT * L_PROMPT
T_SAMPLE = N_SAMPLE * L_SAMPLE
T = T_PROMPT + T_SAMPLE
TM = 256
NT = T // TM
PROMPT_TILES = T_PROMPT // TM
TILES_PER_SAMPLE = L_SAMPLE // TM
N_MOD = 6
HALF = 512
B_COLS = 1792
HEAD = 64
W_DECAY_SCALE = math.exp(-0.5)
RWKV_GN_EPS = 64e-5
RW_CHUNK = 64
RW_STEP = 256
RET_CHUNK = 128
RET_EXP = ((5.0, 7.0, 9.0, 11.0), (6.0, 8.0, 10.0, 12.0))
N_EXPERTS = 32
TOP_K = 4
SWIGLU_LIMIT = 7.0
SWIGLU_ALPHA = 1.702
N_ASSIGN = T * TOP_K
TMX = 512
MOE_TILES = N_ASSIGN // TMX + N_EXPERTS
R_PAD = MOE_TILES * TMX
LANES = 128

NN = (((1,), (0,)), ((), ()))
NT_DIMS = (((1,), (1,)), ((), ()))
TN = (((0,), (0,)), ((), ()))


def _group(i):
    return jnp.where(i < PROMPT_TILES, 0, 1 + (i - PROMPT_TILES) // TILES_PER_SAMPLE)


def _mm(a, b, dims=NN, passes=1):
    dg = functools.partial(lax.dot_general, dimension_numbers=dims, preferred_element_type=F32)
    if passes == 1:
        return dg(a.astype(BF16), b.astype(BF16))
    a = a.astype(F32)
    b = b.astype(F32)
    ah = a.astype(BF16)
    al = (a - ah.astype(F32)).astype(BF16)
    bh = b.astype(BF16)
    if passes == 2:
        assert dims[0][0] == (1,)
        m = a.shape[0]
        both = dg(jnp.concatenate([ah, al], axis=0), bh)
        return both[0:m] + both[m:2 * m]
    bl = (b - bh.astype(F32)).astype(BF16)
    if dims[0][0] == (1,):
        m = a.shape[0]
        both = dg(jnp.concatenate([ah, al], axis=0), bh)
        return both[0:m] + (dg(ah, bl) + both[m:2 * m])
    return dg(ah, bh) + (dg(ah, bl) + dg(al, bh))


def _group_sum(x, ones_bd):
    xh = x.astype(BF16)
    xl = (x - xh.astype(F32)).astype(BF16)
    return (jnp.dot(xh, ones_bd, preferred_element_type=F32)
            + jnp.dot(xl, ones_bd, preferred_element_type=F32))


def _full(shape):
    nd = len(shape)
    return pl.BlockSpec(shape, lambda *_: (0,) * nd)


def _params(sem, vmem_mb=None):
    kw = {}
    if vmem_mb is not None:
        kw["vmem_limit_bytes"] = vmem_mb * 1024 * 1024
    return pltpu.CompilerParams(dimension_semantics=sem, **kw)


def _seq_tables(chunk):
    blk_f, blk_b, first, last, seq = [], [], [], [], []
    row = 0
    for s in range(N_SEQ):
        n = (L_PROMPT if s < N_PROMPT else L_SAMPLE) // chunk
        base = row // chunk
        for j in range(n):
            blk_f.append(base + j)
            blk_b.append(base + n - 1 - j)
            first.append(int(j == 0))
            last.append(int(j == n - 1))
            seq.append(s)
        row += n * chunk
    return tuple(np.asarray(a, np.int32) for a in (blk_f, blk_b, first, last, seq))


def _adaln_kernel(c_ref, w_ref, b_ref, o_ref):
    c = c_ref[...]
    s = c * jax.nn.sigmoid(c)
    o_ref[0] = _mm(s, w_ref[0], NN, 3) + b_ref[0]


def _adaln(cvec8, ada_w, ada_b):
    depth, _, n = ada_w.shape
    bn = 1536
    return pl.pallas_call(
        _adaln_kernel,
        out_shape=jax.ShapeDtypeStruct((depth, 8, n), F32),
        grid=(depth, n // bn),
        in_specs=[pl.BlockSpec((8, D), lambda l, j: (0, 0)),
                  pl.BlockSpec((1, D, bn), lambda l, j: (l, 0, j)),
                  pl.BlockSpec((1, 1, bn), lambda l, j: (l, 0, j))],
        out_specs=pl.BlockSpec((1, 8, bn), lambda l, j: (l, 0, j)),
        compiler_params=_params(("arbitrary", "arbitrary"), 40),
        name="adaln",
    )(cvec8, ada_w, ada_b.reshape(depth, 1, n))


def _token_specs(x):
    if isinstance(x, tuple):
        return ([pl.BlockSpec((TM, D), lambda i, *_: (jnp.minimum(i, PROMPT_TILES - 1), 0)),
                 pl.BlockSpec((TM, D), lambda i, *_: (jnp.maximum(i - PROMPT_TILES, 0), 0))], list(x))
    return [pl.BlockSpec((TM, D), lambda i, *_: (i, 0))], [x]


def _token_rows(x_refs):
    if len(x_refs) == 2:
        return jnp.where(pl.program_id(0) < PROMPT_TILES, x_refs[0][...], x_refs[1][...])
    return x_refs[0][...]


def _in_kernel(*refs, splits, n_x):
    x_refs, (g_ref, mod_ref, w_ref), outs = refs[:n_x], refs[n_x:n_x + 3], refs[n_x + 3:]
    x = _token_rows(x_refs)
    mod = mod_ref[0]
    y = x * lax.rsqrt(jnp.mean(x * x, axis=-1, keepdims=True) + 1e-6) * g_ref[...]
    h = (y * (1.0 + mod[:, D:2 * D]) + mod[:, 0:D]).astype(BF16)
    off = 0
    for o_ref, n in zip(outs, splits):
        o_ref[...] = jnp.dot(h, w_ref[:, off:off + n], preferred_element_type=F32)
        off += n


def _in_proj(x, g, mod, w_bf16, splits):
    n = w_bf16.shape[1]
    row = lambda i: (i, 0)
    x_specs, x_args = _token_specs(x)
    return pl.pallas_call(
        functools.partial(_in_kernel, splits=splits, n_x=len(x_args)),
        out_shape=[jax.ShapeDtypeStruct((T, s), F32) for s in splits],
        grid=(NT,),
        in_specs=x_specs + [_full((1, D)), pl.BlockSpec((1, 1, N_MOD * D), lambda i: (_group(i), 0, 0)),
                            _full((D, n))],
        out_specs=[pl.BlockSpec((TM, s), row) for s in splits],
        compiler_params=_params(("arbitrary",), 48),
        name="in_proj",
    )(*x_args, g.reshape(1, D), mod, w_bf16)


def _gelu(x):
    return 0.5 * x * (1.0 + lax.erf(x * (1.0 / math.sqrt(2.0))))


def _sgu_kernel(za_ref, lng_ref, ws_ref, bs_ref, o_ref):
    u = _gelu(za_ref[:, 0:HALF])
    va = _gelu(za_ref[:, HALF:2 * HALF])
    mu = jnp.mean(va, axis=-1, keepdims=True)
    dv = va - mu
    var = jnp.mean(dv * dv, axis=-1, keepdims=True)
    vn = dv * lax.rsqrt(var + 1e-5) * lng_ref[...]
    lane = lax.broadcasted_iota(I32, (LANES, LANES), 1)
    first = lane < HEAD
    for c in range(TM // LANES):
        rows = slice(c * LANES, (c + 1) * LANES)
        for p in range(HALF // LANES):
            cols = slice(p * LANES, (p + 1) * LANES)
            vp = vn[rows, cols]
            s = (jnp.dot(ws_ref[2 * p], jnp.where(first, vp, 0.0).astype(BF16), preferred_element_type=F32)
                 + jnp.dot(ws_ref[2 * p + 1], jnp.where(first, 0.0, vp).astype(BF16), preferred_element_type=F32))
            o_ref[rows, cols] = u[rows, cols] * (s + bs_ref[:, cols])


def _sgu(za, ln_g, w_s_bf16, bs_full):
    return pl.pallas_call(
        _sgu_kernel,
        out_shape=jax.ShapeDtypeStruct((T, HALF), F32),
        grid=(NT,),
        in_specs=[pl.BlockSpec((TM, 2 * HALF), lambda i: (i, 0)), _full((1, HALF)),
                  _full((8, LANES, LANES)), _full((LANES, HALF))],
        out_specs=pl.BlockSpec((TM, HALF), lambda i: (i, 0)),
        compiler_params=_params(("arbitrary",)),
        name="sgu",
    )(za, ln_g.reshape(1, HALF), w_s_bf16, bs_full)


def _rwkv_prep_kernel(zb_ref, zp_ref, zn_ref, mu_ref, kk_ref, ka_ref, rk_ref, w0_ref, a0_ref,
                      wup_ref, aup_ref, gup_ref, ones_ref,
                      r_ref, v_ref, kkn_ref, bonus_ref, g_ref, lw_ref, kt_ref, b_ref):
    i = pl.program_id(0)
    in_sample = i >= PROMPT_TILES
    pos = (i - PROMPT_TILES) % TILES_PER_SAMPLE
    is_first = jnp.logical_or(jnp.logical_not(in_sample), pos == 0)
    is_last = jnp.logical_or(jnp.logical_not(in_sample), pos == TILES_PER_SAMPLE - 1)
    zb = zb_ref[...]
    prev_row = jnp.where(is_first, 0.0, zp_ref[7:8, :])
    next_row = jnp.where(is_last, 0.0, zn_ref[0:1, :])
    rowid = lax.broadcasted_iota(I32, (TM, 1), 0)
    zp = jnp.where(rowid == 0, prev_row, pltpu.roll(zb, 1, 0))
    zn = jnp.where(rowid == TM - 1, next_row, pltpu.roll(zb, TM - 1, 0))
    zs = zb + mu_ref[0:1, :] * (zp - zb) + mu_ref[1:2, :] * (zn - zb)
    r = zs[:, 0:HALF]
    k = zs[:, HALF:2 * HALF]
    v = zs[:, 2 * HALF:3 * HALF]
    wa = zs[:, 3 * HALF:3 * HALF + LANES]
    gd = zs[:, 3 * HALF + LANES:B_COLS]
    ones_bd = ones_ref[...]
    r_ref[...] = r
    v_ref[...] = v
    g_ref[...] = jnp.dot(jax.nn.sigmoid(gd).astype(BF16), gup_ref[...], preferred_element_type=F32)
    kk = k * kk_ref[...]
    kkn = kk / jnp.maximum(jnp.sqrt(_group_sum(kk * kk, ones_bd)), 1e-6)
    kkn_ref[...] = kkn
    bonus_ref[...] = _group_sum(r * k * rk_ref[...], ones_bd) * v
    tw = jnp.tanh(wa).astype(BF16)
    wa16 = wa.astype(BF16)
    for dd in range(2):
        lw_ref[dd] = -W_DECAY_SCALE * jax.nn.sigmoid(
            w0_ref[dd:dd + 1, :] + jnp.dot(tw, wup_ref[dd], preferred_element_type=F32))
        a = jax.nn.sigmoid(a0_ref[dd:dd + 1, :] + jnp.dot(wa16, aup_ref[dd], preferred_element_type=F32))
        kt_ref[dd] = k * (1.0 + (a - 1.0) * ka_ref[...])
        b_ref[dd] = a * kkn


def _rwkv_prep(zb, mu, k_k, k_a, r_k, w0, a0, wup_pad, aup_pad, g_up, ones_bd):
    row = lambda i: (i, 0)
    halo = TM // 8
    one = jax.ShapeDtypeStruct((T, HALF), F32)
    two = jax.ShapeDtypeStruct((2, T, HALF), F32)
    o1 = pl.BlockSpec((TM, HALF), row)
    o2 = pl.BlockSpec((2, TM, HALF), lambda i: (0, i, 0))
    return pl.pallas_call(
        _rwkv_prep_kernel,
        out_shape=[one, one, one, one, one, two, two, two],
        grid=(NT,),
        in_specs=[pl.BlockSpec((TM, B_COLS), row),
                  pl.BlockSpec((8, B_COLS), lambda i: (jnp.maximum(i * halo - 1, 0), 0)),
                  pl.BlockSpec((8, B_COLS), lambda i: (jnp.minimum((i + 1) * halo, T // 8 - 1), 0)),
                  _full((2, B_COLS)), _full((1, HALF)), _full((1, HALF)), _full((1, HALF)),
                  _full((2, HALF)), _full((2, HALF)),
                  _full((2, LANES, HALF)), _full((2, LANES, HALF)), _full((LANES, HALF)),
                  _full((HALF, HALF))],
        out_specs=[o1, o1, o1, o1, o1, o2, o2, o2],
        compiler_params=_params(("arbitrary",), 48),
        name="rwkv_prep",
    )(zb, zb, zb, mu, k_k.reshape(1, HALF), k_a.reshape(1, HALF), r_k.reshape(1, HALF), w0, a0,
      wup_pad, aup_pad, g_up, ones_bd)


def _rwkv_chunks(dirs):
    C = RW_CHUNK
    ti = lax.broadcasted_iota(I32, (C, C), 0)
    tj = lax.broadcasted_iota(I32, (C, C), 1)
    bi = lax.broadcasted_iota(I32, (LANES, LANES), 0)
    bj = lax.broadcasted_iota(I32, (LANES, LANES), 1)
    same = (bi >> 6) == (bj >> 6)
    pi = bi & (C - 1)
    pj = bj & (C - 1)
    eye = (bi == bj).astype(F32)
    h0 = lax.broadcasted_iota(I32, (C, LANES), 1) < HEAD

    def stack(x):
        return jnp.concatenate([jnp.where(h0, x, 0.0), jnp.where(h0, 0.0, x)], axis=0)

    def fold(x):
        return x[0:C] + x[C:2 * C]

    chains = []
    for rev, r, v, kk, lw, kt, b, s_ref, y_ref in dirs:
        tri = jnp.where((tj >= ti) if rev else (tj <= ti), 1.0, 0.0).astype(F32)
        p1 = lw.astype(BF16)
        r1 = lw - p1.astype(F32)
        p2 = r1.astype(BF16)
        p3 = (r1 - p2.astype(F32)).astype(BF16)
        cs3 = jnp.dot(tri.astype(BF16), jnp.concatenate([p1, p2, p3], axis=1), preferred_element_type=F32)
        cs = cs3[:, 0:HALF] + (cs3[:, HALF:2 * HALF] + cs3[:, 2 * HALF:3 * HALF])
        ctot = cs[0:1, :] if rev else cs[C - 1:C, :]
        e_neg = jnp.exp(-cs)
        e_tail = jnp.exp(ctot - cs)
        q1 = kk * jnp.exp(cs - lw)
        k1 = kt * e_neg
        b1 = b * e_neg
        r1 = r * jnp.exp(cs)
        k2 = kt * e_tail
        b2 = b * e_tail
        e_tot = jnp.exp(ctot)
        strict = jnp.logical_and(same, (pj > pi) if rev else (pj < pi))
        incl = jnp.logical_and(same, (pj >= pi) if rev else (pj <= pi))
        for p in range(HALF // LANES):
            cols = slice(p * LANES, (p + 1) * LANES)
            chains.append(dict(p=p, cols=cols, strict=strict, incl=incl, s_ref=s_ref, y_ref=y_ref,
                               q1=q1[:, cols], k1=k1[:, cols], b1=b1[:, cols], r1=r1[:, cols],
                               k2=k2[:, cols], b2=b2[:, cols], v=v[:, cols], e_tot=e_tot[:, cols]))

    for ch in chains:
        lhs = jnp.concatenate([stack(ch["q1"]), stack(ch["r1"])], axis=0)
        rhs = jnp.concatenate([ch["k1"], ch["k1"], ch["b1"], ch["b1"]], axis=0)
        gm = _mm(lhs, rhs, NT_DIMS, 2)
        ch["mk"] = jnp.where(ch["strict"], gm[0:2 * C, 0:2 * C], 0.0)
        ch["mb"] = jnp.where(ch["strict"], gm[0:2 * C, 2 * C:4 * C], 0.0)
        ch["nk"] = jnp.where(ch["incl"], gm[2 * C:4 * C, 0:2 * C], 0.0)
        ch["nb"] = jnp.where(ch["incl"], gm[2 * C:4 * C, 2 * C:4 * C], 0.0)
        ch["tinv"] = eye - jnp.where((pi >> 1) == (pj >> 1), ch["mb"], 0.0)
    size = 2
    while size < C:
        sh = size.bit_length() - 1
        blk = jnp.logical_and((pi >> (sh + 1)) == (pj >> (sh + 1)), (pi >> sh) != (pj >> sh))
        for ch in chains:
            ch["tn"] = _mm(ch["tinv"], jnp.where(blk, ch["mb"], 0.0), NN, 1)
        for ch in chains:
            ch["tinv"] = ch["tinv"] - _mm(ch["tn"], ch["tinv"], NN, 2)
        size *= 2
    for ch in chains:
        vst = stack(ch["v"])
        ch["mkv"] = fold(_mm(ch["mk"], vst, NN, 2))
        ch["nkv"] = fold(_mm(ch["nk"], vst, NN, 1))
        ch["s"] = ch["s_ref"][ch["p"]]
        ch["qr"] = _mm(jnp.concatenate([ch["q1"], ch["r1"]], axis=0), ch["s"], NT_DIMS, 2)
    for ch in chains:
        ch["u"] = fold(_mm(ch["tinv"], stack(ch["mkv"] + ch["qr"][0:C]), NN, 2))
    for ch in chains:
        ch["y_ref"][:, ch["cols"]] = ch["qr"][C:2 * C] + ch["nkv"] - fold(_mm(ch["nb"], stack(ch["u"]), NN, 1))
        upd = _mm(jnp.concatenate([ch["v"], ch["u"]], axis=0),
                  jnp.concatenate([ch["k2"], -ch["b2"]], axis=0), TN, 3)
        ch["s_ref"][ch["p"]] = ch["s"] * ch["e_tot"] + jnp.where(same, upd, 0.0)


def _rwkv_scan_kernel(bf_ref, bb_ref, first_ref, last_ref, seq_ref,
                      rf_ref, vf_ref, kkf_ref, lwf_ref, ktf_ref, bfw_ref,
                      rb_ref, vb_ref, kkb_ref, lwb_ref, ktb_ref, bbw_ref, s0_ref,
                      yf_ref, yb_ref, sfin_ref, s_ref):
    step = pl.program_id(0)

    @pl.when(first_ref[step] == 1)
    def _():
        s_ref[...] = s0_ref[:, 0]

    C = RW_CHUNK
    for sub in range(RW_STEP // C):
        f = pl.ds(sub * C, C)
        b = pl.ds(RW_STEP - (sub + 1) * C, C)
        _rwkv_chunks([
            (False, rf_ref[f, :], vf_ref[f, :], kkf_ref[f, :], lwf_ref[0, f, :], ktf_ref[0, f, :], bfw_ref[0, f, :],
             s_ref.at[0], yf_ref.at[f]),
            (True, rb_ref[b, :], vb_ref[b, :], kkb_ref[b, :], lwb_ref[0, b, :], ktb_ref[0, b, :], bbw_ref[0, b, :],
             s_ref.at[1], yb_ref.at[b])])

    @pl.when(last_ref[step] == 1)
    def _():
        sfin_ref[:, 0] = s_ref[...]


def _rwkv_scan(r, v, kk, lw, kt, b, s0_bd):
    C = RW_STEP
    tabs = _seq_tables(C)
    fwd = lambda i, bf, bb, fi, la, sq: (bf[i], 0)
    bwd = lambda i, bf, bb, fi, la, sq: (bb[i], 0)
    fwd3 = lambda i, bf, bb, fi, la, sq: (0, bf[i], 0)
    bwd3 = lambda i, bf, bb, fi, la, sq: (1, bb[i], 0)
    st = pl.BlockSpec((2, 1, 4, LANES, LANES), lambda i, bf, bb, fi, la, sq: (0, sq[i], 0, 0, 0))
    one_f, one_b = pl.BlockSpec((C, HALF), fwd), pl.BlockSpec((C, HALF), bwd)
    two_f, two_b = pl.BlockSpec((1, C, HALF), fwd3), pl.BlockSpec((1, C, HALF), bwd3)
    return pl.pallas_call(
        _rwkv_scan_kernel,
        out_shape=[jax.ShapeDtypeStruct((T, HALF), F32), jax.ShapeDtypeStruct((T, HALF), F32),
                   jax.ShapeDtypeStruct((2, N_SEQ, 4, LANES, LANES), F32)],
        grid_spec=pltpu.PrefetchScalarGridSpec(
            num_scalar_prefetch=5, grid=(len(tabs[0]),),
            in_specs=[one_f, one_f, one_f, two_f, two_f, two_f,
                      one_b, one_b, one_b, two_b, two_b, two_b, st],
            out_specs=[one_f, one_b, st],
            scratch_shapes=[pltpu.VMEM((2, 4, LANES, LANES), F32)]),
        compiler_params=_params(("arbitrary",)),
        name="rwkv_scan",
    )(*tabs, r, v, kk, lw, kt, b, r, v, kk, lw, kt, b, s0_bd)


def _rope(x, cos, sin_signed, first16):
    w = x.shape[1]
    partner = jnp.where(first16, pltpu.roll(x, w - 16, 1), pltpu.roll(x, 16, 1))
    return x * cos + partner * sin_signed


def _odd_prep_kernel(zc_ref, zr_ref, cos_ref, sin_ref, qkg_ref, ones_ref, cq_ref, ck_ref, ckraw_ref, rqk_ref):
    ones_bd = ones_ref[...]
    cos = cos_ref[...]
    sin = sin_ref[...]
    lane = lax.broadcasted_iota(I32, (TM, HALF), 1)
    first16 = (lane & 31) < 16
    for idx, (o_ref, raw_ref) in enumerate(((cq_ref, None), (ck_ref, ckraw_ref))):
        x = zc_ref[:, idx * HALF:(idx + 1) * HALF]
        ms = _group_sum(x * x, ones_bd) * (1.0 / HEAD)
        xn = x * lax.rsqrt(ms + 1e-6) * qkg_ref[idx:idx + 1, :]
        if raw_ref is not None:
            raw_ref[...] = xn
        o_ref[...] = _rope(xn, cos, sin, first16)
    rqk = _rope(zr_ref[...], cos, sin, first16)
    rqk_ref[...] = jnp.where(lane < HALF // 2, rqk * (HEAD ** -0.5), rqk)


def _odd_prep(zc, zr, cos_tab, sin_tab, qkg_tiled, ones_bd):
    row = lambda i: (i, 0)
    tab = lambda i: (jnp.where(i < PROMPT_TILES, 0, 1 + (i - PROMPT_TILES) % TILES_PER_SAMPLE), 0)
    one = jax.ShapeDtypeStruct((T, HALF), F32)
    o1 = pl.BlockSpec((TM, HALF), row)
    return pl.pallas_call(
        _odd_prep_kernel,
        out_shape=[one, one, one, one], grid=(NT,),
        in_specs=[pl.BlockSpec((TM, 2 * HALF), row), pl.BlockSpec((TM, HALF), row),
                  pl.BlockSpec((TM, HALF), tab), pl.BlockSpec((TM, HALF), tab),
                  _full((2, HALF)), _full((HALF, HALF))],
        out_specs=[o1, o1, o1, o1],
        compiler_params=_params(("arbitrary",)),
        name="odd_prep",
    )(zc, zr, cos_tab, sin_tab, qkg_tiled, ones_bd)


def _attn_kernel(*refs, has_ctx, one_minus_li):
    if has_ctx:
        q_ref, k_ref, v_ref, kc_ref, vc_ref, lam_ref, sg_ref, _, o_ref = refs
    else:
        q_ref, k_ref, v_ref, lam_ref, sg_ref, _, o_ref = refs
    lam = lam_ref[...]
    lane = lax.broadcasted_iota(I32, (LANES, LANES), 1)
    m0 = lane < HEAD
    scale = HEAD ** -0.5
    for h in range(4):
        cols = slice(h * LANES, (h + 1) * LANES)
        qp = q_ref[:, cols]
        segs = [(k_ref[:, cols], v_ref[:, cols])]
        if has_ctx:
            segs.append((kc_ref[0, :, cols], vc_ref[0, :, cols]))
        outs = []
        for qm in (jnp.where(m0, qp, 0.0), jnp.where(m0, 0.0, qp)):
            qm16 = qm.astype(BF16)
            ss = [lax.dot_general(qm16, ks.astype(BF16), NT_DIMS, preferred_element_type=F32) * scale
                  for ks, _ in segs]
            mx = ss[0].max(axis=-1, keepdims=True)
            for s_ in ss[1:]:
                mx = jnp.maximum(mx, s_.max(axis=-1, keepdims=True))
            ps = [jnp.exp(s_ - mx) for s_ in ss]
            den = ps[0].sum(axis=-1, keepdims=True)
            for p_ in ps[1:]:
                den = den + p_.sum(axis=-1, keepdims=True)
            outs.append([p_ / den for p_ in ps])
        acc = None
        for si, (_, vs) in enumerate(segs):
            amap = outs[0][si] - lam * outs[1][si]
            t = jnp.dot(amap.astype(BF16), vs.astype(BF16), preferred_element_type=F32)
            acc = t if acc is None else acc + t
        nrm = acc * lax.rsqrt(jnp.mean(acc * acc, axis=-1, keepdims=True) + 1e-6) * sg_ref[...]
        o_ref[:, cols] = nrm * one_minus_li


def _attn(cq, ck, zc, row0, n_seq, seq_len, lam, subln_g, one_minus_li, prev, ctx_k=None, ctx_v=None):
    nq = seq_len // LANES
    qb0 = row0 // LANES
    sb0 = row0 // seq_len
    in_specs = [pl.BlockSpec((LANES, HALF), lambda s, q: (qb0 + s * nq + q, 0)),
                pl.BlockSpec((seq_len, HALF), lambda s, q: (sb0 + s, 0)),
                pl.BlockSpec((seq_len, HALF), lambda s, q: (sb0 + s, 2))]
    args = [cq, ck, zc]
    if ctx_k is not None:
        in_specs += [pl.BlockSpec((1, PAST, HALF), lambda s, q: (s, 0, 0))] * 2
        args += [ctx_k, ctx_v]
    in_specs += [_full((1, 1)), _full((1, LANES))]
    args += [lam.reshape(1, 1), subln_g.reshape(1, LANES)]
    in_specs.append(pl.BlockSpec(memory_space=pl.ANY))
    args.append(prev)
    aliases = {len(args) - 1: 0}
    return pl.pallas_call(
        functools.partial(_attn_kernel, has_ctx=ctx_k is not None, one_minus_li=one_minus_li),
        out_shape=jax.ShapeDtypeStruct((T, HALF), F32),
        grid=(n_seq, nq), in_specs=in_specs,
        out_specs=pl.BlockSpec((LANES, HALF), lambda s, q: (qb0 + s * nq + q, 0)),
        input_output_aliases=aliases,
        compiler_params=_params(("arbitrary", "arbitrary"), 48),
        name="diff_attn",
    )(*args)


_LOG_GAMMA = tuple(tuple(float(np.log1p(-np.exp2(-np.float32(e)), dtype=np.float32)) for e in es)
                   for es in RET_EXP)


def _ret_chunks(dirs):
    C = RET_CHUNK
    ii = lax.broadcasted_iota(I32, (C, C), 0)
    jj = lax.broadcasted_iota(I32, (C, C), 1)
    ri = lax.broadcasted_iota(I32, (C, 1), 0)
    lane = lax.broadcasted_iota(I32, (C, LANES), 1)
    chains = []
    for rev, qk_ref, v_ref, s_ref, o_ref in dirs:
        mask = (jj > ii) if rev else (jj <= ii)
        dist = jnp.where(mask, (jj - ii) if rev else (ii - jj), 0).astype(F32)
        kpow = (ri if rev else (C - 1 - ri)).astype(F32)
        qpow = ((C - ri) if rev else (ri + 1)).astype(F32)
        for h in range(4):
            lg = _LOG_GAMMA[1 if rev else 0][h]
            p = h // 2
            hm = (lane < HEAD) if h % 2 == 0 else (lane >= HEAD)
            qp = jnp.where(hm, qk_ref[:, p * LANES:(p + 1) * LANES], 0.0)
            kp = jnp.where(hm, qk_ref[:, HALF // 2 + p * LANES:HALF // 2 + (p + 1) * LANES], 0.0)
            chains.append(dict(
                h=h, lg=lg, s_ref=s_ref, o_ref=o_ref, q16=qp.astype(BF16), k16=kp.astype(BF16),
                qw16=(qp * jnp.exp(lg * qpow)).astype(BF16), kw16=(kp * jnp.exp(lg * kpow)).astype(BF16),
                v16=v_ref[:, h * LANES:(h + 1) * LANES].astype(BF16),
                decay=jnp.where(mask, jnp.exp(lg * dist), 0.0)))
    for ch in chains:
        ch["sc"] = (lax.dot_general(ch["q16"], ch["k16"], NT_DIMS, preferred_element_type=F32)
                    * ch["decay"]).astype(BF16)
        ch["s"] = ch["s_ref"][ch["h"]]
    for ch in chains:
        ch["o"] = (jnp.dot(ch["sc"], ch["v16"], preferred_element_type=F32)
                   + jnp.dot(ch["qw16"], ch["s"].astype(BF16), preferred_element_type=F32))
        ch["kv"] = lax.dot_general(ch["kw16"], ch["v16"], TN, preferred_element_type=F32)
    for ch in chains:
        h = ch["h"]
        ch["o_ref"][:, h * LANES:(h + 1) * LANES] = ch["o"]
        ch["s_ref"][h] = math.exp(ch["lg"] * C) * ch["s"] + ch["kv"]


def _ret_kernel(bf_ref, bb_ref, first_ref, last_ref, seq_ref,
                qkf_ref, vf_ref, qkb_ref, vb_ref, s0_ref, of_ref, ob_ref, sfin_ref, s_ref):
    step = pl.program_id(0)

    @pl.when(first_ref[step] == 1)
    def _():
        s_ref[...] = s0_ref[:, 0]

    _ret_chunks([(False, qkf_ref, vf_ref, s_ref.at[0], of_ref), (True, qkb_ref, vb_ref, s_ref.at[1], ob_ref)])

    @pl.when(last_ref[step] == 1)
    def _():
        sfin_ref[:, 0] = s_ref[...]


def _retention(rqk, zr, s0):
    C = RET_CHUNK
    tabs = _seq_tables(C)
    st = pl.BlockSpec((2, 1, 4, LANES, LANES), lambda i, bf, bb, fi, la, sq: (0, sq[i], 0, 0, 0))
    spec = lambda use_b, col: pl.BlockSpec(
        (C, HALF), lambda i, bf, bb, fi, la, sq: ((bb if use_b else bf)[i], col))
    return pl.pallas_call(
        _ret_kernel,
        out_shape=[jax.ShapeDtypeStruct((T, HALF), F32), jax.ShapeDtypeStruct((T, HALF), F32),
                   jax.ShapeDtypeStruct((2, N_SEQ, 4, LANES, LANES), F32)],
        grid_spec=pltpu.PrefetchScalarGridSpec(
            num_scalar_prefetch=5, grid=(len(tabs[0]),),
            in_specs=[spec(False, 0), spec(False, 1), spec(True, 0), spec(True, 1), st],
            out_specs=[spec(False, 0), spec(True, 0), st],
            scratch_shapes=[pltpu.VMEM((2, 4, LANES, LANES), F32)]),
        compiler_params=_params(("arbitrary",)),
        name="retention",
    )(*tabs, rqk, zr, rqk, zr, s0)


def _out_kernel(*refs, even, n_x):
    x_refs, refs = refs[:n_x], refs[n_x:]
    if even:
        (a_ref, yf_ref, yb_ref, bonus_ref, g_ref, gng_ref, gnb_ref, ones_ref,
         mod_ref, ng_ref, wo_ref, rw_ref, rb_ref,
         y_ref, xp_ref, ti_ref, tg_ref, rk_ref, cnt_ref, run_ref) = refs
        ones_bd = ones_ref[...]
        ys = yf_ref[...] + yb_ref[...]
        mu = _group_sum(ys, ones_bd) * (1.0 / HEAD)
        dv = ys - mu
        var = _group_sum(dv * dv, ones_bd) * (1.0 / HEAD)
        yn = dv * lax.rsqrt(var + RWKV_GN_EPS) * gng_ref[...] + gnb_ref[...]
        left = a_ref[...]
        right = (yn + bonus_ref[...]) * g_ref[...]
    else:
        (c_ref, of_ref, ob_ref, rg_ref, gng_ref,
         mod_ref, ng_ref, wo_ref, rw_ref, rb_ref,
         y_ref, xp_ref, ti_ref, tg_ref, rk_ref, cnt_ref, run_ref) = refs
        left = c_ref[...]
        rg = rg_ref[...]
        gate = rg * jax.nn.sigmoid(rg)
        os_ = of_ref[...] + ob_ref[...]
        parts = []
        for h in range(4):
            oh = os_[:, h * LANES:(h + 1) * LANES]
            mu = jnp.mean(oh, axis=-1, keepdims=True)
            dv = oh - mu
            var = jnp.mean(dv * dv, axis=-1, keepdims=True)
            parts.append(dv * lax.rsqrt(var + 1e-5))
        right = gate * (jnp.concatenate(parts, axis=1) * gng_ref[...])
    mod = mod_ref[0]
    o = (jnp.dot(left.astype(BF16), wo_ref[0:HALF, :], preferred_element_type=F32)
         + jnp.dot(right.astype(BF16), wo_ref[HALF:2 * HALF, :], preferred_element_type=F32))
    y = _token_rows(x_refs) + mod[:, 2 * D:3 * D] * o
    y_ref[...] = y
    yn2 = y * lax.rsqrt(jnp.mean(y * y, axis=-1, keepdims=True) + 1e-6) * ng_ref[...]
    t = yn2 * (1.0 + mod[:, 4 * D:5 * D]) + mod[:, 3 * D:4 * D]
    xp_ref[...] = t
    logits = _mm(t, rw_ref[...], NN, 3) + rb_ref[...]
    lane = lax.broadcasted_iota(I32, (TM, LANES), 1)
    neg = jnp.float32(-jnp.inf)
    lg = jnp.where(lane < N_EXPERTS, logits, neg)
    vals, hits = [], []
    for _ in range(TOP_K):
        m = jnp.max(lg, axis=-1, keepdims=True)
        ix = jnp.min(jnp.where(lg == m, lane, LANES), axis=-1, keepdims=True)
        hit = lane == ix
        vals.append(m)
        hits.append((ix, hit))
        lg = jnp.where(hit, neg, lg)
    es = [jnp.exp(vv - vals[0]) for vv in vals]
    den = es[0] + es[1] + es[2] + es[3]

    @pl.when(pl.program_id(0) == 0)
    def _():
        run_ref[...] = jnp.zeros_like(run_ref)

    member = jnp.zeros((TM, LANES), F32)
    for _, hit in hits:
        member = member + jnp.where(hit, 1.0, 0.0)
    ri = lax.broadcasted_iota(I32, (TM, TM), 0)
    ci = lax.broadcasted_iota(I32, (TM, TM), 1)
    before = jnp.where(ci < ri, 1.0, 0.0).astype(BF16)
    seen = run_ref[...] + jnp.dot(before, member.astype(BF16), preferred_element_type=F32)
    ti = jnp.zeros((TM, LANES), I32)
    tg = jnp.zeros((TM, LANES), F32)
    rk = jnp.zeros((TM, LANES), F32)
    for kk, (ix, hit) in enumerate(hits):
        ti = jnp.where(lane == kk, ix, ti)
        tg = jnp.where(lane == kk, es[kk] / den, tg)
        rk = jnp.where(lane == kk, jnp.sum(jnp.where(hit, seen, 0.0), axis=-1, keepdims=True), rk)
    ti_ref[...] = ti
    tg_ref[...] = tg
    rk_ref[...] = rk.astype(I32)
    run_ref[...] = run_ref[...] + jnp.sum(member, axis=0, keepdims=True)
    cnt_ref[...] = run_ref[...]


def _out_proj(even, mix_args, mix_specs, x, mod, norm_g, w_out_bf16, rw_pad, rb_pad):
    row = lambda i: (i, 0)
    modspec = pl.BlockSpec((1, 1, N_MOD * D), lambda i: (_group(i), 0, 0))
    x_specs, x_args = _token_specs(x)
    in_specs = x_specs + list(mix_specs) + [modspec, _full((1, D)), _full((D, D)), _full((D, LANES)),
                                            _full((1, LANES))]
    args = x_args + list(mix_args) + [mod, norm_g.reshape(1, D), w_out_bf16, rw_pad, rb_pad]
    lane_i = jax.ShapeDtypeStruct((T, LANES), I32)
    lane_spec = pl.BlockSpec((TM, LANES), row)
    return pl.pallas_call(
        functools.partial(_out_kernel, even=even, n_x=len(x_args)),
        out_shape=[jax.ShapeDtypeStruct((T, D), F32), jax.ShapeDtypeStruct((T, D), F32),
                   lane_i, jax.ShapeDtypeStruct((T, LANES), F32), lane_i,
                   jax.ShapeDtypeStruct((1, LANES), F32)],
        grid=(NT,), in_specs=in_specs,
        out_specs=[pl.BlockSpec((TM, D), row), pl.BlockSpec((TM, D), row),
                   lane_spec, lane_spec, lane_spec, _full((1, LANES))],
        scratch_shapes=[pltpu.VMEM((1, LANES), F32)],
        compiler_params=_params(("arbitrary",), 48),
        name="out_proj",
    )(*args)


def _route_kernel(cnt_ref, ti_ref, rk_ref, dest_ref, te_ref, nt_ref):
    cnt = cnt_ref[...].astype(I32)
    ntile = lax.shift_right_logical(cnt + (TMX - 1), TMX.bit_length() - 1)
    ei = lax.broadcasted_iota(I32, (LANES, LANES), 0)
    ej = lax.broadcasted_iota(I32, (LANES, LANES), 1)
    upto = jnp.where(ei <= ej, 1.0, 0.0).astype(BF16)
    ntile_f = jnp.broadcast_to(ntile.astype(F32), (8, LANES))
    tile_end = jnp.dot(ntile_f.astype(BF16), upto, preferred_element_type=F32)[0:1, :]
    row_start = (tile_end - ntile.astype(F32)) * float(TMX)
    lane = lax.broadcasted_iota(I32, (TM, LANES), 1)
    ti = ti_ref[...]
    rk = rk_ref[...]
    spread = jnp.zeros((TM, LANES), F32)
    for k in range(TOP_K):
        hit = lane == ti[:, k:k + 1]
        start = jnp.sum(jnp.where(hit, row_start, 0.0), axis=-1, keepdims=True)
        spread = jnp.where((lane & (TOP_K - 1)) == k, start + rk[:, k:k + 1].astype(F32), spread)
    tok = lax.broadcasted_iota(I32, (TM, LANES), 0)
    keep = (tok & (LANES // TOP_K - 1)) == lax.shift_right_logical(lane, 2)
    flat = jnp.where(keep, spread, 0.0).reshape(TM * TOP_K // LANES, LANES // TOP_K, LANES).sum(axis=1)
    dest_ref[...] = flat.astype(I32)

    @pl.when(pl.program_id(0) == 0)
    def _():
        lane1 = lax.broadcasted_iota(I32, (1, LANES), 1)
        n_tiles = jnp.max(tile_end, axis=-1, keepdims=True)
        last_e = jnp.max(jnp.where(cnt > 0, lane1, 0), axis=-1, keepdims=True)
        tile = lax.broadcasted_iota(I32, (TM, 1), 0).astype(F32)
        te = jnp.sum(jnp.where(tile_end <= tile, 1, 0), axis=-1, keepdims=True)
        te = jnp.where(tile < n_tiles, te, last_e)
        te_ref[...] = jnp.broadcast_to(te, (TM, LANES)).astype(I32)
        first_row = lax.broadcasted_iota(I32, (8, LANES), 0) == 0
        nt_ref[...] = jnp.where(first_row, n_tiles, tile_end).astype(I32)


def _route(cnt, ti, rk):
    row = lambda i: (i, 0)
    return pl.pallas_call(
        _route_kernel,
        out_shape=[jax.ShapeDtypeStruct((N_ASSIGN // LANES, LANES), I32), jax.ShapeDtypeStruct((TM, LANES), I32),
                   jax.ShapeDtypeStruct((8, LANES), I32)],
        grid=(NT,),
        in_specs=[_full((1, LANES)), pl.BlockSpec((TM, LANES), row), pl.BlockSpec((TM, LANES), row)],
        out_specs=[pl.BlockSpec((TM * TOP_K // LANES, LANES), row), _full((TM, LANES)), _full((8, LANES))],
        compiler_params=_params(("arbitrary",)),
        name="moe_route",
    )(cnt, ti, rk)


def _start_row_copy(src_ref, src_row, dst_ref, dst_row, sem, queue):
    pltpu.async_copy(src_ref.at[pl.ds(src_row, 1)], dst_ref.at[pl.ds(dst_row, 1)], sem, priority=queue)


def _wait_tiles(n, src_ref, dst_ref, sem):
    for _ in range(n):
        pltpu.make_async_copy(src_ref, dst_ref, sem).wait()


def _dispatch_kernel(dest_ref, tend_ref, x_ref, xs_ref, zero_ref, sem):
    i = pl.program_id(0)

    @pl.when(i == 0)
    def _():
        zero_ref[...] = jnp.zeros_like(zero_ref)

        def last_tile(e, fn):
            end = tend_ref[e]
            begin = tend_ref[e - 1] if e > 0 else 0

            @pl.when(end > begin)
            def _():
                fn(pltpu.make_async_copy(zero_ref, xs_ref.at[pl.ds((end - 1) * TMX, TMX)], sem))

        def unused_tile(j):
            return pltpu.make_async_copy(zero_ref, xs_ref.at[pl.ds(j * TMX, TMX)], sem)

        def start_unused(j, carry):
            unused_tile(j).start()
            return carry

        def wait_unused(j, carry):
            unused_tile(j).wait()
            return carry

        n_used = tend_ref[N_EXPERTS - 1]
        for e in range(N_EXPERTS):
            last_tile(e, lambda c: c.start())
        lax.fori_loop(n_used, MOE_TILES, start_unused, 0)
        for e in range(N_EXPERTS):
            last_tile(e, lambda c: c.wait())
        lax.fori_loop(n_used, MOE_TILES, wait_unused, 0)

    base = i * (TM * TOP_K)

    def start(r, carry):
        for k in range(TOP_K):
            _start_row_copy(x_ref, r, xs_ref, dest_ref[base + r * TOP_K + k], sem, k % 2)
        return carry

    lax.fori_loop(0, TM, start, 0, unroll=4)
    _wait_tiles(TOP_K, x_ref, xs_ref.at[pl.ds(0, TM)], sem)


def _dispatch(dest_flat, tile_end, xt):
    return pl.pallas_call(
        _dispatch_kernel,
        out_shape=jax.ShapeDtypeStruct((R_PAD, D), F32),
        grid_spec=pltpu.PrefetchScalarGridSpec(
            num_scalar_prefetch=2, grid=(NT,),
            in_specs=[pl.BlockSpec((TM, D), lambda i, d, te: (i, 0))],
            out_specs=pl.BlockSpec(memory_space=pl.ANY),
            scratch_shapes=[pltpu.VMEM((TMX, D), F32), pltpu.SemaphoreType.DMA(())]),
        compiler_params=_params(("arbitrary",)),
        name="moe_dispatch",
    )(dest_flat, tile_end, xt)


W_PARTS = 8


def _expert_weights(i, nt, te_ref, w_ref, wbuf_ref, wsem, w16_ref, group_ref):
    rows = w_ref.shape[1] // W_PARTS

    def fetch(e, buf):
        return [pltpu.make_async_copy(w_ref.at[e, pl.ds(p * rows, rows)], wbuf_ref.at[buf, pl.ds(p * rows, rows)],
                                      wsem.at[buf]) for p in range(W_PARTS)]

    @pl.when(i == 0)
    def _():
        group_ref[0] = 0
        for c in fetch(te_ref[0], 0):
            c.start()

    first = jnp.logical_or(i == 0, te_ref[i] != te_ref[jnp.maximum(i - 1, 0)])

    @pl.when(jnp.logical_and(first, i < nt))
    def _():
        cur = group_ref[0] % 2
        nxt = lax.while_loop(
            lambda j: jnp.logical_and(j < nt, te_ref[jnp.minimum(j, MOE_TILES - 1)] == te_ref[i]),
            lambda j: j + 1, i + 1)

        @pl.when(nxt < nt)
        def _():
            for c in fetch(te_ref[jnp.minimum(nxt, MOE_TILES - 1)], 1 - cur):
                c.start()

        for c in fetch(0, cur):
            c.wait()
        w16_ref[...] = wbuf_ref[cur].astype(BF16)
        group_ref[0] = group_ref[0] + 1


def _experts_kernel(te_ref, nt_ref, xs_ref, wgu_ref, bgu_ref, wdn_ref, bdn_ref, y_ref,
                    gu_buf, gu_sem, gu16_ref, gu_group, dn_buf, dn_sem, dn16_ref, dn_group):
    i = pl.program_id(0)
    nt = nt_ref[0]
    _expert_weights(i, nt, te_ref, wgu_ref, gu_buf, gu_sem, gu16_ref, gu_group)
    _expert_weights(i, nt, te_ref, wdn_ref, dn_buf, dn_sem, dn16_ref, dn_group)

    @pl.when(i < nt)
    def _():
        x16 = xs_ref[...].astype(BF16)
        y = bdn_ref[0]
        for h in range(2):
            gc = slice(h * HALF, (h + 1) * HALF)
            uc = slice(D + h * HALF, D + (h + 1) * HALF)
            g = jnp.dot(x16, gu16_ref[:, gc], preferred_element_type=F32) + bgu_ref[0, :, gc]
            u = jnp.dot(x16, gu16_ref[:, uc], preferred_element_type=F32) + bgu_ref[0, :, uc]
            gt = jnp.minimum(g, SWIGLU_LIMIT)
            up = jnp.clip(u, -SWIGLU_LIMIT, SWIGLU_LIMIT)
            act = ((up + 1.0) * gt * jax.nn.sigmoid(SWIGLU_ALPHA * gt)).astype(BF16)
            y = y + jnp.dot(act, dn16_ref[gc, :], preferred_element_type=F32)
        y_ref[...] = y

    @pl.when(i >= nt)
    def _():
        y_ref[...] = jnp.zeros_like(y_ref)


def _tile_clamped(i, te, nt):
    return (jnp.minimum(i, jnp.maximum(nt[0] - 1, 0)), 0)


def _weight_scratch(n_out):
    return [pltpu.VMEM((2, D, n_out), F32), pltpu.SemaphoreType.DMA((2,)), pltpu.VMEM((D, n_out), BF16),
            pltpu.SMEM((1,), I32)]


def _experts(te, n_tiles, xs, w_gu, b_gu, w_dn, b_dn):
    return pl.pallas_call(
        _experts_kernel,
        out_shape=jax.ShapeDtypeStruct((R_PAD, D), F32),
        grid_spec=pltpu.PrefetchScalarGridSpec(
            num_scalar_prefetch=2, grid=(MOE_TILES,),
            in_specs=[pl.BlockSpec((TMX, D), _tile_clamped), pl.BlockSpec(memory_space=pl.ANY),
                      pl.BlockSpec((1, 1, 2 * D), lambda i, te, nt: (te[i], 0, 0)),
                      pl.BlockSpec(memory_space=pl.ANY),
                      pl.BlockSpec((1, 1, D), lambda i, te, nt: (te[i], 0, 0))],
            out_specs=pl.BlockSpec((TMX, D), lambda i, te, nt: (i, 0)),
            scratch_shapes=_weight_scratch(2 * D) + _weight_scratch(D)),
        compiler_params=_params(("arbitrary",), 58),
        name="moe_experts",
    )(te, n_tiles, xs, w_gu, b_gu, w_dn, b_dn)


def _combine_kernel(dest_ref, x_ref, tg_ref, mod_ref, ys_ref, *rest):
    *o_refs, buf_ref, sem = rest
    i = pl.program_id(0)
    slot = i % 2

    def gather(tile, b):
        base = tile * (TM * TOP_K)

        def body(r, carry):
            for k in range(TOP_K):
                _start_row_copy(ys_ref, dest_ref[base + r * TOP_K + k], buf_ref.at[b, k], r, sem.at[b], k % 2)
            return carry

        lax.fori_loop(0, TM, body, 0, unroll=4)

    @pl.when(i == 0)
    def _():
        gather(0, 0)

    @pl.when(i + 1 < NT)
    def _():
        gather(i + 1, 1 - slot)

    _wait_tiles(TOP_K, ys_ref.at[pl.ds(0, TM)], buf_ref.at[slot, 0], sem.at[slot])
    tg = tg_ref[...]
    f = tg[:, 0:1] * buf_ref[slot, 0]
    for k in range(1, TOP_K):
        f = f + tg[:, k:k + 1] * buf_ref[slot, k]
    res = x_ref[...] + mod_ref[0][:, 5 * D:6 * D] * f
    if len(o_refs) == 1:
        o_refs[0][...] = res
    else:
        @pl.when(i < PROMPT_TILES)
        def _():
            o_refs[0][...] = res

        @pl.when(i >= PROMPT_TILES)
        def _():
            o_refs[1][...] = res


def _combine(dest_flat, x, tg, mod, ys, split=False):
    row = lambda i, d: (i, 0)
    if split:
        out_shape = [jax.ShapeDtypeStruct((T_PROMPT, D), F32), jax.ShapeDtypeStruct((T_SAMPLE, D), F32)]
        out_specs = [pl.BlockSpec((TM, D), lambda i, d: (jnp.minimum(i, PROMPT_TILES - 1), 0)),
                     pl.BlockSpec((TM, D), lambda i, d: (jnp.maximum(i - PROMPT_TILES, 0), 0))]
    else:
        out_shape = jax.ShapeDtypeStruct((T, D), F32)
        out_specs = pl.BlockSpec((TM, D), row)
    return pl.pallas_call(
        _combine_kernel,
        out_shape=out_shape,
        grid_spec=pltpu.PrefetchScalarGridSpec(
            num_scalar_prefetch=1, grid=(NT,),
            in_specs=[pl.BlockSpec((TM, D), row), pl.BlockSpec((TM, LANES), row),
                      pl.BlockSpec((1, 1, N_MOD * D), lambda i, d: (_group(i), 0, 0)),
                      pl.BlockSpec(memory_space=pl.ANY)],
            out_specs=out_specs,
            scratch_shapes=[pltpu.VMEM((2, TOP_K, TM, D), F32), pltpu.SemaphoreType.DMA((2,))]),
        compiler_params=_params(("arbitrary",), 40),
        name="moe_combine",
    )(dest_flat, x, tg, mod, ys)


def _moe(layer, y, xt, ti, tg, rk, cnt, mod, w_gu, b_gu, w_dn, b_dn):
    dest, te, nt = _route(cnt, ti, rk)
    dest_flat = dest.reshape(-1)
    te = te[:MOE_TILES, 0] + layer * N_EXPERTS
    n_tiles = nt[0, :1]
    xs = _dispatch(dest_flat, nt[1, :N_EXPERTS], xt)
    n_all = w_gu.shape[0] * N_EXPERTS
    ys = _experts(te, n_tiles, xs, w_gu.reshape(n_all, D, 2 * D), b_gu.reshape(n_all, 1, 2 * D),
                  w_dn.reshape(n_all, D, D), b_dn.reshape(n_all, 1, D))
    return _combine(dest_flat, y, tg, mod, ys, split=layer == 1)


def _ones_blockdiag():
    idx = np.arange(HALF) // HEAD
    return jnp.asarray((idx[:, None] == idx[None, :]).astype(np.float32), dtype=BF16)


def _rope_tables():
    pos = jnp.arange(L_SAMPLE)
    rowp = (pos // 64).astype(F32)
    colp = (pos % 64).astype(F32)
    nf = HEAD // 4
    inv = jnp.power(10000.0, -jnp.arange(nf, dtype=F32) / nf)
    ar = rowp[:, None] * inv[None, :]
    ac = colp[:, None] * inv[None, :]
    cos64 = jnp.concatenate([jnp.cos(ar), jnp.cos(ar), jnp.cos(ac), jnp.cos(ac)], axis=1)
    sin64 = jnp.concatenate([-jnp.sin(ar), jnp.sin(ar), -jnp.sin(ac), jnp.sin(ac)], axis=1)
    cos = jnp.tile(cos64, (1, HALF // HEAD))
    sin = jnp.tile(sin64, (1, HALF // HEAD))
    ident = jnp.ones((TM, HALF), F32)
    return (jnp.concatenate([ident, cos], axis=0), jnp.concatenate([jnp.zeros((TM, HALF), F32), sin], axis=0))


def _bd_pairs(s):
    lead = s.shape[:-3]
    s = s.reshape(lead + (4, 2, HEAD, HEAD))
    z = jnp.zeros_like(s[..., 0, :, :])
    top = jnp.concatenate([s[..., 0, :, :], z], axis=-1)
    bot = jnp.concatenate([z, s[..., 1, :, :]], axis=-1)
    return jnp.concatenate([top, bot], axis=-2)


def _bd_unpairs(s):
    a = s[..., 0:HEAD, 0:HEAD]
    b = s[..., HEAD:, HEAD:]
    out = jnp.stack([a, b], axis=-3)
    return out.reshape(s.shape[:-3] + (8, HEAD, HEAD))


def kernel(x_prompt, x_sample, state_rwkv, cache_k_diff, cache_v_diff, state_retention, c, c_ctx, norm_g, ada_w, ada_b, e_w_in, e_w_out, sgu_ln_g, sgu_w, sgu_b, rw_mu, rw_w0, rw_w_up, rw_a0, rw_a_up, rw_g_up, rw_k_k, rw_k_a, rw_r_k, rw_gn_g, rw_gn_b, o_w_in, o_w_out, da_qk_g, da_lam, da_subln_g, ret_gn_g, router_w, router_b, ex_w_gu, ex_b_gu, ex_w_dn, ex_b_dn):
    x = (x_prompt.reshape(T_PROMPT, D), x_sample.reshape(T_SAMPLE, D))
    cvec8 = jnp.concatenate([c_ctx[None, :], c, jnp.zeros((3, D), F32)], axis=0)
    mods = _adaln(cvec8, ada_w, ada_b)
    mod0 = mods[0].reshape(8, 1, N_MOD * D)
    mod1 = mods[1].reshape(8, 1, N_MOD * D)
    ones_bd = _ones_blockdiag()
    rw_pad = jnp.pad(router_w, ((0, 0), (0, 0), (0, LANES - N_EXPERTS)))
    rb_pad = jnp.pad(router_b, ((0, 0), (0, LANES - N_EXPERTS))).reshape(2, 1, LANES)
    row = lambda i: (i, 0)
    half = pl.BlockSpec((TM, HALF), row)

    za, zb = _in_proj(x, norm_g[0, 0], mod0, e_w_in[0].astype(BF16), (2 * HALF, B_COLS))
    bs_full = jnp.repeat(sgu_b[0].T, HEAD, axis=1)
    a_out = _sgu(za, sgu_ln_g[0], sgu_w[0].astype(BF16), bs_full)
    zpad = jnp.zeros((2, HEAD, HALF), F32)
    wup_pad = jnp.concatenate([rw_w_up[0], zpad], axis=1).astype(BF16)
    aup_pad = jnp.concatenate([zpad, rw_a_up[0]], axis=1).astype(BF16)
    r, v, kkn, bonus, g, lw, kt, b = _rwkv_prep(zb, rw_mu[0], rw_k_k[0], rw_k_a[0], rw_r_k[0], rw_w0[0], rw_a0[0],
                                                wup_pad, aup_pad, rw_g_up[0].astype(BF16), ones_bd)
    s0_sample = _bd_pairs(jnp.moveaxis(state_rwkv[:, 0], 1, 0))
    s0_rw = jnp.concatenate([jnp.zeros((2, N_PROMPT, 4, LANES, LANES), F32), s0_sample], axis=1)
    yf_rw, yb_rw, sfin_rw = _rwkv_scan(r, v, kkn, lw, kt, b, s0_rw)
    new_rwkv = jnp.moveaxis(_bd_unpairs(sfin_rw[:, :N_PROMPT]), 0, 1)[:, None]
    y0, xp0, ti0, tg0, rk0, cnt0 = _out_proj(
        True,
        [a_out, yf_rw, yb_rw, bonus, g, rw_gn_g[0].reshape(1, HALF), rw_gn_b[0].reshape(1, HALF), ones_bd],
        [half, half, half, half, half, _full((1, HALF)), _full((1, HALF)), _full((HALF, HALF))],
        x, mod0, norm_g[0, 1], e_w_out[0].astype(BF16), rw_pad[0], rb_pad[0])
    x1 = _moe(0, y0, xp0, ti0, tg0, rk0, cnt0, mod0, ex_w_gu, ex_b_gu, ex_w_dn, ex_b_dn)

    zc, zr = _in_proj(x1, norm_g[1, 0], mod1, o_w_in[0].astype(BF16), (3 * HALF, 3 * HALF))
    cos_tab, sin_tab = _rope_tables()
    qkg = jnp.tile(da_qk_g[0], (1, HALF // HEAD))
    cq, ck, ck_raw, rqk = _odd_prep(zc, zr, cos_tab, sin_tab, qkg, ones_bd)
    lambda_init = 0.8 - 0.6 * math.exp(-0.3 * 1)
    lv = da_lam[0]
    lam = jnp.exp(jnp.sum(lv[0] * lv[1])) - jnp.exp(jnp.sum(lv[2] * lv[3])) + lambda_init
    c_out = _attn(cq, ck, zc, 0, N_PROMPT, L_PROMPT, lam, da_subln_g[0], 1.0 - lambda_init,
                  jnp.zeros((T, HALF), F32))
    ctx_k = cache_k_diff[:, 0].reshape(N_SAMPLE, PAST, HALF)
    ctx_v = cache_v_diff[:, 0].reshape(N_SAMPLE, PAST, HALF)
    c_out = _attn(cq, ck, zc, T_PROMPT, N_SAMPLE, L_SAMPLE, lam, da_subln_g[0], 1.0 - lambda_init,
                  c_out, ctx_k, ctx_v)
    sr = jnp.moveaxis(state_retention[:, 0], 1, 0)
    zr0 = jnp.zeros_like(sr)
    s0_sample = jnp.stack([jnp.concatenate([sr[:, :, 0], zr0[:, :, 0]], axis=-2),
                           jnp.concatenate([zr0[:, :, 1], sr[:, :, 1]], axis=-2),
                           jnp.concatenate([sr[:, :, 2], zr0[:, :, 2]], axis=-2),
                           jnp.concatenate([zr0[:, :, 3], sr[:, :, 3]], axis=-2)], axis=2)
    s0_ret = jnp.concatenate([jnp.zeros((2, N_PROMPT, 4, LANES, LANES), F32), s0_sample], axis=1)
    of_ret, ob_ret, rfin = _retention(rqk, zr, s0_ret)
    rfin_p = rfin[:, :N_PROMPT]
    new_ret = jnp.stack([rfin_p[:, :, 0, 0:HEAD], rfin_p[:, :, 1, HEAD:], rfin_p[:, :, 2, 0:HEAD],
                         rfin_p[:, :, 3, HEAD:]], axis=2)
    new_ret = jnp.moveaxis(new_ret, 0, 1)[:, None]
    y1, xp1, ti1, tg1, rk1, cnt1 = _out_proj(
        False,
        [c_out, of_ret, ob_ret, zr, ret_gn_g[0].reshape(1, HALF)],
        [half, half, half, pl.BlockSpec((TM, HALF), lambda i: (i, 2)), _full((1, HALF))],
        x1, mod1, norm_g[1, 1], o_w_out[0].astype(BF16), rw_pad[1], rb_pad[1])
    y_prompt, y_sample = _moe(1, y1, xp1, ti1, tg1, rk1, cnt1, mod1, ex_w_gu, ex_b_gu, ex_w_dn, ex_b_dn)

    new_k = ck_raw[:T_PROMPT].reshape(N_PROMPT, 1, L_PROMPT, 4, LANES)
    new_v = zc[:T_PROMPT, 2 * HALF:3 * HALF].reshape(N_PROMPT, 1, L_PROMPT, 4, LANES)
    return (y_prompt.reshape(N_PROMPT, L_PROMPT, D), y_sample.reshape(N_SAMPLE, L_SAMPLE, D),
            new_rwkv, new_k, new_v, new_ret)
```

```python
import functools
import math

import numpy as np
import jax
import jax.numpy as jnp
from jax import lax
from jax.experimental import pallas as pl
from jax.experimental.pallas import tpu as pltpu

F32 = jnp.float32
BF16 = jnp.bfloat16
I32 = jnp.int32

D = 1024
N_PROMPT, L_PROMPT = 16, 256
N_SAMPLE, L_SAMPLE = 4, 1024
N_SEQ = N_PROMPT + N_SAMPLE
PAST = 256
T_PROMPT = N_PROMPT * L_PROMPT
T_SAMPLE = N_SAMPLE * L_SAMPLE
T = T_PROMPT + T_SAMPLE
TM = 256
NT = T // TM
PROMPT_TILES = T_PROMPT // TM
TILES_PER_SAMPLE = L_SAMPLE // TM
N_MOD = 6
HALF = 512
B_COLS = 1792
HEAD = 64
W_DECAY_SCALE = math.exp(-0.5)
RWKV_GN_EPS = 64e-5
RW_CHUNK = 64
RW_STEP = 256
RET_CHUNK = 128
RET_STEP = 256
RET_EXP = ((5.0, 7.0, 9.0, 11.0), (6.0, 8.0, 10.0, 12.0))
N_EXPERTS = 32
TOP_K = 4
SWIGLU_LIMIT = 7.0
SWIGLU_ALPHA = 1.702
N_ASSIGN = T * TOP_K
TMX = 512
MOE_TILES = N_ASSIGN // TMX + N_EXPERTS
R_PAD = MOE_TILES * TMX
LANES = 128

NN = (((1,), (0,)), ((), ()))
NT_DIMS = (((1,), (1,)), ((), ()))
TN = (((0,), (0,)), ((), ()))


def _group(i):
    return jnp.where(i < PROMPT_TILES, 0, 1 + (i - PROMPT_TILES) // TILES_PER_SAMPLE)


def _mm(a, b, dims=NN, passes=1):
    dg = functools.partial(lax.dot_general, dimension_numbers=dims, preferred_element_type=F32)
    if passes == 1:
        return dg(a.astype(BF16), b.astype(BF16))
    a = a.astype(F32)
    b = b.astype(F32)
    ah = a.astype(BF16)
    al = (a - ah.astype(F32)).astype(BF16)
    bh = b.astype(BF16)
    if passes == 2:
        assert dims[0][0] == (1,)
        m = a.shape[0]
        both = dg(jnp.concatenate([ah, al], axis=0), bh)
        return both[0:m] + both[m:2 * m]
    bl = (b - bh.astype(F32)).astype(BF16)
    if dims[0][0] == (1,):
        m = a.shape[0]
        both = dg(jnp.concatenate([ah, al], axis=0), bh)
        return both[0:m] + (dg(ah, bl) + both[m:2 * m])
    return dg(ah, bh) + (dg(ah, bl) + dg(al, bh))


def _group_sum(x, ones_bd):
    xh = x.astype(BF16)
    xl = (x - xh.astype(F32)).astype(BF16)
    return (jnp.dot(xh, ones_bd, preferred_element_type=F32)
            + jnp.dot(xl, ones_bd, preferred_element_type=F32))


def _full(shape):
    nd = len(shape)
    return pl.BlockSpec(shape, lambda *_: (0,) * nd)


def _params(sem, vmem_mb=None):
    kw = {}
    if vmem_mb is not None:
        kw["vmem_limit_bytes"] = vmem_mb * 1024 * 1024
    return pltpu.CompilerParams(dimension_semantics=sem, **kw)


def _seq_tables(chunk):
    blk_f, blk_b, first, last, seq = [], [], [], [], []
    row = 0
    for s in range(N_SEQ):
        n = (L_PROMPT if s < N_PROMPT else L_SAMPLE) // chunk
        base = row // chunk
        for j in range(n):
            blk_f.append(base + j)
            blk_b.append(base + n - 1 - j)
            first.append(int(j == 0))
            last.append(int(j == n - 1))
            seq.append(s)
        row += n * chunk
    return tuple(np.asarray(a, np.int32) for a in (blk_f, blk_b, first, last, seq))


def _adaln_kernel(c_ref, w_ref, b_ref, o_ref):
    c = c_ref[...]
    s = c * jax.nn.sigmoid(c)
    o_ref[0] = _mm(s, w_ref[0], NN, 3) + b_ref[0]


def _adaln(cvec8, ada_w, ada_b):
    depth, _, n = ada_w.shape
    bn = 1536
    return pl.pallas_call(
        _adaln_kernel,
        out_shape=jax.ShapeDtypeStruct((depth, 8, n), F32),
        grid=(depth, n // bn),
        in_specs=[pl.BlockSpec((8, D), lambda l, j: (0, 0)),
                  pl.BlockSpec((1, D, bn), lambda l, j: (l, 0, j)),
                  pl.BlockSpec((1, 1, bn), lambda l, j: (l, 0, j))],
        out_specs=pl.BlockSpec((1, 8, bn), lambda l, j: (l, 0, j)),
        compiler_params=_params(("arbitrary", "arbitrary"), 40),
        name="adaln",
    )(cvec8, ada_w, ada_b.reshape(depth, 1, n))


def _token_specs(x):
    if isinstance(x, tuple):
        return ([pl.BlockSpec((TM, D), lambda i, *_: (jnp.minimum(i, PROMPT_TILES - 1), 0)),
                 pl.BlockSpec((TM, D), lambda i, *_: (jnp.maximum(i - PROMPT_TILES, 0), 0))], list(x))
    return [pl.BlockSpec((TM, D), lambda i, *_: (i, 0))], [x]


def _token_rows(x_refs):
    if len(x_refs) == 2:
        return jnp.where(pl.program_id(0) < PROMPT_TILES, x_refs[0][...], x_refs[1][...])
    return x_refs[0][...]


def _in_kernel(*refs, splits, n_x):
    x_refs, (g_ref, mod_ref, w_ref), outs = refs[:n_x], refs[n_x:n_x + 3], refs[n_x + 3:]
    x = _token_rows(x_refs)
    mod = mod_ref[0]
    y = x * lax.rsqrt(jnp.mean(x * x, axis=-1, keepdims=True) + 1e-6) * g_ref[...]
    h = (y * (1.0 + mod[:, D:2 * D]) + mod[:, 0:D]).astype(BF16)
    off = 0
    for o_ref, n in zip(outs, splits):
        o_ref[...] = jnp.dot(h, w_ref[:, off:off + n], preferred_element_type=F32)
        off += n


def _in_proj(x, g, mod, w_bf16, splits):
    n = w_bf16.shape[1]
    row = lambda i: (i, 0)
    x_specs, x_args = _token_specs(x)
    return pl.pallas_call(
        functools.partial(_in_kernel, splits=splits, n_x=len(x_args)),
        out_shape=[jax.ShapeDtypeStruct((T, s), F32) for s in splits],
        grid=(NT,),
        in_specs=x_specs + [_full((1, D)), pl.BlockSpec((1, 1, N_MOD * D), lambda i: (_group(i), 0, 0)),
                            _full((D, n))],
        out_specs=[pl.BlockSpec((TM, s), row) for s in splits],
        compiler_params=_params(("arbitrary",), 48),
        name="in_proj",
    )(*x_args, g.reshape(1, D), mod, w_bf16)


def _gelu(x):
    return 0.5 * x * (1.0 + lax.erf(x * (1.0 / math.sqrt(2.0))))


def _sgu_kernel(za_ref, lng_ref, ws_ref, bs_ref, o_ref):
    u = _gelu(za_ref[:, 0:HALF])
    va = _gelu(za_ref[:, HALF:2 * HALF])
    mu = jnp.mean(va, axis=-1, keepdims=True)
    dv = va - mu
    var = jnp.mean(dv * dv, axis=-1, keepdims=True)
    vn = dv * lax.rsqrt(var + 1e-5) * lng_ref[...]
    lane = lax.broadcasted_iota(I32, (LANES, LANES), 1)
    first = lane < HEAD
    for c in range(TM // LANES):
        rows = slice(c * LANES, (c + 1) * LANES)
        for p in range(HALF // LANES):
            cols = slice(p * LANES, (p + 1) * LANES)
            vp = vn[rows, cols]
            s = (jnp.dot(ws_ref[2 * p], jnp.where(first, vp, 0.0).astype(BF16), preferred_element_type=F32)
                 + jnp.dot(ws_ref[2 * p + 1], jnp.where(first, 0.0, vp).astype(BF16), preferred_element_type=F32))
            o_ref[rows, cols] = u[rows, cols] * (s + bs_ref[:, cols])


def _sgu(za, ln_g, w_s_bf16, bs_full):
    return pl.pallas_call(
        _sgu_kernel,
        out_shape=jax.ShapeDtypeStruct((T, HALF), F32),
        grid=(NT,),
        in_specs=[pl.BlockSpec((TM, 2 * HALF), lambda i: (i, 0)), _full((1, HALF)),
                  _full((8, LANES, LANES)), _full((LANES, HALF))],
        out_specs=pl.BlockSpec((TM, HALF), lambda i: (i, 0)),
        compiler_params=_params(("arbitrary",)),
        name="sgu",
    )(za, ln_g.reshape(1, HALF), w_s_bf16, bs_full)


def _rwkv_prep_kernel(zb_ref, zp_ref, zn_ref, mu_ref, kk_ref, ka_ref, rk_ref, w0_ref, a0_ref,
                      wup_ref, aup_ref, gup_ref, ones_ref,
                      r_ref, v_ref, kkn_ref, bonus_ref, g_ref, lw_ref, kt_ref, b_ref):
    i = pl.program_id(0)
    in_sample = i >= PROMPT_TILES
    pos = (i - PROMPT_TILES) % TILES_PER_SAMPLE
    is_first = jnp.logical_or(jnp.logical_not(in_sample), pos == 0)
    is_last = jnp.logical_or(jnp.logical_not(in_sample), pos == TILES_PER_SAMPLE - 1)
    zb = zb_ref[...]
    prev_row = jnp.where(is_first, 0.0, zp_ref[7:8, :])
    next_row = jnp.where(is_last, 0.0, zn_ref[0:1, :])
    rowid = lax.broadcasted_iota(I32, (TM, 1), 0)
    zp = jnp.where(rowid == 0, prev_row, pltpu.roll(zb, 1, 0))
    zn = jnp.where(rowid == TM - 1, next_row, pltpu.roll(zb, TM - 1, 0))
    zs = zb + mu_ref[0:1, :] * (zp - zb) + mu_ref[1:2, :] * (zn - zb)
    r = zs[:, 0:HALF]
    k = zs[:, HALF:2 * HALF]
    v = zs[:, 2 * HALF:3 * HALF]
    wa = zs[:, 3 * HALF:3 * HALF + LANES]
    gd = zs[:, 3 * HALF + LANES:B_COLS]
    ones_bd = ones_ref[...]
    r_ref[...] = r
    v_ref[...] = v
    g_ref[...] = jnp.dot(jax.nn.sigmoid(gd).astype(BF16), gup_ref[...], preferred_element_type=F32)
    kk = k * kk_ref[...]
    kkn = kk / jnp.maximum(jnp.sqrt(_group_sum(kk * kk, ones_bd)), 1e-6)
    kkn_ref[...] = kkn
    bonus_ref[...] = _group_sum(r * k * rk_ref[...], ones_bd) * v
    tw = jnp.tanh(wa).astype(BF16)
    wa16 = wa.astype(BF16)
    for dd in range(2):
        lw_ref[dd] = -W_DECAY_SCALE * jax.nn.sigmoid(
            w0_ref[dd:dd + 1, :] + jnp.dot(tw, wup_ref[dd], preferred_element_type=F32))
        a = jax.nn.sigmoid(a0_ref[dd:dd + 1, :] + jnp.dot(wa16, aup_ref[dd], preferred_element_type=F32))
        kt_ref[dd] = k * (1.0 + (a - 1.0) * ka_ref[...])
        b_ref[dd] = a * kkn


def _rwkv_prep(zb, mu, k_k, k_a, r_k, w0, a0, wup_pad, aup_pad, g_up, ones_bd):
    row = lambda i: (i, 0)
    halo = TM // 8
    one = jax.ShapeDtypeStruct((T, HALF), F32)
    two = jax.ShapeDtypeStruct((2, T, HALF), F32)
    o1 = pl.BlockSpec((TM, HALF), row)
    o2 = pl.BlockSpec((2, TM, HALF), lambda i: (0, i, 0))
    return pl.pallas_call(
        _rwkv_prep_kernel,
        out_shape=[one, one, one, one, one, two, two, two],
        grid=(NT,),
        in_specs=[pl.BlockSpec((TM, B_COLS), row),
                  pl.BlockSpec((8, B_COLS), lambda i: (jnp.maximum(i * halo - 1, 0), 0)),
                  pl.BlockSpec((8, B_COLS), lambda i: (jnp.minimum((i + 1) * halo, T // 8 - 1), 0)),
                  _full((2, B_COLS)), _full((1, HALF)), _full((1, HALF)), _full((1, HALF)),
                  _full((2, HALF)), _full((2, HALF)),
                  _full((2, LANES, HALF)), _full((2, LANES, HALF)), _full((LANES, HALF)),
                  _full((HALF, HALF))],
        out_specs=[o1, o1, o1, o1, o1, o2, o2, o2],
        compiler_params=_params(("arbitrary",), 48),
        name="rwkv_prep",
    )(zb, zb, zb, mu, k_k.reshape(1, HALF), k_a.reshape(1, HALF), r_k.reshape(1, HALF), w0, a0,
      wup_pad, aup_pad, g_up, ones_bd)


def _rwkv_chunks(dirs):
    C = RW_CHUNK
    ti = lax.broadcasted_iota(I32, (C, C), 0)
    tj = lax.broadcasted_iota(I32, (C, C), 1)
    bi = lax.broadcasted_iota(I32, (LANES, LANES), 0)
    bj = lax.broadcasted_iota(I32, (LANES, LANES), 1)
    same = (bi >> 6) == (bj >> 6)
    pi = bi & (C - 1)
    pj = bj & (C - 1)
    eye = (bi == bj).astype(F32)
    h0 = lax.broadcasted_iota(I32, (C, LANES), 1) < HEAD

    def stack(x):
        return jnp.concatenate([jnp.where(h0, x, 0.0), jnp.where(h0, 0.0, x)], axis=0)

    def fold(x):
        return x[0:C] + x[C:2 * C]

    chains = []
    for rev, r, v, kk, lw, kt, b, s_ref, y_ref in dirs:
        tri = jnp.where((tj >= ti) if rev else (tj <= ti), 1.0, 0.0).astype(F32)
        p1 = lw.astype(BF16)
        r1 = lw - p1.astype(F32)
        p2 = r1.astype(BF16)
        p3 = (r1 - p2.astype(F32)).astype(BF16)
        cs3 = jnp.dot(tri.astype(BF16), jnp.concatenate([p1, p2, p3], axis=1), preferred_element_type=F32)
        cs = cs3[:, 0:HALF] + (cs3[:, HALF:2 * HALF] + cs3[:, 2 * HALF:3 * HALF])
        ctot = cs[0:1, :] if rev else cs[C - 1:C, :]
        e_neg = jnp.exp(-cs)
        e_tail = jnp.exp(ctot - cs)
        q1 = kk * jnp.exp(cs - lw)
        k1 = kt * e_neg
        b1 = b * e_neg
        r1 = r * jnp.exp(cs)
        k2 = kt * e_tail
        b2 = b * e_tail
        e_tot = jnp.exp(ctot)
        strict = jnp.logical_and(same, (pj > pi) if rev else (pj < pi))
        incl = jnp.logical_and(same, (pj >= pi) if rev else (pj <= pi))
        for p in range(HALF // LANES):
            cols = slice(p * LANES, (p + 1) * LANES)
            chains.append(dict(p=p, cols=cols, strict=strict, incl=incl, s_ref=s_ref, y_ref=y_ref,
                               q1=q1[:, cols], k1=k1[:, cols], b1=b1[:, cols], r1=r1[:, cols],
                               k2=k2[:, cols], b2=b2[:, cols], v=v[:, cols], e_tot=e_tot[:, cols]))

    for ch in chains:
        lhs = jnp.concatenate([stack(ch["q1"]), stack(ch["r1"])], axis=0)
        rhs = jnp.concatenate([ch["k1"], ch["k1"], ch["b1"], ch["b1"]], axis=0)
        gm = _mm(lhs, rhs, NT_DIMS, 2)
        ch["mk"] = jnp.where(ch["strict"], gm[0:2 * C, 0:2 * C], 0.0)
        ch["mb"] = jnp.where(ch["strict"], gm[0:2 * C, 2 * C:4 * C], 0.0)
        ch["nk"] = jnp.where(ch["incl"], gm[2 * C:4 * C, 0:2 * C], 0.0)
        ch["nb"] = jnp.where(ch["incl"], gm[2 * C:4 * C, 2 * C:4 * C], 0.0)
        ch["tinv"] = eye - jnp.where((pi >> 1) == (pj >> 1), ch["mb"], 0.0)
    size = 2
    while size < C:
        sh = size.bit_length() - 1
        blk = jnp.logical_and((pi >> (sh + 1)) == (pj >> (sh + 1)), (pi >> sh) != (pj >> sh))
        for ch in chains:
            ch["tn"] = _mm(ch["tinv"], jnp.where(blk, ch["mb"], 0.0), NN, 1)
        for ch in chains:
            ch["tinv"] = ch["tinv"] - _mm(ch["tn"], ch["tinv"], NN, 2)
        size *= 2
    for ch in chains:
        vst = stack(ch["v"])
        ch["mkv"] = fold(_mm(ch["mk"], vst, NN, 2))
        ch["nkv"] = fold(_mm(ch["nk"], vst, NN, 1))
        ch["s"] = ch["s_ref"][ch["p"]]
        ch["qr"] = _mm(jnp.concatenate([ch["q1"], ch["r1"]], axis=0), ch["s"], NT_DIMS, 2)
    for ch in chains:
        ch["u"] = fold(_mm(ch["tinv"], stack(ch["mkv"] + ch["qr"][0:C]), NN, 2))
    for ch in chains:
        ch["y_ref"][:, ch["cols"]] = ch["qr"][C:2 * C] + ch["nkv"] - fold(_mm(ch["nb"], stack(ch["u"]), NN, 1))
        upd = _mm(jnp.concatenate([ch["v"], ch["u"]], axis=0),
                  jnp.concatenate([ch["k2"], -ch["b2"]], axis=0), TN, 3)
        ch["s_ref"][ch["p"]] = ch["s"] * ch["e_tot"] + jnp.where(same, upd, 0.0)


def _rwkv_scan_kernel(bf_ref, bb_ref, first_ref, last_ref, seq_ref,
                      rf_ref, vf_ref, kkf_ref, lwf_ref, ktf_ref, bfw_ref,
                      rb_ref, vb_ref, kkb_ref, lwb_ref, ktb_ref, bbw_ref, s0_ref,
                      yf_ref, yb_ref, sfin_ref, s_ref):
    step = pl.program_id(0)

    @pl.when(first_ref[step] == 1)
    def _():
        s_ref[...] = s0_ref[:, 0]

    C = RW_CHUNK
    for sub in range(RW_STEP // C):
        f = pl.ds(sub * C, C)
        b = pl.ds(RW_STEP - (sub + 1) * C, C)
        _rwkv_chunks([
            (False, rf_ref[f, :], vf_ref[f, :], kkf_ref[f, :], lwf_ref[0, f, :], ktf_ref[0, f, :], bfw_ref[0, f, :],
             s_ref.at[0], yf_ref.at[f]),
            (True, rb_ref[b, :], vb_ref[b, :], kkb_ref[b, :], lwb_ref[0, b, :], ktb_ref[0, b, :], bbw_ref[0, b, :],
             s_ref.at[1], yb_ref.at[b])])

    @pl.when(last_ref[step] == 1)
    def _():
        sfin_ref[:, 0] = s_ref[...]


def _rwkv_scan(r, v, kk, lw, kt, b, s0_bd):
    C = RW_STEP
    tabs = _seq_tables(C)
    fwd = lambda i, bf, bb, fi, la, sq: (bf[i], 0)
    bwd = lambda i, bf, bb, fi, la, sq: (bb[i], 0)
    fwd3 = lambda i, bf, bb, fi, la, sq: (0, bf[i], 0)
    bwd3 = lambda i, bf, bb, fi, la, sq: (1, bb[i], 0)
    st = pl.BlockSpec((2, 1, 4, LANES, LANES), lambda i, bf, bb, fi, la, sq: (0, sq[i], 0, 0, 0))
    one_f, one_b = pl.BlockSpec((C, HALF), fwd), pl.BlockSpec((C, HALF), bwd)
    two_f, two_b = pl.BlockSpec((1, C, HALF), fwd3), pl.BlockSpec((1, C, HALF), bwd3)
    return pl.pallas_call(
        _rwkv_scan_kernel,
        out_shape=[jax.ShapeDtypeStruct((T, HALF), F32), jax.ShapeDtypeStruct((T, HALF), F32),
                   jax.ShapeDtypeStruct((2, N_SEQ, 4, LANES, LANES), F32)],
        grid_spec=pltpu.PrefetchScalarGridSpec(
            num_scalar_prefetch=5, grid=(len(tabs[0]),),
            in_specs=[one_f, one_f, one_f, two_f, two_f, two_f,
                      one_b, one_b, one_b, two_b, two_b, two_b, st],
            out_specs=[one_f, one_b, st],
            scratch_shapes=[pltpu.VMEM((2, 4, LANES, LANES), F32)]),
        compiler_params=_params(("arbitrary",)),
        name="rwkv_scan",
    )(*tabs, r, v, kk, lw, kt, b, r, v, kk, lw, kt, b, s0_bd)


def _rope(x, cos, sin_signed, first16):
    w = x.shape[1]
    partner = jnp.where(first16, pltpu.roll(x, w - 16, 1), pltpu.roll(x, 16, 1))
    return x * cos + partner * sin_signed


def _odd_prep_kernel(zc_ref, zr_ref, cos_ref, sin_ref, qkg_ref, ones_ref, cq_ref, ck_ref, ckraw_ref, rqk_ref):
    ones_bd = ones_ref[...]
    cos = cos_ref[...]
    sin = sin_ref[...]
    lane = lax.broadcasted_iota(I32, (TM, HALF), 1)
    first16 = (lane & 31) < 16
    for idx, (o_ref, raw_ref) in enumerate(((cq_ref, None), (ck_ref, ckraw_ref))):
        x = zc_ref[:, idx * HALF:(idx + 1) * HALF]
        ms = _group_sum(x * x, ones_bd) * (1.0 / HEAD)
        xn = x * lax.rsqrt(ms + 1e-6) * qkg_ref[idx:idx + 1, :]
        if raw_ref is not None:
            raw_ref[...] = xn
        o_ref[...] = _rope(xn, cos, sin, first16)
    rqk = _rope(zr_ref[...], cos, sin, first16)
    rqk_ref[...] = jnp.where(lane < HALF // 2, rqk * (HEAD ** -0.5), rqk)


def _odd_prep(zc, zr, cos_tab, sin_tab, qkg_tiled, ones_bd):
    row = lambda i: (i, 0)
    tab = lambda i: (jnp.where(i < PROMPT_TILES, 0, 1 + (i - PROMPT_TILES) % TILES_PER_SAMPLE), 0)
    one = jax.ShapeDtypeStruct((T, HALF), F32)
    o1 = pl.BlockSpec((TM, HALF), row)
    return pl.pallas_call(
        _odd_prep_kernel,
        out_shape=[one, one, one, one], grid=(NT,),
        in_specs=[pl.BlockSpec((TM, 2 * HALF), row), pl.BlockSpec((TM, HALF), row),
                  pl.BlockSpec((TM, HALF), tab), pl.BlockSpec((TM, HALF), tab),
                  _full((2, HALF)), _full((HALF, HALF))],
        out_specs=[o1, o1, o1, o1],
        compiler_params=_params(("arbitrary",)),
        name="odd_prep",
    )(zc, zr, cos_tab, sin_tab, qkg_tiled, ones_bd)


def _attn_kernel(*refs, has_ctx, one_minus_li):
    if has_ctx:
        q_ref, k_ref, v_ref, kc_ref, vc_ref, lam_ref, sg_ref, _, o_ref = refs
    else:
        q_ref, k_ref, v_ref, lam_ref, sg_ref, _, o_ref = refs
    lam = lam_ref[...]
    lane = lax.broadcasted_iota(I32, (LANES, LANES), 1)
    m0 = lane < HEAD
    scale = HEAD ** -0.5
    for h in range(4):
        cols = slice(h * LANES, (h + 1) * LANES)
        qp = q_ref[:, cols]
        segs = [(k_ref[:, cols], v_ref[:, cols])]
        if has_ctx:
            segs.append((kc_ref[0, :, cols], vc_ref[0, :, cols]))
        maps = []
        for qm in (jnp.where(m0, qp, 0.0), jnp.where(m0, 0.0, qp)):
            qm16 = qm.astype(BF16)
            ss = [lax.dot_general(qm16, ks.astype(BF16), NT_DIMS, preferred_element_type=F32) * scale
                  for ks, _ in segs]
            mx = ss[0].max(axis=-1, keepdims=True)
            for s_ in ss[1:]:
                mx = jnp.maximum(mx, s_.max(axis=-1, keepdims=True))
            den = None
            pv = None
            for s_, (_, vs) in zip(ss, segs):
                p_ = jnp.exp(s_ - mx)
                l_ = p_.sum(axis=-1, keepdims=True)
                t = jnp.dot(p_.astype(BF16), vs.astype(BF16), preferred_element_type=F32)
                den = l_ if den is None else den + l_
                pv = t if pv is None else pv + t
            maps.append(pv / den)
        acc = maps[0] - lam * maps[1]
        nrm = acc * lax.rsqrt(jnp.mean(acc * acc, axis=-1, keepdims=True) + 1e-6) * sg_ref[...]
        o_ref[:, cols] = nrm * one_minus_li


def _attn(cq, ck, zc, row0, n_seq, seq_len, lam, subln_g, one_minus_li, prev, ctx_k=None, ctx_v=None):
    nq = seq_len // LANES
    qb0 = row0 // LANES
    sb0 = row0 // seq_len
    in_specs = [pl.BlockSpec((LANES, HALF), lambda s, q: (qb0 + s * nq + q, 0)),
                pl.BlockSpec((seq_len, HALF), lambda s, q: (sb0 + s, 0)),
                pl.BlockSpec((seq_len, HALF), lambda s, q: (sb0 + s, 2))]
    args = [cq, ck, zc]
    if ctx_k is not None:
        in_specs += [pl.BlockSpec((1, PAST, HALF), lambda s, q: (s, 0, 0))] * 2
        args += [ctx_k, ctx_v]
    in_specs += [_full((1, 1)), _full((1, LANES))]
    args += [lam.reshape(1, 1), subln_g.reshape(1, LANES)]
    in_specs.append(pl.BlockSpec(memory_space=pl.ANY))
    args.append(prev)
    aliases = {len(args) - 1: 0}
    return pl.pallas_call(
        functools.partial(_attn_kernel, has_ctx=ctx_k is not None, one_minus_li=one_minus_li),
        out_shape=jax.ShapeDtypeStruct((T, HALF), F32),
        grid=(n_seq, nq), in_specs=in_specs,
        out_specs=pl.BlockSpec((LANES, HALF), lambda s, q: (qb0 + s * nq + q, 0)),
        input_output_aliases=aliases,
        compiler_params=_params(("arbitrary", "arbitrary"), 48),
        name="diff_attn",
    )(*args)


_LOG_GAMMA = tuple(tuple(float(np.log1p(-np.exp2(-np.float32(e)), dtype=np.float32)) for e in es)
                   for es in RET_EXP)


def _ret_chunks(dirs):
    C = RET_CHUNK
    ii = lax.broadcasted_iota(I32, (C, C), 0)
    jj = lax.broadcasted_iota(I32, (C, C), 1)
    ri = lax.broadcasted_iota(I32, (C, 1), 0)
    lane = lax.broadcasted_iota(I32, (C, LANES), 1)
    chains = []
    for rev, qk_ref, v_ref, s_ref, o_ref in dirs:
        mask = (jj > ii) if rev else (jj <= ii)
        dist = jnp.where(mask, (jj - ii) if rev else (ii - jj), 0).astype(F32)
        kpow = (ri if rev else (C - 1 - ri)).astype(F32)
        qpow = ((C - ri) if rev else (ri + 1)).astype(F32)
        for h in range(4):
            lg = _LOG_GAMMA[1 if rev else 0][h]
            p = h // 2
            hm = (lane < HEAD) if h % 2 == 0 else (lane >= HEAD)
            qp = jnp.where(hm, qk_ref[:, p * LANES:(p + 1) * LANES], 0.0)
            kp = jnp.where(hm, qk_ref[:, HALF // 2 + p * LANES:HALF // 2 + (p + 1) * LANES], 0.0)
            chains.append(dict(
                h=h, lg=lg, s_ref=s_ref, o_ref=o_ref, q16=qp.astype(BF16), k16=kp.astype(BF16),
                qw16=(qp * jnp.exp(lg * qpow)).astype(BF16), kw16=(kp * jnp.exp(lg * kpow)).astype(BF16),
                v16=v_ref[:, h * LANES:(h + 1) * LANES].astype(BF16),
                decay=jnp.where(mask, jnp.exp(lg * dist), 0.0)))
    for ch in chains:
        ch["sc"] = (lax.dot_general(ch["q16"], ch["k16"], NT_DIMS, preferred_element_type=F32)
                    * ch["decay"]).astype(BF16)
        ch["s"] = ch["s_ref"][ch["h"]]
    for ch in chains:
        ch["o"] = (jnp.dot(ch["sc"], ch["v16"], preferred_element_type=F32)
                   + jnp.dot(ch["qw16"], ch["s"].astype(BF16), preferred_element_type=F32))
        ch["kv"] = lax.dot_general(ch["kw16"], ch["v16"], TN, preferred_element_type=F32)
    for ch in chains:
        h = ch["h"]
        ch["o_ref"][:, h * LANES:(h + 1) * LANES] = ch["o"]
        ch["s_ref"][h] = math.exp(ch["lg"] * C) * ch["s"] + ch["kv"]


def _ret_kernel(bf_ref, bb_ref, first_ref, last_ref, seq_ref,
                qkf_ref, vf_ref, qkb_ref, vb_ref, s0_ref, of_ref, ob_ref, sfin_ref, s_ref):
    step = pl.program_id(0)

    @pl.when(first_ref[step] == 1)
    def _():
        s_ref[...] = s0_ref[:, 0]

    C = RET_CHUNK
    for sub in range(RET_STEP // C):
        f = pl.ds(sub * C, C)
        b = pl.ds(RET_STEP - (sub + 1) * C, C)
        _ret_chunks([(False, qkf_ref.at[f], vf_ref.at[f], s_ref.at[0], of_ref.at[f]),
                     (True, qkb_ref.at[b], vb_ref.at[b], s_ref.at[1], ob_ref.at[b])])

    @pl.when(last_ref[step] == 1)
    def _():
        sfin_ref[:, 0] = s_ref[...]


def _retention(rqk, zr, s0):
    C = RET_STEP
    tabs = _seq_tables(C)
    st = pl.BlockSpec((2, 1, 4, LANES, LANES), lambda i, bf, bb, fi, la, sq: (0, sq[i], 0, 0, 0))
    spec = lambda use_b, col: pl.BlockSpec(
        (C, HALF), lambda i, bf, bb, fi, la, sq: ((bb if use_b else bf)[i], col))
    return pl.pallas_call(
        _ret_kernel,
        out_shape=[jax.ShapeDtypeStruct((T, HALF), F32), jax.ShapeDtypeStruct((T, HALF), F32),
                   jax.ShapeDtypeStruct((2, N_SEQ, 4, LANES, LANES), F32)],
        grid_spec=pltpu.PrefetchScalarGridSpec(
            num_scalar_prefetch=5, grid=(len(tabs[0]),),
            in_specs=[spec(False, 0), spec(False, 1), spec(True, 0), spec(True, 1), st],
            out_specs=[spec(False, 0), spec(True, 0), st],
            scratch_shapes=[pltpu.VMEM((2, 4, LANES, LANES), F32)]),
        compiler_params=_params(("arbitrary",)),
        name="retention",
    )(*tabs, rqk, zr, rqk, zr, s0)


def _out_kernel(*refs, even, n_x):
    x_refs, refs = refs[:n_x], refs[n_x:]
    if even:
        (a_ref, yf_ref, yb_ref, bonus_ref, g_ref, gng_ref, gnb_ref, ones_ref,
         mod_ref, ng_ref, wo_ref, rw_ref, rb_ref,
         y_ref, xp_ref, ti_ref, tg_ref, rk_ref, cnt_ref, run_ref) = refs
        ones_bd = ones_ref[...]
        ys = yf_ref[...] + yb_ref[...]
        mu = _group_sum(ys, ones_bd) * (1.0 / HEAD)
        dv = ys - mu
        var = _group_sum(dv * dv, ones_bd) * (1.0 / HEAD)
        yn = dv * lax.rsqrt(var + RWKV_GN_EPS) * gng_ref[...] + gnb_ref[...]
        left = a_ref[...]
        right = (yn + bonus_ref[...]) * g_ref[...]
    else:
        (c_ref, of_ref, ob_ref, rg_ref, gng_ref,
         mod_ref, ng_ref, wo_ref, rw_ref, rb_ref,
         y_ref, xp_ref, ti_ref, tg_ref, rk_ref, cnt_ref, run_ref) = refs
        left = c_ref[...]
        rg = rg_ref[...]
        gate = rg * jax.nn.sigmoid(rg)
        os_ = of_ref[...] + ob_ref[...]
        parts = []
        for h in range(4):
            oh = os_[:, h * LANES:(h + 1) * LANES]
            mu = jnp.mean(oh, axis=-1, keepdims=True)
            dv = oh - mu
            var = jnp.mean(dv * dv, axis=-1, keepdims=True)
            parts.append(dv * lax.rsqrt(var + 1e-5))
        right = gate * (jnp.concatenate(parts, axis=1) * gng_ref[...])
    mod = mod_ref[0]
    o = (jnp.dot(left.astype(BF16), wo_ref[0:HALF, :], preferred_element_type=F32)
         + jnp.dot(right.astype(BF16), wo_ref[HALF:2 * HALF, :], preferred_element_type=F32))
    y = _token_rows(x_refs) + mod[:, 2 * D:3 * D] * o
    y_ref[...] = y
    yn2 = y * lax.rsqrt(jnp.mean(y * y, axis=-1, keepdims=True) + 1e-6) * ng_ref[...]
    t = yn2 * (1.0 + mod[:, 4 * D:5 * D]) + mod[:, 3 * D:4 * D]
    xp_ref[...] = t
    logits = _mm(t, rw_ref[...], NN, 3) + rb_ref[...]
    lane = lax.broadcasted_iota(I32, (TM, LANES), 1)
    neg = jnp.float32(-jnp.inf)
    lg = jnp.where(lane < N_EXPERTS, logits, neg)
    vals, hits = [], []
    for _ in range(TOP_K):
        m = jnp.max(lg, axis=-1, keepdims=True)
        ix = jnp.min(jnp.where(lg == m, lane, LANES), axis=-1, keepdims=True)
        hit = lane == ix
        vals.append(m)
        hits.append((ix, hit))
        lg = jnp.where(hit, neg, lg)
    es = [jnp.exp(vv - vals[0]) for vv in vals]
    den = es[0] + es[1] + es[2] + es[3]

    @pl.when(pl.program_id(0) == 0)
    def _():
        run_ref[...] = jnp.zeros_like(run_ref)

    member = jnp.zeros((TM, LANES), F32)
    for _, hit in hits:
        member = member + jnp.where(hit, 1.0, 0.0)
    ri = lax.broadcasted_iota(I32, (TM, TM), 0)
    ci = lax.broadcasted_iota(I32, (TM, TM), 1)
    before = jnp.where(ci < ri, 1.0, 0.0).astype(BF16)
    seen = run_ref[...] + jnp.dot(before, member.astype(BF16), preferred_element_type=F32)
    ti = jnp.zeros((TM, LANES), I32)
    tg = jnp.zeros((TM, LANES), F32)
    rk = jnp.zeros((TM, LANES), F32)
    for kk, (ix, hit) in enumerate(hits):
        ti = jnp.where(lane == kk, ix, ti)
        tg = jnp.where(lane == kk, es[kk] / den, tg)
        rk = jnp.where(lane == kk, jnp.sum(jnp.where(hit, seen, 0.0), axis=-1, keepdims=True), rk)
    ti_ref[...] = ti
    tg_ref[...] = tg
    rk_ref[...] = rk.astype(I32)
    run_ref[...] = run_ref[...] + jnp.sum(member, axis=0, keepdims=True)
    cnt_ref[...] = run_ref[...]


def _out_proj(even, mix_args, mix_specs, x, mod, norm_g, w_out_bf16, rw_pad, rb_pad):
    row = lambda i: (i, 0)
    modspec = pl.BlockSpec((1, 1, N_MOD * D), lambda i: (_group(i), 0, 0))
    x_specs, x_args = _token_specs(x)
    in_specs = x_specs + list(mix_specs) + [modspec, _full((1, D)), _full((D, D)), _full((D, LANES)),
                                            _full((1, LANES))]
    args = x_args + list(mix_args) + [mod, norm_g.reshape(1, D), w_out_bf16, rw_pad, rb_pad]
    lane_i = jax.ShapeDtypeStruct((T, LANES), I32)
    lane_spec = pl.BlockSpec((TM, LANES), row)
    return pl.pallas_call(
        functools.partial(_out_kernel, even=even, n_x=len(x_args)),
        out_shape=[jax.ShapeDtypeStruct((T, D), F32), jax.ShapeDtypeStruct((T, D), F32),
                   lane_i, jax.ShapeDtypeStruct((T, LANES), F32), lane_i,
                   jax.ShapeDtypeStruct((1, LANES), F32)],
        grid=(NT,), in_specs=in_specs,
        out_specs=[pl.BlockSpec((TM, D), row), pl.BlockSpec((TM, D), row),
                   lane_spec, lane_spec, lane_spec, _full((1, LANES))],
        scratch_shapes=[pltpu.VMEM((1, LANES), F32)],
        compiler_params=_params(("arbitrary",), 48),
        name="out_proj",
    )(*args)


def _route_kernel(cnt_ref, ti_ref, rk_ref, dest_ref, te_ref, nt_ref):
    cnt = cnt_ref[...].astype(I32)
    ntile = lax.shift_right_logical(cnt + (TMX - 1), TMX.bit_length() - 1)
    ei = lax.broadcasted_iota(I32, (LANES, LANES), 0)
    ej = lax.broadcasted_iota(I32, (LANES, LANES), 1)
    upto = jnp.where(ei <= ej, 1.0, 0.0).astype(BF16)
    ntile_f = jnp.broadcast_to(ntile.astype(F32), (8, LANES))
    tile_end = jnp.dot(ntile_f.astype(BF16), upto, preferred_element_type=F32)[0:1, :]
    row_start = (tile_end - ntile.astype(F32)) * float(TMX)
    lane = lax.broadcasted_iota(I32, (TM, LANES), 1)
    ti = ti_ref[...]
    rk = rk_ref[...]
    spread = jnp.zeros((TM, LANES), F32)
    for k in range(TOP_K):
        hit = lane == ti[:, k:k + 1]
        start = jnp.sum(jnp.where(hit, row_start, 0.0), axis=-1, keepdims=True)
        spread = jnp.where((lane & (TOP_K - 1)) == k, start + rk[:, k:k + 1].astype(F32), spread)
    tok = lax.broadcasted_iota(I32, (TM, LANES), 0)
    keep = (tok & (LANES // TOP_K - 1)) == lax.shift_right_logical(lane, 2)
    flat = jnp.where(keep, spread, 0.0).reshape(TM * TOP_K // LANES, LANES // TOP_K, LANES).sum(axis=1)
    dest_ref[...] = flat.astype(I32)

    @pl.when(pl.program_id(0) == 0)
    def _():
        lane1 = lax.broadcasted_iota(I32, (1, LANES), 1)
        n_tiles = jnp.max(tile_end, axis=-1, keepdims=True)
        last_e = jnp.max(jnp.where(cnt > 0, lane1, 0), axis=-1, keepdims=True)
        tile = lax.broadcasted_iota(I32, (TM, 1), 0).astype(F32)
        te = jnp.sum(jnp.where(tile_end <= tile, 1, 0), axis=-1, keepdims=True)
        te = jnp.where(tile < n_tiles, te, last_e)
        te_ref[...] = jnp.broadcast_to(te, (TM, LANES)).astype(I32)
        first_row = lax.broadcasted_iota(I32, (8, LANES), 0) == 0
        nt_ref[...] = jnp.where(first_row, n_tiles, tile_end).astype(I32)


def _route(cnt, ti, rk):
    row = lambda i: (i, 0)
    return pl.pallas_call(
        _route_kernel,
        out_shape=[jax.ShapeDtypeStruct((N_ASSIGN // LANES, LANES), I32), jax.ShapeDtypeStruct((TM, LANES), I32),
                   jax.ShapeDtypeStruct((8, LANES), I32)],
        grid=(NT,),
        in_specs=[_full((1, LANES)), pl.BlockSpec((TM, LANES), row), pl.BlockSpec((TM, LANES), row)],
        out_specs=[pl.BlockSpec((TM * TOP_K // LANES, LANES), row), _full((TM, LANES)), _full((8, LANES))],
        compiler_params=_params(("arbitrary",)),
        name="moe_route",
    )(cnt, ti, rk)


def _start_row_copy(src_ref, src_row, dst_ref, dst_row, sem, queue):
    pltpu.async_copy(src_ref.at[pl.ds(src_row, 1)], dst_ref.at[pl.ds(dst_row, 1)], sem, priority=queue)


def _wait_tiles(n, src_ref, dst_ref, sem):
    for _ in range(n):
        pltpu.make_async_copy(src_ref, dst_ref, sem).wait()


def _dispatch_kernel(dest_ref, tend_ref, x_ref, xs_ref, zero_ref, sem):
    i = pl.program_id(0)

    @pl.when(i == 0)
    def _():
        zero_ref[...] = jnp.zeros_like(zero_ref)

        def last_tile(e, fn):
            end = tend_ref[e]
            begin = tend_ref[e - 1] if e > 0 else 0

            @pl.when(end > begin)
            def _():
                fn(pltpu.make_async_copy(zero_ref, xs_ref.at[pl.ds((end - 1) * TMX, TMX)], sem))

        def unused_tile(j):
            return pltpu.make_async_copy(zero_ref, xs_ref.at[pl.ds(j * TMX, TMX)], sem)

        def start_unused(j, carry):
            unused_tile(j).start()
            return carry

        def wait_unused(j, carry):
            unused_tile(j).wait()
            return carry

        n_used = tend_ref[N_EXPERTS - 1]
        for e in range(N_EXPERTS):
            last_tile(e, lambda c: c.start())
        lax.fori_loop(n_used, MOE_TILES, start_unused, 0)
        for e in range(N_EXPERTS):
            last_tile(e, lambda c: c.wait())
        lax.fori_loop(n_used, MOE_TILES, wait_unused, 0)

    base = i * (TM * TOP_K)

    def start(r, carry):
        for k in range(TOP_K):
            _start_row_copy(x_ref, r, xs_ref, dest_ref[base + r * TOP_K + k], sem, k % 2)
        return carry

    lax.fori_loop(0, TM, start, 0, unroll=4)
    _wait_tiles(TOP_K, x_ref, xs_ref.at[pl.ds(0, TM)], sem)


def _dispatch(dest_flat, tile_end, xt):
    return pl.pallas_call(
        _dispatch_kernel,
        out_shape=jax.ShapeDtypeStruct((R_PAD, D), F32),
        grid_spec=pltpu.PrefetchScalarGridSpec(
            num_scalar_prefetch=2, grid=(NT,),
            in_specs=[pl.BlockSpec((TM, D), lambda i, d, te: (i, 0))],
            out_specs=pl.BlockSpec(memory_space=pl.ANY),
            scratch_shapes=[pltpu.VMEM((TMX, D), F32), pltpu.SemaphoreType.DMA(())]),
        compiler_params=_params(("arbitrary",)),
        name="moe_dispatch",
    )(dest_flat, tile_end, xt)


W_PARTS = 8


def _expert_weights(i, nt, te_ref, w_ref, wbuf_ref, wsem, w16_ref, group_ref):
    rows = w_ref.shape[1] // W_PARTS

    def fetch(e, buf):
        return [pltpu.make_async_copy(w_ref.at[e, pl.ds(p * rows, rows)], wbuf_ref.at[buf, pl.ds(p * rows, rows)],
                                      wsem.at[buf]) for p in range(W_PARTS)]

    @pl.when(i == 0)
    def _():
        group_ref[0] = 0
        for c in fetch(te_ref[0], 0):
            c.start()

    first = jnp.logical_or(i == 0, te_ref[i] != te_ref[jnp.maximum(i - 1, 0)])

    @pl.when(jnp.logical_and(first, i < nt))
    def _():
        cur = group_ref[0] % 2
        nxt = lax.while_loop(
            lambda j: jnp.logical_and(j < nt, te_ref[jnp.minimum(j, MOE_TILES - 1)] == te_ref[i]),
            lambda j: j + 1, i + 1)

        @pl.when(nxt < nt)
        def _():
            for c in fetch(te_ref[jnp.minimum(nxt, MOE_TILES - 1)], 1 - cur):
                c.start()

        for c in fetch(0, cur):
            c.wait()
        w16_ref[...] = wbuf_ref[cur].astype(BF16)
        group_ref[0] = group_ref[0] + 1


def _experts_kernel(te_ref, nt_ref, xs_ref, wgu_ref, bgu_ref, wdn_ref, bdn_ref, y_ref,
                    gu_buf, gu_sem, gu16_ref, gu_group, dn_buf, dn_sem, dn16_ref, dn_group):
    i = pl.program_id(0)
    nt = nt_ref[0]
    _expert_weights(i, nt, te_ref, wgu_ref, gu_buf, gu_sem, gu16_ref, gu_group)
    _expert_weights(i, nt, te_ref, wdn_ref, dn_buf, dn_sem, dn16_ref, dn_group)

    @pl.when(i < nt)
    def _():
        x16 = xs_ref[...].astype(BF16)
        y = bdn_ref[0]
        for h in range(2):
            gc = slice(h * HALF, (h + 1) * HALF)
            uc = slice(D + h * HALF, D + (h + 1) * HALF)
            g = jnp.dot(x16, gu16_ref[:, gc], preferred_element_type=F32) + bgu_ref[0, :, gc]
            u = jnp.dot(x16, gu16_ref[:, uc], preferred_element_type=F32) + bgu_ref[0, :, uc]
            gt = jnp.minimum(g, SWIGLU_LIMIT)
            up = jnp.clip(u, -SWIGLU_LIMIT, SWIGLU_LIMIT)
            act = ((up + 1.0) * gt * jax.nn.sigmoid(SWIGLU_ALPHA * gt)).astype(BF16)
            y = y + jnp.dot(act, dn16_ref[gc, :], preferred_element_type=F32)
        y_ref[...] = y

    @pl.when(i >= nt)
    def _():
        y_ref[...] = jnp.zeros_like(y_ref)


def _tile_clamped(i, te, nt):
    return (jnp.minimum(i, jnp.maximum(nt[0] - 1, 0)), 0)


def _weight_scratch(n_out):
    return [pltpu.VMEM((2, D, n_out), F32), pltpu.SemaphoreType.DMA((2,)), pltpu.VMEM((D, n_out), BF16),
            pltpu.SMEM((1,), I32)]


def _experts(te, n_tiles, xs, w_gu, b_gu, w_dn, b_dn):
    return pl.pallas_call(
        _experts_kernel,
        out_shape=jax.ShapeDtypeStruct((R_PAD, D), F32),
        grid_spec=pltpu.PrefetchScalarGridSpec(
            num_scalar_prefetch=2, grid=(MOE_TILES,),
            in_specs=[pl.BlockSpec((TMX, D), _tile_clamped), pl.BlockSpec(memory_space=pl.ANY),
                      pl.BlockSpec((1, 1, 2 * D), lambda i, te, nt: (te[i], 0, 0)),
                      pl.BlockSpec(memory_space=pl.ANY),
                      pl.BlockSpec((1, 1, D), lambda i, te, nt: (te[i], 0, 0))],
            out_specs=pl.BlockSpec((TMX, D), lambda i, te, nt: (i, 0)),
            scratch_shapes=_weight_scratch(2 * D) + _weight_scratch(D)),
        compiler_params=_params(("arbitrary",), 58),
        name="moe_experts",
    )(te, n_tiles, xs, w_gu, b_gu, w_dn, b_dn)


def _combine_kernel(dest_ref, x_ref, tg_ref, mod_ref, ys_ref, *rest):
    *o_refs, buf_ref, sem = rest
    i = pl.program_id(0)
    slot = i % 2

    def gather(tile, b):
        base = tile * (TM * TOP_K)

        def body(r, carry):
            for k in range(TOP_K):
                _start_row_copy(ys_ref, dest_ref[base + r * TOP_K + k], buf_ref.at[b, k], r, sem.at[b], k % 2)
            return carry

        lax.fori_loop(0, TM, body, 0, unroll=4)

    @pl.when(i == 0)
    def _():
        gather(0, 0)

    @pl.when(i + 1 < NT)
    def _():
        gather(i + 1, 1 - slot)

    _wait_tiles(TOP_K, ys_ref.at[pl.ds(0, TM)], buf_ref.at[slot, 0], sem.at[slot])
    tg = tg_ref[...]
    f = tg[:, 0:1] * buf_ref[slot, 0]
    for k in range(1, TOP_K):
        f = f + tg[:, k:k + 1] * buf_ref[slot, k]
    res = x_ref[...] + mod_ref[0][:, 5 * D:6 * D] * f
    if len(o_refs) == 1:
        o_refs[0][...] = res
    else:
        @pl.when(i < PROMPT_TILES)
        def _():
            o_refs[0][...] = res

        @pl.when(i >= PROMPT_TILES)
        def _():
            o_refs[1][...] = res


def _combine(dest_flat, x, tg, mod, ys, split=False):
    row = lambda i, d: (i, 0)
    if split:
        out_shape = [jax.ShapeDtypeStruct((T_PROMPT, D), F32), jax.ShapeDtypeStruct((T_SAMPLE, D), F32)]
        out_specs = [pl.BlockSpec((TM, D), lambda i, d: (jnp.minimum(i, PROMPT_TILES - 1), 0)),
                     pl.BlockSpec((TM, D), lambda i, d: (jnp.maximum(i - PROMPT_TILES, 0), 0))]
    else:
        out_shape = jax.ShapeDtypeStruct((T, D), F32)
        out_specs = pl.BlockSpec((TM, D), row)
    return pl.pallas_call(
        _combine_kernel,
        out_shape=out_shape,
        grid_spec=pltpu.PrefetchScalarGridSpec(
            num_scalar_prefetch=1, grid=(NT,),
            in_specs=[pl.BlockSpec((TM, D), row), pl.BlockSpec((TM, LANES), row),
                      pl.BlockSpec((1, 1, N_MOD * D), lambda i, d: (_group(i), 0, 0)),
                      pl.BlockSpec(memory_space=pl.ANY)],
            out_specs=out_specs,
            scratch_shapes=[pltpu.VMEM((2, TOP_K, TM, D), F32), pltpu.SemaphoreType.DMA((2,))]),
        compiler_params=_params(("arbitrary",), 40),
        name="moe_combine",
    )(dest_flat, x, tg, mod, ys)


def _moe(layer, y, xt, ti, tg, rk, cnt, mod, w_gu, b_gu, w_dn, b_dn):
    dest, te, nt = _route(cnt, ti, rk)
    dest_flat = dest.reshape(-1)
    te = te[:MOE_TILES, 0] + layer * N_EXPERTS
    n_tiles = nt[0, :1]
    xs = _dispatch(dest_flat, nt[1, :N_EXPERTS], xt)
    n_all = w_gu.shape[0] * N_EXPERTS
    ys = _experts(te, n_tiles, xs, w_gu.reshape(n_all, D, 2 * D), b_gu.reshape(n_all, 1, 2 * D),
                  w_dn.reshape(n_all, D, D), b_dn.reshape(n_all, 1, D))
    return _combine(dest_flat, y, tg, mod, ys, split=layer == 1)


def _ones_blockdiag():
    idx = np.arange(HALF) // HEAD
    return jnp.asarray((idx[:, None] == idx[None, :]).astype(np.float32), dtype=BF16)


def _rope_tables():
    pos = jnp.arange(L_SAMPLE)
    rowp = (pos // 64).astype(F32)
    colp = (pos % 64).astype(F32)
    nf = HEAD // 4
    inv = jnp.power(10000.0, -jnp.arange(nf, dtype=F32) / nf)
    ar = rowp[:, None] * inv[None, :]
    ac = colp[:, None] * inv[None, :]
    cos64 = jnp.concatenate([jnp.cos(ar), jnp.cos(ar), jnp.cos(ac), jnp.cos(ac)], axis=1)
    sin64 = jnp.concatenate([-jnp.sin(ar), jnp.sin(ar), -jnp.sin(ac), jnp.sin(ac)], axis=1)
    cos = jnp.tile(cos64, (1, HALF // HEAD))
    sin = jnp.tile(sin64, (1, HALF // HEAD))
    ident = jnp.ones((TM, HALF), F32)
    return (jnp.concatenate([ident, cos], axis=0), jnp.concatenate([jnp.zeros((TM, HALF), F32), sin], axis=0))


def _bd_pairs(s):
    lead = s.shape[:-3]
    s = s.reshape(lead + (4, 2, HEAD, HEAD))
    z = jnp.zeros_like(s[..., 0, :, :])
    top = jnp.concatenate([s[..., 0, :, :], z], axis=-1)
    bot = jnp.concatenate([z, s[..., 1, :, :]], axis=-1)
    return jnp.concatenate([top, bot], axis=-2)


def _bd_unpairs(s):
    a = s[..., 0:HEAD, 0:HEAD]
    b = s[..., HEAD:, HEAD:]
    out = jnp.stack([a, b], axis=-3)
    return out.reshape(s.shape[:-3] + (8, HEAD, HEAD))


def kernel(x_prompt, x_sample, state_rwkv, cache_k_diff, cache_v_diff, state_retention, c, c_ctx, norm_g, ada_w, ada_b, e_w_in, e_w_out, sgu_ln_g, sgu_w, sgu_b, rw_mu, rw_w0, rw_w_up, rw_a0, rw_a_up, rw_g_up, rw_k_k, rw_k_a, rw_r_k, rw_gn_g, rw_gn_b, o_w_in, o_w_out, da_qk_g, da_lam, da_subln_g, ret_gn_g, router_w, router_b, ex_w_gu, ex_b_gu, ex_w_dn, ex_b_dn):
    x = (x_prompt.reshape(T_PROMPT, D), x_sample.reshape(T_SAMPLE, D))
    cvec8 = jnp.concatenate([c_ctx[None, :], c, jnp.zeros((3, D), F32)], axis=0)
    mods = _adaln(cvec8, ada_w, ada_b)
    mod0 = mods[0].reshape(8, 1, N_MOD * D)
    mod1 = mods[1].reshape(8, 1, N_MOD * D)
    ones_bd = _ones_blockdiag()
    rw_pad = jnp.pad(router_w, ((0, 0), (0, 0), (0, LANES - N_EXPERTS)))
    rb_pad = jnp.pad(router_b, ((0, 0), (0, LANES - N_EXPERTS))).reshape(2, 1, LANES)
    row = lambda i: (i, 0)
    half = pl.BlockSpec((TM, HALF), row)

    za, zb = _in_proj(x, norm_g[0, 0], mod0, e_w_in[0].astype(BF16), (2 * HALF, B_COLS))
    bs_full = jnp.repeat(sgu_b[0].T, HEAD, axis=1)
    a_out = _sgu(za, sgu_ln_g[0], sgu_w[0].astype(BF16), bs_full)
    zpad = jnp.zeros((2, HEAD, HALF), F32)
    wup_pad = jnp.concatenate([rw_w_up[0], zpad], axis=1).astype(BF16)
    aup_pad = jnp.concatenate([zpad, rw_a_up[0]], axis=1).astype(BF16)
    r, v, kkn, bonus, g, lw, kt, b = _rwkv_prep(zb, rw_mu[0], rw_k_k[0], rw_k_a[0], rw_r_k[0], rw_w0[0], rw_a0[0],
                                                wup_pad, aup_pad, rw_g_up[0].astype(BF16), ones_bd)
    s0_sample = _bd_pairs(jnp.moveaxis(state_rwkv[:, 0], 1, 0))
    s0_rw = jnp.concatenate([jnp.zeros((2, N_PROMPT, 4, LANES, LANES), F32), s0_sample], axis=1)
    yf_rw, yb_rw, sfin_rw = _rwkv_scan(r, v, kkn, lw, kt, b, s0_rw)
    new_rwkv = jnp.moveaxis(_bd_unpairs(sfin_rw[:, :N_PROMPT]), 0, 1)[:, None]
    y0, xp0, ti0, tg0, rk0, cnt0 = _out_proj(
        True,
        [a_out, yf_rw, yb_rw, bonus, g, rw_gn_g[0].reshape(1, HALF), rw_gn_b[0].reshape(1, HALF), ones_bd],
        [half, half, half, half, half, _full((1, HALF)), _full((1, HALF)), _full((HALF, HALF))],
        x, mod0, norm_g[0, 1], e_w_out[0].astype(BF16), rw_pad[0], rb_pad[0])
    x1 = _moe(0, y0, xp0, ti0, tg0, rk0, cnt0, mod0, ex_w_gu, ex_b_gu, ex_w_dn, ex_b_dn)

    zc, zr = _in_proj(x1, norm_g[1, 0], mod1, o_w_in[0].astype(BF16), (3 * HALF, 3 * HALF))
    cos_tab, sin_tab = _rope_tables()
    qkg = jnp.tile(da_qk_g[0], (1, HALF // HEAD))
    cq, ck, ck_raw, rqk = _odd_prep(zc, zr, cos_tab, sin_tab, qkg, ones_bd)
    lambda_init = 0.8 - 0.6 * math.exp(-0.3 * 1)
    lv = da_lam[0]
    lam = jnp.exp(jnp.sum(lv[0] * lv[1])) - jnp.exp(jnp.sum(lv[2] * lv[3])) + lambda_init
    c_out = _attn(cq, ck, zc, 0, N_PROMPT, L_PROMPT, lam, da_subln_g[0], 1.0 - lambda_init,
                  jnp.zeros((T, HALF), F32))
    ctx_k = cache_k_diff[:, 0].reshape(N_SAMPLE, PAST, HALF)
    ctx_v = cache_v_diff[:, 0].reshape(N_SAMPLE, PAST, HALF)
    c_out = _attn(cq, ck, zc, T_PROMPT, N_SAMPLE, L_SAMPLE, lam, da_subln_g[0], 1.0 - lambda_init,
                  c_out, ctx_k, ctx_v)
    sr = jnp.moveaxis(state_retention[:, 0], 1, 0)
    zr0 = jnp.zeros_like(sr)
    s0_sample = jnp.stack([jnp.concatenate([sr[:, :, 0], zr0[:, :, 0]], axis=-2),
                           jnp.concatenate([zr0[:, :, 1], sr[:, :, 1]], axis=-2),
                           jnp.concatenate([sr[:, :, 2], zr0[:, :, 2]], axis=-2),
                           jnp.concatenate([zr0[:, :, 3], sr[:, :, 3]], axis=-2)], axis=2)
    s0_ret = jnp.concatenate([jnp.zeros((2, N_PROMPT, 4, LANES, LANES), F32), s0_sample], axis=1)
    of_ret, ob_ret, rfin = _retention(rqk, zr, s0_ret)
    rfin_p = rfin[:, :N_PROMPT]
    new_ret = jnp.stack([rfin_p[:, :, 0, 0:HEAD], rfin_p[:, :, 1, HEAD:], rfin_p[:, :, 2, 0:HEAD],
                         rfin_p[:, :, 3, HEAD:]], axis=2)
    new_ret = jnp.moveaxis(new_ret, 0, 1)[:, None]
    y1, xp1, ti1, tg1, rk1, cnt1 = _out_proj(
        False,
        [c_out, of_ret, ob_ret, zr, ret_gn_g[0].reshape(1, HALF)],
        [half, half, half, pl.BlockSpec((TM, HALF), lambda i: (i, 2)), _full((1, HALF))],
        x1, mod1, norm_g[1, 1], o_w_out[0].astype(BF16), rw_pad[1], rb_pad[1])
    y_prompt, y_sample = _moe(1, y1, xp1, ti1, tg1, rk1, cnt1, mod1, ex_w_gu, ex_b_gu, ex_w_dn, ex_b_dn)

    new_k = ck_raw[:T_PROMPT].reshape(N_PROMPT, 1, L_PROMPT, 4, LANES)
    new_v = zc[:T_PROMPT, 2 * HALF:3 * HALF].reshape(N_PROMPT, 1, L_PROMPT, 4, LANES)
    return (y_prompt.reshape(N_PROMPT, L_PROMPT, D), y_sample.reshape(N_SAMPLE, L_SAMPLE, D),
            new_rwkv, new_k, new_v, new_ret)
```

```python
import functools
import math

import numpy as np
import jax
import jax.numpy as jnp
from jax import lax
from jax.experimental import pallas as pl
from jax.experimental.pallas import tpu as pltpu

F32 = jnp.float32
BF16 = jnp.bfloat16
I32 = jnp.int32

D = 1024
N_PROMPT, L_PROMPT = 16, 256
N_SAMPLE, L_SAMPLE = 4, 1024
N_SEQ = N_PROMPT + N_SAMPLE
PAST = 256
T_PROMPT = N_PROMPT * L_PROMPT
T_SAMPLE = N_SAMPLE * L_SAMPLE
T = T_PROMPT + T_SAMPLE
TM = 256
NT = T // TM
PROMPT_TILES = T_PROMPT // TM
TILES_PER_SAMPLE = L_SAMPLE // TM
N_MOD = 6
HALF = 512
B_COLS = 1792
HEAD = 64
W_DECAY_SCALE = math.exp(-0.5)
RWKV_GN_EPS = 64e-5
RW_CHUNK = 64
RW_STEP = 256
RET_CHUNK = 128
RET_STEP = 256
RET_EXP = ((5.0, 7.0, 9.0, 11.0), (6.0, 8.0, 10.0, 12.0))
N_EXPERTS = 32
TOP_K = 4
SWIGLU_LIMIT = 7.0
SWIGLU_ALPHA = 1.702
N_ASSIGN = T * TOP_K
TMX = 512
MOE_TILES = N_ASSIGN // TMX + N_EXPERTS
R_PAD = MOE_TILES * TMX
LANES = 128

NN = (((1,), (0,)), ((), ()))
NT_DIMS = (((1,), (1,)), ((), ()))
TN = (((0,), (0,)), ((), ()))


def _group(i):
    return jnp.where(i < PROMPT_TILES, 0, 1 + (i - PROMPT_TILES) // TILES_PER_SAMPLE)


def _mm(a, b, dims=NN, passes=1):
    dg = functools.partial(lax.dot_general, dimension_numbers=dims, preferred_element_type=F32)
    if passes == 1:
        return dg(a.astype(BF16), b.astype(BF16))
    a = a.astype(F32)
    b = b.astype(F32)
    ah = a.astype(BF16)
    al = (a - ah.astype(F32)).astype(BF16)
    bh = b.astype(BF16)
    if passes == 2:
        assert dims[0][0] == (1,)
        m = a.shape[0]
        both = dg(jnp.concatenate([ah, al], axis=0), bh)
        return both[0:m] + both[m:2 * m]
    bl = (b - bh.astype(F32)).astype(BF16)
    if dims[0][0] == (1,):
        m = a.shape[0]
        both = dg(jnp.concatenate([ah, al], axis=0), bh)
        return both[0:m] + (dg(ah, bl) + both[m:2 * m])
    return dg(ah, bh) + (dg(ah, bl) + dg(al, bh))


def _group_sum(x, ones_bd):
    xh = x.astype(BF16)
    xl = (x - xh.astype(F32)).astype(BF16)
    return (jnp.dot(xh, ones_bd, preferred_element_type=F32)
            + jnp.dot(xl, ones_bd, preferred_element_type=F32))


def _full(shape):
    nd = len(shape)
    return pl.BlockSpec(shape, lambda *_: (0,) * nd)


def _params(sem, vmem_mb=None):
    kw = {}
    if vmem_mb is not None:
        kw["vmem_limit_bytes"] = vmem_mb * 1024 * 1024
    return pltpu.CompilerParams(dimension_semantics=sem, **kw)


def _seq_tables(chunk):
    blk_f, blk_b, first, last, seq = [], [], [], [], []
    row = 0
    for s in range(N_SEQ):
        n = (L_PROMPT if s < N_PROMPT else L_SAMPLE) // chunk
        base = row // chunk
        for j in range(n):
            blk_f.append(base + j)
            blk_b.append(base + n - 1 - j)
            first.append(int(j == 0))
            last.append(int(j == n - 1))
            seq.append(s)
        row += n * chunk
    return tuple(np.asarray(a, np.int32) for a in (blk_f, blk_b, first, last, seq))


def _adaln_kernel(c_ref, w_ref, b_ref, o_ref):
    c = c_ref[...]
    s = c * jax.nn.sigmoid(c)
    o_ref[0] = _mm(s, w_ref[0], NN, 3) + b_ref[0]


def _adaln(cvec8, ada_w, ada_b):
    depth, _, n = ada_w.shape
    bn = 1536
    return pl.pallas_call(
        _adaln_kernel,
        out_shape=jax.ShapeDtypeStruct((depth, 8, n), F32),
        grid=(depth, n // bn),
        in_specs=[pl.BlockSpec((8, D), lambda l, j: (0, 0)),
                  pl.BlockSpec((1, D, bn), lambda l, j: (l, 0, j)),
                  pl.BlockSpec((1, 1, bn), lambda l, j: (l, 0, j))],
        out_specs=pl.BlockSpec((1, 8, bn), lambda l, j: (l, 0, j)),
        compiler_params=_params(("arbitrary", "arbitrary"), 40),
        name="adaln",
    )(cvec8, ada_w, ada_b.reshape(depth, 1, n))


def _token_specs(x):
    if isinstance(x, tuple):
        return ([pl.BlockSpec((TM, D), lambda i, *_: (jnp.minimum(i, PROMPT_TILES - 1), 0)),
                 pl.BlockSpec((TM, D), lambda i, *_: (jnp.maximum(i - PROMPT_TILES, 0), 0))], list(x))
    return [pl.BlockSpec((TM, D), lambda i, *_: (i, 0))], [x]


def _token_rows(x_refs):
    if len(x_refs) == 2:
        return jnp.where(pl.program_id(0) < PROMPT_TILES, x_refs[0][...], x_refs[1][...])
    return x_refs[0][...]


def _in_kernel(*refs, splits, n_x):
    x_refs, (g_ref, mod_ref, w_ref), outs = refs[:n_x], refs[n_x:n_x + 3], refs[n_x + 3:]
    x = _token_rows(x_refs)
    mod = mod_ref[0]
    y = x * lax.rsqrt(jnp.mean(x * x, axis=-1, keepdims=True) + 1e-6) * g_ref[...]
    h = (y * (1.0 + mod[:, D:2 * D]) + mod[:, 0:D]).astype(BF16)
    off = 0
    for o_ref, n in zip(outs, splits):
        o_ref[...] = jnp.dot(h, w_ref[:, off:off + n], preferred_element_type=F32)
        off += n


def _in_proj(x, g, mod, w_bf16, splits):
    n = w_bf16.shape[1]
    row = lambda i: (i, 0)
    x_specs, x_args = _token_specs(x)
    return pl.pallas_call(
        functools.partial(_in_kernel, splits=splits, n_x=len(x_args)),
        out_shape=[jax.ShapeDtypeStruct((T, s), F32) for s in splits],
        grid=(NT,),
        in_specs=x_specs + [_full((1, D)), pl.BlockSpec((1, 1, N_MOD * D), lambda i: (_group(i), 0, 0)),
                            _full((D, n))],
        out_specs=[pl.BlockSpec((TM, s), row) for s in splits],
        compiler_params=_params(("arbitrary",), 48),
        name="in_proj",
    )(*x_args, g.reshape(1, D), mod, w_bf16)


def _gelu(x):
    return 0.5 * x * (1.0 + lax.erf(x * (1.0 / math.sqrt(2.0))))


def _sgu_kernel(za_ref, lng_ref, ws_ref, bs_ref, o_ref):
    u = _gelu(za_ref[:, 0:HALF])
    va = _gelu(za_ref[:, HALF:2 * HALF])
    mu = jnp.mean(va, axis=-1, keepdims=True)
    dv = va - mu
    var = jnp.mean(dv * dv, axis=-1, keepdims=True)
    vn = dv * lax.rsqrt(var + 1e-5) * lng_ref[...]
    lane = lax.broadcasted_iota(I32, (LANES, LANES), 1)
    first = lane < HEAD
    for c in range(TM // LANES):
        rows = slice(c * LANES, (c + 1) * LANES)
        for p in range(HALF // LANES):
            cols = slice(p * LANES, (p + 1) * LANES)
            vp = vn[rows, cols]
            s = (jnp.dot(ws_ref[2 * p], jnp.where(first, vp, 0.0).astype(BF16), preferred_element_type=F32)
                 + jnp.dot(ws_ref[2 * p + 1], jnp.where(first, 0.0, vp).astype(BF16), preferred_element_type=F32))
            o_ref[rows, cols] = u[rows, cols] * (s + bs_ref[:, cols])


def _sgu(za, ln_g, w_s_bf16, bs_full):
    return pl.pallas_call(
        _sgu_kernel,
        out_shape=jax.ShapeDtypeStruct((T, HALF), F32),
        grid=(NT,),
        in_specs=[pl.BlockSpec((TM, 2 * HALF), lambda i: (i, 0)), _full((1, HALF)),
                  _full((8, LANES, LANES)), _full((LANES, HALF))],
        out_specs=pl.BlockSpec((TM, HALF), lambda i: (i, 0)),
        compiler_params=_params(("arbitrary",)),
        name="sgu",
    )(za, ln_g.reshape(1, HALF), w_s_bf16, bs_full)


def _rwkv_prep_kernel(zb_ref, zp_ref, zn_ref, mu_ref, kk_ref, ka_ref, rk_ref, w0_ref, a0_ref,
                      wup_ref, aup_ref, gup_ref, ones_ref,
                      r_ref, v_ref, kkn_ref, bonus_ref, g_ref, lw_ref, kt_ref, b_ref):
    i = pl.program_id(0)
    in_sample = i >= PROMPT_TILES
    pos = (i - PROMPT_TILES) % TILES_PER_SAMPLE
    is_first = jnp.logical_or(jnp.logical_not(in_sample), pos == 0)
    is_last = jnp.logical_or(jnp.logical_not(in_sample), pos == TILES_PER_SAMPLE - 1)
    zb = zb_ref[...]
    prev_row = jnp.where(is_first, 0.0, zp_ref[7:8, :])
    next_row = jnp.where(is_last, 0.0, zn_ref[0:1, :])
    rowid = lax.broadcasted_iota(I32, (TM, 1), 0)
    zp = jnp.where(rowid == 0, prev_row, pltpu.roll(zb, 1, 0))
    zn = jnp.where(rowid == TM - 1, next_row, pltpu.roll(zb, TM - 1, 0))
    zs = zb + mu_ref[0:1, :] * (zp - zb) + mu_ref[1:2, :] * (zn - zb)
    r = zs[:, 0:HALF]
    k = zs[:, HALF:2 * HALF]
    v = zs[:, 2 * HALF:3 * HALF]
    wa = zs[:, 3 * HALF:3 * HALF + LANES]
    gd = zs[:, 3 * HALF + LANES:B_COLS]
    ones_bd = ones_ref[...]
    r_ref[...] = r
    v_ref[...] = v
    g_ref[...] = jnp.dot(jax.nn.sigmoid(gd).astype(BF16), gup_ref[...], preferred_element_type=F32)
    kk = k * kk_ref[...]
    kkn = kk / jnp.maximum(jnp.sqrt(_group_sum(kk * kk, ones_bd)), 1e-6)
    kkn_ref[...] = kkn
    bonus_ref[...] = _group_sum(r * k * rk_ref[...], ones_bd) * v
    tw = jnp.tanh(wa).astype(BF16)
    wa16 = wa.astype(BF16)
    for dd in range(2):
        lw_ref[dd] = -W_DECAY_SCALE * jax.nn.sigmoid(
            w0_ref[dd:dd + 1, :] + jnp.dot(tw, wup_ref[dd], preferred_element_type=F32))
        a = jax.nn.sigmoid(a0_ref[dd:dd + 1, :] + jnp.dot(wa16, aup_ref[dd], preferred_element_type=F32))
        kt_ref[dd] = k * (1.0 + (a - 1.0) * ka_ref[...])
        b_ref[dd] = a * kkn


def _rwkv_prep(zb, mu, k_k, k_a, r_k, w0, a0, wup_pad, aup_pad, g_up, ones_bd):
    row = lambda i: (i, 0)
    halo = TM // 8
    one = jax.ShapeDtypeStruct((T, HALF), F32)
    two = jax.ShapeDtypeStruct((2, T, HALF), F32)
    o1 = pl.BlockSpec((TM, HALF), row)
    o2 = pl.BlockSpec((2, TM, HALF), lambda i: (0, i, 0))
    return pl.pallas_call(
        _rwkv_prep_kernel,
        out_shape=[one, one, one, one, one, two, two, two],
        grid=(NT,),
        in_specs=[pl.BlockSpec((TM, B_COLS), row),
                  pl.BlockSpec((8, B_COLS), lambda i: (jnp.maximum(i * halo - 1, 0), 0)),
                  pl.BlockSpec((8, B_COLS), lambda i: (jnp.minimum((i + 1) * halo, T // 8 - 1), 0)),
                  _full((2, B_COLS)), _full((1, HALF)), _full((1, HALF)), _full((1, HALF)),
                  _full((2, HALF)), _full((2, HALF)),
                  _full((2, LANES, HALF)), _full((2, LANES, HALF)), _full((LANES, HALF)),
                  _full((HALF, HALF))],
        out_specs=[o1, o1, o1, o1, o1, o2, o2, o2],
        compiler_params=_params(("arbitrary",), 48),
        name="rwkv_prep",
    )(zb, zb, zb, mu, k_k.reshape(1, HALF), k_a.reshape(1, HALF), r_k.reshape(1, HALF), w0, a0,
      wup_pad, aup_pad, g_up, ones_bd)


def _rwkv_chunks(dirs):
    C = RW_CHUNK
    ti = lax.broadcasted_iota(I32, (C, C), 0)
    tj = lax.broadcasted_iota(I32, (C, C), 1)
    bi = lax.broadcasted_iota(I32, (LANES, LANES), 0)
    bj = lax.broadcasted_iota(I32, (LANES, LANES), 1)
    same = (bi >> 6) == (bj >> 6)
    pi = bi & (C - 1)
    pj = bj & (C - 1)
    eye = (bi == bj).astype(F32)
    h0 = lax.broadcasted_iota(I32, (C, LANES), 1) < HEAD

    def stack(x):
        return jnp.concatenate([jnp.where(h0, x, 0.0), jnp.where(h0, 0.0, x)], axis=0)

    def fold(x):
        return x[0:C] + x[C:2 * C]

    chains = []
    for rev, r, v, kk, lw, kt, b, s_ref, y_ref in dirs:
        tri = jnp.where((tj >= ti) if rev else (tj <= ti), 1.0, 0.0).astype(F32)
        p1 = lw.astype(BF16)
        r1 = lw - p1.astype(F32)
        p2 = r1.astype(BF16)
        p3 = (r1 - p2.astype(F32)).astype(BF16)
        cs3 = jnp.dot(tri.astype(BF16), jnp.concatenate([p1, p2, p3], axis=1), preferred_element_type=F32)
        cs = cs3[:, 0:HALF] + (cs3[:, HALF:2 * HALF] + cs3[:, 2 * HALF:3 * HALF])
        ctot = cs[0:1, :] if rev else cs[C - 1:C, :]
        e_neg = jnp.exp(-cs)
        e_tail = jnp.exp(ctot - cs)
        q1 = kk * jnp.exp(cs - lw)
        k1 = kt * e_neg
        b1 = b * e_neg
        r1 = r * jnp.exp(cs)
        k2 = kt * e_tail
        b2 = b * e_tail
        e_tot = jnp.exp(ctot)
        strict = jnp.logical_and(same, (pj > pi) if rev else (pj < pi))
        incl = jnp.logical_and(same, (pj >= pi) if rev else (pj <= pi))
        for p in range(HALF // LANES):
            cols = slice(p * LANES, (p + 1) * LANES)
            chains.append(dict(p=p, cols=cols, strict=strict, incl=incl, s_ref=s_ref, y_ref=y_ref,
                               q1=q1[:, cols], k1=k1[:, cols], b1=b1[:, cols], r1=r1[:, cols],
                               k2=k2[:, cols], b2=b2[:, cols], v=v[:, cols], e_tot=e_tot[:, cols]))

    for ch in chains:
        lhs = jnp.concatenate([stack(ch["q1"]), stack(ch["r1"])], axis=0)
        rhs = jnp.concatenate([ch["k1"], ch["k1"], ch["b1"], ch["b1"]], axis=0)
        gm = _mm(lhs, rhs, NT_DIMS, 2)
        ch["mk"] = jnp.where(ch["strict"], gm[0:2 * C, 0:2 * C], 0.0)
        ch["mb"] = jnp.where(ch["strict"], gm[0:2 * C, 2 * C:4 * C], 0.0)
        ch["nk"] = jnp.where(ch["incl"], gm[2 * C:4 * C, 0:2 * C], 0.0)
        ch["nb"] = jnp.where(ch["incl"], gm[2 * C:4 * C, 2 * C:4 * C], 0.0)
        ch["tinv"] = eye - jnp.where((pi >> 1) == (pj >> 1), ch["mb"], 0.0)
    size = 2
    while size < C:
        sh = size.bit_length() - 1
        blk = jnp.logical_and((pi >> (sh + 1)) == (pj >> (sh + 1)), (pi >> sh) != (pj >> sh))
        for ch in chains:
            ch["tn"] = _mm(ch["tinv"], jnp.where(blk, ch["mb"], 0.0), NN, 1)
        for ch in chains:
            ch["tinv"] = ch["tinv"] - _mm(ch["tn"], ch["tinv"], NN, 2)
        size *= 2
    for ch in chains:
        vst = stack(ch["v"])
        ch["mkv"] = fold(_mm(ch["mk"], vst, NN, 2))
        ch["nkv"] = fold(_mm(ch["nk"], vst, NN, 1))
        ch["s"] = ch["s_ref"][ch["p"]]
        ch["qr"] = _mm(jnp.concatenate([ch["q1"], ch["r1"]], axis=0), ch["s"], NT_DIMS, 2)
    for ch in chains:
        ch["u"] = fold(_mm(ch["tinv"], stack(ch["mkv"] + ch["qr"][0:C]), NN, 2))
    for ch in chains:
        ch["y_ref"][:, ch["cols"]] = ch["qr"][C:2 * C] + ch["nkv"] - fold(_mm(ch["nb"], stack(ch["u"]), NN, 1))
        upd = _mm(jnp.concatenate([ch["v"], ch["u"]], axis=0),
                  jnp.concatenate([ch["k2"], -ch["b2"]], axis=0), TN, 3)
        ch["s_ref"][ch["p"]] = ch["s"] * ch["e_tot"] + jnp.where(same, upd, 0.0)


def _rwkv_scan_kernel(bf_ref, bb_ref, first_ref, last_ref, seq_ref,
                      rf_ref, vf_ref, kkf_ref, lwf_ref, ktf_ref, bfw_ref,
                      rb_ref, vb_ref, kkb_ref, lwb_ref, ktb_ref, bbw_ref, s0_ref,
                      yf_ref, yb_ref, sfin_ref, s_ref):
    step = pl.program_id(0)

    @pl.when(first_ref[step] == 1)
    def _():
        s_ref[...] = s0_ref[:, 0]

    C = RW_CHUNK
    for sub in range(RW_STEP // C):
        f = pl.ds(sub * C, C)
        b = pl.ds(RW_STEP - (sub + 1) * C, C)
        _rwkv_chunks([
            (False, rf_ref[f, :], vf_ref[f, :], kkf_ref[f, :], lwf_ref[0, f, :], ktf_ref[0, f, :], bfw_ref[0, f, :],
             s_ref.at[0], yf_ref.at[f]),
            (True, rb_ref[b, :], vb_ref[b, :], kkb_ref[b, :], lwb_ref[0, b, :], ktb_ref[0, b, :], bbw_ref[0, b, :],
             s_ref.at[1], yb_ref.at[b])])

    @pl.when(last_ref[step] == 1)
    def _():
        sfin_ref[:, 0] = s_ref[...]


def _rwkv_scan(r, v, kk, lw, kt, b, s0_bd):
    C = RW_STEP
    tabs = _seq_tables(C)
    fwd = lambda i, bf, bb, fi, la, sq: (bf[i], 0)
    bwd = lambda i, bf, bb, fi, la, sq: (bb[i], 0)
    fwd3 = lambda i, bf, bb, fi, la, sq: (0, bf[i], 0)
    bwd3 = lambda i, bf, bb, fi, la, sq: (1, bb[i], 0)
    st = pl.BlockSpec((2, 1, 4, LANES, LANES), lambda i, bf, bb, fi, la, sq: (0, sq[i], 0, 0, 0))
    one_f, one_b = pl.BlockSpec((C, HALF), fwd), pl.BlockSpec((C, HALF), bwd)
    two_f, two_b = pl.BlockSpec((1, C, HALF), fwd3), pl.BlockSpec((1, C, HALF), bwd3)
    return pl.pallas_call(
        _rwkv_scan_kernel,
        out_shape=[jax.ShapeDtypeStruct((T, HALF), F32), jax.ShapeDtypeStruct((T, HALF), F32),
                   jax.ShapeDtypeStruct((2, N_SEQ, 4, LANES, LANES), F32)],
        grid_spec=pltpu.PrefetchScalarGridSpec(
            num_scalar_prefetch=5, grid=(len(tabs[0]),),
            in_specs=[one_f, one_f, one_f, two_f, two_f, two_f,
                      one_b, one_b, one_b, two_b, two_b, two_b, st],
            out_specs=[one_f, one_b, st],
            scratch_shapes=[pltpu.VMEM((2, 4, LANES, LANES), F32)]),
        compiler_params=_params(("arbitrary",)),
        name="rwkv_scan",
    )(*tabs, r, v, kk, lw, kt, b, r, v, kk, lw, kt, b, s0_bd)


def _rope(x, cos, sin_signed, first16):
    w = x.shape[1]
    partner = jnp.where(first16, pltpu.roll(x, w - 16, 1), pltpu.roll(x, 16, 1))
    return x * cos + partner * sin_signed


def _odd_prep_kernel(zc_ref, zr_ref, cos_ref, sin_ref, qkg_ref, ones_ref, cq_ref, ck_ref, ckraw_ref, rqk_ref):
    ones_bd = ones_ref[...]
    cos = cos_ref[...]
    sin = sin_ref[...]
    lane = lax.broadcasted_iota(I32, (TM, HALF), 1)
    first16 = (lane & 31) < 16
    for idx, (o_ref, raw_ref) in enumerate(((cq_ref, None), (ck_ref, ckraw_ref))):
        x = zc_ref[:, idx * HALF:(idx + 1) * HALF]
        ms = _group_sum(x * x, ones_bd) * (1.0 / HEAD)
        xn = x * lax.rsqrt(ms + 1e-6) * qkg_ref[idx:idx + 1, :]
        if raw_ref is not None:
            raw_ref[...] = xn
        o_ref[...] = _rope(xn, cos, sin, first16)
    rqk = _rope(zr_ref[...], cos, sin, first16)
    rqk_ref[...] = jnp.where(lane < HALF // 2, rqk * (HEAD ** -0.5), rqk)


def _odd_prep(zc, zr, cos_tab, sin_tab, qkg_tiled, ones_bd):
    row = lambda i: (i, 0)
    tab = lambda i: (jnp.where(i < PROMPT_TILES, 0, 1 + (i - PROMPT_TILES) % TILES_PER_SAMPLE), 0)
    one = jax.ShapeDtypeStruct((T, HALF), F32)
    o1 = pl.BlockSpec((TM, HALF), row)
    return pl.pallas_call(
        _odd_prep_kernel,
        out_shape=[one, one, one, one], grid=(NT,),
        in_specs=[pl.BlockSpec((TM, 2 * HALF), row), pl.BlockSpec((TM, HALF), row),
                  pl.BlockSpec((TM, HALF), tab), pl.BlockSpec((TM, HALF), tab),
                  _full((2, HALF)), _full((HALF, HALF))],
        out_specs=[o1, o1, o1, o1],
        compiler_params=_params(("arbitrary",)),
        name="odd_prep",
    )(zc, zr, cos_tab, sin_tab, qkg_tiled, ones_bd)


def _attn_kernel(*refs, has_ctx, one_minus_li):
    if has_ctx:
        q_ref, k_ref, v_ref, kc_ref, vc_ref, lam_ref, sg_ref, _, o_ref = refs
    else:
        q_ref, k_ref, v_ref, lam_ref, sg_ref, _, o_ref = refs
    lam = lam_ref[...]
    lane = lax.broadcasted_iota(I32, (LANES, LANES), 1)
    m0 = lane < HEAD
    scale = HEAD ** -0.5
    for h in range(4):
        cols = slice(h * LANES, (h + 1) * LANES)
        qp = q_ref[:, cols]
        segs = [(k_ref[:, cols], v_ref[:, cols])]
        if has_ctx:
            segs.append((kc_ref[0, :, cols], vc_ref[0, :, cols]))
        probs, dens = [], []
        for qm in (jnp.where(m0, qp, 0.0), jnp.where(m0, 0.0, qp)):
            qm16 = qm.astype(BF16)
            ss = [lax.dot_general(qm16, ks.astype(BF16), NT_DIMS, preferred_element_type=F32) * scale
                  for ks, _ in segs]
            mx = ss[0].max(axis=-1, keepdims=True)
            for s_ in ss[1:]:
                mx = jnp.maximum(mx, s_.max(axis=-1, keepdims=True))
            ps = [jnp.exp(s_ - mx) for s_ in ss]
            den = ps[0].sum(axis=-1, keepdims=True)
            for p_ in ps[1:]:
                den = den + p_.sum(axis=-1, keepdims=True)
            probs.append(ps)
            dens.append(den)
        if has_ctx:
            acc = None
            for si, (_, vs) in enumerate(segs):
                amap = probs[0][si] / dens[0] - lam * (probs[1][si] / dens[1])
                t = jnp.dot(amap.astype(BF16), vs.astype(BF16), preferred_element_type=F32)
                acc = t if acc is None else acc + t
        else:
            v16 = segs[0][1].astype(BF16)
            pv = [jnp.dot(ps[0].astype(BF16), v16, preferred_element_type=F32) for ps in probs]
            acc = pv[0] / dens[0] - lam * (pv[1] / dens[1])
        nrm = acc * lax.rsqrt(jnp.mean(acc * acc, axis=-1, keepdims=True) + 1e-6) * sg_ref[...]
        o_ref[:, cols] = nrm * one_minus_li


def _attn(cq, ck, zc, row0, n_seq, seq_len, lam, subln_g, one_minus_li, prev, ctx_k=None, ctx_v=None):
    nq = seq_len // LANES
    qb0 = row0 // LANES
    sb0 = row0 // seq_len
    in_specs = [pl.BlockSpec((LANES, HALF), lambda s, q: (qb0 + s * nq + q, 0)),
                pl.BlockSpec((seq_len, HALF), lambda s, q: (sb0 + s, 0)),
                pl.BlockSpec((seq_len, HALF), lambda s, q: (sb0 + s, 2))]
    args = [cq, ck, zc]
    if ctx_k is not None:
        in_specs += [pl.BlockSpec((1, PAST, HALF), lambda s, q: (s, 0, 0))] * 2
        args += [ctx_k, ctx_v]
    in_specs += [_full((1, 1)), _full((1, LANES))]
    args += [lam.reshape(1, 1), subln_g.reshape(1, LANES)]
    in_specs.append(pl.BlockSpec(memory_space=pl.ANY))
    args.append(prev)
    aliases = {len(args) - 1: 0}
    return pl.pallas_call(
        functools.partial(_attn_kernel, has_ctx=ctx_k is not None, one_minus_li=one_minus_li),
        out_shape=jax.ShapeDtypeStruct((T, HALF), F32),
        grid=(n_seq, nq), in_specs=in_specs,
        out_specs=pl.BlockSpec((LANES, HALF), lambda s, q: (qb0 + s * nq + q, 0)),
        input_output_aliases=aliases,
        compiler_params=_params(("arbitrary", "arbitrary"), 48),
        name="diff_attn",
    )(*args)


_LOG_GAMMA = tuple(tuple(float(np.log1p(-np.exp2(-np.float32(e)), dtype=np.float32)) for e in es)
                   for es in RET_EXP)


def _ret_chunks(dirs):
    C = RET_CHUNK
    ii = lax.broadcasted_iota(I32, (C, C), 0)
    jj = lax.broadcasted_iota(I32, (C, C), 1)
    ri = lax.broadcasted_iota(I32, (C, 1), 0)
    lane = lax.broadcasted_iota(I32, (C, LANES), 1)
    chains = []
    for rev, qk_ref, v_ref, s_ref, o_ref in dirs:
        mask = (jj > ii) if rev else (jj <= ii)
        dist = jnp.where(mask, (jj - ii) if rev else (ii - jj), 0).astype(F32)
        kpow = (ri if rev else (C - 1 - ri)).astype(F32)
        qpow = ((C - ri) if rev else (ri + 1)).astype(F32)
        for h in range(4):
            lg = _LOG_GAMMA[1 if rev else 0][h]
            p = h // 2
            hm = (lane < HEAD) if h % 2 == 0 else (lane >= HEAD)
            qp = jnp.where(hm, qk_ref[:, p * LANES:(p + 1) * LANES], 0.0)
            kp = jnp.where(hm, qk_ref[:, HALF // 2 + p * LANES:HALF // 2 + (p + 1) * LANES], 0.0)
            chains.append(dict(
                h=h, lg=lg, s_ref=s_ref, o_ref=o_ref, q16=qp.astype(BF16), k16=kp.astype(BF16),
                qw16=(qp * jnp.exp(lg * qpow)).astype(BF16), kw16=(kp * jnp.exp(lg * kpow)).astype(BF16),
                v16=v_ref[:, h * LANES:(h + 1) * LANES].astype(BF16),
                decay=jnp.where(mask, jnp.exp(lg * dist), 0.0)))
    for ch in chains:
        ch["sc"] = (lax.dot_general(ch["q16"], ch["k16"], NT_DIMS, preferred_element_type=F32)
                    * ch["decay"]).astype(BF16)
        ch["s"] = ch["s_ref"][ch["h"]]
    for ch in chains:
        ch["o"] = (jnp.dot(ch["sc"], ch["v16"], preferred_element_type=F32)
                   + jnp.dot(ch["qw16"], ch["s"].astype(BF16), preferred_element_type=F32))
        ch["kv"] = lax.dot_general(ch["kw16"], ch["v16"], TN, preferred_element_type=F32)
    for ch in chains:
        h = ch["h"]
        ch["o_ref"][:, h * LANES:(h + 1) * LANES] = ch["o"]
        ch["s_ref"][h] = math.exp(ch["lg"] * C) * ch["s"] + ch["kv"]


def _ret_kernel(bf_ref, bb_ref, first_ref, last_ref, seq_ref,
                qkf_ref, vf_ref, qkb_ref, vb_ref, s0_ref, of_ref, ob_ref, sfin_ref, s_ref):
    step = pl.program_id(0)

    @pl.when(first_ref[step] == 1)
    def _():
        s_ref[...] = s0_ref[:, 0]

    C = RET_CHUNK
    for sub in range(RET_STEP // C):
        f = pl.ds(sub * C, C)
        b = pl.ds(RET_STEP - (sub + 1) * C, C)
        _ret_chunks([(False, qkf_ref.at[f], vf_ref.at[f], s_ref.at[0], of_ref.at[f]),
                     (True, qkb_ref.at[b], vb_ref.at[b], s_ref.at[1], ob_ref.at[b])])

    @pl.when(last_ref[step] == 1)
    def _():
        sfin_ref[:, 0] = s_ref[...]


def _retention(rqk, zr, s0):
    C = RET_STEP
    tabs = _seq_tables(C)
    st = pl.BlockSpec((2, 1, 4, LANES, LANES), lambda i, bf, bb, fi, la, sq: (0, sq[i], 0, 0, 0))
    spec = lambda use_b, col: pl.BlockSpec(
        (C, HALF), lambda i, bf, bb, fi, la, sq: ((bb if use_b else bf)[i], col))
    return pl.pallas_call(
        _ret_kernel,
        out_shape=[jax.ShapeDtypeStruct((T, HALF), F32), jax.ShapeDtypeStruct((T, HALF), F32),
                   jax.ShapeDtypeStruct((2, N_SEQ, 4, LANES, LANES), F32)],
        grid_spec=pltpu.PrefetchScalarGridSpec(
            num_scalar_prefetch=5, grid=(len(tabs[0]),),
            in_specs=[spec(False, 0), spec(False, 1), spec(True, 0), spec(True, 1), st],
            out_specs=[spec(False, 0), spec(True, 0), st],
            scratch_shapes=[pltpu.VMEM((2, 4, LANES, LANES), F32)]),
        compiler_params=_params(("arbitrary",)),
        name="retention",
    )(*tabs, rqk, zr, rqk, zr, s0)


def _out_kernel(*refs, even, n_x):
    x_refs, refs = refs[:n_x], refs[n_x:]
    if even:
        (a_ref, yf_ref, yb_ref, bonus_ref, g_ref, gng_ref, gnb_ref, ones_ref,
         mod_ref, ng_ref, wo_ref, rw_ref, rb_ref,
         y_ref, xp_ref, ti_ref, tg_ref, rk_ref, cnt_ref, run_ref) = refs
        ones_bd = ones_ref[...]
        ys = yf_ref[...] + yb_ref[...]
        mu = _group_sum(ys, ones_bd) * (1.0 / HEAD)
        dv = ys - mu
        var = _group_sum(dv * dv, ones_bd) * (1.0 / HEAD)
        yn = dv * lax.rsqrt(var + RWKV_GN_EPS) * gng_ref[...] + gnb_ref[...]
        left = a_ref[...]
        right = (yn + bonus_ref[...]) * g_ref[...]
    else:
        (c_ref, of_ref, ob_ref, rg_ref, gng_ref,
         mod_ref, ng_ref, wo_ref, rw_ref, rb_ref,
         y_ref, xp_ref, ti_ref, tg_ref, rk_ref, cnt_ref, run_ref) = refs
        left = c_ref[...]
        rg = rg_ref[...]
        gate = rg * jax.nn.sigmoid(rg)
        os_ = of_ref[...] + ob_ref[...]
        parts = []
        for h in range(4):
            oh = os_[:, h * LANES:(h + 1) * LANES]
            mu = jnp.mean(oh, axis=-1, keepdims=True)
            dv = oh - mu
            var = jnp.mean(dv * dv, axis=-1, keepdims=True)
            parts.append(dv * lax.rsqrt(var + 1e-5))
        right = gate * (jnp.concatenate(parts, axis=1) * gng_ref[...])
    mod = mod_ref[0]
    o = (jnp.dot(left.astype(BF16), wo_ref[0:HALF, :], preferred_element_type=F32)
         + jnp.dot(right.astype(BF16), wo_ref[HALF:2 * HALF, :], preferred_element_type=F32))
    y = _token_rows(x_refs) + mod[:, 2 * D:3 * D] * o
    y_ref[...] = y
    yn2 = y * lax.rsqrt(jnp.mean(y * y, axis=-1, keepdims=True) + 1e-6) * ng_ref[...]
    t = yn2 * (1.0 + mod[:, 4 * D:5 * D]) + mod[:, 3 * D:4 * D]
    xp_ref[...] = t
    logits = _mm(t, rw_ref[...], NN, 3) + rb_ref[...]
    lane = lax.broadcasted_iota(I32, (TM, LANES), 1)
    neg = jnp.float32(-jnp.inf)
    lg = jnp.where(lane < N_EXPERTS, logits, neg)
    vals, hits = [], []
    for _ in range(TOP_K):
        m = jnp.max(lg, axis=-1, keepdims=True)
        ix = jnp.min(jnp.where(lg == m, lane, LANES), axis=-1, keepdims=True)
        hit = lane == ix
        vals.append(m)
        hits.append((ix, hit))
        lg = jnp.where(hit, neg, lg)
    es = [jnp.exp(vv - vals[0]) for vv in vals]
    den = es[0] + es[1] + es[2] + es[3]

    @pl.when(pl.program_id(0) == 0)
    def _():
        run_ref[...] = jnp.zeros_like(run_ref)

    member = jnp.zeros((TM, LANES), F32)
    for _, hit in hits:
        member = member + jnp.where(hit, 1.0, 0.0)
    ri = lax.broadcasted_iota(I32, (TM, TM), 0)
    ci = lax.broadcasted_iota(I32, (TM, TM), 1)
    before = jnp.where(ci < ri, 1.0, 0.0).astype(BF16)
    seen = run_ref[...] + jnp.dot(before, member.astype(BF16), preferred_element_type=F32)
    ti = jnp.zeros((TM, LANES), I32)
    tg = jnp.zeros((TM, LANES), F32)
    rk = jnp.zeros((TM, LANES), F32)
    for kk, (ix, hit) in enumerate(hits):
        ti = jnp.where(lane == kk, ix, ti)
        tg = jnp.where(lane == kk, es[kk] / den, tg)
        rk = jnp.where(lane == kk, jnp.sum(jnp.where(hit, seen, 0.0), axis=-1, keepdims=True), rk)
    ti_ref[...] = ti
    tg_ref[...] = tg
    rk_ref[...] = rk.astype(I32)
    run_ref[...] = run_ref[...] + jnp.sum(member, axis=0, keepdims=True)
    cnt_ref[...] = run_ref[...]


def _out_proj(even, mix_args, mix_specs, x, mod, norm_g, w_out_bf16, rw_pad, rb_pad):
    row = lambda i: (i, 0)
    modspec = pl.BlockSpec((1, 1, N_MOD * D), lambda i: (_group(i), 0, 0))
    x_specs, x_args = _token_specs(x)
    in_specs = x_specs + list(mix_specs) + [modspec, _full((1, D)), _full((D, D)), _full((D, LANES)),
                                            _full((1, LANES))]
    args = x_args + list(mix_args) + [mod, norm_g.reshape(1, D), w_out_bf16, rw_pad, rb_pad]
    lane_i = jax.ShapeDtypeStruct((T, LANES), I32)
    lane_spec = pl.BlockSpec((TM, LANES), row)
    return pl.pallas_call(
        functools.partial(_out_kernel, even=even, n_x=len(x_args)),
        out_shape=[jax.ShapeDtypeStruct((T, D), F32), jax.ShapeDtypeStruct((T, D), F32),
                   lane_i, jax.ShapeDtypeStruct((T, LANES), F32), lane_i,
                   jax.ShapeDtypeStruct((1, LANES), F32)],
        grid=(NT,), in_specs=in_specs,
        out_specs=[pl.BlockSpec((TM, D), row), pl.BlockSpec((TM, D), row),
                   lane_spec, lane_spec, lane_spec, _full((1, LANES))],
        scratch_shapes=[pltpu.VMEM((1, LANES), F32)],
        compiler_params=_params(("arbitrary",), 48),
        name="out_proj",
    )(*args)


def _route_kernel(cnt_ref, ti_ref, rk_ref, dest_ref, te_ref, nt_ref):
    cnt = cnt_ref[...].astype(I32)
    ntile = lax.shift_right_logical(cnt + (TMX - 1), TMX.bit_length() - 1)
    ei = lax.broadcasted_iota(I32, (LANES, LANES), 0)
    ej = lax.broadcasted_iota(I32, (LANES, LANES), 1)
    upto = jnp.where(ei <= ej, 1.0, 0.0).astype(BF16)
    ntile_f = jnp.broadcast_to(ntile.astype(F32), (8, LANES))
    tile_end = jnp.dot(ntile_f.astype(BF16), upto, preferred_element_type=F32)[0:1, :]
    row_start = (tile_end - ntile.astype(F32)) * float(TMX)
    lane = lax.broadcasted_iota(I32, (TM, LANES), 1)
    ti = ti_ref[...]
    rk = rk_ref[...]
    spread = jnp.zeros((TM, LANES), F32)
    for k in range(TOP_K):
        hit = lane == ti[:, k:k + 1]
        start = jnp.sum(jnp.where(hit, row_start, 0.0), axis=-1, keepdims=True)
        spread = jnp.where((lane & (TOP_K - 1)) == k, start + rk[:, k:k + 1].astype(F32), spread)
    tok = lax.broadcasted_iota(I32, (TM, LANES), 0)
    keep = (tok & (LANES // TOP_K - 1)) == lax.shift_right_logical(lane, 2)
    flat = jnp.where(keep, spread, 0.0).reshape(TM * TOP_K // LANES, LANES // TOP_K, LANES).sum(axis=1)
    dest_ref[...] = flat.astype(I32)

    @pl.when(pl.program_id(0) == 0)
    def _():
        lane1 = lax.broadcasted_iota(I32, (1, LANES), 1)
        n_tiles = jnp.max(tile_end, axis=-1, keepdims=True)
        last_e = jnp.max(jnp.where(cnt > 0, lane1, 0), axis=-1, keepdims=True)
        tile = lax.broadcasted_iota(I32, (TM, 1), 0).astype(F32)
        te = jnp.sum(jnp.where(tile_end <= tile, 1, 0), axis=-1, keepdims=True)
        te = jnp.where(tile < n_tiles, te, last_e)
        te_ref[...] = jnp.broadcast_to(te, (TM, LANES)).astype(I32)
        first_row = lax.broadcasted_iota(I32, (8, LANES), 0) == 0
        nt_ref[...] = jnp.where(first_row, n_tiles, tile_end).astype(I32)


def _route(cnt, ti, rk):
    row = lambda i: (i, 0)
    return pl.pallas_call(
        _route_kernel,
        out_shape=[jax.ShapeDtypeStruct((N_ASSIGN // LANES, LANES), I32), jax.ShapeDtypeStruct((TM, LANES), I32),
                   jax.ShapeDtypeStruct((8, LANES), I32)],
        grid=(NT,),
        in_specs=[_full((1, LANES)), pl.BlockSpec((TM, LANES), row), pl.BlockSpec((TM, LANES), row)],
        out_specs=[pl.BlockSpec((TM * TOP_K // LANES, LANES), row), _full((TM, LANES)), _full((8, LANES))],
        compiler_params=_params(("arbitrary",)),
        name="moe_route",
    )(cnt, ti, rk)


def _start_row_copy(src_ref, src_row, dst_ref, dst_row, sem, queue):
    pltpu.async_copy(src_ref.at[pl.ds(src_row, 1)], dst_ref.at[pl.ds(dst_row, 1)], sem, priority=queue)


def _wait_tiles(n, src_ref, dst_ref, sem):
    for _ in range(n):
        pltpu.make_async_copy(src_ref, dst_ref, sem).wait()


def _dispatch_kernel(dest_ref, tend_ref, x_ref, xs_ref, zero_ref, sem):
    i = pl.program_id(0)

    @pl.when(i == 0)
    def _():
        zero_ref[...] = jnp.zeros_like(zero_ref)

        def last_tile(e, fn):
            end = tend_ref[e]
            begin = tend_ref[e - 1] if e > 0 else 0

            @pl.when(end > begin)
            def _():
                fn(pltpu.make_async_copy(zero_ref, xs_ref.at[pl.ds((end - 1) * TMX, TMX)], sem))

        def unused_tile(j):
            return pltpu.make_async_copy(zero_ref, xs_ref.at[pl.ds(j * TMX, TMX)], sem)

        def start_unused(j, carry):
            unused_tile(j).start()
            return carry

        def wait_unused(j, carry):
            unused_tile(j).wait()
            return carry

        n_used = tend_ref[N_EXPERTS - 1]
        for e in range(N_EXPERTS):
            last_tile(e, lambda c: c.start())
        lax.fori_loop(n_used, MOE_TILES, start_unused, 0)
        for e in range(N_EXPERTS):
            last_tile(e, lambda c: c.wait())
        lax.fori_loop(n_used, MOE_TILES, wait_unused, 0)

    base = i * (TM * TOP_K)

    def start(r, carry):
        for k in range(TOP_K):
            _start_row_copy(x_ref, r, xs_ref, dest_ref[base + r * TOP_K + k], sem, k % 2)
        return carry

    lax.fori_loop(0, TM, start, 0, unroll=4)
    _wait_tiles(TOP_K, x_ref, xs_ref.at[pl.ds(0, TM)], sem)


def _dispatch(dest_flat, tile_end, xt):
    return pl.pallas_call(
        _dispatch_kernel,
        out_shape=jax.ShapeDtypeStruct((R_PAD, D), F32),
        grid_spec=pltpu.PrefetchScalarGridSpec(
            num_scalar_prefetch=2, grid=(NT,),
            in_specs=[pl.BlockSpec((TM, D), lambda i, d, te: (i, 0))],
            out_specs=pl.BlockSpec(memory_space=pl.ANY),
            scratch_shapes=[pltpu.VMEM((TMX, D), F32), pltpu.SemaphoreType.DMA(())]),
        compiler_params=_params(("arbitrary",)),
        name="moe_dispatch",
    )(dest_flat, tile_end, xt)


W_PARTS = 8


def _expert_weights(i, nt, te_ref, w_ref, wbuf_ref, wsem, w16_ref, group_ref):
    rows = w_ref.shape[1] // W_PARTS

    def fetch(e, buf):
        return [pltpu.make_async_copy(w_ref.at[e, pl.ds(p * rows, rows)], wbuf_ref.at[buf, pl.ds(p * rows, rows)],
                                      wsem.at[buf]) for p in range(W_PARTS)]

    @pl.when(i == 0)
    def _():
        group_ref[0] = 0
        for c in fetch(te_ref[0], 0):
            c.start()

    first = jnp.logical_or(i == 0, te_ref[i] != te_ref[jnp.maximum(i - 1, 0)])

    @pl.when(jnp.logical_and(first, i < nt))
    def _():
        cur = group_ref[0] % 2
        nxt = lax.while_loop(
            lambda j: jnp.logical_and(j < nt, te_ref[jnp.minimum(j, MOE_TILES - 1)] == te_ref[i]),
            lambda j: j + 1, i + 1)

        @pl.when(nxt < nt)
        def _():
            for c in fetch(te_ref[jnp.minimum(nxt, MOE_TILES - 1)], 1 - cur):
                c.start()

        for c in fetch(0, cur):
            c.wait()
        w16_ref[...] = wbuf_ref[cur].astype(BF16)
        group_ref[0] = group_ref[0] + 1


def _experts_kernel(te_ref, nt_ref, xs_ref, wgu_ref, bgu_ref, wdn_ref, bdn_ref, y_ref,
                    gu_buf, gu_sem, gu16_ref, gu_group, dn_buf, dn_sem, dn16_ref, dn_group):
    i = pl.program_id(0)
    nt = nt_ref[0]
    _expert_weights(i, nt, te_ref, wgu_ref, gu_buf, gu_sem, gu16_ref, gu_group)
    _expert_weights(i, nt, te_ref, wdn_ref, dn_buf, dn_sem, dn16_ref, dn_group)

    @pl.when(i < nt)
    def _():
        x16 = xs_ref[...].astype(BF16)
        y = bdn_ref[0]
        for h in range(2):
            gc = slice(h * HALF, (h + 1) * HALF)
            uc = slice(D + h * HALF, D + (h + 1) * HALF)
            g = jnp.dot(x16, gu16_ref[:, gc], preferred_element_type=F32) + bgu_ref[0, :, gc]
            u = jnp.dot(x16, gu16_ref[:, uc], preferred_element_type=F32) + bgu_ref[0, :, uc]
            gt = jnp.minimum(g, SWIGLU_LIMIT)
            up = jnp.clip(u, -SWIGLU_LIMIT, SWIGLU_LIMIT)
            act = ((up + 1.0) * gt * jax.nn.sigmoid(SWIGLU_ALPHA * gt)).astype(BF16)
            y = y + jnp.dot(act, dn16_ref[gc, :], preferred_element_type=F32)
        y_ref[...] = y

    @pl.when(i >= nt)
    def _():
        y_ref[...] = jnp.zeros_like(y_ref)


def _tile_clamped(i, te, nt):
    return (jnp.minimum(i, jnp.maximum(nt[0] - 1, 0)), 0)


def _weight_scratch(n_out):
    return [pltpu.VMEM((2, D, n_out), F32), pltpu.SemaphoreType.DMA((2,)), pltpu.VMEM((D, n_out), BF16),
            pltpu.SMEM((1,), I32)]


def _experts(te, n_tiles, xs, w_gu, b_gu, w_dn, b_dn):
    return pl.pallas_call(
        _experts_kernel,
        out_shape=jax.ShapeDtypeStruct((R_PAD, D), F32),
        grid_spec=pltpu.PrefetchScalarGridSpec(
            num_scalar_prefetch=2, grid=(MOE_TILES,),
            in_specs=[pl.BlockSpec((TMX, D), _tile_clamped), pl.BlockSpec(memory_space=pl.ANY),
                      pl.BlockSpec((1, 1, 2 * D), lambda i, te, nt: (te[i], 0, 0)),
                      pl.BlockSpec(memory_space=pl.ANY),
                      pl.BlockSpec((1, 1, D), lambda i, te, nt: (te[i], 0, 0))],
            out_specs=pl.BlockSpec((TMX, D), lambda i, te, nt: (i, 0)),
            scratch_shapes=_weight_scratch(2 * D) + _weight_scratch(D)),
        compiler_params=_params(("arbitrary",), 58),
        name="moe_experts",
    )(te, n_tiles, xs, w_gu, b_gu, w_dn, b_dn)


def _combine_kernel(dest_ref, x_ref, tg_ref, mod_ref, ys_ref, *rest):
    *o_refs, buf_ref, sem = rest
    i = pl.program_id(0)
    slot = i % 2

    def gather(tile, b):
        base = tile * (TM * TOP_K)

        def body(r, carry):
            for k in range(TOP_K):
                _start_row_copy(ys_ref, dest_ref[base + r * TOP_K + k], buf_ref.at[b, k], r, sem.at[b], k % 2)
            return carry

        lax.fori_loop(0, TM, body, 0, unroll=4)

    @pl.when(i == 0)
    def _():
        gather(0, 0)

    @pl.when(i + 1 < NT)
    def _():
        gather(i + 1, 1 - slot)

    _wait_tiles(TOP_K, ys_ref.at[pl.ds(0, TM)], buf_ref.at[slot, 0], sem.at[slot])
    tg = tg_ref[...]
    f = tg[:, 0:1] * buf_ref[slot, 0]
    for k in range(1, TOP_K):
        f = f + tg[:, k:k + 1] * buf_ref[slot, k]
    res = x_ref[...] + mod_ref[0][:, 5 * D:6 * D] * f
    if len(o_refs) == 1:
        o_refs[0][...] = res
    else:
        @pl.when(i < PROMPT_TILES)
        def _():
            o_refs[0][...] = res

        @pl.when(i >= PROMPT_TILES)
        def _():
            o_refs[1][...] = res


def _combine(dest_flat, x, tg, mod, ys, split=False):
    row = lambda i, d: (i, 0)
    if split:
        out_shape = [jax.ShapeDtypeStruct((T_PROMPT, D), F32), jax.ShapeDtypeStruct((T_SAMPLE, D), F32)]
        out_specs = [pl.BlockSpec((TM, D), lambda i, d: (jnp.minimum(i, PROMPT_TILES - 1), 0)),
                     pl.BlockSpec((TM, D), lambda i, d: (jnp.maximum(i - PROMPT_TILES, 0), 0))]
    else:
        out_shape = jax.ShapeDtypeStruct((T, D), F32)
        out_specs = pl.BlockSpec((TM, D), row)
    return pl.pallas_call(
        _combine_kernel,
        out_shape=out_shape,
        grid_spec=pltpu.PrefetchScalarGridSpec(
            num_scalar_prefetch=1, grid=(NT,),
            in_specs=[pl.BlockSpec((TM, D), row), pl.BlockSpec((TM, LANES), row),
                      pl.BlockSpec((1, 1, N_MOD * D), lambda i, d: (_group(i), 0, 0)),
                      pl.BlockSpec(memory_space=pl.ANY)],
            out_specs=out_specs,
            scratch_shapes=[pltpu.VMEM((2, TOP_K, TM, D), F32), pltpu.SemaphoreType.DMA((2,))]),
        compiler_params=_params(("arbitrary",), 40),
        name="moe_combine",
    )(dest_flat, x, tg, mod, ys)


def _moe(layer, y, xt, ti, tg, rk, cnt, mod, w_gu, b_gu, w_dn, b_dn):
    dest, te, nt = _route(cnt, ti, rk)
    dest_flat = dest.reshape(-1)
    te = te[:MOE_TILES, 0] + layer * N_EXPERTS
    n_tiles = nt[0, :1]
    xs = _dispatch(dest_flat, nt[1, :N_EXPERTS], xt)
    n_all = w_gu.shape[0] * N_EXPERTS
    ys = _experts(te, n_tiles, xs, w_gu.reshape(n_all, D, 2 * D), b_gu.reshape(n_all, 1, 2 * D),
                  w_dn.reshape(n_all, D, D), b_dn.reshape(n_all, 1, D))
    return _combine(dest_flat, y, tg, mod, ys, split=layer == 1)


def _ones_blockdiag():
    idx = np.arange(HALF) // HEAD
    return jnp.asarray((idx[:, None] == idx[None, :]).astype(np.float32), dtype=BF16)


def _rope_tables():
    pos = jnp.arange(L_SAMPLE)
    rowp = (pos // 64).astype(F32)
    colp = (pos % 64).astype(F32)
    nf = HEAD // 4
    inv = jnp.power(10000.0, -jnp.arange(nf, dtype=F32) / nf)
    ar = rowp[:, None] * inv[None, :]
    ac = colp[:, None] * inv[None, :]
    cos64 = jnp.concatenate([jnp.cos(ar), jnp.cos(ar), jnp.cos(ac), jnp.cos(ac)], axis=1)
    sin64 = jnp.concatenate([-jnp.sin(ar), jnp.sin(ar), -jnp.sin(ac), jnp.sin(ac)], axis=1)
    cos = jnp.tile(cos64, (1, HALF // HEAD))
    sin = jnp.tile(sin64, (1, HALF // HEAD))
    ident = jnp.ones((TM, HALF), F32)
    return (jnp.concatenate([ident, cos], axis=0), jnp.concatenate([jnp.zeros((TM, HALF), F32), sin], axis=0))


def _bd_pairs(s):
    lead = s.shape[:-3]
    s = s.reshape(lead + (4, 2, HEAD, HEAD))
    z = jnp.zeros_like(s[..., 0, :, :])
    top = jnp.concatenate([s[..., 0, :, :], z], axis=-1)
    bot = jnp.concatenate([z, s[..., 1, :, :]], axis=-1)
    return jnp.concatenate([top, bot], axis=-2)


def _bd_unpairs(s):
    a = s[..., 0:HEAD, 0:HEAD]
    b = s[..., HEAD:, HEAD:]
    out = jnp.stack([a, b], axis=-3)
    return out.reshape(s.shape[:-3] + (8, HEAD, HEAD))


def kernel(x_prompt, x_sample, state_rwkv, cache_k_diff, cache_v_diff, state_retention, c, c_ctx, norm_g, ada_w, ada_b, e_w_in, e_w_out, sgu_ln_g, sgu_w, sgu_b, rw_mu, rw_w0, rw_w_up, rw_a0, rw_a_up, rw_g_up, rw_k_k, rw_k_a, rw_r_k, rw_gn_g, rw_gn_b, o_w_in, o_w_out, da_qk_g, da_lam, da_subln_g, ret_gn_g, router_w, router_b, ex_w_gu, ex_b_gu, ex_w_dn, ex_b_dn):
    x = (x_prompt.reshape(T_PROMPT, D), x_sample.reshape(T_SAMPLE, D))
    cvec8 = jnp.concatenate([c_ctx[None, :], c, jnp.zeros((3, D), F32)], axis=0)
    mods = _adaln(cvec8, ada_w, ada_b)
    mod0 = mods[0].reshape(8, 1, N_MOD * D)
    mod1 = mods[1].reshape(8, 1, N_MOD * D)
    ones_bd = _ones_blockdiag()
    rw_pad = jnp.pad(router_w, ((0, 0), (0, 0), (0, LANES - N_EXPERTS)))
    rb_pad = jnp.pad(router_b, ((0, 0), (0, LANES - N_EXPERTS))).reshape(2, 1, LANES)
    row = lambda i: (i, 0)
    half = pl.BlockSpec((TM, HALF), row)

    za, zb = _in_proj(x, norm_g[0, 0], mod0, e_w_in[0].astype(BF16), (2 * HALF, B_COLS))
    bs_full = jnp.repeat(sgu_b[0].T, HEAD, axis=1)
    a_out = _sgu(za, sgu_ln_g[0], sgu_w[0].astype(BF16), bs_full)
    zpad = jnp.zeros((2, HEAD, HALF), F32)
    wup_pad = jnp.concatenate([rw_w_up[0], zpad], axis=1).astype(BF16)
    aup_pad = jnp.concatenate([zpad, rw_a_up[0]], axis=1).astype(BF16)
    r, v, kkn, bonus, g, lw, kt, b = _rwkv_prep(zb, rw_mu[0], rw_k_k[0], rw_k_a[0], rw_r_k[0], rw_w0[0], rw_a0[0],
                                                wup_pad, aup_pad, rw_g_up[0].astype(BF16), ones_bd)
    s0_sample = _bd_pairs(jnp.moveaxis(state_rwkv[:, 0], 1, 0))
    s0_rw = jnp.concatenate([jnp.zeros((2, N_PROMPT, 4, LANES, LANES), F32), s0_sample], axis=1)
    yf_rw, yb_rw, sfin_rw = _rwkv_scan(r, v, kkn, lw, kt, b, s0_rw)
    new_rwkv = jnp.moveaxis(_bd_unpairs(sfin_rw[:, :N_PROMPT]), 0, 1)[:, None]
    y0, xp0, ti0, tg0, rk0, cnt0 = _out_proj(
        True,
        [a_out, yf_rw, yb_rw, bonus, g, rw_gn_g[0].reshape(1, HALF), rw_gn_b[0].reshape(1, HALF), ones_bd],
        [half, half, half, half, half, _full((1, HALF)), _full((1, HALF)), _full((HALF, HALF))],
        x, mod0, norm_g[0, 1], e_w_out[0].astype(BF16), rw_pad[0], rb_pad[0])
    x1 = _moe(0, y0, xp0, ti0, tg0, rk0, cnt0, mod0, ex_w_gu, ex_b_gu, ex_w_dn, ex_b_dn)

    zc, zr = _in_proj(x1, norm_g[1, 0], mod1, o_w_in[0].astype(BF16), (3 * HALF, 3 * HALF))
    cos_tab, sin_tab = _rope_tables()
    qkg = jnp.tile(da_qk_g[0], (1, HALF // HEAD))
    cq, ck, ck_raw, rqk = _odd_prep(zc, zr, cos_tab, sin_tab, qkg, ones_bd)
    lambda_init = 0.8 - 0.6 * math.exp(-0.3 * 1)
    lv = da_lam[0]
    lam = jnp.exp(jnp.sum(lv[0] * lv[1])) - jnp.exp(jnp.sum(lv[2] * lv[3])) + lambda_init
    c_out = _attn(cq, ck, zc, 0, N_PROMPT, L_PROMPT, lam, da_subln_g[0], 1.0 - lambda_init,
                  jnp.zeros((T, HALF), F32))
    ctx_k = cache_k_diff[:, 0].reshape(N_SAMPLE, PAST, HALF)
    ctx_v = cache_v_diff[:, 0].reshape(N_SAMPLE, PAST, HALF)
    c_out = _attn(cq, ck, zc, T_PROMPT, N_SAMPLE, L_SAMPLE, lam, da_subln_g[0], 1.0 - lambda_init,
                  c_out, ctx_k, ctx_v)
    sr = jnp.moveaxis(state_retention[:, 0], 1, 0)
    zr0 = jnp.zeros_like(sr)
    s0_sample = jnp.stack([jnp.concatenate([sr[:, :, 0], zr0[:, :, 0]], axis=-2),
                           jnp.concatenate([zr0[:, :, 1], sr[:, :, 1]], axis=-2),
                           jnp.concatenate([sr[:, :, 2], zr0[:, :, 2]], axis=-2),
                           jnp.concatenate([zr0[:, :, 3], sr[:, :, 3]], axis=-2)], axis=2)
    s0_ret = jnp.concatenate([jnp.zeros((2, N_PROMPT, 4, LANES, LANES), F32), s0_sample], axis=1)
    of_ret, ob_ret, rfin = _retention(rqk, zr, s0_ret)
    rfin_p = rfin[:, :N_PROMPT]
    new_ret = jnp.stack([rfin_p[:, :, 0, 0:HEAD], rfin_p[:, :, 1, HEAD:], rfin_p[:, :, 2, 0:HEAD],
                         rfin_p[:, :, 3, HEAD:]], axis=2)
    new_ret = jnp.moveaxis(new_ret, 0, 1)[:, None]
    y1, xp1, ti1, tg1, rk1, cnt1 = _out_proj(
        False,
        [c_out, of_ret, ob_ret, zr, ret_gn_g[0].reshape(1, HALF)],
        [half, half, half, pl.BlockSpec((TM, HALF), lambda i: (i, 2)), _full((1, HALF))],
        x1, mod1, norm_g[1, 1], o_w_out[0].astype(BF16), rw_pad[1], rb_pad[1])
    y_prompt, y_sample = _moe(1, y1, xp1, ti1, tg1, rk1, cnt1, mod1, ex_w_gu, ex_b_gu, ex_w_dn, ex_b_dn)

    new_k = ck_raw[:T_PROMPT].reshape(N_PROMPT, 1, L_PROMPT, 4, LANES)
    new_v = zc[:T_PROMPT, 2 * HALF:3 * HALF].reshape(N_PROMPT, 1, L_PROMPT, 4, LANES)
    return (y_prompt.reshape(N_PROMPT, L_PROMPT, D), y_sample.reshape(N_SAMPLE, L_SAMPLE, D),
            new_rwkv, new_k, new_v, new_ret)
```

```python
import functools
import math

import numpy as np
import jax
import jax.numpy as jnp
from jax import lax
from jax.experimental import pallas as pl
from jax.experimental.pallas import tpu as pltpu

F32 = jnp.float32
BF16 = jnp.bfloat16
I32 = jnp.int32

D = 1024
N_PROMPT, L_PROMPT = 16, 256
N_SAMPLE, L_SAMPLE = 4, 1024
N_SEQ = N_PROMPT + N_SAMPLE
PAST = 256
T_PROMPT = N_PROMPT * L_PROMPT
T_SAMPLE = N_SAMPLE * L_SAMPLE
T = T_PROMPT + T_SAMPLE
TM = 256
NT = T // TM
PROMPT_TILES = T_PROMPT // TM
TILES_PER_SAMPLE = L_SAMPLE // TM
N_MOD = 6
HALF = 512
B_COLS = 1792
HEAD = 64
W_DECAY_SCALE = math.exp(-0.5)
RWKV_GN_EPS = 64e-5
RW_CHUNK = 64
RW_STEP = 256
RET_CHUNK = 128
RET_STEP = 256
RET_EXP = ((5.0, 7.0, 9.0, 11.0), (6.0, 8.0, 10.0, 12.0))
N_EXPERTS = 32
TOP_K = 4
SWIGLU_LIMIT = 7.0
SWIGLU_ALPHA = 1.702
N_ASSIGN = T * TOP_K
TMX = 512
MOE_TILES = N_ASSIGN // TMX + N_EXPERTS
R_PAD = MOE_TILES * TMX
LANES = 128

NN = (((1,), (0,)), ((), ()))
NT_DIMS = (((1,), (1,)), ((), ()))
TN = (((0,), (0,)), ((), ()))


def _group(i):
    return jnp.where(i < PROMPT_TILES, 0, 1 + (i - PROMPT_TILES) // TILES_PER_SAMPLE)


def _mm(a, b, dims=NN, passes=1):
    dg = functools.partial(lax.dot_general, dimension_numbers=dims, preferred_element_type=F32)
    if passes == 1:
        return dg(a.astype(BF16), b.astype(BF16))
    a = a.astype(F32)
    b = b.astype(F32)
    ah = a.astype(BF16)
    al = (a - ah.astype(F32)).astype(BF16)
    bh = b.astype(BF16)
    free = 1 - dims[0][0][0]
    m = a.shape[free]
    both = dg(jnp.concatenate([ah, al], axis=free), bh)
    if passes == 2:
        return both[0:m] + both[m:2 * m]
    bl = (b - bh.astype(F32)).astype(BF16)
    return both[0:m] + (dg(ah, bl) + both[m:2 * m])


def _group_sum(x, ones_bd):
    xh = x.astype(BF16)
    xl = (x - xh.astype(F32)).astype(BF16)
    return (jnp.dot(xh, ones_bd, preferred_element_type=F32)
            + jnp.dot(xl, ones_bd, preferred_element_type=F32))


def _full(shape):
    nd = len(shape)
    return pl.BlockSpec(shape, lambda *_: (0,) * nd)


def _params(sem, vmem_mb=None):
    kw = {}
    if vmem_mb is not None:
        kw["vmem_limit_bytes"] = vmem_mb * 1024 * 1024
    return pltpu.CompilerParams(dimension_semantics=sem, **kw)


def _seq_tables(chunk):
    blk_f, blk_b, first, last, seq = [], [], [], [], []
    row = 0
    for s in range(N_SEQ):
        n = (L_PROMPT if s < N_PROMPT else L_SAMPLE) // chunk
        base = row // chunk
        for j in range(n):
            blk_f.append(base + j)
            blk_b.append(base + n - 1 - j)
            first.append(int(j == 0))
            last.append(int(j == n - 1))
            seq.append(s)
        row += n * chunk
    return tuple(np.asarray(a, np.int32) for a in (blk_f, blk_b, first, last, seq))


def _adaln_kernel(c_ref, w_ref, b_ref, o_ref):
    c = c_ref[...]
    s = c * jax.nn.sigmoid(c)
    o_ref[0] = _mm(s, w_ref[0], NN, 3) + b_ref[0]


def _adaln(cvec8, ada_w, ada_b):
    depth, _, n = ada_w.shape
    bn = 1536
    return pl.pallas_call(
        _adaln_kernel,
        out_shape=jax.ShapeDtypeStruct((depth, 8, n), F32),
        grid=(depth, n // bn),
        in_specs=[pl.BlockSpec((8, D), lambda l, j: (0, 0)),
                  pl.BlockSpec((1, D, bn), lambda l, j: (l, 0, j)),
                  pl.BlockSpec((1, 1, bn), lambda l, j: (l, 0, j))],
        out_specs=pl.BlockSpec((1, 8, bn), lambda l, j: (l, 0, j)),
        compiler_params=_params(("arbitrary", "arbitrary"), 40),
        name="adaln",
    )(cvec8, ada_w, ada_b.reshape(depth, 1, n))


def _token_specs(x):
    if isinstance(x, tuple):
        return ([pl.BlockSpec((TM, D), lambda i, *_: (jnp.minimum(i, PROMPT_TILES - 1), 0)),
                 pl.BlockSpec((TM, D), lambda i, *_: (jnp.maximum(i - PROMPT_TILES, 0), 0))], list(x))
    return [pl.BlockSpec((TM, D), lambda i, *_: (i, 0))], [x]


def _token_rows(x_refs):
    if len(x_refs) == 2:
        return jnp.where(pl.program_id(0) < PROMPT_TILES, x_refs[0][...], x_refs[1][...])
    return x_refs[0][...]


def _in_kernel(*refs, splits, n_x):
    x_refs, (g_ref, mod_ref, w_ref), outs = refs[:n_x], refs[n_x:n_x + 3], refs[n_x + 3:]
    x = _token_rows(x_refs)
    mod = mod_ref[0]
    y = x * lax.rsqrt(jnp.mean(x * x, axis=-1, keepdims=True) + 1e-6) * g_ref[...]
    h = (y * (1.0 + mod[:, D:2 * D]) + mod[:, 0:D]).astype(BF16)
    off = 0
    for o_ref, n in zip(outs, splits):
        o_ref[...] = jnp.dot(h, w_ref[:, off:off + n], preferred_element_type=F32)
        off += n


def _in_proj(x, g, mod, w_bf16, splits):
    n = w_bf16.shape[1]
    row = lambda i: (i, 0)
    x_specs, x_args = _token_specs(x)
    return pl.pallas_call(
        functools.partial(_in_kernel, splits=splits, n_x=len(x_args)),
        out_shape=[jax.ShapeDtypeStruct((T, s), F32) for s in splits],
        grid=(NT,),
        in_specs=x_specs + [_full((1, D)), pl.BlockSpec((1, 1, N_MOD * D), lambda i: (_group(i), 0, 0)),
                            _full((D, n))],
        out_specs=[pl.BlockSpec((TM, s), row) for s in splits],
        compiler_params=_params(("arbitrary",), 48),
        name="in_proj",
    )(*x_args, g.reshape(1, D), mod, w_bf16)


def _gelu(x):
    return 0.5 * x * (1.0 + lax.erf(x * (1.0 / math.sqrt(2.0))))


def _sgu_kernel(za_ref, lng_ref, ws_ref, bs_ref, o_ref):
    u = _gelu(za_ref[:, 0:HALF])
    va = _gelu(za_ref[:, HALF:2 * HALF])
    mu = jnp.mean(va, axis=-1, keepdims=True)
    dv = va - mu
    var = jnp.mean(dv * dv, axis=-1, keepdims=True)
    vn = dv * lax.rsqrt(var + 1e-5) * lng_ref[...]
    lane = lax.broadcasted_iota(I32, (LANES, LANES), 1)
    first = lane < HEAD
    for c in range(TM // LANES):
        rows = slice(c * LANES, (c + 1) * LANES)
        for p in range(HALF // LANES):
            cols = slice(p * LANES, (p + 1) * LANES)
            vp = vn[rows, cols]
            s = (jnp.dot(ws_ref[2 * p], jnp.where(first, vp, 0.0).astype(BF16), preferred_element_type=F32)
                 + jnp.dot(ws_ref[2 * p + 1], jnp.where(first, 0.0, vp).astype(BF16), preferred_element_type=F32))
            o_ref[rows, cols] = u[rows, cols] * (s + bs_ref[:, cols])


def _sgu(za, ln_g, w_s_bf16, bs_full):
    return pl.pallas_call(
        _sgu_kernel,
        out_shape=jax.ShapeDtypeStruct((T, HALF), F32),
        grid=(NT,),
        in_specs=[pl.BlockSpec((TM, 2 * HALF), lambda i: (i, 0)), _full((1, HALF)),
                  _full((8, LANES, LANES)), _full((LANES, HALF))],
        out_specs=pl.BlockSpec((TM, HALF), lambda i: (i, 0)),
        compiler_params=_params(("arbitrary",)),
        name="sgu",
    )(za, ln_g.reshape(1, HALF), w_s_bf16, bs_full)


def _rwkv_prep_kernel(zb_ref, zp_ref, zn_ref, mu_ref, kk_ref, ka_ref, rk_ref, w0_ref, a0_ref,
                      wup_ref, aup_ref, gup_ref, ones_ref,
                      r_ref, v_ref, kkn_ref, bonus_ref, g_ref, lw_ref, kt_ref, b_ref):
    i = pl.program_id(0)
    in_sample = i >= PROMPT_TILES
    pos = (i - PROMPT_TILES) % TILES_PER_SAMPLE
    is_first = jnp.logical_or(jnp.logical_not(in_sample), pos == 0)
    is_last = jnp.logical_or(jnp.logical_not(in_sample), pos == TILES_PER_SAMPLE - 1)
    zb = zb_ref[...]
    prev_row = jnp.where(is_first, 0.0, zp_ref[7:8, :])
    next_row = jnp.where(is_last, 0.0, zn_ref[0:1, :])
    rowid = lax.broadcasted_iota(I32, (TM, 1), 0)
    zp = jnp.where(rowid == 0, prev_row, pltpu.roll(zb, 1, 0))
    zn = jnp.where(rowid == TM - 1, next_row, pltpu.roll(zb, TM - 1, 0))
    zs = zb + mu_ref[0:1, :] * (zp - zb) + mu_ref[1:2, :] * (zn - zb)
    r = zs[:, 0:HALF]
    k = zs[:, HALF:2 * HALF]
    v = zs[:, 2 * HALF:3 * HALF]
    wa = zs[:, 3 * HALF:3 * HALF + LANES]
    gd = zs[:, 3 * HALF + LANES:B_COLS]
    ones_bd = ones_ref[...]
    r_ref[...] = r
    v_ref[...] = v
    g_ref[...] = jnp.dot(jax.nn.sigmoid(gd).astype(BF16), gup_ref[...], preferred_element_type=F32)
    kk = k * kk_ref[...]
    kkn = kk / jnp.maximum(jnp.sqrt(_group_sum(kk * kk, ones_bd)), 1e-6)
    kkn_ref[...] = kkn
    bonus_ref[...] = _group_sum(r * k * rk_ref[...], ones_bd) * v
    tw = jnp.tanh(wa).astype(BF16)
    wa16 = wa.astype(BF16)
    for dd in range(2):
        lw_ref[dd] = -W_DECAY_SCALE * jax.nn.sigmoid(
            w0_ref[dd:dd + 1, :] + jnp.dot(tw, wup_ref[dd], preferred_element_type=F32))
        a = jax.nn.sigmoid(a0_ref[dd:dd + 1, :] + jnp.dot(wa16, aup_ref[dd], preferred_element_type=F32))
        kt_ref[dd] = k * (1.0 + (a - 1.0) * ka_ref[...])
        b_ref[dd] = a * kkn


def _rwkv_prep(zb, mu, k_k, k_a, r_k, w0, a0, wup_pad, aup_pad, g_up, ones_bd):
    row = lambda i: (i, 0)
    halo = TM // 8
    one = jax.ShapeDtypeStruct((T, HALF), F32)
    two = jax.ShapeDtypeStruct((2, T, HALF), F32)
    o1 = pl.BlockSpec((TM, HALF), row)
    o2 = pl.BlockSpec((2, TM, HALF), lambda i: (0, i, 0))
    return pl.pallas_call(
        _rwkv_prep_kernel,
        out_shape=[one, one, one, one, one, two, two, two],
        grid=(NT,),
        in_specs=[pl.BlockSpec((TM, B_COLS), row),
                  pl.BlockSpec((8, B_COLS), lambda i: (jnp.maximum(i * halo - 1, 0), 0)),
                  pl.BlockSpec((8, B_COLS), lambda i: (jnp.minimum((i + 1) * halo, T // 8 - 1), 0)),
                  _full((2, B_COLS)), _full((1, HALF)), _full((1, HALF)), _full((1, HALF)),
                  _full((2, HALF)), _full((2, HALF)),
                  _full((2, LANES, HALF)), _full((2, LANES, HALF)), _full((LANES, HALF)),
                  _full((HALF, HALF))],
        out_specs=[o1, o1, o1, o1, o1, o2, o2, o2],
        compiler_params=_params(("arbitrary",), 48),
        name="rwkv_prep",
    )(zb, zb, zb, mu, k_k.reshape(1, HALF), k_a.reshape(1, HALF), r_k.reshape(1, HALF), w0, a0,
      wup_pad, aup_pad, g_up, ones_bd)


def _rwkv_chunks(dirs):
    C = RW_CHUNK
    ti = lax.broadcasted_iota(I32, (C, C), 0)
    tj = lax.broadcasted_iota(I32, (C, C), 1)
    bi = lax.broadcasted_iota(I32, (LANES, LANES), 0)
    bj = lax.broadcasted_iota(I32, (LANES, LANES), 1)
    same = (bi >> 6) == (bj >> 6)
    pi = bi & (C - 1)
    pj = bj & (C - 1)
    eye = (bi == bj).astype(F32)
    h0 = lax.broadcasted_iota(I32, (C, LANES), 1) < HEAD

    def stack(x):
        return jnp.concatenate([jnp.where(h0, x, 0.0), jnp.where(h0, 0.0, x)], axis=0)

    def fold(x):
        return x[0:C] + x[C:2 * C]

    chains = []
    for rev, r, v, kk, lw, kt, b, s_ref, y_ref in dirs:
        tri = jnp.where((tj >= ti) if rev else (tj <= ti), 1.0, 0.0).astype(F32)
        p1 = lw.astype(BF16)
        r1 = lw - p1.astype(F32)
        p2 = r1.astype(BF16)
        p3 = (r1 - p2.astype(F32)).astype(BF16)
        cs3 = jnp.dot(tri.astype(BF16), jnp.concatenate([p1, p2, p3], axis=1), preferred_element_type=F32)
        cs = cs3[:, 0:HALF] + (cs3[:, HALF:2 * HALF] + cs3[:, 2 * HALF:3 * HALF])
        ctot = cs[0:1, :] if rev else cs[C - 1:C, :]
        e_neg = jnp.exp(-cs)
        e_tail = jnp.exp(ctot - cs)
        q1 = kk * jnp.exp(cs - lw)
        k1 = kt * e_neg
        b1 = b * e_neg
        r1 = r * jnp.exp(cs)
        k2 = kt * e_tail
        b2 = b * e_tail
        e_tot = jnp.exp(ctot)
        strict = jnp.logical_and(same, (pj > pi) if rev else (pj < pi))
        incl = jnp.logical_and(same, (pj >= pi) if rev else (pj <= pi))
        for p in range(HALF // LANES):
            cols = slice(p * LANES, (p + 1) * LANES)
            chains.append(dict(p=p, cols=cols, strict=strict, incl=incl, s_ref=s_ref, y_ref=y_ref,
                               q1=q1[:, cols], k1=k1[:, cols], b1=b1[:, cols], r1=r1[:, cols],
                               k2=k2[:, cols], b2=b2[:, cols], v=v[:, cols], e_tot=e_tot[:, cols]))

    for ch in chains:
        lhs = jnp.concatenate([stack(ch["q1"]), stack(ch["r1"])], axis=0)
        rhs = jnp.concatenate([ch["k1"], ch["k1"], ch["b1"], ch["b1"]], axis=0)
        gm = _mm(lhs, rhs, NT_DIMS, 2)
        ch["mk"] = jnp.where(ch["strict"], gm[0:2 * C, 0:2 * C], 0.0)
        ch["mb"] = jnp.where(ch["strict"], gm[0:2 * C, 2 * C:4 * C], 0.0)
        ch["nk"] = jnp.where(ch["incl"], gm[2 * C:4 * C, 0:2 * C], 0.0)
        ch["nb"] = jnp.where(ch["incl"], gm[2 * C:4 * C, 2 * C:4 * C], 0.0)
        ch["tinv"] = eye - jnp.where((pi >> 1) == (pj >> 1), ch["mb"], 0.0)
    size = 2
    while size < C:
        sh = size.bit_length() - 1
        blk = jnp.logical_and((pi >> (sh + 1)) == (pj >> (sh + 1)), (pi >> sh) != (pj >> sh))
        for ch in chains:
            ch["tn"] = _mm(ch["tinv"], jnp.where(blk, ch["mb"], 0.0), NN, 1)
        for ch in chains:
            ch["tinv"] = ch["tinv"] - _mm(ch["tn"], ch["tinv"], NN, 2)
        size *= 2
    for ch in chains:
        vst = stack(ch["v"])
        ch["mkv"] = fold(_mm(ch["mk"], vst, NN, 2))
        ch["nkv"] = fold(_mm(ch["nk"], vst, NN, 1))
        ch["s"] = ch["s_ref"][ch["p"]]
        ch["qr"] = _mm(jnp.concatenate([ch["q1"], ch["r1"]], axis=0), ch["s"], NT_DIMS, 2)
    for ch in chains:
        ch["u"] = fold(_mm(ch["tinv"], stack(ch["mkv"] + ch["qr"][0:C]), NN, 2))
    for ch in chains:
        ch["y_ref"][:, ch["cols"]] = ch["qr"][C:2 * C] + ch["nkv"] - fold(_mm(ch["nb"], stack(ch["u"]), NN, 1))
        upd = _mm(jnp.concatenate([ch["v"], ch["u"]], axis=0),
                  jnp.concatenate([ch["k2"], -ch["b2"]], axis=0), TN, 2)
        ch["s_ref"][ch["p"]] = ch["s"] * ch["e_tot"] + jnp.where(same, upd, 0.0)


def _rwkv_scan_kernel(bf_ref, bb_ref, first_ref, last_ref, seq_ref,
                      rf_ref, vf_ref, kkf_ref, lwf_ref, ktf_ref, bfw_ref,
                      rb_ref, vb_ref, kkb_ref, lwb_ref, ktb_ref, bbw_ref, s0_ref,
                      yf_ref, yb_ref, sfin_ref, s_ref):
    step = pl.program_id(0)

    @pl.when(first_ref[step] == 1)
    def _():
        s_ref[...] = s0_ref[:, 0]

    C = RW_CHUNK
    for sub in range(RW_STEP // C):
        f = pl.ds(sub * C, C)
        b = pl.ds(RW_STEP - (sub + 1) * C, C)
        _rwkv_chunks([
            (False, rf_ref[f, :], vf_ref[f, :], kkf_ref[f, :], lwf_ref[0, f, :], ktf_ref[0, f, :], bfw_ref[0, f, :],
             s_ref.at[0], yf_ref.at[f]),
            (True, rb_ref[b, :], vb_ref[b, :], kkb_ref[b, :], lwb_ref[0, b, :], ktb_ref[0, b, :], bbw_ref[0, b, :],
             s_ref.at[1], yb_ref.at[b])])

    @pl.when(last_ref[step] == 1)
    def _():
        sfin_ref[:, 0] = s_ref[...]


def _rwkv_scan(r, v, kk, lw, kt, b, s0_bd):
    C = RW_STEP
    tabs = _seq_tables(C)
    fwd = lambda i, bf, bb, fi, la, sq: (bf[i], 0)
    bwd = lambda i, bf, bb, fi, la, sq: (bb[i], 0)
    fwd3 = lambda i, bf, bb, fi, la, sq: (0, bf[i], 0)
    bwd3 = lambda i, bf, bb, fi, la, sq: (1, bb[i], 0)
    st = pl.BlockSpec((2, 1, 4, LANES, LANES), lambda i, bf, bb, fi, la, sq: (0, sq[i], 0, 0, 0))
    one_f, one_b = pl.BlockSpec((C, HALF), fwd), pl.BlockSpec((C, HALF), bwd)
    two_f, two_b = pl.BlockSpec((1, C, HALF), fwd3), pl.BlockSpec((1, C, HALF), bwd3)
    return pl.pallas_call(
        _rwkv_scan_kernel,
        out_shape=[jax.ShapeDtypeStruct((T, HALF), F32), jax.ShapeDtypeStruct((T, HALF), F32),
                   jax.ShapeDtypeStruct((2, N_SEQ, 4, LANES, LANES), F32)],
        grid_spec=pltpu.PrefetchScalarGridSpec(
            num_scalar_prefetch=5, grid=(len(tabs[0]),),
            in_specs=[one_f, one_f, one_f, two_f, two_f, two_f,
                      one_b, one_b, one_b, two_b, two_b, two_b, st],
            out_specs=[one_f, one_b, st],
            scratch_shapes=[pltpu.VMEM((2, 4, LANES, LANES), F32)]),
        compiler_params=_params(("arbitrary",)),
        name="rwkv_scan",
    )(*tabs, r, v, kk, lw, kt, b, r, v, kk, lw, kt, b, s0_bd)


def _rope(x, cos, sin_signed, first16):
    w = x.shape[1]
    partner = jnp.where(first16, pltpu.roll(x, w - 16, 1), pltpu.roll(x, 16, 1))
    return x * cos + partner * sin_signed


def _odd_prep_kernel(zc_ref, zr_ref, cos_ref, sin_ref, qkg_ref, ones_ref, cq_ref, ck_ref, ckraw_ref, rqk_ref):
    ones_bd = ones_ref[...]
    cos = cos_ref[...]
    sin = sin_ref[...]
    lane = lax.broadcasted_iota(I32, (TM, HALF), 1)
    first16 = (lane & 31) < 16
    for idx, (o_ref, raw_ref) in enumerate(((cq_ref, None), (ck_ref, ckraw_ref))):
        x = zc_ref[:, idx * HALF:(idx + 1) * HALF]
        ms = _group_sum(x * x, ones_bd) * (1.0 / HEAD)
        xn = x * lax.rsqrt(ms + 1e-6) * qkg_ref[idx:idx + 1, :]
        if raw_ref is not None:
            raw_ref[...] = xn
        o_ref[...] = _rope(xn, cos, sin, first16)
    rqk = _rope(zr_ref[...], cos, sin, first16)
    rqk_ref[...] = jnp.where(lane < HALF // 2, rqk * (HEAD ** -0.5), rqk)


def _odd_prep(zc, zr, cos_tab, sin_tab, qkg_tiled, ones_bd):
    row = lambda i: (i, 0)
    tab = lambda i: (jnp.where(i < PROMPT_TILES, 0, 1 + (i - PROMPT_TILES) % TILES_PER_SAMPLE), 0)
    one = jax.ShapeDtypeStruct((T, HALF), F32)
    o1 = pl.BlockSpec((TM, HALF), row)
    return pl.pallas_call(
        _odd_prep_kernel,
        out_shape=[one, one, one, one], grid=(NT,),
        in_specs=[pl.BlockSpec((TM, 2 * HALF), row), pl.BlockSpec((TM, HALF), row),
                  pl.BlockSpec((TM, HALF), tab), pl.BlockSpec((TM, HALF), tab),
                  _full((2, HALF)), _full((HALF, HALF))],
        out_specs=[o1, o1, o1, o1],
        compiler_params=_params(("arbitrary",)),
        name="odd_prep",
    )(zc, zr, cos_tab, sin_tab, qkg_tiled, ones_bd)


def _attn_kernel(*refs, has_ctx, one_minus_li):
    if has_ctx:
        q_ref, k_ref, v_ref, kc_ref, vc_ref, lam_ref, sg_ref, _, o_ref = refs
    else:
        q_ref, k_ref, v_ref, lam_ref, sg_ref, _, o_ref = refs
    lam = lam_ref[...]
    lane = lax.broadcasted_iota(I32, (LANES, LANES), 1)
    m0 = lane < HEAD
    scale = HEAD ** -0.5
    for h in range(4):
        cols = slice(h * LANES, (h + 1) * LANES)
        qp = q_ref[:, cols]
        segs = [(k_ref[:, cols], v_ref[:, cols])]
        if has_ctx:
            segs.append((kc_ref[0, :, cols], vc_ref[0, :, cols]))
        probs, dens = [], []
        for qm in (jnp.where(m0, qp, 0.0), jnp.where(m0, 0.0, qp)):
            qm16 = qm.astype(BF16)
            ss = [lax.dot_general(qm16, ks.astype(BF16), NT_DIMS, preferred_element_type=F32) * scale
                  for ks, _ in segs]
            mx = ss[0].max(axis=-1, keepdims=True)
            for s_ in ss[1:]:
                mx = jnp.maximum(mx, s_.max(axis=-1, keepdims=True))
            ps = [jnp.exp(s_ - mx) for s_ in ss]
            den = ps[0].sum(axis=-1, keepdims=True)
            for p_ in ps[1:]:
                den = den + p_.sum(axis=-1, keepdims=True)
            probs.append(ps)
            dens.append(den)
        if has_ctx:
            acc = None
            for si, (_, vs) in enumerate(segs):
                amap = probs[0][si] / dens[0] - lam * (probs[1][si] / dens[1])
                t = jnp.dot(amap.astype(BF16), vs.astype(BF16), preferred_element_type=F32)
                acc = t if acc is None else acc + t
        else:
            v16 = segs[0][1].astype(BF16)
            pv = [jnp.dot(ps[0].astype(BF16), v16, preferred_element_type=F32) for ps in probs]
            acc = pv[0] / dens[0] - lam * (pv[1] / dens[1])
        nrm = acc * lax.rsqrt(jnp.mean(acc * acc, axis=-1, keepdims=True) + 1e-6) * sg_ref[...]
        o_ref[:, cols] = nrm * one_minus_li


def _attn(cq, ck, zc, row0, n_seq, seq_len, lam, subln_g, one_minus_li, prev, ctx_k=None, ctx_v=None):
    nq = seq_len // LANES
    qb0 = row0 // LANES
    sb0 = row0 // seq_len
    in_specs = [pl.BlockSpec((LANES, HALF), lambda s, q: (qb0 + s * nq + q, 0)),
                pl.BlockSpec((seq_len, HALF), lambda s, q: (sb0 + s, 0)),
                pl.BlockSpec((seq_len, HALF), lambda s, q: (sb0 + s, 2))]
    args = [cq, ck, zc]
    if ctx_k is not None:
        in_specs += [pl.BlockSpec((1, PAST, HALF), lambda s, q: (s, 0, 0))] * 2
        args += [ctx_k, ctx_v]
    in_specs += [_full((1, 1)), _full((1, LANES))]
    args += [lam.reshape(1, 1), subln_g.reshape(1, LANES)]
    in_specs.append(pl.BlockSpec(memory_space=pl.ANY))
    args.append(prev)
    aliases = {len(args) - 1: 0}
    return pl.pallas_call(
        functools.partial(_attn_kernel, has_ctx=ctx_k is not None, one_minus_li=one_minus_li),
        out_shape=jax.ShapeDtypeStruct((T, HALF), F32),
        grid=(n_seq, nq), in_specs=in_specs,
        out_specs=pl.BlockSpec((LANES, HALF), lambda s, q: (qb0 + s * nq + q, 0)),
        input_output_aliases=aliases,
        compiler_params=_params(("arbitrary", "arbitrary"), 48),
        name="diff_attn",
    )(*args)


_LOG_GAMMA = tuple(tuple(float(np.log1p(-np.exp2(-np.float32(e)), dtype=np.float32)) for e in es)
                   for es in RET_EXP)


def _ret_chunks(dirs):
    C = RET_CHUNK
    ii = lax.broadcasted_iota(I32, (C, C), 0)
    jj = lax.broadcasted_iota(I32, (C, C), 1)
    ri = lax.broadcasted_iota(I32, (C, 1), 0)
    lane = lax.broadcasted_iota(I32, (C, LANES), 1)
    chains = []
    for rev, qk_ref, v_ref, s_ref, o_ref in dirs:
        mask = (jj > ii) if rev else (jj <= ii)
        dist = jnp.where(mask, (jj - ii) if rev else (ii - jj), 0).astype(F32)
        kpow = (ri if rev else (C - 1 - ri)).astype(F32)
        qpow = ((C - ri) if rev else (ri + 1)).astype(F32)
        for h in range(4):
            lg = _LOG_GAMMA[1 if rev else 0][h]
            p = h // 2
            hm = (lane < HEAD) if h % 2 == 0 else (lane >= HEAD)
            qp = jnp.where(hm, qk_ref[:, p * LANES:(p + 1) * LANES], 0.0)
            kp = jnp.where(hm, qk_ref[:, HALF // 2 + p * LANES:HALF // 2 + (p + 1) * LANES], 0.0)
            chains.append(dict(
                h=h, lg=lg, s_ref=s_ref, o_ref=o_ref, q16=qp.astype(BF16), k16=kp.astype(BF16),
                qw16=(qp * jnp.exp(lg * qpow)).astype(BF16), kw16=(kp * jnp.exp(lg * kpow)).astype(BF16),
                v16=v_ref[:, h * LANES:(h + 1) * LANES].astype(BF16),
                decay=jnp.where(mask, jnp.exp(lg * dist), 0.0)))
    for ch in chains:
        ch["sc"] = (lax.dot_general(ch["q16"], ch["k16"], NT_DIMS, preferred_element_type=F32)
                    * ch["decay"]).astype(BF16)
        ch["s"] = ch["s_ref"][ch["h"]]
    for ch in chains:
        ch["o"] = (jnp.dot(ch["sc"], ch["v16"], preferred_element_type=F32)
                   + jnp.dot(ch["qw16"], ch["s"].astype(BF16), preferred_element_type=F32))
        ch["kv"] = lax.dot_general(ch["kw16"], ch["v16"], TN, preferred_element_type=F32)
    for ch in chains:
        h = ch["h"]
        ch["o_ref"][:, h * LANES:(h + 1) * LANES] = ch["o"]
        ch["s_ref"][h] = math.exp(ch["lg"] * C) * ch["s"] + ch["kv"]


def _ret_kernel(bf_ref, bb_ref, first_ref, last_ref, seq_ref,
                qkf_ref, vf_ref, qkb_ref, vb_ref, s0_ref, of_ref, ob_ref, sfin_ref, s_ref):
    step = pl.program_id(0)

    @pl.when(first_ref[step] == 1)
    def _():
        s_ref[...] = s0_ref[:, 0]

    C = RET_CHUNK
    for sub in range(RET_STEP // C):
        f = pl.ds(sub * C, C)
        b = pl.ds(RET_STEP - (sub + 1) * C, C)
        _ret_chunks([(False, qkf_ref.at[f], vf_ref.at[f], s_ref.at[0], of_ref.at[f]),
                     (True, qkb_ref.at[b], vb_ref.at[b], s_ref.at[1], ob_ref.at[b])])

    @pl.when(last_ref[step] == 1)
    def _():
        sfin_ref[:, 0] = s_ref[...]


def _retention(rqk, zr, s0):
    C = RET_STEP
    tabs = _seq_tables(C)
    st = pl.BlockSpec((2, 1, 4, LANES, LANES), lambda i, bf, bb, fi, la, sq: (0, sq[i], 0, 0, 0))
    spec = lambda use_b, col: pl.BlockSpec(
        (C, HALF), lambda i, bf, bb, fi, la, sq: ((bb if use_b else bf)[i], col))
    return pl.pallas_call(
        _ret_kernel,
        out_shape=[jax.ShapeDtypeStruct((T, HALF), F32), jax.ShapeDtypeStruct((T, HALF), F32),
                   jax.ShapeDtypeStruct((2, N_SEQ, 4, LANES, LANES), F32)],
        grid_spec=pltpu.PrefetchScalarGridSpec(
            num_scalar_prefetch=5, grid=(len(tabs[0]),),
            in_specs=[spec(False, 0), spec(False, 1), spec(True, 0), spec(True, 1), st],
            out_specs=[spec(False, 0), spec(True, 0), st],
            scratch_shapes=[pltpu.VMEM((2, 4, LANES, LANES), F32)]),
        compiler_params=_params(("arbitrary",)),
        name="retention",
    )(*tabs, rqk, zr, rqk, zr, s0)


def _out_kernel(*refs, even, n_x):
    x_refs, refs = refs[:n_x], refs[n_x:]
    if even:
        (a_ref, yf_ref, yb_ref, bonus_ref, g_ref, gng_ref, gnb_ref, ones_ref,
         mod_ref, ng_ref, wo_ref, rw_ref, rb_ref,
         y_ref, xp_ref, ti_ref, tg_ref, rk_ref, cnt_ref, run_ref) = refs
        ones_bd = ones_ref[...]
        ys = yf_ref[...] + yb_ref[...]
        mu = _group_sum(ys, ones_bd) * (1.0 / HEAD)
        dv = ys - mu
        var = _group_sum(dv * dv, ones_bd) * (1.0 / HEAD)
        yn = dv * lax.rsqrt(var + RWKV_GN_EPS) * gng_ref[...] + gnb_ref[...]
        left = a_ref[...]
        right = (yn + bonus_ref[...]) * g_ref[...]
    else:
        (c_ref, of_ref, ob_ref, rg_ref, gng_ref,
         mod_ref, ng_ref, wo_ref, rw_ref, rb_ref,
         y_ref, xp_ref, ti_ref, tg_ref, rk_ref, cnt_ref, run_ref) = refs
        left = c_ref[...]
        rg = rg_ref[...]
        gate = rg * jax.nn.sigmoid(rg)
        os_ = of_ref[...] + ob_ref[...]
        parts = []
        for h in range(4):
            oh = os_[:, h * LANES:(h + 1) * LANES]
            mu = jnp.mean(oh, axis=-1, keepdims=True)
            dv = oh - mu
            var = jnp.mean(dv * dv, axis=-1, keepdims=True)
            parts.append(dv * lax.rsqrt(var + 1e-5))
        right = gate * (jnp.concatenate(parts, axis=1) * gng_ref[...])
    mod = mod_ref[0]
    o = (jnp.dot(left.astype(BF16), wo_ref[0:HALF, :], preferred_element_type=F32)
         + jnp.dot(right.astype(BF16), wo_ref[HALF:2 * HALF, :], preferred_element_type=F32))
    y = _token_rows(x_refs) + mod[:, 2 * D:3 * D] * o
    y_ref[...] = y
    yn2 = y * lax.rsqrt(jnp.mean(y * y, axis=-1, keepdims=True) + 1e-6) * ng_ref[...]
    t = yn2 * (1.0 + mod[:, 4 * D:5 * D]) + mod[:, 3 * D:4 * D]
    xp_ref[...] = t
    logits = _mm(t, rw_ref[...], NN, 3) + rb_ref[...]
    lane = lax.broadcasted_iota(I32, (TM, LANES), 1)
    neg = jnp.float32(-jnp.inf)
    lg = jnp.where(lane < N_EXPERTS, logits, neg)
    vals, hits = [], []
    for _ in range(TOP_K):
        m = jnp.max(lg, axis=-1, keepdims=True)
        ix = jnp.min(jnp.where(lg == m, lane, LANES), axis=-1, keepdims=True)
        hit = lane == ix
        vals.append(m)
        hits.append((ix, hit))
        lg = jnp.where(hit, neg, lg)
    es = [jnp.exp(vv - vals[0]) for vv in vals]
    den = es[0] + es[1] + es[2] + es[3]

    @pl.when(pl.program_id(0) == 0)
    def _():
        run_ref[...] = jnp.zeros_like(run_ref)

    member = jnp.zeros((TM, LANES), F32)
    for _, hit in hits:
        member = member + jnp.where(hit, 1.0, 0.0)
    ri = lax.broadcasted_iota(I32, (TM, TM), 0)
    ci = lax.broadcasted_iota(I32, (TM, TM), 1)
    before = jnp.where(ci < ri, 1.0, 0.0).astype(BF16)
    seen = run_ref[...] + jnp.dot(before, member.astype(BF16), preferred_element_type=F32)
    ti = jnp.zeros((TM, LANES), I32)
    tg = jnp.zeros((TM, LANES), F32)
    rk = jnp.zeros((TM, LANES), F32)
    for kk, (ix, hit) in enumerate(hits):
        ti = jnp.where(lane == kk, ix, ti)
        tg = jnp.where(lane == kk, es[kk] / den, tg)
        rk = jnp.where(lane == kk, jnp.sum(jnp.where(hit, seen, 0.0), axis=-1, keepdims=True), rk)
    ti_ref[...] = ti
    tg_ref[...] = tg
    rk_ref[...] = rk.astype(I32)
    run_ref[...] = run_ref[...] + jnp.sum(member, axis=0, keepdims=True)
    cnt_ref[...] = run_ref[...]


def _out_proj(even, mix_args, mix_specs, x, mod, norm_g, w_out_bf16, rw_pad, rb_pad):
    row = lambda i: (i, 0)
    modspec = pl.BlockSpec((1, 1, N_MOD * D), lambda i: (_group(i), 0, 0))
    x_specs, x_args = _token_specs(x)
    in_specs = x_specs + list(mix_specs) + [modspec, _full((1, D)), _full((D, D)), _full((D, LANES)),
                                            _full((1, LANES))]
    args = x_args + list(mix_args) + [mod, norm_g.reshape(1, D), w_out_bf16, rw_pad, rb_pad]
    lane_i = jax.ShapeDtypeStruct((T, LANES), I32)
    lane_spec = pl.BlockSpec((TM, LANES), row)
    return pl.pallas_call(
        functools.partial(_out_kernel, even=even, n_x=len(x_args)),
        out_shape=[jax.ShapeDtypeStruct((T, D), F32), jax.ShapeDtypeStruct((T, D), F32),
                   lane_i, jax.ShapeDtypeStruct((T, LANES), F32), lane_i,
                   jax.ShapeDtypeStruct((1, LANES), F32)],
        grid=(NT,), in_specs=in_specs,
        out_specs=[pl.BlockSpec((TM, D), row), pl.BlockSpec((TM, D), row),
                   lane_spec, lane_spec, lane_spec, _full((1, LANES))],
        scratch_shapes=[pltpu.VMEM((1, LANES), F32)],
        compiler_params=_params(("arbitrary",), 48),
        name="out_proj",
    )(*args)


def _route_kernel(cnt_ref, ti_ref, rk_ref, dest_ref, te_ref, nt_ref):
    cnt = cnt_ref[...].astype(I32)
    ntile = lax.shift_right_logical(cnt + (TMX - 1), TMX.bit_length() - 1)
    ei = lax.broadcasted_iota(I32, (LANES, LANES), 0)
    ej = lax.broadcasted_iota(I32, (LANES, LANES), 1)
    upto = jnp.where(ei <= ej, 1.0, 0.0).astype(BF16)
    ntile_f = jnp.broadcast_to(ntile.astype(F32), (8, LANES))
    tile_end = jnp.dot(ntile_f.astype(BF16), upto, preferred_element_type=F32)[0:1, :]
    row_start = (tile_end - ntile.astype(F32)) * float(TMX)
    lane = lax.broadcasted_iota(I32, (TM, LANES), 1)
    ti = ti_ref[...]
    rk = rk_ref[...]
    spread = jnp.zeros((TM, LANES), F32)
    for k in range(TOP_K):
        hit = lane == ti[:, k:k + 1]
        start = jnp.sum(jnp.where(hit, row_start, 0.0), axis=-1, keepdims=True)
        spread = jnp.where((lane & (TOP_K - 1)) == k, start + rk[:, k:k + 1].astype(F32), spread)
    tok = lax.broadcasted_iota(I32, (TM, LANES), 0)
    keep = (tok & (LANES // TOP_K - 1)) == lax.shift_right_logical(lane, 2)
    flat = jnp.where(keep, spread, 0.0).reshape(TM * TOP_K // LANES, LANES // TOP_K, LANES).sum(axis=1)
    dest_ref[...] = flat.astype(I32)

    @pl.when(pl.program_id(0) == 0)
    def _():
        lane1 = lax.broadcasted_iota(I32, (1, LANES), 1)
        n_tiles = jnp.max(tile_end, axis=-1, keepdims=True)
        last_e = jnp.max(jnp.where(cnt > 0, lane1, 0), axis=-1, keepdims=True)
        tile = lax.broadcasted_iota(I32, (TM, 1), 0).astype(F32)
        te = jnp.sum(jnp.where(tile_end <= tile, 1, 0), axis=-1, keepdims=True)
        te = jnp.where(tile < n_tiles, te, last_e)
        te_ref[...] = jnp.broadcast_to(te, (TM, LANES)).astype(I32)
        first_row = lax.broadcasted_iota(I32, (8, LANES), 0) == 0
        nt_ref[...] = jnp.where(first_row, n_tiles, tile_end).astype(I32)


def _route(cnt, ti, rk):
    row = lambda i: (i, 0)
    return pl.pallas_call(
        _route_kernel,
        out_shape=[jax.ShapeDtypeStruct((N_ASSIGN // LANES, LANES), I32), jax.ShapeDtypeStruct((TM, LANES), I32),
                   jax.ShapeDtypeStruct((8, LANES), I32)],
        grid=(NT,),
        in_specs=[_full((1, LANES)), pl.BlockSpec((TM, LANES), row), pl.BlockSpec((TM, LANES), row)],
        out_specs=[pl.BlockSpec((TM * TOP_K // LANES, LANES), row), _full((TM, LANES)), _full((8, LANES))],
        compiler_params=_params(("arbitrary",)),
        name="moe_route",
    )(cnt, ti, rk)


def _start_row_copy(src_ref, src_row, dst_ref, dst_row, sem, queue):
    pltpu.async_copy(src_ref.at[pl.ds(src_row, 1)], dst_ref.at[pl.ds(dst_row, 1)], sem, priority=queue)


def _wait_tiles(n, src_ref, dst_ref, sem):
    for _ in range(n):
        pltpu.make_async_copy(src_ref, dst_ref, sem).wait()


def _dispatch_kernel(dest_ref, tend_ref, x_ref, xs_ref, zero_ref, sem):
    i = pl.program_id(0)

    @pl.when(i == 0)
    def _():
        zero_ref[...] = jnp.zeros_like(zero_ref)

        def last_tile(e, fn):
            end = tend_ref[e]
            begin = tend_ref[e - 1] if e > 0 else 0

            @pl.when(end > begin)
            def _():
                fn(pltpu.make_async_copy(zero_ref, xs_ref.at[pl.ds((end - 1) * TMX, TMX)], sem))

        def unused_tile(j):
            return pltpu.make_async_copy(zero_ref, xs_ref.at[pl.ds(j * TMX, TMX)], sem)

        def start_unused(j, carry):
            unused_tile(j).start()
            return carry

        def wait_unused(j, carry):
            unused_tile(j).wait()
            return carry

        n_used = tend_ref[N_EXPERTS - 1]
        for e in range(N_EXPERTS):
            last_tile(e, lambda c: c.start())
        lax.fori_loop(n_used, MOE_TILES, start_unused, 0)
        for e in range(N_EXPERTS):
            last_tile(e, lambda c: c.wait())
        lax.fori_loop(n_used, MOE_TILES, wait_unused, 0)

    base = i * (TM * TOP_K)

    def start(r, carry):
        for k in range(TOP_K):
            _start_row_copy(x_ref, r, xs_ref, dest_ref[base + r * TOP_K + k], sem, k % 2)
        return carry

    lax.fori_loop(0, TM, start, 0, unroll=4)
    _wait_tiles(TOP_K, x_ref, xs_ref.at[pl.ds(0, TM)], sem)


def _dispatch(dest_flat, tile_end, xt):
    return pl.pallas_call(
        _dispatch_kernel,
        out_shape=jax.ShapeDtypeStruct((R_PAD, D), F32),
        grid_spec=pltpu.PrefetchScalarGridSpec(
            num_scalar_prefetch=2, grid=(NT,),
            in_specs=[pl.BlockSpec((TM, D), lambda i, d, te: (i, 0))],
            out_specs=pl.BlockSpec(memory_space=pl.ANY),
            scratch_shapes=[pltpu.VMEM((TMX, D), F32), pltpu.SemaphoreType.DMA(())]),
        compiler_params=_params(("arbitrary",)),
        name="moe_dispatch",
    )(dest_flat, tile_end, xt)


W_PARTS = 8


def _expert_weights(i, nt, te_ref, w_ref, wbuf_ref, wsem, w16_ref, group_ref):
    rows = w_ref.shape[1] // W_PARTS

    def fetch(e, buf):
        return [pltpu.make_async_copy(w_ref.at[e, pl.ds(p * rows, rows)], wbuf_ref.at[buf, pl.ds(p * rows, rows)],
                                      wsem.at[buf]) for p in range(W_PARTS)]

    @pl.when(i == 0)
    def _():
        group_ref[0] = 0
        for c in fetch(te_ref[0], 0):
            c.start()

    first = jnp.logical_or(i == 0, te_ref[i] != te_ref[jnp.maximum(i - 1, 0)])

    @pl.when(jnp.logical_and(first, i < nt))
    def _():
        cur = group_ref[0] % 2
        nxt = lax.while_loop(
            lambda j: jnp.logical_and(j < nt, te_ref[jnp.minimum(j, MOE_TILES - 1)] == te_ref[i]),
            lambda j: j + 1, i + 1)

        @pl.when(nxt < nt)
        def _():
            for c in fetch(te_ref[jnp.minimum(nxt, MOE_TILES - 1)], 1 - cur):
                c.start()

        for c in fetch(0, cur):
            c.wait()
        w16_ref[...] = wbuf_ref[cur].astype(BF16)
        group_ref[0] = group_ref[0] + 1


def _experts_kernel(te_ref, nt_ref, xs_ref, wgu_ref, bgu_ref, wdn_ref, bdn_ref, y_ref,
                    gu_buf, gu_sem, gu16_ref, gu_group, dn_buf, dn_sem, dn16_ref, dn_group):
    i = pl.program_id(0)
    nt = nt_ref[0]
    _expert_weights(i, nt, te_ref, wgu_ref, gu_buf, gu_sem, gu16_ref, gu_group)
    _expert_weights(i, nt, te_ref, wdn_ref, dn_buf, dn_sem, dn16_ref, dn_group)

    @pl.when(i < nt)
    def _():
        x16 = xs_ref[...].astype(BF16)
        y = bdn_ref[0]
        for h in range(2):
            gc = slice(h * HALF, (h + 1) * HALF)
            uc = slice(D + h * HALF, D + (h + 1) * HALF)
            g = jnp.dot(x16, gu16_ref[:, gc], preferred_element_type=F32) + bgu_ref[0, :, gc]
            u = jnp.dot(x16, gu16_ref[:, uc], preferred_element_type=F32) + bgu_ref[0, :, uc]
            gt = jnp.minimum(g, SWIGLU_LIMIT)
            up = jnp.clip(u, -SWIGLU_LIMIT, SWIGLU_LIMIT)
            act = ((up + 1.0) * gt * jax.nn.sigmoid(SWIGLU_ALPHA * gt)).astype(BF16)
            y = y + jnp.dot(act, dn16_ref[gc, :], preferred_element_type=F32)
        y_ref[...] = y

    @pl.when(i >= nt)
    def _():
        y_ref[...] = jnp.zeros_like(y_ref)


def _tile_clamped(i, te, nt):
    return (jnp.minimum(i, jnp.maximum(nt[0] - 1, 0)), 0)


def _weight_scratch(n_out):
    return [pltpu.VMEM((2, D, n_out), F32), pltpu.SemaphoreType.DMA((2,)), pltpu.VMEM((D, n_out), BF16),
            pltpu.SMEM((1,), I32)]


def _experts(te, n_tiles, xs, w_gu, b_gu, w_dn, b_dn):
    return pl.pallas_call(
        _experts_kernel,
        out_shape=jax.ShapeDtypeStruct((R_PAD, D), F32),
        grid_spec=pltpu.PrefetchScalarGridSpec(
            num_scalar_prefetch=2, grid=(MOE_TILES,),
            in_specs=[pl.BlockSpec((TMX, D), _tile_clamped), pl.BlockSpec(memory_space=pl.ANY),
                      pl.BlockSpec((1, 1, 2 * D), lambda i, te, nt: (te[i], 0, 0)),
                      pl.BlockSpec(memory_space=pl.ANY),
                      pl.BlockSpec((1, 1, D), lambda i, te, nt: (te[i], 0, 0))],
            out_specs=pl.BlockSpec((TMX, D), lambda i, te, nt: (i, 0)),
            scratch_shapes=_weight_scratch(2 * D) + _weight_scratch(D)),
        compiler_params=_params(("arbitrary",), 58),
        name="moe_experts",
    )(te, n_tiles, xs, w_gu, b_gu, w_dn, b_dn)


def _combine_kernel(dest_ref, x_ref, tg_ref, mod_ref, ys_ref, *rest):
    *o_refs, buf_ref, sem = rest
    i = pl.program_id(0)
    slot = i % 2

    def gather(tile, b):
        base = tile * (TM * TOP_K)

        def body(r, carry):
            for k in range(TOP_K):
                _start_row_copy(ys_ref, dest_ref[base + r * TOP_K + k], buf_ref.at[b, k], r, sem.at[b], k % 2)
            return carry

        lax.fori_loop(0, TM, body, 0, unroll=4)

    @pl.when(i == 0)
    def _():
        gather(0, 0)

    @pl.when(i + 1 < NT)
    def _():
        gather(i + 1, 1 - slot)

    _wait_tiles(TOP_K, ys_ref.at[pl.ds(0, TM)], buf_ref.at[slot, 0], sem.at[slot])
    tg = tg_ref[...]
    f = tg[:, 0:1] * buf_ref[slot, 0]
    for k in range(1, TOP_K):
        f = f + tg[:, k:k + 1] * buf_ref[slot, k]
    res = x_ref[...] + mod_ref[0][:, 5 * D:6 * D] * f
    if len(o_refs) == 1:
        o_refs[0][...] = res
    else:
        @pl.when(i < PROMPT_TILES)
        def _():
            o_refs[0][...] = res

        @pl.when(i >= PROMPT_TILES)
        def _():
            o_refs[1][...] = res


def _combine(dest_flat, x, tg, mod, ys, split=False):
    row = lambda i, d: (i, 0)
    if split:
        out_shape = [jax.ShapeDtypeStruct((T_PROMPT, D), F32), jax.ShapeDtypeStruct((T_SAMPLE, D), F32)]
        out_specs = [pl.BlockSpec((TM, D), lambda i, d: (jnp.minimum(i, PROMPT_TILES - 1), 0)),
                     pl.BlockSpec((TM, D), lambda i, d: (jnp.maximum(i - PROMPT_TILES, 0), 0))]
    else:
        out_shape = jax.ShapeDtypeStruct((T, D), F32)
        out_specs = pl.BlockSpec((TM, D), row)
    return pl.pallas_call(
        _combine_kernel,
        out_shape=out_shape,
        grid_spec=pltpu.PrefetchScalarGridSpec(
            num_scalar_prefetch=1, grid=(NT,),
            in_specs=[pl.BlockSpec((TM, D), row), pl.BlockSpec((TM, LANES), row),
                      pl.BlockSpec((1, 1, N_MOD * D), lambda i, d: (_group(i), 0, 0)),
                      pl.BlockSpec(memory_space=pl.ANY)],
            out_specs=out_specs,
            scratch_shapes=[pltpu.VMEM((2, TOP_K, TM, D), F32), pltpu.SemaphoreType.DMA((2,))]),
        compiler_params=_params(("arbitrary",), 40),
        name="moe_combine",
    )(dest_flat, x, tg, mod, ys)


def _moe(layer, y, xt, ti, tg, rk, cnt, mod, w_gu, b_gu, w_dn, b_dn):
    dest, te, nt = _route(cnt, ti, rk)
    dest_flat = dest.reshape(-1)
    te = te[:MOE_TILES, 0] + layer * N_EXPERTS
    n_tiles = nt[0, :1]
    xs = _dispatch(dest_flat, nt[1, :N_EXPERTS], xt)
    n_all = w_gu.shape[0] * N_EXPERTS
    ys = _experts(te, n_tiles, xs, w_gu.reshape(n_all, D, 2 * D), b_gu.reshape(n_all, 1, 2 * D),
                  w_dn.reshape(n_all, D, D), b_dn.reshape(n_all, 1, D))
    return _combine(dest_flat, y, tg, mod, ys, split=layer == 1)


def _ones_blockdiag():
    idx = np.arange(HALF) // HEAD
    return jnp.asarray((idx[:, None] == idx[None, :]).astype(np.float32), dtype=BF16)


def _rope_tables():
    pos = jnp.arange(L_SAMPLE)
    rowp = (pos // 64).astype(F32)
    colp = (pos % 64).astype(F32)
    nf = HEAD // 4
    inv = jnp.power(10000.0, -jnp.arange(nf, dtype=F32) / nf)
    ar = rowp[:, None] * inv[None, :]
    ac = colp[:, None] * inv[None, :]
    cos64 = jnp.concatenate([jnp.cos(ar), jnp.cos(ar), jnp.cos(ac), jnp.cos(ac)], axis=1)
    sin64 = jnp.concatenate([-jnp.sin(ar), jnp.sin(ar), -jnp.sin(ac), jnp.sin(ac)], axis=1)
    cos = jnp.tile(cos64, (1, HALF // HEAD))
    sin = jnp.tile(sin64, (1, HALF // HEAD))
    ident = jnp.ones((TM, HALF), F32)
    return (jnp.concatenate([ident, cos], axis=0), jnp.concatenate([jnp.zeros((TM, HALF), F32), sin], axis=0))


def _bd_pairs(s):
    lead = s.shape[:-3]
    s = s.reshape(lead + (4, 2, HEAD, HEAD))
    z = jnp.zeros_like(s[..., 0, :, :])
    top = jnp.concatenate([s[..., 0, :, :], z], axis=-1)
    bot = jnp.concatenate([z, s[..., 1, :, :]], axis=-1)
    return jnp.concatenate([top, bot], axis=-2)


def _bd_unpairs(s):
    a = s[..., 0:HEAD, 0:HEAD]
    b = s[..., HEAD:, HEAD:]
    out = jnp.stack([a, b], axis=-3)
    return out.reshape(s.shape[:-3] + (8, HEAD, HEAD))


def kernel(x_prompt, x_sample, state_rwkv, cache_k_diff, cache_v_diff, state_retention, c, c_ctx, norm_g, ada_w, ada_b, e_w_in, e_w_out, sgu_ln_g, sgu_w, sgu_b, rw_mu, rw_w0, rw_w_up, rw_a0, rw_a_up, rw_g_up, rw_k_k, rw_k_a, rw_r_k, rw_gn_g, rw_gn_b, o_w_in, o_w_out, da_qk_g, da_lam, da_subln_g, ret_gn_g, router_w, router_b, ex_w_gu, ex_b_gu, ex_w_dn, ex_b_dn):
    x = (x_prompt.reshape(T_PROMPT, D), x_sample.reshape(T_SAMPLE, D))
    cvec8 = jnp.concatenate([c_ctx[None, :], c, jnp.zeros((3, D), F32)], axis=0)
    mods = _adaln(cvec8, ada_w, ada_b)
    mod0 = mods[0].reshape(8, 1, N_MOD * D)
    mod1 = mods[1].reshape(8, 1, N_MOD * D)
    ones_bd = _ones_blockdiag()
    rw_pad = jnp.pad(router_w, ((0, 0), (0, 0), (0, LANES - N_EXPERTS)))
    rb_pad = jnp.pad(router_b, ((0, 0), (0, LANES - N_EXPERTS))).reshape(2, 1, LANES)
    row = lambda i: (i, 0)
    half = pl.BlockSpec((TM, HALF), row)

    za, zb = _in_proj(x, norm_g[0, 0], mod0, e_w_in[0].astype(BF16), (2 * HALF, B_COLS))
    bs_full = jnp.repeat(sgu_b[0].T, HEAD, axis=1)
    a_out = _sgu(za, sgu_ln_g[0], sgu_w[0].astype(BF16), bs_full)
    zpad = jnp.zeros((2, HEAD, HALF), F32)
    wup_pad = jnp.concatenate([rw_w_up[0], zpad], axis=1).astype(BF16)
    aup_pad = jnp.concatenate([zpad, rw_a_up[0]], axis=1).astype(BF16)
    r, v, kkn, bonus, g, lw, kt, b = _rwkv_prep(zb, rw_mu[0], rw_k_k[0], rw_k_a[0], rw_r_k[0], rw_w0[0], rw_a0[0],
                                                wup_pad, aup_pad, rw_g_up[0].astype(BF16), ones_bd)
    s0_sample = _bd_pairs(jnp.moveaxis(state_rwkv[:, 0], 1, 0))
    s0_rw = jnp.concatenate([jnp.zeros((2, N_PROMPT, 4, LANES, LANES), F32), s0_sample], axis=1)
    yf_rw, yb_rw, sfin_rw = _rwkv_scan(r, v, kkn, lw, kt, b, s0_rw)
    new_rwkv = jnp.moveaxis(_bd_unpairs(sfin_rw[:, :N_PROMPT]), 0, 1)[:, None]
    y0, xp0, ti0, tg0, rk0, cnt0 = _out_proj(
        True,
        [a_out, yf_rw, yb_rw, bonus, g, rw_gn_g[0].reshape(1, HALF), rw_gn_b[0].reshape(1, HALF), ones_bd],
        [half, half, half, half, half, _full((1, HALF)), _full((1, HALF)), _full((HALF, HALF))],
        x, mod0, norm_g[0, 1], e_w_out[0].astype(BF16), rw_pad[0], rb_pad[0])
    x1 = _moe(0, y0, xp0, ti0, tg0, rk0, cnt0, mod0, ex_w_gu, ex_b_gu, ex_w_dn, ex_b_dn)

    zc, zr = _in_proj(x1, norm_g[1, 0], mod1, o_w_in[0].astype(BF16), (3 * HALF, 3 * HALF))
    cos_tab, sin_tab = _rope_tables()
    qkg = jnp.tile(da_qk_g[0], (1, HALF // HEAD))
    cq, ck, ck_raw, rqk = _odd_prep(zc, zr, cos_tab, sin_tab, qkg, ones_bd)
    lambda_init = 0.8 - 0.6 * math.exp(-0.3 * 1)
    lv = da_lam[0]
    lam = jnp.exp(jnp.sum(lv[0] * lv[1])) - jnp.exp(jnp.sum(lv[2] * lv[3])) + lambda_init
    c_out = _attn(cq, ck, zc, 0, N_PROMPT, L_PROMPT, lam, da_subln_g[0], 1.0 - lambda_init,
                  jnp.zeros((T, HALF), F32))
    ctx_k = cache_k_diff[:, 0].reshape(N_SAMPLE, PAST, HALF)
    ctx_v = cache_v_diff[:, 0].reshape(N_SAMPLE, PAST, HALF)
    c_out = _attn(cq, ck, zc, T_PROMPT, N_SAMPLE, L_SAMPLE, lam, da_subln_g[0], 1.0 - lambda_init,
                  c_out, ctx_k, ctx_v)
    sr = jnp.moveaxis(state_retention[:, 0], 1, 0)
    zr0 = jnp.zeros_like(sr)
    s0_sample = jnp.stack([jnp.concatenate([sr[:, :, 0], zr0[:, :, 0]], axis=-2),
                           jnp.concatenate([zr0[:, :, 1], sr[:, :, 1]], axis=-2),
                           jnp.concatenate([sr[:, :, 2], zr0[:, :, 2]], axis=-2),
                           jnp.concatenate([zr0[:, :, 3], sr[:, :, 3]], axis=-2)], axis=2)
    s0_ret = jnp.concatenate([jnp.zeros((2, N_PROMPT, 4, LANES, LANES), F32), s0_sample], axis=1)
    of_ret, ob_ret, rfin = _retention(rqk, zr, s0_ret)
    rfin_p = rfin[:, :N_PROMPT]
    new_ret = jnp.stack([rfin_p[:, :, 0, 0:HEAD], rfin_p[:, :, 1, HEAD:], rfin_p[:, :, 2, 0:HEAD],
                         rfin_p[:, :, 3, HEAD:]], axis=2)
    new_ret = jnp.moveaxis(new_ret, 0, 1)[:, None]
    y1, xp1, ti1, tg1, rk1, cnt1 = _out_proj(
        False,
        [c_out, of_ret, ob_ret, zr, ret_gn_g[0].reshape(1, HALF)],
        [half, half, half, pl.BlockSpec((TM, HALF), lambda i: (i, 2)), _full((1, HALF))],
        x1, mod1, norm_g[1, 1], o_w_out[0].astype(BF16), rw_pad[1], rb_pad[1])
    y_prompt, y_sample = _moe(1, y1, xp1, ti1, tg1, rk1, cnt1, mod1, ex_w_gu, ex_b_gu, ex_w_dn, ex_b_dn)

    new_k = ck_raw[:T_PROMPT].reshape(N_PROMPT, 1, L_PROMPT, 4, LANES)
    new_v = zc[:T_PROMPT, 2 * HALF:3 * HALF].reshape(N_PROMPT, 1, L_PROMPT, 4, LANES)
    return (y_prompt.reshape(N_PROMPT, L_PROMPT, D), y_sample.reshape(N_SAMPLE, L_SAMPLE, D),
            new_rwkv, new_k, new_v, new_ret)
```

```python
import functools
import math

import numpy as np
import jax
import jax.numpy as jnp
from jax import lax
from jax.experimental import pallas as pl
from jax.experimental.pallas import tpu as pltpu

F32 = jnp.float32
BF16 = jnp.bfloat16
I32 = jnp.int32

D = 1024
N_PROMPT, L_PROMPT = 16, 256
N_SAMPLE, L_SAMPLE = 4, 1024
N_SEQ = N_PROMPT + N_SAMPLE
PAST = 256
T_PROMPT = N_PROMPT * L_PROMPT
T_SAMPLE = N_SAMPLE * L_SAMPLE
T = T_PROMPT + T_SAMPLE
TM = 256
NT = T // TM
PROMPT_TILES = T_PROMPT // TM
TILES_PER_SAMPLE = L_SAMPLE // TM
N_MOD = 6
HALF = 512
B_COLS = 1792
HEAD = 64
W_DECAY_SCALE = math.exp(-0.5)
RWKV_GN_EPS = 64e-5
RW_CHUNK = 64
RW_STEP = 256
RET_CHUNK = 128
RET_STEP = 256
RET_EXP = ((5.0, 7.0, 9.0, 11.0), (6.0, 8.0, 10.0, 12.0))
N_EXPERTS = 32
TOP_K = 4
SWIGLU_LIMIT = 7.0
SWIGLU_ALPHA = 1.702
N_ASSIGN = T * TOP_K
TMX = 512
MOE_TILES = N_ASSIGN // TMX + N_EXPERTS
R_PAD = MOE_TILES * TMX
LANES = 128

NN = (((1,), (0,)), ((), ()))
NT_DIMS = (((1,), (1,)), ((), ()))
TN = (((0,), (0,)), ((), ()))


def _group(i):
    return jnp.where(i < PROMPT_TILES, 0, 1 + (i - PROMPT_TILES) // TILES_PER_SAMPLE)


def _mm(a, b, dims=NN, passes=1):
    dg = functools.partial(lax.dot_general, dimension_numbers=dims, preferred_element_type=F32)
    if passes == 1:
        return dg(a.astype(BF16), b.astype(BF16))
    a = a.astype(F32)
    b = b.astype(F32)
    ah = a.astype(BF16)
    al = (a - ah.astype(F32)).astype(BF16)
    bh = b.astype(BF16)
    free = 1 - dims[0][0][0]
    m = a.shape[free]
    both = dg(jnp.concatenate([ah, al], axis=free), bh)
    if passes == 2:
        return both[0:m] + both[m:2 * m]
    bl = (b - bh.astype(F32)).astype(BF16)
    return both[0:m] + (dg(ah, bl) + both[m:2 * m])


def _group_sum(x, ones_bd):
    xh = x.astype(BF16)
    xl = (x - xh.astype(F32)).astype(BF16)
    return (jnp.dot(xh, ones_bd, preferred_element_type=F32)
            + jnp.dot(xl, ones_bd, preferred_element_type=F32))


def _full(shape):
    nd = len(shape)
    return pl.BlockSpec(shape, lambda *_: (0,) * nd)


def _params(sem, vmem_mb=None):
    kw = {}
    if vmem_mb is not None:
        kw["vmem_limit_bytes"] = vmem_mb * 1024 * 1024
    return pltpu.CompilerParams(dimension_semantics=sem, **kw)


def _seq_tables(chunk):
    blk_f, blk_b, first, last, seq = [], [], [], [], []
    row = 0
    for s in range(N_SEQ):
        n = (L_PROMPT if s < N_PROMPT else L_SAMPLE) // chunk
        base = row // chunk
        for j in range(n):
            blk_f.append(base + j)
            blk_b.append(base + n - 1 - j)
            first.append(int(j == 0))
            last.append(int(j == n - 1))
            seq.append(s)
        row += n * chunk
    return tuple(np.asarray(a, np.int32) for a in (blk_f, blk_b, first, last, seq))


def _adaln_kernel(c_ref, w_ref, b_ref, o_ref):
    c = c_ref[...]
    s = c * jax.nn.sigmoid(c)
    o_ref[0] = _mm(s, w_ref[0], NN, 3) + b_ref[0]


def _adaln(cvec8, ada_w, ada_b):
    depth, _, n = ada_w.shape
    bn = 1536
    return pl.pallas_call(
        _adaln_kernel,
        out_shape=jax.ShapeDtypeStruct((depth, 8, n), F32),
        grid=(depth, n // bn),
        in_specs=[pl.BlockSpec((8, D), lambda l, j: (0, 0)),
                  pl.BlockSpec((1, D, bn), lambda l, j: (l, 0, j)),
                  pl.BlockSpec((1, 1, bn), lambda l, j: (l, 0, j))],
        out_specs=pl.BlockSpec((1, 8, bn), lambda l, j: (l, 0, j)),
        compiler_params=_params(("arbitrary", "arbitrary"), 40),
        name="adaln",
    )(cvec8, ada_w, ada_b.reshape(depth, 1, n))


def _token_specs(x):
    if isinstance(x, tuple):
        return ([pl.BlockSpec((TM, D), lambda i, *_: (jnp.minimum(i, PROMPT_TILES - 1), 0)),
                 pl.BlockSpec((TM, D), lambda i, *_: (jnp.maximum(i - PROMPT_TILES, 0), 0))], list(x))
    return [pl.BlockSpec((TM, D), lambda i, *_: (i, 0))], [x]


def _token_rows(x_refs):
    if len(x_refs) == 2:
        return jnp.where(pl.program_id(0) < PROMPT_TILES, x_refs[0][...], x_refs[1][...])
    return x_refs[0][...]


def _in_kernel(*refs, splits, n_x, gated):
    n_in = n_x + (6 if gated else 3)
    x_refs, (g_ref, mod_ref, w_ref), outs = refs[:n_x], refs[n_x:n_x + 3], refs[n_in:]
    x = _token_rows(x_refs)
    mod = mod_ref[0]
    y = x * lax.rsqrt(jnp.mean(x * x, axis=-1, keepdims=True) + 1e-6) * g_ref[...]
    h = (y * (1.0 + mod[:, D:2 * D]) + mod[:, 0:D]).astype(BF16)
    off = 0
    for idx, (o_ref, n) in enumerate(zip(outs, splits)):
        z = jnp.dot(h, w_ref[:, off:off + n], preferred_element_type=F32)
        if gated and idx == 0:
            _sgu(z, *refs[n_x + 3:n_in], o_ref)
        else:
            o_ref[...] = z
        off += n


def _in_proj(x, g, mod, w_bf16, splits, sgu=None):
    n = w_bf16.shape[1]
    row = lambda i: (i, 0)
    x_specs, x_args = _token_specs(x)
    out_w = list(splits)
    extra_specs, extra_args = [], []
    if sgu is not None:
        out_w[0] = HALF
        extra_specs = [_full((1, HALF)), _full((8, LANES, LANES)), _full((LANES, HALF))]
        extra_args = [sgu[0].reshape(1, HALF), sgu[1], sgu[2]]
    return pl.pallas_call(
        functools.partial(_in_kernel, splits=splits, n_x=len(x_args), gated=sgu is not None),
        out_shape=[jax.ShapeDtypeStruct((T, s), F32) for s in out_w],
        grid=(NT,),
        in_specs=x_specs + [_full((1, D)), pl.BlockSpec((1, 1, N_MOD * D), lambda i: (_group(i), 0, 0)),
                            _full((D, n))] + extra_specs,
        out_specs=[pl.BlockSpec((TM, s), row) for s in out_w],
        compiler_params=_params(("arbitrary",), 48),
        name="in_proj",
    )(*x_args, g.reshape(1, D), mod, w_bf16, *extra_args)


def _gelu(x):
    return 0.5 * x * (1.0 + lax.erf(x * (1.0 / math.sqrt(2.0))))


def _sgu(za, lng_ref, ws_ref, bs_ref, o_ref):
    u = _gelu(za[:, 0:HALF])
    va = _gelu(za[:, HALF:2 * HALF])
    mu = jnp.mean(va, axis=-1, keepdims=True)
    dv = va - mu
    var = jnp.mean(dv * dv, axis=-1, keepdims=True)
    vn = dv * lax.rsqrt(var + 1e-5) * lng_ref[...]
    lane = lax.broadcasted_iota(I32, (LANES, LANES), 1)
    first = lane < HEAD
    for c in range(TM // LANES):
        rows = slice(c * LANES, (c + 1) * LANES)
        for p in range(HALF // LANES):
            cols = slice(p * LANES, (p + 1) * LANES)
            vp = vn[rows, cols]
            s = (jnp.dot(ws_ref[2 * p], jnp.where(first, vp, 0.0).astype(BF16), preferred_element_type=F32)
                 + jnp.dot(ws_ref[2 * p + 1], jnp.where(first, 0.0, vp).astype(BF16), preferred_element_type=F32))
            o_ref[rows, cols] = u[rows, cols] * (s + bs_ref[:, cols])


def _rwkv_prep_kernel(zb_ref, zp_ref, zn_ref, mu_ref, kk_ref, ka_ref, rk_ref, w0_ref, a0_ref,
                      wup_ref, aup_ref, gup_ref, ones_ref,
                      r_ref, v_ref, kkn_ref, bonus_ref, g_ref, lw_ref, kt_ref, b_ref):
    i = pl.program_id(0)
    in_sample = i >= PROMPT_TILES
    pos = (i - PROMPT_TILES) % TILES_PER_SAMPLE
    is_first = jnp.logical_or(jnp.logical_not(in_sample), pos == 0)
    is_last = jnp.logical_or(jnp.logical_not(in_sample), pos == TILES_PER_SAMPLE - 1)
    zb = zb_ref[...]
    prev_row = jnp.where(is_first, 0.0, zp_ref[7:8, :])
    next_row = jnp.where(is_last, 0.0, zn_ref[0:1, :])
    rowid = lax.broadcasted_iota(I32, (TM, 1), 0)
    zp = jnp.where(rowid == 0, prev_row, pltpu.roll(zb, 1, 0))
    zn = jnp.where(rowid == TM - 1, next_row, pltpu.roll(zb, TM - 1, 0))
    zs = zb + mu_ref[0:1, :] * (zp - zb) + mu_ref[1:2, :] * (zn - zb)
    r = zs[:, 0:HALF]
    k = zs[:, HALF:2 * HALF]
    v = zs[:, 2 * HALF:3 * HALF]
    wa = zs[:, 3 * HALF:3 * HALF + LANES]
    gd = zs[:, 3 * HALF + LANES:B_COLS]
    ones_bd = ones_ref[...]
    r_ref[...] = r
    v_ref[...] = v
    g_ref[...] = jnp.dot(jax.nn.sigmoid(gd).astype(BF16), gup_ref[...], preferred_element_type=F32)
    kk = k * kk_ref[...]
    kkn = kk / jnp.maximum(jnp.sqrt(_group_sum(kk * kk, ones_bd)), 1e-6)
    kkn_ref[...] = kkn
    bonus_ref[...] = _group_sum(r * k * rk_ref[...], ones_bd) * v
    tw = jnp.tanh(wa).astype(BF16)
    wa16 = wa.astype(BF16)
    for dd in range(2):
        lw_ref[dd] = -W_DECAY_SCALE * jax.nn.sigmoid(
            w0_ref[dd:dd + 1, :] + jnp.dot(tw, wup_ref[dd], preferred_element_type=F32))
        a = jax.nn.sigmoid(a0_ref[dd:dd + 1, :] + jnp.dot(wa16, aup_ref[dd], preferred_element_type=F32))
        kt_ref[dd] = k * (1.0 + (a - 1.0) * ka_ref[...])
        b_ref[dd] = a * kkn


def _rwkv_prep(zb, mu, k_k, k_a, r_k, w0, a0, wup_pad, aup_pad, g_up, ones_bd):
    row = lambda i: (i, 0)
    halo = TM // 8
    one = jax.ShapeDtypeStruct((T, HALF), F32)
    two = jax.ShapeDtypeStruct((2, T, HALF), F32)
    o1 = pl.BlockSpec((TM, HALF), row)
    o2 = pl.BlockSpec((2, TM, HALF), lambda i: (0, i, 0))
    return pl.pallas_call(
        _rwkv_prep_kernel,
        out_shape=[one, one, one, one, one, two, two, two],
        grid=(NT,),
        in_specs=[pl.BlockSpec((TM, B_COLS), row),
                  pl.BlockSpec((8, B_COLS), lambda i: (jnp.maximum(i * halo - 1, 0), 0)),
                  pl.BlockSpec((8, B_COLS), lambda i: (jnp.minimum((i + 1) * halo, T // 8 - 1), 0)),
                  _full((2, B_COLS)), _full((1, HALF)), _full((1, HALF)), _full((1, HALF)),
                  _full((2, HALF)), _full((2, HALF)),
                  _full((2, LANES, HALF)), _full((2, LANES, HALF)), _full((LANES, HALF)),
                  _full((HALF, HALF))],
        out_specs=[o1, o1, o1, o1, o1, o2, o2, o2],
        compiler_params=_params(("arbitrary",), 48),
        name="rwkv_prep",
    )(zb, zb, zb, mu, k_k.reshape(1, HALF), k_a.reshape(1, HALF), r_k.reshape(1, HALF), w0, a0,
      wup_pad, aup_pad, g_up, ones_bd)


def _rwkv_chunks(dirs):
    C = RW_CHUNK
    ti = lax.broadcasted_iota(I32, (C, C), 0)
    tj = lax.broadcasted_iota(I32, (C, C), 1)
    bi = lax.broadcasted_iota(I32, (LANES, LANES), 0)
    bj = lax.broadcasted_iota(I32, (LANES, LANES), 1)
    same = (bi >> 6) == (bj >> 6)
    pi = bi & (C - 1)
    pj = bj & (C - 1)
    eye = (bi == bj).astype(F32)
    h0 = lax.broadcasted_iota(I32, (C, LANES), 1) < HEAD

    def stack(x):
        return jnp.concatenate([jnp.where(h0, x, 0.0), jnp.where(h0, 0.0, x)], axis=0)

    def fold(x):
        return x[0:C] + x[C:2 * C]

    chains = []
    for rev, r, v, kk, lw, kt, b, s_ref, y_ref in dirs:
        tri = jnp.where((tj >= ti) if rev else (tj <= ti), 1.0, 0.0).astype(F32)
        p1 = lw.astype(BF16)
        r1 = lw - p1.astype(F32)
        p2 = r1.astype(BF16)
        p3 = (r1 - p2.astype(F32)).astype(BF16)
        cs3 = jnp.dot(tri.astype(BF16), jnp.concatenate([p1, p2, p3], axis=1), preferred_element_type=F32)
        cs = cs3[:, 0:HALF] + (cs3[:, HALF:2 * HALF] + cs3[:, 2 * HALF:3 * HALF])
        ctot = cs[0:1, :] if rev else cs[C - 1:C, :]
        e_neg = jnp.exp(-cs)
        e_tail = jnp.exp(ctot - cs)
        q1 = kk * jnp.exp(cs - lw)
        k1 = kt * e_neg
        b1 = b * e_neg
        r1 = r * jnp.exp(cs)
        k2 = kt * e_tail
        b2 = b * e_tail
        e_tot = jnp.exp(ctot)
        strict = jnp.logical_and(same, (pj > pi) if rev else (pj < pi))
        incl = jnp.logical_and(same, (pj >= pi) if rev else (pj <= pi))
        for p in range(HALF // LANES):
            cols = slice(p * LANES, (p + 1) * LANES)
            chains.append(dict(p=p, cols=cols, strict=strict, incl=incl, s_ref=s_ref, y_ref=y_ref,
                               q1=q1[:, cols], k1=k1[:, cols], b1=b1[:, cols], r1=r1[:, cols],
                               k2=k2[:, cols], b2=b2[:, cols], v=v[:, cols], e_tot=e_tot[:, cols]))

    for ch in chains:
        lhs = jnp.concatenate([stack(ch["q1"]), stack(ch["r1"])], axis=0)
        rhs = jnp.concatenate([ch["k1"], ch["k1"], ch["b1"], ch["b1"]], axis=0)
        gm = _mm(lhs, rhs, NT_DIMS, 2)
        ch["mk"] = jnp.where(ch["strict"], gm[0:2 * C, 0:2 * C], 0.0)
        ch["mb"] = jnp.where(ch["strict"], gm[0:2 * C, 2 * C:4 * C], 0.0)
        ch["nk"] = jnp.where(ch["incl"], gm[2 * C:4 * C, 0:2 * C], 0.0)
        ch["nb"] = jnp.where(ch["incl"], gm[2 * C:4 * C, 2 * C:4 * C], 0.0)
        ch["tinv"] = eye - jnp.where((pi >> 1) == (pj >> 1), ch["mb"], 0.0)
    size = 2
    while size < C:
        sh = size.bit_length() - 1
        blk = jnp.logical_and((pi >> (sh + 1)) == (pj >> (sh + 1)), (pi >> sh) != (pj >> sh))
        for ch in chains:
            ch["tn"] = _mm(ch["tinv"], jnp.where(blk, ch["mb"], 0.0), NN, 1)
        for ch in chains:
            ch["tinv"] = ch["tinv"] - _mm(ch["tn"], ch["tinv"], NN, 2)
        size *= 2
    for ch in chains:
        vst = stack(ch["v"])
        ch["mkv"] = fold(_mm(ch["mk"], vst, NN, 2))
        ch["nkv"] = fold(_mm(ch["nk"], vst, NN, 1))
        ch["s"] = ch["s_ref"][ch["p"]]
        ch["qr"] = _mm(jnp.concatenate([ch["q1"], ch["r1"]], axis=0), ch["s"], NT_DIMS, 2)
    for ch in chains:
        ch["u"] = fold(_mm(ch["tinv"], stack(ch["mkv"] + ch["qr"][0:C]), NN, 2))
    for ch in chains:
        ch["y_ref"][:, ch["cols"]] = ch["qr"][C:2 * C] + ch["nkv"] - fold(_mm(ch["nb"], stack(ch["u"]), NN, 1))
        upd = _mm(jnp.concatenate([ch["v"], ch["u"]], axis=0),
                  jnp.concatenate([ch["k2"], -ch["b2"]], axis=0), TN, 2)
        ch["s_ref"][ch["p"]] = ch["s"] * ch["e_tot"] + jnp.where(same, upd, 0.0)


def _rwkv_scan_kernel(bf_ref, bb_ref, first_ref, last_ref, seq_ref,
                      rf_ref, vf_ref, kkf_ref, lwf_ref, ktf_ref, bfw_ref,
                      rb_ref, vb_ref, kkb_ref, lwb_ref, ktb_ref, bbw_ref, s0_ref,
                      yf_ref, yb_ref, sfin_ref, s_ref):
    step = pl.program_id(0)

    @pl.when(first_ref[step] == 1)
    def _():
        s_ref[...] = s0_ref[:, 0]

    C = RW_CHUNK
    for sub in range(RW_STEP // C):
        f = pl.ds(sub * C, C)
        b = pl.ds(RW_STEP - (sub + 1) * C, C)
        _rwkv_chunks([
            (False, rf_ref[f, :], vf_ref[f, :], kkf_ref[f, :], lwf_ref[0, f, :], ktf_ref[0, f, :], bfw_ref[0, f, :],
             s_ref.at[0], yf_ref.at[f]),
            (True, rb_ref[b, :], vb_ref[b, :], kkb_ref[b, :], lwb_ref[0, b, :], ktb_ref[0, b, :], bbw_ref[0, b, :],
             s_ref.at[1], yb_ref.at[b])])

    @pl.when(last_ref[step] == 1)
    def _():
        sfin_ref[:, 0] = s_ref[...]


def _rwkv_scan(r, v, kk, lw, kt, b, s0_bd):
    C = RW_STEP
    tabs = _seq_tables(C)
    fwd = lambda i, bf, bb, fi, la, sq: (bf[i], 0)
    bwd = lambda i, bf, bb, fi, la, sq: (bb[i], 0)
    fwd3 = lambda i, bf, bb, fi, la, sq: (0, bf[i], 0)
    bwd3 = lambda i, bf, bb, fi, la, sq: (1, bb[i], 0)
    st = pl.BlockSpec((2, 1, 4, LANES, LANES), lambda i, bf, bb, fi, la, sq: (0, sq[i], 0, 0, 0))
    one_f, one_b = pl.BlockSpec((C, HALF), fwd), pl.BlockSpec((C, HALF), bwd)
    two_f, two_b = pl.BlockSpec((1, C, HALF), fwd3), pl.BlockSpec((1, C, HALF), bwd3)
    return pl.pallas_call(
        _rwkv_scan_kernel,
        out_shape=[jax.ShapeDtypeStruct((T, HALF), F32), jax.ShapeDtypeStruct((T, HALF), F32),
                   jax.ShapeDtypeStruct((2, N_SEQ, 4, LANES, LANES), F32)],
        grid_spec=pltpu.PrefetchScalarGridSpec(
            num_scalar_prefetch=5, grid=(len(tabs[0]),),
            in_specs=[one_f, one_f, one_f, two_f, two_f, two_f,
                      one_b, one_b, one_b, two_b, two_b, two_b, st],
            out_specs=[one_f, one_b, st],
            scratch_shapes=[pltpu.VMEM((2, 4, LANES, LANES), F32)]),
        compiler_params=_params(("arbitrary",)),
        name="rwkv_scan",
    )(*tabs, r, v, kk, lw, kt, b, r, v, kk, lw, kt, b, s0_bd)


def _rope(x, cos, sin_signed, first16):
    w = x.shape[1]
    partner = jnp.where(first16, pltpu.roll(x, w - 16, 1), pltpu.roll(x, 16, 1))
    return x * cos + partner * sin_signed


def _odd_prep_kernel(zc_ref, zr_ref, cos_ref, sin_ref, qkg_ref, ones_ref, cq_ref, ck_ref, ckraw_ref, rqk_ref):
    ones_bd = ones_ref[...]
    cos = cos_ref[...]
    sin = sin_ref[...]
    lane = lax.broadcasted_iota(I32, (TM, HALF), 1)
    first16 = (lane & 31) < 16
    for idx, (o_ref, raw_ref) in enumerate(((cq_ref, None), (ck_ref, ckraw_ref))):
        x = zc_ref[:, idx * HALF:(idx + 1) * HALF]
        ms = _group_sum(x * x, ones_bd) * (1.0 / HEAD)
        xn = x * lax.rsqrt(ms + 1e-6) * qkg_ref[idx:idx + 1, :]
        if raw_ref is not None:
            raw_ref[...] = xn
        o_ref[...] = _rope(xn, cos, sin, first16)
    rqk = _rope(zr_ref[...], cos, sin, first16)
    rqk_ref[...] = jnp.where(lane < HALF // 2, rqk * (HEAD ** -0.5), rqk)


def _odd_prep(zc, zr, cos_tab, sin_tab, qkg_tiled, ones_bd):
    row = lambda i: (i, 0)
    tab = lambda i: (jnp.where(i < PROMPT_TILES, 0, 1 + (i - PROMPT_TILES) % TILES_PER_SAMPLE), 0)
    one = jax.ShapeDtypeStruct((T, HALF), F32)
    o1 = pl.BlockSpec((TM, HALF), row)
    return pl.pallas_call(
        _odd_prep_kernel,
        out_shape=[one, one, one, one], grid=(NT,),
        in_specs=[pl.BlockSpec((TM, 2 * HALF), row), pl.BlockSpec((TM, HALF), row),
                  pl.BlockSpec((TM, HALF), tab), pl.BlockSpec((TM, HALF), tab),
                  _full((2, HALF)), _full((HALF, HALF))],
        out_specs=[o1, o1, o1, o1],
        compiler_params=_params(("arbitrary",)),
        name="odd_prep",
    )(zc, zr, cos_tab, sin_tab, qkg_tiled, ones_bd)


def _attn_kernel(*refs, has_ctx, one_minus_li):
    if has_ctx:
        q_ref, k_ref, v_ref, kc_ref, vc_ref, lam_ref, sg_ref, _, o_ref = refs
    else:
        q_ref, k_ref, v_ref, lam_ref, sg_ref, _, o_ref = refs
    lam = lam_ref[...]
    lane = lax.broadcasted_iota(I32, (LANES, LANES), 1)
    m0 = lane < HEAD
    scale = HEAD ** -0.5
    for h in range(4):
        cols = slice(h * LANES, (h + 1) * LANES)
        qp = q_ref[:, cols]
        segs = [(k_ref[:, cols], v_ref[:, cols])]
        if has_ctx:
            segs.append((kc_ref[0, :, cols], vc_ref[0, :, cols]))
        probs, dens = [], []
        for qm in (jnp.where(m0, qp, 0.0), jnp.where(m0, 0.0, qp)):
            qm16 = qm.astype(BF16)
            ss = [lax.dot_general(qm16, ks.astype(BF16), NT_DIMS, preferred_element_type=F32) * scale
                  for ks, _ in segs]
            mx = ss[0].max(axis=-1, keepdims=True)
            for s_ in ss[1:]:
                mx = jnp.maximum(mx, s_.max(axis=-1, keepdims=True))
            ps = [jnp.exp(s_ - mx) for s_ in ss]
            den = ps[0].sum(axis=-1, keepdims=True)
            for p_ in ps[1:]:
                den = den + p_.sum(axis=-1, keepdims=True)
            probs.append(ps)
            dens.append(den)
        if has_ctx:
            acc = None
            for si, (_, vs) in enumerate(segs):
                amap = probs[0][si] / dens[0] - lam * (probs[1][si] / dens[1])
                t = jnp.dot(amap.astype(BF16), vs.astype(BF16), preferred_element_type=F32)
                acc = t if acc is None else acc + t
        else:
            v16 = segs[0][1].astype(BF16)
            pv = [jnp.dot(ps[0].astype(BF16), v16, preferred_element_type=F32) for ps in probs]
            acc = pv[0] / dens[0] - lam * (pv[1] / dens[1])
        nrm = acc * lax.rsqrt(jnp.mean(acc * acc, axis=-1, keepdims=True) + 1e-6) * sg_ref[...]
        o_ref[:, cols] = nrm * one_minus_li


def _attn(cq, ck, zc, row0, n_seq, seq_len, lam, subln_g, one_minus_li, prev, ctx_k=None, ctx_v=None):
    nq = seq_len // LANES
    qb0 = row0 // LANES
    sb0 = row0 // seq_len
    in_specs = [pl.BlockSpec((LANES, HALF), lambda s, q: (qb0 + s * nq + q, 0)),
                pl.BlockSpec((seq_len, HALF), lambda s, q: (sb0 + s, 0)),
                pl.BlockSpec((seq_len, HALF), lambda s, q: (sb0 + s, 2))]
    args = [cq, ck, zc]
    if ctx_k is not None:
        in_specs += [pl.BlockSpec((1, PAST, HALF), lambda s, q: (s, 0, 0))] * 2
        args += [ctx_k, ctx_v]
    in_specs += [_full((1, 1)), _full((1, LANES))]
    args += [lam.reshape(1, 1), subln_g.reshape(1, LANES)]
    in_specs.append(pl.BlockSpec(memory_space=pl.ANY))
    args.append(prev)
    aliases = {len(args) - 1: 0}
    return pl.pallas_call(
        functools.partial(_attn_kernel, has_ctx=ctx_k is not None, one_minus_li=one_minus_li),
        out_shape=jax.ShapeDtypeStruct((T, HALF), F32),
        grid=(n_seq, nq), in_specs=in_specs,
        out_specs=pl.BlockSpec((LANES, HALF), lambda s, q: (qb0 + s * nq + q, 0)),
        input_output_aliases=aliases,
        compiler_params=_params(("arbitrary", "arbitrary"), 48),
        name="diff_attn",
    )(*args)


_LOG_GAMMA = tuple(tuple(float(np.log1p(-np.exp2(-np.float32(e)), dtype=np.float32)) for e in es)
                   for es in RET_EXP)


def _ret_chunks(dirs):
    C = RET_CHUNK
    ii = lax.broadcasted_iota(I32, (C, C), 0)
    jj = lax.broadcasted_iota(I32, (C, C), 1)
    ri = lax.broadcasted_iota(I32, (C, 1), 0)
    lane = lax.broadcasted_iota(I32, (C, LANES), 1)
    chains = []
    for rev, qk_ref, v_ref, s_ref, o_ref in dirs:
        mask = (jj > ii) if rev else (jj <= ii)
        dist = jnp.where(mask, (jj - ii) if rev else (ii - jj), 0).astype(F32)
        kpow = (ri if rev else (C - 1 - ri)).astype(F32)
        qpow = ((C - ri) if rev else (ri + 1)).astype(F32)
        for h in range(4):
            lg = _LOG_GAMMA[1 if rev else 0][h]
            p = h // 2
            hm = (lane < HEAD) if h % 2 == 0 else (lane >= HEAD)
            qp = jnp.where(hm, qk_ref[:, p * LANES:(p + 1) * LANES], 0.0)
            kp = jnp.where(hm, qk_ref[:, HALF // 2 + p * LANES:HALF // 2 + (p + 1) * LANES], 0.0)
            chains.append(dict(
                h=h, lg=lg, s_ref=s_ref, o_ref=o_ref, q16=qp.astype(BF16), k16=kp.astype(BF16),
                qw16=(qp * jnp.exp(lg * qpow)).astype(BF16), kw16=(kp * jnp.exp(lg * kpow)).astype(BF16),
                v16=v_ref[:, h * LANES:(h + 1) * LANES].astype(BF16),
                decay=jnp.where(mask, jnp.exp(lg * dist), 0.0)))
    for ch in chains:
        ch["sc"] = (lax.dot_general(ch["q16"], ch["k16"], NT_DIMS, preferred_element_type=F32)
                    * ch["decay"]).astype(BF16)
        ch["s"] = ch["s_ref"][ch["h"]]
    for ch in chains:
        ch["o"] = (jnp.dot(ch["sc"], ch["v16"], preferred_element_type=F32)
                   + jnp.dot(ch["qw16"], ch["s"].astype(BF16), preferred_element_type=F32))
        ch["kv"] = lax.dot_general(ch["kw16"], ch["v16"], TN, preferred_element_type=F32)
    for ch in chains:
        h = ch["h"]
        ch["o_ref"][:, h * LANES:(h + 1) * LANES] = ch["o"]
        ch["s_ref"][h] = math.exp(ch["lg"] * C) * ch["s"] + ch["kv"]


def _ret_kernel(bf_ref, bb_ref, first_ref, last_ref, seq_ref,
                qkf_ref, vf_ref, qkb_ref, vb_ref, s0_ref, of_ref, ob_ref, sfin_ref, s_ref):
    step = pl.program_id(0)

    @pl.when(first_ref[step] == 1)
    def _():
        s_ref[...] = s0_ref[:, 0]

    C = RET_CHUNK
    for sub in range(RET_STEP // C):
        f = pl.ds(sub * C, C)
        b = pl.ds(RET_STEP - (sub + 1) * C, C)
        _ret_chunks([(False, qkf_ref.at[f], vf_ref.at[f], s_ref.at[0], of_ref.at[f]),
                     (True, qkb_ref.at[b], vb_ref.at[b], s_ref.at[1], ob_ref.at[b])])

    @pl.when(last_ref[step] == 1)
    def _():
        sfin_ref[:, 0] = s_ref[...]


def _retention(rqk, zr, s0):
    C = RET_STEP
    tabs = _seq_tables(C)
    st = pl.BlockSpec((2, 1, 4, LANES, LANES), lambda i, bf, bb, fi, la, sq: (0, sq[i], 0, 0, 0))
    spec = lambda use_b, col: pl.BlockSpec(
        (C, HALF), lambda i, bf, bb, fi, la, sq: ((bb if use_b else bf)[i], col))
    return pl.pallas_call(
        _ret_kernel,
        out_shape=[jax.ShapeDtypeStruct((T, HALF), F32), jax.ShapeDtypeStruct((T, HALF), F32),
                   jax.ShapeDtypeStruct((2, N_SEQ, 4, LANES, LANES), F32)],
        grid_spec=pltpu.PrefetchScalarGridSpec(
            num_scalar_prefetch=5, grid=(len(tabs[0]),),
            in_specs=[spec(False, 0), spec(False, 1), spec(True, 0), spec(True, 1), st],
            out_specs=[spec(False, 0), spec(True, 0), st],
            scratch_shapes=[pltpu.VMEM((2, 4, LANES, LANES), F32)]),
        compiler_params=_params(("arbitrary",)),
        name="retention",
    )(*tabs, rqk, zr, rqk, zr, s0)


def _out_kernel(*refs, even, n_x):
    x_refs, refs = refs[:n_x], refs[n_x:]
    if even:
        (a_ref, yf_ref, yb_ref, bonus_ref, g_ref, gng_ref, gnb_ref, ones_ref,
         mod_ref, ng_ref, wo_ref, rw_ref, rb_ref,
         y_ref, xp_ref, ti_ref, tg_ref, rk_ref, cnt_ref, run_ref) = refs
        ones_bd = ones_ref[...]
        ys = yf_ref[...] + yb_ref[...]
        mu = _group_sum(ys, ones_bd) * (1.0 / HEAD)
        dv = ys - mu
        var = _group_sum(dv * dv, ones_bd) * (1.0 / HEAD)
        yn = dv * lax.rsqrt(var + RWKV_GN_EPS) * gng_ref[...] + gnb_ref[...]
        left = a_ref[...]
        right = (yn + bonus_ref[...]) * g_ref[...]
    else:
        (c_ref, of_ref, ob_ref, rg_ref, gng_ref,
         mod_ref, ng_ref, wo_ref, rw_ref, rb_ref,
         y_ref, xp_ref, ti_ref, tg_ref, rk_ref, cnt_ref, run_ref) = refs
        left = c_ref[...]
        rg = rg_ref[...]
        gate = rg * jax.nn.sigmoid(rg)
        os_ = of_ref[...] + ob_ref[...]
        parts = []
        for h in range(4):
            oh = os_[:, h * LANES:(h + 1) * LANES]
            mu = jnp.mean(oh, axis=-1, keepdims=True)
            dv = oh - mu
            var = jnp.mean(dv * dv, axis=-1, keepdims=True)
            parts.append(dv * lax.rsqrt(var + 1e-5))
        right = gate * (jnp.concatenate(parts, axis=1) * gng_ref[...])
    mod = mod_ref[0]
    o = (jnp.dot(left.astype(BF16), wo_ref[0:HALF, :], preferred_element_type=F32)
         + jnp.dot(right.astype(BF16), wo_ref[HALF:2 * HALF, :], preferred_element_type=F32))
    y = _token_rows(x_refs) + mod[:, 2 * D:3 * D] * o
    y_ref[...] = y
    yn2 = y * lax.rsqrt(jnp.mean(y * y, axis=-1, keepdims=True) + 1e-6) * ng_ref[...]
    t = yn2 * (1.0 + mod[:, 4 * D:5 * D]) + mod[:, 3 * D:4 * D]
    xp_ref[...] = t
    logits = _mm(t, rw_ref[...], NN, 3) + rb_ref[...]
    lane = lax.broadcasted_iota(I32, (TM, LANES), 1)
    neg = jnp.float32(-jnp.inf)
    lg = jnp.where(lane < N_EXPERTS, logits, neg)
    vals, hits = [], []
    for _ in range(TOP_K):
        m = jnp.max(lg, axis=-1, keepdims=True)
        ix = jnp.min(jnp.where(lg == m, lane, LANES), axis=-1, keepdims=True)
        hit = lane == ix
        vals.append(m)
        hits.append((ix, hit))
        lg = jnp.where(hit, neg, lg)
    es = [jnp.exp(vv - vals[0]) for vv in vals]
    den = es[0] + es[1] + es[2] + es[3]

    @pl.when(pl.program_id(0) == 0)
    def _():
        run_ref[...] = jnp.zeros_like(run_ref)

    member = jnp.zeros((TM, LANES), F32)
    for _, hit in hits:
        member = member + jnp.where(hit, 1.0, 0.0)
    ri = lax.broadcasted_iota(I32, (TM, TM), 0)
    ci = lax.broadcasted_iota(I32, (TM, TM), 1)
    before = jnp.where(ci < ri, 1.0, 0.0).astype(BF16)
    seen = run_ref[...] + jnp.dot(before, member.astype(BF16), preferred_element_type=F32)
    ti = jnp.zeros((TM, LANES), I32)
    tg = jnp.zeros((TM, LANES), F32)
    rk = jnp.zeros((TM, LANES), F32)
    for kk, (ix, hit) in enumerate(hits):
        ti = jnp.where(lane == kk, ix, ti)
        tg = jnp.where(lane == kk, es[kk] / den, tg)
        rk = jnp.where(lane == kk, jnp.sum(jnp.where(hit, seen, 0.0), axis=-1, keepdims=True), rk)
    ti_ref[...] = ti
    tg_ref[...] = tg
    rk_ref[...] = rk.astype(I32)
    run_ref[...] = run_ref[...] + jnp.sum(member, axis=0, keepdims=True)
    cnt_ref[...] = run_ref[...]


def _out_proj(even, mix_args, mix_specs, x, mod, norm_g, w_out_bf16, rw_pad, rb_pad):
    row = lambda i: (i, 0)
    modspec = pl.BlockSpec((1, 1, N_MOD * D), lambda i: (_group(i), 0, 0))
    x_specs, x_args = _token_specs(x)
    in_specs = x_specs + list(mix_specs) + [modspec, _full((1, D)), _full((D, D)), _full((D, LANES)),
                                            _full((1, LANES))]
    args = x_args + list(mix_args) + [mod, norm_g.reshape(1, D), w_out_bf16, rw_pad, rb_pad]
    lane_i = jax.ShapeDtypeStruct((T, LANES), I32)
    lane_spec = pl.BlockSpec((TM, LANES), row)
    return pl.pallas_call(
        functools.partial(_out_kernel, even=even, n_x=len(x_args)),
        out_shape=[jax.ShapeDtypeStruct((T, D), F32), jax.ShapeDtypeStruct((T, D), F32),
                   lane_i, jax.ShapeDtypeStruct((T, LANES), F32), lane_i,
                   jax.ShapeDtypeStruct((1, LANES), F32)],
        grid=(NT,), in_specs=in_specs,
        out_specs=[pl.BlockSpec((TM, D), row), pl.BlockSpec((TM, D), row),
                   lane_spec, lane_spec, lane_spec, _full((1, LANES))],
        scratch_shapes=[pltpu.VMEM((1, LANES), F32)],
        compiler_params=_params(("arbitrary",), 48),
        name="out_proj",
    )(*args)


def _route_kernel(cnt_ref, ti_ref, rk_ref, dest_ref, te_ref, nt_ref):
    cnt = cnt_ref[...].astype(I32)
    ntile = lax.shift_right_logical(cnt + (TMX - 1), TMX.bit_length() - 1)
    ei = lax.broadcasted_iota(I32, (LANES, LANES), 0)
    ej = lax.broadcasted_iota(I32, (LANES, LANES), 1)
    upto = jnp.where(ei <= ej, 1.0, 0.0).astype(BF16)
    ntile_f = jnp.broadcast_to(ntile.astype(F32), (8, LANES))
    tile_end = jnp.dot(ntile_f.astype(BF16), upto, preferred_element_type=F32)[0:1, :]
    row_start = (tile_end - ntile.astype(F32)) * float(TMX)
    lane = lax.broadcasted_iota(I32, (TM, LANES), 1)
    ti = ti_ref[...]
    rk = rk_ref[...]
    spread = jnp.zeros((TM, LANES), F32)
    for k in range(TOP_K):
        hit = lane == ti[:, k:k + 1]
        start = jnp.sum(jnp.where(hit, row_start, 0.0), axis=-1, keepdims=True)
        spread = jnp.where((lane & (TOP_K - 1)) == k, start + rk[:, k:k + 1].astype(F32), spread)
    tok = lax.broadcasted_iota(I32, (TM, LANES), 0)
    keep = (tok & (LANES // TOP_K - 1)) == lax.shift_right_logical(lane, 2)
    flat = jnp.where(keep, spread, 0.0).reshape(TM * TOP_K // LANES, LANES // TOP_K, LANES).sum(axis=1)
    dest_ref[...] = flat.astype(I32)

    @pl.when(pl.program_id(0) == 0)
    def _():
        lane1 = lax.broadcasted_iota(I32, (1, LANES), 1)
        n_tiles = jnp.max(tile_end, axis=-1, keepdims=True)
        last_e = jnp.max(jnp.where(cnt > 0, lane1, 0), axis=-1, keepdims=True)
        tile = lax.broadcasted_iota(I32, (TM, 1), 0).astype(F32)
        te = jnp.sum(jnp.where(tile_end <= tile, 1, 0), axis=-1, keepdims=True)
        te = jnp.where(tile < n_tiles, te, last_e)
        te_ref[...] = jnp.broadcast_to(te, (TM, LANES)).astype(I32)
        first_row = lax.broadcasted_iota(I32, (8, LANES), 0) == 0
        nt_ref[...] = jnp.where(first_row, n_tiles, tile_end).astype(I32)


def _route(cnt, ti, rk):
    row = lambda i: (i, 0)
    return pl.pallas_call(
        _route_kernel,
        out_shape=[jax.ShapeDtypeStruct((N_ASSIGN // LANES, LANES), I32), jax.ShapeDtypeStruct((TM, LANES), I32),
                   jax.ShapeDtypeStruct((8, LANES), I32)],
        grid=(NT,),
        in_specs=[_full((1, LANES)), pl.BlockSpec((TM, LANES), row), pl.BlockSpec((TM, LANES), row)],
        out_specs=[pl.BlockSpec((TM * TOP_K // LANES, LANES), row), _full((TM, LANES)), _full((8, LANES))],
        compiler_params=_params(("arbitrary",)),
        name="moe_route",
    )(cnt, ti, rk)


def _start_row_copy(src_ref, src_row, dst_ref, dst_row, sem, queue):
    pltpu.async_copy(src_ref.at[pl.ds(src_row, 1)], dst_ref.at[pl.ds(dst_row, 1)], sem, priority=queue)


def _wait_tiles(n, src_ref, dst_ref, sem):
    for _ in range(n):
        pltpu.make_async_copy(src_ref, dst_ref, sem).wait()


def _dispatch_kernel(dest_ref, tend_ref, x_ref, xs_ref, zero_ref, sem):
    i = pl.program_id(0)

    @pl.when(i == 0)
    def _():
        zero_ref[...] = jnp.zeros_like(zero_ref)

        def last_tile(e, fn):
            end = tend_ref[e]
            begin = tend_ref[e - 1] if e > 0 else 0

            @pl.when(end > begin)
            def _():
                fn(pltpu.make_async_copy(zero_ref, xs_ref.at[pl.ds((end - 1) * TMX, TMX)], sem))

        def unused_tile(j):
            return pltpu.make_async_copy(zero_ref, xs_ref.at[pl.ds(j * TMX, TMX)], sem)

        def start_unused(j, carry):
            unused_tile(j).start()
            return carry

        def wait_unused(j, carry):
            unused_tile(j).wait()
            return carry

        n_used = tend_ref[N_EXPERTS - 1]
        for e in range(N_EXPERTS):
            last_tile(e, lambda c: c.start())
        lax.fori_loop(n_used, MOE_TILES, start_unused, 0)
        for e in range(N_EXPERTS):
            last_tile(e, lambda c: c.wait())
        lax.fori_loop(n_used, MOE_TILES, wait_unused, 0)

    base = i * (TM * TOP_K)

    def start(r, carry):
        for k in range(TOP_K):
            _start_row_copy(x_ref, r, xs_ref, dest_ref[base + r * TOP_K + k], sem, k % 2)
        return carry

    lax.fori_loop(0, TM, start, 0, unroll=4)
    _wait_tiles(TOP_K, x_ref, xs_ref.at[pl.ds(0, TM)], sem)


def _dispatch(dest_flat, tile_end, xt):
    return pl.pallas_call(
        _dispatch_kernel,
        out_shape=jax.ShapeDtypeStruct((R_PAD, D), F32),
        grid_spec=pltpu.PrefetchScalarGridSpec(
            num_scalar_prefetch=2, grid=(NT,),
            in_specs=[pl.BlockSpec((TM, D), lambda i, d, te: (i, 0))],
            out_specs=pl.BlockSpec(memory_space=pl.ANY),
            scratch_shapes=[pltpu.VMEM((TMX, D), F32), pltpu.SemaphoreType.DMA(())]),
        compiler_params=_params(("arbitrary",)),
        name="moe_dispatch",
    )(dest_flat, tile_end, xt)


W_PARTS = 8


def _expert_weights(i, nt, te_ref, w_ref, wbuf_ref, wsem, w16_ref, group_ref):
    rows = w_ref.shape[1] // W_PARTS

    def fetch(e, buf):
        return [pltpu.make_async_copy(w_ref.at[e, pl.ds(p * rows, rows)], wbuf_ref.at[buf, pl.ds(p * rows, rows)],
                                      wsem.at[buf]) for p in range(W_PARTS)]

    @pl.when(i == 0)
    def _():
        group_ref[0] = 0
        for c in fetch(te_ref[0], 0):
            c.start()

    first = jnp.logical_or(i == 0, te_ref[i] != te_ref[jnp.maximum(i - 1, 0)])

    @pl.when(jnp.logical_and(first, i < nt))
    def _():
        cur = group_ref[0] % 2
        nxt = lax.while_loop(
            lambda j: jnp.logical_and(j < nt, te_ref[jnp.minimum(j, MOE_TILES - 1)] == te_ref[i]),
            lambda j: j + 1, i + 1)

        @pl.when(nxt < nt)
        def _():
            for c in fetch(te_ref[jnp.minimum(nxt, MOE_TILES - 1)], 1 - cur):
                c.start()

        for c in fetch(0, cur):
            c.wait()
        w16_ref[...] = wbuf_ref[cur].astype(BF16)
        group_ref[0] = group_ref[0] + 1


def _experts_kernel(te_ref, nt_ref, xs_ref, wgu_ref, bgu_ref, wdn_ref, bdn_ref, y_ref,
                    gu_buf, gu_sem, gu16_ref, gu_group, dn_buf, dn_sem, dn16_ref, dn_group):
    i = pl.program_id(0)
    nt = nt_ref[0]
    _expert_weights(i, nt, te_ref, wgu_ref, gu_buf, gu_sem, gu16_ref, gu_group)
    _expert_weights(i, nt, te_ref, wdn_ref, dn_buf, dn_sem, dn16_ref, dn_group)

    @pl.when(i < nt)
    def _():
        x16 = xs_ref[...].astype(BF16)
        y = bdn_ref[0]
        for h in range(2):
            gc = slice(h * HALF, (h + 1) * HALF)
            uc = slice(D + h * HALF, D + (h + 1) * HALF)
            g = jnp.dot(x16, gu16_ref[:, gc], preferred_element_type=F32) + bgu_ref[0, :, gc]
            u = jnp.dot(x16, gu16_ref[:, uc], preferred_element_type=F32) + bgu_ref[0, :, uc]
            gt = jnp.minimum(g, SWIGLU_LIMIT)
            up = jnp.clip(u, -SWIGLU_LIMIT, SWIGLU_LIMIT)
            act = ((up + 1.0) * gt * jax.nn.sigmoid(SWIGLU_ALPHA * gt)).astype(BF16)
            y = y + jnp.dot(act, dn16_ref[gc, :], preferred_element_type=F32)
        y_ref[...] = y

    @pl.when(i >= nt)
    def _():
        y_ref[...] = jnp.zeros_like(y_ref)


def _tile_clamped(i, te, nt):
    return (jnp.minimum(i, jnp.maximum(nt[0] - 1, 0)), 0)


def _weight_scratch(n_out):
    return [pltpu.VMEM((2, D, n_out), F32), pltpu.SemaphoreType.DMA((2,)), pltpu.VMEM((D, n_out), BF16),
            pltpu.SMEM((1,), I32)]


def _experts(te, n_tiles, xs, w_gu, b_gu, w_dn, b_dn):
    return pl.pallas_call(
        _experts_kernel,
        out_shape=jax.ShapeDtypeStruct((R_PAD, D), F32),
        grid_spec=pltpu.PrefetchScalarGridSpec(
            num_scalar_prefetch=2, grid=(MOE_TILES,),
            in_specs=[pl.BlockSpec((TMX, D), _tile_clamped), pl.BlockSpec(memory_space=pl.ANY),
                      pl.BlockSpec((1, 1, 2 * D), lambda i, te, nt: (te[i], 0, 0)),
                      pl.BlockSpec(memory_space=pl.ANY),
                      pl.BlockSpec((1, 1, D), lambda i, te, nt: (te[i], 0, 0))],
            out_specs=pl.BlockSpec((TMX, D), lambda i, te, nt: (i, 0)),
            scratch_shapes=_weight_scratch(2 * D) + _weight_scratch(D)),
        compiler_params=_params(("arbitrary",), 58),
        name="moe_experts",
    )(te, n_tiles, xs, w_gu, b_gu, w_dn, b_dn)


def _combine_kernel(dest_ref, x_ref, tg_ref, mod_ref, ys_ref, *rest):
    *o_refs, buf_ref, sem = rest
    i = pl.program_id(0)
    slot = i % 2

    def gather(tile, b):
        base = tile * (TM * TOP_K)

        def body(r, carry):
            for k in range(TOP_K):
                _start_row_copy(ys_ref, dest_ref[base + r * TOP_K + k], buf_ref.at[b, k], r, sem.at[b], k % 2)
            return carry

        lax.fori_loop(0, TM, body, 0, unroll=4)

    @pl.when(i == 0)
    def _():
        gather(0, 0)

    @pl.when(i + 1 < NT)
    def _():
        gather(i + 1, 1 - slot)

    _wait_tiles(TOP_K, ys_ref.at[pl.ds(0, TM)], buf_ref.at[slot, 0], sem.at[slot])
    tg = tg_ref[...]
    f = tg[:, 0:1] * buf_ref[slot, 0]
    for k in range(1, TOP_K):
        f = f + tg[:, k:k + 1] * buf_ref[slot, k]
    res = x_ref[...] + mod_ref[0][:, 5 * D:6 * D] * f
    if len(o_refs) == 1:
        o_refs[0][...] = res
    else:
        @pl.when(i < PROMPT_TILES)
        def _():
            o_refs[0][...] = res

        @pl.when(i >= PROMPT_TILES)
        def _():
            o_refs[1][...] = res


def _combine(dest_flat, x, tg, mod, ys, split=False):
    row = lambda i, d: (i, 0)
    if split:
        out_shape = [jax.ShapeDtypeStruct((T_PROMPT, D), F32), jax.ShapeDtypeStruct((T_SAMPLE, D), F32)]
        out_specs = [pl.BlockSpec((TM, D), lambda i, d: (jnp.minimum(i, PROMPT_TILES - 1), 0)),
                     pl.BlockSpec((TM, D), lambda i, d: (jnp.maximum(i - PROMPT_TILES, 0), 0))]
    else:
        out_shape = jax.ShapeDtypeStruct((T, D), F32)
        out_specs = pl.BlockSpec((TM, D), row)
    return pl.pallas_call(
        _combine_kernel,
        out_shape=out_shape,
        grid_spec=pltpu.PrefetchScalarGridSpec(
            num_scalar_prefetch=1, grid=(NT,),
            in_specs=[pl.BlockSpec((TM, D), row), pl.BlockSpec((TM, LANES), row),
                      pl.BlockSpec((1, 1, N_MOD * D), lambda i, d: (_group(i), 0, 0)),
                      pl.BlockSpec(memory_space=pl.ANY)],
            out_specs=out_specs,
            scratch_shapes=[pltpu.VMEM((2, TOP_K, TM, D), F32), pltpu.SemaphoreType.DMA((2,))]),
        compiler_params=_params(("arbitrary",), 40),
        name="moe_combine",
    )(dest_flat, x, tg, mod, ys)


def _moe(layer, y, xt, ti, tg, rk, cnt, mod, w_gu, b_gu, w_dn, b_dn):
    dest, te, nt = _route(cnt, ti, rk)
    dest_flat = dest.reshape(-1)
    te = te[:MOE_TILES, 0] + layer * N_EXPERTS
    n_tiles = nt[0, :1]
    xs = _dispatch(dest_flat, nt[1, :N_EXPERTS], xt)
    n_all = w_gu.shape[0] * N_EXPERTS
    ys = _experts(te, n_tiles, xs, w_gu.reshape(n_all, D, 2 * D), b_gu.reshape(n_all, 1, 2 * D),
                  w_dn.reshape(n_all, D, D), b_dn.reshape(n_all, 1, D))
    return _combine(dest_flat, y, tg, mod, ys, split=layer == 1)


def _ones_blockdiag():
    idx = np.arange(HALF) // HEAD
    return jnp.asarray((idx[:, None] == idx[None, :]).astype(np.float32), dtype=BF16)


def _rope_tables():
    pos = jnp.arange(L_SAMPLE)
    rowp = (pos // 64).astype(F32)
    colp = (pos % 64).astype(F32)
    nf = HEAD // 4
    inv = jnp.power(10000.0, -jnp.arange(nf, dtype=F32) / nf)
    ar = rowp[:, None] * inv[None, :]
    ac = colp[:, None] * inv[None, :]
    cos64 = jnp.concatenate([jnp.cos(ar), jnp.cos(ar), jnp.cos(ac), jnp.cos(ac)], axis=1)
    sin64 = jnp.concatenate([-jnp.sin(ar), jnp.sin(ar), -jnp.sin(ac), jnp.sin(ac)], axis=1)
    cos = jnp.tile(cos64, (1, HALF // HEAD))
    sin = jnp.tile(sin64, (1, HALF // HEAD))
    ident = jnp.ones((TM, HALF), F32)
    return (jnp.concatenate([ident, cos], axis=0), jnp.concatenate([jnp.zeros((TM, HALF), F32), sin], axis=0))


def _bd_pairs(s):
    lead = s.shape[:-3]
    s = s.reshape(lead + (4, 2, HEAD, HEAD))
    z = jnp.zeros_like(s[..., 0, :, :])
    top = jnp.concatenate([s[..., 0, :, :], z], axis=-1)
    bot = jnp.concatenate([z, s[..., 1, :, :]], axis=-1)
    return jnp.concatenate([top, bot], axis=-2)


def _bd_unpairs(s):
    a = s[..., 0:HEAD, 0:HEAD]
    b = s[..., HEAD:, HEAD:]
    out = jnp.stack([a, b], axis=-3)
    return out.reshape(s.shape[:-3] + (8, HEAD, HEAD))


def kernel(x_prompt, x_sample, state_rwkv, cache_k_diff, cache_v_diff, state_retention, c, c_ctx, norm_g, ada_w, ada_b, e_w_in, e_w_out, sgu_ln_g, sgu_w, sgu_b, rw_mu, rw_w0, rw_w_up, rw_a0, rw_a_up, rw_g_up, rw_k_k, rw_k_a, rw_r_k, rw_gn_g, rw_gn_b, o_w_in, o_w_out, da_qk_g, da_lam, da_subln_g, ret_gn_g, router_w, router_b, ex_w_gu, ex_b_gu, ex_w_dn, ex_b_dn):
    x = (x_prompt.reshape(T_PROMPT, D), x_sample.reshape(T_SAMPLE, D))
    cvec8 = jnp.concatenate([c_ctx[None, :], c, jnp.zeros((3, D), F32)], axis=0)
    mods = _adaln(cvec8, ada_w, ada_b)
    mod0 = mods[0].reshape(8, 1, N_MOD * D)
    mod1 = mods[1].reshape(8, 1, N_MOD * D)
    ones_bd = _ones_blockdiag()
    rw_pad = jnp.pad(router_w, ((0, 0), (0, 0), (0, LANES - N_EXPERTS)))
    rb_pad = jnp.pad(router_b, ((0, 0), (0, LANES - N_EXPERTS))).reshape(2, 1, LANES)
    row = lambda i: (i, 0)
    half = pl.BlockSpec((TM, HALF), row)

    bs_full = jnp.repeat(sgu_b[0].T, HEAD, axis=1)
    a_out, zb = _in_proj(x, norm_g[0, 0], mod0, e_w_in[0].astype(BF16), (2 * HALF, B_COLS),
                         sgu=(sgu_ln_g[0], sgu_w[0].astype(BF16), bs_full))
    zpad = jnp.zeros((2, HEAD, HALF), F32)
    wup_pad = jnp.concatenate([rw_w_up[0], zpad], axis=1).astype(BF16)
    aup_pad = jnp.concatenate([zpad, rw_a_up[0]], axis=1).astype(BF16)
    r, v, kkn, bonus, g, lw, kt, b = _rwkv_prep(zb, rw_mu[0], rw_k_k[0], rw_k_a[0], rw_r_k[0], rw_w0[0], rw_a0[0],
                                                wup_pad, aup_pad, rw_g_up[0].astype(BF16), ones_bd)
    s0_sample = _bd_pairs(jnp.moveaxis(state_rwkv[:, 0], 1, 0))
    s0_rw = jnp.concatenate([jnp.zeros((2, N_PROMPT, 4, LANES, LANES), F32), s0_sample], axis=1)
    yf_rw, yb_rw, sfin_rw = _rwkv_scan(r, v, kkn, lw, kt, b, s0_rw)
    new_rwkv = jnp.moveaxis(_bd_unpairs(sfin_rw[:, :N_PROMPT]), 0, 1)[:, None]
    y0, xp0, ti0, tg0, rk0, cnt0 = _out_proj(
        True,
        [a_out, yf_rw, yb_rw, bonus, g, rw_gn_g[0].reshape(1, HALF), rw_gn_b[0].reshape(1, HALF), ones_bd],
        [half, half, half, half, half, _full((1, HALF)), _full((1, HALF)), _full((HALF, HALF))],
        x, mod0, norm_g[0, 1], e_w_out[0].astype(BF16), rw_pad[0], rb_pad[0])
    x1 = _moe(0, y0, xp0, ti0, tg0, rk0, cnt0, mod0, ex_w_gu, ex_b_gu, ex_w_dn, ex_b_dn)

    zc, zr = _in_proj(x1, norm_g[1, 0], mod1, o_w_in[0].astype(BF16), (3 * HALF, 3 * HALF))
    cos_tab, sin_tab = _rope_tables()
    qkg = jnp.tile(da_qk_g[0], (1, HALF // HEAD))
    cq, ck, ck_raw, rqk = _odd_prep(zc, zr, cos_tab, sin_tab, qkg, ones_bd)
    lambda_init = 0.8 - 0.6 * math.exp(-0.3 * 1)
    lv = da_lam[0]
    lam = jnp.exp(jnp.sum(lv[0] * lv[1])) - jnp.exp(jnp.sum(lv[2] * lv[3])) + lambda_init
    c_out = _attn(cq, ck, zc, 0, N_PROMPT, L_PROMPT, lam, da_subln_g[0], 1.0 - lambda_init,
                  jnp.zeros((T, HALF), F32))
    ctx_k = cache_k_diff[:, 0].reshape(N_SAMPLE, PAST, HALF)
    ctx_v = cache_v_diff[:, 0].reshape(N_SAMPLE, PAST, HALF)
    c_out = _attn(cq, ck, zc, T_PROMPT, N_SAMPLE, L_SAMPLE, lam, da_subln_g[0], 1.0 - lambda_init,
                  c_out, ctx_k, ctx_v)
    sr = jnp.moveaxis(state_retention[:, 0], 1, 0)
    zr0 = jnp.zeros_like(sr)
    s0_sample = jnp.stack([jnp.concatenate([sr[:, :, 0], zr0[:, :, 0]], axis=-2),
                           jnp.concatenate([zr0[:, :, 1], sr[:, :, 1]], axis=-2),
                           jnp.concatenate([sr[:, :, 2], zr0[:, :, 2]], axis=-2),
                           jnp.concatenate([zr0[:, :, 3], sr[:, :, 3]], axis=-2)], axis=2)
    s0_ret = jnp.concatenate([jnp.zeros((2, N_PROMPT, 4, LANES, LANES), F32), s0_sample], axis=1)
    of_ret, ob_ret, rfin = _retention(rqk, zr, s0_ret)
    rfin_p = rfin[:, :N_PROMPT]
    new_ret = jnp.stack([rfin_p[:, :, 0, 0:HEAD], rfin_p[:, :, 1, HEAD:], rfin_p[:, :, 2, 0:HEAD],
                         rfin_p[:, :, 3, HEAD:]], axis=2)
    new_ret = jnp.moveaxis(new_ret, 0, 1)[:, None]
    y1, xp1, ti1, tg1, rk1, cnt1 = _out_proj(
        False,
        [c_out, of_ret, ob_ret, zr, ret_gn_g[0].reshape(1, HALF)],
        [half, half, half, pl.BlockSpec((TM, HALF), lambda i: (i, 2)), _full((1, HALF))],
        x1, mod1, norm_g[1, 1], o_w_out[0].astype(BF16), rw_pad[1], rb_pad[1])
    y_prompt, y_sample = _moe(1, y1, xp1, ti1, tg1, rk1, cnt1, mod1, ex_w_gu, ex_b_gu, ex_w_dn, ex_b_dn)

    new_k = ck_raw[:T_PROMPT].reshape(N_PROMPT, 1, L_PROMPT, 4, LANES)
    new_v = zc[:T_PROMPT, 2 * HALF:3 * HALF].reshape(N_PROMPT, 1, L_PROMPT, 4, LANES)
    return (y_prompt.reshape(N_PROMPT, L_PROMPT, D), y_sample.reshape(N_SAMPLE, L_SAMPLE, D),
            new_rwkv, new_k, new_v, new_ret)
```

```python
import functools
import math

import numpy as np
import jax
import jax.numpy as jnp
from jax import lax
from jax.experimental import pallas as pl
from jax.experimental.pallas import tpu as pltpu

F32 = jnp.float32
BF16 = jnp.bfloat16
I32 = jnp.int32

D = 1024
N_PROMPT, L_PROMPT = 16, 256
N_SAMPLE, L_SAMPLE = 4, 1024
N_SEQ = N_PROMPT + N_SAMPLE
PAST = 256
T_PROMPT = N_PROMPT * L_PROMPT
T_SAMPLE = N_SAMPLE * L_SAMPLE
T = T_PROMPT + T_SAMPLE
TM = 256
NT = T // TM
PROMPT_TILES = T_PROMPT // TM
TILES_PER_SAMPLE = L_SAMPLE // TM
N_MOD = 6
HALF = 512
B_COLS = 1792
HEAD = 64
W_DECAY_SCALE = math.exp(-0.5)
RWKV_GN_EPS = 64e-5
RW_CHUNK = 64
RW_STEP = 256
RET_CHUNK = 128
RET_STEP = 256
RET_EXP = ((5.0, 7.0, 9.0, 11.0), (6.0, 8.0, 10.0, 12.0))
N_EXPERTS = 32
TOP_K = 4
SWIGLU_LIMIT = 7.0
SWIGLU_ALPHA = 1.702
N_ASSIGN = T * TOP_K
TMX = 512
MOE_TILES = N_ASSIGN // TMX + N_EXPERTS
R_PAD = MOE_TILES * TMX
LANES = 128

NN = (((1,), (0,)), ((), ()))
NT_DIMS = (((1,), (1,)), ((), ()))
TN = (((0,), (0,)), ((), ()))


def _group(i):
    return jnp.where(i < PROMPT_TILES, 0, 1 + (i - PROMPT_TILES) // TILES_PER_SAMPLE)


def _mm(a, b, dims=NN, passes=1):
    dg = functools.partial(lax.dot_general, dimension_numbers=dims, preferred_element_type=F32)
    if passes == 1:
        return dg(a.astype(BF16), b.astype(BF16))
    a = a.astype(F32)
    b = b.astype(F32)
    ah = a.astype(BF16)
    al = (a - ah.astype(F32)).astype(BF16)
    bh = b.astype(BF16)
    free = 1 - dims[0][0][0]
    m = a.shape[free]
    both = dg(jnp.concatenate([ah, al], axis=free), bh)
    if passes == 2:
        return both[0:m] + both[m:2 * m]
    bl = (b - bh.astype(F32)).astype(BF16)
    return both[0:m] + (dg(ah, bl) + both[m:2 * m])


def _group_sum(x, ones_bd):
    xh = x.astype(BF16)
    xl = (x - xh.astype(F32)).astype(BF16)
    return (jnp.dot(xh, ones_bd, preferred_element_type=F32)
            + jnp.dot(xl, ones_bd, preferred_element_type=F32))


def _full(shape):
    nd = len(shape)
    return pl.BlockSpec(shape, lambda *_: (0,) * nd)


def _params(sem, vmem_mb=None):
    kw = {}
    if vmem_mb is not None:
        kw["vmem_limit_bytes"] = vmem_mb * 1024 * 1024
    return pltpu.CompilerParams(dimension_semantics=sem, **kw)


def _seq_tables(chunk):
    blk_f, blk_b, first, last, seq = [], [], [], [], []
    row = 0
    for s in range(N_SEQ):
        n = (L_PROMPT if s < N_PROMPT else L_SAMPLE) // chunk
        base = row // chunk
        for j in range(n):
            blk_f.append(base + j)
            blk_b.append(base + n - 1 - j)
            first.append(int(j == 0))
            last.append(int(j == n - 1))
            seq.append(s)
        row += n * chunk
    return tuple(np.asarray(a, np.int32) for a in (blk_f, blk_b, first, last, seq))


def _adaln_kernel(c_ref, w_ref, b_ref, o_ref):
    c = c_ref[...]
    s = c * jax.nn.sigmoid(c)
    o_ref[0] = _mm(s, w_ref[0], NN, 3) + b_ref[0]


def _adaln(cvec8, ada_w, ada_b):
    depth, _, n = ada_w.shape
    bn = 1536
    return pl.pallas_call(
        _adaln_kernel,
        out_shape=jax.ShapeDtypeStruct((depth, 8, n), F32),
        grid=(depth, n // bn),
        in_specs=[pl.BlockSpec((8, D), lambda l, j: (0, 0)),
                  pl.BlockSpec((1, D, bn), lambda l, j: (l, 0, j)),
                  pl.BlockSpec((1, 1, bn), lambda l, j: (l, 0, j))],
        out_specs=pl.BlockSpec((1, 8, bn), lambda l, j: (l, 0, j)),
        compiler_params=_params(("arbitrary", "arbitrary"), 40),
        name="adaln",
    )(cvec8, ada_w, ada_b.reshape(depth, 1, n))


def _token_specs(x):
    if isinstance(x, tuple):
        return ([pl.BlockSpec((TM, D), lambda i, *_: (jnp.minimum(i, PROMPT_TILES - 1), 0)),
                 pl.BlockSpec((TM, D), lambda i, *_: (jnp.maximum(i - PROMPT_TILES, 0), 0))], list(x))
    return [pl.BlockSpec((TM, D), lambda i, *_: (i, 0))], [x]


def _token_rows(x_refs):
    if len(x_refs) == 2:
        return jnp.where(pl.program_id(0) < PROMPT_TILES, x_refs[0][...], x_refs[1][...])
    return x_refs[0][...]


def _in_kernel(*refs, splits, n_x, gated):
    n_in = n_x + (6 if gated else 3)
    x_refs, (g_ref, mod_ref, w_ref), outs = refs[:n_x], refs[n_x:n_x + 3], refs[n_in:]
    x = _token_rows(x_refs)
    mod = mod_ref[0]
    y = x * lax.rsqrt(jnp.mean(x * x, axis=-1, keepdims=True) + 1e-6) * g_ref[...]
    h = (y * (1.0 + mod[:, D:2 * D]) + mod[:, 0:D]).astype(BF16)
    off = 0
    for idx, (o_ref, n) in enumerate(zip(outs, splits)):
        z = jnp.dot(h, w_ref[:, off:off + n], preferred_element_type=F32)
        if gated and idx == 0:
            _sgu(z, *refs[n_x + 3:n_in], o_ref)
        else:
            o_ref[...] = z
        off += n


def _in_proj(x, g, mod, w_bf16, splits, sgu=None):
    n = w_bf16.shape[1]
    row = lambda i: (i, 0)
    x_specs, x_args = _token_specs(x)
    out_w = list(splits)
    extra_specs, extra_args = [], []
    if sgu is not None:
        out_w[0] = HALF
        extra_specs = [_full((1, HALF)), _full((8, LANES, LANES)), _full((LANES, HALF))]
        extra_args = [sgu[0].reshape(1, HALF), sgu[1], sgu[2]]
    return pl.pallas_call(
        functools.partial(_in_kernel, splits=splits, n_x=len(x_args), gated=sgu is not None),
        out_shape=[jax.ShapeDtypeStruct((T, s), F32) for s in out_w],
        grid=(NT,),
        in_specs=x_specs + [_full((1, D)), pl.BlockSpec((1, 1, N_MOD * D), lambda i: (_group(i), 0, 0)),
                            _full((D, n))] + extra_specs,
        out_specs=[pl.BlockSpec((TM, s), row) for s in out_w],
        compiler_params=_params(("arbitrary",), 48),
        name="in_proj",
    )(*x_args, g.reshape(1, D), mod, w_bf16, *extra_args)


def _gelu(x):
    return 0.5 * x * (1.0 + lax.erf(x * (1.0 / math.sqrt(2.0))))


def _sgu(za, lng_ref, ws_ref, bs_ref, o_ref):
    u = _gelu(za[:, 0:HALF])
    va = _gelu(za[:, HALF:2 * HALF])
    mu = jnp.mean(va, axis=-1, keepdims=True)
    dv = va - mu
    var = jnp.mean(dv * dv, axis=-1, keepdims=True)
    vn = dv * lax.rsqrt(var + 1e-5) * lng_ref[...]
    lane = lax.broadcasted_iota(I32, (LANES, LANES), 1)
    first = lane < HEAD
    for c in range(TM // LANES):
        rows = slice(c * LANES, (c + 1) * LANES)
        for p in range(HALF // LANES):
            cols = slice(p * LANES, (p + 1) * LANES)
            vp = vn[rows, cols]
            s = (jnp.dot(ws_ref[2 * p], jnp.where(first, vp, 0.0).astype(BF16), preferred_element_type=F32)
                 + jnp.dot(ws_ref[2 * p + 1], jnp.where(first, 0.0, vp).astype(BF16), preferred_element_type=F32))
            o_ref[rows, cols] = u[rows, cols] * (s + bs_ref[:, cols])


def _rwkv_prep_kernel(zb_ref, zp_ref, zn_ref, mu_ref, kk_ref, ka_ref, rk_ref, w0_ref, a0_ref,
                      wup_ref, aup_ref, gup_ref, ones_ref,
                      r_ref, v_ref, kkn_ref, bonus_ref, g_ref, lw_ref, kt_ref, b_ref):
    i = pl.program_id(0)
    in_sample = i >= PROMPT_TILES
    pos = (i - PROMPT_TILES) % TILES_PER_SAMPLE
    is_first = jnp.logical_or(jnp.logical_not(in_sample), pos == 0)
    is_last = jnp.logical_or(jnp.logical_not(in_sample), pos == TILES_PER_SAMPLE - 1)
    zb = zb_ref[...]
    prev_row = jnp.where(is_first, 0.0, zp_ref[7:8, :])
    next_row = jnp.where(is_last, 0.0, zn_ref[0:1, :])
    rowid = lax.broadcasted_iota(I32, (TM, 1), 0)
    zp = jnp.where(rowid == 0, prev_row, pltpu.roll(zb, 1, 0))
    zn = jnp.where(rowid == TM - 1, next_row, pltpu.roll(zb, TM - 1, 0))
    zs = zb + mu_ref[0:1, :] * (zp - zb) + mu_ref[1:2, :] * (zn - zb)
    r = zs[:, 0:HALF]
    k = zs[:, HALF:2 * HALF]
    v = zs[:, 2 * HALF:3 * HALF]
    wa = zs[:, 3 * HALF:3 * HALF + LANES]
    gd = zs[:, 3 * HALF + LANES:B_COLS]
    ones_bd = ones_ref[...]
    r_ref[...] = r
    v_ref[...] = v
    g_ref[...] = jnp.dot(jax.nn.sigmoid(gd).astype(BF16), gup_ref[...], preferred_element_type=F32)
    kk = k * kk_ref[...]
    kkn = kk / jnp.maximum(jnp.sqrt(_group_sum(kk * kk, ones_bd)), 1e-6)
    kkn_ref[...] = kkn
    bonus_ref[...] = _group_sum(r * k * rk_ref[...], ones_bd) * v
    tw = jnp.tanh(wa).astype(BF16)
    wa16 = wa.astype(BF16)
    for dd in range(2):
        lw_ref[dd] = -W_DECAY_SCALE * jax.nn.sigmoid(
            w0_ref[dd:dd + 1, :] + jnp.dot(tw, wup_ref[dd], preferred_element_type=F32))
        a = jax.nn.sigmoid(a0_ref[dd:dd + 1, :] + jnp.dot(wa16, aup_ref[dd], preferred_element_type=F32))
        kt_ref[dd] = k * (1.0 + (a - 1.0) * ka_ref[...])
        b_ref[dd] = a * kkn


def _rwkv_prep(zb, mu, k_k, k_a, r_k, w0, a0, wup_pad, aup_pad, g_up, ones_bd):
    row = lambda i: (i, 0)
    halo = TM // 8
    one = jax.ShapeDtypeStruct((T, HALF), F32)
    two = jax.ShapeDtypeStruct((2, T, HALF), F32)
    o1 = pl.BlockSpec((TM, HALF), row)
    o2 = pl.BlockSpec((2, TM, HALF), lambda i: (0, i, 0))
    return pl.pallas_call(
        _rwkv_prep_kernel,
        out_shape=[one, one, one, one, one, two, two, two],
        grid=(NT,),
        in_specs=[pl.BlockSpec((TM, B_COLS), row),
                  pl.BlockSpec((8, B_COLS), lambda i: (jnp.maximum(i * halo - 1, 0), 0)),
                  pl.BlockSpec((8, B_COLS), lambda i: (jnp.minimum((i + 1) * halo, T // 8 - 1), 0)),
                  _full((2, B_COLS)), _full((1, HALF)), _full((1, HALF)), _full((1, HALF)),
                  _full((2, HALF)), _full((2, HALF)),
                  _full((2, LANES, HALF)), _full((2, LANES, HALF)), _full((LANES, HALF)),
                  _full((HALF, HALF))],
        out_specs=[o1, o1, o1, o1, o1, o2, o2, o2],
        compiler_params=_params(("arbitrary",), 48),
        name="rwkv_prep",
    )(zb, zb, zb, mu, k_k.reshape(1, HALF), k_a.reshape(1, HALF), r_k.reshape(1, HALF), w0, a0,
      wup_pad, aup_pad, g_up, ones_bd)


def _rwkv_chunks(dirs):
    C = RW_CHUNK
    ti = lax.broadcasted_iota(I32, (C, C), 0)
    tj = lax.broadcasted_iota(I32, (C, C), 1)
    bi = lax.broadcasted_iota(I32, (LANES, LANES), 0)
    bj = lax.broadcasted_iota(I32, (LANES, LANES), 1)
    same = (bi >> 6) == (bj >> 6)
    pi = bi & (C - 1)
    pj = bj & (C - 1)
    eye = (bi == bj).astype(F32)
    h0 = lax.broadcasted_iota(I32, (C, LANES), 1) < HEAD

    def stack(x):
        return jnp.concatenate([jnp.where(h0, x, 0.0), jnp.where(h0, 0.0, x)], axis=0)

    def fold(x):
        return x[0:C] + x[C:2 * C]

    chains = []
    for rev, r, v, kk, lw, kt, b, s_ref, y_ref in dirs:
        tri = jnp.where((tj >= ti) if rev else (tj <= ti), 1.0, 0.0).astype(F32)
        p1 = lw.astype(BF16)
        r1 = lw - p1.astype(F32)
        p2 = r1.astype(BF16)
        p3 = (r1 - p2.astype(F32)).astype(BF16)
        cs3 = jnp.dot(tri.astype(BF16), jnp.concatenate([p1, p2, p3], axis=1), preferred_element_type=F32)
        cs = cs3[:, 0:HALF] + (cs3[:, HALF:2 * HALF] + cs3[:, 2 * HALF:3 * HALF])
        ctot = cs[0:1, :] if rev else cs[C - 1:C, :]
        e_neg = jnp.exp(-cs)
        e_tail = jnp.exp(ctot - cs)
        q1 = kk * jnp.exp(cs - lw)
        k1 = kt * e_neg
        b1 = b * e_neg
        r1 = r * jnp.exp(cs)
        k2 = kt * e_tail
        b2 = b * e_tail
        e_tot = jnp.exp(ctot)
        strict = jnp.logical_and(same, (pj > pi) if rev else (pj < pi))
        incl = jnp.logical_and(same, (pj >= pi) if rev else (pj <= pi))
        for p in range(HALF // LANES):
            cols = slice(p * LANES, (p + 1) * LANES)
            chains.append(dict(p=p, cols=cols, strict=strict, incl=incl, s_ref=s_ref, y_ref=y_ref,
                               q1=q1[:, cols], k1=k1[:, cols], b1=b1[:, cols], r1=r1[:, cols],
                               k2=k2[:, cols], b2=b2[:, cols], v=v[:, cols], e_tot=e_tot[:, cols]))

    for ch in chains:
        lhs = jnp.concatenate([stack(ch["q1"]), stack(ch["r1"])], axis=0)
        rhs = jnp.concatenate([ch["k1"], ch["k1"], ch["b1"], ch["b1"]], axis=0)
        gm = _mm(lhs, rhs, NT_DIMS, 2)
        ch["mk"] = jnp.where(ch["strict"], gm[0:2 * C, 0:2 * C], 0.0)
        ch["mb"] = jnp.where(ch["strict"], gm[0:2 * C, 2 * C:4 * C], 0.0)
        ch["nk"] = jnp.where(ch["incl"], gm[2 * C:4 * C, 0:2 * C], 0.0)
        ch["nb"] = jnp.where(ch["incl"], gm[2 * C:4 * C, 2 * C:4 * C], 0.0)
        ch["tinv"] = eye - jnp.where((pi >> 1) == (pj >> 1), ch["mb"], 0.0)
    size = 2
    while size < C:
        sh = size.bit_length() - 1
        blk = jnp.logical_and((pi >> (sh + 1)) == (pj >> (sh + 1)), (pi >> sh) != (pj >> sh))
        for ch in chains:
            ch["tn"] = _mm(ch["tinv"], jnp.where(blk, ch["mb"], 0.0), NN, 1)
        for ch in chains:
            ch["tinv"] = ch["tinv"] - _mm(ch["tn"], ch["tinv"], NN, 2)
        size *= 2
    for ch in chains:
        vst = stack(ch["v"])
        ch["mkv"] = fold(_mm(ch["mk"], vst, NN, 2))
        ch["nkv"] = fold(_mm(ch["nk"], vst, NN, 1))
        ch["s"] = ch["s_ref"][ch["p"]]
        ch["qr"] = _mm(jnp.concatenate([ch["q1"], ch["r1"]], axis=0), ch["s"], NT_DIMS, 2)
    for ch in chains:
        ch["u"] = fold(_mm(ch["tinv"], stack(ch["mkv"] + ch["qr"][0:C]), NN, 2))
    for ch in chains:
        ch["y_ref"][:, ch["cols"]] = ch["qr"][C:2 * C] + ch["nkv"] - fold(_mm(ch["nb"], stack(ch["u"]), NN, 1))
        upd = _mm(jnp.concatenate([ch["v"], ch["u"]], axis=0),
                  jnp.concatenate([ch["k2"], -ch["b2"]], axis=0), TN, 2)
        ch["s_ref"][ch["p"]] = ch["s"] * ch["e_tot"] + jnp.where(same, upd, 0.0)


def _rwkv_scan_kernel(bf_ref, bb_ref, first_ref, last_ref, seq_ref,
                      rf_ref, vf_ref, kkf_ref, lwf_ref, ktf_ref, bfw_ref,
                      rb_ref, vb_ref, kkb_ref, lwb_ref, ktb_ref, bbw_ref, s0_ref,
                      yf_ref, yb_ref, sfin_ref, s_ref):
    step = pl.program_id(0)

    @pl.when(first_ref[step] == 1)
    def _():
        s_ref[...] = s0_ref[:, 0]

    C = RW_CHUNK
    for sub in range(RW_STEP // C):
        f = pl.ds(sub * C, C)
        b = pl.ds(RW_STEP - (sub + 1) * C, C)
        _rwkv_chunks([
            (False, rf_ref[f, :], vf_ref[f, :], kkf_ref[f, :], lwf_ref[0, f, :], ktf_ref[0, f, :], bfw_ref[0, f, :],
             s_ref.at[0], yf_ref.at[f]),
            (True, rb_ref[b, :], vb_ref[b, :], kkb_ref[b, :], lwb_ref[0, b, :], ktb_ref[0, b, :], bbw_ref[0, b, :],
             s_ref.at[1], yb_ref.at[b])])

    @pl.when(last_ref[step] == 1)
    def _():
        sfin_ref[:, 0] = s_ref[...]


def _rwkv_scan(r, v, kk, lw, kt, b, s0_bd):
    C = RW_STEP
    tabs = _seq_tables(C)
    fwd = lambda i, bf, bb, fi, la, sq: (bf[i], 0)
    bwd = lambda i, bf, bb, fi, la, sq: (bb[i], 0)
    fwd3 = lambda i, bf, bb, fi, la, sq: (0, bf[i], 0)
    bwd3 = lambda i, bf, bb, fi, la, sq: (1, bb[i], 0)
    st = pl.BlockSpec((2, 1, 4, LANES, LANES), lambda i, bf, bb, fi, la, sq: (0, sq[i], 0, 0, 0))
    one_f, one_b = pl.BlockSpec((C, HALF), fwd), pl.BlockSpec((C, HALF), bwd)
    two_f, two_b = pl.BlockSpec((1, C, HALF), fwd3), pl.BlockSpec((1, C, HALF), bwd3)
    return pl.pallas_call(
        _rwkv_scan_kernel,
        out_shape=[jax.ShapeDtypeStruct((T, HALF), F32), jax.ShapeDtypeStruct((T, HALF), F32),
                   jax.ShapeDtypeStruct((2, N_SEQ, 4, LANES, LANES), F32)],
        grid_spec=pltpu.PrefetchScalarGridSpec(
            num_scalar_prefetch=5, grid=(len(tabs[0]),),
            in_specs=[one_f, one_f, one_f, two_f, two_f, two_f,
                      one_b, one_b, one_b, two_b, two_b, two_b, st],
            out_specs=[one_f, one_b, st],
            scratch_shapes=[pltpu.VMEM((2, 4, LANES, LANES), F32)]),
        compiler_params=_params(("arbitrary",)),
        name="rwkv_scan",
    )(*tabs, r, v, kk, lw, kt, b, r, v, kk, lw, kt, b, s0_bd)


def _rope(x, cos, sin_signed, first16):
    w = x.shape[1]
    partner = jnp.where(first16, pltpu.roll(x, w - 16, 1), pltpu.roll(x, 16, 1))
    return x * cos + partner * sin_signed


def _odd_prep_kernel(zc_ref, zr_ref, cos_ref, sin_ref, qkg_ref, ones_ref, cq_ref, ck_ref, ckraw_ref, rqk_ref):
    ones_bd = ones_ref[...]
    cos = cos_ref[...]
    sin = sin_ref[...]
    lane = lax.broadcasted_iota(I32, (TM, HALF), 1)
    first16 = (lane & 31) < 16
    for idx, (o_ref, raw_ref) in enumerate(((cq_ref, None), (ck_ref, ckraw_ref))):
        x = zc_ref[:, idx * HALF:(idx + 1) * HALF]
        ms = _group_sum(x * x, ones_bd) * (1.0 / HEAD)
        xn = x * lax.rsqrt(ms + 1e-6) * qkg_ref[idx:idx + 1, :]
        if raw_ref is not None:
            raw_ref[...] = xn
        o_ref[...] = _rope(xn, cos, sin, first16)
    rqk = _rope(zr_ref[...], cos, sin, first16)
    rqk_ref[...] = jnp.where(lane < HALF // 2, rqk * (HEAD ** -0.5), rqk)


def _odd_prep(zc, zr, cos_tab, sin_tab, qkg_tiled, ones_bd):
    row = lambda i: (i, 0)
    tab = lambda i: (jnp.where(i < PROMPT_TILES, 0, 1 + (i - PROMPT_TILES) % TILES_PER_SAMPLE), 0)
    one = jax.ShapeDtypeStruct((T, HALF), F32)
    o1 = pl.BlockSpec((TM, HALF), row)
    return pl.pallas_call(
        _odd_prep_kernel,
        out_shape=[one, one, one, one], grid=(NT,),
        in_specs=[pl.BlockSpec((TM, 2 * HALF), row), pl.BlockSpec((TM, HALF), row),
                  pl.BlockSpec((TM, HALF), tab), pl.BlockSpec((TM, HALF), tab),
                  _full((2, HALF)), _full((HALF, HALF))],
        out_specs=[o1, o1, o1, o1],
        compiler_params=_params(("arbitrary",)),
        name="odd_prep",
    )(zc, zr, cos_tab, sin_tab, qkg_tiled, ones_bd)


def _attn_kernel(*refs, has_ctx, one_minus_li):
    if has_ctx:
        q_ref, k_ref, v_ref, kc_ref, vc_ref, lam_ref, sg_ref, _, o_ref = refs
    else:
        q_ref, k_ref, v_ref, lam_ref, sg_ref, _, o_ref = refs
    lam = lam_ref[...]
    lane = lax.broadcasted_iota(I32, (LANES, LANES), 1)
    m0 = lane < HEAD
    scale = HEAD ** -0.5
    for h in range(4):
        cols = slice(h * LANES, (h + 1) * LANES)
        qp = q_ref[:, cols]
        segs = [(k_ref[:, cols], v_ref[:, cols])]
        if has_ctx:
            segs.append((kc_ref[0, :, cols], vc_ref[0, :, cols]))
        probs, dens = [], []
        for qm in (jnp.where(m0, qp, 0.0), jnp.where(m0, 0.0, qp)):
            qm16 = qm.astype(BF16)
            ss = [lax.dot_general(qm16, ks.astype(BF16), NT_DIMS, preferred_element_type=F32) * scale
                  for ks, _ in segs]
            mx = ss[0].max(axis=-1, keepdims=True)
            for s_ in ss[1:]:
                mx = jnp.maximum(mx, s_.max(axis=-1, keepdims=True))
            ps = [jnp.exp(s_ - mx) for s_ in ss]
            den = ps[0].sum(axis=-1, keepdims=True)
            for p_ in ps[1:]:
                den = den + p_.sum(axis=-1, keepdims=True)
            probs.append(ps)
            dens.append(den)
        if has_ctx:
            acc = None
            for si, (_, vs) in enumerate(segs):
                amap = probs[0][si] / dens[0] - lam * (probs[1][si] / dens[1])
                t = jnp.dot(amap.astype(BF16), vs.astype(BF16), preferred_element_type=F32)
                acc = t if acc is None else acc + t
        else:
            v16 = segs[0][1].astype(BF16)
            pv = [jnp.dot(ps[0].astype(BF16), v16, preferred_element_type=F32) for ps in probs]
            acc = pv[0] / dens[0] - lam * (pv[1] / dens[1])
        nrm = acc * lax.rsqrt(jnp.mean(acc * acc, axis=-1, keepdims=True) + 1e-6) * sg_ref[...]
        o_ref[:, cols] = nrm * one_minus_li


def _attn(cq, ck, zc, row0, n_seq, seq_len, lam, subln_g, one_minus_li, prev, ctx_k=None, ctx_v=None):
    nq = seq_len // LANES
    qb0 = row0 // LANES
    sb0 = row0 // seq_len
    in_specs = [pl.BlockSpec((LANES, HALF), lambda s, q: (qb0 + s * nq + q, 0)),
                pl.BlockSpec((seq_len, HALF), lambda s, q: (sb0 + s, 0)),
                pl.BlockSpec((seq_len, HALF), lambda s, q: (sb0 + s, 2))]
    args = [cq, ck, zc]
    if ctx_k is not None:
        in_specs += [pl.BlockSpec((1, PAST, HALF), lambda s, q: (s, 0, 0))] * 2
        args += [ctx_k, ctx_v]
    in_specs += [_full((1, 1)), _full((1, LANES))]
    args += [lam.reshape(1, 1), subln_g.reshape(1, LANES)]
    in_specs.append(pl.BlockSpec(memory_space=pl.ANY))
    args.append(prev)
    aliases = {len(args) - 1: 0}
    return pl.pallas_call(
        functools.partial(_attn_kernel, has_ctx=ctx_k is not None, one_minus_li=one_minus_li),
        out_shape=jax.ShapeDtypeStruct((T, HALF), F32),
        grid=(n_seq, nq), in_specs=in_specs,
        out_specs=pl.BlockSpec((LANES, HALF), lambda s, q: (qb0 + s * nq + q, 0)),
        input_output_aliases=aliases,
        compiler_params=_params(("arbitrary", "arbitrary"), 48),
        name="diff_attn",
    )(*args)


_LOG_GAMMA = tuple(tuple(float(np.log1p(-np.exp2(-np.float32(e)), dtype=np.float32)) for e in es)
                   for es in RET_EXP)


def _ret_chunks(dirs):
    C = RET_CHUNK
    ii = lax.broadcasted_iota(I32, (C, C), 0)
    jj = lax.broadcasted_iota(I32, (C, C), 1)
    ri = lax.broadcasted_iota(I32, (C, 1), 0)
    lane = lax.broadcasted_iota(I32, (C, LANES), 1)
    chains = []
    for rev, qk_ref, v_ref, s_ref, o_ref in dirs:
        mask = (jj > ii) if rev else (jj <= ii)
        dist = jnp.where(mask, (jj - ii) if rev else (ii - jj), 0).astype(F32)
        kpow = (ri if rev else (C - 1 - ri)).astype(F32)
        qpow = ((C - ri) if rev else (ri + 1)).astype(F32)
        for h in range(4):
            lg = _LOG_GAMMA[1 if rev else 0][h]
            p = h // 2
            hm = (lane < HEAD) if h % 2 == 0 else (lane >= HEAD)
            qp = jnp.where(hm, qk_ref[:, p * LANES:(p + 1) * LANES], 0.0)
            kp = jnp.where(hm, qk_ref[:, HALF // 2 + p * LANES:HALF // 2 + (p + 1) * LANES], 0.0)
            chains.append(dict(
                h=h, lg=lg, s_ref=s_ref, o_ref=o_ref, q16=qp.astype(BF16), k16=kp.astype(BF16),
                qw16=(qp * jnp.exp(lg * qpow)).astype(BF16), kw16=(kp * jnp.exp(lg * kpow)).astype(BF16),
                v16=v_ref[:, h * LANES:(h + 1) * LANES].astype(BF16),
                decay=jnp.where(mask, jnp.exp(lg * dist), 0.0)))
    for ch in chains:
        ch["sc"] = (lax.dot_general(ch["q16"], ch["k16"], NT_DIMS, preferred_element_type=F32)
                    * ch["decay"]).astype(BF16)
        ch["s"] = ch["s_ref"][ch["h"]]
    for ch in chains:
        ch["o"] = (jnp.dot(ch["sc"], ch["v16"], preferred_element_type=F32)
                   + jnp.dot(ch["qw16"], ch["s"].astype(BF16), preferred_element_type=F32))
        ch["kv"] = lax.dot_general(ch["kw16"], ch["v16"], TN, preferred_element_type=F32)
    for ch in chains:
        h = ch["h"]
        ch["o_ref"][:, h * LANES:(h + 1) * LANES] = ch["o"]
        ch["s_ref"][h] = math.exp(ch["lg"] * C) * ch["s"] + ch["kv"]


def _ret_kernel(bf_ref, bb_ref, first_ref, last_ref, seq_ref,
                qkf_ref, vf_ref, qkb_ref, vb_ref, s0_ref, of_ref, ob_ref, sfin_ref, s_ref):
    step = pl.program_id(0)

    @pl.when(first_ref[step] == 1)
    def _():
        s_ref[...] = s0_ref[:, 0]

    C = RET_CHUNK
    for sub in range(RET_STEP // C):
        f = pl.ds(sub * C, C)
        b = pl.ds(RET_STEP - (sub + 1) * C, C)
        _ret_chunks([(False, qkf_ref.at[f], vf_ref.at[f], s_ref.at[0], of_ref.at[f]),
                     (True, qkb_ref.at[b], vb_ref.at[b], s_ref.at[1], ob_ref.at[b])])

    @pl.when(last_ref[step] == 1)
    def _():
        sfin_ref[:, 0] = s_ref[...]


def _retention(rqk, zr, s0):
    C = RET_STEP
    tabs = _seq_tables(C)
    st = pl.BlockSpec((2, 1, 4, LANES, LANES), lambda i, bf, bb, fi, la, sq: (0, sq[i], 0, 0, 0))
    spec = lambda use_b, col: pl.BlockSpec(
        (C, HALF), lambda i, bf, bb, fi, la, sq: ((bb if use_b else bf)[i], col))
    return pl.pallas_call(
        _ret_kernel,
        out_shape=[jax.ShapeDtypeStruct((T, HALF), F32), jax.ShapeDtypeStruct((T, HALF), F32),
                   jax.ShapeDtypeStruct((2, N_SEQ, 4, LANES, LANES), F32)],
        grid_spec=pltpu.PrefetchScalarGridSpec(
            num_scalar_prefetch=5, grid=(len(tabs[0]),),
            in_specs=[spec(False, 0), spec(False, 1), spec(True, 0), spec(True, 1), st],
            out_specs=[spec(False, 0), spec(True, 0), st],
            scratch_shapes=[pltpu.VMEM((2, 4, LANES, LANES), F32)]),
        compiler_params=_params(("arbitrary",)),
        name="retention",
    )(*tabs, rqk, zr, rqk, zr, s0)


def _out_kernel(*refs, even, n_x):
    x_refs, refs = refs[:n_x], refs[n_x:]
    if even:
        (a_ref, yf_ref, yb_ref, bonus_ref, g_ref, gng_ref, gnb_ref, ones_ref,
         mod_ref, ng_ref, wo_ref, rw_ref, rb_ref,
         y_ref, xp_ref, ti_ref, tg_ref, rk_ref, cnt_ref, run_ref) = refs
        ones_bd = ones_ref[...]
        ys = yf_ref[...] + yb_ref[...]
        mu = _group_sum(ys, ones_bd) * (1.0 / HEAD)
        dv = ys - mu
        var = _group_sum(dv * dv, ones_bd) * (1.0 / HEAD)
        yn = dv * lax.rsqrt(var + RWKV_GN_EPS) * gng_ref[...] + gnb_ref[...]
        left = a_ref[...]
        right = (yn + bonus_ref[...]) * g_ref[...]
    else:
        (c_ref, of_ref, ob_ref, rg_ref, gng_ref,
         mod_ref, ng_ref, wo_ref, rw_ref, rb_ref,
         y_ref, xp_ref, ti_ref, tg_ref, rk_ref, cnt_ref, run_ref) = refs
        left = c_ref[...]
        rg = rg_ref[...]
        gate = rg * jax.nn.sigmoid(rg)
        os_ = of_ref[...] + ob_ref[...]
        parts = []
        for h in range(4):
            oh = os_[:, h * LANES:(h + 1) * LANES]
            mu = jnp.mean(oh, axis=-1, keepdims=True)
            dv = oh - mu
            var = jnp.mean(dv * dv, axis=-1, keepdims=True)
            parts.append(dv * lax.rsqrt(var + 1e-5))
        right = gate * (jnp.concatenate(parts, axis=1) * gng_ref[...])
    mod = mod_ref[0]
    o = (jnp.dot(left.astype(BF16), wo_ref[0:HALF, :], preferred_element_type=F32)
         + jnp.dot(right.astype(BF16), wo_ref[HALF:2 * HALF, :], preferred_element_type=F32))
    y = _token_rows(x_refs) + mod[:, 2 * D:3 * D] * o
    y_ref[...] = y
    yn2 = y * lax.rsqrt(jnp.mean(y * y, axis=-1, keepdims=True) + 1e-6) * ng_ref[...]
    t = yn2 * (1.0 + mod[:, 4 * D:5 * D]) + mod[:, 3 * D:4 * D]
    xp_ref[...] = t
    logits = _mm(t, rw_ref[...], NN, 3) + rb_ref[...]
    lane = lax.broadcasted_iota(I32, (TM, LANES), 1)
    neg = jnp.float32(-jnp.inf)
    lg = jnp.where(lane < N_EXPERTS, logits, neg)
    vals, hits = [], []
    for _ in range(TOP_K):
        m = jnp.max(lg, axis=-1, keepdims=True)
        ix = jnp.min(jnp.where(lg == m, lane, LANES), axis=-1, keepdims=True)
        hit = lane == ix
        vals.append(m)
        hits.append((ix, hit))
        lg = jnp.where(hit, neg, lg)
    es = [jnp.exp(vv - vals[0]) for vv in vals]
    den = es[0] + es[1] + es[2] + es[3]

    @pl.when(pl.program_id(0) == 0)
    def _():
        run_ref[...] = jnp.zeros_like(run_ref)

    member = jnp.zeros((TM, LANES), F32)
    for _, hit in hits:
        member = member + jnp.where(hit, 1.0, 0.0)
    ri = lax.broadcasted_iota(I32, (TM, TM), 0)
    ci = lax.broadcasted_iota(I32, (TM, TM), 1)
    before = jnp.where(ci < ri, 1.0, 0.0).astype(BF16)
    seen = run_ref[...] + jnp.dot(before, member.astype(BF16), preferred_element_type=F32)
    ti = jnp.zeros((TM, LANES), I32)
    tg = jnp.zeros((TM, LANES), F32)
    rk = jnp.zeros((TM, LANES), F32)
    for kk, (ix, hit) in enumerate(hits):
        ti = jnp.where(lane == kk, ix, ti)
        tg = jnp.where(lane == kk, es[kk] / den, tg)
        rk = jnp.where(lane == kk, jnp.sum(jnp.where(hit, seen, 0.0), axis=-1, keepdims=True), rk)
    ti_ref[...] = ti
    tg_ref[...] = tg
    rk_ref[...] = rk.astype(I32)
    run_ref[...] = run_ref[...] + jnp.sum(member, axis=0, keepdims=True)
    cnt_ref[...] = run_ref[...]


def _out_proj(even, mix_args, mix_specs, x, mod, norm_g, w_out_bf16, rw_pad, rb_pad):
    row = lambda i: (i, 0)
    modspec = pl.BlockSpec((1, 1, N_MOD * D), lambda i: (_group(i), 0, 0))
    x_specs, x_args = _token_specs(x)
    in_specs = x_specs + list(mix_specs) + [modspec, _full((1, D)), _full((D, D)), _full((D, LANES)),
                                            _full((1, LANES))]
    args = x_args + list(mix_args) + [mod, norm_g.reshape(1, D), w_out_bf16, rw_pad, rb_pad]
    lane_i = jax.ShapeDtypeStruct((T, LANES), I32)
    lane_spec = pl.BlockSpec((TM, LANES), row)
    return pl.pallas_call(
        functools.partial(_out_kernel, even=even, n_x=len(x_args)),
        out_shape=[jax.ShapeDtypeStruct((T, D), F32), jax.ShapeDtypeStruct((T, D), F32),
                   lane_i, jax.ShapeDtypeStruct((T, LANES), F32), lane_i,
                   jax.ShapeDtypeStruct((1, LANES), F32)],
        grid=(NT,), in_specs=in_specs,
        out_specs=[pl.BlockSpec((TM, D), row), pl.BlockSpec((TM, D), row),
                   lane_spec, lane_spec, lane_spec, _full((1, LANES))],
        scratch_shapes=[pltpu.VMEM((1, LANES), F32)],
        compiler_params=_params(("arbitrary",), 48),
        name="out_proj",
    )(*args)


def _route_kernel(cnt_ref, ti_ref, rk_ref, dest_ref, te_ref, nt_ref):
    cnt = cnt_ref[...].astype(I32)
    ntile = lax.shift_right_logical(cnt + (TMX - 1), TMX.bit_length() - 1)
    ei = lax.broadcasted_iota(I32, (LANES, LANES), 0)
    ej = lax.broadcasted_iota(I32, (LANES, LANES), 1)
    upto = jnp.where(ei <= ej, 1.0, 0.0).astype(BF16)
    ntile_f = jnp.broadcast_to(ntile.astype(F32), (8, LANES))
    tile_end = jnp.dot(ntile_f.astype(BF16), upto, preferred_element_type=F32)[0:1, :]
    row_start = (tile_end - ntile.astype(F32)) * float(TMX)
    lane = lax.broadcasted_iota(I32, (TM, LANES), 1)
    ti = ti_ref[...]
    rk = rk_ref[...]
    spread = jnp.zeros((TM, LANES), F32)
    for k in range(TOP_K):
        hit = lane == ti[:, k:k + 1]
        start = jnp.sum(jnp.where(hit, row_start, 0.0), axis=-1, keepdims=True)
        spread = jnp.where((lane & (TOP_K - 1)) == k, start + rk[:, k:k + 1].astype(F32), spread)
    tok = lax.broadcasted_iota(I32, (TM, LANES), 0)
    keep = (tok & (LANES // TOP_K - 1)) == lax.shift_right_logical(lane, 2)
    flat = jnp.where(keep, spread, 0.0).reshape(TM * TOP_K // LANES, LANES // TOP_K, LANES).sum(axis=1)
    dest_ref[...] = flat.astype(I32)

    @pl.when(pl.program_id(0) == 0)
    def _():
        lane1 = lax.broadcasted_iota(I32, (1, LANES), 1)
        n_tiles = jnp.max(tile_end, axis=-1, keepdims=True)
        last_e = jnp.max(jnp.where(cnt > 0, lane1, 0), axis=-1, keepdims=True)
        tile = lax.broadcasted_iota(I32, (TM, 1), 0).astype(F32)
        te = jnp.sum(jnp.where(tile_end <= tile, 1, 0), axis=-1, keepdims=True)
        te = jnp.where(tile < n_tiles, te, last_e)
        te_ref[...] = jnp.broadcast_to(te, (TM, LANES)).astype(I32)
        first_row = lax.broadcasted_iota(I32, (8, LANES), 0) == 0
        nt_ref[...] = jnp.where(first_row, n_tiles, tile_end).astype(I32)


def _route(cnt, ti, rk):
    row = lambda i: (i, 0)
    return pl.pallas_call(
        _route_kernel,
        out_shape=[jax.ShapeDtypeStruct((N_ASSIGN // LANES, LANES), I32), jax.ShapeDtypeStruct((TM, LANES), I32),
                   jax.ShapeDtypeStruct((8, LANES), I32)],
        grid=(NT,),
        in_specs=[_full((1, LANES)), pl.BlockSpec((TM, LANES), row), pl.BlockSpec((TM, LANES), row)],
        out_specs=[pl.BlockSpec((TM * TOP_K // LANES, LANES), row), _full((TM, LANES)), _full((8, LANES))],
        compiler_params=_params(("arbitrary",)),
        name="moe_route",
    )(cnt, ti, rk)


def _start_row_copy(src_ref, src_row, dst_ref, dst_row, sem, queue):
    pltpu.async_copy(src_ref.at[pl.ds(src_row, 1)], dst_ref.at[pl.ds(dst_row, 1)], sem, priority=queue)


def _wait_tiles(n, src_ref, dst_ref, sem):
    for _ in range(n):
        pltpu.make_async_copy(src_ref, dst_ref, sem).wait()


def _dispatch_kernel(dest_ref, tend_ref, x_ref, xs_ref, zero_ref, sem):
    i = pl.program_id(0)

    @pl.when(i == 0)
    def _():
        zero_ref[...] = jnp.zeros_like(zero_ref)

        def last_tile(e, fn):
            end = tend_ref[e]
            begin = tend_ref[e - 1] if e > 0 else 0

            @pl.when(end > begin)
            def _():
                fn(pltpu.make_async_copy(zero_ref, xs_ref.at[pl.ds((end - 1) * TMX, TMX)], sem))

        def unused_tile(j):
            return pltpu.make_async_copy(zero_ref, xs_ref.at[pl.ds(j * TMX, TMX)], sem)

        def start_unused(j, carry):
            unused_tile(j).start()
            return carry

        def wait_unused(j, carry):
            unused_tile(j).wait()
            return carry

        n_used = tend_ref[N_EXPERTS - 1]
        for e in range(N_EXPERTS):
            last_tile(e, lambda c: c.start())
        lax.fori_loop(n_used, MOE_TILES, start_unused, 0)
        for e in range(N_EXPERTS):
            last_tile(e, lambda c: c.wait())
        lax.fori_loop(n_used, MOE_TILES, wait_unused, 0)

    base = i * (TM * TOP_K)

    def start(r, carry):
        for k in range(TOP_K):
            _start_row_copy(x_ref, r, xs_ref, dest_ref[base + r * TOP_K + k], sem, k % 2)
        return carry

    lax.fori_loop(0, TM, start, 0, unroll=8)
    _wait_tiles(TOP_K, x_ref, xs_ref.at[pl.ds(0, TM)], sem)


def _dispatch(dest_flat, tile_end, xt):
    return pl.pallas_call(
        _dispatch_kernel,
        out_shape=jax.ShapeDtypeStruct((R_PAD, D), F32),
        grid_spec=pltpu.PrefetchScalarGridSpec(
            num_scalar_prefetch=2, grid=(NT,),
            in_specs=[pl.BlockSpec((TM, D), lambda i, d, te: (i, 0))],
            out_specs=pl.BlockSpec(memory_space=pl.ANY),
            scratch_shapes=[pltpu.VMEM((TMX, D), F32), pltpu.SemaphoreType.DMA(())]),
        compiler_params=_params(("arbitrary",)),
        name="moe_dispatch",
    )(dest_flat, tile_end, xt)


W_PARTS = 8


def _expert_weights(i, nt, te_ref, w_ref, wbuf_ref, wsem, w16_ref, group_ref):
    rows = w_ref.shape[1] // W_PARTS

    def fetch(e, buf):
        return [pltpu.make_async_copy(w_ref.at[e, pl.ds(p * rows, rows)], wbuf_ref.at[buf, pl.ds(p * rows, rows)],
                                      wsem.at[buf]) for p in range(W_PARTS)]

    @pl.when(i == 0)
    def _():
        group_ref[0] = 0
        for c in fetch(te_ref[0], 0):
            c.start()

    first = jnp.logical_or(i == 0, te_ref[i] != te_ref[jnp.maximum(i - 1, 0)])

    @pl.when(jnp.logical_and(first, i < nt))
    def _():
        cur = group_ref[0] % 2
        nxt = lax.while_loop(
            lambda j: jnp.logical_and(j < nt, te_ref[jnp.minimum(j, MOE_TILES - 1)] == te_ref[i]),
            lambda j: j + 1, i + 1)

        @pl.when(nxt < nt)
        def _():
            for c in fetch(te_ref[jnp.minimum(nxt, MOE_TILES - 1)], 1 - cur):
                c.start()

        for c in fetch(0, cur):
            c.wait()
        w16_ref[...] = wbuf_ref[cur].astype(BF16)
        group_ref[0] = group_ref[0] + 1


def _experts_kernel(te_ref, nt_ref, xs_ref, wgu_ref, bgu_ref, wdn_ref, bdn_ref, y_ref,
                    gu_buf, gu_sem, gu16_ref, gu_group, dn_buf, dn_sem, dn16_ref, dn_group):
    i = pl.program_id(0)
    nt = nt_ref[0]
    _expert_weights(i, nt, te_ref, wgu_ref, gu_buf, gu_sem, gu16_ref, gu_group)
    _expert_weights(i, nt, te_ref, wdn_ref, dn_buf, dn_sem, dn16_ref, dn_group)

    @pl.when(i < nt)
    def _():
        x16 = xs_ref[...].astype(BF16)
        y = bdn_ref[0]
        for h in range(2):
            gc = slice(h * HALF, (h + 1) * HALF)
            uc = slice(D + h * HALF, D + (h + 1) * HALF)
            g = jnp.dot(x16, gu16_ref[:, gc], preferred_element_type=F32) + bgu_ref[0, :, gc]
            u = jnp.dot(x16, gu16_ref[:, uc], preferred_element_type=F32) + bgu_ref[0, :, uc]
            gt = jnp.minimum(g, SWIGLU_LIMIT)
            up = jnp.clip(u, -SWIGLU_LIMIT, SWIGLU_LIMIT)
            act = ((up + 1.0) * gt * jax.nn.sigmoid(SWIGLU_ALPHA * gt)).astype(BF16)
            y = y + jnp.dot(act, dn16_ref[gc, :], preferred_element_type=F32)
        y_ref[...] = y

    @pl.when(i >= nt)
    def _():
        y_ref[...] = jnp.zeros_like(y_ref)


def _tile_clamped(i, te, nt):
    return (jnp.minimum(i, jnp.maximum(nt[0] - 1, 0)), 0)


def _weight_scratch(n_out):
    return [pltpu.VMEM((2, D, n_out), F32), pltpu.SemaphoreType.DMA((2,)), pltpu.VMEM((D, n_out), BF16),
            pltpu.SMEM((1,), I32)]


def _experts(te, n_tiles, xs, w_gu, b_gu, w_dn, b_dn):
    return pl.pallas_call(
        _experts_kernel,
        out_shape=jax.ShapeDtypeStruct((R_PAD, D), F32),
        grid_spec=pltpu.PrefetchScalarGridSpec(
            num_scalar_prefetch=2, grid=(MOE_TILES,),
            in_specs=[pl.BlockSpec((TMX, D), _tile_clamped), pl.BlockSpec(memory_space=pl.ANY),
                      pl.BlockSpec((1, 1, 2 * D), lambda i, te, nt: (te[i], 0, 0)),
                      pl.BlockSpec(memory_space=pl.ANY),
                      pl.BlockSpec((1, 1, D), lambda i, te, nt: (te[i], 0, 0))],
            out_specs=pl.BlockSpec((TMX, D), lambda i, te, nt: (i, 0)),
            scratch_shapes=_weight_scratch(2 * D) + _weight_scratch(D)),
        compiler_params=_params(("arbitrary",), 58),
        name="moe_experts",
    )(te, n_tiles, xs, w_gu, b_gu, w_dn, b_dn)


def _combine_kernel(dest_ref, x_ref, tg_ref, mod_ref, ys_ref, *rest):
    *o_refs, buf_ref, sem = rest
    i = pl.program_id(0)
    slot = i % 2

    def gather(tile, b):
        base = tile * (TM * TOP_K)

        def body(r, carry):
            for k in range(TOP_K):
                _start_row_copy(ys_ref, dest_ref[base + r * TOP_K + k], buf_ref.at[b, k], r, sem.at[b], k % 2)
            return carry

        lax.fori_loop(0, TM, body, 0, unroll=8)

    @pl.when(i == 0)
    def _():
        gather(0, 0)

    @pl.when(i + 1 < NT)
    def _():
        gather(i + 1, 1 - slot)

    _wait_tiles(TOP_K, ys_ref.at[pl.ds(0, TM)], buf_ref.at[slot, 0], sem.at[slot])
    tg = tg_ref[...]
    f = tg[:, 0:1] * buf_ref[slot, 0]
    for k in range(1, TOP_K):
        f = f + tg[:, k:k + 1] * buf_ref[slot, k]
    res = x_ref[...] + mod_ref[0][:, 5 * D:6 * D] * f
    if len(o_refs) == 1:
        o_refs[0][...] = res
    else:
        @pl.when(i < PROMPT_TILES)
        def _():
            o_refs[0][...] = res

        @pl.when(i >= PROMPT_TILES)
        def _():
            o_refs[1][...] = res


def _combine(dest_flat, x, tg, mod, ys, split=False):
    row = lambda i, d: (i, 0)
    if split:
        out_shape = [jax.ShapeDtypeStruct((T_PROMPT, D), F32), jax.ShapeDtypeStruct((T_SAMPLE, D), F32)]
        out_specs = [pl.BlockSpec((TM, D), lambda i, d: (jnp.minimum(i, PROMPT_TILES - 1), 0)),
                     pl.BlockSpec((TM, D), lambda i, d: (jnp.maximum(i - PROMPT_TILES, 0), 0))]
    else:
        out_shape = jax.ShapeDtypeStruct((T, D), F32)
        out_specs = pl.BlockSpec((TM, D), row)
    return pl.pallas_call(
        _combine_kernel,
        out_shape=out_shape,
        grid_spec=pltpu.PrefetchScalarGridSpec(
            num_scalar_prefetch=1, grid=(NT,),
            in_specs=[pl.BlockSpec((TM, D), row), pl.BlockSpec((TM, LANES), row),
                      pl.BlockSpec((1, 1, N_MOD * D), lambda i, d: (_group(i), 0, 0)),
                      pl.BlockSpec(memory_space=pl.ANY)],
            out_specs=out_specs,
            scratch_shapes=[pltpu.VMEM((2, TOP_K, TM, D), F32), pltpu.SemaphoreType.DMA((2,))]),
        compiler_params=_params(("arbitrary",), 40),
        name="moe_combine",
    )(dest_flat, x, tg, mod, ys)


def _moe(layer, y, xt, ti, tg, rk, cnt, mod, w_gu, b_gu, w_dn, b_dn):
    dest, te, nt = _route(cnt, ti, rk)
    dest_flat = dest.reshape(-1)
    te = te[:MOE_TILES, 0] + layer * N_EXPERTS
    n_tiles = nt[0, :1]
    xs = _dispatch(dest_flat, nt[1, :N_EXPERTS], xt)
    n_all = w_gu.shape[0] * N_EXPERTS
    ys = _experts(te, n_tiles, xs, w_gu.reshape(n_all, D, 2 * D), b_gu.reshape(n_all, 1, 2 * D),
                  w_dn.reshape(n_all, D, D), b_dn.reshape(n_all, 1, D))
    return _combine(dest_flat, y, tg, mod, ys, split=layer == 1)


def _ones_blockdiag():
    idx = np.arange(HALF) // HEAD
    return jnp.asarray((idx[:, None] == idx[None, :]).astype(np.float32), dtype=BF16)


def _rope_tables():
    pos = jnp.arange(L_SAMPLE)
    rowp = (pos // 64).astype(F32)
    colp = (pos % 64).astype(F32)
    nf = HEAD // 4
    inv = jnp.power(10000.0, -jnp.arange(nf, dtype=F32) / nf)
    ar = rowp[:, None] * inv[None, :]
    ac = colp[:, None] * inv[None, :]
    cos64 = jnp.concatenate([jnp.cos(ar), jnp.cos(ar), jnp.cos(ac), jnp.cos(ac)], axis=1)
    sin64 = jnp.concatenate([-jnp.sin(ar), jnp.sin(ar), -jnp.sin(ac), jnp.sin(ac)], axis=1)
    cos = jnp.tile(cos64, (1, HALF // HEAD))
    sin = jnp.tile(sin64, (1, HALF // HEAD))
    ident = jnp.ones((TM, HALF), F32)
    return (jnp.concatenate([ident, cos], axis=0), jnp.concatenate([jnp.zeros((TM, HALF), F32), sin], axis=0))


def _bd_pairs(s):
    lead = s.shape[:-3]
    s = s.reshape(lead + (4, 2, HEAD, HEAD))
    z = jnp.zeros_like(s[..., 0, :, :])
    top = jnp.concatenate([s[..., 0, :, :], z], axis=-1)
    bot = jnp.concatenate([z, s[..., 1, :, :]], axis=-1)
    return jnp.concatenate([top, bot], axis=-2)


def _bd_unpairs(s):
    a = s[..., 0:HEAD, 0:HEAD]
    b = s[..., HEAD:, HEAD:]
    out = jnp.stack([a, b], axis=-3)
    return out.reshape(s.shape[:-3] + (8, HEAD, HEAD))


def kernel(x_prompt, x_sample, state_rwkv, cache_k_diff, cache_v_diff, state_retention, c, c_ctx, norm_g, ada_w, ada_b, e_w_in, e_w_out, sgu_ln_g, sgu_w, sgu_b, rw_mu, rw_w0, rw_w_up, rw_a0, rw_a_up, rw_g_up, rw_k_k, rw_k_a, rw_r_k, rw_gn_g, rw_gn_b, o_w_in, o_w_out, da_qk_g, da_lam, da_subln_g, ret_gn_g, router_w, router_b, ex_w_gu, ex_b_gu, ex_w_dn, ex_b_dn):
    x = (x_prompt.reshape(T_PROMPT, D), x_sample.reshape(T_SAMPLE, D))
    cvec8 = jnp.concatenate([c_ctx[None, :], c, jnp.zeros((3, D), F32)], axis=0)
    mods = _adaln(cvec8, ada_w, ada_b)
    mod0 = mods[0].reshape(8, 1, N_MOD * D)
    mod1 = mods[1].reshape(8, 1, N_MOD * D)
    ones_bd = _ones_blockdiag()
    rw_pad = jnp.pad(router_w, ((0, 0), (0, 0), (0, LANES - N_EXPERTS)))
    rb_pad = jnp.pad(router_b, ((0, 0), (0, LANES - N_EXPERTS))).reshape(2, 1, LANES)
    row = lambda i: (i, 0)
    half = pl.BlockSpec((TM, HALF), row)

    bs_full = jnp.repeat(sgu_b[0].T, HEAD, axis=1)
    a_out, zb = _in_proj(x, norm_g[0, 0], mod0, e_w_in[0].astype(BF16), (2 * HALF, B_COLS),
                         sgu=(sgu_ln_g[0], sgu_w[0].astype(BF16), bs_full))
    zpad = jnp.zeros((2, HEAD, HALF), F32)
    wup_pad = jnp.concatenate([rw_w_up[0], zpad], axis=1).astype(BF16)
    aup_pad = jnp.concatenate([zpad, rw_a_up[0]], axis=1).astype(BF16)
    r, v, kkn, bonus, g, lw, kt, b = _rwkv_prep(zb, rw_mu[0], rw_k_k[0], rw_k_a[0], rw_r_k[0], rw_w0[0], rw_a0[0],
                                                wup_pad, aup_pad, rw_g_up[0].astype(BF16), ones_bd)
    s0_sample = _bd_pairs(jnp.moveaxis(state_rwkv[:, 0], 1, 0))
    s0_rw = jnp.concatenate([jnp.zeros((2, N_PROMPT, 4, LANES, LANES), F32), s0_sample], axis=1)
    yf_rw, yb_rw, sfin_rw = _rwkv_scan(r, v, kkn, lw, kt, b, s0_rw)
    new_rwkv = jnp.moveaxis(_bd_unpairs(sfin_rw[:, :N_PROMPT]), 0, 1)[:, None]
    y0, xp0, ti0, tg0, rk0, cnt0 = _out_proj(
        True,
        [a_out, yf_rw, yb_rw, bonus, g, rw_gn_g[0].reshape(1, HALF), rw_gn_b[0].reshape(1, HALF), ones_bd],
        [half, half, half, half, half, _full((1, HALF)), _full((1, HALF)), _full((HALF, HALF))],
        x, mod0, norm_g[0, 1], e_w_out[0].astype(BF16), rw_pad[0], rb_pad[0])
    x1 = _moe(0, y0, xp0, ti0, tg0, rk0, cnt0, mod0, ex_w_gu, ex_b_gu, ex_w_dn, ex_b_dn)

    zc, zr = _in_proj(x1, norm_g[1, 0], mod1, o_w_in[0].astype(BF16), (3 * HALF, 3 * HALF))
    cos_tab, sin_tab = _rope_tables()
    qkg = jnp.tile(da_qk_g[0], (1, HALF // HEAD))
    cq, ck, ck_raw, rqk = _odd_prep(zc, zr, cos_tab, sin_tab, qkg, ones_bd)
    lambda_init = 0.8 - 0.6 * math.exp(-0.3 * 1)
    lv = da_lam[0]
    lam = jnp.exp(jnp.sum(lv[0] * lv[1])) - jnp.exp(jnp.sum(lv[2] * lv[3])) + lambda_init
    c_out = _attn(cq, ck, zc, 0, N_PROMPT, L_PROMPT, lam, da_subln_g[0], 1.0 - lambda_init,
                  jnp.zeros((T, HALF), F32))
    ctx_k = cache_k_diff[:, 0].reshape(N_SAMPLE, PAST, HALF)
    ctx_v = cache_v_diff[:, 0].reshape(N_SAMPLE, PAST, HALF)
    c_out = _attn(cq, ck, zc, T_PROMPT, N_SAMPLE, L_SAMPLE, lam, da_subln_g[0], 1.0 - lambda_init,
                  c_out, ctx_k, ctx_v)
    sr = jnp.moveaxis(state_retention[:, 0], 1, 0)
    zr0 = jnp.zeros_like(sr)
    s0_sample = jnp.stack([jnp.concatenate([sr[:, :, 0], zr0[:, :, 0]], axis=-2),
                           jnp.concatenate([zr0[:, :, 1], sr[:, :, 1]], axis=-2),
                           jnp.concatenate([sr[:, :, 2], zr0[:, :, 2]], axis=-2),
                           jnp.concatenate([zr0[:, :, 3], sr[:, :, 3]], axis=-2)], axis=2)
    s0_ret = jnp.concatenate([jnp.zeros((2, N_PROMPT, 4, LANES, LANES), F32), s0_sample], axis=1)
    of_ret, ob_ret, rfin = _retention(rqk, zr, s0_ret)
    rfin_p = rfin[:, :N_PROMPT]
    new_ret = jnp.stack([rfin_p[:, :, 0, 0:HEAD], rfin_p[:, :, 1, HEAD:], rfin_p[:, :, 2, 0:HEAD],
                         rfin_p[:, :, 3, HEAD:]], axis=2)
    new_ret = jnp.moveaxis(new_ret, 0, 1)[:, None]
    y1, xp1, ti1, tg1, rk1, cnt1 = _out_proj(
        False,
        [c_out, of_ret, ob_ret, zr, ret_gn_g[0].reshape(1, HALF)],
        [half, half, half, pl.BlockSpec((TM, HALF), lambda i: (i, 2)), _full((1, HALF))],
        x1, mod1, norm_g[1, 1], o_w_out[0].astype(BF16), rw_pad[1], rb_pad[1])
    y_prompt, y_sample = _moe(1, y1, xp1, ti1, tg1, rk1, cnt1, mod1, ex_w_gu, ex_b_gu, ex_w_dn, ex_b_dn)

    new_k = ck_raw[:T_PROMPT].reshape(N_PROMPT, 1, L_PROMPT, 4, LANES)
    new_v = zc[:T_PROMPT, 2 * HALF:3 * HALF].reshape(N_PROMPT, 1, L_PROMPT, 4, LANES)
    return (y_prompt.reshape(N_PROMPT, L_PROMPT, D), y_sample.reshape(N_SAMPLE, L_SAMPLE, D),
            new_rwkv, new_k, new_v, new_ret)
```
